```python
import jax, jax.numpy as jnp
from jax import lax
import numpy as np

D_MODEL = 1024
BATCH = 8
SEQ = 2048
DEPTH = 1
DEC_BATCH = 128
DEC_SEQ = 1
PAST_LEN = 16384
PAGE_SIZE = 128

HGRN_HEADS = 4
HGRN_DK = 128
HGRN_DV = 128
HGRN_KEY_WIDTH = HGRN_HEADS * HGRN_DK
HGRN_VAL_WIDTH = HGRN_HEADS * HGRN_DV
CHUNK = 32
CONV_WIDTH = 512
CONV_K = 3
IN_WIDTH = 2 * HGRN_KEY_WIDTH + 2 * HGRN_VAL_WIDTH + 3 * CONV_WIDTH + 2 * D_MODEL
N_EXPERTS = 64
TOP_K = 8
N_GROUPS = 8
TOPK_GROUPS = 4
EXPERT_FF = 256
SHARED_FF = 256
ROUTED_SCALE = 2.5
MOE_BLOCK = 128
NORM_EPS = 1e-6

kernel_name = "hgrn2_shortconv_moe_adaln_decode_step"


def _split_points():
    sizes = (HGRN_KEY_WIDTH, HGRN_KEY_WIDTH, HGRN_VAL_WIDTH, HGRN_VAL_WIDTH,
             CONV_WIDTH, CONV_WIDTH, CONV_WIDTH, D_MODEL, D_MODEL)
    return list(np.cumsum(sizes)[:-1])


def rmsnorm(x, g):
    xf = x.astype(jnp.float32)
    y = xf * lax.rsqrt(jnp.mean(xf * xf, axis=-1, keepdims=True) + NORM_EPS)
    return (y * g.astype(jnp.float32)).astype(x.dtype)


def swiglu(x, w_gate, w_up, w_down):
    return (jax.nn.silu(x @ w_gate) * (x @ w_up)) @ w_down


def hgrn2_chunked(q, k, v, log_f, s0):
    B, T, H, _ = q.shape
    pad = (-T) % CHUNK
    n_chunks = (T + pad) // CHUNK

    def to_chunks(a):
        a = jnp.pad(a, ((0, 0), (0, pad), (0, 0), (0, 0)))
        return a.reshape(B, n_chunks, CHUNK, H, a.shape[-1]).transpose(1, 0, 3, 2, 4)

    q, k, v, log_f = (to_chunks(a) for a in (q, k, v, log_f))
    b = jnp.cumsum(log_f, axis=3)
    b_last = b[:, :, :, -1:, :]
    q_in = q * jnp.exp(b)
    k_in = k * jnp.exp(-b)
    k_end = k * jnp.exp(b_last - b)
    causal = jnp.tril(jnp.ones((CHUNK, CHUNK), dtype=bool))
    att = jnp.where(causal, jnp.einsum("nbhtd,nbhsd->nbhts", q_in, k_in), 0.0)
    o_intra = jnp.einsum("nbhts,nbhsv->nbhtv", att, v)

    def step(S, xs):
        q_c, k_c, v_c, decay = xs
        o_c = jnp.einsum("bhtd,bhdv->bhtv", q_c, S)
        S = decay[..., None] * S + jnp.einsum("bhsd,bhsv->bhdv", k_c, v_c)
        return S, o_c

    s_fin, o_inter = lax.scan(step, s0, (q_in, k_end, v, jnp.exp(b_last[:, :, :, 0, :])))
    o = (o_intra + o_inter).transpose(1, 0, 3, 2, 4).reshape(B, n_chunks * CHUNK, H, -1)[:, :T]
    return o, s_fin


def hgrn2_branch(q_raw, f_raw, i_raw, g_raw, s0, lb, head_g):
    B, T, _ = q_raw.shape
    split_heads = lambda a, d: a.astype(jnp.float32).reshape(B, T, HGRN_HEADS, d)
    lb_h = lb.reshape(HGRN_HEADS, HGRN_DK)
    f = lb_h + (1.0 - lb_h) * jax.nn.sigmoid(split_heads(f_raw, HGRN_DK))
    q = jax.nn.silu(split_heads(q_raw, HGRN_DK))
    o, s_new = hgrn2_chunked(q, 1.0 - f, split_heads(i_raw, HGRN_DV), jnp.log(f), s0.astype(jnp.float32))
    o = rmsnorm(o, head_g) * jax.nn.silu(split_heads(g_raw, HGRN_DV))
    return o.reshape(B, T, HGRN_VAL_WIDTH).astype(q_raw.dtype), s_new.astype(s0.dtype)


def short_conv_branch(b_gate, c_gate, v, buf0, w_conv, b_conv):
    u = c_gate * v
    buf = jnp.concatenate([buf0.astype(u.dtype), u], axis=1)
    T = u.shape[1]
    conv = sum((buf[:, j:j + T] * w_conv[j] for j in range(CONV_K)), b_conv)
    return b_gate * conv, buf[:, T:]


def route(h, w_router, router_bias):
    T = h.shape[0]
    scores = jax.nn.sigmoid((h @ w_router).astype(jnp.float32))
    sel = scores + router_bias.astype(jnp.float32)
    grp = sel.reshape(T, N_GROUPS, N_EXPERTS // N_GROUPS)
    grp_score = lax.top_k(grp, 2)[0].sum(-1)
    _, top_g = lax.top_k(grp_score, TOPK_GROUPS)
    gmask = jax.nn.one_hot(top_g, N_GROUPS, dtype=jnp.float32).sum(1)
    emask = jnp.repeat(gmask, N_EXPERTS // N_GROUPS, axis=1) > 0
    _, top_i = lax.top_k(jnp.where(emask, sel, -jnp.inf), TOP_K)
    w = jnp.take_along_axis(scores, top_i, axis=1)
    w = w / jnp.sum(w, axis=-1, keepdims=True) * ROUTED_SCALE
    return top_i, w.astype(h.dtype)


def routed_experts(h, top_i, top_w, w_gate, w_up, w_down):
    T, D = h.shape
    A = T * TOP_K
    n_blocks = (A + N_EXPERTS * (MOE_BLOCK - 1)) // MOE_BLOCK
    n_rows = n_blocks * MOE_BLOCK
    e_flat = top_i.reshape(A).astype(jnp.int32)
    tok_flat = jnp.repeat(jnp.arange(T, dtype=jnp.int32), TOP_K)
    w_flat = top_w.reshape(A)
    order = jnp.argsort(e_flat)
    e_sorted = e_flat[order]
    counts = jnp.bincount(e_flat, length=N_EXPERTS).astype(jnp.int32)
    padded = (counts + MOE_BLOCK - 1) // MOE_BLOCK * MOE_BLOCK
    pad_end = jnp.cumsum(padded)
    pad_start = pad_end - padded
    start = jnp.cumsum(counts) - counts
    dest = pad_start[e_sorted] + jnp.arange(A, dtype=jnp.int32) - start[e_sorted]
    row_tok = jnp.full((n_rows,), T, jnp.int32).at[dest].set(tok_flat[order])
    row_w = jnp.zeros((n_rows,), h.dtype).at[dest].set(w_flat[order])
    blk_exp = jnp.minimum(
        jnp.searchsorted(pad_end, jnp.arange(n_blocks, dtype=jnp.int32) * MOE_BLOCK, side="right"),
        N_EXPERTS - 1)
    h_pad = jnp.concatenate([h, jnp.zeros((1, D), h.dtype)], axis=0)

    def block(args):
        tok, e = args
        return swiglu(h_pad[tok], w_gate[e], w_up[e], w_down[e])

    out = lax.map(block, (row_tok.reshape(n_blocks, MOE_BLOCK), blk_exp))
    out = out.reshape(n_rows, D) * row_w[:, None]
    return jax.ops.segment_sum(out, row_tok, num_segments=T + 1)[:T]


def layer(x, c, s_hgrn, s_conv, lb, p):
    B, T, D = x.shape
    mod = jax.nn.silu(c) @ p["w_ada"] + p["b_ada"]
    sh1, sc1, g1, sh2, sc2, g2 = [m[:, None, :] for m in jnp.split(mod, 6, axis=-1)]
    h = rmsnorm(x, p["norm_mix_g"]) * (1.0 + sc1) + sh1
    proj = h @ p["w_in"]
    qa, fa, ia, ga, bb, cc, vb, mga, mgb = jnp.split(proj, _split_points(), axis=-1)
    ya, s_hgrn_new = hgrn2_branch(qa, fa, ia, ga, s_hgrn, lb, p["hgrn_norm_g"])
    yb, s_conv_new = short_conv_branch(bb, cc, vb, s_conv, p["conv_w"], p["conv_b"])
    mixed = (jax.nn.sigmoid(mga) * (ya @ p["w_out_hgrn"])
             + jax.nn.sigmoid(mgb) * (yb @ p["w_out_conv"]))
    x = x + g1 * (mixed @ p["w_o"])
    h2 = (rmsnorm(x, p["norm_ffn_g"]) * (1.0 + sc2) + sh2).reshape(B * T, D)
    top_i, top_w = route(h2, p["w_router"], p["router_bias"])
    ffn = (routed_experts(h2, top_i, top_w, p["w_exp_gate"], p["w_exp_up"], p["w_exp_down"])
           + swiglu(h2, p["w_sh_gate"], p["w_sh_up"], p["w_sh_down"]))
    x = x + g2 * ffn.reshape(B, T, D)
    return x, s_hgrn_new, s_conv_new


def setup_inputs(seed: int = 0) -> dict:
    key = jax.random.key(seed)
    ks = jax.random.split(key, 32)
    nrm = lambda k, shape, scale: jax.random.normal(k, shape, jnp.float32) * scale
    D = D_MODEL
    return {
        "x_prompt": nrm(ks[0], (BATCH, SEQ, D), 1.0),
        "x_sample": nrm(ks[1], (DEC_BATCH, DEC_SEQ, D), 1.0),
        "state_hgrn": nrm(ks[2], (DEPTH, DEC_BATCH, HGRN_HEADS, HGRN_DK, HGRN_DV), 0.3),
        "state_conv": nrm(ks[3], (DEPTH, DEC_BATCH, CONV_K - 1, CONV_WIDTH), 1.0),
        "c_prompt": nrm(ks[4], (BATCH, D), 1.0),
        "c_sample": nrm(ks[5], (DEC_BATCH, D), 1.0),
        "w_ada": nrm(ks[6], (DEPTH, D, 6 * D), 0.5 * D ** -0.5),
        "b_ada": nrm(ks[7], (DEPTH, 6 * D), 0.02),
        "norm_mix_g": 1.0 + nrm(ks[8], (DEPTH, D), 0.02),
        "norm_ffn_g": 1.0 + nrm(ks[9], (DEPTH, D), 0.02),
        "w_in": nrm(ks[10], (DEPTH, D, IN_WIDTH), D ** -0.5),
        "lb_logits": nrm(ks[11], (DEPTH + 1, HGRN_KEY_WIDTH), 0.3),
        "hgrn_norm_g": 1.0 + nrm(ks[12], (DEPTH, HGRN_HEADS, HGRN_DV), 0.02),
        "conv_w": nrm(ks[13], (DEPTH, CONV_K, CONV_WIDTH), CONV_K ** -0.5),
        "conv_b": nrm(ks[14], (DEPTH, CONV_WIDTH), 0.02),
        "w_out_hgrn": nrm(ks[15], (DEPTH, HGRN_VAL_WIDTH, D), HGRN_VAL_WIDTH ** -0.5),
        "w_out_conv": nrm(ks[16], (DEPTH, CONV_WIDTH, D), CONV_WIDTH ** -0.5),
        "w_o": nrm(ks[17], (DEPTH, D, D), D ** -0.5),
        "w_router": nrm(ks[18], (DEPTH, D, N_EXPERTS), D ** -0.5),
        "router_bias": nrm(ks[19], (DEPTH, N_EXPERTS), 0.01),
        "w_exp_gate": nrm(ks[20], (DEPTH, N_EXPERTS, D, EXPERT_FF), D ** -0.5),
        "w_exp_up": nrm(ks[21], (DEPTH, N_EXPERTS, D, EXPERT_FF), D ** -0.5),
        "w_exp_down": nrm(ks[22], (DEPTH, N_EXPERTS, EXPERT_FF, D), EXPERT_FF ** -0.5),
        "w_sh_gate": nrm(ks[23], (DEPTH, D, SHARED_FF), D ** -0.5),
        "w_sh_up": nrm(ks[24], (DEPTH, D, SHARED_FF), D ** -0.5),
        "w_sh_down": nrm(ks[25], (DEPTH, SHARED_FF, D), SHARED_FF ** -0.5),
        "final_norm_g": 1.0 + nrm(ks[26], (D,), 0.02),
    }


def reference(x_prompt, x_sample, state_hgrn, state_conv, c_prompt, c_sample,
              w_ada, b_ada, norm_mix_g, norm_ffn_g, w_in, lb_logits, hgrn_norm_g,
              conv_w, conv_b, w_out_hgrn, w_out_conv, w_o, w_router, router_bias,
              w_exp_gate, w_exp_up, w_exp_down, w_sh_gate, w_sh_up, w_sh_down, final_norm_g):
    lb_all = jnp.cumsum(jax.nn.softmax(lb_logits.astype(jnp.float32), axis=0), axis=0)
    bp = x_prompt.shape[0]
    xp, xs = x_prompt, x_sample
    zero_h = jnp.zeros((bp, HGRN_HEADS, HGRN_DK, HGRN_DV), x_prompt.dtype)
    zero_c = jnp.zeros((bp, CONV_K - 1, CONV_WIDTH), x_prompt.dtype)
    hp_list, cp_list, hs_list, cs_list = [], [], [], []
    for l in range(DEPTH):
        p = {
            "w_ada": w_ada[l], "b_ada": b_ada[l], "norm_mix_g": norm_mix_g[l], "norm_ffn_g": norm_ffn_g[l],
            "w_in": w_in[l], "hgrn_norm_g": hgrn_norm_g[l], "conv_w": conv_w[l], "conv_b": conv_b[l],
            "w_out_hgrn": w_out_hgrn[l], "w_out_conv": w_out_conv[l], "w_o": w_o[l],
            "w_router": w_router[l], "router_bias": router_bias[l],
            "w_exp_gate": w_exp_gate[l], "w_exp_up": w_exp_up[l], "w_exp_down": w_exp_down[l],
            "w_sh_gate": w_sh_gate[l], "w_sh_up": w_sh_up[l], "w_sh_down": w_sh_down[l],
        }
        xp, hp, cp = layer(xp, c_prompt, zero_h, zero_c, lb_all[l], p)
        xs, hs, cs = layer(xs, c_sample, state_hgrn[l], state_conv[l], lb_all[l], p)
        hp_list.append(hp)
        cp_list.append(cp)
        hs_list.append(hs)
        cs_list.append(cs)
    y_prompt = rmsnorm(xp, final_norm_g)
    y_sample = rmsnorm(xs, final_norm_g)
    return (y_prompt, y_sample, jnp.stack(hp_list), jnp.stack(cp_list), jnp.stack(hs_list), jnp.stack(cs_list))
```

```python
import functools

import jax
import jax.numpy as jnp
from jax import lax
from jax.experimental import pallas as pl
from jax.experimental.pallas import tpu as pltpu

F32 = jnp.float32
BF16 = jnp.bfloat16

D_MODEL = 1024
HEADS = 4
DK = 128
KEY_W = HEADS * DK
CONV_W = 512
CONV_K = 3
IN_W = 2 * KEY_W + 2 * KEY_W + 3 * CONV_W + 2 * D_MODEL
N_EXP = 64
TOP_K = 8
N_GRP = 8
GRP_SZ = N_EXP // N_GRP
TOPK_GRP = 4
EXP_FF = 256
SH_FF = 256
ROUTED_SCALE = 2.5
EPS = 1e-6

C_Q, C_F, C_I, C_G = 0, 512, 1024, 1536
C_BB, C_CC, C_VB = 2048, 2560, 3072
C_MGA, C_MGB = 3584, 4608

MIX_TILE = 256
SUB = 128
HALF = SUB // 2
ROUTE_TILE = 384
GMM_BM = 256
FINAL_TILE = 256
SMP_GROUP = 8
VMEM_LIMIT = 56 * 1024 * 1024


def _dot(a, b):
    return jnp.dot(a, b, preferred_element_type=F32)


def _dot_nt(a, b):
    return lax.dot_general(a, b, (((1,), (1,)), ((), ())), preferred_element_type=F32)


def _dot_tn(a, b):
    return lax.dot_general(a, b, (((0,), (0,)), ((), ())), preferred_element_type=F32)


def _sigmoid(x):
    return 1.0 / (1.0 + jnp.exp(-x))


def _silu(x):
    return x * _sigmoid(x)


def _rms(x):
    return x * lax.rsqrt(jnp.mean(x * x, axis=-1, keepdims=True) + EPS)


def _lower_bound(lbl):
    a, b = lbl[0:1], lbl[1:2]
    m = jnp.maximum(a, b)
    ea, eb = jnp.exp(a - m), jnp.exp(b - m)
    return ea / (ea + eb)


def _split3(x):
    hi = x.astype(BF16)
    r1 = x - hi.astype(F32)
    mid = r1.astype(BF16)
    lo = (r1 - mid.astype(F32)).astype(BF16)
    return hi, mid, lo


def _ada_kernel(c_ref, w_ref, b_ref, o_ref):
    a = _silu(c_ref[...]).astype(BF16)
    o_ref[...] = _dot(a, w_ref[...].astype(BF16)) + b_ref[...]


def _ada(c_all, w_ada, b_ada):
    n = c_all.shape[0]
    blk = 1024
    return pl.pallas_call(
        _ada_kernel,
        grid=(6 * D_MODEL // blk,),
        in_specs=[pl.BlockSpec((n, D_MODEL), lambda j: (0, 0)),
                  pl.BlockSpec((D_MODEL, blk), lambda j: (0, j)),
                  pl.BlockSpec((1, blk), lambda j: (0, j))],
        out_specs=pl.BlockSpec((n, blk), lambda j: (0, j)),
        out_shape=jax.ShapeDtypeStruct((n, 6 * D_MODEL), F32),
        name="ada",
    )(c_all, w_ada, b_ada.reshape(1, -1))


def _ffn_pre(x1, mod_rows, gffn, w_sgu, w_sd, wr_hi, wr_lo):
    sh2, sc2, g2 = mod_rows
    h2 = _rms(x1) * gffn * (1.0 + sc2) + sh2
    h2b = h2.astype(BF16)
    gu = _dot(h2b, w_sgu)
    act = _silu(gu[:, :SH_FF]) * gu[:, SH_FF:]
    xmid = x1 + g2 * _dot(act.astype(BF16), w_sd)
    h2lo = (h2 - h2b.astype(F32)).astype(BF16)
    lgt = _dot_nt(wr_hi, h2b) + _dot_nt(wr_hi, h2lo) + _dot_nt(wr_lo, h2b)
    return xmid, h2b, lgt


def _mix_kernel(x_ref, mod_ref, gmix_ref, gffn_ref, w_in_ref, lbl_ref, hg_ref, cw_ref, cb_ref,
                w_oh_ref, w_oc_ref, w_o_ref, wr_hi_ref, wr_lo_ref, w_sgu_ref, w_sd_ref,
                xmid_ref, h2_ref, lgt_ref, s_out_ref, cv_out_ref,
                proj_ref, st_ref, cbuf_ref, ya_ref):
    t = pl.program_id(1)
    nt = pl.num_programs(1)
    tt = x_ref.shape[1]

    @pl.when(t == 0)
    def _():
        st_ref[...] = jnp.zeros_like(st_ref)
        cbuf_ref[...] = jnp.zeros_like(cbuf_ref)

    x = x_ref[0]
    mod = mod_ref[0]
    sh1, sc1, g1 = mod[0:1], mod[1:2], mod[2:3]
    h = _rms(x) * gmix_ref[...] * (1.0 + sc1) + sh1
    hb = h.astype(BF16)
    for c in range(0, IN_W, 512):
        proj_ref[:, c:c + 512] = _dot(hb, w_in_ref[:, c:c + 512])

    lb = _lower_bound(lbl_ref[...])
    row = lax.broadcasted_iota(jnp.int32, (SUB, SUB), 0)
    col = lax.broadcasted_iota(jnp.int32, (SUB, SUB), 1)
    tri = (col <= row).astype(BF16)
    mask_d = (col <= row) & ((row >= HALF) == (col >= HALF))
    top_half = lax.broadcasted_iota(jnp.int32, (SUB, DK), 0) < HALF

    for s in range(tt // SUB):
        r0 = s * SUB
        f = lb + (1.0 - lb) * _sigmoid(proj_ref[r0:r0 + SUB, C_F:C_F + KEY_W])
        kk = 1.0 - f
        hi, mid, lo = _split3(jnp.log(f))
        bc = _dot(tri, hi) + _dot(tri, mid) + _dot(tri, lo)
        for hd in range(HEADS):
            hs = slice(hd * DK, (hd + 1) * DK)
            bh = bc[:, hs]
            b31, b63 = bh[HALF // 2 - 1:HALF // 2], bh[HALF - 1:HALF]
            b95, b127 = bh[HALF + HALF // 2 - 1:HALF + HALF // 2], bh[SUB - 1:SUB]
            arg = bh - jnp.where(top_half, b31, b95)
            e_pos, e_neg = jnp.exp(arg), jnp.exp(-arg)
            q = _silu(proj_ref[r0:r0 + SUB, C_Q + hd * DK:C_Q + (hd + 1) * DK])
            v = proj_ref[r0:r0 + SUB, C_I + hd * DK:C_I + (hd + 1) * DK]
            qd = q * e_pos
            kd = kk[:, hs] * e_neg
            q_in = qd * jnp.where(top_half, jnp.exp(b31), jnp.exp(b95))
            k_end = kd * jnp.where(top_half, jnp.exp(b127 - b31), jnp.exp(b127 - b95))
            qa = jnp.where(top_half, 0.0, qd * jnp.exp(b95 - b63))
            ka = jnp.where(top_half, kd * jnp.exp(b63 - b31), 0.0)
            att = jnp.where(mask_d, _dot_nt(qd.astype(BF16), kd.astype(BF16)), 0.0)
            att = att + _dot_nt(qa.astype(BF16), ka.astype(BF16))
            vb = v.astype(BF16)
            st = st_ref[hd]
            o = _dot(att.astype(BF16), vb) + _dot_nt(q_in.astype(BF16), st.astype(BF16))
            st_ref[hd] = st * jnp.exp(b127) + _dot_tn(vb, k_end.astype(BF16))
            gate = _silu(proj_ref[r0:r0 + SUB, C_G + hd * DK:C_G + (hd + 1) * DK])
            ya_ref[r0:r0 + SUB, hs] = _rms(o) * hg_ref[:, hs] * gate

    u = proj_ref[:, C_CC:C_CC + CONV_W] * proj_ref[:, C_VB:C_VB + CONV_W]
    rows = lax.broadcasted_iota(jnp.int32, (tt, CONV_W), 0)
    c0, c1 = cbuf_ref[0:1], cbuf_ref[1:2]
    u1 = jnp.where(rows == 0, c1, pltpu.roll(u, 1, axis=0))
    u2 = jnp.where(rows == 0, c0, jnp.where(rows == 1, c1, pltpu.roll(u, 2, axis=0)))
    conv = cw_ref[0:1] * u2 + cw_ref[1:2] * u1 + cw_ref[2:3] * u + cb_ref[...]
    yb = proj_ref[:, C_BB:C_BB + CONV_W] * conv
    cbuf_ref[...] = u[tt - 2:tt]

    mixed = (_sigmoid(proj_ref[:, C_MGA:C_MGA + D_MODEL]) * _dot(ya_ref[...].astype(BF16), w_oh_ref[...])
             + _sigmoid(proj_ref[:, C_MGB:C_MGB + D_MODEL]) * _dot(yb.astype(BF16), w_oc_ref[...]))
    x1 = x + g1 * _dot(mixed.astype(BF16), w_o_ref[...])

    xmid, h2b, lgt = _ffn_pre(x1, (mod[3:4], mod[4:5], mod[5:6]), gffn_ref[...],
                              w_sgu_ref[...], w_sd_ref[...], wr_hi_ref[...], wr_lo_ref[...])
    xmid_ref[0] = xmid
    h2_ref[0] = h2b
    lgt_ref[0] = lgt

    @pl.when(t == nt - 1)
    def _():
        for hd in range(HEADS):
            s_out_ref[0, hd] = st_ref[hd].T
        cv_out_ref[0] = cbuf_ref[...]


def _const_spec(shape):
    nd = len(shape)
    return pl.BlockSpec(shape, lambda b, t, _nd=nd: (0,) * _nd, pipeline_mode=pl.Buffered(1))


def _mix(x, mod, gmix, gffn, w_in, lbl, hg, cw, cb, w_oh, w_oc, w_o, wr_hi, wr_lo, w_sgu, w_sd):
    bsz, seq, _ = x.shape
    tt = MIX_TILE
    consts = [gmix, gffn, w_in, lbl, hg, cw, cb, w_oh, w_oc, w_o, wr_hi, wr_lo, w_sgu, w_sd]
    return pl.pallas_call(
        _mix_kernel,
        grid=(bsz, seq // tt),
        in_specs=[pl.BlockSpec((1, tt, D_MODEL), lambda b, t: (b, t, 0)),
                  pl.BlockSpec((1, 6, D_MODEL), lambda b, t: (b, 0, 0))]
                 + [_const_spec(a.shape) for a in consts],
        out_specs=[pl.BlockSpec((1, tt, D_MODEL), lambda b, t: (b, t, 0)),
                   pl.BlockSpec((1, tt, D_MODEL), lambda b, t: (b, t, 0)),
                   pl.BlockSpec((1, N_EXP, tt), lambda b, t: (b, 0, t)),
                   pl.BlockSpec((1, HEADS, DK, DK), lambda b, t: (b, 0, 0, 0)),
                   pl.BlockSpec((1, CONV_K - 1, CONV_W), lambda b, t: (b, 0, 0))],
        out_shape=[jax.ShapeDtypeStruct((bsz, seq, D_MODEL), F32),
                   jax.ShapeDtypeStruct((bsz, seq, D_MODEL), BF16),
                   jax.ShapeDtypeStruct((bsz, N_EXP, seq), F32),
                   jax.ShapeDtypeStruct((bsz, HEADS, DK, DK), F32),
                   jax.ShapeDtypeStruct((bsz, CONV_K - 1, CONV_W), F32)],
        scratch_shapes=[pltpu.VMEM((tt, IN_W), F32),
                        pltpu.VMEM((HEADS, DK, DK), F32),
                        pltpu.VMEM((CONV_K - 1, CONV_W), F32),
                        pltpu.VMEM((tt, KEY_W), F32)],
        compiler_params=pltpu.CompilerParams(
            dimension_semantics=("arbitrary", "arbitrary"), vmem_limit_bytes=VMEM_LIMIT),
        name="mix",
    )(x, mod, *consts)


def _smp1_kernel(x_ref, mod_ref, gmix_ref, w_in_ref, lbl_ref, cw_ref, cb_ref, cst_ref,
                 f_ref, k_ref, q_ref, v_ref, gate_ref, yb_ref, sga_ref, sgb_ref, cv_out_ref):
    x = x_ref[...]
    sh1, sc1 = mod_ref[:, 0:D_MODEL], mod_ref[:, D_MODEL:2 * D_MODEL]
    h = _rms(x) * gmix_ref[...] * (1.0 + sc1) + sh1
    hb = h.astype(BF16)

    def proj(c, w):
        return _dot(hb, w_in_ref[:, c:c + w])

    lb = _lower_bound(lbl_ref[...])
    f = lb + (1.0 - lb) * _sigmoid(proj(C_F, KEY_W))
    f_ref[...] = f
    k_ref[...] = 1.0 - f
    q_ref[...] = _silu(proj(C_Q, KEY_W))
    v_ref[...] = proj(C_I, KEY_W)
    gate_ref[...] = _silu(proj(C_G, KEY_W))
    u = proj(C_CC, CONV_W) * proj(C_VB, CONV_W)
    c0, c1 = cst_ref[:, 0:CONV_W], cst_ref[:, CONV_W:2 * CONV_W]
    conv = cw_ref[0:1] * c0 + cw_ref[1:2] * c1 + cw_ref[2:3] * u + cb_ref[...]
    yb_ref[...] = proj(C_BB, CONV_W) * conv
    cv_out_ref[:, 0:CONV_W] = c1
    cv_out_ref[:, CONV_W:2 * CONV_W] = u
    sga_ref[...] = _sigmoid(proj(C_MGA, D_MODEL))
    sgb_ref[...] = _sigmoid(proj(C_MGB, D_MODEL))


def _smp1(x, mod, gmix, w_in, lbl, cw, cb, cst):
    n = x.shape[0]
    kw = jax.ShapeDtypeStruct((n, KEY_W), F32)
    dm = jax.ShapeDtypeStruct((n, D_MODEL), F32)
    return pl.pallas_call(
        _smp1_kernel,
        out_shape=[kw, kw, kw, kw, kw, kw, dm, dm,
                   jax.ShapeDtypeStruct((n, (CONV_K - 1) * CONV_W), F32)],
        compiler_params=pltpu.CompilerParams(vmem_limit_bytes=VMEM_LIMIT),
        name="smp1",
    )(x, mod, gmix, w_in, lbl, cw, cb, cst)


def _smp2_kernel(f_ref, k_ref, q_ref, v_ref, s_ref, s_out_ref, o_ref):
    g = f_ref.shape[0]
    for i in range(g):
        for hd in range(HEADS):
            hs = slice(hd * DK, (hd + 1) * DK)

            def col(ref):
                return jnp.broadcast_to(ref[i:i + 1, hs], (DK, DK)).T

            vrow = v_ref[i:i + 1, hs]
            s_new = col(f_ref) * s_ref[i, hd] + col(k_ref) * vrow
            s_out_ref[i, hd] = s_new
            o_ref[i:i + 1, hs] = jnp.sum(col(q_ref) * s_new, axis=0, keepdims=True)


def _smp2(f, k, q, v, state):
    n = f.shape[0]
    g = SMP_GROUP
    row_spec = pl.BlockSpec((g, KEY_W), lambda i: (i, 0))
    st_spec = pl.BlockSpec((g, HEADS, DK, DK), lambda i: (i, 0, 0, 0))
    return pl.pallas_call(
        _smp2_kernel,
        grid=(n // g,),
        in_specs=[row_spec, row_spec, row_spec, row_spec, st_spec],
        out_specs=[st_spec, row_spec],
        out_shape=[jax.ShapeDtypeStruct(state.shape, F32), jax.ShapeDtypeStruct((n, KEY_W), F32)],
        compiler_params=pltpu.CompilerParams(dimension_semantics=("arbitrary",)),
        name="smp2",
    )(f, k, q, v, state)


def _smp3_kernel(x_ref, mod_ref, o_ref, gate_ref, yb_ref, sga_ref, sgb_ref, hg_ref, gffn_ref,
                 w_oh_ref, w_oc_ref, w_o_ref, wr_hi_ref, wr_lo_ref, w_sgu_ref, w_sd_ref,
                 xmid_ref, h2_ref, lgt_ref):
    parts = []
    for hd in range(HEADS):
        hs = slice(hd * DK, (hd + 1) * DK)
        parts.append(_rms(o_ref[:, hs]) * hg_ref[:, hs] * gate_ref[:, hs])
    ya = jnp.concatenate(parts, axis=1)
    mixed = (sga_ref[...] * _dot(ya.astype(BF16), w_oh_ref[...])
             + sgb_ref[...] * _dot(yb_ref[...].astype(BF16), w_oc_ref[...]))
    g1 = mod_ref[:, 2 * D_MODEL:3 * D_MODEL]
    x1 = x_ref[...] + g1 * _dot(mixed.astype(BF16), w_o_ref[...])
    mod_rows = tuple(mod_ref[:, j * D_MODEL:(j + 1) * D_MODEL] for j in (3, 4, 5))
    xmid, h2b, lgt = _ffn_pre(x1, mod_rows, gffn_ref[...], w_sgu_ref[...], w_sd_ref[...],
                              wr_hi_ref[...], wr_lo_ref[...])
    xmid_ref[...] = xmid
    h2_ref[...] = h2b
    lgt_ref[...] = lgt


def _smp3(x, mod, o, gate, yb, sga, sgb, hg, gffn, w_oh, w_oc, w_o, wr_hi, wr_lo, w_sgu, w_sd):
    n = x.shape[0]
    return pl.pallas_call(
        _smp3_kernel,
        out_shape=[jax.ShapeDtypeStruct((n, D_MODEL), F32),
                   jax.ShapeDtypeStruct((n, D_MODEL), BF16),
                   jax.ShapeDtypeStruct((N_EXP, n), F32)],
        compiler_params=pltpu.CompilerParams(vmem_limit_bytes=VMEM_LIMIT),
        name="smp3",
    )(x, mod, o, gate, yb, sga, sgb, hg, gffn, w_oh, w_oc, w_o, wr_hi, wr_lo, w_sgu, w_sd)


def _route_kernel(lgt_ref, bias_ref, idx_ref, w_ref, rank_ref, cnt_ref, carry_ref):
    i = pl.program_id(0)
    tr = lgt_ref.shape[1]
    neg = -jnp.inf

    @pl.when(i == 0)
    def _():
        carry_ref[...] = jnp.zeros_like(carry_ref)

    scores = _sigmoid(lgt_ref[...])
    sel = scores + bias_ref[...]
    j8 = lax.broadcasted_iota(jnp.int32, (GRP_SZ, tr), 0)
    groups = [sel[g * GRP_SZ:(g + 1) * GRP_SZ] for g in range(N_GRP)]
    gscore = []
    for grp in groups:
        m1 = jnp.max(grp, axis=0, keepdims=True)
        i1 = jnp.min(jnp.where(grp == m1, j8, GRP_SZ), axis=0, keepdims=True)
        m2 = jnp.max(jnp.where(j8 == i1, neg, grp), axis=0, keepdims=True)
        gscore.append(m1 + m2)
    kept = []
    for g in range(N_GRP):
        beaten = jnp.zeros((1, tr), jnp.int32)
        for o in range(N_GRP):
            if o < g:
                beaten = beaten + (gscore[o] >= gscore[g]).astype(jnp.int32)
            elif o > g:
                beaten = beaten + (gscore[o] > gscore[g]).astype(jnp.int32)
        kept.append(jnp.where(beaten < TOPK_GRP, groups[g], neg))
    masked = jnp.concatenate(kept, axis=0)
    ei = lax.broadcasted_iota(jnp.int32, masked.shape, 0)
    chosen = jnp.zeros(masked.shape, jnp.bool_)
    picks, weights = [], []
    for _ in range(TOP_K):
        m = jnp.max(masked, axis=0, keepdims=True)
        pick = jnp.min(jnp.where(masked == m, ei, N_EXP), axis=0, keepdims=True)
        hit = ei == pick
        weights.append(jnp.sum(jnp.where(hit, scores, 0.0), axis=0, keepdims=True))
        picks.append(pick)
        chosen = chosen | hit
        masked = jnp.where(hit, neg, masked)
    wsum = weights[0]
    for w in weights[1:]:
        wsum = wsum + w
    sel01 = chosen.astype(F32)
    r = lax.broadcasted_iota(jnp.int32, (tr, tr), 0)
    c = lax.broadcasted_iota(jnp.int32, (tr, tr), 1)
    before = (r < c).astype(BF16)
    cnt = _dot(sel01.astype(BF16), before) + carry_ref[...]
    for k in range(TOP_K):
        idx_ref[k:k + 1, :] = picks[k]
        w_ref[k:k + 1, :] = weights[k] / wsum * ROUTED_SCALE
        rank = jnp.sum(jnp.where(ei == picks[k], cnt, 0.0), axis=0, keepdims=True)
        rank_ref[k:k + 1, :] = rank.astype(jnp.int32)
    carry_ref[...] = carry_ref[...] + jnp.sum(sel01, axis=1, keepdims=True)
    cnt_ref[...] = jnp.broadcast_to(carry_ref[...], cnt_ref.shape).astype(jnp.int32)


def _route(lgt, bias):
    n = lgt.shape[1]
    tr = ROUTE_TILE
    slot = lambda dt: jax.ShapeDtypeStruct((TOP_K, n), dt)
    slot_spec = pl.BlockSpec((TOP_K, tr), lambda i: (0, i))
    return pl.pallas_call(
        _route_kernel,
        grid=(n // tr,),
        in_specs=[pl.BlockSpec((N_EXP, tr), lambda i: (0, i)),
                  pl.BlockSpec((N_EXP, 1), lambda i: (0, 0))],
        out_specs=[slot_spec, slot_spec, slot_spec, pl.BlockSpec((N_EXP, 128), lambda i: (0, 0))],
        out_shape=[slot(jnp.int32), slot(F32), slot(jnp.int32),
                   jax.ShapeDtypeStruct((N_EXP, 128), jnp.int32)],
        scratch_shapes=[pltpu.VMEM((N_EXP, 1), F32)],
        compiler_params=pltpu.CompilerParams(dimension_semantics=("arbitrary",)),
        name="route",
    )(lgt, bias)


def _gmm_kernel(blk_exp_ref, n_used_ref, x_ref, wg_ref, wu_ref, wd_ref, y_ref):
    j = pl.program_id(0)

    @pl.when(j < n_used_ref[0])
    def _():
        x = x_ref[...]
        g = _dot(x, wg_ref[0].astype(BF16))
        u = _dot(x, wu_ref[0].astype(BF16))
        act = (_silu(g) * u).astype(BF16)
        y_ref[...] = _dot(act, wd_ref[0].astype(BF16)).astype(BF16)


def _gmm(blk_exp, n_used, xs, w_gate, w_up, w_down):
    n_rows = xs.shape[0]
    bm = GMM_BM
    nb = n_rows // bm

    def row_map(j, be, nu):
        return (jnp.minimum(j, nu[0] - 1), 0)

    def w_map(j, be, nu):
        return (be[jnp.minimum(j, nu[0] - 1)], 0, 0)

    grid_spec = pltpu.PrefetchScalarGridSpec(
        num_scalar_prefetch=2,
        grid=(nb,),
        in_specs=[pl.BlockSpec((bm, D_MODEL), row_map),
                  pl.BlockSpec((1, D_MODEL, EXP_FF), w_map),
                  pl.BlockSpec((1, D_MODEL, EXP_FF), w_map),
                  pl.BlockSpec((1, EXP_FF, D_MODEL), w_map)],
        out_specs=pl.BlockSpec((bm, D_MODEL), row_map),
    )
    return pl.pallas_call(
        _gmm_kernel,
        grid_spec=grid_spec,
        out_shape=jax.ShapeDtypeStruct((n_rows, D_MODEL), BF16),
        compiler_params=pltpu.CompilerParams(dimension_semantics=("arbitrary",)),
        name="gmm",
    )(blk_exp, n_used, xs, w_gate, w_up, w_down)


def _final_kernel(xmid_ref, g2_ref, z_ref, w_ref, gfin_ref, y_ref):
    acc = w_ref[:, 0:1] * z_ref[0].astype(F32)
    for k in range(1, TOP_K):
        acc = acc + w_ref[:, k:k + 1] * z_ref[k].astype(F32)
    y_ref[...] = _rms(xmid_ref[...] + g2_ref[0] * acc) * gfin_ref[...]


def _final(xmid, g2, z, w_t, gfin, tile, tok0):
    n_tok = xmid.shape[0]
    t0 = tok0 // tile
    tiles_per_g2 = n_tok // tile // g2.shape[0]
    return pl.pallas_call(
        _final_kernel,
        grid=(n_tok // tile,),
        in_specs=[pl.BlockSpec((tile, D_MODEL), lambda i: (i, 0)),
                  pl.BlockSpec((1, g2.shape[1], D_MODEL), lambda i: (i // tiles_per_g2, 0, 0)),
                  pl.BlockSpec((TOP_K, tile, D_MODEL), lambda i: (0, t0 + i, 0)),
                  pl.BlockSpec((tile, TOP_K), lambda i: (t0 + i, 0)),
                  pl.BlockSpec((1, D_MODEL), lambda i: (0, 0))],
        out_specs=pl.BlockSpec((tile, D_MODEL), lambda i: (i, 0)),
        out_shape=jax.ShapeDtypeStruct((n_tok, D_MODEL), F32),
        compiler_params=pltpu.CompilerParams(dimension_semantics=("arbitrary",)),
        name="final",
    )(xmid, g2, z, w_t, gfin)


def kernel(x_prompt, x_sample, state_hgrn, state_conv, c_prompt, c_sample, w_ada, b_ada, norm_mix_g, norm_ffn_g, w_in, lb_logits, hgrn_norm_g, conv_w, conv_b, w_out_hgrn, w_out_conv, w_o, w_router, router_bias, w_exp_gate, w_exp_up, w_exp_down, w_sh_gate, w_sh_up, w_sh_down, final_norm_g):
    assert w_ada.shape[0] == 1 and lb_logits.shape[0] == 2
    bsz, seq, _ = x_prompt.shape
    n_smp = x_sample.shape[0]
    n_prompt = bsz * seq
    n_tok = n_prompt + n_smp

    w_in_b = w_in[0].astype(BF16)
    w_oh_b = w_out_hgrn[0].astype(BF16)
    w_oc_b = w_out_conv[0].astype(BF16)
    w_o_b = w_o[0].astype(BF16)
    wr_t = w_router[0].T
    wr_hi = wr_t.astype(BF16)
    wr_lo = (wr_t - wr_hi.astype(F32)).astype(BF16)
    w_sgu = jnp.concatenate([w_sh_gate[0], w_sh_up[0]], axis=1).astype(BF16)
    w_sd = w_sh_down[0].astype(BF16)
    gmix = norm_mix_g[0].reshape(1, D_MODEL)
    gffn = norm_ffn_g[0].reshape(1, D_MODEL)
    hg = hgrn_norm_g[0].reshape(1, KEY_W)
    cw = conv_w[0]
    cb = conv_b[0].reshape(1, CONV_W)
    gfin = final_norm_g.reshape(1, D_MODEL)

    mod = _ada(jnp.concatenate([c_prompt, c_sample], axis=0), w_ada[0], b_ada[0])
    mod_p, mod_s = mod[:bsz], mod[bsz:]

    xmid_p, h2_p, lgt_p, s_p, cv_p = _mix(
        x_prompt, mod_p.reshape(bsz, 6, D_MODEL), gmix, gffn, w_in_b, lb_logits, hg, cw, cb,
        w_oh_b, w_oc_b, w_o_b, wr_hi, wr_lo, w_sgu, w_sd)

    xs2 = x_sample.reshape(n_smp, D_MODEL)
    f, kk, q, v, gate, yb, sga, sgb, cv_s = _smp1(xs2, mod_s, gmix, w_in_b, lb_logits, cw, cb,
        state_conv[0].reshape(n_smp, (CONV_K - 1) * CONV_W))
    s_s, o_s = _smp2(f, kk, q, v, state_hgrn[0])
    xmid_s, h2_s, lgt_s = _smp3(xs2, mod_s, o_s, gate, yb, sga, sgb, hg, gffn,
                                w_oh_b, w_oc_b, w_o_b, wr_hi, wr_lo, w_sgu, w_sd)

    lgt_all = jnp.concatenate([lgt_p.transpose(1, 0, 2).reshape(N_EXP, n_prompt), lgt_s], axis=1)
    idx, w_tok, rank, cnt = _route(lgt_all, router_bias[0].reshape(N_EXP, 1))

    bm = GMM_BM
    n_assign = n_tok * TOP_K
    n_blocks = (n_assign + N_EXP * (bm - 1)) // bm
    n_rows = n_blocks * bm
    counts = cnt[:, 0]
    padded = (counts + bm - 1) // bm * bm
    pad_end = jnp.cumsum(padded)
    pad_start = pad_end - padded
    dest = (jnp.take(pad_start, idx) + rank).reshape(-1)
    blk_exp = jnp.minimum(
        jnp.searchsorted(pad_end, jnp.arange(n_blocks, dtype=jnp.int32) * bm, side="right"),
        N_EXP - 1).astype(jnp.int32)
    n_used = (pad_end[-1:] // bm).astype(jnp.int32)

    h2_all = jnp.concatenate([h2_p.reshape(n_prompt, D_MODEL), h2_s], axis=0)
    tok_of = jnp.tile(jnp.arange(n_tok, dtype=jnp.int32), TOP_K)
    row_tok = jnp.zeros((n_rows,), jnp.int32).at[dest].set(tok_of)
    xs = jnp.take(h2_all, row_tok, axis=0)
    ys = _gmm(blk_exp, n_used, xs, w_exp_gate[0], w_exp_up[0], w_exp_down[0])
    z = jnp.take(ys, dest, axis=0).reshape(TOP_K, n_tok, D_MODEL)

    w_t = w_tok.T
    y_p = _final(xmid_p.reshape(n_prompt, D_MODEL), mod_p[:, 5 * D_MODEL:].reshape(bsz, 1, D_MODEL),
                 z, w_t, gfin, FINAL_TILE, 0)
    y_s = _final(xmid_s, mod_s[:, 5 * D_MODEL:].reshape(1, n_smp, D_MODEL), z, w_t, gfin, n_smp, n_prompt)

    return (y_p.reshape(bsz, seq, D_MODEL), y_s.reshape(n_smp, 1, D_MODEL),
            s_p[None], cv_p[None], s_s[None], cv_s.reshape(1, n_smp, CONV_K - 1, CONV_W))
```

```python
import functools

import jax
import jax.numpy as jnp
from jax import lax
from jax.experimental import pallas as pl
from jax.experimental.pallas import tpu as pltpu
from jax.experimental.pallas import tpu_sc as plsc

F32 = jnp.float32
BF16 = jnp.bfloat16
I32 = jnp.int32

D_MODEL = 1024
HALF_D = D_MODEL // 2
HEADS = 4
DK = 128
KEY_W = HEADS * DK
CONV_W = 512
CONV_K = 3
IN_W = 2 * KEY_W + 2 * KEY_W + 3 * CONV_W + 2 * D_MODEL
N_EXP = 64
TOP_K = 8
N_GRP = 8
GRP_SZ = N_EXP // N_GRP
TOPK_GRP = 4
EXP_FF = 256
SH_FF = 256
ROUTED_SCALE = 2.5
EPS = 1e-6

C_Q, C_F, C_I, C_G = 0, 512, 1024, 1536
C_BB, C_CC, C_VB = 2048, 2560, 3072
C_MGA, C_MGB = 3584, 4608

MIX_TILE = 256
SUB = 128
HALF = SUB // 2
ROUTE_TILE = 384
GMM_BM = 256
FINAL_TILE = 256
SMP_GROUP = 8
SC_WINDOW = 128
PLANE_W = HALF_D // 2
VMEM_LIMIT = 56 * 1024 * 1024


def _dot(a, b):
    return jnp.dot(a, b, preferred_element_type=F32)


def _dot_nt(a, b):
    return lax.dot_general(a, b, (((1,), (1,)), ((), ())), preferred_element_type=F32)


def _dot_tn(a, b):
    return lax.dot_general(a, b, (((0,), (0,)), ((), ())), preferred_element_type=F32)


def _sigmoid(x):
    return 1.0 / (1.0 + jnp.exp(-x))


def _silu(x):
    return x * _sigmoid(x)


def _rms(x):
    return x * lax.rsqrt(jnp.mean(x * x, axis=-1, keepdims=True) + EPS)


def _lower_bound(lbl):
    a, b = lbl[0:1], lbl[1:2]
    m = jnp.maximum(a, b)
    ea, eb = jnp.exp(a - m), jnp.exp(b - m)
    return ea / (ea + eb)


def _split3(x):
    hi = x.astype(BF16)
    r1 = x - hi.astype(F32)
    mid = r1.astype(BF16)
    lo = (r1 - mid.astype(F32)).astype(BF16)
    return hi, mid, lo


def _pack_rows(xb, out_ref):
    bits = lax.bitcast_convert_type(xb.astype(F32), I32)
    lo = lax.shift_right_logical(bits[:, :HALF_D], 16)
    hi = bits[:, HALF_D:] & jnp.int32(-65536)
    words = lo | hi
    out_ref[0] = words[:, :PLANE_W]
    out_ref[1] = words[:, PLANE_W:]


def _unpack_rows(p0, p1):
    def halves(w):
        lo = lax.bitcast_convert_type(lax.shift_left(w, 16), F32)
        hi = lax.bitcast_convert_type(w & jnp.int32(-65536), F32)
        return lo.astype(BF16), hi.astype(BF16)

    c0, c2 = halves(p0)
    c1, c3 = halves(p1)
    return c0, c1, c2, c3


def _ada_kernel(c_ref, w_ref, b_ref, o_ref):
    a = _silu(c_ref[...]).astype(BF16)
    o_ref[...] = _dot(a, w_ref[...].astype(BF16)) + b_ref[...]


def _ada(c_all, w_ada, b_ada):
    n = c_all.shape[0]
    blk = 1024
    return pl.pallas_call(
        _ada_kernel,
        grid=(6 * D_MODEL // blk,),
        in_specs=[pl.BlockSpec((n, D_MODEL), lambda j: (0, 0)),
                  pl.BlockSpec((D_MODEL, blk), lambda j: (0, j)),
                  pl.BlockSpec((1, blk), lambda j: (0, j))],
        out_specs=pl.BlockSpec((n, blk), lambda j: (0, j)),
        out_shape=jax.ShapeDtypeStruct((n, 6 * D_MODEL), F32),
        name="ada",
    )(c_all, w_ada, b_ada.reshape(1, -1))


def _ffn_pre(x1, mod_rows, gffn, w_sgu, w_sd, wr_hi, wr_lo):
    sh2, sc2, g2 = mod_rows
    h2 = _rms(x1) * gffn * (1.0 + sc2) + sh2
    h2b = h2.astype(BF16)
    gu = _dot(h2b, w_sgu)
    act = _silu(gu[:, :SH_FF]) * gu[:, SH_FF:]
    xmid = x1 + g2 * _dot(act.astype(BF16), w_sd)
    h2lo = (h2 - h2b.astype(F32)).astype(BF16)
    lgt = _dot_nt(wr_hi, h2b) + _dot_nt(wr_hi, h2lo) + _dot_nt(wr_lo, h2b)
    return xmid, h2b, lgt


def _mix_kernel(x_ref, mod_ref, gmix_ref, gffn_ref, w_in_ref, lbl_ref, hg_ref, cw_ref, cb_ref,
                w_oh_ref, w_oc_ref, w_o_ref, wr_hi_ref, wr_lo_ref, w_sgu_ref, w_sd_ref,
                h2_all_ref, lgt_all_ref,
                xmid_ref, h2_ref, lgt_ref, s_out_ref, cv_out_ref,
                proj_ref, st_ref, cbuf_ref, ya_ref):
    del h2_all_ref, lgt_all_ref
    t = pl.program_id(1)
    nt = pl.num_programs(1)
    tt = x_ref.shape[1]

    @pl.when(t == 0)
    def _():
        st_ref[...] = jnp.zeros_like(st_ref)
        cbuf_ref[...] = jnp.zeros_like(cbuf_ref)

    x = x_ref[0]
    mod = mod_ref[0]
    sh1, sc1, g1 = mod[0:1], mod[1:2], mod[2:3]
    h = _rms(x) * gmix_ref[...] * (1.0 + sc1) + sh1
    hb = h.astype(BF16)
    for c in range(0, IN_W, 512):
        proj_ref[:, c:c + 512] = _dot(hb, w_in_ref[:, c:c + 512])

    lb = _lower_bound(lbl_ref[...])
    row = lax.broadcasted_iota(I32, (SUB, SUB), 0)
    col = lax.broadcasted_iota(I32, (SUB, SUB), 1)
    tri = (col <= row).astype(BF16)
    mask_d = (col <= row) & ((row >= HALF) == (col >= HALF))
    top_half = lax.broadcasted_iota(I32, (SUB, DK), 0) < HALF

    for s in range(tt // SUB):
        r0 = s * SUB
        f = lb + (1.0 - lb) * _sigmoid(proj_ref[r0:r0 + SUB, C_F:C_F + KEY_W])
        kk = 1.0 - f
        hi, mid, lo = _split3(jnp.log(f))
        bc = _dot(tri, hi) + _dot(tri, mid) + _dot(tri, lo)
        for hd in range(HEADS):
            hs = slice(hd * DK, (hd + 1) * DK)
            bh = bc[:, hs]
            b31, b63 = bh[HALF // 2 - 1:HALF // 2], bh[HALF - 1:HALF]
            b95, b127 = bh[HALF + HALF // 2 - 1:HALF + HALF // 2], bh[SUB - 1:SUB]
            arg = bh - jnp.where(top_half, b31, b95)
            e_pos, e_neg = jnp.exp(arg), jnp.exp(-arg)
            q = _silu(proj_ref[r0:r0 + SUB, C_Q + hd * DK:C_Q + (hd + 1) * DK])
            v = proj_ref[r0:r0 + SUB, C_I + hd * DK:C_I + (hd + 1) * DK]
            qd = q * e_pos
            kd = kk[:, hs] * e_neg
            q_in = qd * jnp.where(top_half, jnp.exp(b31), jnp.exp(b95))
            k_end = kd * jnp.where(top_half, jnp.exp(b127 - b31), jnp.exp(b127 - b95))
            qa = jnp.where(top_half, 0.0, qd * jnp.exp(b95 - b63))
            ka = jnp.where(top_half, kd * jnp.exp(b63 - b31), 0.0)
            att = jnp.where(mask_d, _dot_nt(qd.astype(BF16), kd.astype(BF16)), 0.0)
            att = att + _dot_nt(qa.astype(BF16), ka.astype(BF16))
            vb = v.astype(BF16)
            st = st_ref[hd]
            o = _dot(att.astype(BF16), vb) + _dot_nt(q_in.astype(BF16), st.astype(BF16))
            st_ref[hd] = st * jnp.exp(b127) + _dot_tn(vb, k_end.astype(BF16))
            gate = _silu(proj_ref[r0:r0 + SUB, C_G + hd * DK:C_G + (hd + 1) * DK])
            ya_ref[r0:r0 + SUB, hs] = _rms(o) * hg_ref[:, hs] * gate

    u = proj_ref[:, C_CC:C_CC + CONV_W] * proj_ref[:, C_VB:C_VB + CONV_W]
    rows = lax.broadcasted_iota(I32, (tt, CONV_W), 0)
    c0, c1 = cbuf_ref[0:1], cbuf_ref[1:2]
    u1 = jnp.where(rows == 0, c1, pltpu.roll(u, 1, axis=0))
    u2 = jnp.where(rows == 0, c0, jnp.where(rows == 1, c1, pltpu.roll(u, 2, axis=0)))
    conv = cw_ref[0:1] * u2 + cw_ref[1:2] * u1 + cw_ref[2:3] * u + cb_ref[...]
    yb = proj_ref[:, C_BB:C_BB + CONV_W] * conv
    cbuf_ref[...] = u[tt - 2:tt]

    mixed = (_sigmoid(proj_ref[:, C_MGA:C_MGA + D_MODEL]) * _dot(ya_ref[...].astype(BF16), w_oh_ref[...])
             + _sigmoid(proj_ref[:, C_MGB:C_MGB + D_MODEL]) * _dot(yb.astype(BF16), w_oc_ref[...]))
    x1 = x + g1 * _dot(mixed.astype(BF16), w_o_ref[...])

    xmid, h2b, lgt = _ffn_pre(x1, (mod[3:4], mod[4:5], mod[5:6]), gffn_ref[...],
                              w_sgu_ref[...], w_sd_ref[...], wr_hi_ref[...], wr_lo_ref[...])
    xmid_ref[0] = xmid
    _pack_rows(h2b, h2_ref)
    lgt_ref[...] = lgt

    @pl.when(t == nt - 1)
    def _():
        for hd in range(HEADS):
            s_out_ref[0, hd] = st_ref[hd].T
        cv_out_ref[0] = cbuf_ref[...]


def _const_spec(shape):
    nd = len(shape)
    return pl.BlockSpec(shape, lambda b, t, _nd=nd: (0,) * _nd, pipeline_mode=pl.Buffered(1))


def _mix(x, mod, h2_all, lgt_all, gmix, gffn, w_in, lbl, hg, cw, cb, w_oh, w_oc, w_o, wr_hi, wr_lo,
         w_sgu, w_sd):
    bsz, seq, _ = x.shape
    n_tok = h2_all.shape[1]
    tt = MIX_TILE
    nt = seq // tt
    consts = [gmix, gffn, w_in, lbl, hg, cw, cb, w_oh, w_oc, w_o, wr_hi, wr_lo, w_sgu, w_sd]
    return pl.pallas_call(
        _mix_kernel,
        grid=(bsz, nt),
        in_specs=[pl.BlockSpec((1, tt, D_MODEL), lambda b, t: (b, t, 0)),
                  pl.BlockSpec((1, 6, D_MODEL), lambda b, t: (b, 0, 0))]
                 + [_const_spec(a.shape) for a in consts]
                 + [pl.BlockSpec(memory_space=pl.ANY), pl.BlockSpec(memory_space=pl.ANY)],
        input_output_aliases={2 + len(consts): 1, 3 + len(consts): 2},
        out_specs=[pl.BlockSpec((1, tt, D_MODEL), lambda b, t: (b, t, 0)),
                   pl.BlockSpec((2, tt, PLANE_W), lambda b, t: (0, b * nt + t, 0)),
                   pl.BlockSpec((N_EXP, tt), lambda b, t: (0, b * nt + t)),
                   pl.BlockSpec((1, HEADS, DK, DK), lambda b, t: (b, 0, 0, 0)),
                   pl.BlockSpec((1, CONV_K - 1, CONV_W), lambda b, t: (b, 0, 0))],
        out_shape=[jax.ShapeDtypeStruct((bsz, seq, D_MODEL), F32),
                   jax.ShapeDtypeStruct((2, n_tok, PLANE_W), I32),
                   jax.ShapeDtypeStruct((N_EXP, n_tok), F32),
                   jax.ShapeDtypeStruct((bsz, HEADS, DK, DK), F32),
                   jax.ShapeDtypeStruct((bsz, CONV_K - 1, CONV_W), F32)],
        scratch_shapes=[pltpu.VMEM((tt, IN_W), F32),
                        pltpu.VMEM((HEADS, DK, DK), F32),
                        pltpu.VMEM((CONV_K - 1, CONV_W), F32),
                        pltpu.VMEM((tt, KEY_W), F32)],
        compiler_params=pltpu.CompilerParams(
            dimension_semantics=("arbitrary", "arbitrary"), vmem_limit_bytes=VMEM_LIMIT),
        name="mix",
    )(x, mod, *consts, h2_all, lgt_all)


def _smp1_kernel(x_ref, mod_ref, gmix_ref, w_in_ref, lbl_ref, cw_ref, cb_ref, cst_ref,
                 f_ref, k_ref, q_ref, v_ref, gate_ref, yb_ref, sga_ref, sgb_ref, cv_out_ref):
    x = x_ref[...]
    sh1, sc1 = mod_ref[:, 0:D_MODEL], mod_ref[:, D_MODEL:2 * D_MODEL]
    h = _rms(x) * gmix_ref[...] * (1.0 + sc1) + sh1
    hb = h.astype(BF16)

    def proj(c, w):
        return _dot(hb, w_in_ref[:, c:c + w])

    lb = _lower_bound(lbl_ref[...])
    f = lb + (1.0 - lb) * _sigmoid(proj(C_F, KEY_W))
    f_ref[...] = f
    k_ref[...] = 1.0 - f
    q_ref[...] = _silu(proj(C_Q, KEY_W))
    v_ref[...] = proj(C_I, KEY_W)
    gate_ref[...] = _silu(proj(C_G, KEY_W))
    u = proj(C_CC, CONV_W) * proj(C_VB, CONV_W)
    c0, c1 = cst_ref[:, 0:CONV_W], cst_ref[:, CONV_W:2 * CONV_W]
    conv = cw_ref[0:1] * c0 + cw_ref[1:2] * c1 + cw_ref[2:3] * u + cb_ref[...]
    yb_ref[...] = proj(C_BB, CONV_W) * conv
    cv_out_ref[:, 0:CONV_W] = c1
    cv_out_ref[:, CONV_W:2 * CONV_W] = u
    sga_ref[...] = _sigmoid(proj(C_MGA, D_MODEL))
    sgb_ref[...] = _sigmoid(proj(C_MGB, D_MODEL))


def _smp1(x, mod, gmix, w_in, lbl, cw, cb, cst):
    n = x.shape[0]
    kw = jax.ShapeDtypeStruct((n, KEY_W), F32)
    dm = jax.ShapeDtypeStruct((n, D_MODEL), F32)
    return pl.pallas_call(
        _smp1_kernel,
        out_shape=[kw, kw, kw, kw, kw, kw, dm, dm,
                   jax.ShapeDtypeStruct((n, (CONV_K - 1) * CONV_W), F32)],
        compiler_params=pltpu.CompilerParams(vmem_limit_bytes=VMEM_LIMIT),
        name="smp1",
    )(x, mod, gmix, w_in, lbl, cw, cb, cst)


def _smp2_kernel(f_ref, k_ref, q_ref, v_ref, s_ref, s_out_ref, o_ref):
    g = f_ref.shape[0]
    for i in range(g):
        for hd in range(HEADS):
            hs = slice(hd * DK, (hd + 1) * DK)

            def col(ref):
                return jnp.broadcast_to(ref[i:i + 1, hs], (DK, DK)).T

            vrow = v_ref[i:i + 1, hs]
            s_new = col(f_ref) * s_ref[i, hd] + col(k_ref) * vrow
            s_out_ref[i, hd] = s_new
            o_ref[i:i + 1, hs] = jnp.sum(col(q_ref) * s_new, axis=0, keepdims=True)


def _smp2(f, k, q, v, state):
    n = f.shape[0]
    g = SMP_GROUP
    row_spec = pl.BlockSpec((g, KEY_W), lambda i: (i, 0))
    st_spec = pl.BlockSpec((g, HEADS, DK, DK), lambda i: (i, 0, 0, 0))
    return pl.pallas_call(
        _smp2_kernel,
        grid=(n // g,),
        in_specs=[row_spec, row_spec, row_spec, row_spec, st_spec],
        out_specs=[st_spec, row_spec],
        out_shape=[jax.ShapeDtypeStruct(state.shape, F32), jax.ShapeDtypeStruct((n, KEY_W), F32)],
        compiler_params=pltpu.CompilerParams(dimension_semantics=("arbitrary",)),
        name="smp2",
    )(f, k, q, v, state)


def _smp3_kernel(x_ref, mod_ref, o_ref, gate_ref, yb_ref, sga_ref, sgb_ref, hg_ref, gffn_ref,
                 w_oh_ref, w_oc_ref, w_o_ref, wr_hi_ref, wr_lo_ref, w_sgu_ref, w_sd_ref,
                 h2_all_ref, lgt_all_ref, xmid_ref, h2_ref, lgt_ref):
    del h2_all_ref, lgt_all_ref
    parts = []
    for hd in range(HEADS):
        hs = slice(hd * DK, (hd + 1) * DK)
        parts.append(_rms(o_ref[:, hs]) * hg_ref[:, hs] * gate_ref[:, hs])
    ya = jnp.concatenate(parts, axis=1)
    mixed = (sga_ref[...] * _dot(ya.astype(BF16), w_oh_ref[...])
             + sgb_ref[...] * _dot(yb_ref[...].astype(BF16), w_oc_ref[...]))
    g1 = mod_ref[:, 2 * D_MODEL:3 * D_MODEL]
    x1 = x_ref[...] + g1 * _dot(mixed.astype(BF16), w_o_ref[...])
    mod_rows = tuple(mod_ref[:, j * D_MODEL:(j + 1) * D_MODEL] for j in (3, 4, 5))
    xmid, h2b, lgt = _ffn_pre(x1, mod_rows, gffn_ref[...], w_sgu_ref[...], w_sd_ref[...],
                              wr_hi_ref[...], wr_lo_ref[...])
    xmid_ref[...] = xmid
    _pack_rows(h2b, h2_ref)
    lgt_ref[...] = lgt


def _smp3(x, mod, o, gate, yb, sga, sgb, hg, gffn, w_oh, w_oc, w_o, wr_hi, wr_lo, w_sgu, w_sd,
          h2_all, lgt_all, n_prompt):
    n = x.shape[0]
    vmem_args = [x, mod, o, gate, yb, sga, sgb, hg, gffn, w_oh, w_oc, w_o, wr_hi, wr_lo, w_sgu, w_sd]
    blk = n_prompt // n

    def full(a):
        nd = a.ndim
        return pl.BlockSpec(a.shape, lambda i, _nd=nd: (0,) * _nd)

    return pl.pallas_call(
        _smp3_kernel,
        grid=(1,),
        in_specs=[full(a) for a in vmem_args]
                 + [pl.BlockSpec(memory_space=pl.ANY), pl.BlockSpec(memory_space=pl.ANY)],
        out_specs=[pl.BlockSpec((n, D_MODEL), lambda i: (0, 0)),
                   pl.BlockSpec((2, n, PLANE_W), lambda i: (0, blk, 0)),
                   pl.BlockSpec((N_EXP, n), lambda i: (0, blk))],
        out_shape=[jax.ShapeDtypeStruct((n, D_MODEL), F32),
                   jax.ShapeDtypeStruct(h2_all.shape, h2_all.dtype),
                   jax.ShapeDtypeStruct(lgt_all.shape, lgt_all.dtype)],
        input_output_aliases={len(vmem_args): 1, len(vmem_args) + 1: 2},
        compiler_params=pltpu.CompilerParams(
            dimension_semantics=("arbitrary",), vmem_limit_bytes=VMEM_LIMIT),
        name="smp3",
    )(*vmem_args, h2_all, lgt_all)


def _route_kernel(lgt_ref, bias_ref, idx_ref, w_ref, rank_ref, cnt_ref, carry_ref):
    i = pl.program_id(0)
    tr = lgt_ref.shape[1]
    neg = -jnp.inf

    @pl.when(i == 0)
    def _():
        carry_ref[...] = jnp.zeros_like(carry_ref)

    scores = _sigmoid(lgt_ref[...])
    sel = scores + bias_ref[...]
    j8 = lax.broadcasted_iota(I32, (GRP_SZ, tr), 0)
    groups = [sel[g * GRP_SZ:(g + 1) * GRP_SZ] for g in range(N_GRP)]
    gscore = []
    for grp in groups:
        m1 = jnp.max(grp, axis=0, keepdims=True)
        i1 = jnp.min(jnp.where(grp == m1, j8, GRP_SZ), axis=0, keepdims=True)
        m2 = jnp.max(jnp.where(j8 == i1, neg, grp), axis=0, keepdims=True)
        gscore.append(m1 + m2)
    kept = []
    for g in range(N_GRP):
        beaten = jnp.zeros((1, tr), I32)
        for o in range(N_GRP):
            if o < g:
                beaten = beaten + (gscore[o] >= gscore[g]).astype(I32)
            elif o > g:
                beaten = beaten + (gscore[o] > gscore[g]).astype(I32)
        kept.append(jnp.where(beaten < TOPK_GRP, groups[g], neg))
    masked = jnp.concatenate(kept, axis=0)
    ei = lax.broadcasted_iota(I32, masked.shape, 0)
    chosen = jnp.zeros(masked.shape, jnp.bool_)
    picks, weights = [], []
    for _ in range(TOP_K):
        m = jnp.max(masked, axis=0, keepdims=True)
        pick = jnp.min(jnp.where(masked == m, ei, N_EXP), axis=0, keepdims=True)
        hit = ei == pick
        weights.append(jnp.sum(jnp.where(hit, scores, 0.0), axis=0, keepdims=True))
        picks.append(pick)
        chosen = chosen | hit
        masked = jnp.where(hit, neg, masked)
    wsum = weights[0]
    for w in weights[1:]:
        wsum = wsum + w
    sel01 = chosen.astype(F32)
    r = lax.broadcasted_iota(I32, (tr, tr), 0)
    c = lax.broadcasted_iota(I32, (tr, tr), 1)
    before = (r < c).astype(BF16)
    cnt = _dot(sel01.astype(BF16), before) + carry_ref[...]
    for k in range(TOP_K):
        idx_ref[k:k + 1, :] = picks[k]
        w_ref[k:k + 1, :] = weights[k] / wsum * ROUTED_SCALE
        rank = jnp.sum(jnp.where(ei == picks[k], cnt, 0.0), axis=0, keepdims=True)
        rank_ref[k:k + 1, :] = rank.astype(I32)
    carry_ref[...] = carry_ref[...] + jnp.sum(sel01, axis=1, keepdims=True)
    cnt_ref[...] = jnp.broadcast_to(carry_ref[...], cnt_ref.shape).astype(I32)


def _route(lgt, bias):
    n = lgt.shape[1]
    tr = ROUTE_TILE
    slot = lambda dt: jax.ShapeDtypeStruct((TOP_K, n), dt)
    slot_spec = pl.BlockSpec((TOP_K, tr), lambda i: (0, i))
    return pl.pallas_call(
        _route_kernel,
        grid=(n // tr,),
        in_specs=[pl.BlockSpec((N_EXP, tr), lambda i: (0, i)),
                  pl.BlockSpec((N_EXP, 1), lambda i: (0, 0))],
        out_specs=[slot_spec, slot_spec, slot_spec, pl.BlockSpec((N_EXP, 128), lambda i: (0, 0))],
        out_shape=[slot(I32), slot(F32), slot(I32),
                   jax.ShapeDtypeStruct((N_EXP, 128), I32)],
        scratch_shapes=[pltpu.VMEM((N_EXP, 1), F32)],
        compiler_params=pltpu.CompilerParams(dimension_semantics=("arbitrary",)),
        name="route",
    )(lgt, bias)


def _dest_kernel(start_ref, idx_ref, rank_ref, dest_ref, *, n_rows):
    n_tok = idx_ref.shape[1]
    idx = idx_ref[...]
    acc = rank_ref[...]
    for e in range(N_EXP):
        acc = acc + jnp.where(idx == e, start_ref[e], 0)
    dest_ref[:, 0:n_tok] = acc
    dest_ref[:, n_tok:2 * n_tok] = acc + n_rows


def _dest(pad_start, idx, rank, n_rows):
    k, n_tok = idx.shape
    return pl.pallas_call(
        functools.partial(_dest_kernel, n_rows=n_rows),
        in_specs=[pl.BlockSpec(memory_space=pltpu.SMEM),
                  pl.BlockSpec(memory_space=pltpu.VMEM),
                  pl.BlockSpec(memory_space=pltpu.VMEM)],
        out_specs=pl.BlockSpec(memory_space=pltpu.VMEM),
        out_shape=jax.ShapeDtypeStruct((k, 2 * n_tok), I32),
        name="dest",
    )(pad_start, idx, rank)


def _sc_mesh():
    return plsc.VectorSubcoreMesh(core_axis_name="core", subcore_axis_name="subcore")


def _dispatch(rows, dest, n_out):
    n, width = rows.shape
    win = SC_WINDOW
    steps = n // win

    @pl.kernel(out_type=jax.ShapeDtypeStruct((n_out, width), rows.dtype), mesh=_sc_mesh(),
               scratch_types=[], name="dispatch")
    def run(x_hbm, *refs):
        i_hbms, o_hbm = refs[:TOP_K], refs[TOP_K]

        def body(x_vmem, *i_vmems):
            for i_vmem in i_vmems:
                pltpu.sync_copy(x_vmem, o_hbm.at[i_vmem.at[0]])

        pltpu.emit_pipeline(
            body,
            grid=(steps,),
            in_specs=[pl.BlockSpec((win, width), lambda i: (i, 0))]
                     + [pl.BlockSpec((1, win), lambda i, k=k: (0, k * steps + i)) for k in range(TOP_K)],
            out_specs=[],
            core_axis_name=("core", "subcore"),
            dimension_semantics=(pltpu.PARALLEL,),
        )(x_hbm, *i_hbms)

    dest_flat = dest.reshape(1, TOP_K * n)
    return run(rows, *([dest_flat] * TOP_K))


def _combine(rows, dest_flat):
    width = rows.shape[1]
    n = dest_flat.shape[0]
    win = SC_WINDOW

    @pl.kernel(out_type=jax.ShapeDtypeStruct((n, width), rows.dtype), mesh=_sc_mesh(),
               scratch_types=[], name="combine")
    def run(y_hbm, i_hbm, o_hbm):
        def body(i_vmem, o_vmem):
            pltpu.sync_copy(y_hbm.at[i_vmem.at[0]], o_vmem)

        pltpu.emit_pipeline(
            body,
            grid=(n // win,),
            in_specs=[pl.BlockSpec((1, win), lambda i: (0, i))],
            out_specs=[pl.BlockSpec((win, width), lambda i: (i, 0))],
            core_axis_name=("core", "subcore"),
            dimension_semantics=(pltpu.PARALLEL,),
        )(i_hbm, o_hbm)

    return run(rows, dest_flat.reshape(1, n))


def _gmm_kernel(blk_exp_ref, n_used_ref, x_ref, wg_ref, wu_ref, wd_ref, y_ref):
    j = pl.program_id(0)

    @pl.when(j < n_used_ref[0])
    def _():
        xc = _unpack_rows(x_ref[0], x_ref[1])
        wg = wg_ref[0].astype(BF16)
        wu = wu_ref[0].astype(BF16)
        g = sum(_dot(c, wg[i * PLANE_W:(i + 1) * PLANE_W]) for i, c in enumerate(xc))
        u = sum(_dot(c, wu[i * PLANE_W:(i + 1) * PLANE_W]) for i, c in enumerate(xc))
        act = (_silu(g) * u).astype(BF16)
        _pack_rows(_dot(act, wd_ref[0].astype(BF16)).astype(BF16), y_ref)


def _gmm(blk_exp, n_used, xs, w_gate, w_up, w_down):
    n_rows = xs.shape[1]
    bm = GMM_BM
    nb = n_rows // bm

    def row_map(j, be, nu):
        return (0, jnp.minimum(j, nu[0] - 1), 0)

    def w_map(j, be, nu):
        return (be[jnp.minimum(j, nu[0] - 1)], 0, 0)

    grid_spec = pltpu.PrefetchScalarGridSpec(
        num_scalar_prefetch=2,
        grid=(nb,),
        in_specs=[pl.BlockSpec((2, bm, PLANE_W), row_map),
                  pl.BlockSpec((1, D_MODEL, EXP_FF), w_map),
                  pl.BlockSpec((1, D_MODEL, EXP_FF), w_map),
                  pl.BlockSpec((1, EXP_FF, D_MODEL), w_map)],
        out_specs=pl.BlockSpec((2, bm, PLANE_W), row_map),
    )
    return pl.pallas_call(
        _gmm_kernel,
        grid_spec=grid_spec,
        out_shape=jax.ShapeDtypeStruct((2, n_rows, PLANE_W), I32),
        compiler_params=pltpu.CompilerParams(dimension_semantics=("arbitrary",)),
        name="gmm",
    )(blk_exp, n_used, xs, w_gate, w_up, w_down)


def _final_kernel(xmid_ref, g2_ref, z_ref, w_ref, gfin_ref, y_ref):
    accs = [jnp.zeros((xmid_ref.shape[0], PLANE_W), F32) for _ in range(4)]
    for k in range(TOP_K):
        wk = w_ref[:, k:k + 1]
        cols = _unpack_rows(z_ref[k, 0], z_ref[k, 1])
        accs = [a + wk * c.astype(F32) for a, c in zip(accs, cols)]
    acc = jnp.concatenate(accs, axis=1)
    y_ref[...] = _rms(xmid_ref[...] + g2_ref[0] * acc) * gfin_ref[...]


def _final(xmid, g2, z, w_t, gfin, tile, tok0):
    n_tok = xmid.shape[0]
    t0 = tok0 // tile
    tiles_per_g2 = n_tok // tile // g2.shape[0]
    return pl.pallas_call(
        _final_kernel,
        grid=(n_tok // tile,),
        in_specs=[pl.BlockSpec((tile, D_MODEL), lambda i: (i, 0)),
                  pl.BlockSpec((1, g2.shape[1], D_MODEL), lambda i: (i // tiles_per_g2, 0, 0)),
                  pl.BlockSpec((TOP_K, 2, tile, PLANE_W), lambda i: (0, 0, t0 + i, 0)),
                  pl.BlockSpec((tile, TOP_K), lambda i: (t0 + i, 0)),
                  pl.BlockSpec((1, D_MODEL), lambda i: (0, 0))],
        out_specs=pl.BlockSpec((tile, D_MODEL), lambda i: (i, 0)),
        out_shape=jax.ShapeDtypeStruct((n_tok, D_MODEL), F32),
        compiler_params=pltpu.CompilerParams(dimension_semantics=("arbitrary",)),
        name="final",
    )(xmid, g2, z, w_t, gfin)


def kernel(x_prompt, x_sample, state_hgrn, state_conv, c_prompt, c_sample, w_ada, b_ada, norm_mix_g, norm_ffn_g, w_in, lb_logits, hgrn_norm_g, conv_w, conv_b, w_out_hgrn, w_out_conv, w_o, w_router, router_bias, w_exp_gate, w_exp_up, w_exp_down, w_sh_gate, w_sh_up, w_sh_down, final_norm_g):
    assert w_ada.shape[0] == 1 and lb_logits.shape[0] == 2
    bsz, seq, _ = x_prompt.shape
    n_smp = x_sample.shape[0]
    n_prompt = bsz * seq
    n_tok = n_prompt + n_smp

    w_in_b = w_in[0].astype(BF16)
    w_oh_b = w_out_hgrn[0].astype(BF16)
    w_oc_b = w_out_conv[0].astype(BF16)
    w_o_b = w_o[0].astype(BF16)
    wr_t = w_router[0].T
    wr_hi = wr_t.astype(BF16)
    wr_lo = (wr_t - wr_hi.astype(F32)).astype(BF16)
    w_sgu = jnp.concatenate([w_sh_gate[0], w_sh_up[0]], axis=1).astype(BF16)
    w_sd = w_sh_down[0].astype(BF16)
    gmix = norm_mix_g[0].reshape(1, D_MODEL)
    gffn = norm_ffn_g[0].reshape(1, D_MODEL)
    hg = hgrn_norm_g[0].reshape(1, KEY_W)
    cw = conv_w[0]
    cb = conv_b[0].reshape(1, CONV_W)
    gfin = final_norm_g.reshape(1, D_MODEL)

    mod = _ada(jnp.concatenate([c_prompt, c_sample], axis=0), w_ada[0], b_ada[0])
    mod_p, mod_s = mod[:bsz], mod[bsz:]

    xmid_p, h2_all, lgt_all, s_p, cv_p = _mix(
        x_prompt, mod_p.reshape(bsz, 6, D_MODEL),
        jnp.zeros((2, n_tok, PLANE_W), I32), jnp.zeros((N_EXP, n_tok), F32),
        gmix, gffn, w_in_b, lb_logits, hg, cw, cb, w_oh_b, w_oc_b, w_o_b, wr_hi, wr_lo, w_sgu, w_sd)

    xs2 = x_sample.reshape(n_smp, D_MODEL)
    f, kk, q, v, gate, yb, sga, sgb, cv_s = _smp1(
        xs2, mod_s, gmix, w_in_b, lb_logits, cw, cb, state_conv[0].reshape(n_smp, (CONV_K - 1) * CONV_W))
    s_s, o_s = _smp2(f, kk, q, v, state_hgrn[0])
    xmid_s, h2_all, lgt_all = _smp3(xs2, mod_s, o_s, gate, yb, sga, sgb, hg, gffn,
                                    w_oh_b, w_oc_b, w_o_b, wr_hi, wr_lo, w_sgu, w_sd,
                                    h2_all, lgt_all, n_prompt)

    idx, w_tok, rank, cnt = _route(lgt_all, router_bias[0].reshape(N_EXP, 1))

    bm = GMM_BM
    n_blocks = (n_tok * TOP_K + N_EXP * (bm - 1)) // bm
    n_rows = n_blocks * bm
    counts = cnt[:, 0]
    padded = (counts + bm - 1) // bm * bm
    pad_end = jnp.cumsum(padded)
    pad_start = pad_end - padded
    blk_row0 = jnp.arange(n_blocks, dtype=I32) * bm
    blk_exp = jnp.minimum(jnp.sum((pad_end[None, :] <= blk_row0[:, None]).astype(I32), axis=1), N_EXP - 1)
    n_used = (pad_end[-1:] // bm).astype(I32)
    dest = _dest(pad_start.astype(I32), idx, rank, n_rows)

    xs = _dispatch(h2_all.reshape(2 * n_tok, PLANE_W), dest, 2 * n_rows).reshape(2, n_rows, PLANE_W)
    ys = _gmm(blk_exp, n_used, xs, w_exp_gate[0], w_exp_up[0], w_exp_down[0])
    z = _combine(ys.reshape(2 * n_rows, PLANE_W), dest.reshape(-1)).reshape(TOP_K, 2, n_tok, PLANE_W)

    w_t = w_tok.T
    y_p = _final(xmid_p.reshape(n_prompt, D_MODEL), mod_p[:, 5 * D_MODEL:].reshape(bsz, 1, D_MODEL),
                 z, w_t, gfin, FINAL_TILE, 0)
    y_s = _final(xmid_s, mod_s[:, 5 * D_MODEL:].reshape(1, n_smp, D_MODEL), z, w_t, gfin, n_smp, n_prompt)

    return (y_p.reshape(bsz, seq, D_MODEL), y_s.reshape(n_smp, 1, D_MODEL),
            s_p[None], cv_p[None], s_s[None], cv_s.reshape(1, n_smp, CONV_K - 1, CONV_W))
```

```python
import functools

import jax
import jax.numpy as jnp
from jax import lax
from jax.experimental import pallas as pl
from jax.experimental.pallas import tpu as pltpu
from jax.experimental.pallas import tpu_sc as plsc

F32 = jnp.float32
BF16 = jnp.bfloat16
I32 = jnp.int32

D_MODEL = 1024
HALF_D = D_MODEL // 2
HEADS = 4
DK = 128
KEY_W = HEADS * DK
CONV_W = 512
CONV_K = 3
IN_W = 2 * KEY_W + 2 * KEY_W + 3 * CONV_W + 2 * D_MODEL
N_EXP = 64
TOP_K = 8
N_GRP = 8
GRP_SZ = N_EXP // N_GRP
TOPK_GRP = 4
EXP_FF = 256
SH_FF = 256
ROUTED_SCALE = 2.5
EPS = 1e-6

C_Q, C_F, C_I, C_G = 0, 512, 1024, 1536
C_BB, C_CC, C_VB = 2048, 2560, 3072
C_MGA, C_MGB = 3584, 4608

MIX_TILE = 512
SUB = 128
HALF = SUB // 2
ROUTE_TILE = 384
GMM_BM = 512
FINAL_TILE = 256
SMP_GROUP = 8
SC_WINDOW = 128
PLANE_W = HALF_D // 2
VMEM_LIMIT = 56 * 1024 * 1024


def _dot(a, b):
    return jnp.dot(a, b, preferred_element_type=F32)


def _dot_nt(a, b):
    return lax.dot_general(a, b, (((1,), (1,)), ((), ())), preferred_element_type=F32)


def _dot_tn(a, b):
    return lax.dot_general(a, b, (((0,), (0,)), ((), ())), preferred_element_type=F32)


def _sigmoid(x):
    return 0.5 * jnp.tanh(0.5 * x) + 0.5


def _silu(x):
    h = 0.5 * x
    return h * jnp.tanh(h) + h


def _rms(x):
    return x * lax.rsqrt(jnp.mean(x * x, axis=-1, keepdims=True) + EPS)


def _lower_bound(lbl):
    a, b = lbl[0:1], lbl[1:2]
    m = jnp.maximum(a, b)
    ea, eb = jnp.exp(a - m), jnp.exp(b - m)
    return ea / (ea + eb)


def _split3(x):
    hi = x.astype(BF16)
    r1 = x - hi.astype(F32)
    mid = r1.astype(BF16)
    lo = (r1 - mid.astype(F32)).astype(BF16)
    return hi, mid, lo


def _pack_rows(xb, out_ref):
    bits = lax.bitcast_convert_type(xb.astype(F32), I32)
    lo = lax.shift_right_logical(bits[:, :HALF_D], 16)
    hi = bits[:, HALF_D:] & jnp.int32(-65536)
    words = lo | hi
    out_ref[0] = words[:, :PLANE_W]
    out_ref[1] = words[:, PLANE_W:]


def _unpack_rows(p0, p1):
    def halves(w):
        lo = lax.bitcast_convert_type(lax.shift_left(w, 16), F32)
        hi = lax.bitcast_convert_type(w & jnp.int32(-65536), F32)
        return lo.astype(BF16), hi.astype(BF16)

    c0, c2 = halves(p0)
    c1, c3 = halves(p1)
    return c0, c1, c2, c3


def _ada_kernel(c_ref, w_ref, b_ref, o_ref):
    a = _silu(c_ref[...]).astype(BF16)
    o_ref[...] = _dot(a, w_ref[...].astype(BF16)) + b_ref[...]


def _ada(c_all, w_ada, b_ada):
    n = c_all.shape[0]
    blk = 1024
    return pl.pallas_call(
        _ada_kernel,
        grid=(6 * D_MODEL // blk,),
        in_specs=[pl.BlockSpec((n, D_MODEL), lambda j: (0, 0)),
                  pl.BlockSpec((D_MODEL, blk), lambda j: (0, j)),
                  pl.BlockSpec((1, blk), lambda j: (0, j))],
        out_specs=pl.BlockSpec((n, blk), lambda j: (0, j)),
        out_shape=jax.ShapeDtypeStruct((n, 6 * D_MODEL), F32),
        name="ada",
    )(c_all, w_ada, b_ada.reshape(1, -1))


def _ffn_pre(x1, mod_rows, gffn, w_sgu, w_sd, wr_hi, wr_lo):
    sh2, sc2, g2 = mod_rows
    h2 = _rms(x1) * gffn * (1.0 + sc2) + sh2
    h2b = h2.astype(BF16)
    gu = _dot(h2b, w_sgu)
    act = _silu(gu[:, :SH_FF]) * gu[:, SH_FF:]
    xmid = x1 + g2 * _dot(act.astype(BF16), w_sd)
    h2lo = (h2 - h2b.astype(F32)).astype(BF16)
    lgt = _dot_nt(wr_hi, h2b) + _dot_nt(wr_hi, h2lo) + _dot_nt(wr_lo, h2b)
    return xmid, h2b, lgt


def _mix_kernel(x_ref, mod_ref, gmix_ref, gffn_ref, w_in_ref, lbl_ref, hg_ref, cw_ref, cb_ref,
                w_oh_ref, w_oc_ref, w_o_ref, wr_hi_ref, wr_lo_ref, w_sgu_ref, w_sd_ref,
                h2_all_ref, lgt_all_ref,
                xmid_ref, h2_ref, lgt_ref, s_out_ref, cv_out_ref,
                proj_ref, st_ref, cbuf_ref, ya_ref):
    del h2_all_ref, lgt_all_ref
    t = pl.program_id(1)
    nt = pl.num_programs(1)
    tt = x_ref.shape[1]

    @pl.when(t == 0)
    def _():
        st_ref[...] = jnp.zeros_like(st_ref)
        cbuf_ref[...] = jnp.zeros_like(cbuf_ref)

    x = x_ref[0]
    mod = mod_ref[0]
    sh1, sc1, g1 = mod[0:1], mod[1:2], mod[2:3]
    h = _rms(x) * gmix_ref[...] * (1.0 + sc1) + sh1
    hb = h.astype(BF16)
    for c in range(0, IN_W, 512):
        proj_ref[:, c:c + 512] = _dot(hb, w_in_ref[:, c:c + 512])

    lb = _lower_bound(lbl_ref[...])
    row = lax.broadcasted_iota(I32, (SUB, SUB), 0)
    col = lax.broadcasted_iota(I32, (SUB, SUB), 1)
    tri = (col <= row).astype(BF16)
    mask_d = (col <= row) & ((row >= HALF) == (col >= HALF))
    top_half = lax.broadcasted_iota(I32, (SUB, DK), 0) < HALF

    for s in range(tt // SUB):
        r0 = s * SUB
        f = lb + (1.0 - lb) * _sigmoid(proj_ref[r0:r0 + SUB, C_F:C_F + KEY_W])
        kk = 1.0 - f
        hi, mid, lo = _split3(jnp.log(f))
        bc = _dot(tri, hi) + _dot(tri, mid) + _dot(tri, lo)
        for hd in range(HEADS):
            hs = slice(hd * DK, (hd + 1) * DK)
            bh = bc[:, hs]
            b31, b63 = bh[HALF // 2 - 1:HALF // 2], bh[HALF - 1:HALF]
            b95, b127 = bh[HALF + HALF // 2 - 1:HALF + HALF // 2], bh[SUB - 1:SUB]
            arg = bh - jnp.where(top_half, b31, b95)
            e_pos, e_neg = jnp.exp(arg), jnp.exp(-arg)
            q = _silu(proj_ref[r0:r0 + SUB, C_Q + hd * DK:C_Q + (hd + 1) * DK])
            v = proj_ref[r0:r0 + SUB, C_I + hd * DK:C_I + (hd + 1) * DK]
            qd = q * e_pos
            kd = kk[:, hs] * e_neg
            q_in = qd * jnp.where(top_half, jnp.exp(b31), jnp.exp(b95))
            k_end = kd * jnp.where(top_half, jnp.exp(b127 - b31), jnp.exp(b127 - b95))
            qa = jnp.where(top_half, 0.0, qd * jnp.exp(b95 - b63))
            ka = jnp.where(top_half, kd * jnp.exp(b63 - b31), 0.0)
            att = jnp.where(mask_d, _dot_nt(qd.astype(BF16), kd.astype(BF16)), 0.0)
            att = att + _dot_nt(qa.astype(BF16), ka.astype(BF16))
            vb = v.astype(BF16)
            st = st_ref[hd]
            o = _dot(att.astype(BF16), vb) + _dot_nt(q_in.astype(BF16), st.astype(BF16))
            st_ref[hd] = st * jnp.exp(b127) + _dot_tn(vb, k_end.astype(BF16))
            gate = _silu(proj_ref[r0:r0 + SUB, C_G + hd * DK:C_G + (hd + 1) * DK])
            ya_ref[r0:r0 + SUB, hs] = _rms(o) * hg_ref[:, hs] * gate

    u = proj_ref[:, C_CC:C_CC + CONV_W] * proj_ref[:, C_VB:C_VB + CONV_W]
    rows = lax.broadcasted_iota(I32, (tt, CONV_W), 0)
    c0, c1 = cbuf_ref[0:1], cbuf_ref[1:2]
    u1 = jnp.where(rows == 0, c1, pltpu.roll(u, 1, axis=0))
    u2 = jnp.where(rows == 0, c0, jnp.where(rows == 1, c1, pltpu.roll(u, 2, axis=0)))
    conv = cw_ref[0:1] * u2 + cw_ref[1:2] * u1 + cw_ref[2:3] * u + cb_ref[...]
    yb = proj_ref[:, C_BB:C_BB + CONV_W] * conv
    cbuf_ref[...] = u[tt - 2:tt]

    mixed = (_sigmoid(proj_ref[:, C_MGA:C_MGA + D_MODEL]) * _dot(ya_ref[...].astype(BF16), w_oh_ref[...])
             + _sigmoid(proj_ref[:, C_MGB:C_MGB + D_MODEL]) * _dot(yb.astype(BF16), w_oc_ref[...]))
    x1 = x + g1 * _dot(mixed.astype(BF16), w_o_ref[...])

    xmid, h2b, lgt = _ffn_pre(x1, (mod[3:4], mod[4:5], mod[5:6]), gffn_ref[...],
                              w_sgu_ref[...], w_sd_ref[...], wr_hi_ref[...], wr_lo_ref[...])
    xmid_ref[0] = xmid
    _pack_rows(h2b, h2_ref)
    lgt_ref[...] = lgt

    @pl.when(t == nt - 1)
    def _():
        for hd in range(HEADS):
            s_out_ref[0, hd] = st_ref[hd].T
        cv_out_ref[0] = cbuf_ref[...]


def _const_spec(shape):
    nd = len(shape)
    return pl.BlockSpec(shape, lambda b, t, _nd=nd: (0,) * _nd, pipeline_mode=pl.Buffered(1))


def _mix(x, mod, h2_all, lgt_all, gmix, gffn, w_in, lbl, hg, cw, cb, w_oh, w_oc, w_o, wr_hi, wr_lo,
         w_sgu, w_sd):
    bsz, seq, _ = x.shape
    n_tok = h2_all.shape[1]
    tt = MIX_TILE
    nt = seq // tt
    consts = [gmix, gffn, w_in, lbl, hg, cw, cb, w_oh, w_oc, w_o, wr_hi, wr_lo, w_sgu, w_sd]
    return pl.pallas_call(
        _mix_kernel,
        grid=(bsz, nt),
        in_specs=[pl.BlockSpec((1, tt, D_MODEL), lambda b, t: (b, t, 0)),
                  pl.BlockSpec((1, 6, D_MODEL), lambda b, t: (b, 0, 0))]
                 + [_const_spec(a.shape) for a in consts]
                 + [pl.BlockSpec(memory_space=pl.ANY), pl.BlockSpec(memory_space=pl.ANY)],
        input_output_aliases={2 + len(consts): 1, 3 + len(consts): 2},
        out_specs=[pl.BlockSpec((1, tt, D_MODEL), lambda b, t: (b, t, 0)),
                   pl.BlockSpec((2, tt, PLANE_W), lambda b, t: (0, b * nt + t, 0)),
                   pl.BlockSpec((N_EXP, tt), lambda b, t: (0, b * nt + t)),
                   pl.BlockSpec((1, HEADS, DK, DK), lambda b, t: (b, 0, 0, 0)),
                   pl.BlockSpec((1, CONV_K - 1, CONV_W), lambda b, t: (b, 0, 0))],
        out_shape=[jax.ShapeDtypeStruct((bsz, seq, D_MODEL), F32),
                   jax.ShapeDtypeStruct((2, n_tok, PLANE_W), I32),
                   jax.ShapeDtypeStruct((N_EXP, n_tok), F32),
                   jax.ShapeDtypeStruct((bsz, HEADS, DK, DK), F32),
                   jax.ShapeDtypeStruct((bsz, CONV_K - 1, CONV_W), F32)],
        scratch_shapes=[pltpu.VMEM((tt, IN_W), F32),
                        pltpu.VMEM((HEADS, DK, DK), F32),
                        pltpu.VMEM((CONV_K - 1, CONV_W), F32),
                        pltpu.VMEM((tt, KEY_W), F32)],
        compiler_params=pltpu.CompilerParams(
            dimension_semantics=("arbitrary", "arbitrary"), vmem_limit_bytes=VMEM_LIMIT),
        name="mix",
    )(x, mod, *consts, h2_all, lgt_all)


def _smp1_kernel(x_ref, mod_ref, gmix_ref, w_in_ref, lbl_ref, cw_ref, cb_ref, cst_ref,
                 f_ref, k_ref, q_ref, v_ref, gate_ref, yb_ref, sga_ref, sgb_ref, cv_out_ref):
    x = x_ref[...]
    sh1, sc1 = mod_ref[:, 0:D_MODEL], mod_ref[:, D_MODEL:2 * D_MODEL]
    h = _rms(x) * gmix_ref[...] * (1.0 + sc1) + sh1
    hb = h.astype(BF16)

    def proj(c, w):
        return _dot(hb, w_in_ref[:, c:c + w])

    lb = _lower_bound(lbl_ref[...])
    f = lb + (1.0 - lb) * _sigmoid(proj(C_F, KEY_W))
    f_ref[...] = f
    k_ref[...] = 1.0 - f
    q_ref[...] = _silu(proj(C_Q, KEY_W))
    v_ref[...] = proj(C_I, KEY_W)
    gate_ref[...] = _silu(proj(C_G, KEY_W))
    u = proj(C_CC, CONV_W) * proj(C_VB, CONV_W)
    c0, c1 = cst_ref[:, 0:CONV_W], cst_ref[:, CONV_W:2 * CONV_W]
    conv = cw_ref[0:1] * c0 + cw_ref[1:2] * c1 + cw_ref[2:3] * u + cb_ref[...]
    yb_ref[...] = proj(C_BB, CONV_W) * conv
    cv_out_ref[:, 0:CONV_W] = c1
    cv_out_ref[:, CONV_W:2 * CONV_W] = u
    sga_ref[...] = _sigmoid(proj(C_MGA, D_MODEL))
    sgb_ref[...] = _sigmoid(proj(C_MGB, D_MODEL))


def _smp1(x, mod, gmix, w_in, lbl, cw, cb, cst):
    n = x.shape[0]
    kw = jax.ShapeDtypeStruct((n, KEY_W), F32)
    dm = jax.ShapeDtypeStruct((n, D_MODEL), F32)
    return pl.pallas_call(
        _smp1_kernel,
        out_shape=[kw, kw, kw, kw, kw, kw, dm, dm,
                   jax.ShapeDtypeStruct((n, (CONV_K - 1) * CONV_W), F32)],
        compiler_params=pltpu.CompilerParams(vmem_limit_bytes=VMEM_LIMIT),
        name="smp1",
    )(x, mod, gmix, w_in, lbl, cw, cb, cst)


def _smp2_kernel(f_ref, k_ref, q_ref, v_ref, s_ref, s_out_ref, o_ref):
    g = f_ref.shape[0]
    for i in range(g):
        for hd in range(HEADS):
            hs = slice(hd * DK, (hd + 1) * DK)

            def col(ref):
                return jnp.broadcast_to(ref[i:i + 1, hs], (DK, DK)).T

            vrow = v_ref[i:i + 1, hs]
            s_new = col(f_ref) * s_ref[i, hd] + col(k_ref) * vrow
            s_out_ref[i, hd] = s_new
            o_ref[i:i + 1, hs] = jnp.sum(col(q_ref) * s_new, axis=0, keepdims=True)


def _smp2(f, k, q, v, state):
    n = f.shape[0]
    g = SMP_GROUP
    row_spec = pl.BlockSpec((g, KEY_W), lambda i: (i, 0))
    st_spec = pl.BlockSpec((g, HEADS, DK, DK), lambda i: (i, 0, 0, 0))
    return pl.pallas_call(
        _smp2_kernel,
        grid=(n // g,),
        in_specs=[row_spec, row_spec, row_spec, row_spec, st_spec],
        out_specs=[st_spec, row_spec],
        out_shape=[jax.ShapeDtypeStruct(state.shape, F32), jax.ShapeDtypeStruct((n, KEY_W), F32)],
        compiler_params=pltpu.CompilerParams(dimension_semantics=("arbitrary",)),
        name="smp2",
    )(f, k, q, v, state)


def _smp3_kernel(x_ref, mod_ref, o_ref, gate_ref, yb_ref, sga_ref, sgb_ref, hg_ref, gffn_ref,
                 w_oh_ref, w_oc_ref, w_o_ref, wr_hi_ref, wr_lo_ref, w_sgu_ref, w_sd_ref,
                 h2_all_ref, lgt_all_ref, xmid_ref, h2_ref, lgt_ref):
    del h2_all_ref, lgt_all_ref
    parts = []
    for hd in range(HEADS):
        hs = slice(hd * DK, (hd + 1) * DK)
        parts.append(_rms(o_ref[:, hs]) * hg_ref[:, hs] * gate_ref[:, hs])
    ya = jnp.concatenate(parts, axis=1)
    mixed = (sga_ref[...] * _dot(ya.astype(BF16), w_oh_ref[...])
             + sgb_ref[...] * _dot(yb_ref[...].astype(BF16), w_oc_ref[...]))
    g1 = mod_ref[:, 2 * D_MODEL:3 * D_MODEL]
    x1 = x_ref[...] + g1 * _dot(mixed.astype(BF16), w_o_ref[...])
    mod_rows = tuple(mod_ref[:, j * D_MODEL:(j + 1) * D_MODEL] for j in (3, 4, 5))
    xmid, h2b, lgt = _ffn_pre(x1, mod_rows, gffn_ref[...], w_sgu_ref[...], w_sd_ref[...],
                              wr_hi_ref[...], wr_lo_ref[...])
    xmid_ref[...] = xmid
    _pack_rows(h2b, h2_ref)
    lgt_ref[...] = lgt


def _smp3(x, mod, o, gate, yb, sga, sgb, hg, gffn, w_oh, w_oc, w_o, wr_hi, wr_lo, w_sgu, w_sd,
          h2_all, lgt_all, n_prompt):
    n = x.shape[0]
    vmem_args = [x, mod, o, gate, yb, sga, sgb, hg, gffn, w_oh, w_oc, w_o, wr_hi, wr_lo, w_sgu, w_sd]
    blk = n_prompt // n

    def full(a):
        nd = a.ndim
        return pl.BlockSpec(a.shape, lambda i, _nd=nd: (0,) * _nd)

    return pl.pallas_call(
        _smp3_kernel,
        grid=(1,),
        in_specs=[full(a) for a in vmem_args]
                 + [pl.BlockSpec(memory_space=pl.ANY), pl.BlockSpec(memory_space=pl.ANY)],
        out_specs=[pl.BlockSpec((n, D_MODEL), lambda i: (0, 0)),
                   pl.BlockSpec((2, n, PLANE_W), lambda i: (0, blk, 0)),
                   pl.BlockSpec((N_EXP, n), lambda i: (0, blk))],
        out_shape=[jax.ShapeDtypeStruct((n, D_MODEL), F32),
                   jax.ShapeDtypeStruct(h2_all.shape, h2_all.dtype),
                   jax.ShapeDtypeStruct(lgt_all.shape, lgt_all.dtype)],
        input_output_aliases={len(vmem_args): 1, len(vmem_args) + 1: 2},
        compiler_params=pltpu.CompilerParams(
            dimension_semantics=("arbitrary",), vmem_limit_bytes=VMEM_LIMIT),
        name="smp3",
    )(*vmem_args, h2_all, lgt_all)


def _route_kernel(lgt_ref, bias_ref, idx_ref, w_ref, rank_ref, cnt_ref, carry_ref):
    i = pl.program_id(0)
    tr = lgt_ref.shape[1]
    neg = -jnp.inf

    @pl.when(i == 0)
    def _():
        carry_ref[...] = jnp.zeros_like(carry_ref)

    scores = _sigmoid(lgt_ref[...])
    sel = scores + bias_ref[...]
    j8 = lax.broadcasted_iota(I32, (GRP_SZ, tr), 0)
    groups = [sel[g * GRP_SZ:(g + 1) * GRP_SZ] for g in range(N_GRP)]
    gscore = []
    for grp in groups:
        m1 = jnp.max(grp, axis=0, keepdims=True)
        i1 = jnp.min(jnp.where(grp == m1, j8, GRP_SZ), axis=0, keepdims=True)
        m2 = jnp.max(jnp.where(j8 == i1, neg, grp), axis=0, keepdims=True)
        gscore.append(m1 + m2)
    kept = []
    for g in range(N_GRP):
        beaten = jnp.zeros((1, tr), I32)
        for o in range(N_GRP):
            if o < g:
                beaten = beaten + (gscore[o] >= gscore[g]).astype(I32)
            elif o > g:
                beaten = beaten + (gscore[o] > gscore[g]).astype(I32)
        kept.append(jnp.where(beaten < TOPK_GRP, groups[g], neg))
    masked = jnp.concatenate(kept, axis=0)
    ei = lax.broadcasted_iota(I32, masked.shape, 0)
    chosen = jnp.zeros(masked.shape, jnp.bool_)
    picks, weights = [], []
    for _ in range(TOP_K):
        m = jnp.max(masked, axis=0, keepdims=True)
        pick = jnp.min(jnp.where(masked == m, ei, N_EXP), axis=0, keepdims=True)
        hit = ei == pick
        weights.append(jnp.sum(jnp.where(hit, scores, 0.0), axis=0, keepdims=True))
        picks.append(pick)
        chosen = chosen | hit
        masked = jnp.where(hit, neg, masked)
    wsum = weights[0]
    for w in weights[1:]:
        wsum = wsum + w
    sel01 = chosen.astype(F32)
    r = lax.broadcasted_iota(I32, (tr, tr), 0)
    c = lax.broadcasted_iota(I32, (tr, tr), 1)
    before = (r < c).astype(BF16)
    cnt = _dot(sel01.astype(BF16), before) + carry_ref[...]
    for k in range(TOP_K):
        idx_ref[k:k + 1, :] = picks[k]
        w_ref[k:k + 1, :] = weights[k] / wsum * ROUTED_SCALE
        rank = jnp.sum(jnp.where(ei == picks[k], cnt, 0.0), axis=0, keepdims=True)
        rank_ref[k:k + 1, :] = rank.astype(I32)
    carry_ref[...] = carry_ref[...] + jnp.sum(sel01, axis=1, keepdims=True)
    cnt_ref[...] = jnp.broadcast_to(carry_ref[...], cnt_ref.shape).astype(I32)


def _route(lgt, bias):
    n = lgt.shape[1]
    tr = ROUTE_TILE
    slot = lambda dt: jax.ShapeDtypeStruct((TOP_K, n), dt)
    slot_spec = pl.BlockSpec((TOP_K, tr), lambda i: (0, i))
    return pl.pallas_call(
        _route_kernel,
        grid=(n // tr,),
        in_specs=[pl.BlockSpec((N_EXP, tr), lambda i: (0, i)),
                  pl.BlockSpec((N_EXP, 1), lambda i: (0, 0))],
        out_specs=[slot_spec, slot_spec, slot_spec, pl.BlockSpec((N_EXP, 128), lambda i: (0, 0))],
        out_shape=[slot(I32), slot(F32), slot(I32),
                   jax.ShapeDtypeStruct((N_EXP, 128), I32)],
        scratch_shapes=[pltpu.VMEM((N_EXP, 1), F32)],
        compiler_params=pltpu.CompilerParams(dimension_semantics=("arbitrary",)),
        name="route",
    )(lgt, bias)


def _dest_kernel(start_ref, idx_ref, rank_ref, dest_ref, *, n_rows):
    n_tok = idx_ref.shape[1]
    idx = idx_ref[...]
    acc = rank_ref[...]
    for e in range(N_EXP):
        acc = acc + jnp.where(idx == e, start_ref[e], 0)
    dest_ref[:, 0:n_tok] = acc
    dest_ref[:, n_tok:2 * n_tok] = acc + n_rows


def _dest(pad_start, idx, rank, n_rows):
    k, n_tok = idx.shape
    return pl.pallas_call(
        functools.partial(_dest_kernel, n_rows=n_rows),
        in_specs=[pl.BlockSpec(memory_space=pltpu.SMEM),
                  pl.BlockSpec(memory_space=pltpu.VMEM),
                  pl.BlockSpec(memory_space=pltpu.VMEM)],
        out_specs=pl.BlockSpec(memory_space=pltpu.VMEM),
        out_shape=jax.ShapeDtypeStruct((k, 2 * n_tok), I32),
        name="dest",
    )(pad_start, idx, rank)


def _sc_mesh():
    return plsc.VectorSubcoreMesh(core_axis_name="core", subcore_axis_name="subcore")


def _dispatch(rows, dest, n_out):
    n, width = rows.shape
    win = SC_WINDOW
    steps = n // win

    @pl.kernel(out_type=jax.ShapeDtypeStruct((n_out, width), rows.dtype), mesh=_sc_mesh(),
               scratch_types=[], name="dispatch")
    def run(x_hbm, *refs):
        i_hbms, o_hbm = refs[:TOP_K], refs[TOP_K]

        def body(x_vmem, *i_vmems):
            for i_vmem in i_vmems:
                pltpu.sync_copy(x_vmem, o_hbm.at[i_vmem.at[0]])

        pltpu.emit_pipeline(
            body,
            grid=(steps,),
            in_specs=[pl.BlockSpec((win, width), lambda i: (i, 0))]
                     + [pl.BlockSpec((1, win), lambda i, k=k: (0, k * steps + i)) for k in range(TOP_K)],
            out_specs=[],
            core_axis_name=("core", "subcore"),
            dimension_semantics=(pltpu.PARALLEL,),
        )(x_hbm, *i_hbms)

    dest_flat = dest.reshape(1, TOP_K * n)
    return run(rows, *([dest_flat] * TOP_K))


def _combine(rows, dest_flat):
    width = rows.shape[1]
    n = dest_flat.shape[0]
    win = SC_WINDOW

    @pl.kernel(out_type=jax.ShapeDtypeStruct((n, width), rows.dtype), mesh=_sc_mesh(),
               scratch_types=[], name="combine")
    def run(y_hbm, i_hbm, o_hbm):
        def body(i_vmem, o_vmem):
            pltpu.sync_copy(y_hbm.at[i_vmem.at[0]], o_vmem)

        pltpu.emit_pipeline(
            body,
            grid=(n // win,),
            in_specs=[pl.BlockSpec((1, win), lambda i: (0, i))],
            out_specs=[pl.BlockSpec((win, width), lambda i: (i, 0))],
            core_axis_name=("core", "subcore"),
            dimension_semantics=(pltpu.PARALLEL,),
        )(i_hbm, o_hbm)

    return run(rows, dest_flat.reshape(1, n))


def _gmm_kernel(blk_exp_ref, n_used_ref, x_ref, wg_ref, wu_ref, wd_ref, y_ref, wgu_b, wd_b):
    j = pl.program_id(0)
    live = j < n_used_ref[0]
    new_expert = (j == 0) | (blk_exp_ref[j] != blk_exp_ref[jnp.maximum(j - 1, 0)])

    @pl.when(live & new_expert)
    def _():
        wgu_b[:, 0:EXP_FF] = wg_ref[0].astype(BF16)
        wgu_b[:, EXP_FF:2 * EXP_FF] = wu_ref[0].astype(BF16)
        wd_b[...] = wd_ref[0].astype(BF16)

    @pl.when(live)
    def _():
        xc = _unpack_rows(x_ref[0], x_ref[1])
        gu = sum(_dot(c, wgu_b[i * PLANE_W:(i + 1) * PLANE_W, :]) for i, c in enumerate(xc))
        act = (_silu(gu[:, :EXP_FF]) * gu[:, EXP_FF:]).astype(BF16)
        _pack_rows(_dot(act, wd_b[...]).astype(BF16), y_ref)


def _gmm(blk_exp, n_used, xs, w_gate, w_up, w_down):
    n_rows = xs.shape[1]
    bm = GMM_BM
    nb = n_rows // bm

    def row_map(j, be, nu):
        return (0, jnp.minimum(j, nu[0] - 1), 0)

    def w_map(j, be, nu):
        return (be[jnp.minimum(j, nu[0] - 1)], 0, 0)

    grid_spec = pltpu.PrefetchScalarGridSpec(
        num_scalar_prefetch=2,
        grid=(nb,),
        in_specs=[pl.BlockSpec((2, bm, PLANE_W), row_map),
                  pl.BlockSpec((1, D_MODEL, EXP_FF), w_map),
                  pl.BlockSpec((1, D_MODEL, EXP_FF), w_map),
                  pl.BlockSpec((1, EXP_FF, D_MODEL), w_map)],
        out_specs=pl.BlockSpec((2, bm, PLANE_W), row_map),
        scratch_shapes=[pltpu.VMEM((D_MODEL, 2 * EXP_FF), BF16), pltpu.VMEM((EXP_FF, D_MODEL), BF16)],
    )
    return pl.pallas_call(
        _gmm_kernel,
        grid_spec=grid_spec,
        out_shape=jax.ShapeDtypeStruct((2, n_rows, PLANE_W), I32),
        compiler_params=pltpu.CompilerParams(dimension_semantics=("arbitrary",)),
        name="gmm",
    )(blk_exp, n_used, xs, w_gate, w_up, w_down)


def _final_kernel(xmid_ref, g2_ref, z_ref, w_ref, gfin_ref, y_ref):
    accs = [jnp.zeros((xmid_ref.shape[0], PLANE_W), F32) for _ in range(4)]
    for k in range(TOP_K):
        wk = w_ref[:, k:k + 1]
        cols = _unpack_rows(z_ref[k, 0], z_ref[k, 1])
        accs = [a + wk * c.astype(F32) for a, c in zip(accs, cols)]
    acc = jnp.concatenate(accs, axis=1)
    y_ref[...] = _rms(xmid_ref[...] + g2_ref[0] * acc) * gfin_ref[...]


def _final(xmid, g2, z, w_t, gfin, tile, tok0):
    n_tok = xmid.shape[0]
    t0 = tok0 // tile
    tiles_per_g2 = n_tok // tile // g2.shape[0]
    return pl.pallas_call(
        _final_kernel,
        grid=(n_tok // tile,),
        in_specs=[pl.BlockSpec((tile, D_MODEL), lambda i: (i, 0)),
                  pl.BlockSpec((1, g2.shape[1], D_MODEL), lambda i: (i // tiles_per_g2, 0, 0)),
                  pl.BlockSpec((TOP_K, 2, tile, PLANE_W), lambda i: (0, 0, t0 + i, 0)),
                  pl.BlockSpec((tile, TOP_K), lambda i: (t0 + i, 0)),
                  pl.BlockSpec((1, D_MODEL), lambda i: (0, 0))],
        out_specs=pl.BlockSpec((tile, D_MODEL), lambda i: (i, 0)),
        out_shape=jax.ShapeDtypeStruct((n_tok, D_MODEL), F32),
        compiler_params=pltpu.CompilerParams(dimension_semantics=("arbitrary",)),
        name="final",
    )(xmid, g2, z, w_t, gfin)


def kernel(x_prompt, x_sample, state_hgrn, state_conv, c_prompt, c_sample, w_ada, b_ada, norm_mix_g, norm_ffn_g, w_in, lb_logits, hgrn_norm_g, conv_w, conv_b, w_out_hgrn, w_out_conv, w_o, w_router, router_bias, w_exp_gate, w_exp_up, w_exp_down, w_sh_gate, w_sh_up, w_sh_down, final_norm_g):
    assert w_ada.shape[0] == 1 and lb_logits.shape[0] == 2
    bsz, seq, _ = x_prompt.shape
    n_smp = x_sample.shape[0]
    n_prompt = bsz * seq
    n_tok = n_prompt + n_smp

    w_in_b = w_in[0].astype(BF16)
    w_oh_b = w_out_hgrn[0].astype(BF16)
    w_oc_b = w_out_conv[0].astype(BF16)
    w_o_b = w_o[0].astype(BF16)
    wr_t = w_router[0].T
    wr_hi = wr_t.astype(BF16)
    wr_lo = (wr_t - wr_hi.astype(F32)).astype(BF16)
    w_sgu = jnp.concatenate([w_sh_gate[0], w_sh_up[0]], axis=1).astype(BF16)
    w_sd = w_sh_down[0].astype(BF16)
    gmix = norm_mix_g[0].reshape(1, D_MODEL)
    gffn = norm_ffn_g[0].reshape(1, D_MODEL)
    hg = hgrn_norm_g[0].reshape(1, KEY_W)
    cw = conv_w[0]
    cb = conv_b[0].reshape(1, CONV_W)
    gfin = final_norm_g.reshape(1, D_MODEL)

    mod = _ada(jnp.concatenate([c_prompt, c_sample], axis=0), w_ada[0], b_ada[0])
    mod_p, mod_s = mod[:bsz], mod[bsz:]

    xmid_p, h2_all, lgt_all, s_p, cv_p = _mix(
        x_prompt, mod_p.reshape(bsz, 6, D_MODEL),
        jnp.zeros((2, n_tok, PLANE_W), I32), jnp.zeros((N_EXP, n_tok), F32),
        gmix, gffn, w_in_b, lb_logits, hg, cw, cb, w_oh_b, w_oc_b, w_o_b, wr_hi, wr_lo, w_sgu, w_sd)

    xs2 = x_sample.reshape(n_smp, D_MODEL)
    f, kk, q, v, gate, yb, sga, sgb, cv_s = _smp1(
        xs2, mod_s, gmix, w_in_b, lb_logits, cw, cb, state_conv[0].reshape(n_smp, (CONV_K - 1) * CONV_W))
    s_s, o_s = _smp2(f, kk, q, v, state_hgrn[0])
    xmid_s, h2_all, lgt_all = _smp3(xs2, mod_s, o_s, gate, yb, sga, sgb, hg, gffn,
                                    w_oh_b, w_oc_b, w_o_b, wr_hi, wr_lo, w_sgu, w_sd,
                                    h2_all, lgt_all, n_prompt)

    idx, w_tok, rank, cnt = _route(lgt_all, router_bias[0].reshape(N_EXP, 1))

    bm = GMM_BM
    n_blocks = (n_tok * TOP_K + N_EXP * (bm - 1)) // bm
    n_rows = n_blocks * bm
    counts = cnt[:, 0]
    padded = (counts + bm - 1) // bm * bm
    pad_end = jnp.cumsum(padded)
    pad_start = pad_end - padded
    blk_row0 = jnp.arange(n_blocks, dtype=I32) * bm
    blk_exp = jnp.minimum(jnp.sum((pad_end[None, :] <= blk_row0[:, None]).astype(I32), axis=1), N_EXP - 1)
    n_used = (pad_end[-1:] // bm).astype(I32)
    dest = _dest(pad_start.astype(I32), idx, rank, n_rows)

    xs = _dispatch(h2_all.reshape(2 * n_tok, PLANE_W), dest, 2 * n_rows).reshape(2, n_rows, PLANE_W)
    ys = _gmm(blk_exp, n_used, xs, w_exp_gate[0], w_exp_up[0], w_exp_down[0])
    z = _combine(ys.reshape(2 * n_rows, PLANE_W), dest.reshape(-1)).reshape(TOP_K, 2, n_tok, PLANE_W)

    w_t = w_tok.T
    y_p = _final(xmid_p.reshape(n_prompt, D_MODEL), mod_p[:, 5 * D_MODEL:].reshape(bsz, 1, D_MODEL),
                 z, w_t, gfin, FINAL_TILE, 0)
    y_s = _final(xmid_s, mod_s[:, 5 * D_MODEL:].reshape(1, n_smp, D_MODEL), z, w_t, gfin, n_smp, n_prompt)

    return (y_p.reshape(bsz, seq, D_MODEL), y_s.reshape(n_smp, 1, D_MODEL),
            s_p[None], cv_p[None], s_s[None], cv_s.reshape(1, n_smp, CONV_K - 1, CONV_W))
```

```python
import functools

import jax
import jax.numpy as jnp
from jax import lax
from jax.experimental import pallas as pl
from jax.experimental.pallas import tpu as pltpu
from jax.experimental.pallas import tpu_sc as plsc

F32 = jnp.float32
BF16 = jnp.bfloat16
I32 = jnp.int32

D_MODEL = 1024
HALF_D = D_MODEL // 2
HEADS = 4
DK = 128
KEY_W = HEADS * DK
CONV_W = 512
CONV_K = 3
IN_W = 2 * KEY_W + 2 * KEY_W + 3 * CONV_W + 2 * D_MODEL
N_EXP = 64
TOP_K = 8
N_GRP = 8
GRP_SZ = N_EXP // N_GRP
TOPK_GRP = 4
EXP_FF = 256
SH_FF = 256
ROUTED_SCALE = 2.5
EPS = 1e-6

C_Q, C_F, C_I, C_G = 0, 512, 1024, 1536
C_BB, C_CC, C_VB = 2048, 2560, 3072
C_MGA, C_MGB = 3584, 4608

MIX_TILE = 512
SUB = 128
HALF = SUB // 2
ROUTE_TILE = 384
GMM_BM = 512
FINAL_TILE = 256
FINAL_CHUNKS = 4
SMP_GROUP = 8
SC_WINDOW = 128
PLANE_W = HALF_D // 2
VMEM_LIMIT = 56 * 1024 * 1024


def _dot(a, b):
    return jnp.dot(a, b, preferred_element_type=F32)


def _dot_nt(a, b):
    return lax.dot_general(a, b, (((1,), (1,)), ((), ())), preferred_element_type=F32)


def _dot_tn(a, b):
    return lax.dot_general(a, b, (((0,), (0,)), ((), ())), preferred_element_type=F32)


def _sigmoid(x):
    return 0.5 * jnp.tanh(0.5 * x) + 0.5


def _silu(x):
    h = 0.5 * x
    return h * jnp.tanh(h) + h


def _rms(x):
    return x * lax.rsqrt(jnp.mean(x * x, axis=-1, keepdims=True) + EPS)


def _lower_bound(lbl):
    a, b = lbl[0:1], lbl[1:2]
    m = jnp.maximum(a, b)
    ea, eb = jnp.exp(a - m), jnp.exp(b - m)
    return ea / (ea + eb)


def _split3(x):
    hi = x.astype(BF16)
    r1 = x - hi.astype(F32)
    mid = r1.astype(BF16)
    lo = (r1 - mid.astype(F32)).astype(BF16)
    return hi, mid, lo


def _pack_rows(xb, out_ref):
    bits = lax.bitcast_convert_type(xb.astype(F32), I32)
    lo = lax.shift_right_logical(bits[:, :HALF_D], 16)
    hi = bits[:, HALF_D:] & jnp.int32(-65536)
    words = lo | hi
    out_ref[0] = words[:, :PLANE_W]
    out_ref[1] = words[:, PLANE_W:]


def _unpack_rows(p0, p1):
    def halves(w):
        lo = lax.bitcast_convert_type(lax.shift_left(w, 16), F32)
        hi = lax.bitcast_convert_type(w & jnp.int32(-65536), F32)
        return lo.astype(BF16), hi.astype(BF16)

    c0, c2 = halves(p0)
    c1, c3 = halves(p1)
    return c0, c1, c2, c3


def _ada_kernel(c_ref, w_ref, b_ref, o_ref):
    a = _silu(c_ref[...]).astype(BF16)
    o_ref[...] = _dot(a, w_ref[...].astype(BF16)) + b_ref[...]


def _ada(c_all, w_ada, b_ada):
    n = c_all.shape[0]
    blk = 1024
    return pl.pallas_call(
        _ada_kernel,
        grid=(6 * D_MODEL // blk,),
        in_specs=[pl.BlockSpec((n, D_MODEL), lambda j: (0, 0)),
                  pl.BlockSpec((D_MODEL, blk), lambda j: (0, j)),
                  pl.BlockSpec((1, blk), lambda j: (0, j))],
        out_specs=pl.BlockSpec((n, blk), lambda j: (0, j)),
        out_shape=jax.ShapeDtypeStruct((n, 6 * D_MODEL), F32),
        name="ada",
    )(c_all, w_ada, b_ada.reshape(1, -1))


def _ffn_pre(x1, mod_rows, gffn, w_sgu, w_sd, wr_hi, wr_lo):
    sh2, sc2, g2 = mod_rows
    h2 = _rms(x1) * gffn * (1.0 + sc2) + sh2
    h2b = h2.astype(BF16)
    gu = _dot(h2b, w_sgu)
    act = _silu(gu[:, :SH_FF]) * gu[:, SH_FF:]
    xmid = x1 + g2 * _dot(act.astype(BF16), w_sd)
    h2lo = (h2 - h2b.astype(F32)).astype(BF16)
    lgt = _dot_nt(wr_hi, h2b) + _dot_nt(wr_hi, h2lo) + _dot_nt(wr_lo, h2b)
    return xmid, h2b, lgt


def _mix_kernel(x_ref, mod_ref, gmix_ref, gffn_ref, w_in_ref, lbl_ref, hg_ref, cw_ref, cb_ref,
                w_oh_ref, w_oc_ref, w_o_ref, wr_hi_ref, wr_lo_ref, w_sgu_ref, w_sd_ref,
                h2_all_ref, lgt_all_ref,
                xmid_ref, h2_ref, lgt_ref, s_out_ref, cv_out_ref,
                proj_ref, st_ref, cbuf_ref, ya_ref):
    del h2_all_ref, lgt_all_ref
    t = pl.program_id(1)
    nt = pl.num_programs(1)
    tt = x_ref.shape[1]

    @pl.when(t == 0)
    def _():
        st_ref[...] = jnp.zeros_like(st_ref)
        cbuf_ref[...] = jnp.zeros_like(cbuf_ref)

    x = x_ref[0]
    mod = mod_ref[0]
    sh1, sc1, g1 = mod[0:1], mod[1:2], mod[2:3]
    h = _rms(x) * gmix_ref[...] * (1.0 + sc1) + sh1
    hb = h.astype(BF16)
    for c in range(0, IN_W, 512):
        proj_ref[:, c:c + 512] = _dot(hb, w_in_ref[:, c:c + 512])

    lb = _lower_bound(lbl_ref[...])
    row = lax.broadcasted_iota(I32, (SUB, SUB), 0)
    col = lax.broadcasted_iota(I32, (SUB, SUB), 1)
    tri = (col <= row).astype(BF16)
    mask_d = (col <= row) & ((row >= HALF) == (col >= HALF))
    top_half = lax.broadcasted_iota(I32, (SUB, DK), 0) < HALF

    for s in range(tt // SUB):
        r0 = s * SUB
        f = lb + (1.0 - lb) * _sigmoid(proj_ref[r0:r0 + SUB, C_F:C_F + KEY_W])
        kk = 1.0 - f
        hi, mid, lo = _split3(jnp.log(f))
        bc = _dot(tri, hi) + _dot(tri, mid) + _dot(tri, lo)
        for hd in range(HEADS):
            hs = slice(hd * DK, (hd + 1) * DK)
            bh = bc[:, hs]
            b31, b63 = bh[HALF // 2 - 1:HALF // 2], bh[HALF - 1:HALF]
            b95, b127 = bh[HALF + HALF // 2 - 1:HALF + HALF // 2], bh[SUB - 1:SUB]
            arg = bh - jnp.where(top_half, b31, b95)
            e_pos, e_neg = jnp.exp(arg), jnp.exp(-arg)
            q = _silu(proj_ref[r0:r0 + SUB, C_Q + hd * DK:C_Q + (hd + 1) * DK])
            v = proj_ref[r0:r0 + SUB, C_I + hd * DK:C_I + (hd + 1) * DK]
            qd = q * e_pos
            kd = kk[:, hs] * e_neg
            q_in = qd * jnp.where(top_half, jnp.exp(b31), jnp.exp(b95))
            k_end = kd * jnp.where(top_half, jnp.exp(b127 - b31), jnp.exp(b127 - b95))
            qa = jnp.where(top_half, 0.0, qd * jnp.exp(b95 - b63))
            ka = jnp.where(top_half, kd * jnp.exp(b63 - b31), 0.0)
            att = jnp.where(mask_d, _dot_nt(qd.astype(BF16), kd.astype(BF16)), 0.0)
            att = att + _dot_nt(qa.astype(BF16), ka.astype(BF16))
            vb = v.astype(BF16)
            st = st_ref[hd]
            o = _dot(att.astype(BF16), vb) + _dot_nt(q_in.astype(BF16), st.astype(BF16))
            st_ref[hd] = st * jnp.exp(b127) + _dot_tn(vb, k_end.astype(BF16))
            gate = _silu(proj_ref[r0:r0 + SUB, C_G + hd * DK:C_G + (hd + 1) * DK])
            ya_ref[r0:r0 + SUB, hs] = _rms(o) * hg_ref[:, hs] * gate

    u = proj_ref[:, C_CC:C_CC + CONV_W] * proj_ref[:, C_VB:C_VB + CONV_W]
    rows = lax.broadcasted_iota(I32, (tt, CONV_W), 0)
    c0, c1 = cbuf_ref[0:1], cbuf_ref[1:2]
    u1 = jnp.where(rows == 0, c1, pltpu.roll(u, 1, axis=0))
    u2 = jnp.where(rows == 0, c0, jnp.where(rows == 1, c1, pltpu.roll(u, 2, axis=0)))
    conv = cw_ref[0:1] * u2 + cw_ref[1:2] * u1 + cw_ref[2:3] * u + cb_ref[...]
    yb = proj_ref[:, C_BB:C_BB + CONV_W] * conv
    cbuf_ref[...] = u[tt - 2:tt]

    mixed = (_sigmoid(proj_ref[:, C_MGA:C_MGA + D_MODEL]) * _dot(ya_ref[...].astype(BF16), w_oh_ref[...])
             + _sigmoid(proj_ref[:, C_MGB:C_MGB + D_MODEL]) * _dot(yb.astype(BF16), w_oc_ref[...]))
    x1 = x + g1 * _dot(mixed.astype(BF16), w_o_ref[...])

    xmid, h2b, lgt = _ffn_pre(x1, (mod[3:4], mod[4:5], mod[5:6]), gffn_ref[...],
                              w_sgu_ref[...], w_sd_ref[...], wr_hi_ref[...], wr_lo_ref[...])
    xmid_ref[0] = xmid
    _pack_rows(h2b, h2_ref)
    lgt_ref[...] = lgt

    @pl.when(t == nt - 1)
    def _():
        for hd in range(HEADS):
            s_out_ref[0, hd] = st_ref[hd].T
        cv_out_ref[0] = cbuf_ref[...]


def _const_spec(shape):
    nd = len(shape)
    return pl.BlockSpec(shape, lambda b, t, _nd=nd: (0,) * _nd, pipeline_mode=pl.Buffered(1))


def _mix(x, mod, h2_all, lgt_all, gmix, gffn, w_in, lbl, hg, cw, cb, w_oh, w_oc, w_o, wr_hi, wr_lo,
         w_sgu, w_sd):
    bsz, seq, _ = x.shape
    n_tok = h2_all.shape[1]
    tt = MIX_TILE
    nt = seq // tt
    consts = [gmix, gffn, w_in, lbl, hg, cw, cb, w_oh, w_oc, w_o, wr_hi, wr_lo, w_sgu, w_sd]
    return pl.pallas_call(
        _mix_kernel,
        grid=(bsz, nt),
        in_specs=[pl.BlockSpec((1, tt, D_MODEL), lambda b, t: (b, t, 0)),
                  pl.BlockSpec((1, 6, D_MODEL), lambda b, t: (b, 0, 0))]
                 + [_const_spec(a.shape) for a in consts]
                 + [pl.BlockSpec(memory_space=pl.ANY), pl.BlockSpec(memory_space=pl.ANY)],
        input_output_aliases={2 + len(consts): 1, 3 + len(consts): 2},
        out_specs=[pl.BlockSpec((1, tt, D_MODEL), lambda b, t: (b, t, 0)),
                   pl.BlockSpec((2, tt, PLANE_W), lambda b, t: (0, b * nt + t, 0)),
                   pl.BlockSpec((N_EXP, tt), lambda b, t: (0, b * nt + t)),
                   pl.BlockSpec((1, HEADS, DK, DK), lambda b, t: (b, 0, 0, 0)),
                   pl.BlockSpec((1, CONV_K - 1, CONV_W), lambda b, t: (b, 0, 0))],
        out_shape=[jax.ShapeDtypeStruct((bsz, seq, D_MODEL), F32),
                   jax.ShapeDtypeStruct((2, n_tok, PLANE_W), I32),
                   jax.ShapeDtypeStruct((N_EXP, n_tok), F32),
                   jax.ShapeDtypeStruct((bsz, HEADS, DK, DK), F32),
                   jax.ShapeDtypeStruct((bsz, CONV_K - 1, CONV_W), F32)],
        scratch_shapes=[pltpu.VMEM((tt, IN_W), F32),
                        pltpu.VMEM((HEADS, DK, DK), F32),
                        pltpu.VMEM((CONV_K - 1, CONV_W), F32),
                        pltpu.VMEM((tt, KEY_W), F32)],
        compiler_params=pltpu.CompilerParams(
            dimension_semantics=("arbitrary", "arbitrary"), vmem_limit_bytes=VMEM_LIMIT),
        name="mix",
    )(x, mod, *consts, h2_all, lgt_all)


def _smp1_kernel(x_ref, mod_ref, gmix_ref, w_in_ref, lbl_ref, cw_ref, cb_ref, cst_ref,
                 f_ref, k_ref, q_ref, v_ref, gate_ref, yb_ref, sga_ref, sgb_ref, cv_out_ref):
    x = x_ref[...]
    sh1, sc1 = mod_ref[:, 0:D_MODEL], mod_ref[:, D_MODEL:2 * D_MODEL]
    h = _rms(x) * gmix_ref[...] * (1.0 + sc1) + sh1
    hb = h.astype(BF16)

    def proj(c, w):
        return _dot(hb, w_in_ref[:, c:c + w])

    lb = _lower_bound(lbl_ref[...])
    f = lb + (1.0 - lb) * _sigmoid(proj(C_F, KEY_W))
    f_ref[...] = f
    k_ref[...] = 1.0 - f
    q_ref[...] = _silu(proj(C_Q, KEY_W))
    v_ref[...] = proj(C_I, KEY_W)
    gate_ref[...] = _silu(proj(C_G, KEY_W))
    u = proj(C_CC, CONV_W) * proj(C_VB, CONV_W)
    c0, c1 = cst_ref[:, 0:CONV_W], cst_ref[:, CONV_W:2 * CONV_W]
    conv = cw_ref[0:1] * c0 + cw_ref[1:2] * c1 + cw_ref[2:3] * u + cb_ref[...]
    yb_ref[...] = proj(C_BB, CONV_W) * conv
    cv_out_ref[:, 0:CONV_W] = c1
    cv_out_ref[:, CONV_W:2 * CONV_W] = u
    sga_ref[...] = _sigmoid(proj(C_MGA, D_MODEL))
    sgb_ref[...] = _sigmoid(proj(C_MGB, D_MODEL))


def _smp1(x, mod, gmix, w_in, lbl, cw, cb, cst):
    n = x.shape[0]
    kw = jax.ShapeDtypeStruct((n, KEY_W), F32)
    dm = jax.ShapeDtypeStruct((n, D_MODEL), F32)
    return pl.pallas_call(
        _smp1_kernel,
        out_shape=[kw, kw, kw, kw, kw, kw, dm, dm,
                   jax.ShapeDtypeStruct((n, (CONV_K - 1) * CONV_W), F32)],
        compiler_params=pltpu.CompilerParams(vmem_limit_bytes=VMEM_LIMIT),
        name="smp1",
    )(x, mod, gmix, w_in, lbl, cw, cb, cst)


def _smp2_kernel(f_ref, k_ref, q_ref, v_ref, s_ref, s_out_ref, o_ref):
    g = f_ref.shape[0]
    for i in range(g):
        for hd in range(HEADS):
            hs = slice(hd * DK, (hd + 1) * DK)

            def col(ref):
                return jnp.broadcast_to(ref[i:i + 1, hs], (DK, DK)).T

            vrow = v_ref[i:i + 1, hs]
            s_new = col(f_ref) * s_ref[i, hd] + col(k_ref) * vrow
            s_out_ref[i, hd] = s_new
            o_ref[i:i + 1, hs] = jnp.sum(col(q_ref) * s_new, axis=0, keepdims=True)


def _smp2(f, k, q, v, state):
    n = f.shape[0]
    g = SMP_GROUP
    row_spec = pl.BlockSpec((g, KEY_W), lambda i: (i, 0))
    st_spec = pl.BlockSpec((g, HEADS, DK, DK), lambda i: (i, 0, 0, 0))
    return pl.pallas_call(
        _smp2_kernel,
        grid=(n // g,),
        in_specs=[row_spec, row_spec, row_spec, row_spec, st_spec],
        out_specs=[st_spec, row_spec],
        out_shape=[jax.ShapeDtypeStruct(state.shape, F32), jax.ShapeDtypeStruct((n, KEY_W), F32)],
        compiler_params=pltpu.CompilerParams(dimension_semantics=("arbitrary",)),
        name="smp2",
    )(f, k, q, v, state)


def _smp3_kernel(x_ref, mod_ref, o_ref, gate_ref, yb_ref, sga_ref, sgb_ref, hg_ref, gffn_ref,
                 w_oh_ref, w_oc_ref, w_o_ref, wr_hi_ref, wr_lo_ref, w_sgu_ref, w_sd_ref,
                 h2_all_ref, lgt_all_ref, xmid_ref, h2_ref, lgt_ref):
    del h2_all_ref, lgt_all_ref
    parts = []
    for hd in range(HEADS):
        hs = slice(hd * DK, (hd + 1) * DK)
        parts.append(_rms(o_ref[:, hs]) * hg_ref[:, hs] * gate_ref[:, hs])
    ya = jnp.concatenate(parts, axis=1)
    mixed = (sga_ref[...] * _dot(ya.astype(BF16), w_oh_ref[...])
             + sgb_ref[...] * _dot(yb_ref[...].astype(BF16), w_oc_ref[...]))
    g1 = mod_ref[:, 2 * D_MODEL:3 * D_MODEL]
    x1 = x_ref[...] + g1 * _dot(mixed.astype(BF16), w_o_ref[...])
    mod_rows = tuple(mod_ref[:, j * D_MODEL:(j + 1) * D_MODEL] for j in (3, 4, 5))
    xmid, h2b, lgt = _ffn_pre(x1, mod_rows, gffn_ref[...], w_sgu_ref[...], w_sd_ref[...],
                              wr_hi_ref[...], wr_lo_ref[...])
    xmid_ref[...] = xmid
    _pack_rows(h2b, h2_ref)
    lgt_ref[...] = lgt


def _smp3(x, mod, o, gate, yb, sga, sgb, hg, gffn, w_oh, w_oc, w_o, wr_hi, wr_lo, w_sgu, w_sd,
          h2_all, lgt_all, n_prompt):
    n = x.shape[0]
    vmem_args = [x, mod, o, gate, yb, sga, sgb, hg, gffn, w_oh, w_oc, w_o, wr_hi, wr_lo, w_sgu, w_sd]
    blk = n_prompt // n

    def full(a):
        nd = a.ndim
        return pl.BlockSpec(a.shape, lambda i, _nd=nd: (0,) * _nd)

    return pl.pallas_call(
        _smp3_kernel,
        grid=(1,),
        in_specs=[full(a) for a in vmem_args]
                 + [pl.BlockSpec(memory_space=pl.ANY), pl.BlockSpec(memory_space=pl.ANY)],
        out_specs=[pl.BlockSpec((n, D_MODEL), lambda i: (0, 0)),
                   pl.BlockSpec((2, n, PLANE_W), lambda i: (0, blk, 0)),
                   pl.BlockSpec((N_EXP, n), lambda i: (0, blk))],
        out_shape=[jax.ShapeDtypeStruct((n, D_MODEL), F32),
                   jax.ShapeDtypeStruct(h2_all.shape, h2_all.dtype),
                   jax.ShapeDtypeStruct(lgt_all.shape, lgt_all.dtype)],
        input_output_aliases={len(vmem_args): 1, len(vmem_args) + 1: 2},
        compiler_params=pltpu.CompilerParams(
            dimension_semantics=("arbitrary",), vmem_limit_bytes=VMEM_LIMIT),
        name="smp3",
    )(*vmem_args, h2_all, lgt_all)


def _route_kernel(lgt_ref, bias_ref, idx_ref, w_ref, rank_ref, cnt_ref, carry_ref):
    i = pl.program_id(0)
    tr = lgt_ref.shape[1]
    neg = -jnp.inf

    @pl.when(i == 0)
    def _():
        carry_ref[...] = jnp.zeros_like(carry_ref)

    scores = _sigmoid(lgt_ref[...])
    sel = scores + bias_ref[...]
    j8 = lax.broadcasted_iota(I32, (GRP_SZ, tr), 0)
    groups = [sel[g * GRP_SZ:(g + 1) * GRP_SZ] for g in range(N_GRP)]
    gscore = []
    for grp in groups:
        m1 = jnp.max(grp, axis=0, keepdims=True)
        i1 = jnp.min(jnp.where(grp == m1, j8, GRP_SZ), axis=0, keepdims=True)
        m2 = jnp.max(jnp.where(j8 == i1, neg, grp), axis=0, keepdims=True)
        gscore.append(m1 + m2)
    kept = []
    for g in range(N_GRP):
        beaten = jnp.zeros((1, tr), I32)
        for o in range(N_GRP):
            if o < g:
                beaten = beaten + (gscore[o] >= gscore[g]).astype(I32)
            elif o > g:
                beaten = beaten + (gscore[o] > gscore[g]).astype(I32)
        kept.append(jnp.where(beaten < TOPK_GRP, groups[g], neg))
    masked = jnp.concatenate(kept, axis=0)
    ei = lax.broadcasted_iota(I32, masked.shape, 0)
    chosen = jnp.zeros(masked.shape, jnp.bool_)
    picks, weights = [], []
    for _ in range(TOP_K):
        m = jnp.max(masked, axis=0, keepdims=True)
        pick = jnp.min(jnp.where(masked == m, ei, N_EXP), axis=0, keepdims=True)
        hit = ei == pick
        weights.append(jnp.sum(jnp.where(hit, scores, 0.0), axis=0, keepdims=True))
        picks.append(pick)
        chosen = chosen | hit
        masked = jnp.where(hit, neg, masked)
    wsum = weights[0]
    for w in weights[1:]:
        wsum = wsum + w
    sel01 = chosen.astype(F32)
    r = lax.broadcasted_iota(I32, (tr, tr), 0)
    c = lax.broadcasted_iota(I32, (tr, tr), 1)
    before = (r < c).astype(BF16)
    cnt = _dot(sel01.astype(BF16), before) + carry_ref[...]
    for k in range(TOP_K):
        idx_ref[k:k + 1, :] = picks[k]
        w_ref[k:k + 1, :] = weights[k] / wsum * ROUTED_SCALE
        rank = jnp.sum(jnp.where(ei == picks[k], cnt, 0.0), axis=0, keepdims=True)
        rank_ref[k:k + 1, :] = rank.astype(I32)
    carry_ref[...] = carry_ref[...] + jnp.sum(sel01, axis=1, keepdims=True)
    cnt_ref[...] = jnp.broadcast_to(carry_ref[...], cnt_ref.shape).astype(I32)


def _route(lgt, bias):
    n = lgt.shape[1]
    tr = ROUTE_TILE
    slot = lambda dt: jax.ShapeDtypeStruct((TOP_K, n), dt)
    slot_spec = pl.BlockSpec((TOP_K, tr), lambda i: (0, i))
    return pl.pallas_call(
        _route_kernel,
        grid=(n // tr,),
        in_specs=[pl.BlockSpec((N_EXP, tr), lambda i: (0, i)),
                  pl.BlockSpec((N_EXP, 1), lambda i: (0, 0))],
        out_specs=[slot_spec, slot_spec, slot_spec, pl.BlockSpec((N_EXP, 128), lambda i: (0, 0))],
        out_shape=[slot(I32), slot(F32), slot(I32),
                   jax.ShapeDtypeStruct((N_EXP, 128), I32)],
        scratch_shapes=[pltpu.VMEM((N_EXP, 1), F32)],
        compiler_params=pltpu.CompilerParams(dimension_semantics=("arbitrary",)),
        name="route",
    )(lgt, bias)


def _dest_kernel(start_ref, idx_ref, rank_ref, dest_ref, *, n_rows):
    n_tok = idx_ref.shape[1]
    idx = idx_ref[...]
    acc = rank_ref[...]
    for e in range(N_EXP):
        acc = acc + jnp.where(idx == e, start_ref[e], 0)
    dest_ref[:, 0:n_tok] = acc
    dest_ref[:, n_tok:2 * n_tok] = acc + n_rows


def _dest(pad_start, idx, rank, n_rows):
    k, n_tok = idx.shape
    return pl.pallas_call(
        functools.partial(_dest_kernel, n_rows=n_rows),
        in_specs=[pl.BlockSpec(memory_space=pltpu.SMEM),
                  pl.BlockSpec(memory_space=pltpu.VMEM),
                  pl.BlockSpec(memory_space=pltpu.VMEM)],
        out_specs=pl.BlockSpec(memory_space=pltpu.VMEM),
        out_shape=jax.ShapeDtypeStruct((k, 2 * n_tok), I32),
        name="dest",
    )(pad_start, idx, rank)


def _sc_mesh():
    return plsc.VectorSubcoreMesh(core_axis_name="core", subcore_axis_name="subcore")


def _dispatch(rows, dest, n_out):
    n, width = rows.shape
    win = SC_WINDOW
    steps = n // win

    @pl.kernel(out_type=jax.ShapeDtypeStruct((n_out, width), rows.dtype), mesh=_sc_mesh(),
               scratch_types=[], name="dispatch")
    def run(x_hbm, *refs):
        i_hbms, o_hbm = refs[:TOP_K], refs[TOP_K]

        def body(x_vmem, *i_vmems):
            for i_vmem in i_vmems:
                pltpu.sync_copy(x_vmem, o_hbm.at[i_vmem.at[0]])

        pltpu.emit_pipeline(
            body,
            grid=(steps,),
            in_specs=[pl.BlockSpec((win, width), lambda i: (i, 0))]
                     + [pl.BlockSpec((1, win), lambda i, k=k: (0, k * steps + i)) for k in range(TOP_K)],
            out_specs=[],
            core_axis_name=("core", "subcore"),
            dimension_semantics=(pltpu.PARALLEL,),
        )(x_hbm, *i_hbms)

    dest_flat = dest.reshape(1, TOP_K * n)
    return run(rows, *([dest_flat] * TOP_K))


def _combine(rows, dest_flat):
    width = rows.shape[1]
    n = dest_flat.shape[0]
    win = SC_WINDOW

    @pl.kernel(out_type=jax.ShapeDtypeStruct((n, width), rows.dtype), mesh=_sc_mesh(),
               scratch_types=[], name="combine")
    def run(y_hbm, i_hbm, o_hbm):
        def body(i_vmem, o_vmem):
            pltpu.sync_copy(y_hbm.at[i_vmem.at[0]], o_vmem)

        pltpu.emit_pipeline(
            body,
            grid=(n // win,),
            in_specs=[pl.BlockSpec((1, win), lambda i: (0, i))],
            out_specs=[pl.BlockSpec((win, width), lambda i: (i, 0))],
            core_axis_name=("core", "subcore"),
            dimension_semantics=(pltpu.PARALLEL,),
        )(i_hbm, o_hbm)

    return run(rows, dest_flat.reshape(1, n))


def _gmm_kernel(blk_exp_ref, n_used_ref, x_ref, wg_ref, wu_ref, wd_ref, y_ref, wgu_b, wd_b):
    j = pl.program_id(0)
    live = j < n_used_ref[0]
    new_expert = (j == 0) | (blk_exp_ref[j] != blk_exp_ref[jnp.maximum(j - 1, 0)])

    @pl.when(live & new_expert)
    def _():
        wgu_b[:, 0:EXP_FF] = wg_ref[0].astype(BF16)
        wgu_b[:, EXP_FF:2 * EXP_FF] = wu_ref[0].astype(BF16)
        wd_b[...] = wd_ref[0].astype(BF16)

    @pl.when(live)
    def _():
        xc = _unpack_rows(x_ref[0], x_ref[1])
        gu = sum(_dot(c, wgu_b[i * PLANE_W:(i + 1) * PLANE_W, :]) for i, c in enumerate(xc))
        act = (_silu(gu[:, :EXP_FF]) * gu[:, EXP_FF:]).astype(BF16)
        _pack_rows(_dot(act, wd_b[...]).astype(BF16), y_ref)


def _gmm(blk_exp, n_used, xs, w_gate, w_up, w_down):
    n_rows = xs.shape[1]
    bm = GMM_BM
    nb = n_rows // bm

    def row_map(j, be, nu):
        return (0, jnp.minimum(j, nu[0] - 1), 0)

    def w_map(j, be, nu):
        return (be[jnp.minimum(j, nu[0] - 1)], 0, 0)

    grid_spec = pltpu.PrefetchScalarGridSpec(
        num_scalar_prefetch=2,
        grid=(nb,),
        in_specs=[pl.BlockSpec((2, bm, PLANE_W), row_map),
                  pl.BlockSpec((1, D_MODEL, EXP_FF), w_map),
                  pl.BlockSpec((1, D_MODEL, EXP_FF), w_map),
                  pl.BlockSpec((1, EXP_FF, D_MODEL), w_map)],
        out_specs=pl.BlockSpec((2, bm, PLANE_W), row_map),
        scratch_shapes=[pltpu.VMEM((D_MODEL, 2 * EXP_FF), BF16), pltpu.VMEM((EXP_FF, D_MODEL), BF16)],
    )
    return pl.pallas_call(
        _gmm_kernel,
        grid_spec=grid_spec,
        out_shape=jax.ShapeDtypeStruct((2, n_rows, PLANE_W), I32),
        compiler_params=pltpu.CompilerParams(dimension_semantics=("arbitrary",)),
        name="gmm",
    )(blk_exp, n_used, xs, w_gate, w_up, w_down)


def _final_kernel(xmid_ref, g2_ref, z_ref, w_ref, gfin_ref, *rest):
    y_ref = rest[-1]
    accs = [jnp.zeros((xmid_ref.shape[0], PLANE_W), F32) for _ in range(4)]
    for k in range(TOP_K):
        wk = w_ref[:, k:k + 1]
        cols = _unpack_rows(z_ref[k, 0], z_ref[k, 1])
        accs = [a + wk * c.astype(F32) for a, c in zip(accs, cols)]
    acc = jnp.concatenate(accs, axis=1)
    y_ref[...] = _rms(xmid_ref[...] + g2_ref[0] * acc) * gfin_ref[...]


def _final(xmid, g2, z, w_t, gfin, tile, *, n_tiles, x_tile0, z_tile0, w_tile0, tiles_per_g2, y_prev=None):
    args = [xmid, g2, z, w_t, gfin]
    in_specs = [pl.BlockSpec((tile, D_MODEL), lambda i: (x_tile0 + i, 0)),
                pl.BlockSpec((1, g2.shape[1], D_MODEL), lambda i: ((x_tile0 + i) // tiles_per_g2, 0, 0)),
                pl.BlockSpec((TOP_K, 2, tile, PLANE_W), lambda i: (0, 0, z_tile0 + i, 0)),
                pl.BlockSpec((tile, TOP_K), lambda i: (w_tile0 + i, 0)),
                pl.BlockSpec((1, D_MODEL), lambda i: (0, 0))]
    aliases = {}
    if y_prev is not None:
        args.append(y_prev)
        in_specs.append(pl.BlockSpec(memory_space=pl.ANY))
        aliases = {len(args) - 1: 0}
    return pl.pallas_call(
        _final_kernel,
        grid=(n_tiles,),
        in_specs=in_specs,
        out_specs=pl.BlockSpec((tile, D_MODEL), lambda i: (x_tile0 + i, 0)),
        out_shape=jax.ShapeDtypeStruct(xmid.shape, F32),
        input_output_aliases=aliases,
        compiler_params=pltpu.CompilerParams(dimension_semantics=("arbitrary",)),
        name="final",
    )(*args)


def kernel(x_prompt, x_sample, state_hgrn, state_conv, c_prompt, c_sample, w_ada, b_ada, norm_mix_g, norm_ffn_g, w_in, lb_logits, hgrn_norm_g, conv_w, conv_b, w_out_hgrn, w_out_conv, w_o, w_router, router_bias, w_exp_gate, w_exp_up, w_exp_down, w_sh_gate, w_sh_up, w_sh_down, final_norm_g):
    assert w_ada.shape[0] == 1 and lb_logits.shape[0] == 2
    bsz, seq, _ = x_prompt.shape
    n_smp = x_sample.shape[0]
    n_prompt = bsz * seq
    n_tok = n_prompt + n_smp

    w_in_b = w_in[0].astype(BF16)
    w_oh_b = w_out_hgrn[0].astype(BF16)
    w_oc_b = w_out_conv[0].astype(BF16)
    w_o_b = w_o[0].astype(BF16)
    wr_t = w_router[0].T
    wr_hi = wr_t.astype(BF16)
    wr_lo = (wr_t - wr_hi.astype(F32)).astype(BF16)
    w_sgu = jnp.concatenate([w_sh_gate[0], w_sh_up[0]], axis=1).astype(BF16)
    w_sd = w_sh_down[0].astype(BF16)
    gmix = norm_mix_g[0].reshape(1, D_MODEL)
    gffn = norm_ffn_g[0].reshape(1, D_MODEL)
    hg = hgrn_norm_g[0].reshape(1, KEY_W)
    cw = conv_w[0]
    cb = conv_b[0].reshape(1, CONV_W)
    gfin = final_norm_g.reshape(1, D_MODEL)

    mod = _ada(jnp.concatenate([c_prompt, c_sample], axis=0), w_ada[0], b_ada[0])
    mod_p, mod_s = mod[:bsz], mod[bsz:]

    xmid_p, h2_all, lgt_all, s_p, cv_p = _mix(
        x_prompt, mod_p.reshape(bsz, 6, D_MODEL),
        jnp.zeros((2, n_tok, PLANE_W), I32), jnp.zeros((N_EXP, n_tok), F32),
        gmix, gffn, w_in_b, lb_logits, hg, cw, cb, w_oh_b, w_oc_b, w_o_b, wr_hi, wr_lo, w_sgu, w_sd)

    xs2 = x_sample.reshape(n_smp, D_MODEL)
    f, kk, q, v, gate, yb, sga, sgb, cv_s = _smp1(
        xs2, mod_s, gmix, w_in_b, lb_logits, cw, cb, state_conv[0].reshape(n_smp, (CONV_K - 1) * CONV_W))
    s_s, o_s = _smp2(f, kk, q, v, state_hgrn[0])
    xmid_s, h2_all, lgt_all = _smp3(xs2, mod_s, o_s, gate, yb, sga, sgb, hg, gffn,
                                    w_oh_b, w_oc_b, w_o_b, wr_hi, wr_lo, w_sgu, w_sd,
                                    h2_all, lgt_all, n_prompt)

    idx, w_tok, rank, cnt = _route(lgt_all, router_bias[0].reshape(N_EXP, 1))

    bm = GMM_BM
    n_blocks = (n_tok * TOP_K + N_EXP * (bm - 1)) // bm
    n_rows = n_blocks * bm
    counts = cnt[:, 0]
    padded = (counts + bm - 1) // bm * bm
    pad_end = jnp.cumsum(padded)
    pad_start = pad_end - padded
    blk_row0 = jnp.arange(n_blocks, dtype=I32) * bm
    blk_exp = jnp.minimum(jnp.sum((pad_end[None, :] <= blk_row0[:, None]).astype(I32), axis=1), N_EXP - 1)
    n_used = (pad_end[-1:] // bm).astype(I32)
    dest = _dest(pad_start.astype(I32), idx, rank, n_rows)

    xs = _dispatch(h2_all.reshape(2 * n_tok, PLANE_W), dest, 2 * n_rows).reshape(2, n_rows, PLANE_W)
    ys = _gmm(blk_exp, n_used, xs, w_exp_gate[0], w_exp_up[0], w_exp_down[0])

    ys_flat = ys.reshape(2 * n_rows, PLANE_W)
    dest3 = dest.reshape(TOP_K, 2, n_tok)
    w_t = w_tok.T
    xmid_p2 = xmid_p.reshape(n_prompt, D_MODEL)
    g2_p = mod_p[:, 5 * D_MODEL:].reshape(bsz, 1, D_MODEL)
    g2_s = mod_s[:, 5 * D_MODEL:].reshape(1, n_smp, D_MODEL)
    chunk = n_prompt // FINAL_CHUNKS
    y_p = None
    for c in range(FINAL_CHUNKS):
        c0 = c * chunk
        c1 = n_tok if c == FINAL_CHUNKS - 1 else c0 + chunk
        z = _combine(ys_flat, dest3[:, :, c0:c1].reshape(-1)).reshape(TOP_K, 2, c1 - c0, PLANE_W)
        y_p = _final(xmid_p2, g2_p, z, w_t, gfin, FINAL_TILE, n_tiles=chunk // FINAL_TILE,
                     x_tile0=c0 // FINAL_TILE, z_tile0=0, w_tile0=c0 // FINAL_TILE,
                     tiles_per_g2=seq // FINAL_TILE, y_prev=y_p)
    y_s = _final(xmid_s, g2_s, z, w_t, gfin, n_smp, n_tiles=1, x_tile0=0, z_tile0=chunk // n_smp,
                 w_tile0=n_prompt // n_smp, tiles_per_g2=1)

    return (y_p.reshape(bsz, seq, D_MODEL), y_s.reshape(n_smp, 1, D_MODEL),
            s_p[None], cv_p[None], s_s[None], cv_s.reshape(1, n_smp, CONV_K - 1, CONV_W))
```

```python
import functools

import jax
import jax.numpy as jnp
from jax import lax
from jax.experimental import pallas as pl
from jax.experimental.pallas import tpu as pltpu
from jax.experimental.pallas import tpu_sc as plsc

F32 = jnp.float32
BF16 = jnp.bfloat16
I32 = jnp.int32

D_MODEL = 1024
HALF_D = D_MODEL // 2
HEADS = 4
DK = 128
KEY_W = HEADS * DK
CONV_W = 512
CONV_K = 3
IN_W = 2 * KEY_W + 2 * KEY_W + 3 * CONV_W + 2 * D_MODEL
N_EXP = 64
TOP_K = 8
N_GRP = 8
GRP_SZ = N_EXP // N_GRP
TOPK_GRP = 4
EXP_FF = 256
SH_FF = 256
ROUTED_SCALE = 2.5
EPS = 1e-6

C_Q, C_F, C_I, C_G = 0, 512, 1024, 1536
C_BB, C_CC, C_VB = 2048, 2560, 3072
C_MGA, C_MGB = 3584, 4608

MIX_TILE = 512
SUB = 128
HALF = SUB // 2
ROUTE_TILE = 384
GMM_BM = 512
FINAL_TILE = 256
FINAL_CHUNKS = 4
SMP_GROUP = 8
SC_WINDOW = 128
PLANE_W = HALF_D // 2
VMEM_LIMIT = 56 * 1024 * 1024


def _dot(a, b):
    return jnp.dot(a, b, preferred_element_type=F32)


def _dot_nt(a, b):
    return lax.dot_general(a, b, (((1,), (1,)), ((), ())), preferred_element_type=F32)


def _dot_tn(a, b):
    return lax.dot_general(a, b, (((0,), (0,)), ((), ())), preferred_element_type=F32)


def _sigmoid(x):
    return 0.5 * jnp.tanh(0.5 * x) + 0.5


def _silu(x):
    h = 0.5 * x
    return h * jnp.tanh(h) + h


def _rms(x):
    return x * lax.rsqrt(jnp.mean(x * x, axis=-1, keepdims=True) + EPS)


def _lower_bound(lbl):
    a, b = lbl[0:1], lbl[1:2]
    m = jnp.maximum(a, b)
    ea, eb = jnp.exp(a - m), jnp.exp(b - m)
    return ea / (ea + eb)


def _split3(x):
    hi = x.astype(BF16)
    r1 = x - hi.astype(F32)
    mid = r1.astype(BF16)
    lo = (r1 - mid.astype(F32)).astype(BF16)
    return hi, mid, lo


def _pack_rows(xb, out_ref):
    bits = lax.bitcast_convert_type(xb.astype(F32), I32)
    lo = lax.shift_right_logical(bits[:, :HALF_D], 16)
    hi = bits[:, HALF_D:] & jnp.int32(-65536)
    words = lo | hi
    out_ref[0] = words[:, :PLANE_W]
    out_ref[1] = words[:, PLANE_W:]


def _unpack_rows(p0, p1):
    def halves(w):
        lo = lax.bitcast_convert_type(lax.shift_left(w, 16), F32)
        hi = lax.bitcast_convert_type(w & jnp.int32(-65536), F32)
        return lo.astype(BF16), hi.astype(BF16)

    c0, c2 = halves(p0)
    c1, c3 = halves(p1)
    return c0, c1, c2, c3


def _ada_kernel(c_ref, w_ref, b_ref, o_ref):
    a = _silu(c_ref[...]).astype(BF16)
    o_ref[...] = _dot(a, w_ref[...].astype(BF16)) + b_ref[...]


def _ada(c_all, w_ada, b_ada):
    n = c_all.shape[0]
    blk = 1024
    return pl.pallas_call(
        _ada_kernel,
        grid=(6 * D_MODEL // blk,),
        in_specs=[pl.BlockSpec((n, D_MODEL), lambda j: (0, 0)),
                  pl.BlockSpec((D_MODEL, blk), lambda j: (0, j)),
                  pl.BlockSpec((1, blk), lambda j: (0, j))],
        out_specs=pl.BlockSpec((n, blk), lambda j: (0, j)),
        out_shape=jax.ShapeDtypeStruct((n, 6 * D_MODEL), F32),
        name="ada",
    )(c_all, w_ada, b_ada.reshape(1, -1))


def _ffn_pre(x1, mod_rows, gffn, w_sgu, w_sd, wr_hi, wr_lo):
    sh2, sc2, g2 = mod_rows
    h2 = _rms(x1) * gffn * (1.0 + sc2) + sh2
    h2b = h2.astype(BF16)
    gu = _dot(h2b, w_sgu)
    act = _silu(gu[:, :SH_FF]) * gu[:, SH_FF:]
    xmid = x1 + g2 * _dot(act.astype(BF16), w_sd)
    h2lo = (h2 - h2b.astype(F32)).astype(BF16)
    lgt = _dot_nt(wr_hi, h2b) + _dot_nt(wr_hi, h2lo) + _dot_nt(wr_lo, h2b)
    return xmid, h2b, lgt


def _mix_kernel(x_ref, mod_ref, gmix_ref, gffn_ref, w_in_ref, lbl_ref, hg_ref, cw_ref, cb_ref,
                w_oh_ref, w_oc_ref, w_o_ref, wr_hi_ref, wr_lo_ref, w_sgu_ref, w_sd_ref,
                h2_all_ref, lgt_all_ref,
                xmid_ref, h2_ref, lgt_ref, s_out_ref, cv_out_ref,
                proj_ref, st_ref, cbuf_ref, ya_ref):
    del h2_all_ref, lgt_all_ref
    t = pl.program_id(1)
    nt = pl.num_programs(1)
    tt = x_ref.shape[1]

    @pl.when(t == 0)
    def _():
        st_ref[...] = jnp.zeros_like(st_ref)
        cbuf_ref[...] = jnp.zeros_like(cbuf_ref)

    x = x_ref[0]
    mod = mod_ref[0]
    sh1, sc1, g1 = mod[0:1], mod[1:2], mod[2:3]
    h = _rms(x) * gmix_ref[...] * (1.0 + sc1) + sh1
    hb = h.astype(BF16)
    for c in range(0, IN_W, 512):
        proj_ref[:, c:c + 512] = _dot(hb, w_in_ref[:, c:c + 512])

    lb = _lower_bound(lbl_ref[...])
    row = lax.broadcasted_iota(I32, (SUB, SUB), 0)
    col = lax.broadcasted_iota(I32, (SUB, SUB), 1)
    tri = (col <= row).astype(BF16)
    mask_d = (col <= row) & ((row >= HALF) == (col >= HALF))
    top_half = lax.broadcasted_iota(I32, (SUB, DK), 0) < HALF

    for s in range(tt // SUB):
        r0 = s * SUB
        f = lb + (1.0 - lb) * _sigmoid(proj_ref[r0:r0 + SUB, C_F:C_F + KEY_W])
        kk = 1.0 - f
        hi, mid, lo = _split3(jnp.log(f))
        bc = _dot(tri, hi) + _dot(tri, mid) + _dot(tri, lo)
        for hd in range(HEADS):
            hs = slice(hd * DK, (hd + 1) * DK)
            bh = bc[:, hs]
            b31, b63 = bh[HALF // 2 - 1:HALF // 2], bh[HALF - 1:HALF]
            b95, b127 = bh[HALF + HALF // 2 - 1:HALF + HALF // 2], bh[SUB - 1:SUB]
            arg = bh - jnp.where(top_half, b31, b95)
            e_pos, e_neg = jnp.exp(arg), jnp.exp(-arg)
            q = _silu(proj_ref[r0:r0 + SUB, C_Q + hd * DK:C_Q + (hd + 1) * DK])
            v = proj_ref[r0:r0 + SUB, C_I + hd * DK:C_I + (hd + 1) * DK]
            qd = q * e_pos
            kd = kk[:, hs] * e_neg
            q_in = qd * jnp.where(top_half, jnp.exp(b31), jnp.exp(b95))
            k_end = kd * jnp.where(top_half, jnp.exp(b127 - b31), jnp.exp(b127 - b95))
            qa = jnp.where(top_half, 0.0, qd * jnp.exp(b95 - b63))
            ka = jnp.where(top_half, kd * jnp.exp(b63 - b31), 0.0)
            att = jnp.where(mask_d, _dot_nt(qd.astype(BF16), kd.astype(BF16)), 0.0)
            att = att + _dot_nt(qa.astype(BF16), ka.astype(BF16))
            vb = v.astype(BF16)
            st = st_ref[hd]
            o = _dot(att.astype(BF16), vb) + _dot_nt(q_in.astype(BF16), st.astype(BF16))
            st_ref[hd] = st * jnp.exp(b127) + _dot_tn(vb, k_end.astype(BF16))
            gate = _silu(proj_ref[r0:r0 + SUB, C_G + hd * DK:C_G + (hd + 1) * DK])
            ya_ref[r0:r0 + SUB, hs] = _rms(o) * hg_ref[:, hs] * gate

    u = proj_ref[:, C_CC:C_CC + CONV_W] * proj_ref[:, C_VB:C_VB + CONV_W]
    rows = lax.broadcasted_iota(I32, (tt, CONV_W), 0)
    c0, c1 = cbuf_ref[0:1], cbuf_ref[1:2]
    u1 = jnp.where(rows == 0, c1, pltpu.roll(u, 1, axis=0))
    u2 = jnp.where(rows == 0, c0, jnp.where(rows == 1, c1, pltpu.roll(u, 2, axis=0)))
    conv = cw_ref[0:1] * u2 + cw_ref[1:2] * u1 + cw_ref[2:3] * u + cb_ref[...]
    yb = proj_ref[:, C_BB:C_BB + CONV_W] * conv
    cbuf_ref[...] = u[tt - 2:tt]

    mixed = (_sigmoid(proj_ref[:, C_MGA:C_MGA + D_MODEL]) * _dot(ya_ref[...].astype(BF16), w_oh_ref[...])
             + _sigmoid(proj_ref[:, C_MGB:C_MGB + D_MODEL]) * _dot(yb.astype(BF16), w_oc_ref[...]))
    x1 = x + g1 * _dot(mixed.astype(BF16), w_o_ref[...])

    xmid, h2b, lgt = _ffn_pre(x1, (mod[3:4], mod[4:5], mod[5:6]), gffn_ref[...],
                              w_sgu_ref[...], w_sd_ref[...], wr_hi_ref[...], wr_lo_ref[...])
    xmid_ref[0] = xmid
    _pack_rows(h2b, h2_ref)
    lgt_ref[...] = lgt

    @pl.when(t == nt - 1)
    def _():
        for hd in range(HEADS):
            s_out_ref[0, hd] = st_ref[hd].T
        cv_out_ref[0] = cbuf_ref[...]


def _const_spec(shape):
    nd = len(shape)
    return pl.BlockSpec(shape, lambda b, t, _nd=nd: (0,) * _nd, pipeline_mode=pl.Buffered(1))


def _mix(x, mod, h2_all, lgt_all, gmix, gffn, w_in, lbl, hg, cw, cb, w_oh, w_oc, w_o, wr_hi, wr_lo,
         w_sgu, w_sd):
    bsz, seq, _ = x.shape
    n_tok = h2_all.shape[1]
    tt = MIX_TILE
    nt = seq // tt
    consts = [gmix, gffn, w_in, lbl, hg, cw, cb, w_oh, w_oc, w_o, wr_hi, wr_lo, w_sgu, w_sd]
    return pl.pallas_call(
        _mix_kernel,
        grid=(bsz, nt),
        in_specs=[pl.BlockSpec((1, tt, D_MODEL), lambda b, t: (b, t, 0)),
                  pl.BlockSpec((1, 6, D_MODEL), lambda b, t: (b, 0, 0))]
                 + [_const_spec(a.shape) for a in consts]
                 + [pl.BlockSpec(memory_space=pl.ANY), pl.BlockSpec(memory_space=pl.ANY)],
        input_output_aliases={2 + len(consts): 1, 3 + len(consts): 2},
        out_specs=[pl.BlockSpec((1, tt, D_MODEL), lambda b, t: (b, t, 0)),
                   pl.BlockSpec((2, tt, PLANE_W), lambda b, t: (0, b * nt + t, 0)),
                   pl.BlockSpec((N_EXP, tt), lambda b, t: (0, b * nt + t)),
                   pl.BlockSpec((1, HEADS, DK, DK), lambda b, t: (b, 0, 0, 0)),
                   pl.BlockSpec((1, CONV_K - 1, CONV_W), lambda b, t: (b, 0, 0))],
        out_shape=[jax.ShapeDtypeStruct((bsz, seq, D_MODEL), F32),
                   jax.ShapeDtypeStruct((2, n_tok, PLANE_W), I32),
                   jax.ShapeDtypeStruct((N_EXP, n_tok), F32),
                   jax.ShapeDtypeStruct((bsz, HEADS, DK, DK), F32),
                   jax.ShapeDtypeStruct((bsz, CONV_K - 1, CONV_W), F32)],
        scratch_shapes=[pltpu.VMEM((tt, IN_W), F32),
                        pltpu.VMEM((HEADS, DK, DK), F32),
                        pltpu.VMEM((CONV_K - 1, CONV_W), F32),
                        pltpu.VMEM((tt, KEY_W), F32)],
        compiler_params=pltpu.CompilerParams(
            dimension_semantics=("arbitrary", "arbitrary"), vmem_limit_bytes=VMEM_LIMIT),
        name="mix",
    )(x, mod, *consts, h2_all, lgt_all)


def _smp1_kernel(x_ref, mod_ref, gmix_ref, w_in_ref, lbl_ref, cw_ref, cb_ref, cst_ref,
                 f_ref, k_ref, q_ref, v_ref, gate_ref, yb_ref, sga_ref, sgb_ref, cv_out_ref):
    x = x_ref[...]
    sh1, sc1 = mod_ref[:, 0:D_MODEL], mod_ref[:, D_MODEL:2 * D_MODEL]
    h = _rms(x) * gmix_ref[...] * (1.0 + sc1) + sh1
    hb = h.astype(BF16)

    def proj(c, w):
        return _dot(hb, w_in_ref[:, c:c + w])

    lb = _lower_bound(lbl_ref[...])
    f = lb + (1.0 - lb) * _sigmoid(proj(C_F, KEY_W))
    f_ref[...] = f
    k_ref[...] = 1.0 - f
    q_ref[...] = _silu(proj(C_Q, KEY_W))
    v_ref[...] = proj(C_I, KEY_W)
    gate_ref[...] = _silu(proj(C_G, KEY_W))
    u = proj(C_CC, CONV_W) * proj(C_VB, CONV_W)
    c0, c1 = cst_ref[:, 0:CONV_W], cst_ref[:, CONV_W:2 * CONV_W]
    conv = cw_ref[0:1] * c0 + cw_ref[1:2] * c1 + cw_ref[2:3] * u + cb_ref[...]
    yb_ref[...] = proj(C_BB, CONV_W) * conv
    cv_out_ref[:, 0:CONV_W] = c1
    cv_out_ref[:, CONV_W:2 * CONV_W] = u
    sga_ref[...] = _sigmoid(proj(C_MGA, D_MODEL))
    sgb_ref[...] = _sigmoid(proj(C_MGB, D_MODEL))


def _smp1(x, mod, gmix, w_in, lbl, cw, cb, cst):
    n = x.shape[0]
    kw = jax.ShapeDtypeStruct((n, KEY_W), F32)
    dm = jax.ShapeDtypeStruct((n, D_MODEL), F32)
    return pl.pallas_call(
        _smp1_kernel,
        out_shape=[kw, kw, kw, kw, kw, kw, dm, dm,
                   jax.ShapeDtypeStruct((n, (CONV_K - 1) * CONV_W), F32)],
        compiler_params=pltpu.CompilerParams(vmem_limit_bytes=VMEM_LIMIT),
        name="smp1",
    )(x, mod, gmix, w_in, lbl, cw, cb, cst)


def _smp2_kernel(f_ref, k_ref, q_ref, v_ref, s_ref, s_out_ref, o_ref):
    g = f_ref.shape[0]
    for i in range(g):
        for hd in range(HEADS):
            hs = slice(hd * DK, (hd + 1) * DK)

            def col(ref):
                return jnp.broadcast_to(ref[i:i + 1, hs], (DK, DK)).T

            vrow = v_ref[i:i + 1, hs]
            s_new = col(f_ref) * s_ref[i, hd] + col(k_ref) * vrow
            s_out_ref[i, hd] = s_new
            o_ref[i:i + 1, hs] = jnp.sum(col(q_ref) * s_new, axis=0, keepdims=True)


def _smp2(f, k, q, v, state):
    n = f.shape[0]
    g = SMP_GROUP
    row_spec = pl.BlockSpec((g, KEY_W), lambda i: (i, 0))
    st_spec = pl.BlockSpec((g, HEADS, DK, DK), lambda i: (i, 0, 0, 0))
    return pl.pallas_call(
        _smp2_kernel,
        grid=(n // g,),
        in_specs=[row_spec, row_spec, row_spec, row_spec, st_spec],
        out_specs=[st_spec, row_spec],
        out_shape=[jax.ShapeDtypeStruct(state.shape, F32), jax.ShapeDtypeStruct((n, KEY_W), F32)],
        compiler_params=pltpu.CompilerParams(dimension_semantics=("arbitrary",)),
        name="smp2",
    )(f, k, q, v, state)


def _smp3_kernel(x_ref, mod_ref, o_ref, gate_ref, yb_ref, sga_ref, sgb_ref, hg_ref, gffn_ref,
                 w_oh_ref, w_oc_ref, w_o_ref, wr_hi_ref, wr_lo_ref, w_sgu_ref, w_sd_ref,
                 h2_all_ref, lgt_all_ref, xmid_ref, h2_ref, lgt_ref):
    del h2_all_ref, lgt_all_ref
    parts = []
    for hd in range(HEADS):
        hs = slice(hd * DK, (hd + 1) * DK)
        parts.append(_rms(o_ref[:, hs]) * hg_ref[:, hs] * gate_ref[:, hs])
    ya = jnp.concatenate(parts, axis=1)
    mixed = (sga_ref[...] * _dot(ya.astype(BF16), w_oh_ref[...])
             + sgb_ref[...] * _dot(yb_ref[...].astype(BF16), w_oc_ref[...]))
    g1 = mod_ref[:, 2 * D_MODEL:3 * D_MODEL]
    x1 = x_ref[...] + g1 * _dot(mixed.astype(BF16), w_o_ref[...])
    mod_rows = tuple(mod_ref[:, j * D_MODEL:(j + 1) * D_MODEL] for j in (3, 4, 5))
    xmid, h2b, lgt = _ffn_pre(x1, mod_rows, gffn_ref[...], w_sgu_ref[...], w_sd_ref[...],
                              wr_hi_ref[...], wr_lo_ref[...])
    xmid_ref[...] = xmid
    _pack_rows(h2b, h2_ref)
    lgt_ref[...] = lgt


def _smp3(x, mod, o, gate, yb, sga, sgb, hg, gffn, w_oh, w_oc, w_o, wr_hi, wr_lo, w_sgu, w_sd,
          h2_all, lgt_all, n_prompt):
    n = x.shape[0]
    vmem_args = [x, mod, o, gate, yb, sga, sgb, hg, gffn, w_oh, w_oc, w_o, wr_hi, wr_lo, w_sgu, w_sd]
    blk = n_prompt // n

    def full(a):
        nd = a.ndim
        return pl.BlockSpec(a.shape, lambda i, _nd=nd: (0,) * _nd)

    return pl.pallas_call(
        _smp3_kernel,
        grid=(1,),
        in_specs=[full(a) for a in vmem_args]
                 + [pl.BlockSpec(memory_space=pl.ANY), pl.BlockSpec(memory_space=pl.ANY)],
        out_specs=[pl.BlockSpec((n, D_MODEL), lambda i: (0, 0)),
                   pl.BlockSpec((2, n, PLANE_W), lambda i: (0, blk, 0)),
                   pl.BlockSpec((N_EXP, n), lambda i: (0, blk))],
        out_shape=[jax.ShapeDtypeStruct((n, D_MODEL), F32),
                   jax.ShapeDtypeStruct(h2_all.shape, h2_all.dtype),
                   jax.ShapeDtypeStruct(lgt_all.shape, lgt_all.dtype)],
        input_output_aliases={len(vmem_args): 1, len(vmem_args) + 1: 2},
        compiler_params=pltpu.CompilerParams(
            dimension_semantics=("arbitrary",), vmem_limit_bytes=VMEM_LIMIT),
        name="smp3",
    )(*vmem_args, h2_all, lgt_all)


def _route_kernel(lgt_ref, bias_ref, idx_ref, w_ref, rank_ref, cnt_ref, carry_ref):
    i = pl.program_id(0)
    tr = lgt_ref.shape[1]
    neg = -jnp.inf

    @pl.when(i == 0)
    def _():
        carry_ref[...] = jnp.zeros_like(carry_ref)

    scores = _sigmoid(lgt_ref[...])
    sel = scores + bias_ref[...]
    j8 = lax.broadcasted_iota(I32, (GRP_SZ, tr), 0)
    groups = [sel[g * GRP_SZ:(g + 1) * GRP_SZ] for g in range(N_GRP)]
    gscore = []
    for grp in groups:
        m1 = jnp.max(grp, axis=0, keepdims=True)
        i1 = jnp.min(jnp.where(grp == m1, j8, GRP_SZ), axis=0, keepdims=True)
        m2 = jnp.max(jnp.where(j8 == i1, neg, grp), axis=0, keepdims=True)
        gscore.append(m1 + m2)
    kept = []
    for g in range(N_GRP):
        beaten = jnp.zeros((1, tr), I32)
        for o in range(N_GRP):
            if o < g:
                beaten = beaten + (gscore[o] >= gscore[g]).astype(I32)
            elif o > g:
                beaten = beaten + (gscore[o] > gscore[g]).astype(I32)
        kept.append(jnp.where(beaten < TOPK_GRP, groups[g], neg))
    masked = jnp.concatenate(kept, axis=0)
    ei = lax.broadcasted_iota(I32, masked.shape, 0)
    chosen = jnp.zeros(masked.shape, jnp.bool_)
    picks, weights = [], []
    for _ in range(TOP_K):
        m = jnp.max(masked, axis=0, keepdims=True)
        pick = jnp.min(jnp.where(masked == m, ei, N_EXP), axis=0, keepdims=True)
        hit = ei == pick
        weights.append(jnp.sum(jnp.where(hit, scores, 0.0), axis=0, keepdims=True))
        picks.append(pick)
        chosen = chosen | hit
        masked = jnp.where(hit, neg, masked)
    wsum = weights[0]
    for w in weights[1:]:
        wsum = wsum + w
    sel01 = chosen.astype(F32)
    r = lax.broadcasted_iota(I32, (tr, tr), 0)
    c = lax.broadcasted_iota(I32, (tr, tr), 1)
    before = (r < c).astype(BF16)
    cnt = _dot(sel01.astype(BF16), before) + carry_ref[...]
    for k in range(TOP_K):
        idx_ref[k:k + 1, :] = picks[k]
        w_ref[k:k + 1, :] = weights[k] / wsum * ROUTED_SCALE
        rank = jnp.sum(jnp.where(ei == picks[k], cnt, 0.0), axis=0, keepdims=True)
        rank_ref[k:k + 1, :] = rank.astype(I32)
    carry_ref[...] = carry_ref[...] + jnp.sum(sel01, axis=1, keepdims=True)
    cnt_ref[...] = jnp.broadcast_to(carry_ref[...], cnt_ref.shape).astype(I32)


def _route(lgt, bias):
    n = lgt.shape[1]
    tr = ROUTE_TILE
    slot = lambda dt: jax.ShapeDtypeStruct((TOP_K, n), dt)
    slot_spec = pl.BlockSpec((TOP_K, tr), lambda i: (0, i))
    return pl.pallas_call(
        _route_kernel,
        grid=(n // tr,),
        in_specs=[pl.BlockSpec((N_EXP, tr), lambda i: (0, i)),
                  pl.BlockSpec((N_EXP, 1), lambda i: (0, 0))],
        out_specs=[slot_spec, slot_spec, slot_spec, pl.BlockSpec((N_EXP, 128), lambda i: (0, 0))],
        out_shape=[slot(I32), slot(F32), slot(I32),
                   jax.ShapeDtypeStruct((N_EXP, 128), I32)],
        scratch_shapes=[pltpu.VMEM((N_EXP, 1), F32)],
        compiler_params=pltpu.CompilerParams(dimension_semantics=("arbitrary",)),
        name="route",
    )(lgt, bias)


def _dest_kernel(start_ref, idx_ref, rank_ref, dest_ref, *, n_rows):
    n_tok = idx_ref.shape[1]
    idx = idx_ref[...]
    acc = rank_ref[...]
    for e in range(N_EXP):
        acc = acc + jnp.where(idx == e, start_ref[e], 0)
    dest_ref[:, 0:n_tok] = acc
    dest_ref[:, n_tok:2 * n_tok] = acc + n_rows


def _dest(pad_start, idx, rank, n_rows):
    k, n_tok = idx.shape
    return pl.pallas_call(
        functools.partial(_dest_kernel, n_rows=n_rows),
        in_specs=[pl.BlockSpec(memory_space=pltpu.SMEM),
                  pl.BlockSpec(memory_space=pltpu.VMEM),
                  pl.BlockSpec(memory_space=pltpu.VMEM)],
        out_specs=pl.BlockSpec(memory_space=pltpu.VMEM),
        out_shape=jax.ShapeDtypeStruct((k, 2 * n_tok), I32),
        name="dest",
    )(pad_start, idx, rank)


def _sc_mesh():
    return plsc.VectorSubcoreMesh(core_axis_name="core", subcore_axis_name="subcore")


def _dispatch(rows, dest, n_out):
    n, width = rows.shape
    win = SC_WINDOW
    steps = n // win

    @pl.kernel(out_type=jax.ShapeDtypeStruct((n_out, width), rows.dtype), mesh=_sc_mesh(),
               scratch_types=[], name="dispatch")
    def run(x_hbm, *refs):
        i_hbms, o_hbm = refs[:TOP_K], refs[TOP_K]

        def body(x_vmem, *i_vmems):
            for i_vmem in i_vmems:
                pltpu.sync_copy(x_vmem, o_hbm.at[i_vmem.at[0]])

        pltpu.emit_pipeline(
            body,
            grid=(steps,),
            in_specs=[pl.BlockSpec((win, width), lambda i: (i, 0))]
                     + [pl.BlockSpec((1, win), lambda i, k=k: (0, k * steps + i)) for k in range(TOP_K)],
            out_specs=[],
            core_axis_name=("core", "subcore"),
            dimension_semantics=(pltpu.PARALLEL,),
        )(x_hbm, *i_hbms)

    dest_flat = dest.reshape(1, TOP_K * n)
    return run(rows, *([dest_flat] * TOP_K))


def _combine(rows, dest_flat):
    width = rows.shape[1]
    n = dest_flat.shape[0]
    win = SC_WINDOW

    @pl.kernel(out_type=jax.ShapeDtypeStruct((n, width), rows.dtype), mesh=_sc_mesh(),
               scratch_types=[], name="combine")
    def run(y_hbm, i_hbm, o_hbm):
        def body(i_vmem, o_vmem):
            pltpu.sync_copy(y_hbm.at[i_vmem.at[0]], o_vmem)

        pltpu.emit_pipeline(
            body,
            grid=(n // win,),
            in_specs=[pl.BlockSpec((1, win), lambda i: (0, i))],
            out_specs=[pl.BlockSpec((win, width), lambda i: (i, 0))],
            core_axis_name=("core", "subcore"),
            dimension_semantics=(pltpu.PARALLEL,),
        )(i_hbm, o_hbm)

    return run(rows, dest_flat.reshape(1, n))


def _gmm_kernel(blk_exp_ref, n_used_ref, xs_hbm, wg_hbm, wu_hbm, wd_hbm, ys_hbm,
                xbuf, ybuf, wg32, wu32, wd32, wgu_b, wd_b, xsem, ysem, wsem, run_ref):
    bm = xbuf.shape[2]
    n_used = n_used_ref[0]

    def x_copy(b, slot):
        return pltpu.make_async_copy(xs_hbm.at[:, pl.ds(b * bm, bm), :], xbuf.at[slot], xsem.at[slot])

    def y_copy(b, slot):
        return pltpu.make_async_copy(ybuf.at[slot], ys_hbm.at[:, pl.ds(b * bm, bm), :], ysem.at[slot])

    def w_copies(e, slot):
        return (pltpu.make_async_copy(wg_hbm.at[e], wg32.at[slot], wsem.at[slot, 0]),
                pltpu.make_async_copy(wu_hbm.at[e], wu32.at[slot], wsem.at[slot, 1]),
                pltpu.make_async_copy(wd_hbm.at[e], wd32.at[slot], wsem.at[slot, 2]))

    def run_end(b):
        return lax.while_loop(lambda j: (j < n_used) & (blk_exp_ref[jnp.minimum(j, n_used - 1)] == blk_exp_ref[b]),
                              lambda j: j + 1, b + 1)

    run_ref[0] = 0
    x_copy(0, 0).start()
    for c in w_copies(blk_exp_ref[0], 0):
        c.start()

    def block(b, carry):
        slot = b % 2

        @pl.when(b + 1 < n_used)
        def _():
            x_copy(b + 1, 1 - slot).start()

        @pl.when((b == 0) | (blk_exp_ref[b] != blk_exp_ref[jnp.maximum(b - 1, 0)]))
        def _():
            wslot = run_ref[0] % 2
            run_ref[0] = run_ref[0] + 1
            for c in w_copies(blk_exp_ref[b], wslot):
                c.wait()
            wgu_b[:, 0:EXP_FF] = wg32[wslot].astype(BF16)
            wgu_b[:, EXP_FF:2 * EXP_FF] = wu32[wslot].astype(BF16)
            wd_b[...] = wd32[wslot].astype(BF16)
            nxt = run_end(b)

            @pl.when(nxt < n_used)
            def _():
                for c in w_copies(blk_exp_ref[jnp.minimum(nxt, n_used - 1)], 1 - wslot):
                    c.start()

        x_copy(b, slot).wait()

        @pl.when(b >= 2)
        def _():
            y_copy(b - 2, slot).wait()

        xc = _unpack_rows(xbuf[slot, 0], xbuf[slot, 1])
        gu = sum(_dot(c, wgu_b[i * PLANE_W:(i + 1) * PLANE_W, :]) for i, c in enumerate(xc))
        act = (_silu(gu[:, :EXP_FF]) * gu[:, EXP_FF:]).astype(BF16)
        _pack_rows(_dot(act, wd_b[...]).astype(BF16), ybuf.at[slot])
        y_copy(b, slot).start()
        return carry

    lax.fori_loop(0, n_used, block, 0)

    @pl.when(n_used >= 2)
    def _():
        y_copy(n_used - 2, n_used % 2).wait()

    y_copy(n_used - 1, (n_used - 1) % 2).wait()


def _gmm(blk_exp, n_used, xs, w_gate, w_up, w_down):
    n_rows = xs.shape[1]
    bm = GMM_BM
    nb = n_rows // bm

    assert blk_exp.shape == (nb,)
    any_spec = pl.BlockSpec(memory_space=pl.ANY)
    grid_spec = pltpu.PrefetchScalarGridSpec(
        num_scalar_prefetch=2,
        grid=(1,),
        in_specs=[any_spec, any_spec, any_spec, any_spec],
        out_specs=any_spec,
        scratch_shapes=[pltpu.VMEM((2, 2, bm, PLANE_W), I32), pltpu.VMEM((2, 2, bm, PLANE_W), I32),
                        pltpu.VMEM((2, D_MODEL, EXP_FF), F32), pltpu.VMEM((2, D_MODEL, EXP_FF), F32),
                        pltpu.VMEM((2, EXP_FF, D_MODEL), F32),
                        pltpu.VMEM((D_MODEL, 2 * EXP_FF), BF16), pltpu.VMEM((EXP_FF, D_MODEL), BF16),
                        pltpu.SemaphoreType.DMA((2,)), pltpu.SemaphoreType.DMA((2,)),
                        pltpu.SemaphoreType.DMA((2, 3)), pltpu.SMEM((1,), I32)],
    )
    return pl.pallas_call(
        _gmm_kernel,
        grid_spec=grid_spec,
        out_shape=jax.ShapeDtypeStruct((2, n_rows, PLANE_W), I32),
        compiler_params=pltpu.CompilerParams(dimension_semantics=("arbitrary",)),
        name="gmm",
    )(blk_exp, n_used, xs, w_gate, w_up, w_down)


def _final_kernel(xmid_ref, g2_ref, z_ref, w_ref, gfin_ref, *rest):
    y_ref = rest[-1]
    accs = [jnp.zeros((xmid_ref.shape[0], PLANE_W), F32) for _ in range(4)]
    for k in range(TOP_K):
        wk = w_ref[:, k:k + 1]
        cols = _unpack_rows(z_ref[k, 0], z_ref[k, 1])
        accs = [a + wk * c.astype(F32) for a, c in zip(accs, cols)]
    acc = jnp.concatenate(accs, axis=1)
    y_ref[...] = _rms(xmid_ref[...] + g2_ref[0] * acc) * gfin_ref[...]


def _final(xmid, g2, z, w_t, gfin, tile, *, n_tiles, x_tile0, z_tile0, w_tile0, tiles_per_g2, y_prev=None):
    args = [xmid, g2, z, w_t, gfin]
    in_specs = [pl.BlockSpec((tile, D_MODEL), lambda i: (x_tile0 + i, 0)),
                pl.BlockSpec((1, g2.shape[1], D_MODEL), lambda i: ((x_tile0 + i) // tiles_per_g2, 0, 0)),
                pl.BlockSpec((TOP_K, 2, tile, PLANE_W), lambda i: (0, 0, z_tile0 + i, 0)),
                pl.BlockSpec((tile, TOP_K), lambda i: (w_tile0 + i, 0)),
                pl.BlockSpec((1, D_MODEL), lambda i: (0, 0))]
    aliases = {}
    if y_prev is not None:
        args.append(y_prev)
        in_specs.append(pl.BlockSpec(memory_space=pl.ANY))
        aliases = {len(args) - 1: 0}
    return pl.pallas_call(
        _final_kernel,
        grid=(n_tiles,),
        in_specs=in_specs,
        out_specs=pl.BlockSpec((tile, D_MODEL), lambda i: (x_tile0 + i, 0)),
        out_shape=jax.ShapeDtypeStruct(xmid.shape, F32),
        input_output_aliases=aliases,
        compiler_params=pltpu.CompilerParams(dimension_semantics=("arbitrary",)),
        name="final",
    )(*args)


def kernel(x_prompt, x_sample, state_hgrn, state_conv, c_prompt, c_sample, w_ada, b_ada, norm_mix_g, norm_ffn_g, w_in, lb_logits, hgrn_norm_g, conv_w, conv_b, w_out_hgrn, w_out_conv, w_o, w_router, router_bias, w_exp_gate, w_exp_up, w_exp_down, w_sh_gate, w_sh_up, w_sh_down, final_norm_g):
    assert w_ada.shape[0] == 1 and lb_logits.shape[0] == 2
    bsz, seq, _ = x_prompt.shape
    n_smp = x_sample.shape[0]
    n_prompt = bsz * seq
    n_tok = n_prompt + n_smp

    w_in_b = w_in[0].astype(BF16)
    w_oh_b = w_out_hgrn[0].astype(BF16)
    w_oc_b = w_out_conv[0].astype(BF16)
    w_o_b = w_o[0].astype(BF16)
    wr_t = w_router[0].T
    wr_hi = wr_t.astype(BF16)
    wr_lo = (wr_t - wr_hi.astype(F32)).astype(BF16)
    w_sgu = jnp.concatenate([w_sh_gate[0], w_sh_up[0]], axis=1).astype(BF16)
    w_sd = w_sh_down[0].astype(BF16)
    gmix = norm_mix_g[0].reshape(1, D_MODEL)
    gffn = norm_ffn_g[0].reshape(1, D_MODEL)
    hg = hgrn_norm_g[0].reshape(1, KEY_W)
    cw = conv_w[0]
    cb = conv_b[0].reshape(1, CONV_W)
    gfin = final_norm_g.reshape(1, D_MODEL)

    mod = _ada(jnp.concatenate([c_prompt, c_sample], axis=0), w_ada[0], b_ada[0])
    mod_p, mod_s = mod[:bsz], mod[bsz:]

    xmid_p, h2_all, lgt_all, s_p, cv_p = _mix(
        x_prompt, mod_p.reshape(bsz, 6, D_MODEL),
        jnp.zeros((2, n_tok, PLANE_W), I32), jnp.zeros((N_EXP, n_tok), F32),
        gmix, gffn, w_in_b, lb_logits, hg, cw, cb, w_oh_b, w_oc_b, w_o_b, wr_hi, wr_lo, w_sgu, w_sd)

    xs2 = x_sample.reshape(n_smp, D_MODEL)
    f, kk, q, v, gate, yb, sga, sgb, cv_s = _smp1(
        xs2, mod_s, gmix, w_in_b, lb_logits, cw, cb, state_conv[0].reshape(n_smp, (CONV_K - 1) * CONV_W))
    s_s, o_s = _smp2(f, kk, q, v, state_hgrn[0])
    xmid_s, h2_all, lgt_all = _smp3(xs2, mod_s, o_s, gate, yb, sga, sgb, hg, gffn,
                                    w_oh_b, w_oc_b, w_o_b, wr_hi, wr_lo, w_sgu, w_sd,
                                    h2_all, lgt_all, n_prompt)

    idx, w_tok, rank, cnt = _route(lgt_all, router_bias[0].reshape(N_EXP, 1))

    bm = GMM_BM
    n_blocks = (n_tok * TOP_K + N_EXP * (bm - 1)) // bm
    n_rows = n_blocks * bm
    counts = cnt[:, 0]
    padded = (counts + bm - 1) // bm * bm
    pad_end = jnp.cumsum(padded)
    pad_start = pad_end - padded
    blk_row0 = jnp.arange(n_blocks, dtype=I32) * bm
    blk_exp = jnp.minimum(jnp.sum((pad_end[None, :] <= blk_row0[:, None]).astype(I32), axis=1), N_EXP - 1)
    n_used = (pad_end[-1:] // bm).astype(I32)
    dest = _dest(pad_start.astype(I32), idx, rank, n_rows)

    xs = _dispatch(h2_all.reshape(2 * n_tok, PLANE_W), dest, 2 * n_rows).reshape(2, n_rows, PLANE_W)
    ys = _gmm(blk_exp, n_used, xs, w_exp_gate[0], w_exp_up[0], w_exp_down[0])

    ys_flat = ys.reshape(2 * n_rows, PLANE_W)
    dest3 = dest.reshape(TOP_K, 2, n_tok)
    w_t = w_tok.T
    xmid_p2 = xmid_p.reshape(n_prompt, D_MODEL)
    g2_p = mod_p[:, 5 * D_MODEL:].reshape(bsz, 1, D_MODEL)
    g2_s = mod_s[:, 5 * D_MODEL:].reshape(1, n_smp, D_MODEL)
    chunk = n_prompt // FINAL_CHUNKS
    y_p = None
    for c in range(FINAL_CHUNKS):
        c0 = c * chunk
        c1 = n_tok if c == FINAL_CHUNKS - 1 else c0 + chunk
        z = _combine(ys_flat, dest3[:, :, c0:c1].reshape(-1)).reshape(TOP_K, 2, c1 - c0, PLANE_W)
        y_p = _final(xmid_p2, g2_p, z, w_t, gfin, FINAL_TILE, n_tiles=chunk // FINAL_TILE,
                     x_tile0=c0 // FINAL_TILE, z_tile0=0, w_tile0=c0 // FINAL_TILE,
                     tiles_per_g2=seq // FINAL_TILE, y_prev=y_p)
    y_s = _final(xmid_s, g2_s, z, w_t, gfin, n_smp, n_tiles=1, x_tile0=0, z_tile0=chunk // n_smp,
                 w_tile0=n_prompt // n_smp, tiles_per_g2=1)

    return (y_p.reshape(bsz, seq, D_MODEL), y_s.reshape(n_smp, 1, D_MODEL),
            s_p[None], cv_p[None], s_s[None], cv_s.reshape(1, n_smp, CONV_K - 1, CONV_W))
```

```python
import functools

import jax
import jax.numpy as jnp
from jax import lax
from jax.experimental import pallas as pl
from jax.experimental.pallas import tpu as pltpu
from jax.experimental.pallas import tpu_sc as plsc

F32 = jnp.float32
BF16 = jnp.bfloat16
I32 = jnp.int32

D_MODEL = 1024
HALF_D = D_MODEL // 2
HEADS = 4
DK = 128
KEY_W = HEADS * DK
CONV_W = 512
CONV_K = 3
IN_W = 2 * KEY_W + 2 * KEY_W + 3 * CONV_W + 2 * D_MODEL
N_EXP = 64
TOP_K = 8
N_GRP = 8
GRP_SZ = N_EXP // N_GRP
TOPK_GRP = 4
EXP_FF = 256
SH_FF = 256
ROUTED_SCALE = 2.5
EPS = 1e-6

C_Q, C_F, C_I, C_G = 0, 512, 1024, 1536
C_BB, C_CC, C_VB = 2048, 2560, 3072
C_MGA, C_MGB = 3584, 4608

MIX_TILE = 512
SUB = 128
HALF = SUB // 2
ROUTE_TILE = 384
GMM_BM = 256
FINAL_TILE = 256
FINAL_CHUNKS = 4
SMP_GROUP = 8
SC_WINDOW = 128
PLANE_W = HALF_D // 2
VMEM_LIMIT = 56 * 1024 * 1024


def _dot(a, b):
    return jnp.dot(a, b, preferred_element_type=F32)


def _dot_nt(a, b):
    return lax.dot_general(a, b, (((1,), (1,)), ((), ())), preferred_element_type=F32)


def _dot_tn(a, b):
    return lax.dot_general(a, b, (((0,), (0,)), ((), ())), preferred_element_type=F32)


def _sigmoid(x):
    return 0.5 * jnp.tanh(0.5 * x) + 0.5


def _silu(x):
    h = 0.5 * x
    return h * jnp.tanh(h) + h


def _rms(x):
    return x * lax.rsqrt(jnp.mean(x * x, axis=-1, keepdims=True) + EPS)


def _lower_bound(lbl):
    a, b = lbl[0:1], lbl[1:2]
    m = jnp.maximum(a, b)
    ea, eb = jnp.exp(a - m), jnp.exp(b - m)
    return ea / (ea + eb)


def _split3(x):
    hi = x.astype(BF16)
    r1 = x - hi.astype(F32)
    mid = r1.astype(BF16)
    lo = (r1 - mid.astype(F32)).astype(BF16)
    return hi, mid, lo


def _pack_rows(xb, out_ref):
    bits = lax.bitcast_convert_type(xb.astype(F32), I32)
    lo = lax.shift_right_logical(bits[:, :HALF_D], 16)
    hi = bits[:, HALF_D:] & jnp.int32(-65536)
    words = lo | hi
    out_ref[0] = words[:, :PLANE_W]
    out_ref[1] = words[:, PLANE_W:]


def _unpack_rows(p0, p1):
    def halves(w):
        lo = lax.bitcast_convert_type(lax.shift_left(w, 16), F32)
        hi = lax.bitcast_convert_type(w & jnp.int32(-65536), F32)
        return lo.astype(BF16), hi.astype(BF16)

    c0, c2 = halves(p0)
    c1, c3 = halves(p1)
    return c0, c1, c2, c3


def _ada_kernel(c_ref, w_ref, b_ref, o_ref):
    a = _silu(c_ref[...]).astype(BF16)
    o_ref[...] = _dot(a, w_ref[...].astype(BF16)) + b_ref[...]


def _ada(c_all, w_ada, b_ada):
    n = c_all.shape[0]
    blk = 1024
    return pl.pallas_call(
        _ada_kernel,
        grid=(6 * D_MODEL // blk,),
        in_specs=[pl.BlockSpec((n, D_MODEL), lambda j: (0, 0)),
                  pl.BlockSpec((D_MODEL, blk), lambda j: (0, j)),
                  pl.BlockSpec((1, blk), lambda j: (0, j))],
        out_specs=pl.BlockSpec((n, blk), lambda j: (0, j)),
        out_shape=jax.ShapeDtypeStruct((n, 6 * D_MODEL), F32),
        name="ada",
    )(c_all, w_ada, b_ada.reshape(1, -1))


def _ffn_pre(x1, mod_rows, gffn, w_sgu, w_sd, wr_hi, wr_lo):
    sh2, sc2, g2 = mod_rows
    h2 = _rms(x1) * gffn * (1.0 + sc2) + sh2
    h2b = h2.astype(BF16)
    gu = _dot(h2b, w_sgu)
    act = _silu(gu[:, :SH_FF]) * gu[:, SH_FF:]
    xmid = x1 + g2 * _dot(act.astype(BF16), w_sd)
    h2lo = (h2 - h2b.astype(F32)).astype(BF16)
    lgt = _dot_nt(wr_hi, h2b) + _dot_nt(wr_hi, h2lo) + _dot_nt(wr_lo, h2b)
    return xmid, h2b, lgt


def _mix_kernel(x_ref, mod_ref, gmix_ref, gffn_ref, w_in_ref, lbl_ref, hg_ref, cw_ref, cb_ref,
                w_oh_ref, w_oc_ref, w_o_ref, wr_hi_ref, wr_lo_ref, w_sgu_ref, w_sd_ref,
                h2_all_ref, lgt_all_ref,
                xmid_ref, h2_ref, lgt_ref, s_out_ref, cv_out_ref,
                proj_ref, st_ref, cbuf_ref, ya_ref):
    del h2_all_ref, lgt_all_ref
    t = pl.program_id(1)
    nt = pl.num_programs(1)
    tt = x_ref.shape[1]

    @pl.when(t == 0)
    def _():
        st_ref[...] = jnp.zeros_like(st_ref)
        cbuf_ref[...] = jnp.zeros_like(cbuf_ref)

    x = x_ref[0]
    mod = mod_ref[0]
    sh1, sc1, g1 = mod[0:1], mod[1:2], mod[2:3]
    h = _rms(x) * gmix_ref[...] * (1.0 + sc1) + sh1
    hb = h.astype(BF16)
    for c in range(0, IN_W, 512):
        proj_ref[:, c:c + 512] = _dot(hb, w_in_ref[:, c:c + 512])

    lb = _lower_bound(lbl_ref[...])
    row = lax.broadcasted_iota(I32, (SUB, SUB), 0)
    col = lax.broadcasted_iota(I32, (SUB, SUB), 1)
    tri = (col <= row).astype(BF16)
    mask_d = (col <= row) & ((row >= HALF) == (col >= HALF))
    top_half = lax.broadcasted_iota(I32, (SUB, DK), 0) < HALF

    for s in range(tt // SUB):
        r0 = s * SUB
        f = lb + (1.0 - lb) * _sigmoid(proj_ref[r0:r0 + SUB, C_F:C_F + KEY_W])
        kk = 1.0 - f
        hi, mid, lo = _split3(jnp.log(f))
        bc = _dot(tri, hi) + _dot(tri, mid) + _dot(tri, lo)
        for hd in range(HEADS):
            hs = slice(hd * DK, (hd + 1) * DK)
            bh = bc[:, hs]
            b31, b63 = bh[HALF // 2 - 1:HALF // 2], bh[HALF - 1:HALF]
            b95, b127 = bh[HALF + HALF // 2 - 1:HALF + HALF // 2], bh[SUB - 1:SUB]
            arg = bh - jnp.where(top_half, b31, b95)
            e_pos, e_neg = jnp.exp(arg), jnp.exp(-arg)
            q = _silu(proj_ref[r0:r0 + SUB, C_Q + hd * DK:C_Q + (hd + 1) * DK])
            v = proj_ref[r0:r0 + SUB, C_I + hd * DK:C_I + (hd + 1) * DK]
            qd = q * e_pos
            kd = kk[:, hs] * e_neg
            q_in = qd * jnp.where(top_half, jnp.exp(b31), jnp.exp(b95))
            k_end = kd * jnp.where(top_half, jnp.exp(b127 - b31), jnp.exp(b127 - b95))
            qa = jnp.where(top_half, 0.0, qd * jnp.exp(b95 - b63))
            ka = jnp.where(top_half, kd * jnp.exp(b63 - b31), 0.0)
            att = jnp.where(mask_d, _dot_nt(qd.astype(BF16), kd.astype(BF16)), 0.0)
            att = att + _dot_nt(qa.astype(BF16), ka.astype(BF16))
            vb = v.astype(BF16)
            st = st_ref[hd]
            o = _dot(att.astype(BF16), vb) + _dot_nt(q_in.astype(BF16), st.astype(BF16))
            st_ref[hd] = st * jnp.exp(b127) + _dot_tn(vb, k_end.astype(BF16))
            gate = _silu(proj_ref[r0:r0 + SUB, C_G + hd * DK:C_G + (hd + 1) * DK])
            ya_ref[r0:r0 + SUB, hs] = _rms(o) * hg_ref[:, hs] * gate

    u = proj_ref[:, C_CC:C_CC + CONV_W] * proj_ref[:, C_VB:C_VB + CONV_W]
    rows = lax.broadcasted_iota(I32, (tt, CONV_W), 0)
    c0, c1 = cbuf_ref[0:1], cbuf_ref[1:2]
    u1 = jnp.where(rows == 0, c1, pltpu.roll(u, 1, axis=0))
    u2 = jnp.where(rows == 0, c0, jnp.where(rows == 1, c1, pltpu.roll(u, 2, axis=0)))
    conv = cw_ref[0:1] * u2 + cw_ref[1:2] * u1 + cw_ref[2:3] * u + cb_ref[...]
    yb = proj_ref[:, C_BB:C_BB + CONV_W] * conv
    cbuf_ref[...] = u[tt - 2:tt]

    mixed = (_sigmoid(proj_ref[:, C_MGA:C_MGA + D_MODEL]) * _dot(ya_ref[...].astype(BF16), w_oh_ref[...])
             + _sigmoid(proj_ref[:, C_MGB:C_MGB + D_MODEL]) * _dot(yb.astype(BF16), w_oc_ref[...]))
    x1 = x + g1 * _dot(mixed.astype(BF16), w_o_ref[...])

    xmid, h2b, lgt = _ffn_pre(x1, (mod[3:4], mod[4:5], mod[5:6]), gffn_ref[...],
                              w_sgu_ref[...], w_sd_ref[...], wr_hi_ref[...], wr_lo_ref[...])
    xmid_ref[0] = xmid
    _pack_rows(h2b, h2_ref)
    lgt_ref[...] = lgt

    @pl.when(t == nt - 1)
    def _():
        for hd in range(HEADS):
            s_out_ref[0, hd] = st_ref[hd].T
        cv_out_ref[0] = cbuf_ref[...]


def _const_spec(shape):
    nd = len(shape)
    return pl.BlockSpec(shape, lambda b, t, _nd=nd: (0,) * _nd, pipeline_mode=pl.Buffered(1))


def _mix(x, mod, h2_all, lgt_all, gmix, gffn, w_in, lbl, hg, cw, cb, w_oh, w_oc, w_o, wr_hi, wr_lo,
         w_sgu, w_sd):
    bsz, seq, _ = x.shape
    n_tok = h2_all.shape[1]
    tt = MIX_TILE
    nt = seq // tt
    consts = [gmix, gffn, w_in, lbl, hg, cw, cb, w_oh, w_oc, w_o, wr_hi, wr_lo, w_sgu, w_sd]
    return pl.pallas_call(
        _mix_kernel,
        grid=(bsz, nt),
        in_specs=[pl.BlockSpec((1, tt, D_MODEL), lambda b, t: (b, t, 0)),
                  pl.BlockSpec((1, 6, D_MODEL), lambda b, t: (b, 0, 0))]
                 + [_const_spec(a.shape) for a in consts]
                 + [pl.BlockSpec(memory_space=pl.ANY), pl.BlockSpec(memory_space=pl.ANY)],
        input_output_aliases={2 + len(consts): 1, 3 + len(consts): 2},
        out_specs=[pl.BlockSpec((1, tt, D_MODEL), lambda b, t: (b, t, 0)),
                   pl.BlockSpec((2, tt, PLANE_W), lambda b, t: (0, b * nt + t, 0)),
                   pl.BlockSpec((N_EXP, tt), lambda b, t: (0, b * nt + t)),
                   pl.BlockSpec((1, HEADS, DK, DK), lambda b, t: (b, 0, 0, 0)),
                   pl.BlockSpec((1, CONV_K - 1, CONV_W), lambda b, t: (b, 0, 0))],
        out_shape=[jax.ShapeDtypeStruct((bsz, seq, D_MODEL), F32),
                   jax.ShapeDtypeStruct((2, n_tok, PLANE_W), I32),
                   jax.ShapeDtypeStruct((N_EXP, n_tok), F32),
                   jax.ShapeDtypeStruct((bsz, HEADS, DK, DK), F32),
                   jax.ShapeDtypeStruct((bsz, CONV_K - 1, CONV_W), F32)],
        scratch_shapes=[pltpu.VMEM((tt, IN_W), F32),
                        pltpu.VMEM((HEADS, DK, DK), F32),
                        pltpu.VMEM((CONV_K - 1, CONV_W), F32),
                        pltpu.VMEM((tt, KEY_W), F32)],
        compiler_params=pltpu.CompilerParams(
            dimension_semantics=("arbitrary", "arbitrary"), vmem_limit_bytes=VMEM_LIMIT),
        name="mix",
    )(x, mod, *consts, h2_all, lgt_all)


def _smp1_kernel(x_ref, mod_ref, gmix_ref, w_in_ref, lbl_ref, cw_ref, cb_ref, cst_ref,
                 f_ref, k_ref, q_ref, v_ref, gate_ref, yb_ref, sga_ref, sgb_ref, cv_out_ref):
    x = x_ref[...]
    sh1, sc1 = mod_ref[:, 0:D_MODEL], mod_ref[:, D_MODEL:2 * D_MODEL]
    h = _rms(x) * gmix_ref[...] * (1.0 + sc1) + sh1
    hb = h.astype(BF16)

    def proj(c, w):
        return _dot(hb, w_in_ref[:, c:c + w])

    lb = _lower_bound(lbl_ref[...])
    f = lb + (1.0 - lb) * _sigmoid(proj(C_F, KEY_W))
    f_ref[...] = f
    k_ref[...] = 1.0 - f
    q_ref[...] = _silu(proj(C_Q, KEY_W))
    v_ref[...] = proj(C_I, KEY_W)
    gate_ref[...] = _silu(proj(C_G, KEY_W))
    u = proj(C_CC, CONV_W) * proj(C_VB, CONV_W)
    c0, c1 = cst_ref[:, 0:CONV_W], cst_ref[:, CONV_W:2 * CONV_W]
    conv = cw_ref[0:1] * c0 + cw_ref[1:2] * c1 + cw_ref[2:3] * u + cb_ref[...]
    yb_ref[...] = proj(C_BB, CONV_W) * conv
    cv_out_ref[:, 0:CONV_W] = c1
    cv_out_ref[:, CONV_W:2 * CONV_W] = u
    sga_ref[...] = _sigmoid(proj(C_MGA, D_MODEL))
    sgb_ref[...] = _sigmoid(proj(C_MGB, D_MODEL))


def _smp1(x, mod, gmix, w_in, lbl, cw, cb, cst):
    n = x.shape[0]
    kw = jax.ShapeDtypeStruct((n, KEY_W), F32)
    dm = jax.ShapeDtypeStruct((n, D_MODEL), F32)
    return pl.pallas_call(
        _smp1_kernel,
        out_shape=[kw, kw, kw, kw, kw, kw, dm, dm,
                   jax.ShapeDtypeStruct((n, (CONV_K - 1) * CONV_W), F32)],
        compiler_params=pltpu.CompilerParams(vmem_limit_bytes=VMEM_LIMIT),
        name="smp1",
    )(x, mod, gmix, w_in, lbl, cw, cb, cst)


def _smp2_kernel(f_ref, k_ref, q_ref, v_ref, s_ref, s_out_ref, o_ref):
    g = f_ref.shape[0]
    for i in range(g):
        for hd in range(HEADS):
            hs = slice(hd * DK, (hd + 1) * DK)

            def col(ref):
                return jnp.broadcast_to(ref[i:i + 1, hs], (DK, DK)).T

            vrow = v_ref[i:i + 1, hs]
            s_new = col(f_ref) * s_ref[i, hd] + col(k_ref) * vrow
            s_out_ref[i, hd] = s_new
            o_ref[i:i + 1, hs] = jnp.sum(col(q_ref) * s_new, axis=0, keepdims=True)


def _smp2(f, k, q, v, state):
    n = f.shape[0]
    g = SMP_GROUP
    row_spec = pl.BlockSpec((g, KEY_W), lambda i: (i, 0))
    st_spec = pl.BlockSpec((g, HEADS, DK, DK), lambda i: (i, 0, 0, 0))
    return pl.pallas_call(
        _smp2_kernel,
        grid=(n // g,),
        in_specs=[row_spec, row_spec, row_spec, row_spec, st_spec],
        out_specs=[st_spec, row_spec],
        out_shape=[jax.ShapeDtypeStruct(state.shape, F32), jax.ShapeDtypeStruct((n, KEY_W), F32)],
        compiler_params=pltpu.CompilerParams(dimension_semantics=("arbitrary",)),
        name="smp2",
    )(f, k, q, v, state)


def _smp3_kernel(x_ref, mod_ref, o_ref, gate_ref, yb_ref, sga_ref, sgb_ref, hg_ref, gffn_ref,
                 w_oh_ref, w_oc_ref, w_o_ref, wr_hi_ref, wr_lo_ref, w_sgu_ref, w_sd_ref,
                 h2_all_ref, lgt_all_ref, xmid_ref, h2_ref, lgt_ref):
    del h2_all_ref, lgt_all_ref
    parts = []
    for hd in range(HEADS):
        hs = slice(hd * DK, (hd + 1) * DK)
        parts.append(_rms(o_ref[:, hs]) * hg_ref[:, hs] * gate_ref[:, hs])
    ya = jnp.concatenate(parts, axis=1)
    mixed = (sga_ref[...] * _dot(ya.astype(BF16), w_oh_ref[...])
             + sgb_ref[...] * _dot(yb_ref[...].astype(BF16), w_oc_ref[...]))
    g1 = mod_ref[:, 2 * D_MODEL:3 * D_MODEL]
    x1 = x_ref[...] + g1 * _dot(mixed.astype(BF16), w_o_ref[...])
    mod_rows = tuple(mod_ref[:, j * D_MODEL:(j + 1) * D_MODEL] for j in (3, 4, 5))
    xmid, h2b, lgt = _ffn_pre(x1, mod_rows, gffn_ref[...], w_sgu_ref[...], w_sd_ref[...],
                              wr_hi_ref[...], wr_lo_ref[...])
    xmid_ref[...] = xmid
    _pack_rows(h2b, h2_ref)
    lgt_ref[...] = lgt


def _smp3(x, mod, o, gate, yb, sga, sgb, hg, gffn, w_oh, w_oc, w_o, wr_hi, wr_lo, w_sgu, w_sd,
          h2_all, lgt_all, n_prompt):
    n = x.shape[0]
    vmem_args = [x, mod, o, gate, yb, sga, sgb, hg, gffn, w_oh, w_oc, w_o, wr_hi, wr_lo, w_sgu, w_sd]
    blk = n_prompt // n

    def full(a):
        nd = a.ndim
        return pl.BlockSpec(a.shape, lambda i, _nd=nd: (0,) * _nd)

    return pl.pallas_call(
        _smp3_kernel,
        grid=(1,),
        in_specs=[full(a) for a in vmem_args]
                 + [pl.BlockSpec(memory_space=pl.ANY), pl.BlockSpec(memory_space=pl.ANY)],
        out_specs=[pl.BlockSpec((n, D_MODEL), lambda i: (0, 0)),
                   pl.BlockSpec((2, n, PLANE_W), lambda i: (0, blk, 0)),
                   pl.BlockSpec((N_EXP, n), lambda i: (0, blk))],
        out_shape=[jax.ShapeDtypeStruct((n, D_MODEL), F32),
                   jax.ShapeDtypeStruct(h2_all.shape, h2_all.dtype),
                   jax.ShapeDtypeStruct(lgt_all.shape, lgt_all.dtype)],
        input_output_aliases={len(vmem_args): 1, len(vmem_args) + 1: 2},
        compiler_params=pltpu.CompilerParams(
            dimension_semantics=("arbitrary",), vmem_limit_bytes=VMEM_LIMIT),
        name="smp3",
    )(*vmem_args, h2_all, lgt_all)


def _route_kernel(lgt_ref, bias_ref, idx_ref, w_ref, rank_ref, cnt_ref, carry_ref):
    i = pl.program_id(0)
    tr = lgt_ref.shape[1]
    neg = -jnp.inf

    @pl.when(i == 0)
    def _():
        carry_ref[...] = jnp.zeros_like(carry_ref)

    scores = _sigmoid(lgt_ref[...])
    sel = scores + bias_ref[...]
    j8 = lax.broadcasted_iota(I32, (GRP_SZ, tr), 0)
    groups = [sel[g * GRP_SZ:(g + 1) * GRP_SZ] for g in range(N_GRP)]
    gscore = []
    for grp in groups:
        m1 = jnp.max(grp, axis=0, keepdims=True)
        i1 = jnp.min(jnp.where(grp == m1, j8, GRP_SZ), axis=0, keepdims=True)
        m2 = jnp.max(jnp.where(j8 == i1, neg, grp), axis=0, keepdims=True)
        gscore.append(m1 + m2)
    kept = []
    for g in range(N_GRP):
        beaten = jnp.zeros((1, tr), I32)
        for o in range(N_GRP):
            if o < g:
                beaten = beaten + (gscore[o] >= gscore[g]).astype(I32)
            elif o > g:
                beaten = beaten + (gscore[o] > gscore[g]).astype(I32)
        kept.append(jnp.where(beaten < TOPK_GRP, groups[g], neg))
    masked = jnp.concatenate(kept, axis=0)
    ei = lax.broadcasted_iota(I32, masked.shape, 0)
    chosen = jnp.zeros(masked.shape, jnp.bool_)
    picks, weights = [], []
    for _ in range(TOP_K):
        m = jnp.max(masked, axis=0, keepdims=True)
        pick = jnp.min(jnp.where(masked == m, ei, N_EXP), axis=0, keepdims=True)
        hit = ei == pick
        weights.append(jnp.sum(jnp.where(hit, scores, 0.0), axis=0, keepdims=True))
        picks.append(pick)
        chosen = chosen | hit
        masked = jnp.where(hit, neg, masked)
    wsum = weights[0]
    for w in weights[1:]:
        wsum = wsum + w
    sel01 = chosen.astype(F32)
    r = lax.broadcasted_iota(I32, (tr, tr), 0)
    c = lax.broadcasted_iota(I32, (tr, tr), 1)
    before = (r < c).astype(BF16)
    cnt = _dot(sel01.astype(BF16), before) + carry_ref[...]
    for k in range(TOP_K):
        idx_ref[k:k + 1, :] = picks[k]
        w_ref[k:k + 1, :] = weights[k] / wsum * ROUTED_SCALE
        rank = jnp.sum(jnp.where(ei == picks[k], cnt, 0.0), axis=0, keepdims=True)
        rank_ref[k:k + 1, :] = rank.astype(I32)
    carry_ref[...] = carry_ref[...] + jnp.sum(sel01, axis=1, keepdims=True)
    cnt_ref[...] = jnp.broadcast_to(carry_ref[...], cnt_ref.shape).astype(I32)


def _route(lgt, bias):
    n = lgt.shape[1]
    tr = ROUTE_TILE
    slot = lambda dt: jax.ShapeDtypeStruct((TOP_K, n), dt)
    slot_spec = pl.BlockSpec((TOP_K, tr), lambda i: (0, i))
    return pl.pallas_call(
        _route_kernel,
        grid=(n // tr,),
        in_specs=[pl.BlockSpec((N_EXP, tr), lambda i: (0, i)),
                  pl.BlockSpec((N_EXP, 1), lambda i: (0, 0))],
        out_specs=[slot_spec, slot_spec, slot_spec, pl.BlockSpec((N_EXP, 128), lambda i: (0, 0))],
        out_shape=[slot(I32), slot(F32), slot(I32),
                   jax.ShapeDtypeStruct((N_EXP, 128), I32)],
        scratch_shapes=[pltpu.VMEM((N_EXP, 1), F32)],
        compiler_params=pltpu.CompilerParams(dimension_semantics=("arbitrary",)),
        name="route",
    )(lgt, bias)


def _dest_kernel(start_ref, idx_ref, rank_ref, dest_ref, *, n_rows):
    n_tok = idx_ref.shape[1]
    idx = idx_ref[...]
    acc = rank_ref[...]
    for e in range(N_EXP):
        acc = acc + jnp.where(idx == e, start_ref[e], 0)
    dest_ref[:, 0:n_tok] = acc
    dest_ref[:, n_tok:2 * n_tok] = acc + n_rows


def _dest(pad_start, idx, rank, n_rows):
    k, n_tok = idx.shape
    return pl.pallas_call(
        functools.partial(_dest_kernel, n_rows=n_rows),
        in_specs=[pl.BlockSpec(memory_space=pltpu.SMEM),
                  pl.BlockSpec(memory_space=pltpu.VMEM),
                  pl.BlockSpec(memory_space=pltpu.VMEM)],
        out_specs=pl.BlockSpec(memory_space=pltpu.VMEM),
        out_shape=jax.ShapeDtypeStruct((k, 2 * n_tok), I32),
        name="dest",
    )(pad_start, idx, rank)


def _sc_mesh():
    return plsc.VectorSubcoreMesh(core_axis_name="core", subcore_axis_name="subcore")


def _dispatch(rows, dest, n_out):
    n, width = rows.shape
    win = SC_WINDOW
    steps = n // win

    @pl.kernel(out_type=jax.ShapeDtypeStruct((n_out, width), rows.dtype), mesh=_sc_mesh(),
               scratch_types=[], name="dispatch")
    def run(x_hbm, *refs):
        i_hbms, o_hbm = refs[:TOP_K], refs[TOP_K]

        def body(x_vmem, *i_vmems):
            for i_vmem in i_vmems:
                pltpu.sync_copy(x_vmem, o_hbm.at[i_vmem.at[0]])

        pltpu.emit_pipeline(
            body,
            grid=(steps,),
            in_specs=[pl.BlockSpec((win, width), lambda i: (i, 0))]
                     + [pl.BlockSpec((1, win), lambda i, k=k: (0, k * steps + i)) for k in range(TOP_K)],
            out_specs=[],
            core_axis_name=("core", "subcore"),
            dimension_semantics=(pltpu.PARALLEL,),
        )(x_hbm, *i_hbms)

    dest_flat = dest.reshape(1, TOP_K * n)
    return run(rows, *([dest_flat] * TOP_K))


def _combine(rows, dest_flat):
    width = rows.shape[1]
    n = dest_flat.shape[0]
    win = SC_WINDOW

    @pl.kernel(out_type=jax.ShapeDtypeStruct((n, width), rows.dtype), mesh=_sc_mesh(),
               scratch_types=[], name="combine")
    def run(y_hbm, i_hbm, o_hbm):
        def body(i_vmem, o_vmem):
            pltpu.sync_copy(y_hbm.at[i_vmem.at[0]], o_vmem)

        pltpu.emit_pipeline(
            body,
            grid=(n // win,),
            in_specs=[pl.BlockSpec((1, win), lambda i: (0, i))],
            out_specs=[pl.BlockSpec((win, width), lambda i: (i, 0))],
            core_axis_name=("core", "subcore"),
            dimension_semantics=(pltpu.PARALLEL,),
        )(i_hbm, o_hbm)

    return run(rows, dest_flat.reshape(1, n))


def _gmm_kernel(blk_exp_ref, n_used_ref, xs_hbm, wg_hbm, wu_hbm, wd_hbm, ys_hbm,
                xbuf, ybuf, wg32, wu32, wd32, wgu_b, wd_b, xsem, ysem, wsem, run_ref):
    bm = xbuf.shape[2]
    n_used = n_used_ref[0]

    def x_copy(b, slot):
        return pltpu.make_async_copy(xs_hbm.at[:, pl.ds(b * bm, bm), :], xbuf.at[slot], xsem.at[slot])

    def y_copy(b, slot):
        return pltpu.make_async_copy(ybuf.at[slot], ys_hbm.at[:, pl.ds(b * bm, bm), :], ysem.at[slot])

    def w_copies(e, slot):
        return (pltpu.make_async_copy(wg_hbm.at[e], wg32.at[slot], wsem.at[slot, 0]),
                pltpu.make_async_copy(wu_hbm.at[e], wu32.at[slot], wsem.at[slot, 1]),
                pltpu.make_async_copy(wd_hbm.at[e], wd32.at[slot], wsem.at[slot, 2]))

    def run_end(b):
        return lax.while_loop(lambda j: (j < n_used) & (blk_exp_ref[jnp.minimum(j, n_used - 1)] == blk_exp_ref[b]),
                              lambda j: j + 1, b + 1)

    run_ref[0] = 0
    x_copy(0, 0).start()
    for c in w_copies(blk_exp_ref[0], 0):
        c.start()

    def block(b, carry):
        slot = b % 2

        @pl.when(b + 1 < n_used)
        def _():
            x_copy(b + 1, 1 - slot).start()

        @pl.when((b == 0) | (blk_exp_ref[b] != blk_exp_ref[jnp.maximum(b - 1, 0)]))
        def _():
            wslot = run_ref[0] % 2
            run_ref[0] = run_ref[0] + 1
            for c in w_copies(blk_exp_ref[b], wslot):
                c.wait()
            wgu_b[:, 0:EXP_FF] = wg32[wslot].astype(BF16)
            wgu_b[:, EXP_FF:2 * EXP_FF] = wu32[wslot].astype(BF16)
            wd_b[...] = wd32[wslot].astype(BF16)
            nxt = run_end(b)

            @pl.when(nxt < n_used)
            def _():
                for c in w_copies(blk_exp_ref[jnp.minimum(nxt, n_used - 1)], 1 - wslot):
                    c.start()

        x_copy(b, slot).wait()

        @pl.when(b >= 2)
        def _():
            y_copy(b - 2, slot).wait()

        xc = _unpack_rows(xbuf[slot, 0], xbuf[slot, 1])
        gu = sum(_dot(c, wgu_b[i * PLANE_W:(i + 1) * PLANE_W, :]) for i, c in enumerate(xc))
        act = (_silu(gu[:, :EXP_FF]) * gu[:, EXP_FF:]).astype(BF16)
        _pack_rows(_dot(act, wd_b[...]).astype(BF16), ybuf.at[slot])
        y_copy(b, slot).start()
        return carry

    lax.fori_loop(0, n_used, block, 0)

    @pl.when(n_used >= 2)
    def _():
        y_copy(n_used - 2, n_used % 2).wait()

    y_copy(n_used - 1, (n_used - 1) % 2).wait()


def _gmm(blk_exp, n_used, xs, w_gate, w_up, w_down):
    n_rows = xs.shape[1]
    bm = GMM_BM
    nb = n_rows // bm

    assert blk_exp.shape == (nb,)
    any_spec = pl.BlockSpec(memory_space=pl.ANY)
    grid_spec = pltpu.PrefetchScalarGridSpec(
        num_scalar_prefetch=2,
        grid=(1,),
        in_specs=[any_spec, any_spec, any_spec, any_spec],
        out_specs=any_spec,
        scratch_shapes=[pltpu.VMEM((2, 2, bm, PLANE_W), I32), pltpu.VMEM((2, 2, bm, PLANE_W), I32),
                        pltpu.VMEM((2, D_MODEL, EXP_FF), F32), pltpu.VMEM((2, D_MODEL, EXP_FF), F32),
                        pltpu.VMEM((2, EXP_FF, D_MODEL), F32),
                        pltpu.VMEM((D_MODEL, 2 * EXP_FF), BF16), pltpu.VMEM((EXP_FF, D_MODEL), BF16),
                        pltpu.SemaphoreType.DMA((2,)), pltpu.SemaphoreType.DMA((2,)),
                        pltpu.SemaphoreType.DMA((2, 3)), pltpu.SMEM((1,), I32)],
    )
    return pl.pallas_call(
        _gmm_kernel,
        grid_spec=grid_spec,
        out_shape=jax.ShapeDtypeStruct((2, n_rows, PLANE_W), I32),
        compiler_params=pltpu.CompilerParams(dimension_semantics=("arbitrary",)),
        name="gmm",
    )(blk_exp, n_used, xs, w_gate, w_up, w_down)


def _final_kernel(xmid_ref, g2_ref, z_ref, w_ref, gfin_ref, *rest):
    y_ref = rest[-1]
    accs = [jnp.zeros((xmid_ref.shape[0], PLANE_W), F32) for _ in range(4)]
    for k in range(TOP_K):
        wk = w_ref[:, k:k + 1]
        cols = _unpack_rows(z_ref[k, 0], z_ref[k, 1])
        accs = [a + wk * c.astype(F32) for a, c in zip(accs, cols)]
    acc = jnp.concatenate(accs, axis=1)
    y_ref[...] = _rms(xmid_ref[...] + g2_ref[0] * acc) * gfin_ref[...]


def _final(xmid, g2, z, w_t, gfin, tile, *, n_tiles, x_tile0, z_tile0, w_tile0, tiles_per_g2, y_prev=None):
    args = [xmid, g2, z, w_t, gfin]
    in_specs = [pl.BlockSpec((tile, D_MODEL), lambda i: (x_tile0 + i, 0)),
                pl.BlockSpec((1, g2.shape[1], D_MODEL), lambda i: ((x_tile0 + i) // tiles_per_g2, 0, 0)),
                pl.BlockSpec((TOP_K, 2, tile, PLANE_W), lambda i: (0, 0, z_tile0 + i, 0)),
                pl.BlockSpec((tile, TOP_K), lambda i: (w_tile0 + i, 0)),
                pl.BlockSpec((1, D_MODEL), lambda i: (0, 0))]
    aliases = {}
    if y_prev is not None:
        args.append(y_prev)
        in_specs.append(pl.BlockSpec(memory_space=pl.ANY))
        aliases = {len(args) - 1: 0}
    return pl.pallas_call(
        _final_kernel,
        grid=(n_tiles,),
        in_specs=in_specs,
        out_specs=pl.BlockSpec((tile, D_MODEL), lambda i: (x_tile0 + i, 0)),
        out_shape=jax.ShapeDtypeStruct(xmid.shape, F32),
        input_output_aliases=aliases,
        compiler_params=pltpu.CompilerParams(dimension_semantics=("arbitrary",)),
        name="final",
    )(*args)


def kernel(x_prompt, x_sample, state_hgrn, state_conv, c_prompt, c_sample, w_ada, b_ada, norm_mix_g, norm_ffn_g, w_in, lb_logits, hgrn_norm_g, conv_w, conv_b, w_out_hgrn, w_out_conv, w_o, w_router, router_bias, w_exp_gate, w_exp_up, w_exp_down, w_sh_gate, w_sh_up, w_sh_down, final_norm_g):
    assert w_ada.shape[0] == 1 and lb_logits.shape[0] == 2
    bsz, seq, _ = x_prompt.shape
    n_smp = x_sample.shape[0]
    n_prompt = bsz * seq
    n_tok = n_prompt + n_smp

    w_in_b = w_in[0].astype(BF16)
    w_oh_b = w_out_hgrn[0].astype(BF16)
    w_oc_b = w_out_conv[0].astype(BF16)
    w_o_b = w_o[0].astype(BF16)
    wr_t = w_router[0].T
    wr_hi = wr_t.astype(BF16)
    wr_lo = (wr_t - wr_hi.astype(F32)).astype(BF16)
    w_sgu = jnp.concatenate([w_sh_gate[0], w_sh_up[0]], axis=1).astype(BF16)
    w_sd = w_sh_down[0].astype(BF16)
    gmix = norm_mix_g[0].reshape(1, D_MODEL)
    gffn = norm_ffn_g[0].reshape(1, D_MODEL)
    hg = hgrn_norm_g[0].reshape(1, KEY_W)
    cw = conv_w[0]
    cb = conv_b[0].reshape(1, CONV_W)
    gfin = final_norm_g.reshape(1, D_MODEL)

    mod = _ada(jnp.concatenate([c_prompt, c_sample], axis=0), w_ada[0], b_ada[0])
    mod_p, mod_s = mod[:bsz], mod[bsz:]

    xmid_p, h2_all, lgt_all, s_p, cv_p = _mix(
        x_prompt, mod_p.reshape(bsz, 6, D_MODEL),
        jnp.zeros((2, n_tok, PLANE_W), I32), jnp.zeros((N_EXP, n_tok), F32),
        gmix, gffn, w_in_b, lb_logits, hg, cw, cb, w_oh_b, w_oc_b, w_o_b, wr_hi, wr_lo, w_sgu, w_sd)

    xs2 = x_sample.reshape(n_smp, D_MODEL)
    f, kk, q, v, gate, yb, sga, sgb, cv_s = _smp1(
        xs2, mod_s, gmix, w_in_b, lb_logits, cw, cb, state_conv[0].reshape(n_smp, (CONV_K - 1) * CONV_W))
    s_s, o_s = _smp2(f, kk, q, v, state_hgrn[0])
    xmid_s, h2_all, lgt_all = _smp3(xs2, mod_s, o_s, gate, yb, sga, sgb, hg, gffn,
                                    w_oh_b, w_oc_b, w_o_b, wr_hi, wr_lo, w_sgu, w_sd,
                                    h2_all, lgt_all, n_prompt)

    idx, w_tok, rank, cnt = _route(lgt_all, router_bias[0].reshape(N_EXP, 1))

    bm = GMM_BM
    n_blocks = (n_tok * TOP_K + N_EXP * (bm - 1)) // bm
    n_rows = n_blocks * bm
    counts = cnt[:, 0]
    padded = (counts + bm - 1) // bm * bm
    pad_end = jnp.cumsum(padded)
    pad_start = pad_end - padded
    blk_row0 = jnp.arange(n_blocks, dtype=I32) * bm
    blk_exp = jnp.minimum(jnp.sum((pad_end[None, :] <= blk_row0[:, None]).astype(I32), axis=1), N_EXP - 1)
    n_used = (pad_end[-1:] // bm).astype(I32)
    dest = _dest(pad_start.astype(I32), idx, rank, n_rows)

    xs = _dispatch(h2_all.reshape(2 * n_tok, PLANE_W), dest, 2 * n_rows).reshape(2, n_rows, PLANE_W)
    ys = _gmm(blk_exp, n_used, xs, w_exp_gate[0], w_exp_up[0], w_exp_down[0])

    ys_flat = ys.reshape(2 * n_rows, PLANE_W)
    dest3 = dest.reshape(TOP_K, 2, n_tok)
    w_t = w_tok.T
    xmid_p2 = xmid_p.reshape(n_prompt, D_MODEL)
    g2_p = mod_p[:, 5 * D_MODEL:].reshape(bsz, 1, D_MODEL)
    g2_s = mod_s[:, 5 * D_MODEL:].reshape(1, n_smp, D_MODEL)
    chunk = n_prompt // FINAL_CHUNKS
    y_p = None
    for c in range(FINAL_CHUNKS):
        c0 = c * chunk
        c1 = n_tok if c == FINAL_CHUNKS - 1 else c0 + chunk
        z = _combine(ys_flat, dest3[:, :, c0:c1].reshape(-1)).reshape(TOP_K, 2, c1 - c0, PLANE_W)
        y_p = _final(xmid_p2, g2_p, z, w_t, gfin, FINAL_TILE, n_tiles=chunk // FINAL_TILE,
                     x_tile0=c0 // FINAL_TILE, z_tile0=0, w_tile0=c0 // FINAL_TILE,
                     tiles_per_g2=seq // FINAL_TILE, y_prev=y_p)
    y_s = _final(xmid_s, g2_s, z, w_t, gfin, n_smp, n_tiles=1, x_tile0=0, z_tile0=chunk // n_smp,
                 w_tile0=n_prompt // n_smp, tiles_per_g2=1)

    return (y_p.reshape(bsz, seq, D_MODEL), y_s.reshape(n_smp, 1, D_MODEL),
            s_p[None], cv_p[None], s_s[None], cv_s.reshape(1, n_smp, CONV_K - 1, CONV_W))
```

```python
import functools

import jax
import jax.numpy as jnp
from jax import lax
from jax.experimental import pallas as pl
from jax.experimental.pallas import tpu as pltpu
from jax.experimental.pallas import tpu_sc as plsc

F32 = jnp.float32
BF16 = jnp.bfloat16
I32 = jnp.int32

D_MODEL = 1024
HALF_D = D_MODEL // 2
HEADS = 4
DK = 128
KEY_W = HEADS * DK
CONV_W = 512
CONV_K = 3
IN_W = 2 * KEY_W + 2 * KEY_W + 3 * CONV_W + 2 * D_MODEL
N_EXP = 64
TOP_K = 8
N_GRP = 8
GRP_SZ = N_EXP // N_GRP
TOPK_GRP = 4
EXP_FF = 256
SH_FF = 256
ROUTED_SCALE = 2.5
EPS = 1e-6

C_Q, C_F, C_I, C_G = 0, 512, 1024, 1536
C_BB, C_CC, C_VB = 2048, 2560, 3072
C_MGA, C_MGB = 3584, 4608

MIX_TILE = 512
SUB = 128
HALF = SUB // 2
ROUTE_TILE = 384
GMM_BM = 512
FINAL_TILE = 256
FINAL_CHUNKS = 4
SMP_GROUP = 8
SC_WINDOW = 128
PLANE_W = HALF_D // 2
VMEM_LIMIT = 56 * 1024 * 1024


def _dot(a, b):
    return jnp.dot(a, b, preferred_element_type=F32)


def _dot_nt(a, b):
    return lax.dot_general(a, b, (((1,), (1,)), ((), ())), preferred_element_type=F32)


def _dot_tn(a, b):
    return lax.dot_general(a, b, (((0,), (0,)), ((), ())), preferred_element_type=F32)


def _sigmoid(x):
    return 0.5 * jnp.tanh(0.5 * x) + 0.5


def _silu(x):
    h = 0.5 * x
    return h * jnp.tanh(h) + h


def _rms(x):
    return x * lax.rsqrt(jnp.mean(x * x, axis=-1, keepdims=True) + EPS)


def _lower_bound(lbl):
    a, b = lbl[0:1], lbl[1:2]
    m = jnp.maximum(a, b)
    ea, eb = jnp.exp(a - m), jnp.exp(b - m)
    return ea / (ea + eb)


def _split3(x):
    hi = x.astype(BF16)
    r1 = x - hi.astype(F32)
    mid = r1.astype(BF16)
    lo = (r1 - mid.astype(F32)).astype(BF16)
    return hi, mid, lo


def _pack_rows(xb, out_ref):
    bits = lax.bitcast_convert_type(xb.astype(F32), I32)
    lo = lax.shift_right_logical(bits[:, :HALF_D], 16)
    hi = bits[:, HALF_D:] & jnp.int32(-65536)
    words = lo | hi
    out_ref[0] = words[:, :PLANE_W]
    out_ref[1] = words[:, PLANE_W:]


def _unpack_rows(p0, p1):
    def halves(w):
        lo = lax.bitcast_convert_type(lax.shift_left(w, 16), F32)
        hi = lax.bitcast_convert_type(w & jnp.int32(-65536), F32)
        return lo.astype(BF16), hi.astype(BF16)

    c0, c2 = halves(p0)
    c1, c3 = halves(p1)
    return c0, c1, c2, c3


def _ada_kernel(c_ref, w_ref, b_ref, o_ref):
    a = _silu(c_ref[...]).astype(BF16)
    o_ref[...] = _dot(a, w_ref[...].astype(BF16)) + b_ref[...]


def _ada(c_all, w_ada, b_ada):
    n = c_all.shape[0]
    blk = 1024
    return pl.pallas_call(
        _ada_kernel,
        grid=(6 * D_MODEL // blk,),
        in_specs=[pl.BlockSpec((n, D_MODEL), lambda j: (0, 0)),
                  pl.BlockSpec((D_MODEL, blk), lambda j: (0, j)),
                  pl.BlockSpec((1, blk), lambda j: (0, j))],
        out_specs=pl.BlockSpec((n, blk), lambda j: (0, j)),
        out_shape=jax.ShapeDtypeStruct((n, 6 * D_MODEL), F32),
        name="ada",
    )(c_all, w_ada, b_ada.reshape(1, -1))


def _ffn_pre(x1, mod_rows, gffn, w_sgu, w_sd, wr_hl):
    sh2, sc2, g2 = mod_rows
    h2 = _rms(x1) * gffn * (1.0 + sc2) + sh2
    h2b = h2.astype(BF16)
    gu = _dot(h2b, w_sgu)
    act = _silu(gu[:, :SH_FF]) * gu[:, SH_FF:]
    xmid = x1 + g2 * _dot(act.astype(BF16), w_sd)
    h2lo = (h2 - h2b.astype(F32)).astype(BF16)
    both = _dot_nt(wr_hl, h2b)
    lgt = both[:N_EXP] + both[N_EXP:] + _dot_nt(wr_hl[:N_EXP], h2lo)
    return xmid, h2b, lgt


def _mix_kernel(x_ref, mod_ref, gmix_ref, gffn_ref, w_in_ref, lbl_ref, hg_ref, cw_ref, cb_ref,
                w_oh_ref, w_oc_ref, w_o_ref, wr_hl_ref, w_sgu_ref, w_sd_ref,
                h2_all_ref, lgt_all_ref,
                xmid_ref, h2_ref, lgt_ref, s_out_ref, cv_out_ref,
                proj_ref, st_ref, cbuf_ref, ya_ref):
    del h2_all_ref, lgt_all_ref
    t = pl.program_id(1)
    nt = pl.num_programs(1)
    tt = x_ref.shape[1]

    @pl.when(t == 0)
    def _():
        st_ref[...] = jnp.zeros_like(st_ref)
        cbuf_ref[...] = jnp.zeros_like(cbuf_ref)

    x = x_ref[0]
    mod = mod_ref[0]
    sh1, sc1, g1 = mod[0:1], mod[1:2], mod[2:3]
    h = _rms(x) * gmix_ref[...] * (1.0 + sc1) + sh1
    hb = h.astype(BF16)
    for c in range(0, IN_W, 512):
        proj_ref[:, c:c + 512] = _dot(hb, w_in_ref[:, c:c + 512])

    lb = _lower_bound(lbl_ref[...])
    row = lax.broadcasted_iota(I32, (SUB, SUB), 0)
    col = lax.broadcasted_iota(I32, (SUB, SUB), 1)
    tri = (col <= row).astype(BF16)
    mask_d = (col <= row) & ((row >= HALF) == (col >= HALF))
    top_half = lax.broadcasted_iota(I32, (SUB, DK), 0) < HALF

    for s in range(tt // SUB):
        r0 = s * SUB
        f = lb + (1.0 - lb) * _sigmoid(proj_ref[r0:r0 + SUB, C_F:C_F + KEY_W])
        kk = 1.0 - f
        hi, mid, lo = _split3(jnp.log(f))
        bc = _dot(tri, hi) + _dot(tri, mid) + _dot(tri, lo)
        for hd in range(HEADS):
            hs = slice(hd * DK, (hd + 1) * DK)
            bh = bc[:, hs]
            b31, b63 = bh[HALF // 2 - 1:HALF // 2], bh[HALF - 1:HALF]
            b95, b127 = bh[HALF + HALF // 2 - 1:HALF + HALF // 2], bh[SUB - 1:SUB]
            arg = bh - jnp.where(top_half, b31, b95)
            e_pos, e_neg = jnp.exp(arg), jnp.exp(-arg)
            q = _silu(proj_ref[r0:r0 + SUB, C_Q + hd * DK:C_Q + (hd + 1) * DK])
            v = proj_ref[r0:r0 + SUB, C_I + hd * DK:C_I + (hd + 1) * DK]
            qd = q * e_pos
            kd = kk[:, hs] * e_neg
            q_in = qd * jnp.where(top_half, jnp.exp(b31), jnp.exp(b95))
            k_end = kd * jnp.where(top_half, jnp.exp(b127 - b31), jnp.exp(b127 - b95))
            qa = jnp.where(top_half, 0.0, qd * jnp.exp(b95 - b63))
            ka = jnp.where(top_half, kd * jnp.exp(b63 - b31), 0.0)
            att = jnp.where(mask_d, _dot_nt(qd.astype(BF16), kd.astype(BF16)), 0.0)
            att = att + _dot_nt(qa.astype(BF16), ka.astype(BF16))
            vb = v.astype(BF16)
            st = st_ref[hd]
            o = _dot(att.astype(BF16), vb) + _dot_nt(q_in.astype(BF16), st.astype(BF16))
            st_ref[hd] = st * jnp.exp(b127) + _dot_tn(vb, k_end.astype(BF16))
            gate = _silu(proj_ref[r0:r0 + SUB, C_G + hd * DK:C_G + (hd + 1) * DK])
            ya_ref[r0:r0 + SUB, hs] = _rms(o) * hg_ref[:, hs] * gate

    u = proj_ref[:, C_CC:C_CC + CONV_W] * proj_ref[:, C_VB:C_VB + CONV_W]
    rows = lax.broadcasted_iota(I32, (tt, CONV_W), 0)
    c0, c1 = cbuf_ref[0:1], cbuf_ref[1:2]
    u1 = jnp.where(rows == 0, c1, pltpu.roll(u, 1, axis=0))
    u2 = jnp.where(rows == 0, c0, jnp.where(rows == 1, c1, pltpu.roll(u, 2, axis=0)))
    conv = cw_ref[0:1] * u2 + cw_ref[1:2] * u1 + cw_ref[2:3] * u + cb_ref[...]
    yb = proj_ref[:, C_BB:C_BB + CONV_W] * conv
    cbuf_ref[...] = u[tt - 2:tt]

    mixed = (_sigmoid(proj_ref[:, C_MGA:C_MGA + D_MODEL]) * _dot(ya_ref[...].astype(BF16), w_oh_ref[...])
             + _sigmoid(proj_ref[:, C_MGB:C_MGB + D_MODEL]) * _dot(yb.astype(BF16), w_oc_ref[...]))
    x1 = x + g1 * _dot(mixed.astype(BF16), w_o_ref[...])

    xmid, h2b, lgt = _ffn_pre(x1, (mod[3:4], mod[4:5], mod[5:6]), gffn_ref[...],
                              w_sgu_ref[...], w_sd_ref[...], wr_hl_ref[...])
    xmid_ref[0] = xmid
    _pack_rows(h2b, h2_ref)
    lgt_ref[...] = lgt

    @pl.when(t == nt - 1)
    def _():
        for hd in range(HEADS):
            s_out_ref[0, hd] = st_ref[hd].T
        cv_out_ref[0] = cbuf_ref[...]


def _const_spec(shape):
    nd = len(shape)
    return pl.BlockSpec(shape, lambda b, t, _nd=nd: (0,) * _nd, pipeline_mode=pl.Buffered(1))


def _mix(x, mod, h2_all, lgt_all, gmix, gffn, w_in, lbl, hg, cw, cb, w_oh, w_oc, w_o, wr_hl, w_sgu, w_sd):
    bsz, seq, _ = x.shape
    n_tok = h2_all.shape[1]
    tt = MIX_TILE
    nt = seq // tt
    consts = [gmix, gffn, w_in, lbl, hg, cw, cb, w_oh, w_oc, w_o, wr_hl, w_sgu, w_sd]
    return pl.pallas_call(
        _mix_kernel,
        grid=(bsz, nt),
        in_specs=[pl.BlockSpec((1, tt, D_MODEL), lambda b, t: (b, t, 0)),
                  pl.BlockSpec((1, 6, D_MODEL), lambda b, t: (b, 0, 0))]
                 + [_const_spec(a.shape) for a in consts]
                 + [pl.BlockSpec(memory_space=pl.ANY), pl.BlockSpec(memory_space=pl.ANY)],
        input_output_aliases={2 + len(consts): 1, 3 + len(consts): 2},
        out_specs=[pl.BlockSpec((1, tt, D_MODEL), lambda b, t: (b, t, 0)),
                   pl.BlockSpec((2, tt, PLANE_W), lambda b, t: (0, b * nt + t, 0)),
                   pl.BlockSpec((N_EXP, tt), lambda b, t: (0, b * nt + t)),
                   pl.BlockSpec((1, HEADS, DK, DK), lambda b, t: (b, 0, 0, 0)),
                   pl.BlockSpec((1, CONV_K - 1, CONV_W), lambda b, t: (b, 0, 0))],
        out_shape=[jax.ShapeDtypeStruct((bsz, seq, D_MODEL), F32),
                   jax.ShapeDtypeStruct((2, n_tok, PLANE_W), I32),
                   jax.ShapeDtypeStruct((N_EXP, n_tok), F32),
                   jax.ShapeDtypeStruct((bsz, HEADS, DK, DK), F32),
                   jax.ShapeDtypeStruct((bsz, CONV_K - 1, CONV_W), F32)],
        scratch_shapes=[pltpu.VMEM((tt, IN_W), F32),
                        pltpu.VMEM((HEADS, DK, DK), F32),
                        pltpu.VMEM((CONV_K - 1, CONV_W), F32),
                        pltpu.VMEM((tt, KEY_W), F32)],
        compiler_params=pltpu.CompilerParams(
            dimension_semantics=("arbitrary", "arbitrary"), vmem_limit_bytes=VMEM_LIMIT),
        name="mix",
    )(x, mod, *consts, h2_all, lgt_all)


def _smp1_kernel(x_ref, mod_ref, gmix_ref, w_in_ref, lbl_ref, cw_ref, cb_ref, cst_ref,
                 f_ref, k_ref, q_ref, v_ref, gate_ref, yb_ref, sga_ref, sgb_ref, cv_out_ref):
    x = x_ref[...]
    sh1, sc1 = mod_ref[:, 0:D_MODEL], mod_ref[:, D_MODEL:2 * D_MODEL]
    h = _rms(x) * gmix_ref[...] * (1.0 + sc1) + sh1
    hb = h.astype(BF16)

    def proj(c, w):
        return _dot(hb, w_in_ref[:, c:c + w])

    lb = _lower_bound(lbl_ref[...])
    f = lb + (1.0 - lb) * _sigmoid(proj(C_F, KEY_W))
    f_ref[...] = f
    k_ref[...] = 1.0 - f
    q_ref[...] = _silu(proj(C_Q, KEY_W))
    v_ref[...] = proj(C_I, KEY_W)
    gate_ref[...] = _silu(proj(C_G, KEY_W))
    u = proj(C_CC, CONV_W) * proj(C_VB, CONV_W)
    c0, c1 = cst_ref[:, 0:CONV_W], cst_ref[:, CONV_W:2 * CONV_W]
    conv = cw_ref[0:1] * c0 + cw_ref[1:2] * c1 + cw_ref[2:3] * u + cb_ref[...]
    yb_ref[...] = proj(C_BB, CONV_W) * conv
    cv_out_ref[:, 0:CONV_W] = c1
    cv_out_ref[:, CONV_W:2 * CONV_W] = u
    sga_ref[...] = _sigmoid(proj(C_MGA, D_MODEL))
    sgb_ref[...] = _sigmoid(proj(C_MGB, D_MODEL))


def _smp1(x, mod, gmix, w_in, lbl, cw, cb, cst):
    n = x.shape[0]
    kw = jax.ShapeDtypeStruct((n, KEY_W), F32)
    dm = jax.ShapeDtypeStruct((n, D_MODEL), F32)
    return pl.pallas_call(
        _smp1_kernel,
        out_shape=[kw, kw, kw, kw, kw, kw, dm, dm,
                   jax.ShapeDtypeStruct((n, (CONV_K - 1) * CONV_W), F32)],
        compiler_params=pltpu.CompilerParams(vmem_limit_bytes=VMEM_LIMIT),
        name="smp1",
    )(x, mod, gmix, w_in, lbl, cw, cb, cst)


def _smp2_kernel(f_ref, k_ref, q_ref, v_ref, s_ref, s_out_ref, o_ref):
    g = f_ref.shape[0]
    for i in range(g):
        for hd in range(HEADS):
            hs = slice(hd * DK, (hd + 1) * DK)

            def col(ref):
                return jnp.broadcast_to(ref[i:i + 1, hs], (DK, DK)).T

            vrow = v_ref[i:i + 1, hs]
            s_new = col(f_ref) * s_ref[i, hd] + col(k_ref) * vrow
            s_out_ref[i, hd] = s_new
            o_ref[i:i + 1, hs] = jnp.sum(col(q_ref) * s_new, axis=0, keepdims=True)


def _smp2(f, k, q, v, state):
    n = f.shape[0]
    g = SMP_GROUP
    row_spec = pl.BlockSpec((g, KEY_W), lambda i: (i, 0))
    st_spec = pl.BlockSpec((g, HEADS, DK, DK), lambda i: (i, 0, 0, 0))
    return pl.pallas_call(
        _smp2_kernel,
        grid=(n // g,),
        in_specs=[row_spec, row_spec, row_spec, row_spec, st_spec],
        out_specs=[st_spec, row_spec],
        out_shape=[jax.ShapeDtypeStruct(state.shape, F32), jax.ShapeDtypeStruct((n, KEY_W), F32)],
        compiler_params=pltpu.CompilerParams(dimension_semantics=("arbitrary",)),
        name="smp2",
    )(f, k, q, v, state)


def _smp3_kernel(x_ref, mod_ref, o_ref, gate_ref, yb_ref, sga_ref, sgb_ref, hg_ref, gffn_ref,
                 w_oh_ref, w_oc_ref, w_o_ref, wr_hl_ref, w_sgu_ref, w_sd_ref,
                 h2_all_ref, lgt_all_ref, xmid_ref, h2_ref, lgt_ref):
    del h2_all_ref, lgt_all_ref
    parts = []
    for hd in range(HEADS):
        hs = slice(hd * DK, (hd + 1) * DK)
        parts.append(_rms(o_ref[:, hs]) * hg_ref[:, hs] * gate_ref[:, hs])
    ya = jnp.concatenate(parts, axis=1)
    mixed = (sga_ref[...] * _dot(ya.astype(BF16), w_oh_ref[...])
             + sgb_ref[...] * _dot(yb_ref[...].astype(BF16), w_oc_ref[...]))
    g1 = mod_ref[:, 2 * D_MODEL:3 * D_MODEL]
    x1 = x_ref[...] + g1 * _dot(mixed.astype(BF16), w_o_ref[...])
    mod_rows = tuple(mod_ref[:, j * D_MODEL:(j + 1) * D_MODEL] for j in (3, 4, 5))
    xmid, h2b, lgt = _ffn_pre(x1, mod_rows, gffn_ref[...], w_sgu_ref[...], w_sd_ref[...],
                              wr_hl_ref[...])
    xmid_ref[...] = xmid
    _pack_rows(h2b, h2_ref)
    lgt_ref[...] = lgt


def _smp3(x, mod, o, gate, yb, sga, sgb, hg, gffn, w_oh, w_oc, w_o, wr_hl, w_sgu, w_sd,
          h2_all, lgt_all, n_prompt):
    n = x.shape[0]
    vmem_args = [x, mod, o, gate, yb, sga, sgb, hg, gffn, w_oh, w_oc, w_o, wr_hl, w_sgu, w_sd]
    blk = n_prompt // n

    def full(a):
        nd = a.ndim
        return pl.BlockSpec(a.shape, lambda i, _nd=nd: (0,) * _nd)

    return pl.pallas_call(
        _smp3_kernel,
        grid=(1,),
        in_specs=[full(a) for a in vmem_args]
                 + [pl.BlockSpec(memory_space=pl.ANY), pl.BlockSpec(memory_space=pl.ANY)],
        out_specs=[pl.BlockSpec((n, D_MODEL), lambda i: (0, 0)),
                   pl.BlockSpec((2, n, PLANE_W), lambda i: (0, blk, 0)),
                   pl.BlockSpec((N_EXP, n), lambda i: (0, blk))],
        out_shape=[jax.ShapeDtypeStruct((n, D_MODEL), F32),
                   jax.ShapeDtypeStruct(h2_all.shape, h2_all.dtype),
                   jax.ShapeDtypeStruct(lgt_all.shape, lgt_all.dtype)],
        input_output_aliases={len(vmem_args): 1, len(vmem_args) + 1: 2},
        compiler_params=pltpu.CompilerParams(
            dimension_semantics=("arbitrary",), vmem_limit_bytes=VMEM_LIMIT),
        name="smp3",
    )(*vmem_args, h2_all, lgt_all)


def _route_kernel(lgt_ref, bias_ref, idx_ref, w_ref, rank_ref, cnt_ref):
    tr = ROUTE_TILE
    n_tiles = lgt_ref.shape[1] // tr

    def tile(i, carry):
        cols = pl.ds(pl.multiple_of(i * tr, tr), tr)
        picks, weights, ranks, carry = _route_tile(lgt_ref[:, cols], bias_ref[...], carry)
        for k in range(TOP_K):
            idx_ref[k:k + 1, cols] = picks[k]
            w_ref[k:k + 1, cols] = weights[k]
            rank_ref[k:k + 1, cols] = ranks[k]
        return carry

    total = lax.fori_loop(0, n_tiles, tile, jnp.zeros((N_EXP, 1), F32))
    cnt_ref[...] = jnp.broadcast_to(total, cnt_ref.shape).astype(I32)


def _route_tile(lgt, bias, carry):
    tr = lgt.shape[1]
    neg = -jnp.inf
    scores = _sigmoid(lgt)
    sel = scores + bias
    j8 = lax.broadcasted_iota(I32, (GRP_SZ, tr), 0)
    groups = [sel[g * GRP_SZ:(g + 1) * GRP_SZ] for g in range(N_GRP)]
    gscore = []
    for grp in groups:
        m1 = jnp.max(grp, axis=0, keepdims=True)
        i1 = jnp.min(jnp.where(grp == m1, j8, GRP_SZ), axis=0, keepdims=True)
        m2 = jnp.max(jnp.where(j8 == i1, neg, grp), axis=0, keepdims=True)
        gscore.append(m1 + m2)
    kept = []
    for g in range(N_GRP):
        beaten = jnp.zeros((1, tr), I32)
        for o in range(N_GRP):
            if o < g:
                beaten = beaten + (gscore[o] >= gscore[g]).astype(I32)
            elif o > g:
                beaten = beaten + (gscore[o] > gscore[g]).astype(I32)
        kept.append(jnp.where(beaten < TOPK_GRP, groups[g], neg))
    masked = jnp.concatenate(kept, axis=0)
    ei = lax.broadcasted_iota(I32, masked.shape, 0)
    chosen = jnp.zeros(masked.shape, jnp.bool_)
    picks, weights = [], []
    for _ in range(TOP_K):
        m = jnp.max(masked, axis=0, keepdims=True)
        pick = jnp.min(jnp.where(masked == m, ei, N_EXP), axis=0, keepdims=True)
        hit = ei == pick
        weights.append(jnp.sum(jnp.where(hit, scores, 0.0), axis=0, keepdims=True))
        picks.append(pick)
        chosen = chosen | hit
        masked = jnp.where(hit, neg, masked)
    wsum = weights[0]
    for w in weights[1:]:
        wsum = wsum + w
    sel01 = chosen.astype(F32)
    r = lax.broadcasted_iota(I32, (tr, tr), 0)
    c = lax.broadcasted_iota(I32, (tr, tr), 1)
    before = (r < c).astype(BF16)
    cnt = _dot(sel01.astype(BF16), before) + carry
    weights = [w / wsum * ROUTED_SCALE for w in weights]
    ranks = [jnp.sum(jnp.where(ei == p, cnt, 0.0), axis=0, keepdims=True).astype(I32) for p in picks]
    return picks, weights, ranks, carry + jnp.sum(sel01, axis=1, keepdims=True)


def _route(lgt, bias):
    n = lgt.shape[1]
    assert n % ROUTE_TILE == 0
    slot = lambda dt: jax.ShapeDtypeStruct((TOP_K, n), dt)
    return pl.pallas_call(
        _route_kernel,
        out_shape=[slot(I32), slot(F32), slot(I32),
                   jax.ShapeDtypeStruct((N_EXP, 128), I32)],
        name="route",
    )(lgt, bias)


def _dest_kernel(start_ref, idx_ref, rank_ref, dest_ref, *, n_rows):
    n_tok = idx_ref.shape[1]
    idx = idx_ref[...]
    acc = rank_ref[...]
    for e in range(N_EXP):
        acc = acc + jnp.where(idx == e, start_ref[e], 0)
    dest_ref[:, 0:n_tok] = acc
    dest_ref[:, n_tok:2 * n_tok] = acc + n_rows


def _dest(pad_start, idx, rank, n_rows):
    k, n_tok = idx.shape
    return pl.pallas_call(
        functools.partial(_dest_kernel, n_rows=n_rows),
        in_specs=[pl.BlockSpec(memory_space=pltpu.SMEM),
                  pl.BlockSpec(memory_space=pltpu.VMEM),
                  pl.BlockSpec(memory_space=pltpu.VMEM)],
        out_specs=pl.BlockSpec(memory_space=pltpu.VMEM),
        out_shape=jax.ShapeDtypeStruct((k, 2 * n_tok), I32),
        name="dest",
    )(pad_start, idx, rank)


def _sc_mesh():
    return plsc.VectorSubcoreMesh(core_axis_name="core", subcore_axis_name="subcore")


def _dispatch(rows, dest, n_out):
    n, width = rows.shape
    win = SC_WINDOW
    steps = n // win

    @pl.kernel(out_type=jax.ShapeDtypeStruct((n_out, width), rows.dtype), mesh=_sc_mesh(),
               scratch_types=[], name="dispatch")
    def run(x_hbm, *refs):
        i_hbms, o_hbm = refs[:TOP_K], refs[TOP_K]

        def body(x_vmem, *i_vmems):
            for i_vmem in i_vmems:
                pltpu.sync_copy(x_vmem, o_hbm.at[i_vmem.at[0]])

        pltpu.emit_pipeline(
            body,
            grid=(steps,),
            in_specs=[pl.BlockSpec((win, width), lambda i: (i, 0))]
                     + [pl.BlockSpec((1, win), lambda i, k=k: (0, k * steps + i)) for k in range(TOP_K)],
            out_specs=[],
            core_axis_name=("core", "subcore"),
            dimension_semantics=(pltpu.PARALLEL,),
        )(x_hbm, *i_hbms)

    dest_flat = dest.reshape(1, TOP_K * n)
    return run(rows, *([dest_flat] * TOP_K))


def _combine(rows, dest_flat):
    width = rows.shape[1]
    n = dest_flat.shape[0]
    win = SC_WINDOW

    @pl.kernel(out_type=jax.ShapeDtypeStruct((n, width), rows.dtype), mesh=_sc_mesh(),
               scratch_types=[], name="combine")
    def run(y_hbm, i_hbm, o_hbm):
        def body(i_vmem, o_vmem):
            pltpu.sync_copy(y_hbm.at[i_vmem.at[0]], o_vmem)

        pltpu.emit_pipeline(
            body,
            grid=(n // win,),
            in_specs=[pl.BlockSpec((1, win), lambda i: (0, i))],
            out_specs=[pl.BlockSpec((win, width), lambda i: (i, 0))],
            core_axis_name=("core", "subcore"),
            dimension_semantics=(pltpu.PARALLEL,),
        )(i_hbm, o_hbm)

    return run(rows, dest_flat.reshape(1, n))


def _gmm_kernel(blk_exp_ref, n_used_ref, xs_hbm, wg_hbm, wu_hbm, wd_hbm, ys_hbm,
                xbuf, ybuf, wg32, wu32, wd32, wgu_b, wd_b, xsem, ysem, wsem, run_ref):
    bm = xbuf.shape[2]
    n_used = n_used_ref[0]

    def x_copy(b, slot):
        return pltpu.make_async_copy(xs_hbm.at[:, pl.ds(b * bm, bm), :], xbuf.at[slot], xsem.at[slot])

    def y_copy(b, slot):
        return pltpu.make_async_copy(ybuf.at[slot], ys_hbm.at[:, pl.ds(b * bm, bm), :], ysem.at[slot])

    def w_copies(e, slot):
        return (pltpu.make_async_copy(wg_hbm.at[e], wg32.at[slot], wsem.at[slot, 0]),
                pltpu.make_async_copy(wu_hbm.at[e], wu32.at[slot], wsem.at[slot, 1]),
                pltpu.make_async_copy(wd_hbm.at[e], wd32.at[slot], wsem.at[slot, 2]))

    def run_end(b):
        return lax.while_loop(lambda j: (j < n_used) & (blk_exp_ref[jnp.minimum(j, n_used - 1)] == blk_exp_ref[b]),
                              lambda j: j + 1, b + 1)

    run_ref[0] = 0
    x_copy(0, 0).start()
    for c in w_copies(blk_exp_ref[0], 0):
        c.start()

    def block(b, carry):
        slot = b % 2

        @pl.when(b + 1 < n_used)
        def _():
            x_copy(b + 1, 1 - slot).start()

        @pl.when((b == 0) | (blk_exp_ref[b] != blk_exp_ref[jnp.maximum(b - 1, 0)]))
        def _():
            wslot = run_ref[0] % 2
            run_ref[0] = run_ref[0] + 1
            for c in w_copies(blk_exp_ref[b], wslot):
                c.wait()
            wgu_b[:, 0:EXP_FF] = wg32[wslot].astype(BF16)
            wgu_b[:, EXP_FF:2 * EXP_FF] = wu32[wslot].astype(BF16)
            wd_b[...] = wd32[wslot].astype(BF16)
            nxt = run_end(b)

            @pl.when(nxt < n_used)
            def _():
                for c in w_copies(blk_exp_ref[jnp.minimum(nxt, n_used - 1)], 1 - wslot):
                    c.start()

        x_copy(b, slot).wait()

        @pl.when(b >= 2)
        def _():
            y_copy(b - 2, slot).wait()

        xc = _unpack_rows(xbuf[slot, 0], xbuf[slot, 1])
        gu = sum(_dot(c, wgu_b[i * PLANE_W:(i + 1) * PLANE_W, :]) for i, c in enumerate(xc))
        act = (_silu(gu[:, :EXP_FF]) * gu[:, EXP_FF:]).astype(BF16)
        _pack_rows(_dot(act, wd_b[...]).astype(BF16), ybuf.at[slot])
        y_copy(b, slot).start()
        return carry

    lax.fori_loop(0, n_used, block, 0)

    @pl.when(n_used >= 2)
    def _():
        y_copy(n_used - 2, n_used % 2).wait()

    y_copy(n_used - 1, (n_used - 1) % 2).wait()


def _gmm(blk_exp, n_used, xs, w_gate, w_up, w_down):
    n_rows = xs.shape[1]
    bm = GMM_BM
    nb = n_rows // bm

    assert blk_exp.shape == (nb,)
    any_spec = pl.BlockSpec(memory_space=pl.ANY)
    grid_spec = pltpu.PrefetchScalarGridSpec(
        num_scalar_prefetch=2,
        grid=(1,),
        in_specs=[any_spec, any_spec, any_spec, any_spec],
        out_specs=any_spec,
        scratch_shapes=[pltpu.VMEM((2, 2, bm, PLANE_W), I32), pltpu.VMEM((2, 2, bm, PLANE_W), I32),
                        pltpu.VMEM((2, D_MODEL, EXP_FF), F32), pltpu.VMEM((2, D_MODEL, EXP_FF), F32),
                        pltpu.VMEM((2, EXP_FF, D_MODEL), F32),
                        pltpu.VMEM((D_MODEL, 2 * EXP_FF), BF16), pltpu.VMEM((EXP_FF, D_MODEL), BF16),
                        pltpu.SemaphoreType.DMA((2,)), pltpu.SemaphoreType.DMA((2,)),
                        pltpu.SemaphoreType.DMA((2, 3)), pltpu.SMEM((1,), I32)],
    )
    return pl.pallas_call(
        _gmm_kernel,
        grid_spec=grid_spec,
        out_shape=jax.ShapeDtypeStruct((2, n_rows, PLANE_W), I32),
        compiler_params=pltpu.CompilerParams(dimension_semantics=("arbitrary",)),
        name="gmm",
    )(blk_exp, n_used, xs, w_gate, w_up, w_down)


def _final_kernel(xmid_ref, g2_ref, z_ref, w_ref, gfin_ref, *rest):
    y_ref = rest[-1]
    accs = [jnp.zeros((xmid_ref.shape[0], PLANE_W), F32) for _ in range(4)]
    w_cols = w_ref[...].T
    for k in range(TOP_K):
        wk = w_cols[:, k:k + 1]
        cols = _unpack_rows(z_ref[k, 0], z_ref[k, 1])
        accs = [a + wk * c.astype(F32) for a, c in zip(accs, cols)]
    acc = jnp.concatenate(accs, axis=1)
    y_ref[...] = _rms(xmid_ref[...] + g2_ref[0] * acc) * gfin_ref[...]


def _final(xmid, g2, z, w_t, gfin, tile, *, n_tiles, x_tile0, z_tile0, w_tile0, tiles_per_g2, y_prev=None):
    args = [xmid, g2, z, w_t, gfin]
    in_specs = [pl.BlockSpec((tile, D_MODEL), lambda i: (x_tile0 + i, 0)),
                pl.BlockSpec((1, g2.shape[1], D_MODEL), lambda i: ((x_tile0 + i) // tiles_per_g2, 0, 0)),
                pl.BlockSpec((TOP_K, 2, tile, PLANE_W), lambda i: (0, 0, z_tile0 + i, 0)),
                pl.BlockSpec((TOP_K, tile), lambda i: (0, w_tile0 + i)),
                pl.BlockSpec((1, D_MODEL), lambda i: (0, 0))]
    aliases = {}
    if y_prev is not None:
        args.append(y_prev)
        in_specs.append(pl.BlockSpec(memory_space=pl.ANY))
        aliases = {len(args) - 1: 0}
    return pl.pallas_call(
        _final_kernel,
        grid=(n_tiles,),
        in_specs=in_specs,
        out_specs=pl.BlockSpec((tile, D_MODEL), lambda i: (x_tile0 + i, 0)),
        out_shape=jax.ShapeDtypeStruct(xmid.shape, F32),
        input_output_aliases=aliases,
        compiler_params=pltpu.CompilerParams(dimension_semantics=("arbitrary",)),
        name="final",
    )(*args)


def kernel(x_prompt, x_sample, state_hgrn, state_conv, c_prompt, c_sample, w_ada, b_ada, norm_mix_g, norm_ffn_g, w_in, lb_logits, hgrn_norm_g, conv_w, conv_b, w_out_hgrn, w_out_conv, w_o, w_router, router_bias, w_exp_gate, w_exp_up, w_exp_down, w_sh_gate, w_sh_up, w_sh_down, final_norm_g):
    assert w_ada.shape[0] == 1 and lb_logits.shape[0] == 2
    bsz, seq, _ = x_prompt.shape
    n_smp = x_sample.shape[0]
    n_prompt = bsz * seq
    n_tok = n_prompt + n_smp

    w_in_b = w_in[0].astype(BF16)
    w_oh_b = w_out_hgrn[0].astype(BF16)
    w_oc_b = w_out_conv[0].astype(BF16)
    w_o_b = w_o[0].astype(BF16)
    wr_t = w_router[0].T
    wr_hi = wr_t.astype(BF16)
    wr_hl = jnp.concatenate([wr_hi, (wr_t - wr_hi.astype(F32)).astype(BF16)], axis=0)
    w_sgu = jnp.concatenate([w_sh_gate[0], w_sh_up[0]], axis=1).astype(BF16)
    w_sd = w_sh_down[0].astype(BF16)
    gmix = norm_mix_g[0].reshape(1, D_MODEL)
    gffn = norm_ffn_g[0].reshape(1, D_MODEL)
    hg = hgrn_norm_g[0].reshape(1, KEY_W)
    cw = conv_w[0]
    cb = conv_b[0].reshape(1, CONV_W)
    gfin = final_norm_g.reshape(1, D_MODEL)

    mod = _ada(jnp.concatenate([c_prompt, c_sample], axis=0), w_ada[0], b_ada[0])
    mod_p, mod_s = mod[:bsz], mod[bsz:]

    xmid_p, h2_all, lgt_all, s_p, cv_p = _mix(
        x_prompt, mod_p.reshape(bsz, 6, D_MODEL),
        jnp.zeros((2, n_tok, PLANE_W), I32), jnp.zeros((N_EXP, n_tok), F32),
        gmix, gffn, w_in_b, lb_logits, hg, cw, cb, w_oh_b, w_oc_b, w_o_b, wr_hl, w_sgu, w_sd)

    xs2 = x_sample.reshape(n_smp, D_MODEL)
    f, kk, q, v, gate, yb, sga, sgb, cv_s = _smp1(
        xs2, mod_s, gmix, w_in_b, lb_logits, cw, cb, state_conv[0].reshape(n_smp, (CONV_K - 1) * CONV_W))
    s_s, o_s = _smp2(f, kk, q, v, state_hgrn[0])
    xmid_s, h2_all, lgt_all = _smp3(xs2, mod_s, o_s, gate, yb, sga, sgb, hg, gffn,
                                    w_oh_b, w_oc_b, w_o_b, wr_hl, w_sgu, w_sd,
                                    h2_all, lgt_all, n_prompt)

    idx, w_tok, rank, cnt = _route(lgt_all, router_bias[0].reshape(N_EXP, 1))

    bm = GMM_BM
    n_blocks = (n_tok * TOP_K + N_EXP * (bm - 1)) // bm
    n_rows = n_blocks * bm
    counts = cnt[:, 0]
    padded = (counts + bm - 1) // bm * bm
    pad_end = jnp.cumsum(padded)
    pad_start = pad_end - padded
    blk_row0 = jnp.arange(n_blocks, dtype=I32) * bm
    blk_exp = jnp.minimum(jnp.sum((pad_end[None, :] <= blk_row0[:, None]).astype(I32), axis=1), N_EXP - 1)
    n_used = (pad_end[-1:] // bm).astype(I32)
    dest = _dest(pad_start.astype(I32), idx, rank, n_rows)

    xs = _dispatch(h2_all.reshape(2 * n_tok, PLANE_W), dest, 2 * n_rows).reshape(2, n_rows, PLANE_W)
    ys = _gmm(blk_exp, n_used, xs, w_exp_gate[0], w_exp_up[0], w_exp_down[0])

    ys_flat = ys.reshape(2 * n_rows, PLANE_W)
    dest3 = dest.reshape(TOP_K, 2, n_tok)
    w_t = w_tok
    xmid_p2 = xmid_p.reshape(n_prompt, D_MODEL)
    g2_p = mod_p[:, 5 * D_MODEL:].reshape(bsz, 1, D_MODEL)
    g2_s = mod_s[:, 5 * D_MODEL:].reshape(1, n_smp, D_MODEL)
    chunk = n_prompt // FINAL_CHUNKS
    y_p = None
    for c in range(FINAL_CHUNKS):
        c0 = c * chunk
        c1 = n_tok if c == FINAL_CHUNKS - 1 else c0 + chunk
        z = _combine(ys_flat, dest3[:, :, c0:c1].reshape(-1)).reshape(TOP_K, 2, c1 - c0, PLANE_W)
        y_p = _final(xmid_p2, g2_p, z, w_t, gfin, FINAL_TILE, n_tiles=chunk // FINAL_TILE,
                     x_tile0=c0 // FINAL_TILE, z_tile0=0, w_tile0=c0 // FINAL_TILE,
                     tiles_per_g2=seq // FINAL_TILE, y_prev=y_p)
    y_s = _final(xmid_s, g2_s, z, w_t, gfin, n_smp, n_tiles=1, x_tile0=0, z_tile0=chunk // n_smp,
                 w_tile0=n_prompt // n_smp, tiles_per_g2=1)

    return (y_p.reshape(bsz, seq, D_MODEL), y_s.reshape(n_smp, 1, D_MODEL),
            s_p[None], cv_p[None], s_s[None], cv_s.reshape(1, n_smp, CONV_K - 1, CONV_W))
```

```python
import functools

import jax
import jax.numpy as jnp
from jax import lax
from jax.experimental import pallas as pl
from jax.experimental.pallas import tpu as pltpu
from jax.experimental.pallas import tpu_sc as plsc

F32 = jnp.float32
BF16 = jnp.bfloat16
I32 = jnp.int32

D_MODEL = 1024
HALF_D = D_MODEL // 2
HEADS = 4
DK = 128
KEY_W = HEADS * DK
CONV_W = 512
CONV_K = 3
IN_W = 2 * KEY_W + 2 * KEY_W + 3 * CONV_W + 2 * D_MODEL
N_EXP = 64
TOP_K = 8
N_GRP = 8
GRP_SZ = N_EXP // N_GRP
TOPK_GRP = 4
EXP_FF = 256
SH_FF = 256
ROUTED_SCALE = 2.5
EPS = 1e-6

C_Q, C_F, C_I, C_G = 0, 512, 1024, 1536
C_BB, C_CC, C_VB = 2048, 2560, 3072
C_MGA, C_MGB = 3584, 4608

MIX_TILE = 512
SUB = 128
HALF = SUB // 2
ROUTE_TILE = 384
GMM_BM = 512
FINAL_TILE = 256
FINAL_CHUNKS = 4
SMP_GROUP = 8
SC_WINDOW = 128
PLANE_W = HALF_D // 2
VMEM_LIMIT = 56 * 1024 * 1024


def _dot(a, b):
    return jnp.dot(a, b, preferred_element_type=F32)


def _dot_nt(a, b):
    return lax.dot_general(a, b, (((1,), (1,)), ((), ())), preferred_element_type=F32)


def _dot_tn(a, b):
    return lax.dot_general(a, b, (((0,), (0,)), ((), ())), preferred_element_type=F32)


def _sigmoid(x):
    return 0.5 * jnp.tanh(0.5 * x) + 0.5


def _silu(x):
    h = 0.5 * x
    return h * jnp.tanh(h) + h


def _rms(x):
    return x * lax.rsqrt(jnp.mean(x * x, axis=-1, keepdims=True) + EPS)


def _lower_bound(lbl):
    a, b = lbl[0:1], lbl[1:2]
    m = jnp.maximum(a, b)
    ea, eb = jnp.exp(a - m), jnp.exp(b - m)
    return ea / (ea + eb)


def _split3(x):
    hi = x.astype(BF16)
    r1 = x - hi.astype(F32)
    mid = r1.astype(BF16)
    lo = (r1 - mid.astype(F32)).astype(BF16)
    return hi, mid, lo


def _pack_rows(xb, out_ref, r0=0):
    bits = lax.bitcast_convert_type(xb.astype(F32), I32)
    lo = lax.shift_right_logical(bits[:, :HALF_D], 16)
    hi = bits[:, HALF_D:] & jnp.int32(-65536)
    words = lo | hi
    rows = slice(r0, r0 + xb.shape[0])
    out_ref[0, rows, :] = words[:, :PLANE_W]
    out_ref[1, rows, :] = words[:, PLANE_W:]


def _unpack_rows(p0, p1):
    def halves(w):
        lo = lax.bitcast_convert_type(lax.shift_left(w, 16), F32)
        hi = lax.bitcast_convert_type(w & jnp.int32(-65536), F32)
        return lo.astype(BF16), hi.astype(BF16)

    c0, c2 = halves(p0)
    c1, c3 = halves(p1)
    return c0, c1, c2, c3


def _ada_kernel(c_ref, w_ref, b_ref, o_ref):
    a = _silu(c_ref[...]).astype(BF16)
    o_ref[...] = _dot(a, w_ref[...].astype(BF16)) + b_ref[...]


def _ada(c_all, w_ada, b_ada):
    n = c_all.shape[0]
    blk = 1024
    return pl.pallas_call(
        _ada_kernel,
        grid=(6 * D_MODEL // blk,),
        in_specs=[pl.BlockSpec((n, D_MODEL), lambda j: (0, 0)),
                  pl.BlockSpec((D_MODEL, blk), lambda j: (0, j)),
                  pl.BlockSpec((1, blk), lambda j: (0, j))],
        out_specs=pl.BlockSpec((n, blk), lambda j: (0, j)),
        out_shape=jax.ShapeDtypeStruct((n, 6 * D_MODEL), F32),
        name="ada",
    )(c_all, w_ada, b_ada.reshape(1, -1))


def _ffn_pre(x1, mod_rows, gffn, w_sgu, w_sd, wr_hl):
    sh2, sc2, g2 = mod_rows
    h2 = _rms(x1) * gffn * (1.0 + sc2) + sh2
    h2b = h2.astype(BF16)
    gu = _dot(h2b, w_sgu)
    act = _silu(gu[:, :SH_FF]) * gu[:, SH_FF:]
    xmid = x1 + g2 * _dot(act.astype(BF16), w_sd)
    h2lo = (h2 - h2b.astype(F32)).astype(BF16)
    both = _dot_nt(wr_hl, h2b)
    lgt = both[:N_EXP] + both[N_EXP:] + _dot_nt(wr_hl[:N_EXP], h2lo)
    return xmid, h2b, lgt


def _mix_kernel(x_ref, mod_ref, gmix_ref, gffn_ref, w_in_ref, lbl_ref, hg_ref, cw_ref, cb_ref,
                w_oh_ref, w_oc_ref, w_o_ref, wr_hl_ref, w_sgu_ref, w_sd_ref,
                h2_all_ref, lgt_all_ref,
                xmid_ref, h2_ref, lgt_ref, s_out_ref, cv_out_ref,
                proj_ref, st_ref, cbuf_ref, ya_ref):
    del h2_all_ref, lgt_all_ref
    t = pl.program_id(1)
    nt = pl.num_programs(1)
    tt = x_ref.shape[1]

    @pl.when(t == 0)
    def _():
        st_ref[...] = jnp.zeros_like(st_ref)
        cbuf_ref[...] = jnp.zeros_like(cbuf_ref)

    x = x_ref[0]
    mod = mod_ref[0]
    sh1, sc1, g1 = mod[0:1], mod[1:2], mod[2:3]
    h = _rms(x) * gmix_ref[...] * (1.0 + sc1) + sh1
    hb = h.astype(BF16)
    for c in range(0, IN_W, 512):
        proj_ref[:, c:c + 512] = _dot(hb, w_in_ref[:, c:c + 512])

    lb = _lower_bound(lbl_ref[...])
    row = lax.broadcasted_iota(I32, (SUB, SUB), 0)
    col = lax.broadcasted_iota(I32, (SUB, SUB), 1)
    tri = (col <= row).astype(BF16)
    mask_d = (col <= row) & ((row >= HALF) == (col >= HALF))
    top_half = lax.broadcasted_iota(I32, (SUB, DK), 0) < HALF

    for s in range(tt // SUB):
        r0 = s * SUB
        f = lb + (1.0 - lb) * _sigmoid(proj_ref[r0:r0 + SUB, C_F:C_F + KEY_W])
        kk = 1.0 - f
        hi, mid, lo = _split3(jnp.log(f))
        bc = _dot(tri, hi) + _dot(tri, mid) + _dot(tri, lo)
        for hd in range(HEADS):
            hs = slice(hd * DK, (hd + 1) * DK)
            bh = bc[:, hs]
            b31, b63 = bh[HALF // 2 - 1:HALF // 2], bh[HALF - 1:HALF]
            b95, b127 = bh[HALF + HALF // 2 - 1:HALF + HALF // 2], bh[SUB - 1:SUB]
            arg = bh - jnp.where(top_half, b31, b95)
            e_pos, e_neg = jnp.exp(arg), jnp.exp(-arg)
            q = _silu(proj_ref[r0:r0 + SUB, C_Q + hd * DK:C_Q + (hd + 1) * DK])
            v = proj_ref[r0:r0 + SUB, C_I + hd * DK:C_I + (hd + 1) * DK]
            qd = q * e_pos
            kd = kk[:, hs] * e_neg
            q_in = qd * jnp.where(top_half, jnp.exp(b31), jnp.exp(b95))
            k_end = kd * jnp.where(top_half, jnp.exp(b127 - b31), jnp.exp(b127 - b95))
            qa = jnp.where(top_half, 0.0, qd * jnp.exp(b95 - b63))
            ka = jnp.where(top_half, kd * jnp.exp(b63 - b31), 0.0)
            att = jnp.where(mask_d, _dot_nt(qd.astype(BF16), kd.astype(BF16)), 0.0)
            att = att + _dot_nt(qa.astype(BF16), ka.astype(BF16))
            vb = v.astype(BF16)
            st = st_ref[hd]
            o = _dot(att.astype(BF16), vb) + _dot_nt(q_in.astype(BF16), st.astype(BF16))
            st_ref[hd] = st * jnp.exp(b127) + _dot_tn(vb, k_end.astype(BF16))
            gate = _silu(proj_ref[r0:r0 + SUB, C_G + hd * DK:C_G + (hd + 1) * DK])
            ya_ref[r0:r0 + SUB, hs] = _rms(o) * hg_ref[:, hs] * gate

    u = proj_ref[:, C_CC:C_CC + CONV_W] * proj_ref[:, C_VB:C_VB + CONV_W]
    rows = lax.broadcasted_iota(I32, (tt, CONV_W), 0)
    c0, c1 = cbuf_ref[0:1], cbuf_ref[1:2]
    u1 = jnp.where(rows == 0, c1, pltpu.roll(u, 1, axis=0))
    u2 = jnp.where(rows == 0, c0, jnp.where(rows == 1, c1, pltpu.roll(u, 2, axis=0)))
    conv = cw_ref[0:1] * u2 + cw_ref[1:2] * u1 + cw_ref[2:3] * u + cb_ref[...]
    yb = proj_ref[:, C_BB:C_BB + CONV_W] * conv
    cbuf_ref[...] = u[tt - 2:tt]

    mixed = (_sigmoid(proj_ref[:, C_MGA:C_MGA + D_MODEL]) * _dot(ya_ref[...].astype(BF16), w_oh_ref[...])
             + _sigmoid(proj_ref[:, C_MGB:C_MGB + D_MODEL]) * _dot(yb.astype(BF16), w_oc_ref[...]))
    x1 = x + g1 * _dot(mixed.astype(BF16), w_o_ref[...])

    xmid, h2b, lgt = _ffn_pre(x1, (mod[3:4], mod[4:5], mod[5:6]), gffn_ref[...],
                              w_sgu_ref[...], w_sd_ref[...], wr_hl_ref[...])
    xmid_ref[0] = xmid
    _pack_rows(h2b, h2_ref)
    lgt_ref[...] = lgt

    @pl.when(t == nt - 1)
    def _():
        for hd in range(HEADS):
            s_out_ref[0, hd] = st_ref[hd].T
        cv_out_ref[0] = cbuf_ref[...]


def _const_spec(shape):
    nd = len(shape)
    return pl.BlockSpec(shape, lambda b, t, _nd=nd: (0,) * _nd, pipeline_mode=pl.Buffered(1))


def _mix(x, mod, h2_all, lgt_all, gmix, gffn, w_in, lbl, hg, cw, cb, w_oh, w_oc, w_o, wr_hl, w_sgu, w_sd):
    bsz, seq, _ = x.shape
    n_tok = h2_all.shape[1]
    tt = MIX_TILE
    nt = seq // tt
    consts = [gmix, gffn, w_in, lbl, hg, cw, cb, w_oh, w_oc, w_o, wr_hl, w_sgu, w_sd]
    return pl.pallas_call(
        _mix_kernel,
        grid=(bsz, nt),
        in_specs=[pl.BlockSpec((1, tt, D_MODEL), lambda b, t: (b, t, 0)),
                  pl.BlockSpec((1, 6, D_MODEL), lambda b, t: (b, 0, 0))]
                 + [_const_spec(a.shape) for a in consts]
                 + [pl.BlockSpec(memory_space=pl.ANY), pl.BlockSpec(memory_space=pl.ANY)],
        input_output_aliases={2 + len(consts): 1, 3 + len(consts): 2},
        out_specs=[pl.BlockSpec((1, tt, D_MODEL), lambda b, t: (b, t, 0)),
                   pl.BlockSpec((2, tt, PLANE_W), lambda b, t: (0, b * nt + t, 0)),
                   pl.BlockSpec((N_EXP, tt), lambda b, t: (0, b * nt + t)),
                   pl.BlockSpec((1, HEADS, DK, DK), lambda b, t: (b, 0, 0, 0)),
                   pl.BlockSpec((1, CONV_K - 1, CONV_W), lambda b, t: (b, 0, 0))],
        out_shape=[jax.ShapeDtypeStruct((bsz, seq, D_MODEL), F32),
                   jax.ShapeDtypeStruct((2, n_tok, PLANE_W), I32),
                   jax.ShapeDtypeStruct((N_EXP, n_tok), F32),
                   jax.ShapeDtypeStruct((bsz, HEADS, DK, DK), F32),
                   jax.ShapeDtypeStruct((bsz, CONV_K - 1, CONV_W), F32)],
        scratch_shapes=[pltpu.VMEM((tt, IN_W), F32),
                        pltpu.VMEM((HEADS, DK, DK), F32),
                        pltpu.VMEM((CONV_K - 1, CONV_W), F32),
                        pltpu.VMEM((tt, KEY_W), F32)],
        compiler_params=pltpu.CompilerParams(
            dimension_semantics=("arbitrary", "arbitrary"), vmem_limit_bytes=VMEM_LIMIT),
        name="mix",
    )(x, mod, *consts, h2_all, lgt_all)


def _smp1_kernel(x_ref, mod_ref, gmix_ref, w_in_ref, lbl_ref, cw_ref, cb_ref, cst_ref,
                 f_ref, k_ref, q_ref, v_ref, gate_ref, yb_ref, sga_ref, sgb_ref, cv_out_ref):
    x = x_ref[...]
    sh1, sc1 = mod_ref[:, 0:D_MODEL], mod_ref[:, D_MODEL:2 * D_MODEL]
    h = _rms(x) * gmix_ref[...] * (1.0 + sc1) + sh1
    hb = h.astype(BF16)

    def proj(c, w):
        return _dot(hb, w_in_ref[:, c:c + w])

    lb = _lower_bound(lbl_ref[...])
    f = lb + (1.0 - lb) * _sigmoid(proj(C_F, KEY_W))
    f_ref[...] = f
    k_ref[...] = 1.0 - f
    q_ref[...] = _silu(proj(C_Q, KEY_W))
    v_ref[...] = proj(C_I, KEY_W)
    gate_ref[...] = _silu(proj(C_G, KEY_W))
    u = proj(C_CC, CONV_W) * proj(C_VB, CONV_W)
    c0, c1 = cst_ref[:, 0:CONV_W], cst_ref[:, CONV_W:2 * CONV_W]
    conv = cw_ref[0:1] * c0 + cw_ref[1:2] * c1 + cw_ref[2:3] * u + cb_ref[...]
    yb_ref[...] = proj(C_BB, CONV_W) * conv
    cv_out_ref[:, 0:CONV_W] = c1
    cv_out_ref[:, CONV_W:2 * CONV_W] = u
    sga_ref[...] = _sigmoid(proj(C_MGA, D_MODEL))
    sgb_ref[...] = _sigmoid(proj(C_MGB, D_MODEL))


def _smp1(x, mod, gmix, w_in, lbl, cw, cb, cst):
    n = x.shape[0]
    kw = jax.ShapeDtypeStruct((n, KEY_W), F32)
    dm = jax.ShapeDtypeStruct((n, D_MODEL), F32)
    return pl.pallas_call(
        _smp1_kernel,
        out_shape=[kw, kw, kw, kw, kw, kw, dm, dm,
                   jax.ShapeDtypeStruct((n, (CONV_K - 1) * CONV_W), F32)],
        compiler_params=pltpu.CompilerParams(vmem_limit_bytes=VMEM_LIMIT),
        name="smp1",
    )(x, mod, gmix, w_in, lbl, cw, cb, cst)


def _smp2_kernel(f_ref, k_ref, q_ref, v_ref, s_ref, s_out_ref, o_ref):
    g = f_ref.shape[0]
    for i in range(g):
        for hd in range(HEADS):
            hs = slice(hd * DK, (hd + 1) * DK)

            def col(ref):
                return jnp.broadcast_to(ref[i:i + 1, hs], (DK, DK)).T

            vrow = v_ref[i:i + 1, hs]
            s_new = col(f_ref) * s_ref[i, hd] + col(k_ref) * vrow
            s_out_ref[i, hd] = s_new
            o_ref[i:i + 1, hs] = jnp.sum(col(q_ref) * s_new, axis=0, keepdims=True)


def _smp2(f, k, q, v, state):
    n = f.shape[0]
    g = SMP_GROUP
    row_spec = pl.BlockSpec((g, KEY_W), lambda i: (i, 0))
    st_spec = pl.BlockSpec((g, HEADS, DK, DK), lambda i: (i, 0, 0, 0))
    return pl.pallas_call(
        _smp2_kernel,
        grid=(n // g,),
        in_specs=[row_spec, row_spec, row_spec, row_spec, st_spec],
        out_specs=[st_spec, row_spec],
        out_shape=[jax.ShapeDtypeStruct(state.shape, F32), jax.ShapeDtypeStruct((n, KEY_W), F32)],
        compiler_params=pltpu.CompilerParams(dimension_semantics=("arbitrary",)),
        name="smp2",
    )(f, k, q, v, state)


def _smp3_kernel(x_ref, mod_ref, o_ref, gate_ref, yb_ref, sga_ref, sgb_ref, hg_ref, gffn_ref,
                 w_oh_ref, w_oc_ref, w_o_ref, wr_hl_ref, w_sgu_ref, w_sd_ref,
                 h2_all_ref, lgt_all_ref, xmid_ref, h2_ref, lgt_ref):
    del h2_all_ref, lgt_all_ref
    parts = []
    for hd in range(HEADS):
        hs = slice(hd * DK, (hd + 1) * DK)
        parts.append(_rms(o_ref[:, hs]) * hg_ref[:, hs] * gate_ref[:, hs])
    ya = jnp.concatenate(parts, axis=1)
    mixed = (sga_ref[...] * _dot(ya.astype(BF16), w_oh_ref[...])
             + sgb_ref[...] * _dot(yb_ref[...].astype(BF16), w_oc_ref[...]))
    g1 = mod_ref[:, 2 * D_MODEL:3 * D_MODEL]
    x1 = x_ref[...] + g1 * _dot(mixed.astype(BF16), w_o_ref[...])
    mod_rows = tuple(mod_ref[:, j * D_MODEL:(j + 1) * D_MODEL] for j in (3, 4, 5))
    xmid, h2b, lgt = _ffn_pre(x1, mod_rows, gffn_ref[...], w_sgu_ref[...], w_sd_ref[...],
                              wr_hl_ref[...])
    xmid_ref[...] = xmid
    _pack_rows(h2b, h2_ref)
    lgt_ref[...] = lgt


def _smp3(x, mod, o, gate, yb, sga, sgb, hg, gffn, w_oh, w_oc, w_o, wr_hl, w_sgu, w_sd,
          h2_all, lgt_all, n_prompt):
    n = x.shape[0]
    vmem_args = [x, mod, o, gate, yb, sga, sgb, hg, gffn, w_oh, w_oc, w_o, wr_hl, w_sgu, w_sd]
    blk = n_prompt // n

    def full(a):
        nd = a.ndim
        return pl.BlockSpec(a.shape, lambda i, _nd=nd: (0,) * _nd)

    return pl.pallas_call(
        _smp3_kernel,
        grid=(1,),
        in_specs=[full(a) for a in vmem_args]
                 + [pl.BlockSpec(memory_space=pl.ANY), pl.BlockSpec(memory_space=pl.ANY)],
        out_specs=[pl.BlockSpec((n, D_MODEL), lambda i: (0, 0)),
                   pl.BlockSpec((2, n, PLANE_W), lambda i: (0, blk, 0)),
                   pl.BlockSpec((N_EXP, n), lambda i: (0, blk))],
        out_shape=[jax.ShapeDtypeStruct((n, D_MODEL), F32),
                   jax.ShapeDtypeStruct(h2_all.shape, h2_all.dtype),
                   jax.ShapeDtypeStruct(lgt_all.shape, lgt_all.dtype)],
        input_output_aliases={len(vmem_args): 1, len(vmem_args) + 1: 2},
        compiler_params=pltpu.CompilerParams(
            dimension_semantics=("arbitrary",), vmem_limit_bytes=VMEM_LIMIT),
        name="smp3",
    )(*vmem_args, h2_all, lgt_all)


def _route_kernel(lgt_ref, bias_ref, idx_ref, w_ref, rank_ref, cnt_ref):
    tr = ROUTE_TILE
    n_tiles = lgt_ref.shape[1] // tr

    def tile(i, carry):
        cols = pl.ds(pl.multiple_of(i * tr, tr), tr)
        picks, weights, ranks, carry = _route_tile(lgt_ref[:, cols], bias_ref[...], carry)
        for k in range(TOP_K):
            idx_ref[k:k + 1, cols] = picks[k]
            w_ref[k:k + 1, cols] = weights[k]
            rank_ref[k:k + 1, cols] = ranks[k]
        return carry

    total = lax.fori_loop(0, n_tiles, tile, jnp.zeros((N_EXP, 1), F32))
    cnt_ref[...] = jnp.broadcast_to(total, cnt_ref.shape).astype(I32)


def _route_tile(lgt, bias, carry):
    tr = lgt.shape[1]
    neg = -jnp.inf
    scores = _sigmoid(lgt)
    sel = scores + bias
    j8 = lax.broadcasted_iota(I32, (GRP_SZ, tr), 0)
    groups = [sel[g * GRP_SZ:(g + 1) * GRP_SZ] for g in range(N_GRP)]
    gscore = []
    for grp in groups:
        m1 = jnp.max(grp, axis=0, keepdims=True)
        i1 = jnp.min(jnp.where(grp == m1, j8, GRP_SZ), axis=0, keepdims=True)
        m2 = jnp.max(jnp.where(j8 == i1, neg, grp), axis=0, keepdims=True)
        gscore.append(m1 + m2)
    kept = []
    for g in range(N_GRP):
        beaten = jnp.zeros((1, tr), I32)
        for o in range(N_GRP):
            if o < g:
                beaten = beaten + (gscore[o] >= gscore[g]).astype(I32)
            elif o > g:
                beaten = beaten + (gscore[o] > gscore[g]).astype(I32)
        kept.append(jnp.where(beaten < TOPK_GRP, groups[g], neg))
    masked = jnp.concatenate(kept, axis=0)
    ei = lax.broadcasted_iota(I32, masked.shape, 0)
    chosen = jnp.zeros(masked.shape, jnp.bool_)
    picks, weights = [], []
    for _ in range(TOP_K):
        m = jnp.max(masked, axis=0, keepdims=True)
        pick = jnp.min(jnp.where(masked == m, ei, N_EXP), axis=0, keepdims=True)
        hit = ei == pick
        weights.append(jnp.sum(jnp.where(hit, scores, 0.0), axis=0, keepdims=True))
        picks.append(pick)
        chosen = chosen | hit
        masked = jnp.where(hit, neg, masked)
    wsum = weights[0]
    for w in weights[1:]:
        wsum = wsum + w
    sel01 = chosen.astype(F32)
    r = lax.broadcasted_iota(I32, (tr, tr), 0)
    c = lax.broadcasted_iota(I32, (tr, tr), 1)
    before = (r < c).astype(BF16)
    cnt = _dot(sel01.astype(BF16), before) + carry
    weights = [w / wsum * ROUTED_SCALE for w in weights]
    ranks = [jnp.sum(jnp.where(ei == p, cnt, 0.0), axis=0, keepdims=True).astype(I32) for p in picks]
    return picks, weights, ranks, carry + jnp.sum(sel01, axis=1, keepdims=True)


def _route(lgt, bias):
    n = lgt.shape[1]
    assert n % ROUTE_TILE == 0
    slot = lambda dt: jax.ShapeDtypeStruct((TOP_K, n), dt)
    return pl.pallas_call(
        _route_kernel,
        out_shape=[slot(I32), slot(F32), slot(I32),
                   jax.ShapeDtypeStruct((N_EXP, 128), I32)],
        name="route",
    )(lgt, bias)


def _dest_kernel(start_ref, idx_ref, rank_ref, dest_ref, *, n_rows):
    n_tok = idx_ref.shape[1]
    idx = idx_ref[...]
    acc = rank_ref[...]
    for e in range(N_EXP):
        acc = acc + jnp.where(idx == e, start_ref[e], 0)
    dest_ref[:, 0:n_tok] = acc
    dest_ref[:, n_tok:2 * n_tok] = acc + n_rows


def _dest(pad_start, idx, rank, n_rows):
    k, n_tok = idx.shape
    return pl.pallas_call(
        functools.partial(_dest_kernel, n_rows=n_rows),
        in_specs=[pl.BlockSpec(memory_space=pltpu.SMEM),
                  pl.BlockSpec(memory_space=pltpu.VMEM),
                  pl.BlockSpec(memory_space=pltpu.VMEM)],
        out_specs=pl.BlockSpec(memory_space=pltpu.VMEM),
        out_shape=jax.ShapeDtypeStruct((k, 2 * n_tok), I32),
        name="dest",
    )(pad_start, idx, rank)


def _sc_mesh():
    return plsc.VectorSubcoreMesh(core_axis_name="core", subcore_axis_name="subcore")


def _dispatch(rows, dest, n_out):
    n, width = rows.shape
    win = SC_WINDOW
    steps = n // win

    @pl.kernel(out_type=jax.ShapeDtypeStruct((n_out, width), rows.dtype), mesh=_sc_mesh(),
               scratch_types=[], name="dispatch")
    def run(x_hbm, *refs):
        i_hbms, o_hbm = refs[:TOP_K], refs[TOP_K]

        def body(x_vmem, *i_vmems):
            for i_vmem in i_vmems:
                pltpu.sync_copy(x_vmem, o_hbm.at[i_vmem.at[0]])

        pltpu.emit_pipeline(
            body,
            grid=(steps,),
            in_specs=[pl.BlockSpec((win, width), lambda i: (i, 0))]
                     + [pl.BlockSpec((1, win), lambda i, k=k: (0, k * steps + i)) for k in range(TOP_K)],
            out_specs=[],
            core_axis_name=("core", "subcore"),
            dimension_semantics=(pltpu.PARALLEL,),
        )(x_hbm, *i_hbms)

    dest_flat = dest.reshape(1, TOP_K * n)
    return run(rows, *([dest_flat] * TOP_K))


def _combine(rows, dest_flat):
    width = rows.shape[1]
    n = dest_flat.shape[0]
    win = SC_WINDOW

    @pl.kernel(out_type=jax.ShapeDtypeStruct((n, width), rows.dtype), mesh=_sc_mesh(),
               scratch_types=[], name="combine")
    def run(y_hbm, i_hbm, o_hbm):
        def body(i_vmem, o_vmem):
            pltpu.sync_copy(y_hbm.at[i_vmem.at[0]], o_vmem)

        pltpu.emit_pipeline(
            body,
            grid=(n // win,),
            in_specs=[pl.BlockSpec((1, win), lambda i: (0, i))],
            out_specs=[pl.BlockSpec((win, width), lambda i: (i, 0))],
            core_axis_name=("core", "subcore"),
            dimension_semantics=(pltpu.PARALLEL,),
        )(i_hbm, o_hbm)

    return run(rows, dest_flat.reshape(1, n))


def _gmm_kernel(blk_exp_ref, n_used_ref, xs_hbm, wg_hbm, wu_hbm, wd_hbm, ys_hbm,
                xbuf, ybuf, wg32, wu32, wd32, wgu_b, wd_b, xb0, xb1, acc0, acc1, xsem, ysem, wsem, run_ref):
    bm = xbuf.shape[2]
    n_used = n_used_ref[0]

    def x_copy(b, slot):
        return pltpu.make_async_copy(xs_hbm.at[:, pl.ds(b * bm, bm), :], xbuf.at[slot], xsem.at[slot])

    def y_copy(b, slot):
        return pltpu.make_async_copy(ybuf.at[slot], ys_hbm.at[:, pl.ds(b * bm, bm), :], ysem.at[slot])

    def w_copies(e, slot):
        return (pltpu.make_async_copy(wg_hbm.at[e], wg32.at[slot], wsem.at[slot, 0]),
                pltpu.make_async_copy(wu_hbm.at[e], wu32.at[slot], wsem.at[slot, 1]),
                pltpu.make_async_copy(wd_hbm.at[e], wd32.at[slot], wsem.at[slot, 2]))

    def run_end(b):
        return lax.while_loop(lambda j: (j < n_used) & (blk_exp_ref[jnp.minimum(j, n_used - 1)] == blk_exp_ref[b]),
                              lambda j: j + 1, b + 1)

    piece = bm // 4

    def unpack_piece(xw, xb, r):
        rows = slice(r * piece, (r + 1) * piece)
        for i, c in enumerate(_unpack_rows(xw[0, rows, :], xw[1, rows, :])):
            xb[i, rows, :] = c

    def pack_piece(acc, yw, r):
        _pack_rows(acc[r * piece:(r + 1) * piece, :].astype(BF16), yw, r * piece)

    def stages(xb_cur, xb_nxt, acc_cur, acc_prev, xw_nxt, yw_prev):
        def gu_cols(c0):
            return sum(_dot(xb_cur[i], wgu_b[i * PLANE_W:(i + 1) * PLANE_W, c0:c0 + EXP_FF]) for i in range(4))

        gate = gu_cols(0)
        pack_piece(acc_prev, yw_prev, 0)
        up = gu_cols(EXP_FF)
        pack_piece(acc_prev, yw_prev, 1)
        act = (_silu(gate) * up).astype(BF16)
        side = [lambda: pack_piece(acc_prev, yw_prev, 2),
                lambda: (pack_piece(acc_prev, yw_prev, 3), unpack_piece(xw_nxt, xb_nxt, 0)),
                lambda: (unpack_piece(xw_nxt, xb_nxt, 1), unpack_piece(xw_nxt, xb_nxt, 2)),
                lambda: unpack_piece(xw_nxt, xb_nxt, 3)]
        for n in range(4):
            cols = slice(n * EXP_FF, (n + 1) * EXP_FF)
            acc_cur[:, cols] = _dot(act, wd_b[:, cols])
            side[n]()

    run_ref[0] = 0
    x_copy(0, 0).start()
    for c in w_copies(blk_exp_ref[0], 0):
        c.start()
    acc1[...] = jnp.zeros_like(acc1)
    xbuf[1] = jnp.zeros(xbuf.shape[1:], I32)
    x_copy(0, 0).wait()
    for r in range(4):
        unpack_piece(xbuf.at[0], xb0, r)

    @pl.when(n_used > 1)
    def _():
        x_copy(1, 1).start()

    def block(b, carry):
        @pl.when(b + 1 < n_used)
        def _():
            x_copy(b + 1, (b + 1) % 2).wait()

        @pl.when(b + 2 < n_used)
        def _():
            x_copy(b + 2, b % 2).start()

        @pl.when(b >= 3)
        def _():
            y_copy(b - 3, (b - 1) % 2).wait()

        @pl.when((b < n_used) & ((b == 0) | (blk_exp_ref[jnp.minimum(b, n_used - 1)]
                                             != blk_exp_ref[jnp.maximum(b - 1, 0)])))
        def _():
            wslot = run_ref[0] % 2
            run_ref[0] = run_ref[0] + 1
            for c in w_copies(blk_exp_ref[b], wslot):
                c.wait()
            wgu_b[:, 0:EXP_FF] = wg32[wslot].astype(BF16)
            wgu_b[:, EXP_FF:2 * EXP_FF] = wu32[wslot].astype(BF16)
            wd_b[...] = wd32[wslot].astype(BF16)
            nxt = run_end(b)

            @pl.when(nxt < n_used)
            def _():
                for c in w_copies(blk_exp_ref[jnp.minimum(nxt, n_used - 1)], 1 - wslot):
                    c.start()

        @pl.when(b % 2 == 0)
        def _():
            stages(xb0, xb1, acc0, acc1, xbuf.at[1], ybuf.at[1])

        @pl.when(b % 2 == 1)
        def _():
            stages(xb1, xb0, acc1, acc0, xbuf.at[0], ybuf.at[0])

        @pl.when(b >= 1)
        def _():
            y_copy(b - 1, (b - 1) % 2).start()

        return carry

    lax.fori_loop(0, n_used + 1, block, 0)

    @pl.when(n_used >= 2)
    def _():
        y_copy(n_used - 2, n_used % 2).wait()

    y_copy(n_used - 1, (n_used - 1) % 2).wait()


def _gmm(blk_exp, n_used, xs, w_gate, w_up, w_down):
    n_rows = xs.shape[1]
    bm = GMM_BM
    nb = n_rows // bm

    assert blk_exp.shape == (nb,)
    any_spec = pl.BlockSpec(memory_space=pl.ANY)
    grid_spec = pltpu.PrefetchScalarGridSpec(
        num_scalar_prefetch=2,
        grid=(1,),
        in_specs=[any_spec, any_spec, any_spec, any_spec],
        out_specs=any_spec,
        scratch_shapes=[pltpu.VMEM((2, 2, bm, PLANE_W), I32), pltpu.VMEM((2, 2, bm, PLANE_W), I32),
                        pltpu.VMEM((2, D_MODEL, EXP_FF), F32), pltpu.VMEM((2, D_MODEL, EXP_FF), F32),
                        pltpu.VMEM((2, EXP_FF, D_MODEL), F32),
                        pltpu.VMEM((D_MODEL, 2 * EXP_FF), BF16), pltpu.VMEM((EXP_FF, D_MODEL), BF16),
                        pltpu.VMEM((4, bm, PLANE_W), BF16), pltpu.VMEM((4, bm, PLANE_W), BF16),
                        pltpu.VMEM((bm, D_MODEL), F32), pltpu.VMEM((bm, D_MODEL), F32),
                        pltpu.SemaphoreType.DMA((2,)), pltpu.SemaphoreType.DMA((2,)),
                        pltpu.SemaphoreType.DMA((2, 3)), pltpu.SMEM((1,), I32)],
    )
    return pl.pallas_call(
        _gmm_kernel,
        grid_spec=grid_spec,
        out_shape=jax.ShapeDtypeStruct((2, n_rows, PLANE_W), I32),
        compiler_params=pltpu.CompilerParams(dimension_semantics=("arbitrary",)),
        name="gmm",
    )(blk_exp, n_used, xs, w_gate, w_up, w_down)


def _final_kernel(xmid_ref, g2_ref, z_ref, w_ref, gfin_ref, *rest):
    y_ref = rest[-1]
    accs = [jnp.zeros((xmid_ref.shape[0], PLANE_W), F32) for _ in range(4)]
    w_cols = w_ref[...].T
    for k in range(TOP_K):
        wk = w_cols[:, k:k + 1]
        cols = _unpack_rows(z_ref[k, 0], z_ref[k, 1])
        accs = [a + wk * c.astype(F32) for a, c in zip(accs, cols)]
    acc = jnp.concatenate(accs, axis=1)
    y_ref[...] = _rms(xmid_ref[...] + g2_ref[0] * acc) * gfin_ref[...]


def _final(xmid, g2, z, w_t, gfin, tile, *, n_tiles, x_tile0, z_tile0, w_tile0, tiles_per_g2, y_prev=None):
    args = [xmid, g2, z, w_t, gfin]
    in_specs = [pl.BlockSpec((tile, D_MODEL), lambda i: (x_tile0 + i, 0)),
                pl.BlockSpec((1, g2.shape[1], D_MODEL), lambda i: ((x_tile0 + i) // tiles_per_g2, 0, 0)),
                pl.BlockSpec((TOP_K, 2, tile, PLANE_W), lambda i: (0, 0, z_tile0 + i, 0)),
                pl.BlockSpec((TOP_K, tile), lambda i: (0, w_tile0 + i)),
                pl.BlockSpec((1, D_MODEL), lambda i: (0, 0))]
    aliases = {}
    if y_prev is not None:
        args.append(y_prev)
        in_specs.append(pl.BlockSpec(memory_space=pl.ANY))
        aliases = {len(args) - 1: 0}
    return pl.pallas_call(
        _final_kernel,
        grid=(n_tiles,),
        in_specs=in_specs,
        out_specs=pl.BlockSpec((tile, D_MODEL), lambda i: (x_tile0 + i, 0)),
        out_shape=jax.ShapeDtypeStruct(xmid.shape, F32),
        input_output_aliases=aliases,
        compiler_params=pltpu.CompilerParams(dimension_semantics=("arbitrary",)),
        name="final",
    )(*args)


def kernel(x_prompt, x_sample, state_hgrn, state_conv, c_prompt, c_sample, w_ada, b_ada, norm_mix_g, norm_ffn_g, w_in, lb_logits, hgrn_norm_g, conv_w, conv_b, w_out_hgrn, w_out_conv, w_o, w_router, router_bias, w_exp_gate, w_exp_up, w_exp_down, w_sh_gate, w_sh_up, w_sh_down, final_norm_g):
    assert w_ada.shape[0] == 1 and lb_logits.shape[0] == 2
    bsz, seq, _ = x_prompt.shape
    n_smp = x_sample.shape[0]
    n_prompt = bsz * seq
    n_tok = n_prompt + n_smp

    w_in_b = w_in[0].astype(BF16)
    w_oh_b = w_out_hgrn[0].astype(BF16)
    w_oc_b = w_out_conv[0].astype(BF16)
    w_o_b = w_o[0].astype(BF16)
    wr_t = w_router[0].T
    wr_hi = wr_t.astype(BF16)
    wr_hl = jnp.concatenate([wr_hi, (wr_t - wr_hi.astype(F32)).astype(BF16)], axis=0)
    w_sgu = jnp.concatenate([w_sh_gate[0], w_sh_up[0]], axis=1).astype(BF16)
    w_sd = w_sh_down[0].astype(BF16)
    gmix = norm_mix_g[0].reshape(1, D_MODEL)
    gffn = norm_ffn_g[0].reshape(1, D_MODEL)
    hg = hgrn_norm_g[0].reshape(1, KEY_W)
    cw = conv_w[0]
    cb = conv_b[0].reshape(1, CONV_W)
    gfin = final_norm_g.reshape(1, D_MODEL)

    mod = _ada(jnp.concatenate([c_prompt, c_sample], axis=0), w_ada[0], b_ada[0])
    mod_p, mod_s = mod[:bsz], mod[bsz:]

    xmid_p, h2_all, lgt_all, s_p, cv_p = _mix(
        x_prompt, mod_p.reshape(bsz, 6, D_MODEL),
        jnp.zeros((2, n_tok, PLANE_W), I32), jnp.zeros((N_EXP, n_tok), F32),
        gmix, gffn, w_in_b, lb_logits, hg, cw, cb, w_oh_b, w_oc_b, w_o_b, wr_hl, w_sgu, w_sd)

    xs2 = x_sample.reshape(n_smp, D_MODEL)
    f, kk, q, v, gate, yb, sga, sgb, cv_s = _smp1(
        xs2, mod_s, gmix, w_in_b, lb_logits, cw, cb, state_conv[0].reshape(n_smp, (CONV_K - 1) * CONV_W))
    s_s, o_s = _smp2(f, kk, q, v, state_hgrn[0])
    xmid_s, h2_all, lgt_all = _smp3(xs2, mod_s, o_s, gate, yb, sga, sgb, hg, gffn,
                                    w_oh_b, w_oc_b, w_o_b, wr_hl, w_sgu, w_sd,
                                    h2_all, lgt_all, n_prompt)

    idx, w_tok, rank, cnt = _route(lgt_all, router_bias[0].reshape(N_EXP, 1))

    bm = GMM_BM
    n_blocks = (n_tok * TOP_K + N_EXP * (bm - 1)) // bm
    n_rows = n_blocks * bm
    counts = cnt[:, 0]
    padded = (counts + bm - 1) // bm * bm
    pad_end = jnp.cumsum(padded)
    pad_start = pad_end - padded
    blk_row0 = jnp.arange(n_blocks, dtype=I32) * bm
    blk_exp = jnp.minimum(jnp.sum((pad_end[None, :] <= blk_row0[:, None]).astype(I32), axis=1), N_EXP - 1)
    n_used = (pad_end[-1:] // bm).astype(I32)
    dest = _dest(pad_start.astype(I32), idx, rank, n_rows)

    xs = _dispatch(h2_all.reshape(2 * n_tok, PLANE_W), dest, 2 * n_rows).reshape(2, n_rows, PLANE_W)
    ys = _gmm(blk_exp, n_used, xs, w_exp_gate[0], w_exp_up[0], w_exp_down[0])

    ys_flat = ys.reshape(2 * n_rows, PLANE_W)
    dest3 = dest.reshape(TOP_K, 2, n_tok)
    w_t = w_tok
    xmid_p2 = xmid_p.reshape(n_prompt, D_MODEL)
    g2_p = mod_p[:, 5 * D_MODEL:].reshape(bsz, 1, D_MODEL)
    g2_s = mod_s[:, 5 * D_MODEL:].reshape(1, n_smp, D_MODEL)
    chunk = n_prompt // FINAL_CHUNKS
    y_p = None
    for c in range(FINAL_CHUNKS):
        c0 = c * chunk
        c1 = n_tok if c == FINAL_CHUNKS - 1 else c0 + chunk
        z = _combine(ys_flat, dest3[:, :, c0:c1].reshape(-1)).reshape(TOP_K, 2, c1 - c0, PLANE_W)
        y_p = _final(xmid_p2, g2_p, z, w_t, gfin, FINAL_TILE, n_tiles=chunk // FINAL_TILE,
                     x_tile0=c0 // FINAL_TILE, z_tile0=0, w_tile0=c0 // FINAL_TILE,
                     tiles_per_g2=seq // FINAL_TILE, y_prev=y_p)
    y_s = _final(xmid_s, g2_s, z, w_t, gfin, n_smp, n_tiles=1, x_tile0=0, z_tile0=chunk // n_smp,
                 w_tile0=n_prompt // n_smp, tiles_per_g2=1)

    return (y_p.reshape(bsz, seq, D_MODEL), y_s.reshape(n_smp, 1, D_MODEL),
            s_p[None], cv_p[None], s_s[None], cv_s.reshape(1, n_smp, CONV_K - 1, CONV_W))
```

```python
import functools

import jax
import jax.numpy as jnp
from jax import lax
from jax.experimental import pallas as pl
from jax.experimental.pallas import tpu as pltpu
from jax.experimental.pallas import tpu_sc as plsc

F32 = jnp.float32
BF16 = jnp.bfloat16
I32 = jnp.int32

D_MODEL = 1024
HALF_D = D_MODEL // 2
HEADS = 4
DK = 128
KEY_W = HEADS * DK
CONV_W = 512
CONV_K = 3
IN_W = 2 * KEY_W + 2 * KEY_W + 3 * CONV_W + 2 * D_MODEL
N_EXP = 64
TOP_K = 8
N_GRP = 8
GRP_SZ = N_EXP // N_GRP
TOPK_GRP = 4
EXP_FF = 256
SH_FF = 256
ROUTED_SCALE = 2.5
EPS = 1e-6

C_Q, C_F, C_I, C_G = 0, 512, 1024, 1536
C_BB, C_CC, C_VB = 2048, 2560, 3072
C_MGA, C_MGB = 3584, 4608

MIX_TILE = 512
SUB = 128
HALF = SUB // 2
ROUTE_TILE = 384
GMM_BM = 512
FINAL_TILE = 256
FINAL_CHUNKS = 8
SMP_GROUP = 8
SC_WINDOW = 128
PLANE_W = HALF_D // 2
VMEM_LIMIT = 56 * 1024 * 1024


def _dot(a, b):
    return jnp.dot(a, b, preferred_element_type=F32)


def _dot_nt(a, b):
    return lax.dot_general(a, b, (((1,), (1,)), ((), ())), preferred_element_type=F32)


def _dot_tn(a, b):
    return lax.dot_general(a, b, (((0,), (0,)), ((), ())), preferred_element_type=F32)


def _sigmoid(x):
    return 0.5 * jnp.tanh(0.5 * x) + 0.5


def _silu(x):
    h = 0.5 * x
    return h * jnp.tanh(h) + h


def _rms(x):
    return x * lax.rsqrt(jnp.mean(x * x, axis=-1, keepdims=True) + EPS)


def _lower_bound(lbl):
    a, b = lbl[0:1], lbl[1:2]
    m = jnp.maximum(a, b)
    ea, eb = jnp.exp(a - m), jnp.exp(b - m)
    return ea / (ea + eb)


def _split3(x):
    hi = x.astype(BF16)
    r1 = x - hi.astype(F32)
    mid = r1.astype(BF16)
    lo = (r1 - mid.astype(F32)).astype(BF16)
    return hi, mid, lo


def _pack_rows(xb, out_ref):
    bits = lax.bitcast_convert_type(xb.astype(F32), I32)
    lo = lax.shift_right_logical(bits[:, :HALF_D], 16)
    hi = bits[:, HALF_D:] & jnp.int32(-65536)
    words = lo | hi
    out_ref[0] = words[:, :PLANE_W]
    out_ref[1] = words[:, PLANE_W:]


def _unpack_rows(p0, p1):
    def halves(w):
        lo = lax.bitcast_convert_type(lax.shift_left(w, 16), F32)
        hi = lax.bitcast_convert_type(w & jnp.int32(-65536), F32)
        return lo.astype(BF16), hi.astype(BF16)

    c0, c2 = halves(p0)
    c1, c3 = halves(p1)
    return c0, c1, c2, c3


def _ada_kernel(c_ref, w_ref, b_ref, o_ref):
    a = _silu(c_ref[...]).astype(BF16)
    o_ref[...] = _dot(a, w_ref[...].astype(BF16)) + b_ref[...]


def _ada(c_all, w_ada, b_ada):
    n = c_all.shape[0]
    blk = 1024
    return pl.pallas_call(
        _ada_kernel,
        grid=(6 * D_MODEL // blk,),
        in_specs=[pl.BlockSpec((n, D_MODEL), lambda j: (0, 0)),
                  pl.BlockSpec((D_MODEL, blk), lambda j: (0, j)),
                  pl.BlockSpec((1, blk), lambda j: (0, j))],
        out_specs=pl.BlockSpec((n, blk), lambda j: (0, j)),
        out_shape=jax.ShapeDtypeStruct((n, 6 * D_MODEL), F32),
        name="ada",
    )(c_all, w_ada, b_ada.reshape(1, -1))


def _ffn_pre(x1, mod_rows, gffn, w_sgu, w_sd, wr_hl):
    sh2, sc2, g2 = mod_rows
    h2 = _rms(x1) * gffn * (1.0 + sc2) + sh2
    h2b = h2.astype(BF16)
    gu = _dot(h2b, w_sgu)
    act = _silu(gu[:, :SH_FF]) * gu[:, SH_FF:]
    xmid = x1 + g2 * _dot(act.astype(BF16), w_sd)
    h2lo = (h2 - h2b.astype(F32)).astype(BF16)
    both = _dot_nt(wr_hl, h2b)
    lgt = both[:N_EXP] + both[N_EXP:] + _dot_nt(wr_hl[:N_EXP], h2lo)
    return xmid, h2b, lgt


def _mix_kernel(x_ref, mod_ref, gmix_ref, gffn_ref, w_in_ref, lbl_ref, hg_ref, cw_ref, cb_ref,
                w_oh_ref, w_oc_ref, w_o_ref, wr_hl_ref, w_sgu_ref, w_sd_ref,
                h2_all_ref, lgt_all_ref,
                xmid_ref, h2_ref, lgt_ref, s_out_ref, cv_out_ref,
                proj_ref, st_ref, cbuf_ref, ya_ref):
    del h2_all_ref, lgt_all_ref
    t = pl.program_id(1)
    nt = pl.num_programs(1)
    tt = x_ref.shape[1]

    @pl.when(t == 0)
    def _():
        st_ref[...] = jnp.zeros_like(st_ref)
        cbuf_ref[...] = jnp.zeros_like(cbuf_ref)

    x = x_ref[0]
    mod = mod_ref[0]
    sh1, sc1, g1 = mod[0:1], mod[1:2], mod[2:3]
    h = _rms(x) * gmix_ref[...] * (1.0 + sc1) + sh1
    hb = h.astype(BF16)
    for c in range(0, IN_W, 512):
        proj_ref[:, c:c + 512] = _dot(hb, w_in_ref[:, c:c + 512])

    lb = _lower_bound(lbl_ref[...])
    row = lax.broadcasted_iota(I32, (SUB, SUB), 0)
    col = lax.broadcasted_iota(I32, (SUB, SUB), 1)
    tri = (col <= row).astype(BF16)
    mask_d = (col <= row) & ((row >= HALF) == (col >= HALF))
    top_half = lax.broadcasted_iota(I32, (SUB, DK), 0) < HALF

    for s in range(tt // SUB):
        r0 = s * SUB
        f = lb + (1.0 - lb) * _sigmoid(proj_ref[r0:r0 + SUB, C_F:C_F + KEY_W])
        kk = 1.0 - f
        hi, mid, lo = _split3(jnp.log(f))
        bc = _dot(tri, hi) + _dot(tri, mid) + _dot(tri, lo)
        for hd in range(HEADS):
            hs = slice(hd * DK, (hd + 1) * DK)
            bh = bc[:, hs]
            b31, b63 = bh[HALF // 2 - 1:HALF // 2], bh[HALF - 1:HALF]
            b95, b127 = bh[HALF + HALF // 2 - 1:HALF + HALF // 2], bh[SUB - 1:SUB]
            arg = bh - jnp.where(top_half, b31, b95)
            e_pos, e_neg = jnp.exp(arg), jnp.exp(-arg)
            q = _silu(proj_ref[r0:r0 + SUB, C_Q + hd * DK:C_Q + (hd + 1) * DK])
            v = proj_ref[r0:r0 + SUB, C_I + hd * DK:C_I + (hd + 1) * DK]
            qd = q * e_pos
            kd = kk[:, hs] * e_neg
            q_in = qd * jnp.where(top_half, jnp.exp(b31), jnp.exp(b95))
            k_end = kd * jnp.where(top_half, jnp.exp(b127 - b31), jnp.exp(b127 - b95))
            qa = jnp.where(top_half, 0.0, qd * jnp.exp(b95 - b63))
            ka = jnp.where(top_half, kd * jnp.exp(b63 - b31), 0.0)
            att = jnp.where(mask_d, _dot_nt(qd.astype(BF16), kd.astype(BF16)), 0.0)
            att = att + _dot_nt(qa.astype(BF16), ka.astype(BF16))
            vb = v.astype(BF16)
            st = st_ref[hd]
            o = _dot(att.astype(BF16), vb) + _dot_nt(q_in.astype(BF16), st.astype(BF16))
            st_ref[hd] = st * jnp.exp(b127) + _dot_tn(vb, k_end.astype(BF16))
            gate = _silu(proj_ref[r0:r0 + SUB, C_G + hd * DK:C_G + (hd + 1) * DK])
            ya_ref[r0:r0 + SUB, hs] = _rms(o) * hg_ref[:, hs] * gate

    u = proj_ref[:, C_CC:C_CC + CONV_W] * proj_ref[:, C_VB:C_VB + CONV_W]
    rows = lax.broadcasted_iota(I32, (tt, CONV_W), 0)
    c0, c1 = cbuf_ref[0:1], cbuf_ref[1:2]
    u1 = jnp.where(rows == 0, c1, pltpu.roll(u, 1, axis=0))
    u2 = jnp.where(rows == 0, c0, jnp.where(rows == 1, c1, pltpu.roll(u, 2, axis=0)))
    conv = cw_ref[0:1] * u2 + cw_ref[1:2] * u1 + cw_ref[2:3] * u + cb_ref[...]
    yb = proj_ref[:, C_BB:C_BB + CONV_W] * conv
    cbuf_ref[...] = u[tt - 2:tt]

    mixed = (_sigmoid(proj_ref[:, C_MGA:C_MGA + D_MODEL]) * _dot(ya_ref[...].astype(BF16), w_oh_ref[...])
             + _sigmoid(proj_ref[:, C_MGB:C_MGB + D_MODEL]) * _dot(yb.astype(BF16), w_oc_ref[...]))
    x1 = x + g1 * _dot(mixed.astype(BF16), w_o_ref[...])

    xmid, h2b, lgt = _ffn_pre(x1, (mod[3:4], mod[4:5], mod[5:6]), gffn_ref[...],
                              w_sgu_ref[...], w_sd_ref[...], wr_hl_ref[...])
    xmid_ref[0] = xmid
    _pack_rows(h2b, h2_ref)
    lgt_ref[...] = lgt

    @pl.when(t == nt - 1)
    def _():
        for hd in range(HEADS):
            s_out_ref[0, hd] = st_ref[hd].T
        cv_out_ref[0] = cbuf_ref[...]


def _const_spec(shape):
    nd = len(shape)
    return pl.BlockSpec(shape, lambda b, t, _nd=nd: (0,) * _nd, pipeline_mode=pl.Buffered(1))


def _mix(x, mod, h2_all, lgt_all, gmix, gffn, w_in, lbl, hg, cw, cb, w_oh, w_oc, w_o, wr_hl, w_sgu, w_sd):
    bsz, seq, _ = x.shape
    n_tok = h2_all.shape[1]
    tt = MIX_TILE
    nt = seq // tt
    consts = [gmix, gffn, w_in, lbl, hg, cw, cb, w_oh, w_oc, w_o, wr_hl, w_sgu, w_sd]
    return pl.pallas_call(
        _mix_kernel,
        grid=(bsz, nt),
        in_specs=[pl.BlockSpec((1, tt, D_MODEL), lambda b, t: (b, t, 0)),
                  pl.BlockSpec((1, 6, D_MODEL), lambda b, t: (b, 0, 0))]
                 + [_const_spec(a.shape) for a in consts]
                 + [pl.BlockSpec(memory_space=pl.ANY), pl.BlockSpec(memory_space=pl.ANY)],
        input_output_aliases={2 + len(consts): 1, 3 + len(consts): 2},
        out_specs=[pl.BlockSpec((1, tt, D_MODEL), lambda b, t: (b, t, 0)),
                   pl.BlockSpec((2, tt, PLANE_W), lambda b, t: (0, b * nt + t, 0)),
                   pl.BlockSpec((N_EXP, tt), lambda b, t: (0, b * nt + t)),
                   pl.BlockSpec((1, HEADS, DK, DK), lambda b, t: (b, 0, 0, 0)),
                   pl.BlockSpec((1, CONV_K - 1, CONV_W), lambda b, t: (b, 0, 0))],
        out_shape=[jax.ShapeDtypeStruct((bsz, seq, D_MODEL), F32),
                   jax.ShapeDtypeStruct((2, n_tok, PLANE_W), I32),
                   jax.ShapeDtypeStruct((N_EXP, n_tok), F32),
                   jax.ShapeDtypeStruct((bsz, HEADS, DK, DK), F32),
                   jax.ShapeDtypeStruct((bsz, CONV_K - 1, CONV_W), F32)],
        scratch_shapes=[pltpu.VMEM((tt, IN_W), F32),
                        pltpu.VMEM((HEADS, DK, DK), F32),
                        pltpu.VMEM((CONV_K - 1, CONV_W), F32),
                        pltpu.VMEM((tt, KEY_W), F32)],
        compiler_params=pltpu.CompilerParams(
            dimension_semantics=("arbitrary", "arbitrary"), vmem_limit_bytes=VMEM_LIMIT),
        name="mix",
    )(x, mod, *consts, h2_all, lgt_all)


def _smp1_kernel(x_ref, mod_ref, gmix_ref, w_in_ref, lbl_ref, cw_ref, cb_ref, cst_ref,
                 f_ref, k_ref, q_ref, v_ref, gate_ref, yb_ref, sga_ref, sgb_ref, cv_out_ref):
    x = x_ref[...]
    sh1, sc1 = mod_ref[:, 0:D_MODEL], mod_ref[:, D_MODEL:2 * D_MODEL]
    h = _rms(x) * gmix_ref[...] * (1.0 + sc1) + sh1
    hb = h.astype(BF16)

    def proj(c, w):
        return _dot(hb, w_in_ref[:, c:c + w])

    lb = _lower_bound(lbl_ref[...])
    f = lb + (1.0 - lb) * _sigmoid(proj(C_F, KEY_W))
    f_ref[...] = f
    k_ref[...] = 1.0 - f
    q_ref[...] = _silu(proj(C_Q, KEY_W))
    v_ref[...] = proj(C_I, KEY_W)
    gate_ref[...] = _silu(proj(C_G, KEY_W))
    u = proj(C_CC, CONV_W) * proj(C_VB, CONV_W)
    c0, c1 = cst_ref[:, 0:CONV_W], cst_ref[:, CONV_W:2 * CONV_W]
    conv = cw_ref[0:1] * c0 + cw_ref[1:2] * c1 + cw_ref[2:3] * u + cb_ref[...]
    yb_ref[...] = proj(C_BB, CONV_W) * conv
    cv_out_ref[:, 0:CONV_W] = c1
    cv_out_ref[:, CONV_W:2 * CONV_W] = u
    sga_ref[...] = _sigmoid(proj(C_MGA, D_MODEL))
    sgb_ref[...] = _sigmoid(proj(C_MGB, D_MODEL))


def _smp1(x, mod, gmix, w_in, lbl, cw, cb, cst):
    n = x.shape[0]
    kw = jax.ShapeDtypeStruct((n, KEY_W), F32)
    dm = jax.ShapeDtypeStruct((n, D_MODEL), F32)
    return pl.pallas_call(
        _smp1_kernel,
        out_shape=[kw, kw, kw, kw, kw, kw, dm, dm,
                   jax.ShapeDtypeStruct((n, (CONV_K - 1) * CONV_W), F32)],
        compiler_params=pltpu.CompilerParams(vmem_limit_bytes=VMEM_LIMIT),
        name="smp1",
    )(x, mod, gmix, w_in, lbl, cw, cb, cst)


def _smp2_kernel(f_ref, k_ref, q_ref, v_ref, s_ref, s_out_ref, o_ref):
    g = f_ref.shape[0]
    for i in range(g):
        for hd in range(HEADS):
            hs = slice(hd * DK, (hd + 1) * DK)

            def col(ref):
                return jnp.broadcast_to(ref[i:i + 1, hs], (DK, DK)).T

            vrow = v_ref[i:i + 1, hs]
            s_new = col(f_ref) * s_ref[i, hd] + col(k_ref) * vrow
            s_out_ref[i, hd] = s_new
            o_ref[i:i + 1, hs] = jnp.sum(col(q_ref) * s_new, axis=0, keepdims=True)


def _smp2(f, k, q, v, state):
    n = f.shape[0]
    g = SMP_GROUP
    row_spec = pl.BlockSpec((g, KEY_W), lambda i: (i, 0))
    st_spec = pl.BlockSpec((g, HEADS, DK, DK), lambda i: (i, 0, 0, 0))
    return pl.pallas_call(
        _smp2_kernel,
        grid=(n // g,),
        in_specs=[row_spec, row_spec, row_spec, row_spec, st_spec],
        out_specs=[st_spec, row_spec],
        out_shape=[jax.ShapeDtypeStruct(state.shape, F32), jax.ShapeDtypeStruct((n, KEY_W), F32)],
        compiler_params=pltpu.CompilerParams(dimension_semantics=("arbitrary",)),
        name="smp2",
    )(f, k, q, v, state)


def _smp3_kernel(x_ref, mod_ref, o_ref, gate_ref, yb_ref, sga_ref, sgb_ref, hg_ref, gffn_ref,
                 w_oh_ref, w_oc_ref, w_o_ref, wr_hl_ref, w_sgu_ref, w_sd_ref,
                 h2_all_ref, lgt_all_ref, xmid_ref, h2_ref, lgt_ref):
    del h2_all_ref, lgt_all_ref
    parts = []
    for hd in range(HEADS):
        hs = slice(hd * DK, (hd + 1) * DK)
        parts.append(_rms(o_ref[:, hs]) * hg_ref[:, hs] * gate_ref[:, hs])
    ya = jnp.concatenate(parts, axis=1)
    mixed = (sga_ref[...] * _dot(ya.astype(BF16), w_oh_ref[...])
             + sgb_ref[...] * _dot(yb_ref[...].astype(BF16), w_oc_ref[...]))
    g1 = mod_ref[:, 2 * D_MODEL:3 * D_MODEL]
    x1 = x_ref[...] + g1 * _dot(mixed.astype(BF16), w_o_ref[...])
    mod_rows = tuple(mod_ref[:, j * D_MODEL:(j + 1) * D_MODEL] for j in (3, 4, 5))
    xmid, h2b, lgt = _ffn_pre(x1, mod_rows, gffn_ref[...], w_sgu_ref[...], w_sd_ref[...],
                              wr_hl_ref[...])
    xmid_ref[...] = xmid
    _pack_rows(h2b, h2_ref)
    lgt_ref[...] = lgt


def _smp3(x, mod, o, gate, yb, sga, sgb, hg, gffn, w_oh, w_oc, w_o, wr_hl, w_sgu, w_sd,
          h2_all, lgt_all, n_prompt):
    n = x.shape[0]
    vmem_args = [x, mod, o, gate, yb, sga, sgb, hg, gffn, w_oh, w_oc, w_o, wr_hl, w_sgu, w_sd]
    blk = n_prompt // n

    def full(a):
        nd = a.ndim
        return pl.BlockSpec(a.shape, lambda i, _nd=nd: (0,) * _nd)

    return pl.pallas_call(
        _smp3_kernel,
        grid=(1,),
        in_specs=[full(a) for a in vmem_args]
                 + [pl.BlockSpec(memory_space=pl.ANY), pl.BlockSpec(memory_space=pl.ANY)],
        out_specs=[pl.BlockSpec((n, D_MODEL), lambda i: (0, 0)),
                   pl.BlockSpec((2, n, PLANE_W), lambda i: (0, blk, 0)),
                   pl.BlockSpec((N_EXP, n), lambda i: (0, blk))],
        out_shape=[jax.ShapeDtypeStruct((n, D_MODEL), F32),
                   jax.ShapeDtypeStruct(h2_all.shape, h2_all.dtype),
                   jax.ShapeDtypeStruct(lgt_all.shape, lgt_all.dtype)],
        input_output_aliases={len(vmem_args): 1, len(vmem_args) + 1: 2},
        compiler_params=pltpu.CompilerParams(
            dimension_semantics=("arbitrary",), vmem_limit_bytes=VMEM_LIMIT),
        name="smp3",
    )(*vmem_args, h2_all, lgt_all)


def _route_kernel(lgt_ref, bias_ref, idx_ref, w_ref, rank_ref, cnt_ref):
    tr = ROUTE_TILE
    n_tiles = lgt_ref.shape[1] // tr

    def tile(i, carry):
        cols = pl.ds(pl.multiple_of(i * tr, tr), tr)
        picks, weights, ranks, carry = _route_tile(lgt_ref[:, cols], bias_ref[...], carry)
        for k in range(TOP_K):
            idx_ref[k:k + 1, cols] = picks[k]
            w_ref[k:k + 1, cols] = weights[k]
            rank_ref[k:k + 1, cols] = ranks[k]
        return carry

    total = lax.fori_loop(0, n_tiles, tile, jnp.zeros((N_EXP, 1), F32))
    cnt_ref[...] = jnp.broadcast_to(total, cnt_ref.shape).astype(I32)


def _route_tile(lgt, bias, carry):
    tr = lgt.shape[1]
    neg = -jnp.inf
    scores = _sigmoid(lgt)
    sel = scores + bias
    j8 = lax.broadcasted_iota(I32, (GRP_SZ, tr), 0)
    groups = [sel[g * GRP_SZ:(g + 1) * GRP_SZ] for g in range(N_GRP)]
    gscore = []
    for grp in groups:
        m1 = jnp.max(grp, axis=0, keepdims=True)
        i1 = jnp.min(jnp.where(grp == m1, j8, GRP_SZ), axis=0, keepdims=True)
        m2 = jnp.max(jnp.where(j8 == i1, neg, grp), axis=0, keepdims=True)
        gscore.append(m1 + m2)
    kept = []
    for g in range(N_GRP):
        beaten = jnp.zeros((1, tr), I32)
        for o in range(N_GRP):
            if o < g:
                beaten = beaten + (gscore[o] >= gscore[g]).astype(I32)
            elif o > g:
                beaten = beaten + (gscore[o] > gscore[g]).astype(I32)
        kept.append(jnp.where(beaten < TOPK_GRP, groups[g], neg))
    masked = jnp.concatenate(kept, axis=0)
    ei = lax.broadcasted_iota(I32, masked.shape, 0)
    chosen = jnp.zeros(masked.shape, jnp.bool_)
    picks, weights = [], []
    for _ in range(TOP_K):
        m = jnp.max(masked, axis=0, keepdims=True)
        pick = jnp.min(jnp.where(masked == m, ei, N_EXP), axis=0, keepdims=True)
        hit = ei == pick
        weights.append(jnp.sum(jnp.where(hit, scores, 0.0), axis=0, keepdims=True))
        picks.append(pick)
        chosen = chosen | hit
        masked = jnp.where(hit, neg, masked)
    wsum = weights[0]
    for w in weights[1:]:
        wsum = wsum + w
    sel01 = chosen.astype(F32)
    r = lax.broadcasted_iota(I32, (tr, tr), 0)
    c = lax.broadcasted_iota(I32, (tr, tr), 1)
    before = (r < c).astype(BF16)
    cnt = _dot(sel01.astype(BF16), before) + carry
    weights = [w / wsum * ROUTED_SCALE for w in weights]
    ranks = [jnp.sum(jnp.where(ei == p, cnt, 0.0), axis=0, keepdims=True).astype(I32) for p in picks]
    return picks, weights, ranks, carry + jnp.sum(sel01, axis=1, keepdims=True)


def _route(lgt, bias):
    n = lgt.shape[1]
    assert n % ROUTE_TILE == 0
    slot = lambda dt: jax.ShapeDtypeStruct((TOP_K, n), dt)
    return pl.pallas_call(
        _route_kernel,
        out_shape=[slot(I32), slot(F32), slot(I32),
                   jax.ShapeDtypeStruct((N_EXP, 128), I32)],
        name="route",
    )(lgt, bias)


def _dest_kernel(start_ref, idx_ref, rank_ref, dest_ref, *, n_rows):
    n_tok = idx_ref.shape[1]
    idx = idx_ref[...]
    acc = rank_ref[...]
    for e in range(N_EXP):
        acc = acc + jnp.where(idx == e, start_ref[e], 0)
    dest_ref[:, 0:n_tok] = acc
    dest_ref[:, n_tok:2 * n_tok] = acc + n_rows


def _dest(pad_start, idx, rank, n_rows):
    k, n_tok = idx.shape
    return pl.pallas_call(
        functools.partial(_dest_kernel, n_rows=n_rows),
        in_specs=[pl.BlockSpec(memory_space=pltpu.SMEM),
                  pl.BlockSpec(memory_space=pltpu.VMEM),
                  pl.BlockSpec(memory_space=pltpu.VMEM)],
        out_specs=pl.BlockSpec(memory_space=pltpu.VMEM),
        out_shape=jax.ShapeDtypeStruct((k, 2 * n_tok), I32),
        name="dest",
    )(pad_start, idx, rank)


def _sc_mesh():
    return plsc.VectorSubcoreMesh(core_axis_name="core", subcore_axis_name="subcore")


def _dispatch(rows, dest, n_out):
    n, width = rows.shape
    win = SC_WINDOW
    steps = n // win

    @pl.kernel(out_type=jax.ShapeDtypeStruct((n_out, width), rows.dtype), mesh=_sc_mesh(),
               scratch_types=[], name="dispatch")
    def run(x_hbm, *refs):
        i_hbms, o_hbm = refs[:TOP_K], refs[TOP_K]

        def body(x_vmem, *i_vmems):
            for i_vmem in i_vmems:
                pltpu.sync_copy(x_vmem, o_hbm.at[i_vmem.at[0]])

        pltpu.emit_pipeline(
            body,
            grid=(steps,),
            in_specs=[pl.BlockSpec((win, width), lambda i: (i, 0))]
                     + [pl.BlockSpec((1, win), lambda i, k=k: (0, k * steps + i)) for k in range(TOP_K)],
            out_specs=[],
            core_axis_name=("core", "subcore"),
            dimension_semantics=(pltpu.PARALLEL,),
        )(x_hbm, *i_hbms)

    dest_flat = dest.reshape(1, TOP_K * n)
    return run(rows, *([dest_flat] * TOP_K))


def _combine(rows, dest_flat):
    width = rows.shape[1]
    n = dest_flat.shape[0]
    win = SC_WINDOW

    @pl.kernel(out_type=jax.ShapeDtypeStruct((n, width), rows.dtype), mesh=_sc_mesh(),
               scratch_types=[], name="combine")
    def run(y_hbm, i_hbm, o_hbm):
        def body(i_vmem, o_vmem):
            pltpu.sync_copy(y_hbm.at[i_vmem.at[0]], o_vmem)

        pltpu.emit_pipeline(
            body,
            grid=(n // win,),
            in_specs=[pl.BlockSpec((1, win), lambda i: (0, i))],
            out_specs=[pl.BlockSpec((win, width), lambda i: (i, 0))],
            core_axis_name=("core", "subcore"),
            dimension_semantics=(pltpu.PARALLEL,),
        )(i_hbm, o_hbm)

    return run(rows, dest_flat.reshape(1, n))


def _gmm_kernel(blk_exp_ref, n_used_ref, xs_hbm, wg_hbm, wu_hbm, wd_hbm, ys_hbm,
                xbuf, ybuf, wg32, wu32, wd32, wgu_b, wd_b, xsem, ysem, wsem, run_ref):
    bm = xbuf.shape[2]
    n_used = n_used_ref[0]

    def x_copy(b, slot):
        return pltpu.make_async_copy(xs_hbm.at[:, pl.ds(b * bm, bm), :], xbuf.at[slot], xsem.at[slot])

    def y_copy(b, slot):
        return pltpu.make_async_copy(ybuf.at[slot], ys_hbm.at[:, pl.ds(b * bm, bm), :], ysem.at[slot])

    def w_copies(e, slot):
        return (pltpu.make_async_copy(wg_hbm.at[e], wg32.at[slot], wsem.at[slot, 0]),
                pltpu.make_async_copy(wu_hbm.at[e], wu32.at[slot], wsem.at[slot, 1]),
                pltpu.make_async_copy(wd_hbm.at[e], wd32.at[slot], wsem.at[slot, 2]))

    def run_end(b):
        return lax.while_loop(lambda j: (j < n_used) & (blk_exp_ref[jnp.minimum(j, n_used - 1)] == blk_exp_ref[b]),
                              lambda j: j + 1, b + 1)

    run_ref[0] = 0
    x_copy(0, 0).start()
    for c in w_copies(blk_exp_ref[0], 0):
        c.start()

    def block(b, carry):
        slot = b % 2

        @pl.when(b + 1 < n_used)
        def _():
            x_copy(b + 1, 1 - slot).start()

        @pl.when((b == 0) | (blk_exp_ref[b] != blk_exp_ref[jnp.maximum(b - 1, 0)]))
        def _():
            wslot = run_ref[0] % 2
            run_ref[0] = run_ref[0] + 1
            for c in w_copies(blk_exp_ref[b], wslot):
                c.wait()
            wgu_b[:, 0:EXP_FF] = wg32[wslot].astype(BF16)
            wgu_b[:, EXP_FF:2 * EXP_FF] = wu32[wslot].astype(BF16)
            wd_b[...] = wd32[wslot].astype(BF16)
            nxt = run_end(b)

            @pl.when(nxt < n_used)
            def _():
                for c in w_copies(blk_exp_ref[jnp.minimum(nxt, n_used - 1)], 1 - wslot):
                    c.start()

        x_copy(b, slot).wait()

        @pl.when(b >= 2)
        def _():
            y_copy(b - 2, slot).wait()

        xc = _unpack_rows(xbuf[slot, 0], xbuf[slot, 1])
        gu = sum(_dot(c, wgu_b[i * PLANE_W:(i + 1) * PLANE_W, :]) for i, c in enumerate(xc))
        act = (_silu(gu[:, :EXP_FF]) * gu[:, EXP_FF:]).astype(BF16)
        _pack_rows(_dot(act, wd_b[...]).astype(BF16), ybuf.at[slot])
        y_copy(b, slot).start()
        return carry

    lax.fori_loop(0, n_used, block, 0)

    @pl.when(n_used >= 2)
    def _():
        y_copy(n_used - 2, n_used % 2).wait()

    y_copy(n_used - 1, (n_used - 1) % 2).wait()


def _gmm(blk_exp, n_used, xs, w_gate, w_up, w_down):
    n_rows = xs.shape[1]
    bm = GMM_BM
    nb = n_rows // bm

    assert blk_exp.shape == (nb,)
    any_spec = pl.BlockSpec(memory_space=pl.ANY)
    grid_spec = pltpu.PrefetchScalarGridSpec(
        num_scalar_prefetch=2,
        grid=(1,),
        in_specs=[any_spec, any_spec, any_spec, any_spec],
        out_specs=any_spec,
        scratch_shapes=[pltpu.VMEM((2, 2, bm, PLANE_W), I32), pltpu.VMEM((2, 2, bm, PLANE_W), I32),
                        pltpu.VMEM((2, D_MODEL, EXP_FF), F32), pltpu.VMEM((2, D_MODEL, EXP_FF), F32),
                        pltpu.VMEM((2, EXP_FF, D_MODEL), F32),
                        pltpu.VMEM((D_MODEL, 2 * EXP_FF), BF16), pltpu.VMEM((EXP_FF, D_MODEL), BF16),
                        pltpu.SemaphoreType.DMA((2,)), pltpu.SemaphoreType.DMA((2,)),
                        pltpu.SemaphoreType.DMA((2, 3)), pltpu.SMEM((1,), I32)],
    )
    return pl.pallas_call(
        _gmm_kernel,
        grid_spec=grid_spec,
        out_shape=jax.ShapeDtypeStruct((2, n_rows, PLANE_W), I32),
        compiler_params=pltpu.CompilerParams(dimension_semantics=("arbitrary",)),
        name="gmm",
    )(blk_exp, n_used, xs, w_gate, w_up, w_down)


def _final_kernel(xmid_ref, g2_ref, z_ref, w_ref, gfin_ref, *rest):
    y_ref = rest[-1]
    accs = [jnp.zeros((xmid_ref.shape[0], PLANE_W), F32) for _ in range(4)]
    w_cols = w_ref[...].T
    for k in range(TOP_K):
        wk = w_cols[:, k:k + 1]
        cols = _unpack_rows(z_ref[k, 0], z_ref[k, 1])
        accs = [a + wk * c.astype(F32) for a, c in zip(accs, cols)]
    acc = jnp.concatenate(accs, axis=1)
    y_ref[...] = _rms(xmid_ref[...] + g2_ref[0] * acc) * gfin_ref[...]


def _final(xmid, g2, z, w_t, gfin, tile, *, n_tiles, x_tile0, z_tile0, w_tile0, tiles_per_g2, y_prev=None):
    args = [xmid, g2, z, w_t, gfin]
    in_specs = [pl.BlockSpec((tile, D_MODEL), lambda i: (x_tile0 + i, 0)),
                pl.BlockSpec((1, g2.shape[1], D_MODEL), lambda i: ((x_tile0 + i) // tiles_per_g2, 0, 0)),
                pl.BlockSpec((TOP_K, 2, tile, PLANE_W), lambda i: (0, 0, z_tile0 + i, 0)),
                pl.BlockSpec((TOP_K, tile), lambda i: (0, w_tile0 + i)),
                pl.BlockSpec((1, D_MODEL), lambda i: (0, 0))]
    aliases = {}
    if y_prev is not None:
        args.append(y_prev)
        in_specs.append(pl.BlockSpec(memory_space=pl.ANY))
        aliases = {len(args) - 1: 0}
    return pl.pallas_call(
        _final_kernel,
        grid=(n_tiles,),
        in_specs=in_specs,
        out_specs=pl.BlockSpec((tile, D_MODEL), lambda i: (x_tile0 + i, 0)),
        out_shape=jax.ShapeDtypeStruct(xmid.shape, F32),
        input_output_aliases=aliases,
        compiler_params=pltpu.CompilerParams(dimension_semantics=("arbitrary",)),
        name="final",
    )(*args)


def kernel(x_prompt, x_sample, state_hgrn, state_conv, c_prompt, c_sample, w_ada, b_ada, norm_mix_g, norm_ffn_g, w_in, lb_logits, hgrn_norm_g, conv_w, conv_b, w_out_hgrn, w_out_conv, w_o, w_router, router_bias, w_exp_gate, w_exp_up, w_exp_down, w_sh_gate, w_sh_up, w_sh_down, final_norm_g):
    assert w_ada.shape[0] == 1 and lb_logits.shape[0] == 2
    bsz, seq, _ = x_prompt.shape
    n_smp = x_sample.shape[0]
    n_prompt = bsz * seq
    n_tok = n_prompt + n_smp

    w_in_b = w_in[0].astype(BF16)
    w_oh_b = w_out_hgrn[0].astype(BF16)
    w_oc_b = w_out_conv[0].astype(BF16)
    w_o_b = w_o[0].astype(BF16)
    wr_t = w_router[0].T
    wr_hi = wr_t.astype(BF16)
    wr_hl = jnp.concatenate([wr_hi, (wr_t - wr_hi.astype(F32)).astype(BF16)], axis=0)
    w_sgu = jnp.concatenate([w_sh_gate[0], w_sh_up[0]], axis=1).astype(BF16)
    w_sd = w_sh_down[0].astype(BF16)
    gmix = norm_mix_g[0].reshape(1, D_MODEL)
    gffn = norm_ffn_g[0].reshape(1, D_MODEL)
    hg = hgrn_norm_g[0].reshape(1, KEY_W)
    cw = conv_w[0]
    cb = conv_b[0].reshape(1, CONV_W)
    gfin = final_norm_g.reshape(1, D_MODEL)

    mod = _ada(jnp.concatenate([c_prompt, c_sample], axis=0), w_ada[0], b_ada[0])
    mod_p, mod_s = mod[:bsz], mod[bsz:]

    xmid_p, h2_all, lgt_all, s_p, cv_p = _mix(
        x_prompt, mod_p.reshape(bsz, 6, D_MODEL),
        jnp.zeros((2, n_tok, PLANE_W), I32), jnp.zeros((N_EXP, n_tok), F32),
        gmix, gffn, w_in_b, lb_logits, hg, cw, cb, w_oh_b, w_oc_b, w_o_b, wr_hl, w_sgu, w_sd)

    xs2 = x_sample.reshape(n_smp, D_MODEL)
    f, kk, q, v, gate, yb, sga, sgb, cv_s = _smp1(
        xs2, mod_s, gmix, w_in_b, lb_logits, cw, cb, state_conv[0].reshape(n_smp, (CONV_K - 1) * CONV_W))
    s_s, o_s = _smp2(f, kk, q, v, state_hgrn[0])
    xmid_s, h2_all, lgt_all = _smp3(xs2, mod_s, o_s, gate, yb, sga, sgb, hg, gffn,
                                    w_oh_b, w_oc_b, w_o_b, wr_hl, w_sgu, w_sd,
                                    h2_all, lgt_all, n_prompt)

    idx, w_tok, rank, cnt = _route(lgt_all, router_bias[0].reshape(N_EXP, 1))

    bm = GMM_BM
    n_blocks = (n_tok * TOP_K + N_EXP * (bm - 1)) // bm
    n_rows = n_blocks * bm
    counts = cnt[:, 0]
    padded = (counts + bm - 1) // bm * bm
    pad_end = jnp.cumsum(padded)
    pad_start = pad_end - padded
    blk_row0 = jnp.arange(n_blocks, dtype=I32) * bm
    blk_exp = jnp.minimum(jnp.sum((pad_end[None, :] <= blk_row0[:, None]).astype(I32), axis=1), N_EXP - 1)
    n_used = (pad_end[-1:] // bm).astype(I32)
    dest = _dest(pad_start.astype(I32), idx, rank, n_rows)

    xs = _dispatch(h2_all.reshape(2 * n_tok, PLANE_W), dest, 2 * n_rows).reshape(2, n_rows, PLANE_W)
    ys = _gmm(blk_exp, n_used, xs, w_exp_gate[0], w_exp_up[0], w_exp_down[0])

    ys_flat = ys.reshape(2 * n_rows, PLANE_W)
    dest3 = dest.reshape(TOP_K, 2, n_tok)
    w_t = w_tok
    xmid_p2 = xmid_p.reshape(n_prompt, D_MODEL)
    g2_p = mod_p[:, 5 * D_MODEL:].reshape(bsz, 1, D_MODEL)
    g2_s = mod_s[:, 5 * D_MODEL:].reshape(1, n_smp, D_MODEL)
    chunk = n_prompt // FINAL_CHUNKS
    y_p = None
    for c in range(FINAL_CHUNKS):
        c0 = c * chunk
        c1 = n_tok if c == FINAL_CHUNKS - 1 else c0 + chunk
        z = _combine(ys_flat, dest3[:, :, c0:c1].reshape(-1)).reshape(TOP_K, 2, c1 - c0, PLANE_W)
        y_p = _final(xmid_p2, g2_p, z, w_t, gfin, FINAL_TILE, n_tiles=chunk // FINAL_TILE,
                     x_tile0=c0 // FINAL_TILE, z_tile0=0, w_tile0=c0 // FINAL_TILE,
                     tiles_per_g2=seq // FINAL_TILE, y_prev=y_p)
    y_s = _final(xmid_s, g2_s, z, w_t, gfin, n_smp, n_tiles=1, x_tile0=0, z_tile0=chunk // n_smp,
                 w_tile0=n_prompt // n_smp, tiles_per_g2=1)

    return (y_p.reshape(bsz, seq, D_MODEL), y_s.reshape(n_smp, 1, D_MODEL),
            s_p[None], cv_p[None], s_s[None], cv_s.reshape(1, n_smp, CONV_K - 1, CONV_W))
```

```python
import functools

import jax
import jax.numpy as jnp
from jax import lax
from jax.experimental import pallas as pl
from jax.experimental.pallas import tpu as pltpu
from jax.experimental.pallas import tpu_sc as plsc

F32 = jnp.float32
BF16 = jnp.bfloat16
I32 = jnp.int32

D_MODEL = 1024
HALF_D = D_MODEL // 2
HEADS = 4
DK = 128
KEY_W = HEADS * DK
CONV_W = 512
CONV_K = 3
IN_W = 2 * KEY_W + 2 * KEY_W + 3 * CONV_W + 2 * D_MODEL
N_EXP = 64
TOP_K = 8
N_GRP = 8
GRP_SZ = N_EXP // N_GRP
TOPK_GRP = 4
EXP_FF = 256
SH_FF = 256
ROUTED_SCALE = 2.5
EPS = 1e-6

C_Q, C_F, C_I, C_G = 0, 512, 1024, 1536
C_BB, C_CC, C_VB = 2048, 2560, 3072
C_MGA, C_MGB = 3584, 4608

MIX_TILE = 512
SUB = 128
HALF = SUB // 2
ROUTE_TILE = 384
GMM_BM = 512
FINAL_TILE = 256
FINAL_CHUNKS = 4
SMP_GROUP = 8
SC_WINDOW = 128
PLANE_W = HALF_D // 2
VMEM_LIMIT = 56 * 1024 * 1024


def _dot(a, b):
    return jnp.dot(a, b, preferred_element_type=F32)


def _dot_nt(a, b):
    return lax.dot_general(a, b, (((1,), (1,)), ((), ())), preferred_element_type=F32)


def _dot_tn(a, b):
    return lax.dot_general(a, b, (((0,), (0,)), ((), ())), preferred_element_type=F32)


def _sigmoid(x):
    return 0.5 * jnp.tanh(0.5 * x) + 0.5


def _silu(x):
    h = 0.5 * x
    return h * jnp.tanh(h) + h


def _rms(x):
    return x * lax.rsqrt(jnp.mean(x * x, axis=-1, keepdims=True) + EPS)


def _lower_bound(lbl):
    a, b = lbl[0:1], lbl[1:2]
    m = jnp.maximum(a, b)
    ea, eb = jnp.exp(a - m), jnp.exp(b - m)
    return ea / (ea + eb)


def _split3(x):
    hi = x.astype(BF16)
    r1 = x - hi.astype(F32)
    mid = r1.astype(BF16)
    lo = (r1 - mid.astype(F32)).astype(BF16)
    return hi, mid, lo


def _pack_rows(xb, out_ref):
    bits = lax.bitcast_convert_type(xb.astype(F32), I32)
    lo = lax.shift_right_logical(bits[:, :HALF_D], 16)
    hi = bits[:, HALF_D:] & jnp.int32(-65536)
    words = lo | hi
    out_ref[0] = words[:, :PLANE_W]
    out_ref[1] = words[:, PLANE_W:]


def _unpack_rows(p0, p1):
    def halves(w):
        lo = lax.bitcast_convert_type(lax.shift_left(w, 16), F32)
        hi = lax.bitcast_convert_type(w & jnp.int32(-65536), F32)
        return lo.astype(BF16), hi.astype(BF16)

    c0, c2 = halves(p0)
    c1, c3 = halves(p1)
    return c0, c1, c2, c3


def _ada_kernel(c_ref, w_ref, b_ref, o_ref):
    a = _silu(c_ref[...]).astype(BF16)
    o_ref[...] = _dot(a, w_ref[...].astype(BF16)) + b_ref[...]


def _ada(c_all, w_ada, b_ada):
    n = c_all.shape[0]
    blk = 1024
    return pl.pallas_call(
        _ada_kernel,
        grid=(6 * D_MODEL // blk,),
        in_specs=[pl.BlockSpec((n, D_MODEL), lambda j: (0, 0)),
                  pl.BlockSpec((D_MODEL, blk), lambda j: (0, j)),
                  pl.BlockSpec((1, blk), lambda j: (0, j))],
        out_specs=pl.BlockSpec((n, blk), lambda j: (0, j)),
        out_shape=jax.ShapeDtypeStruct((n, 6 * D_MODEL), F32),
        name="ada",
    )(c_all, w_ada, b_ada.reshape(1, -1))


def _ffn_pre(x1, mod_rows, gffn, w_sgu, w_sd, wr_hl):
    sh2, sc2, g2 = mod_rows
    h2 = _rms(x1) * gffn * (1.0 + sc2) + sh2
    h2b = h2.astype(BF16)
    gu = _dot(h2b, w_sgu)
    act = _silu(gu[:, :SH_FF]) * gu[:, SH_FF:]
    xmid = x1 + g2 * _dot(act.astype(BF16), w_sd)
    h2lo = (h2 - h2b.astype(F32)).astype(BF16)
    both = _dot_nt(wr_hl, h2b)
    lgt = both[:N_EXP] + both[N_EXP:] + _dot_nt(wr_hl[:N_EXP], h2lo)
    return xmid, h2b, lgt


def _mix_kernel(x_ref, mod_ref, gmix_ref, gffn_ref, w_in_ref, lbl_ref, hg_ref, cw_ref, cb_ref,
                w_oh_ref, w_oc_ref, w_o_ref, wr_hl_ref, w_sgu_ref, w_sd_ref,
                h2_all_ref, lgt_all_ref,
                xmid_ref, h2_ref, lgt_ref, s_out_ref, cv_out_ref,
                proj_ref, st_ref, cbuf_ref, ya_ref):
    del h2_all_ref, lgt_all_ref
    t = pl.program_id(1)
    nt = pl.num_programs(1)
    tt = x_ref.shape[1]

    @pl.when(t == 0)
    def _():
        st_ref[...] = jnp.zeros_like(st_ref)
        cbuf_ref[...] = jnp.zeros_like(cbuf_ref)

    x = x_ref[0]
    mod = mod_ref[0]
    sh1, sc1, g1 = mod[0:1], mod[1:2], mod[2:3]
    h = _rms(x) * gmix_ref[...] * (1.0 + sc1) + sh1
    hb = h.astype(BF16)
    for c in range(0, IN_W, 512):
        proj_ref[:, c:c + 512] = _dot(hb, w_in_ref[:, c:c + 512])

    lb = _lower_bound(lbl_ref[...])
    row = lax.broadcasted_iota(I32, (SUB, SUB), 0)
    col = lax.broadcasted_iota(I32, (SUB, SUB), 1)
    tri = (col <= row).astype(BF16)
    mask_d = (col <= row) & ((row >= HALF) == (col >= HALF))
    top_half = lax.broadcasted_iota(I32, (SUB, DK), 0) < HALF

    for s in range(tt // SUB):
        r0 = s * SUB
        f = lb + (1.0 - lb) * _sigmoid(proj_ref[r0:r0 + SUB, C_F:C_F + KEY_W])
        kk = 1.0 - f
        hi, mid, lo = _split3(jnp.log(f))
        bc = _dot(tri, hi) + _dot(tri, mid) + _dot(tri, lo)
        for hd in range(HEADS):
            hs = slice(hd * DK, (hd + 1) * DK)
            bh = bc[:, hs]
            b31, b63 = bh[HALF // 2 - 1:HALF // 2], bh[HALF - 1:HALF]
            b95, b127 = bh[HALF + HALF // 2 - 1:HALF + HALF // 2], bh[SUB - 1:SUB]
            arg = bh - jnp.where(top_half, b31, b95)
            e_pos, e_neg = jnp.exp(arg), jnp.exp(-arg)
            q = _silu(proj_ref[r0:r0 + SUB, C_Q + hd * DK:C_Q + (hd + 1) * DK])
            v = proj_ref[r0:r0 + SUB, C_I + hd * DK:C_I + (hd + 1) * DK]
            qd = q * e_pos
            kd = kk[:, hs] * e_neg
            q_in = qd * jnp.where(top_half, jnp.exp(b31), jnp.exp(b95))
            k_end = kd * jnp.where(top_half, jnp.exp(b127 - b31), jnp.exp(b127 - b95))
            qa = jnp.where(top_half, 0.0, qd * jnp.exp(b95 - b63))
            ka = jnp.where(top_half, kd * jnp.exp(b63 - b31), 0.0)
            att = jnp.where(mask_d, _dot_nt(qd.astype(BF16), kd.astype(BF16)), 0.0)
            att = att + _dot_nt(qa.astype(BF16), ka.astype(BF16))
            vb = v.astype(BF16)
            st = st_ref[hd]
            o = _dot(att.astype(BF16), vb) + _dot_nt(q_in.astype(BF16), st.astype(BF16))
            st_ref[hd] = st * jnp.exp(b127) + _dot_tn(vb, k_end.astype(BF16))
            gate = _silu(proj_ref[r0:r0 + SUB, C_G + hd * DK:C_G + (hd + 1) * DK])
            ya_ref[r0:r0 + SUB, hs] = _rms(o) * hg_ref[:, hs] * gate

    u = proj_ref[:, C_CC:C_CC + CONV_W] * proj_ref[:, C_VB:C_VB + CONV_W]
    rows = lax.broadcasted_iota(I32, (tt, CONV_W), 0)
    c0, c1 = cbuf_ref[0:1], cbuf_ref[1:2]
    u1 = jnp.where(rows == 0, c1, pltpu.roll(u, 1, axis=0))
    u2 = jnp.where(rows == 0, c0, jnp.where(rows == 1, c1, pltpu.roll(u, 2, axis=0)))
    conv = cw_ref[0:1] * u2 + cw_ref[1:2] * u1 + cw_ref[2:3] * u + cb_ref[...]
    yb = proj_ref[:, C_BB:C_BB + CONV_W] * conv
    cbuf_ref[...] = u[tt - 2:tt]

    mixed = (_sigmoid(proj_ref[:, C_MGA:C_MGA + D_MODEL]) * _dot(ya_ref[...].astype(BF16), w_oh_ref[...])
             + _sigmoid(proj_ref[:, C_MGB:C_MGB + D_MODEL]) * _dot(yb.astype(BF16), w_oc_ref[...]))
    x1 = x + g1 * _dot(mixed.astype(BF16), w_o_ref[...])

    xmid, h2b, lgt = _ffn_pre(x1, (mod[3:4], mod[4:5], mod[5:6]), gffn_ref[...],
                              w_sgu_ref[...], w_sd_ref[...], wr_hl_ref[...])
    xmid_ref[0] = xmid
    _pack_rows(h2b, h2_ref)
    lgt_ref[...] = lgt

    @pl.when(t == nt - 1)
    def _():
        for hd in range(HEADS):
            s_out_ref[0, hd] = st_ref[hd].T
        cv_out_ref[0] = cbuf_ref[...]


def _const_spec(shape):
    nd = len(shape)
    return pl.BlockSpec(shape, lambda b, t, _nd=nd: (0,) * _nd, pipeline_mode=pl.Buffered(1))


def _mix(x, mod, h2_all, lgt_all, gmix, gffn, w_in, lbl, hg, cw, cb, w_oh, w_oc, w_o, wr_hl, w_sgu, w_sd):
    bsz, seq, _ = x.shape
    n_tok = h2_all.shape[1]
    tt = MIX_TILE
    nt = seq // tt
    consts = [gmix, gffn, w_in, lbl, hg, cw, cb, w_oh, w_oc, w_o, wr_hl, w_sgu, w_sd]
    return pl.pallas_call(
        _mix_kernel,
        grid=(bsz, nt),
        in_specs=[pl.BlockSpec((1, tt, D_MODEL), lambda b, t: (b, t, 0)),
                  pl.BlockSpec((1, 6, D_MODEL), lambda b, t: (b, 0, 0))]
                 + [_const_spec(a.shape) for a in consts]
                 + [pl.BlockSpec(memory_space=pl.ANY), pl.BlockSpec(memory_space=pl.ANY)],
        input_output_aliases={2 + len(consts): 1, 3 + len(consts): 2},
        out_specs=[pl.BlockSpec((1, tt, D_MODEL), lambda b, t: (b, t, 0)),
                   pl.BlockSpec((2, tt, PLANE_W), lambda b, t: (0, b * nt + t, 0)),
                   pl.BlockSpec((N_EXP, tt), lambda b, t: (0, b * nt + t)),
                   pl.BlockSpec((1, HEADS, DK, DK), lambda b, t: (b, 0, 0, 0)),
                   pl.BlockSpec((1, CONV_K - 1, CONV_W), lambda b, t: (b, 0, 0))],
        out_shape=[jax.ShapeDtypeStruct((bsz, seq, D_MODEL), F32),
                   jax.ShapeDtypeStruct((2, n_tok, PLANE_W), I32),
                   jax.ShapeDtypeStruct((N_EXP, n_tok), F32),
                   jax.ShapeDtypeStruct((bsz, HEADS, DK, DK), F32),
                   jax.ShapeDtypeStruct((bsz, CONV_K - 1, CONV_W), F32)],
        scratch_shapes=[pltpu.VMEM((tt, IN_W), F32),
                        pltpu.VMEM((HEADS, DK, DK), F32),
                        pltpu.VMEM((CONV_K - 1, CONV_W), F32),
                        pltpu.VMEM((tt, KEY_W), F32)],
        compiler_params=pltpu.CompilerParams(
            dimension_semantics=("arbitrary", "arbitrary"), vmem_limit_bytes=VMEM_LIMIT),
        name="mix",
    )(x, mod, *consts, h2_all, lgt_all)


def _smp1_kernel(x_ref, mod_ref, gmix_ref, w_in_ref, lbl_ref, cw_ref, cb_ref, cst_ref,
                 f_ref, k_ref, q_ref, v_ref, gate_ref, yb_ref, sga_ref, sgb_ref, cv_out_ref):
    x = x_ref[...]
    sh1, sc1 = mod_ref[:, 0:D_MODEL], mod_ref[:, D_MODEL:2 * D_MODEL]
    h = _rms(x) * gmix_ref[...] * (1.0 + sc1) + sh1
    hb = h.astype(BF16)

    def proj(c, w):
        return _dot(hb, w_in_ref[:, c:c + w])

    lb = _lower_bound(lbl_ref[...])
    f = lb + (1.0 - lb) * _sigmoid(proj(C_F, KEY_W))
    f_ref[...] = f
    k_ref[...] = 1.0 - f
    q_ref[...] = _silu(proj(C_Q, KEY_W))
    v_ref[...] = proj(C_I, KEY_W)
    gate_ref[...] = _silu(proj(C_G, KEY_W))
    u = proj(C_CC, CONV_W) * proj(C_VB, CONV_W)
    c0, c1 = cst_ref[:, 0:CONV_W], cst_ref[:, CONV_W:2 * CONV_W]
    conv = cw_ref[0:1] * c0 + cw_ref[1:2] * c1 + cw_ref[2:3] * u + cb_ref[...]
    yb_ref[...] = proj(C_BB, CONV_W) * conv
    cv_out_ref[:, 0:CONV_W] = c1
    cv_out_ref[:, CONV_W:2 * CONV_W] = u
    sga_ref[...] = _sigmoid(proj(C_MGA, D_MODEL))
    sgb_ref[...] = _sigmoid(proj(C_MGB, D_MODEL))


def _smp1(x, mod, gmix, w_in, lbl, cw, cb, cst):
    n = x.shape[0]
    kw = jax.ShapeDtypeStruct((n, KEY_W), F32)
    dm = jax.ShapeDtypeStruct((n, D_MODEL), F32)
    return pl.pallas_call(
        _smp1_kernel,
        out_shape=[kw, kw, kw, kw, kw, kw, dm, dm,
                   jax.ShapeDtypeStruct((n, (CONV_K - 1) * CONV_W), F32)],
        compiler_params=pltpu.CompilerParams(vmem_limit_bytes=VMEM_LIMIT),
        name="smp1",
    )(x, mod, gmix, w_in, lbl, cw, cb, cst)


def _smp2_kernel(f_ref, k_ref, q_ref, v_ref, s_ref, s_out_ref, o_ref):
    g = f_ref.shape[0]
    for i in range(g):
        for hd in range(HEADS):
            hs = slice(hd * DK, (hd + 1) * DK)

            def col(ref):
                return jnp.broadcast_to(ref[i:i + 1, hs], (DK, DK)).T

            vrow = v_ref[i:i + 1, hs]
            s_new = col(f_ref) * s_ref[i, hd] + col(k_ref) * vrow
            s_out_ref[i, hd] = s_new
            o_ref[i:i + 1, hs] = jnp.sum(col(q_ref) * s_new, axis=0, keepdims=True)


def _smp2(f, k, q, v, state):
    n = f.shape[0]
    g = SMP_GROUP
    row_spec = pl.BlockSpec((g, KEY_W), lambda i: (i, 0))
    st_spec = pl.BlockSpec((g, HEADS, DK, DK), lambda i: (i, 0, 0, 0))
    return pl.pallas_call(
        _smp2_kernel,
        grid=(n // g,),
        in_specs=[row_spec, row_spec, row_spec, row_spec, st_spec],
        out_specs=[st_spec, row_spec],
        out_shape=[jax.ShapeDtypeStruct(state.shape, F32), jax.ShapeDtypeStruct((n, KEY_W), F32)],
        compiler_params=pltpu.CompilerParams(dimension_semantics=("arbitrary",)),
        name="smp2",
    )(f, k, q, v, state)


def _smp3_kernel(x_ref, mod_ref, o_ref, gate_ref, yb_ref, sga_ref, sgb_ref, hg_ref, gffn_ref,
                 w_oh_ref, w_oc_ref, w_o_ref, wr_hl_ref, w_sgu_ref, w_sd_ref,
                 h2_all_ref, lgt_all_ref, xmid_ref, h2_ref, lgt_ref):
    del h2_all_ref, lgt_all_ref
    parts = []
    for hd in range(HEADS):
        hs = slice(hd * DK, (hd + 1) * DK)
        parts.append(_rms(o_ref[:, hs]) * hg_ref[:, hs] * gate_ref[:, hs])
    ya = jnp.concatenate(parts, axis=1)
    mixed = (sga_ref[...] * _dot(ya.astype(BF16), w_oh_ref[...])
             + sgb_ref[...] * _dot(yb_ref[...].astype(BF16), w_oc_ref[...]))
    g1 = mod_ref[:, 2 * D_MODEL:3 * D_MODEL]
    x1 = x_ref[...] + g1 * _dot(mixed.astype(BF16), w_o_ref[...])
    mod_rows = tuple(mod_ref[:, j * D_MODEL:(j + 1) * D_MODEL] for j in (3, 4, 5))
    xmid, h2b, lgt = _ffn_pre(x1, mod_rows, gffn_ref[...], w_sgu_ref[...], w_sd_ref[...],
                              wr_hl_ref[...])
    xmid_ref[...] = xmid
    _pack_rows(h2b, h2_ref)
    lgt_ref[...] = lgt


def _smp3(x, mod, o, gate, yb, sga, sgb, hg, gffn, w_oh, w_oc, w_o, wr_hl, w_sgu, w_sd,
          h2_all, lgt_all, n_prompt):
    n = x.shape[0]
    vmem_args = [x, mod, o, gate, yb, sga, sgb, hg, gffn, w_oh, w_oc, w_o, wr_hl, w_sgu, w_sd]
    blk = n_prompt // n

    def full(a):
        nd = a.ndim
        return pl.BlockSpec(a.shape, lambda i, _nd=nd: (0,) * _nd)

    return pl.pallas_call(
        _smp3_kernel,
        grid=(1,),
        in_specs=[full(a) for a in vmem_args]
                 + [pl.BlockSpec(memory_space=pl.ANY), pl.BlockSpec(memory_space=pl.ANY)],
        out_specs=[pl.BlockSpec((n, D_MODEL), lambda i: (0, 0)),
                   pl.BlockSpec((2, n, PLANE_W), lambda i: (0, blk, 0)),
                   pl.BlockSpec((N_EXP, n), lambda i: (0, blk))],
        out_shape=[jax.ShapeDtypeStruct((n, D_MODEL), F32),
                   jax.ShapeDtypeStruct(h2_all.shape, h2_all.dtype),
                   jax.ShapeDtypeStruct(lgt_all.shape, lgt_all.dtype)],
        input_output_aliases={len(vmem_args): 1, len(vmem_args) + 1: 2},
        compiler_params=pltpu.CompilerParams(
            dimension_semantics=("arbitrary",), vmem_limit_bytes=VMEM_LIMIT),
        name="smp3",
    )(*vmem_args, h2_all, lgt_all)


def _route_kernel(lgt_ref, bias_ref, idx_ref, w_ref, rank_ref, cnt_ref):
    tr = ROUTE_TILE
    n_tiles = lgt_ref.shape[1] // tr

    def tile(i, carry):
        cols = pl.ds(pl.multiple_of(i * tr, tr), tr)
        picks, weights, ranks, carry = _route_tile(lgt_ref[:, cols], bias_ref[...], carry)
        for k in range(TOP_K):
            idx_ref[k:k + 1, cols] = picks[k]
            w_ref[k:k + 1, cols] = weights[k]
            rank_ref[k:k + 1, cols] = ranks[k]
        return carry

    total = lax.fori_loop(0, n_tiles, tile, jnp.zeros((N_EXP, 1), F32))
    cnt_ref[...] = jnp.broadcast_to(total, cnt_ref.shape).astype(I32)


def _route_tile(lgt, bias, carry):
    tr = lgt.shape[1]
    neg = -jnp.inf
    scores = _sigmoid(lgt)
    sel = scores + bias
    j8 = lax.broadcasted_iota(I32, (GRP_SZ, tr), 0)
    groups = [sel[g * GRP_SZ:(g + 1) * GRP_SZ] for g in range(N_GRP)]
    gscore = []
    for grp in groups:
        m1 = jnp.max(grp, axis=0, keepdims=True)
        i1 = jnp.min(jnp.where(grp == m1, j8, GRP_SZ), axis=0, keepdims=True)
        m2 = jnp.max(jnp.where(j8 == i1, neg, grp), axis=0, keepdims=True)
        gscore.append(m1 + m2)
    kept = []
    for g in range(N_GRP):
        beaten = jnp.zeros((1, tr), I32)
        for o in range(N_GRP):
            if o < g:
                beaten = beaten + (gscore[o] >= gscore[g]).astype(I32)
            elif o > g:
                beaten = beaten + (gscore[o] > gscore[g]).astype(I32)
        kept.append(jnp.where(beaten < TOPK_GRP, groups[g], neg))
    masked = jnp.concatenate(kept, axis=0)
    ei = lax.broadcasted_iota(I32, masked.shape, 0)
    chosen = jnp.zeros(masked.shape, jnp.bool_)
    picks, weights = [], []
    for _ in range(TOP_K):
        m = jnp.max(masked, axis=0, keepdims=True)
        pick = jnp.min(jnp.where(masked == m, ei, N_EXP), axis=0, keepdims=True)
        hit = ei == pick
        weights.append(jnp.sum(jnp.where(hit, scores, 0.0), axis=0, keepdims=True))
        picks.append(pick)
        chosen = chosen | hit
        masked = jnp.where(hit, neg, masked)
    wsum = weights[0]
    for w in weights[1:]:
        wsum = wsum + w
    sel01 = chosen.astype(F32)
    r = lax.broadcasted_iota(I32, (tr, tr), 0)
    c = lax.broadcasted_iota(I32, (tr, tr), 1)
    before = (r < c).astype(BF16)
    cnt = _dot(sel01.astype(BF16), before) + carry
    weights = [w / wsum * ROUTED_SCALE for w in weights]
    ranks = [jnp.sum(jnp.where(ei == p, cnt, 0.0), axis=0, keepdims=True).astype(I32) for p in picks]
    return picks, weights, ranks, carry + jnp.sum(sel01, axis=1, keepdims=True)


def _route(lgt, bias):
    n = lgt.shape[1]
    assert n % ROUTE_TILE == 0
    slot = lambda dt: jax.ShapeDtypeStruct((TOP_K, n), dt)
    return pl.pallas_call(
        _route_kernel,
        out_shape=[slot(I32), slot(F32), slot(I32),
                   jax.ShapeDtypeStruct((N_EXP, 128), I32)],
        name="route",
    )(lgt, bias)


def _dest_kernel(start_ref, idx_ref, rank_ref, dest_ref, *, n_rows):
    n_tok = idx_ref.shape[1]
    idx = idx_ref[...]
    acc = rank_ref[...]
    for e in range(N_EXP):
        acc = acc + jnp.where(idx == e, start_ref[e], 0)
    dest_ref[:, 0:n_tok] = acc
    dest_ref[:, n_tok:2 * n_tok] = acc + n_rows


def _dest(pad_start, idx, rank, n_rows):
    k, n_tok = idx.shape
    return pl.pallas_call(
        functools.partial(_dest_kernel, n_rows=n_rows),
        in_specs=[pl.BlockSpec(memory_space=pltpu.SMEM),
                  pl.BlockSpec(memory_space=pltpu.VMEM),
                  pl.BlockSpec(memory_space=pltpu.VMEM)],
        out_specs=pl.BlockSpec(memory_space=pltpu.VMEM),
        out_shape=jax.ShapeDtypeStruct((k, 2 * n_tok), I32),
        name="dest",
    )(pad_start, idx, rank)


def _sc_mesh():
    return plsc.VectorSubcoreMesh(core_axis_name="core", subcore_axis_name="subcore")


def _dispatch(rows, dest, n_out):
    n, width = rows.shape
    win = SC_WINDOW
    steps = n // win

    @pl.kernel(out_type=jax.ShapeDtypeStruct((n_out, width), rows.dtype), mesh=_sc_mesh(),
               scratch_types=[], name="dispatch")
    def run(x_hbm, *refs):
        i_hbms, o_hbm = refs[:TOP_K], refs[TOP_K]

        def body(x_vmem, *i_vmems):
            for i_vmem in i_vmems:
                pltpu.sync_copy(x_vmem, o_hbm.at[i_vmem.at[0]])

        pltpu.emit_pipeline(
            body,
            grid=(steps,),
            in_specs=[pl.BlockSpec((win, width), lambda i: (i, 0))]
                     + [pl.BlockSpec((1, win), lambda i, k=k: (0, k * steps + i)) for k in range(TOP_K)],
            out_specs=[],
            core_axis_name=("core", "subcore"),
            dimension_semantics=(pltpu.PARALLEL,),
        )(x_hbm, *i_hbms)

    dest_flat = dest.reshape(1, TOP_K * n)
    return run(rows, *([dest_flat] * TOP_K))


def _combine(rows, dest_flat):
    width = rows.shape[1]
    n = dest_flat.shape[0]
    win = SC_WINDOW

    @pl.kernel(out_type=jax.ShapeDtypeStruct((n, width), rows.dtype), mesh=_sc_mesh(),
               scratch_types=[], name="combine")
    def run(y_hbm, i_hbm, o_hbm):
        def body(i_vmem, o_vmem):
            pltpu.sync_copy(y_hbm.at[i_vmem.at[0]], o_vmem)

        pltpu.emit_pipeline(
            body,
            grid=(n // win,),
            in_specs=[pl.BlockSpec((1, win), lambda i: (0, i))],
            out_specs=[pl.BlockSpec((win, width), lambda i: (i, 0))],
            core_axis_name=("core", "subcore"),
            dimension_semantics=(pltpu.PARALLEL,),
        )(i_hbm, o_hbm)

    return run(rows, dest_flat.reshape(1, n))


def _gmm_kernel(blk_exp_ref, n_used_ref, xs_hbm, wg_hbm, wu_hbm, wd_hbm, ys_hbm,
                xbuf, ybuf, wg32, wu32, wd32, wgu_b, wd_b, xb0, xb1, xsem, ysem, wsem, run_ref):
    bm = xbuf.shape[2]
    n_used = n_used_ref[0]

    def x_copy(b, slot):
        return pltpu.make_async_copy(xs_hbm.at[:, pl.ds(b * bm, bm), :], xbuf.at[slot], xsem.at[slot])

    def y_copy(b, slot):
        return pltpu.make_async_copy(ybuf.at[slot], ys_hbm.at[:, pl.ds(b * bm, bm), :], ysem.at[slot])

    def w_copies(e, slot):
        return (pltpu.make_async_copy(wg_hbm.at[e], wg32.at[slot], wsem.at[slot, 0]),
                pltpu.make_async_copy(wu_hbm.at[e], wu32.at[slot], wsem.at[slot, 1]),
                pltpu.make_async_copy(wd_hbm.at[e], wd32.at[slot], wsem.at[slot, 2]))

    def run_end(b):
        return lax.while_loop(lambda j: (j < n_used) & (blk_exp_ref[jnp.minimum(j, n_used - 1)] == blk_exp_ref[b]),
                              lambda j: j + 1, b + 1)

    def unpack(xw, xb):
        for i, c in enumerate(_unpack_rows(xw[0], xw[1])):
            xb[i] = c

    def stages(xb_cur, xb_nxt, xw_nxt, yw):
        gu = sum(_dot(xb_cur[i], wgu_b[i * PLANE_W:(i + 1) * PLANE_W, :]) for i in range(4))
        unpack(xw_nxt, xb_nxt)
        act = (_silu(gu[:, :EXP_FF]) * gu[:, EXP_FF:]).astype(BF16)
        _pack_rows(_dot(act, wd_b[...]).astype(BF16), yw)

    run_ref[0] = 0
    x_copy(0, 0).start()
    for c in w_copies(blk_exp_ref[0], 0):
        c.start()
    xbuf[1] = jnp.zeros(xbuf.shape[1:], I32)
    x_copy(0, 0).wait()
    unpack(xbuf.at[0], xb0)

    @pl.when(n_used > 1)
    def _():
        x_copy(1, 1).start()

    def block(b, carry):
        slot = b % 2

        @pl.when(b + 1 < n_used)
        def _():
            x_copy(b + 1, 1 - slot).wait()

        @pl.when(b + 2 < n_used)
        def _():
            x_copy(b + 2, slot).start()

        @pl.when((b == 0) | (blk_exp_ref[b] != blk_exp_ref[jnp.maximum(b - 1, 0)]))
        def _():
            wslot = run_ref[0] % 2
            run_ref[0] = run_ref[0] + 1
            for c in w_copies(blk_exp_ref[b], wslot):
                c.wait()
            wgu_b[:, 0:EXP_FF] = wg32[wslot].astype(BF16)
            wgu_b[:, EXP_FF:2 * EXP_FF] = wu32[wslot].astype(BF16)
            wd_b[...] = wd32[wslot].astype(BF16)
            nxt = run_end(b)

            @pl.when(nxt < n_used)
            def _():
                for c in w_copies(blk_exp_ref[jnp.minimum(nxt, n_used - 1)], 1 - wslot):
                    c.start()

        @pl.when(b >= 2)
        def _():
            y_copy(b - 2, slot).wait()

        @pl.when(slot == 0)
        def _():
            stages(xb0, xb1, xbuf.at[1], ybuf.at[0])

        @pl.when(slot == 1)
        def _():
            stages(xb1, xb0, xbuf.at[0], ybuf.at[1])

        y_copy(b, slot).start()
        return carry

    lax.fori_loop(0, n_used, block, 0)

    @pl.when(n_used >= 2)
    def _():
        y_copy(n_used - 2, n_used % 2).wait()

    y_copy(n_used - 1, (n_used - 1) % 2).wait()


def _gmm(blk_exp, n_used, xs, w_gate, w_up, w_down):
    n_rows = xs.shape[1]
    bm = GMM_BM
    nb = n_rows // bm

    assert blk_exp.shape == (nb,)
    any_spec = pl.BlockSpec(memory_space=pl.ANY)
    grid_spec = pltpu.PrefetchScalarGridSpec(
        num_scalar_prefetch=2,
        grid=(1,),
        in_specs=[any_spec, any_spec, any_spec, any_spec],
        out_specs=any_spec,
        scratch_shapes=[pltpu.VMEM((2, 2, bm, PLANE_W), I32), pltpu.VMEM((2, 2, bm, PLANE_W), I32),
                        pltpu.VMEM((2, D_MODEL, EXP_FF), F32), pltpu.VMEM((2, D_MODEL, EXP_FF), F32),
                        pltpu.VMEM((2, EXP_FF, D_MODEL), F32),
                        pltpu.VMEM((D_MODEL, 2 * EXP_FF), BF16), pltpu.VMEM((EXP_FF, D_MODEL), BF16),
                        pltpu.VMEM((4, bm, PLANE_W), BF16), pltpu.VMEM((4, bm, PLANE_W), BF16),
                        pltpu.SemaphoreType.DMA((2,)), pltpu.SemaphoreType.DMA((2,)),
                        pltpu.SemaphoreType.DMA((2, 3)), pltpu.SMEM((1,), I32)],
    )
    return pl.pallas_call(
        _gmm_kernel,
        grid_spec=grid_spec,
        out_shape=jax.ShapeDtypeStruct((2, n_rows, PLANE_W), I32),
        compiler_params=pltpu.CompilerParams(dimension_semantics=("arbitrary",)),
        name="gmm",
    )(blk_exp, n_used, xs, w_gate, w_up, w_down)


def _final_kernel(xmid_ref, g2_ref, z_ref, w_ref, gfin_ref, *rest):
    y_ref = rest[-1]
    accs = [jnp.zeros((xmid_ref.shape[0], PLANE_W), F32) for _ in range(4)]
    w_cols = w_ref[...].T
    for k in range(TOP_K):
        wk = w_cols[:, k:k + 1]
        cols = _unpack_rows(z_ref[k, 0], z_ref[k, 1])
        accs = [a + wk * c.astype(F32) for a, c in zip(accs, cols)]
    acc = jnp.concatenate(accs, axis=1)
    y_ref[...] = _rms(xmid_ref[...] + g2_ref[0] * acc) * gfin_ref[...]


def _final(xmid, g2, z, w_t, gfin, tile, *, n_tiles, x_tile0, z_tile0, w_tile0, tiles_per_g2, y_prev=None):
    args = [xmid, g2, z, w_t, gfin]
    in_specs = [pl.BlockSpec((tile, D_MODEL), lambda i: (x_tile0 + i, 0)),
                pl.BlockSpec((1, g2.shape[1], D_MODEL), lambda i: ((x_tile0 + i) // tiles_per_g2, 0, 0)),
                pl.BlockSpec((TOP_K, 2, tile, PLANE_W), lambda i: (0, 0, z_tile0 + i, 0)),
                pl.BlockSpec((TOP_K, tile), lambda i: (0, w_tile0 + i)),
                pl.BlockSpec((1, D_MODEL), lambda i: (0, 0))]
    aliases = {}
    if y_prev is not None:
        args.append(y_prev)
        in_specs.append(pl.BlockSpec(memory_space=pl.ANY))
        aliases = {len(args) - 1: 0}
    return pl.pallas_call(
        _final_kernel,
        grid=(n_tiles,),
        in_specs=in_specs,
        out_specs=pl.BlockSpec((tile, D_MODEL), lambda i: (x_tile0 + i, 0)),
        out_shape=jax.ShapeDtypeStruct(xmid.shape, F32),
        input_output_aliases=aliases,
        compiler_params=pltpu.CompilerParams(dimension_semantics=("arbitrary",)),
        name="final",
    )(*args)


def kernel(x_prompt, x_sample, state_hgrn, state_conv, c_prompt, c_sample, w_ada, b_ada, norm_mix_g, norm_ffn_g, w_in, lb_logits, hgrn_norm_g, conv_w, conv_b, w_out_hgrn, w_out_conv, w_o, w_router, router_bias, w_exp_gate, w_exp_up, w_exp_down, w_sh_gate, w_sh_up, w_sh_down, final_norm_g):
    assert w_ada.shape[0] == 1 and lb_logits.shape[0] == 2
    bsz, seq, _ = x_prompt.shape
    n_smp = x_sample.shape[0]
    n_prompt = bsz * seq
    n_tok = n_prompt + n_smp

    w_in_b = w_in[0].astype(BF16)
    w_oh_b = w_out_hgrn[0].astype(BF16)
    w_oc_b = w_out_conv[0].astype(BF16)
    w_o_b = w_o[0].astype(BF16)
    wr_t = w_router[0].T
    wr_hi = wr_t.astype(BF16)
    wr_hl = jnp.concatenate([wr_hi, (wr_t - wr_hi.astype(F32)).astype(BF16)], axis=0)
    w_sgu = jnp.concatenate([w_sh_gate[0], w_sh_up[0]], axis=1).astype(BF16)
    w_sd = w_sh_down[0].astype(BF16)
    gmix = norm_mix_g[0].reshape(1, D_MODEL)
    gffn = norm_ffn_g[0].reshape(1, D_MODEL)
    hg = hgrn_norm_g[0].reshape(1, KEY_W)
    cw = conv_w[0]
    cb = conv_b[0].reshape(1, CONV_W)
    gfin = final_norm_g.reshape(1, D_MODEL)

    mod = _ada(jnp.concatenate([c_prompt, c_sample], axis=0), w_ada[0], b_ada[0])
    mod_p, mod_s = mod[:bsz], mod[bsz:]

    xmid_p, h2_all, lgt_all, s_p, cv_p = _mix(
        x_prompt, mod_p.reshape(bsz, 6, D_MODEL),
        jnp.zeros((2, n_tok, PLANE_W), I32), jnp.zeros((N_EXP, n_tok), F32),
        gmix, gffn, w_in_b, lb_logits, hg, cw, cb, w_oh_b, w_oc_b, w_o_b, wr_hl, w_sgu, w_sd)

    xs2 = x_sample.reshape(n_smp, D_MODEL)
    f, kk, q, v, gate, yb, sga, sgb, cv_s = _smp1(
        xs2, mod_s, gmix, w_in_b, lb_logits, cw, cb, state_conv[0].reshape(n_smp, (CONV_K - 1) * CONV_W))
    s_s, o_s = _smp2(f, kk, q, v, state_hgrn[0])
    xmid_s, h2_all, lgt_all = _smp3(xs2, mod_s, o_s, gate, yb, sga, sgb, hg, gffn,
                                    w_oh_b, w_oc_b, w_o_b, wr_hl, w_sgu, w_sd,
                                    h2_all, lgt_all, n_prompt)

    idx, w_tok, rank, cnt = _route(lgt_all, router_bias[0].reshape(N_EXP, 1))

    bm = GMM_BM
    n_blocks = (n_tok * TOP_K + N_EXP * (bm - 1)) // bm
    n_rows = n_blocks * bm
    counts = cnt[:, 0]
    padded = (counts + bm - 1) // bm * bm
    pad_end = jnp.cumsum(padded)
    pad_start = pad_end - padded
    blk_row0 = jnp.arange(n_blocks, dtype=I32) * bm
    blk_exp = jnp.minimum(jnp.sum((pad_end[None, :] <= blk_row0[:, None]).astype(I32), axis=1), N_EXP - 1)
    n_used = (pad_end[-1:] // bm).astype(I32)
    dest = _dest(pad_start.astype(I32), idx, rank, n_rows)

    xs = _dispatch(h2_all.reshape(2 * n_tok, PLANE_W), dest, 2 * n_rows).reshape(2, n_rows, PLANE_W)
    ys = _gmm(blk_exp, n_used, xs, w_exp_gate[0], w_exp_up[0], w_exp_down[0])

    ys_flat = ys.reshape(2 * n_rows, PLANE_W)
    dest3 = dest.reshape(TOP_K, 2, n_tok)
    w_t = w_tok
    xmid_p2 = xmid_p.reshape(n_prompt, D_MODEL)
    g2_p = mod_p[:, 5 * D_MODEL:].reshape(bsz, 1, D_MODEL)
    g2_s = mod_s[:, 5 * D_MODEL:].reshape(1, n_smp, D_MODEL)
    chunk = n_prompt // FINAL_CHUNKS
    y_p = None
    for c in range(FINAL_CHUNKS):
        c0 = c * chunk
        c1 = n_tok if c == FINAL_CHUNKS - 1 else c0 + chunk
        z = _combine(ys_flat, dest3[:, :, c0:c1].reshape(-1)).reshape(TOP_K, 2, c1 - c0, PLANE_W)
        y_p = _final(xmid_p2, g2_p, z, w_t, gfin, FINAL_TILE, n_tiles=chunk // FINAL_TILE,
                     x_tile0=c0 // FINAL_TILE, z_tile0=0, w_tile0=c0 // FINAL_TILE,
                     tiles_per_g2=seq // FINAL_TILE, y_prev=y_p)
    y_s = _final(xmid_s, g2_s, z, w_t, gfin, n_smp, n_tiles=1, x_tile0=0, z_tile0=chunk // n_smp,
                 w_tile0=n_prompt // n_smp, tiles_per_g2=1)

    return (y_p.reshape(bsz, seq, D_MODEL), y_s.reshape(n_smp, 1, D_MODEL),
            s_p[None], cv_p[None], s_s[None], cv_s.reshape(1, n_smp, CONV_K - 1, CONV_W))
```

```python
import functools

import jax
import jax.numpy as jnp
from jax import lax
from jax.experimental import pallas as pl
from jax.experimental.pallas import tpu as pltpu
from jax.experimental.pallas import tpu_sc as plsc

F32 = jnp.float32
BF16 = jnp.bfloat16
I32 = jnp.int32

D_MODEL = 1024
HALF_D = D_MODEL // 2
HEADS = 4
DK = 128
KEY_W = HEADS * DK
CONV_W = 512
CONV_K = 3
IN_W = 2 * KEY_W + 2 * KEY_W + 3 * CONV_W + 2 * D_MODEL
N_EXP = 64
TOP_K = 8
N_GRP = 8
GRP_SZ = N_EXP // N_GRP
TOPK_GRP = 4
EXP_FF = 256
SH_FF = 256
ROUTED_SCALE = 2.5
EPS = 1e-6

C_Q, C_F, C_I, C_G = 0, 512, 1024, 1536
C_BB, C_CC, C_VB = 2048, 2560, 3072
C_MGA, C_MGB = 3584, 4608

MIX_TILE = 512
SUB = 256
CHUNK = 64
ROUTE_TILE = 384
GMM_BM = 512
FINAL_TILE = 256
FINAL_CHUNKS = 4
SMP_GROUP = 8
SC_WINDOW = 128
PLANE_W = HALF_D // 2
VMEM_LIMIT = 56 * 1024 * 1024


def _dot(a, b):
    return jnp.dot(a, b, preferred_element_type=F32)


def _dot_nt(a, b):
    return lax.dot_general(a, b, (((1,), (1,)), ((), ())), preferred_element_type=F32)


def _dot_tn(a, b):
    return lax.dot_general(a, b, (((0,), (0,)), ((), ())), preferred_element_type=F32)


def _sigmoid(x):
    return 0.5 * jnp.tanh(0.5 * x) + 0.5


def _silu(x):
    h = 0.5 * x
    return h * jnp.tanh(h) + h


def _rms(x):
    return x * lax.rsqrt(jnp.mean(x * x, axis=-1, keepdims=True) + EPS)


def _lower_bound(lbl):
    a, b = lbl[0:1], lbl[1:2]
    m = jnp.maximum(a, b)
    ea, eb = jnp.exp(a - m), jnp.exp(b - m)
    return ea / (ea + eb)


def _split3(x):
    hi = x.astype(BF16)
    r1 = x - hi.astype(F32)
    mid = r1.astype(BF16)
    lo = (r1 - mid.astype(F32)).astype(BF16)
    return hi, mid, lo


def _pack_rows(xb, out_ref):
    bits = lax.bitcast_convert_type(xb.astype(F32), I32)
    lo = lax.shift_right_logical(bits[:, :HALF_D], 16)
    hi = bits[:, HALF_D:] & jnp.int32(-65536)
    words = lo | hi
    out_ref[0] = words[:, :PLANE_W]
    out_ref[1] = words[:, PLANE_W:]


def _unpack_rows(p0, p1):
    def halves(w):
        lo = lax.bitcast_convert_type(lax.shift_left(w, 16), F32)
        hi = lax.bitcast_convert_type(w & jnp.int32(-65536), F32)
        return lo.astype(BF16), hi.astype(BF16)

    c0, c2 = halves(p0)
    c1, c3 = halves(p1)
    return c0, c1, c2, c3


def _ada_kernel(c_ref, w_ref, b_ref, o_ref):
    a = _silu(c_ref[...]).astype(BF16)
    o_ref[...] = _dot(a, w_ref[...].astype(BF16)) + b_ref[...]


def _ada(c_all, w_ada, b_ada):
    n = c_all.shape[0]
    blk = 1024
    return pl.pallas_call(
        _ada_kernel,
        grid=(6 * D_MODEL // blk,),
        in_specs=[pl.BlockSpec((n, D_MODEL), lambda j: (0, 0)),
                  pl.BlockSpec((D_MODEL, blk), lambda j: (0, j)),
                  pl.BlockSpec((1, blk), lambda j: (0, j))],
        out_specs=pl.BlockSpec((n, blk), lambda j: (0, j)),
        out_shape=jax.ShapeDtypeStruct((n, 6 * D_MODEL), F32),
        name="ada",
    )(c_all, w_ada, b_ada.reshape(1, -1))


def _ffn_pre(x1, mod_rows, gffn, w_sgu, w_sd, wr_hl):
    sh2, sc2, g2 = mod_rows
    h2 = _rms(x1) * gffn * (1.0 + sc2) + sh2
    h2b = h2.astype(BF16)
    gu = _dot(h2b, w_sgu)
    act = _silu(gu[:, :SH_FF]) * gu[:, SH_FF:]
    xmid = x1 + g2 * _dot(act.astype(BF16), w_sd)
    h2lo = (h2 - h2b.astype(F32)).astype(BF16)
    both = _dot_nt(wr_hl, h2b)
    lgt = both[:N_EXP] + both[N_EXP:] + _dot_nt(wr_hl[:N_EXP], h2lo)
    return xmid, h2b, lgt


def _mix_kernel(x_ref, mod_ref, gmix_ref, gffn_ref, w_in_ref, lbl_ref, hg_ref, cw_ref, cb_ref,
                w_oh_ref, w_oc_ref, w_o_ref, wr_hl_ref, w_sgu_ref, w_sd_ref,
                h2_all_ref, lgt_all_ref,
                xmid_ref, h2_ref, lgt_ref, s_out_ref, cv_out_ref,
                proj_ref, st_ref, cbuf_ref, ya_ref):
    del h2_all_ref, lgt_all_ref
    t = pl.program_id(1)
    nt = pl.num_programs(1)
    tt = x_ref.shape[1]

    @pl.when(t == 0)
    def _():
        st_ref[...] = jnp.zeros_like(st_ref)
        cbuf_ref[...] = jnp.zeros_like(cbuf_ref)

    x = x_ref[0]
    mod = mod_ref[0]
    sh1, sc1, g1 = mod[0:1], mod[1:2], mod[2:3]
    h = _rms(x) * gmix_ref[...] * (1.0 + sc1) + sh1
    hb = h.astype(BF16)
    for c in range(0, IN_W, 512):
        proj_ref[:, c:c + 512] = _dot(hb, w_in_ref[:, c:c + 512])

    lb = _lower_bound(lbl_ref[...])
    row = lax.broadcasted_iota(I32, (SUB, SUB), 0)
    col = lax.broadcasted_iota(I32, (SUB, SUB), 1)
    tri = (col <= row).astype(BF16)
    mask_d = (col <= row) & (row // CHUNK == col // CHUNK)
    mask_a = row // (2 * CHUNK) == col // (2 * CHUNK)
    n_ch = SUB // CHUNK

    def by_chunk(vals):
        return jnp.concatenate([jnp.zeros((CHUNK, DK), F32) if v is None
                                else jnp.broadcast_to(v, (CHUNK, DK)) for v in vals], axis=0)

    for s in range(tt // SUB):
        r0 = s * SUB
        f = lb + (1.0 - lb) * _sigmoid(proj_ref[r0:r0 + SUB, C_F:C_F + KEY_W])
        kk = 1.0 - f
        hi, mid, lo = _split3(jnp.log(f))
        bc = _dot(tri, hi) + _dot(tri, mid) + _dot(tri, lo)
        for hd in range(HEADS):
            hs = slice(hd * DK, (hd + 1) * DK)
            bh = bc[:, hs]
            at = lambda r: bh[r:r + 1]
            mids = [at(c * CHUNK + CHUNK // 2 - 1) for c in range(n_ch)]
            pair_mid = [at(CHUNK - 1), at(3 * CHUNK - 1)]
            step_mid, step_end = at(2 * CHUNK - 1), at(SUB - 1)
            arg = bh - by_chunk(mids)
            e_pos, e_neg = jnp.exp(arg), jnp.exp(-arg)
            q = _silu(proj_ref[r0:r0 + SUB, C_Q + hd * DK:C_Q + (hd + 1) * DK])
            v = proj_ref[r0:r0 + SUB, C_I + hd * DK:C_I + (hd + 1) * DK]
            qd = q * e_pos
            kd = kk[:, hs] * e_neg
            q_in = qd * by_chunk([jnp.exp(m) for m in mids])
            k_end = kd * by_chunk([jnp.exp(step_end - m) for m in mids])
            qa = qd * by_chunk([None, jnp.exp(mids[1] - pair_mid[0]), None, jnp.exp(mids[3] - pair_mid[1])])
            ka = kd * by_chunk([jnp.exp(pair_mid[0] - mids[0]), None, jnp.exp(pair_mid[1] - mids[2]), None])
            qb = qd * by_chunk([None, None, jnp.exp(mids[2] - step_mid), jnp.exp(mids[3] - step_mid)])
            kb = kd * by_chunk([jnp.exp(step_mid - mids[0]), jnp.exp(step_mid - mids[1]), None, None])
            att = jnp.where(mask_d, _dot_nt(qd.astype(BF16), kd.astype(BF16)), 0.0)
            att = att + jnp.where(mask_a, _dot_nt(qa.astype(BF16), ka.astype(BF16)), 0.0)
            att = att + _dot_nt(qb.astype(BF16), kb.astype(BF16))
            vb = v.astype(BF16)
            st = st_ref[hd]
            o = _dot(att.astype(BF16), vb) + _dot_nt(q_in.astype(BF16), st.astype(BF16))
            st_ref[hd] = st * jnp.exp(step_end) + _dot_tn(vb, k_end.astype(BF16))
            gate = _silu(proj_ref[r0:r0 + SUB, C_G + hd * DK:C_G + (hd + 1) * DK])
            ya_ref[r0:r0 + SUB, hs] = _rms(o) * hg_ref[:, hs] * gate

    u = proj_ref[:, C_CC:C_CC + CONV_W] * proj_ref[:, C_VB:C_VB + CONV_W]
    rows = lax.broadcasted_iota(I32, (tt, CONV_W), 0)
    c0, c1 = cbuf_ref[0:1], cbuf_ref[1:2]
    u1 = jnp.where(rows == 0, c1, pltpu.roll(u, 1, axis=0))
    u2 = jnp.where(rows == 0, c0, jnp.where(rows == 1, c1, pltpu.roll(u, 2, axis=0)))
    conv = cw_ref[0:1] * u2 + cw_ref[1:2] * u1 + cw_ref[2:3] * u + cb_ref[...]
    yb = proj_ref[:, C_BB:C_BB + CONV_W] * conv
    cbuf_ref[...] = u[tt - 2:tt]

    mixed = (_sigmoid(proj_ref[:, C_MGA:C_MGA + D_MODEL]) * _dot(ya_ref[...].astype(BF16), w_oh_ref[...])
             + _sigmoid(proj_ref[:, C_MGB:C_MGB + D_MODEL]) * _dot(yb.astype(BF16), w_oc_ref[...]))
    x1 = x + g1 * _dot(mixed.astype(BF16), w_o_ref[...])

    xmid, h2b, lgt = _ffn_pre(x1, (mod[3:4], mod[4:5], mod[5:6]), gffn_ref[...],
                              w_sgu_ref[...], w_sd_ref[...], wr_hl_ref[...])
    xmid_ref[0] = xmid
    _pack_rows(h2b, h2_ref)
    lgt_ref[...] = lgt

    @pl.when(t == nt - 1)
    def _():
        for hd in range(HEADS):
            s_out_ref[0, hd] = st_ref[hd].T
        cv_out_ref[0] = cbuf_ref[...]


def _const_spec(shape):
    nd = len(shape)
    return pl.BlockSpec(shape, lambda b, t, _nd=nd: (0,) * _nd, pipeline_mode=pl.Buffered(1))


def _mix(x, mod, h2_all, lgt_all, gmix, gffn, w_in, lbl, hg, cw, cb, w_oh, w_oc, w_o, wr_hl, w_sgu, w_sd):
    bsz, seq, _ = x.shape
    n_tok = h2_all.shape[1]
    tt = MIX_TILE
    nt = seq // tt
    consts = [gmix, gffn, w_in, lbl, hg, cw, cb, w_oh, w_oc, w_o, wr_hl, w_sgu, w_sd]
    return pl.pallas_call(
        _mix_kernel,
        grid=(bsz, nt),
        in_specs=[pl.BlockSpec((1, tt, D_MODEL), lambda b, t: (b, t, 0)),
                  pl.BlockSpec((1, 6, D_MODEL), lambda b, t: (b, 0, 0))]
                 + [_const_spec(a.shape) for a in consts]
                 + [pl.BlockSpec(memory_space=pl.ANY), pl.BlockSpec(memory_space=pl.ANY)],
        input_output_aliases={2 + len(consts): 1, 3 + len(consts): 2},
        out_specs=[pl.BlockSpec((1, tt, D_MODEL), lambda b, t: (b, t, 0)),
                   pl.BlockSpec((2, tt, PLANE_W), lambda b, t: (0, b * nt + t, 0)),
                   pl.BlockSpec((N_EXP, tt), lambda b, t: (0, b * nt + t)),
                   pl.BlockSpec((1, HEADS, DK, DK), lambda b, t: (b, 0, 0, 0)),
                   pl.BlockSpec((1, CONV_K - 1, CONV_W), lambda b, t: (b, 0, 0))],
        out_shape=[jax.ShapeDtypeStruct((bsz, seq, D_MODEL), F32),
                   jax.ShapeDtypeStruct((2, n_tok, PLANE_W), I32),
                   jax.ShapeDtypeStruct((N_EXP, n_tok), F32),
                   jax.ShapeDtypeStruct((bsz, HEADS, DK, DK), F32),
                   jax.ShapeDtypeStruct((bsz, CONV_K - 1, CONV_W), F32)],
        scratch_shapes=[pltpu.VMEM((tt, IN_W), F32),
                        pltpu.VMEM((HEADS, DK, DK), F32),
                        pltpu.VMEM((CONV_K - 1, CONV_W), F32),
                        pltpu.VMEM((tt, KEY_W), F32)],
        compiler_params=pltpu.CompilerParams(
            dimension_semantics=("arbitrary", "arbitrary"), vmem_limit_bytes=VMEM_LIMIT),
        name="mix",
    )(x, mod, *consts, h2_all, lgt_all)


def _smp1_kernel(x_ref, mod_ref, gmix_ref, w_in_ref, lbl_ref, cw_ref, cb_ref, cst_ref,
                 f_ref, k_ref, q_ref, v_ref, gate_ref, yb_ref, sga_ref, sgb_ref, cv_out_ref):
    x = x_ref[...]
    sh1, sc1 = mod_ref[:, 0:D_MODEL], mod_ref[:, D_MODEL:2 * D_MODEL]
    h = _rms(x) * gmix_ref[...] * (1.0 + sc1) + sh1
    hb = h.astype(BF16)

    def proj(c, w):
        return _dot(hb, w_in_ref[:, c:c + w])

    lb = _lower_bound(lbl_ref[...])
    f = lb + (1.0 - lb) * _sigmoid(proj(C_F, KEY_W))
    f_ref[...] = f
    k_ref[...] = 1.0 - f
    q_ref[...] = _silu(proj(C_Q, KEY_W))
    v_ref[...] = proj(C_I, KEY_W)
    gate_ref[...] = _silu(proj(C_G, KEY_W))
    u = proj(C_CC, CONV_W) * proj(C_VB, CONV_W)
    c0, c1 = cst_ref[:, 0:CONV_W], cst_ref[:, CONV_W:2 * CONV_W]
    conv = cw_ref[0:1] * c0 + cw_ref[1:2] * c1 + cw_ref[2:3] * u + cb_ref[...]
    yb_ref[...] = proj(C_BB, CONV_W) * conv
    cv_out_ref[:, 0:CONV_W] = c1
    cv_out_ref[:, CONV_W:2 * CONV_W] = u
    sga_ref[...] = _sigmoid(proj(C_MGA, D_MODEL))
    sgb_ref[...] = _sigmoid(proj(C_MGB, D_MODEL))


def _smp1(x, mod, gmix, w_in, lbl, cw, cb, cst):
    n = x.shape[0]
    kw = jax.ShapeDtypeStruct((n, KEY_W), F32)
    dm = jax.ShapeDtypeStruct((n, D_MODEL), F32)
    return pl.pallas_call(
        _smp1_kernel,
        out_shape=[kw, kw, kw, kw, kw, kw, dm, dm,
                   jax.ShapeDtypeStruct((n, (CONV_K - 1) * CONV_W), F32)],
        compiler_params=pltpu.CompilerParams(vmem_limit_bytes=VMEM_LIMIT),
        name="smp1",
    )(x, mod, gmix, w_in, lbl, cw, cb, cst)


def _smp2_kernel(f_ref, k_ref, q_ref, v_ref, s_ref, s_out_ref, o_ref):
    g = f_ref.shape[0]
    for i in range(g):
        for hd in range(HEADS):
            hs = slice(hd * DK, (hd + 1) * DK)

            def col(ref):
                return jnp.broadcast_to(ref[i:i + 1, hs], (DK, DK)).T

            vrow = v_ref[i:i + 1, hs]
            s_new = col(f_ref) * s_ref[i, hd] + col(k_ref) * vrow
            s_out_ref[i, hd] = s_new
            o_ref[i:i + 1, hs] = jnp.sum(col(q_ref) * s_new, axis=0, keepdims=True)


def _smp2(f, k, q, v, state):
    n = f.shape[0]
    g = SMP_GROUP
    row_spec = pl.BlockSpec((g, KEY_W), lambda i: (i, 0))
    st_spec = pl.BlockSpec((g, HEADS, DK, DK), lambda i: (i, 0, 0, 0))
    return pl.pallas_call(
        _smp2_kernel,
        grid=(n // g,),
        in_specs=[row_spec, row_spec, row_spec, row_spec, st_spec],
        out_specs=[st_spec, row_spec],
        out_shape=[jax.ShapeDtypeStruct(state.shape, F32), jax.ShapeDtypeStruct((n, KEY_W), F32)],
        compiler_params=pltpu.CompilerParams(dimension_semantics=("arbitrary",)),
        name="smp2",
    )(f, k, q, v, state)


def _smp3_kernel(x_ref, mod_ref, o_ref, gate_ref, yb_ref, sga_ref, sgb_ref, hg_ref, gffn_ref,
                 w_oh_ref, w_oc_ref, w_o_ref, wr_hl_ref, w_sgu_ref, w_sd_ref,
                 h2_all_ref, lgt_all_ref, xmid_ref, h2_ref, lgt_ref):
    del h2_all_ref, lgt_all_ref
    parts = []
    for hd in range(HEADS):
        hs = slice(hd * DK, (hd + 1) * DK)
        parts.append(_rms(o_ref[:, hs]) * hg_ref[:, hs] * gate_ref[:, hs])
    ya = jnp.concatenate(parts, axis=1)
    mixed = (sga_ref[...] * _dot(ya.astype(BF16), w_oh_ref[...])
             + sgb_ref[...] * _dot(yb_ref[...].astype(BF16), w_oc_ref[...]))
    g1 = mod_ref[:, 2 * D_MODEL:3 * D_MODEL]
    x1 = x_ref[...] + g1 * _dot(mixed.astype(BF16), w_o_ref[...])
    mod_rows = tuple(mod_ref[:, j * D_MODEL:(j + 1) * D_MODEL] for j in (3, 4, 5))
    xmid, h2b, lgt = _ffn_pre(x1, mod_rows, gffn_ref[...], w_sgu_ref[...], w_sd_ref[...],
                              wr_hl_ref[...])
    xmid_ref[...] = xmid
    _pack_rows(h2b, h2_ref)
    lgt_ref[...] = lgt


def _smp3(x, mod, o, gate, yb, sga, sgb, hg, gffn, w_oh, w_oc, w_o, wr_hl, w_sgu, w_sd,
          h2_all, lgt_all, n_prompt):
    n = x.shape[0]
    vmem_args = [x, mod, o, gate, yb, sga, sgb, hg, gffn, w_oh, w_oc, w_o, wr_hl, w_sgu, w_sd]
    blk = n_prompt // n

    def full(a):
        nd = a.ndim
        return pl.BlockSpec(a.shape, lambda i, _nd=nd: (0,) * _nd)

    return pl.pallas_call(
        _smp3_kernel,
        grid=(1,),
        in_specs=[full(a) for a in vmem_args]
                 + [pl.BlockSpec(memory_space=pl.ANY), pl.BlockSpec(memory_space=pl.ANY)],
        out_specs=[pl.BlockSpec((n, D_MODEL), lambda i: (0, 0)),
                   pl.BlockSpec((2, n, PLANE_W), lambda i: (0, blk, 0)),
                   pl.BlockSpec((N_EXP, n), lambda i: (0, blk))],
        out_shape=[jax.ShapeDtypeStruct((n, D_MODEL), F32),
                   jax.ShapeDtypeStruct(h2_all.shape, h2_all.dtype),
                   jax.ShapeDtypeStruct(lgt_all.shape, lgt_all.dtype)],
        input_output_aliases={len(vmem_args): 1, len(vmem_args) + 1: 2},
        compiler_params=pltpu.CompilerParams(
            dimension_semantics=("arbitrary",), vmem_limit_bytes=VMEM_LIMIT),
        name="smp3",
    )(*vmem_args, h2_all, lgt_all)


def _route_kernel(lgt_ref, bias_ref, idx_ref, w_ref, rank_ref, cnt_ref):
    tr = ROUTE_TILE
    n_tiles = lgt_ref.shape[1] // tr

    def tile(i, carry):
        cols = pl.ds(pl.multiple_of(i * tr, tr), tr)
        picks, weights, ranks, carry = _route_tile(lgt_ref[:, cols], bias_ref[...], carry)
        for k in range(TOP_K):
            idx_ref[k:k + 1, cols] = picks[k]
            w_ref[k:k + 1, cols] = weights[k]
            rank_ref[k:k + 1, cols] = ranks[k]
        return carry

    total = lax.fori_loop(0, n_tiles, tile, jnp.zeros((N_EXP, 1), F32))
    cnt_ref[...] = jnp.broadcast_to(total, cnt_ref.shape).astype(I32)


def _route_tile(lgt, bias, carry):
    tr = lgt.shape[1]
    neg = -jnp.inf
    scores = _sigmoid(lgt)
    sel = scores + bias
    j8 = lax.broadcasted_iota(I32, (GRP_SZ, tr), 0)
    groups = [sel[g * GRP_SZ:(g + 1) * GRP_SZ] for g in range(N_GRP)]
    gscore = []
    for grp in groups:
        m1 = jnp.max(grp, axis=0, keepdims=True)
        i1 = jnp.min(jnp.where(grp == m1, j8, GRP_SZ), axis=0, keepdims=True)
        m2 = jnp.max(jnp.where(j8 == i1, neg, grp), axis=0, keepdims=True)
        gscore.append(m1 + m2)
    kept = []
    for g in range(N_GRP):
        beaten = jnp.zeros((1, tr), I32)
        for o in range(N_GRP):
            if o < g:
                beaten = beaten + (gscore[o] >= gscore[g]).astype(I32)
            elif o > g:
                beaten = beaten + (gscore[o] > gscore[g]).astype(I32)
        kept.append(jnp.where(beaten < TOPK_GRP, groups[g], neg))
    masked = jnp.concatenate(kept, axis=0)
    ei = lax.broadcasted_iota(I32, masked.shape, 0)
    chosen = jnp.zeros(masked.shape, jnp.bool_)
    picks, weights = [], []
    for _ in range(TOP_K):
        m = jnp.max(masked, axis=0, keepdims=True)
        pick = jnp.min(jnp.where(masked == m, ei, N_EXP), axis=0, keepdims=True)
        hit = ei == pick
        weights.append(jnp.sum(jnp.where(hit, scores, 0.0), axis=0, keepdims=True))
        picks.append(pick)
        chosen = chosen | hit
        masked = jnp.where(hit, neg, masked)
    wsum = weights[0]
    for w in weights[1:]:
        wsum = wsum + w
    sel01 = chosen.astype(F32)
    r = lax.broadcasted_iota(I32, (tr, tr), 0)
    c = lax.broadcasted_iota(I32, (tr, tr), 1)
    before = (r < c).astype(BF16)
    cnt = _dot(sel01.astype(BF16), before) + carry
    weights = [w / wsum * ROUTED_SCALE for w in weights]
    ranks = [jnp.sum(jnp.where(ei == p, cnt, 0.0), axis=0, keepdims=True).astype(I32) for p in picks]
    return picks, weights, ranks, carry + jnp.sum(sel01, axis=1, keepdims=True)


def _route(lgt, bias):
    n = lgt.shape[1]
    assert n % ROUTE_TILE == 0
    slot = lambda dt: jax.ShapeDtypeStruct((TOP_K, n), dt)
    return pl.pallas_call(
        _route_kernel,
        out_shape=[slot(I32), slot(F32), slot(I32),
                   jax.ShapeDtypeStruct((N_EXP, 128), I32)],
        name="route",
    )(lgt, bias)


def _dest_kernel(start_ref, idx_ref, rank_ref, dest_ref, *, n_rows):
    n_tok = idx_ref.shape[1]
    idx = idx_ref[...]
    acc = rank_ref[...]
    for e in range(N_EXP):
        acc = acc + jnp.where(idx == e, start_ref[e], 0)
    dest_ref[:, 0:n_tok] = acc
    dest_ref[:, n_tok:2 * n_tok] = acc + n_rows


def _dest(pad_start, idx, rank, n_rows):
    k, n_tok = idx.shape
    return pl.pallas_call(
        functools.partial(_dest_kernel, n_rows=n_rows),
        in_specs=[pl.BlockSpec(memory_space=pltpu.SMEM),
                  pl.BlockSpec(memory_space=pltpu.VMEM),
                  pl.BlockSpec(memory_space=pltpu.VMEM)],
        out_specs=pl.BlockSpec(memory_space=pltpu.VMEM),
        out_shape=jax.ShapeDtypeStruct((k, 2 * n_tok), I32),
        name="dest",
    )(pad_start, idx, rank)


def _sc_mesh():
    return plsc.VectorSubcoreMesh(core_axis_name="core", subcore_axis_name="subcore")


def _dispatch(rows, dest, n_out):
    n, width = rows.shape
    win = SC_WINDOW
    steps = n // win

    @pl.kernel(out_type=jax.ShapeDtypeStruct((n_out, width), rows.dtype), mesh=_sc_mesh(),
               scratch_types=[], name="dispatch")
    def run(x_hbm, *refs):
        i_hbms, o_hbm = refs[:TOP_K], refs[TOP_K]

        def body(x_vmem, *i_vmems):
            for i_vmem in i_vmems:
                pltpu.sync_copy(x_vmem, o_hbm.at[i_vmem.at[0]])

        pltpu.emit_pipeline(
            body,
            grid=(steps,),
            in_specs=[pl.BlockSpec((win, width), lambda i: (i, 0))]
                     + [pl.BlockSpec((1, win), lambda i, k=k: (0, k * steps + i)) for k in range(TOP_K)],
            out_specs=[],
            core_axis_name=("core", "subcore"),
            dimension_semantics=(pltpu.PARALLEL,),
        )(x_hbm, *i_hbms)

    dest_flat = dest.reshape(1, TOP_K * n)
    return run(rows, *([dest_flat] * TOP_K))


def _combine(rows, dest_flat):
    width = rows.shape[1]
    n = dest_flat.shape[0]
    win = SC_WINDOW

    @pl.kernel(out_type=jax.ShapeDtypeStruct((n, width), rows.dtype), mesh=_sc_mesh(),
               scratch_types=[], name="combine")
    def run(y_hbm, i_hbm, o_hbm):
        def body(i_vmem, o_vmem):
            pltpu.sync_copy(y_hbm.at[i_vmem.at[0]], o_vmem)

        pltpu.emit_pipeline(
            body,
            grid=(n // win,),
            in_specs=[pl.BlockSpec((1, win), lambda i: (0, i))],
            out_specs=[pl.BlockSpec((win, width), lambda i: (i, 0))],
            core_axis_name=("core", "subcore"),
            dimension_semantics=(pltpu.PARALLEL,),
        )(i_hbm, o_hbm)

    return run(rows, dest_flat.reshape(1, n))


def _gmm_kernel(blk_exp_ref, n_used_ref, xs_hbm, wg_hbm, wu_hbm, wd_hbm, ys_hbm,
                xbuf, ybuf, wg32, wu32, wd32, wgu_b, wd_b, xsem, ysem, wsem, run_ref):
    bm = xbuf.shape[2]
    n_used = n_used_ref[0]

    def x_copy(b, slot):
        return pltpu.make_async_copy(xs_hbm.at[:, pl.ds(b * bm, bm), :], xbuf.at[slot], xsem.at[slot])

    def y_copy(b, slot):
        return pltpu.make_async_copy(ybuf.at[slot], ys_hbm.at[:, pl.ds(b * bm, bm), :], ysem.at[slot])

    def w_copies(e, slot):
        return (pltpu.make_async_copy(wg_hbm.at[e], wg32.at[slot], wsem.at[slot, 0]),
                pltpu.make_async_copy(wu_hbm.at[e], wu32.at[slot], wsem.at[slot, 1]),
                pltpu.make_async_copy(wd_hbm.at[e], wd32.at[slot], wsem.at[slot, 2]))

    def run_end(b):
        return lax.while_loop(lambda j: (j < n_used) & (blk_exp_ref[jnp.minimum(j, n_used - 1)] == blk_exp_ref[b]),
                              lambda j: j + 1, b + 1)

    run_ref[0] = 0
    x_copy(0, 0).start()
    for c in w_copies(blk_exp_ref[0], 0):
        c.start()

    def block(b, carry):
        slot = b % 2

        @pl.when(b + 1 < n_used)
        def _():
            x_copy(b + 1, 1 - slot).start()

        @pl.when((b == 0) | (blk_exp_ref[b] != blk_exp_ref[jnp.maximum(b - 1, 0)]))
        def _():
            wslot = run_ref[0] % 2
            run_ref[0] = run_ref[0] + 1
            for c in w_copies(blk_exp_ref[b], wslot):
                c.wait()
            wgu_b[:, 0:EXP_FF] = wg32[wslot].astype(BF16)
            wgu_b[:, EXP_FF:2 * EXP_FF] = wu32[wslot].astype(BF16)
            wd_b[...] = wd32[wslot].astype(BF16)
            nxt = run_end(b)

            @pl.when(nxt < n_used)
            def _():
                for c in w_copies(blk_exp_ref[jnp.minimum(nxt, n_used - 1)], 1 - wslot):
                    c.start()

        x_copy(b, slot).wait()

        @pl.when(b >= 2)
        def _():
            y_copy(b - 2, slot).wait()

        xc = _unpack_rows(xbuf[slot, 0], xbuf[slot, 1])
        gu = sum(_dot(c, wgu_b[i * PLANE_W:(i + 1) * PLANE_W, :]) for i, c in enumerate(xc))
        act = (_silu(gu[:, :EXP_FF]) * gu[:, EXP_FF:]).astype(BF16)
        _pack_rows(_dot(act, wd_b[...]).astype(BF16), ybuf.at[slot])
        y_copy(b, slot).start()
        return carry

    lax.fori_loop(0, n_used, block, 0)

    @pl.when(n_used >= 2)
    def _():
        y_copy(n_used - 2, n_used % 2).wait()

    y_copy(n_used - 1, (n_used - 1) % 2).wait()


def _gmm(blk_exp, n_used, xs, w_gate, w_up, w_down):
    n_rows = xs.shape[1]
    bm = GMM_BM
    nb = n_rows // bm

    assert blk_exp.shape == (nb,)
    any_spec = pl.BlockSpec(memory_space=pl.ANY)
    grid_spec = pltpu.PrefetchScalarGridSpec(
        num_scalar_prefetch=2,
        grid=(1,),
        in_specs=[any_spec, any_spec, any_spec, any_spec],
        out_specs=any_spec,
        scratch_shapes=[pltpu.VMEM((2, 2, bm, PLANE_W), I32), pltpu.VMEM((2, 2, bm, PLANE_W), I32),
                        pltpu.VMEM((2, D_MODEL, EXP_FF), F32), pltpu.VMEM((2, D_MODEL, EXP_FF), F32),
                        pltpu.VMEM((2, EXP_FF, D_MODEL), F32),
                        pltpu.VMEM((D_MODEL, 2 * EXP_FF), BF16), pltpu.VMEM((EXP_FF, D_MODEL), BF16),
                        pltpu.SemaphoreType.DMA((2,)), pltpu.SemaphoreType.DMA((2,)),
                        pltpu.SemaphoreType.DMA((2, 3)), pltpu.SMEM((1,), I32)],
    )
    return pl.pallas_call(
        _gmm_kernel,
        grid_spec=grid_spec,
        out_shape=jax.ShapeDtypeStruct((2, n_rows, PLANE_W), I32),
        compiler_params=pltpu.CompilerParams(dimension_semantics=("arbitrary",)),
        name="gmm",
    )(blk_exp, n_used, xs, w_gate, w_up, w_down)


def _final_kernel(xmid_ref, g2_ref, z_ref, w_ref, gfin_ref, *rest):
    y_ref = rest[-1]
    accs = [jnp.zeros((xmid_ref.shape[0], PLANE_W), F32) for _ in range(4)]
    w_cols = w_ref[...].T
    for k in range(TOP_K):
        wk = w_cols[:, k:k + 1]
        cols = _unpack_rows(z_ref[k, 0], z_ref[k, 1])
        accs = [a + wk * c.astype(F32) for a, c in zip(accs, cols)]
    acc = jnp.concatenate(accs, axis=1)
    y_ref[...] = _rms(xmid_ref[...] + g2_ref[0] * acc) * gfin_ref[...]


def _final(xmid, g2, z, w_t, gfin, tile, *, n_tiles, x_tile0, z_tile0, w_tile0, tiles_per_g2, y_prev=None):
    args = [xmid, g2, z, w_t, gfin]
    in_specs = [pl.BlockSpec((tile, D_MODEL), lambda i: (x_tile0 + i, 0)),
                pl.BlockSpec((1, g2.shape[1], D_MODEL), lambda i: ((x_tile0 + i) // tiles_per_g2, 0, 0)),
                pl.BlockSpec((TOP_K, 2, tile, PLANE_W), lambda i: (0, 0, z_tile0 + i, 0)),
                pl.BlockSpec((TOP_K, tile), lambda i: (0, w_tile0 + i)),
                pl.BlockSpec((1, D_MODEL), lambda i: (0, 0))]
    aliases = {}
    if y_prev is not None:
        args.append(y_prev)
        in_specs.append(pl.BlockSpec(memory_space=pl.ANY))
        aliases = {len(args) - 1: 0}
    return pl.pallas_call(
        _final_kernel,
        grid=(n_tiles,),
        in_specs=in_specs,
        out_specs=pl.BlockSpec((tile, D_MODEL), lambda i: (x_tile0 + i, 0)),
        out_shape=jax.ShapeDtypeStruct(xmid.shape, F32),
        input_output_aliases=aliases,
        compiler_params=pltpu.CompilerParams(dimension_semantics=("arbitrary",)),
        name="final",
    )(*args)


def kernel(x_prompt, x_sample, state_hgrn, state_conv, c_prompt, c_sample, w_ada, b_ada, norm_mix_g, norm_ffn_g, w_in, lb_logits, hgrn_norm_g, conv_w, conv_b, w_out_hgrn, w_out_conv, w_o, w_router, router_bias, w_exp_gate, w_exp_up, w_exp_down, w_sh_gate, w_sh_up, w_sh_down, final_norm_g):
    assert w_ada.shape[0] == 1 and lb_logits.shape[0] == 2
    bsz, seq, _ = x_prompt.shape
    n_smp = x_sample.shape[0]
    n_prompt = bsz * seq
    n_tok = n_prompt + n_smp

    w_in_b = w_in[0].astype(BF16)
    w_oh_b = w_out_hgrn[0].astype(BF16)
    w_oc_b = w_out_conv[0].astype(BF16)
    w_o_b = w_o[0].astype(BF16)
    wr_t = w_router[0].T
    wr_hi = wr_t.astype(BF16)
    wr_hl = jnp.concatenate([wr_hi, (wr_t - wr_hi.astype(F32)).astype(BF16)], axis=0)
    w_sgu = jnp.concatenate([w_sh_gate[0], w_sh_up[0]], axis=1).astype(BF16)
    w_sd = w_sh_down[0].astype(BF16)
    gmix = norm_mix_g[0].reshape(1, D_MODEL)
    gffn = norm_ffn_g[0].reshape(1, D_MODEL)
    hg = hgrn_norm_g[0].reshape(1, KEY_W)
    cw = conv_w[0]
    cb = conv_b[0].reshape(1, CONV_W)
    gfin = final_norm_g.reshape(1, D_MODEL)

    mod = _ada(jnp.concatenate([c_prompt, c_sample], axis=0), w_ada[0], b_ada[0])
    mod_p, mod_s = mod[:bsz], mod[bsz:]

    xmid_p, h2_all, lgt_all, s_p, cv_p = _mix(
        x_prompt, mod_p.reshape(bsz, 6, D_MODEL),
        jnp.zeros((2, n_tok, PLANE_W), I32), jnp.zeros((N_EXP, n_tok), F32),
        gmix, gffn, w_in_b, lb_logits, hg, cw, cb, w_oh_b, w_oc_b, w_o_b, wr_hl, w_sgu, w_sd)

    xs2 = x_sample.reshape(n_smp, D_MODEL)
    f, kk, q, v, gate, yb, sga, sgb, cv_s = _smp1(
        xs2, mod_s, gmix, w_in_b, lb_logits, cw, cb, state_conv[0].reshape(n_smp, (CONV_K - 1) * CONV_W))
    s_s, o_s = _smp2(f, kk, q, v, state_hgrn[0])
    xmid_s, h2_all, lgt_all = _smp3(xs2, mod_s, o_s, gate, yb, sga, sgb, hg, gffn,
                                    w_oh_b, w_oc_b, w_o_b, wr_hl, w_sgu, w_sd,
                                    h2_all, lgt_all, n_prompt)

    idx, w_tok, rank, cnt = _route(lgt_all, router_bias[0].reshape(N_EXP, 1))

    bm = GMM_BM
    n_blocks = (n_tok * TOP_K + N_EXP * (bm - 1)) // bm
    n_rows = n_blocks * bm
    counts = cnt[:, 0]
    padded = (counts + bm - 1) // bm * bm
    pad_end = jnp.cumsum(padded)
    pad_start = pad_end - padded
    blk_row0 = jnp.arange(n_blocks, dtype=I32) * bm
    blk_exp = jnp.minimum(jnp.sum((pad_end[None, :] <= blk_row0[:, None]).astype(I32), axis=1), N_EXP - 1)
    n_used = (pad_end[-1:] // bm).astype(I32)
    dest = _dest(pad_start.astype(I32), idx, rank, n_rows)

    xs = _dispatch(h2_all.reshape(2 * n_tok, PLANE_W), dest, 2 * n_rows).reshape(2, n_rows, PLANE_W)
    ys = _gmm(blk_exp, n_used, xs, w_exp_gate[0], w_exp_up[0], w_exp_down[0])

    ys_flat = ys.reshape(2 * n_rows, PLANE_W)
    dest3 = dest.reshape(TOP_K, 2, n_tok)
    w_t = w_tok
    xmid_p2 = xmid_p.reshape(n_prompt, D_MODEL)
    g2_p = mod_p[:, 5 * D_MODEL:].reshape(bsz, 1, D_MODEL)
    g2_s = mod_s[:, 5 * D_MODEL:].reshape(1, n_smp, D_MODEL)
    chunk = n_prompt // FINAL_CHUNKS
    y_p = None
    for c in range(FINAL_CHUNKS):
        c0 = c * chunk
        c1 = n_tok if c == FINAL_CHUNKS - 1 else c0 + chunk
        z = _combine(ys_flat, dest3[:, :, c0:c1].reshape(-1)).reshape(TOP_K, 2, c1 - c0, PLANE_W)
        y_p = _final(xmid_p2, g2_p, z, w_t, gfin, FINAL_TILE, n_tiles=chunk // FINAL_TILE,
                     x_tile0=c0 // FINAL_TILE, z_tile0=0, w_tile0=c0 // FINAL_TILE,
                     tiles_per_g2=seq // FINAL_TILE, y_prev=y_p)
    y_s = _final(xmid_s, g2_s, z, w_t, gfin, n_smp, n_tiles=1, x_tile0=0, z_tile0=chunk // n_smp,
                 w_tile0=n_prompt // n_smp, tiles_per_g2=1)

    return (y_p.reshape(bsz, seq, D_MODEL), y_s.reshape(n_smp, 1, D_MODEL),
            s_p[None], cv_p[None], s_s[None], cv_s.reshape(1, n_smp, CONV_K - 1, CONV_W))
```

```python
import functools

import jax
import jax.numpy as jnp
from jax import lax
from jax.experimental import pallas as pl
from jax.experimental.pallas import tpu as pltpu
from jax.experimental.pallas import tpu_sc as plsc

F32 = jnp.float32
BF16 = jnp.bfloat16
I32 = jnp.int32

D_MODEL = 1024
HALF_D = D_MODEL // 2
HEADS = 4
DK = 128
KEY_W = HEADS * DK
CONV_W = 512
CONV_K = 3
IN_W = 2 * KEY_W + 2 * KEY_W + 3 * CONV_W + 2 * D_MODEL
N_EXP = 64
TOP_K = 8
N_GRP = 8
GRP_SZ = N_EXP // N_GRP
TOPK_GRP = 4
EXP_FF = 256
SH_FF = 256
ROUTED_SCALE = 2.5
EPS = 1e-6

C_Q, C_F, C_I, C_G = 0, 512, 1024, 1536
C_BB, C_CC, C_VB = 2048, 2560, 3072
C_MGA, C_MGB = 3584, 4608

MIX_TILE = 512
SUB = 256
CHUNK = 64
ROUTE_TILE = 384
GMM_BM = 512
FINAL_TILE = 256
FINAL_CHUNKS = 4
SMP_GROUP = 8
SC_WINDOW = 128
PLANE_W = HALF_D // 2
VMEM_LIMIT = 56 * 1024 * 1024


def _dot(a, b):
    return jnp.dot(a, b, preferred_element_type=F32)


def _dot_nt(a, b):
    return lax.dot_general(a, b, (((1,), (1,)), ((), ())), preferred_element_type=F32)


def _dot_tn(a, b):
    return lax.dot_general(a, b, (((0,), (0,)), ((), ())), preferred_element_type=F32)


def _sigmoid(x):
    return 0.5 * jnp.tanh(0.5 * x) + 0.5


def _silu(x):
    h = 0.5 * x
    return h * jnp.tanh(h) + h


def _rms(x):
    return x * lax.rsqrt(jnp.mean(x * x, axis=-1, keepdims=True) + EPS)


def _lower_bound(lbl):
    a, b = lbl[0:1], lbl[1:2]
    m = jnp.maximum(a, b)
    ea, eb = jnp.exp(a - m), jnp.exp(b - m)
    return ea / (ea + eb)


def _split3(x):
    hi = x.astype(BF16)
    r1 = x - hi.astype(F32)
    mid = r1.astype(BF16)
    lo = (r1 - mid.astype(F32)).astype(BF16)
    return hi, mid, lo


def _words(lo_b, hi_b):
    lo = lax.shift_right_logical(lax.bitcast_convert_type(lo_b.astype(F32), I32), 16)
    hi = lax.bitcast_convert_type(hi_b.astype(F32), I32) & jnp.int32(-65536)
    return lo | hi


def _halves(w):
    lo = lax.bitcast_convert_type(lax.shift_left(w, 16), F32)
    hi = lax.bitcast_convert_type(w & jnp.int32(-65536), F32)
    return lo.astype(BF16), hi.astype(BF16)


def _pack_rows(xb, out_ref):
    words = _words(xb[:, :HALF_D], xb[:, HALF_D:])
    out_ref[0] = words[:, :PLANE_W]
    out_ref[1] = words[:, PLANE_W:]


def _unpack_rows(p0, p1):
    c0, c2 = _halves(p0)
    c1, c3 = _halves(p1)
    return c0, c1, c2, c3


def _ada_kernel(cp_ref, cs_ref, w_ref, b_ref, op_ref, os_ref):
    w = w_ref[...].astype(BF16)
    for c_ref, o_ref in ((cp_ref, op_ref), (cs_ref, os_ref)):
        o_ref[...] = _dot(_silu(c_ref[...]).astype(BF16), w) + b_ref[...]


def _ada(c_prompt, c_sample, w_ada, b_ada):
    blk = 1024
    rows = lambda c: pl.BlockSpec((c.shape[0], D_MODEL), lambda j: (0, 0))
    cols = lambda c: pl.BlockSpec((c.shape[0], blk), lambda j: (0, j))
    return pl.pallas_call(
        _ada_kernel,
        grid=(6 * D_MODEL // blk,),
        in_specs=[rows(c_prompt), rows(c_sample),
                  pl.BlockSpec((D_MODEL, blk), lambda j: (0, j)),
                  pl.BlockSpec((1, blk), lambda j: (0, j))],
        out_specs=[cols(c_prompt), cols(c_sample)],
        out_shape=[jax.ShapeDtypeStruct((c.shape[0], 6 * D_MODEL), F32) for c in (c_prompt, c_sample)],
        name="ada",
    )(c_prompt, c_sample, w_ada, b_ada.reshape(1, -1))


def _ffn_pre(x1, mod_rows, gffn, w_sgu, w_sd, wr_hl):
    sh2, sc2, g2 = mod_rows
    h2 = _rms(x1) * gffn * (1.0 + sc2) + sh2
    h2b = h2.astype(BF16)
    gu = _dot(h2b, w_sgu)
    act = _silu(gu[:, :SH_FF]) * gu[:, SH_FF:]
    xmid = x1 + g2 * _dot(act.astype(BF16), w_sd)
    h2lo = (h2 - h2b.astype(F32)).astype(BF16)
    both = _dot_nt(wr_hl, h2b)
    lgt = both[:N_EXP] + both[N_EXP:] + _dot_nt(wr_hl[:N_EXP], h2lo)
    return xmid, h2b, lgt


def _mix_kernel(*refs, nt, n_tiles):
    i = pl.program_id(0)
    h2_ref, lgt_ref = refs[16], refs[17]

    @pl.when(i == n_tiles)
    def _():
        h2_ref[...] = jnp.zeros_like(h2_ref)
        lgt_ref[...] = jnp.zeros_like(lgt_ref)

    @pl.when(i < n_tiles)
    def _():
        _mix_tile(i % nt, nt, *refs)


def _mix_tile(t, nt, x_ref, mod_ref, gmix_ref, gffn_ref, w_in_ref, lbl_ref, hg_ref, cw_ref, cb_ref,
              w_oh_ref, w_oc_ref, w_o_ref, wr_hl_ref, w_sgu_ref, w_sd_ref,
              xmid_ref, h2_ref, lgt_ref, s_out_ref, cv_out_ref,
              proj_ref, st_ref, cbuf_ref, ya_ref):
    tt = x_ref.shape[1]

    @pl.when(t == 0)
    def _():
        st_ref[...] = jnp.zeros_like(st_ref)
        cbuf_ref[...] = jnp.zeros_like(cbuf_ref)

    x = x_ref[0]
    mod = mod_ref[0]
    sh1, sc1, g1 = mod[0:1], mod[1:2], mod[2:3]
    h = _rms(x) * gmix_ref[...] * (1.0 + sc1) + sh1
    hb = h.astype(BF16)
    for c in range(0, IN_W, 512):
        proj_ref[:, c:c + 512] = _dot(hb, w_in_ref[:, c:c + 512])

    lb = _lower_bound(lbl_ref[...])
    row = lax.broadcasted_iota(I32, (SUB, SUB), 0)
    col = lax.broadcasted_iota(I32, (SUB, SUB), 1)
    tri = (col <= row).astype(BF16)
    mask_d = (col <= row) & (row // CHUNK == col // CHUNK)
    mask_a = row // (2 * CHUNK) == col // (2 * CHUNK)
    n_ch = SUB // CHUNK

    def by_chunk(vals):
        return jnp.concatenate([jnp.zeros((CHUNK, DK), F32) if v is None
                                else jnp.broadcast_to(v, (CHUNK, DK)) for v in vals], axis=0)

    for s in range(tt // SUB):
        r0 = s * SUB
        f = lb + (1.0 - lb) * _sigmoid(proj_ref[r0:r0 + SUB, C_F:C_F + KEY_W])
        kk = 1.0 - f
        hi, mid, lo = _split3(jnp.log(f))
        bc = _dot(tri, hi) + _dot(tri, mid) + _dot(tri, lo)
        for hd in range(HEADS):
            hs = slice(hd * DK, (hd + 1) * DK)
            bh = bc[:, hs]
            at = lambda r: bh[r:r + 1]
            mids = [at(c * CHUNK + CHUNK // 2 - 1) for c in range(n_ch)]
            pair_mid = [at(CHUNK - 1), at(3 * CHUNK - 1)]
            step_mid, step_end = at(2 * CHUNK - 1), at(SUB - 1)
            arg = bh - by_chunk(mids)
            e_pos, e_neg = jnp.exp(arg), jnp.exp(-arg)
            q = _silu(proj_ref[r0:r0 + SUB, C_Q + hd * DK:C_Q + (hd + 1) * DK])
            v = proj_ref[r0:r0 + SUB, C_I + hd * DK:C_I + (hd + 1) * DK]
            qd = q * e_pos
            kd = kk[:, hs] * e_neg
            q_in = qd * by_chunk([jnp.exp(m) for m in mids])
            k_end = kd * by_chunk([jnp.exp(step_end - m) for m in mids])
            qa = qd * by_chunk([None, jnp.exp(mids[1] - pair_mid[0]), None, jnp.exp(mids[3] - pair_mid[1])])
            ka = kd * by_chunk([jnp.exp(pair_mid[0] - mids[0]), None, jnp.exp(pair_mid[1] - mids[2]), None])
            qb = qd * by_chunk([None, None, jnp.exp(mids[2] - step_mid), jnp.exp(mids[3] - step_mid)])
            kb = kd * by_chunk([jnp.exp(step_mid - mids[0]), jnp.exp(step_mid - mids[1]), None, None])
            att = jnp.where(mask_d, _dot_nt(qd.astype(BF16), kd.astype(BF16)), 0.0)
            att = att + jnp.where(mask_a, _dot_nt(qa.astype(BF16), ka.astype(BF16)), 0.0)
            att = att + _dot_nt(qb.astype(BF16), kb.astype(BF16))
            vb = v.astype(BF16)
            st = st_ref[hd]
            o = _dot(att.astype(BF16), vb) + _dot_nt(q_in.astype(BF16), st.astype(BF16))
            st_ref[hd] = st * jnp.exp(step_end) + _dot_tn(vb, k_end.astype(BF16))
            gate = _silu(proj_ref[r0:r0 + SUB, C_G + hd * DK:C_G + (hd + 1) * DK])
            ya_ref[r0:r0 + SUB, hs] = _rms(o) * hg_ref[:, hs] * gate

    u = proj_ref[:, C_CC:C_CC + CONV_W] * proj_ref[:, C_VB:C_VB + CONV_W]
    rows = lax.broadcasted_iota(I32, (tt, CONV_W), 0)
    c0, c1 = cbuf_ref[0:1], cbuf_ref[1:2]
    u1 = jnp.where(rows == 0, c1, pltpu.roll(u, 1, axis=0))
    u2 = jnp.where(rows == 0, c0, jnp.where(rows == 1, c1, pltpu.roll(u, 2, axis=0)))
    conv = cw_ref[0:1] * u2 + cw_ref[1:2] * u1 + cw_ref[2:3] * u + cb_ref[...]
    yb = proj_ref[:, C_BB:C_BB + CONV_W] * conv
    cbuf_ref[...] = u[tt - 2:tt]

    mixed = (_sigmoid(proj_ref[:, C_MGA:C_MGA + D_MODEL]) * _dot(ya_ref[...].astype(BF16), w_oh_ref[...])
             + _sigmoid(proj_ref[:, C_MGB:C_MGB + D_MODEL]) * _dot(yb.astype(BF16), w_oc_ref[...]))
    x1 = x + g1 * _dot(mixed.astype(BF16), w_o_ref[...])

    xmid, h2b, lgt = _ffn_pre(x1, (mod[3:4], mod[4:5], mod[5:6]), gffn_ref[...],
                              w_sgu_ref[...], w_sd_ref[...], wr_hl_ref[...])
    xmid_ref[0] = xmid
    _pack_rows(h2b, h2_ref)
    lgt_ref[...] = lgt

    @pl.when(t == nt - 1)
    def _():
        for hd in range(HEADS):
            s_out_ref[0, hd] = st_ref[hd].T
        cv_out_ref[0] = cbuf_ref[...]


def _const_spec(shape):
    nd = len(shape)
    return pl.BlockSpec(shape, lambda i, _nd=nd: (0,) * _nd, pipeline_mode=pl.Buffered(1))


def _mix(x, mod, n_tok, gmix, gffn, w_in, lbl, hg, cw, cb, w_oh, w_oc, w_o, wr_hl, w_sgu, w_sd):
    bsz, seq, _ = x.shape
    tt = MIX_TILE
    nt = seq // tt
    n_tiles = bsz * nt
    assert n_tiles * tt < n_tok <= (n_tiles + 1) * tt
    consts = [gmix, gffn, w_in, lbl, hg, cw, cb, w_oh, w_oc, w_o, wr_hl, w_sgu, w_sd]
    tile = lambda i: jnp.minimum(i, n_tiles - 1)
    return pl.pallas_call(
        functools.partial(_mix_kernel, nt=nt, n_tiles=n_tiles),
        grid=(n_tiles + 1,),
        in_specs=[pl.BlockSpec((1, tt, D_MODEL), lambda i: (tile(i) // nt, tile(i) % nt, 0)),
                  pl.BlockSpec((1, 6, D_MODEL), lambda i: (tile(i) // nt, 0, 0))]
                 + [_const_spec(a.shape) for a in consts],
        out_specs=[pl.BlockSpec((1, tt, D_MODEL), lambda i: (tile(i) // nt, tile(i) % nt, 0)),
                   pl.BlockSpec((2, tt, PLANE_W), lambda i: (0, i, 0)),
                   pl.BlockSpec((N_EXP, tt), lambda i: (0, i)),
                   pl.BlockSpec((1, HEADS, DK, DK), lambda i: (tile(i) // nt, 0, 0, 0)),
                   pl.BlockSpec((1, CONV_K - 1, CONV_W), lambda i: (tile(i) // nt, 0, 0))],
        out_shape=[jax.ShapeDtypeStruct((bsz, seq, D_MODEL), F32),
                   jax.ShapeDtypeStruct((2, n_tok, PLANE_W), I32),
                   jax.ShapeDtypeStruct((N_EXP, n_tok), F32),
                   jax.ShapeDtypeStruct((bsz, HEADS, DK, DK), F32),
                   jax.ShapeDtypeStruct((bsz, CONV_K - 1, CONV_W), F32)],
        scratch_shapes=[pltpu.VMEM((tt, IN_W), F32),
                        pltpu.VMEM((HEADS, DK, DK), F32),
                        pltpu.VMEM((CONV_K - 1, CONV_W), F32),
                        pltpu.VMEM((tt, KEY_W), F32)],
        compiler_params=pltpu.CompilerParams(
            dimension_semantics=("arbitrary",), vmem_limit_bytes=VMEM_LIMIT),
        name="mix",
    )(x, mod, *consts)


def _smp1_kernel(x_ref, mod_ref, gmix_ref, w_in_ref, lbl_ref, cw_ref, cb_ref, cst_ref,
                 f_ref, k_ref, q_ref, v_ref, gate_ref, yb_ref, sga_ref, sgb_ref, cv_out_ref):
    x = x_ref[...]
    sh1, sc1 = mod_ref[:, 0:D_MODEL], mod_ref[:, D_MODEL:2 * D_MODEL]
    h = _rms(x) * gmix_ref[...] * (1.0 + sc1) + sh1
    hb = h.astype(BF16)

    def proj(c, w):
        return _dot(hb, w_in_ref[:, c:c + w])

    lb = _lower_bound(lbl_ref[...])
    f = lb + (1.0 - lb) * _sigmoid(proj(C_F, KEY_W))
    f_ref[...] = f
    k_ref[...] = 1.0 - f
    q_ref[...] = _silu(proj(C_Q, KEY_W))
    v_ref[...] = proj(C_I, KEY_W)
    gate_ref[...] = _silu(proj(C_G, KEY_W))
    u = proj(C_CC, CONV_W) * proj(C_VB, CONV_W)
    c0, c1 = cst_ref[:, 0:CONV_W], cst_ref[:, CONV_W:2 * CONV_W]
    conv = cw_ref[0:1] * c0 + cw_ref[1:2] * c1 + cw_ref[2:3] * u + cb_ref[...]
    yb_ref[...] = proj(C_BB, CONV_W) * conv
    cv_out_ref[:, 0:CONV_W] = c1
    cv_out_ref[:, CONV_W:2 * CONV_W] = u
    sga_ref[...] = _sigmoid(proj(C_MGA, D_MODEL))
    sgb_ref[...] = _sigmoid(proj(C_MGB, D_MODEL))


def _smp1(x, mod, gmix, w_in, lbl, cw, cb, cst):
    n = x.shape[0]
    kw = jax.ShapeDtypeStruct((n, KEY_W), F32)
    dm = jax.ShapeDtypeStruct((n, D_MODEL), F32)
    return pl.pallas_call(
        _smp1_kernel,
        out_shape=[kw, kw, kw, kw, kw, kw, dm, dm,
                   jax.ShapeDtypeStruct((n, (CONV_K - 1) * CONV_W), F32)],
        compiler_params=pltpu.CompilerParams(vmem_limit_bytes=VMEM_LIMIT),
        name="smp1",
    )(x, mod, gmix, w_in, lbl, cw, cb, cst)


def _smp2_kernel(f_ref, k_ref, q_ref, v_ref, s_ref, s_out_ref, o_ref):
    g = f_ref.shape[0]
    for i in range(g):
        for hd in range(HEADS):
            hs = slice(hd * DK, (hd + 1) * DK)

            def col(ref):
                return jnp.broadcast_to(ref[i:i + 1, hs], (DK, DK)).T

            vrow = v_ref[i:i + 1, hs]
            s_new = col(f_ref) * s_ref[i, hd] + col(k_ref) * vrow
            s_out_ref[i, hd] = s_new
            o_ref[i:i + 1, hs] = jnp.sum(col(q_ref) * s_new, axis=0, keepdims=True)


def _smp2(f, k, q, v, state):
    n = f.shape[0]
    g = SMP_GROUP
    row_spec = pl.BlockSpec((g, KEY_W), lambda i: (i, 0))
    st_spec = pl.BlockSpec((g, HEADS, DK, DK), lambda i: (i, 0, 0, 0))
    return pl.pallas_call(
        _smp2_kernel,
        grid=(n // g,),
        in_specs=[row_spec, row_spec, row_spec, row_spec, st_spec],
        out_specs=[st_spec, row_spec],
        out_shape=[jax.ShapeDtypeStruct(state.shape, F32), jax.ShapeDtypeStruct((n, KEY_W), F32)],
        compiler_params=pltpu.CompilerParams(dimension_semantics=("arbitrary",)),
        name="smp2",
    )(f, k, q, v, state)


def _smp3_kernel(x_ref, mod_ref, o_ref, gate_ref, yb_ref, sga_ref, sgb_ref, hg_ref, gffn_ref,
                 w_oh_ref, w_oc_ref, w_o_ref, wr_hl_ref, w_sgu_ref, w_sd_ref,
                 h2_all_ref, lgt_all_ref, xmid_ref, h2_ref, lgt_ref):
    del h2_all_ref, lgt_all_ref
    parts = []
    for hd in range(HEADS):
        hs = slice(hd * DK, (hd + 1) * DK)
        parts.append(_rms(o_ref[:, hs]) * hg_ref[:, hs] * gate_ref[:, hs])
    ya = jnp.concatenate(parts, axis=1)
    mixed = (sga_ref[...] * _dot(ya.astype(BF16), w_oh_ref[...])
             + sgb_ref[...] * _dot(yb_ref[...].astype(BF16), w_oc_ref[...]))
    g1 = mod_ref[:, 2 * D_MODEL:3 * D_MODEL]
    x1 = x_ref[...] + g1 * _dot(mixed.astype(BF16), w_o_ref[...])
    mod_rows = tuple(mod_ref[:, j * D_MODEL:(j + 1) * D_MODEL] for j in (3, 4, 5))
    xmid, h2b, lgt = _ffn_pre(x1, mod_rows, gffn_ref[...], w_sgu_ref[...], w_sd_ref[...],
                              wr_hl_ref[...])
    xmid_ref[...] = xmid
    _pack_rows(h2b, h2_ref)
    lgt_ref[...] = lgt


def _smp3(x, mod, o, gate, yb, sga, sgb, hg, gffn, w_oh, w_oc, w_o, wr_hl, w_sgu, w_sd,
          h2_all, lgt_all, n_prompt):
    n = x.shape[0]
    vmem_args = [x, mod, o, gate, yb, sga, sgb, hg, gffn, w_oh, w_oc, w_o, wr_hl, w_sgu, w_sd]
    blk = n_prompt // n

    def full(a):
        nd = a.ndim
        return pl.BlockSpec(a.shape, lambda i, _nd=nd: (0,) * _nd)

    return pl.pallas_call(
        _smp3_kernel,
        grid=(1,),
        in_specs=[full(a) for a in vmem_args]
                 + [pl.BlockSpec(memory_space=pl.ANY), pl.BlockSpec(memory_space=pl.ANY)],
        out_specs=[pl.BlockSpec((n, D_MODEL), lambda i: (0, 0)),
                   pl.BlockSpec((2, n, PLANE_W), lambda i: (0, blk, 0)),
                   pl.BlockSpec((N_EXP, n), lambda i: (0, blk))],
        out_shape=[jax.ShapeDtypeStruct((n, D_MODEL), F32),
                   jax.ShapeDtypeStruct(h2_all.shape, h2_all.dtype),
                   jax.ShapeDtypeStruct(lgt_all.shape, lgt_all.dtype)],
        input_output_aliases={len(vmem_args): 1, len(vmem_args) + 1: 2},
        compiler_params=pltpu.CompilerParams(
            dimension_semantics=("arbitrary",), vmem_limit_bytes=VMEM_LIMIT),
        name="smp3",
    )(*vmem_args, h2_all, lgt_all)


def _route_kernel(lgt_ref, bias_ref, idx_ref, w_ref, rank_ref, cnt_ref):
    tr = ROUTE_TILE
    n_tiles = lgt_ref.shape[1] // tr

    def tile(i, carry):
        cols = pl.ds(pl.multiple_of(i * tr, tr), tr)
        picks, weights, ranks, carry = _route_tile(lgt_ref[:, cols], bias_ref[...], carry)
        for k in range(TOP_K):
            idx_ref[k:k + 1, cols] = picks[k]
            w_ref[k:k + 1, cols] = weights[k]
            rank_ref[k:k + 1, cols] = ranks[k]
        return carry

    total = lax.fori_loop(0, n_tiles, tile, jnp.zeros((N_EXP, 1), F32))
    cnt_ref[...] = jnp.broadcast_to(total, cnt_ref.shape).astype(I32)


def _route_tile(lgt, bias, carry):
    tr = lgt.shape[1]
    neg = -jnp.inf
    scores = _sigmoid(lgt)
    sel = scores + bias
    j8 = lax.broadcasted_iota(I32, (GRP_SZ, tr), 0)
    groups = [sel[g * GRP_SZ:(g + 1) * GRP_SZ] for g in range(N_GRP)]
    gscore = []
    for grp in groups:
        m1 = jnp.max(grp, axis=0, keepdims=True)
        i1 = jnp.min(jnp.where(grp == m1, j8, GRP_SZ), axis=0, keepdims=True)
        m2 = jnp.max(jnp.where(j8 == i1, neg, grp), axis=0, keepdims=True)
        gscore.append(m1 + m2)
    kept = []
    for g in range(N_GRP):
        beaten = jnp.zeros((1, tr), I32)
        for o in range(N_GRP):
            if o < g:
                beaten = beaten + (gscore[o] >= gscore[g]).astype(I32)
            elif o > g:
                beaten = beaten + (gscore[o] > gscore[g]).astype(I32)
        kept.append(jnp.where(beaten < TOPK_GRP, groups[g], neg))
    masked = jnp.concatenate(kept, axis=0)
    ei = lax.broadcasted_iota(I32, masked.shape, 0)
    chosen = jnp.zeros(masked.shape, jnp.bool_)
    picks, weights = [], []
    for _ in range(TOP_K):
        m = jnp.max(masked, axis=0, keepdims=True)
        pick = jnp.min(jnp.where(masked == m, ei, N_EXP), axis=0, keepdims=True)
        hit = ei == pick
        weights.append(jnp.sum(jnp.where(hit, scores, 0.0), axis=0, keepdims=True))
        picks.append(pick)
        chosen = chosen | hit
        masked = jnp.where(hit, neg, masked)
    wsum = weights[0]
    for w in weights[1:]:
        wsum = wsum + w
    sel01 = chosen.astype(F32)
    r = lax.broadcasted_iota(I32, (tr, tr), 0)
    c = lax.broadcasted_iota(I32, (tr, tr), 1)
    before = (r < c).astype(BF16)
    cnt = _dot(sel01.astype(BF16), before) + carry
    weights = [w / wsum * ROUTED_SCALE for w in weights]
    ranks = [jnp.sum(jnp.where(ei == p, cnt, 0.0), axis=0, keepdims=True).astype(I32) for p in picks]
    return picks, weights, ranks, carry + jnp.sum(sel01, axis=1, keepdims=True)


def _route(lgt, bias):
    n = lgt.shape[1]
    assert n % ROUTE_TILE == 0
    slot = lambda dt: jax.ShapeDtypeStruct((TOP_K, n), dt)
    return pl.pallas_call(
        _route_kernel,
        out_shape=[slot(I32), slot(F32), slot(I32),
                   jax.ShapeDtypeStruct((N_EXP, 128), I32)],
        name="route",
    )(lgt, bias)


def _dest_kernel(start_ref, idx_ref, rank_ref, dest_ref, *, n_rows):
    n_tok = idx_ref.shape[1]
    idx = idx_ref[...]
    acc = rank_ref[...]
    for e in range(N_EXP):
        acc = acc + jnp.where(idx == e, start_ref[e], 0)
    dest_ref[:, 0:n_tok] = acc
    dest_ref[:, n_tok:2 * n_tok] = acc + n_rows


def _dest(pad_start, idx, rank, n_rows):
    k, n_tok = idx.shape
    return pl.pallas_call(
        functools.partial(_dest_kernel, n_rows=n_rows),
        in_specs=[pl.BlockSpec(memory_space=pltpu.SMEM),
                  pl.BlockSpec(memory_space=pltpu.VMEM),
                  pl.BlockSpec(memory_space=pltpu.VMEM)],
        out_specs=pl.BlockSpec(memory_space=pltpu.VMEM),
        out_shape=jax.ShapeDtypeStruct((k, 2 * n_tok), I32),
        name="dest",
    )(pad_start, idx, rank)


def _sc_mesh():
    return plsc.VectorSubcoreMesh(core_axis_name="core", subcore_axis_name="subcore")


def _dispatch(rows, dest, n_out):
    n, width = rows.shape
    win = SC_WINDOW
    steps = n // win

    @pl.kernel(out_type=jax.ShapeDtypeStruct((n_out, width), rows.dtype), mesh=_sc_mesh(),
               scratch_types=[], name="dispatch")
    def run(x_hbm, *refs):
        i_hbms, o_hbm = refs[:TOP_K], refs[TOP_K]

        def body(x_vmem, *i_vmems):
            for i_vmem in i_vmems:
                pltpu.sync_copy(x_vmem, o_hbm.at[i_vmem.at[0]])

        pltpu.emit_pipeline(
            body,
            grid=(steps,),
            in_specs=[pl.BlockSpec((win, width), lambda i: (i, 0))]
                     + [pl.BlockSpec((1, win), lambda i, k=k: (0, k * steps + i)) for k in range(TOP_K)],
            out_specs=[],
            core_axis_name=("core", "subcore"),
            dimension_semantics=(pltpu.PARALLEL,),
        )(x_hbm, *i_hbms)

    dest_flat = dest.reshape(1, TOP_K * n)
    return run(rows, *([dest_flat] * TOP_K))


def _combine(rows, dest_flat):
    width = rows.shape[1]
    n = dest_flat.shape[0]
    win = SC_WINDOW

    @pl.kernel(out_type=jax.ShapeDtypeStruct((n, width), rows.dtype), mesh=_sc_mesh(),
               scratch_types=[], name="combine")
    def run(y_hbm, i_hbm, o_hbm):
        def body(i_vmem, o_vmem):
            pltpu.sync_copy(y_hbm.at[i_vmem.at[0]], o_vmem)

        pltpu.emit_pipeline(
            body,
            grid=(n // win,),
            in_specs=[pl.BlockSpec((1, win), lambda i: (0, i))],
            out_specs=[pl.BlockSpec((win, width), lambda i: (i, 0))],
            core_axis_name=("core", "subcore"),
            dimension_semantics=(pltpu.PARALLEL,),
        )(i_hbm, o_hbm)

    return run(rows, dest_flat.reshape(1, n))


def _gmm_kernel(blk_exp_ref, n_used_ref, xs_hbm, wg_hbm, wu_hbm, wd_hbm, ys_hbm,
                xbuf, ybuf, wg32, wu32, wd32, wgu_b, wd_b, xsem, ysem, wsem, run_ref):
    bm = xbuf.shape[2]
    n_used = n_used_ref[0]

    def x_copy(b, slot):
        return pltpu.make_async_copy(xs_hbm.at[:, pl.ds(b * bm, bm), :], xbuf.at[slot], xsem.at[slot])

    def y_copy(b, slot):
        return pltpu.make_async_copy(ybuf.at[slot], ys_hbm.at[:, pl.ds(b * bm, bm), :], ysem.at[slot])

    def w_copies(e, slot):
        return (pltpu.make_async_copy(wg_hbm.at[e], wg32.at[slot], wsem.at[slot, 0]),
                pltpu.make_async_copy(wu_hbm.at[e], wu32.at[slot], wsem.at[slot, 1]),
                pltpu.make_async_copy(wd_hbm.at[e], wd32.at[slot], wsem.at[slot, 2]))

    def run_end(b):
        return lax.while_loop(lambda j: (j < n_used) & (blk_exp_ref[jnp.minimum(j, n_used - 1)] == blk_exp_ref[b]),
                              lambda j: j + 1, b + 1)

    run_ref[0] = 0
    x_copy(0, 0).start()
    for c in w_copies(blk_exp_ref[0], 0):
        c.start()

    def block(b, carry):
        slot = b % 2

        @pl.when(b + 1 < n_used)
        def _():
            x_copy(b + 1, 1 - slot).start()

        @pl.when((b == 0) | (blk_exp_ref[b] != blk_exp_ref[jnp.maximum(b - 1, 0)]))
        def _():
            wslot = run_ref[0] % 2
            run_ref[0] = run_ref[0] + 1
            for c in w_copies(blk_exp_ref[b], wslot):
                c.wait()
            wgu_b[:, 0:EXP_FF] = wg32[wslot].astype(BF16)
            wgu_b[:, EXP_FF:2 * EXP_FF] = wu32[wslot].astype(BF16)
            wd_b[...] = wd32[wslot].astype(BF16)
            nxt = run_end(b)

            @pl.when(nxt < n_used)
            def _():
                for c in w_copies(blk_exp_ref[jnp.minimum(nxt, n_used - 1)], 1 - wslot):
                    c.start()

        x_copy(b, slot).wait()

        @pl.when(b >= 2)
        def _():
            y_copy(b - 2, slot).wait()

        xc = _unpack_rows(xbuf[slot, 0], xbuf[slot, 1])
        gu = sum(_dot(c, wgu_b[i * PLANE_W:(i + 1) * PLANE_W, :]) for i, c in enumerate(xc))
        act = (_silu(gu[:, :EXP_FF]) * gu[:, EXP_FF:]).astype(BF16)
        _pack_rows(_dot(act, wd_b[...]).astype(BF16), ybuf.at[slot])
        y_copy(b, slot).start()
        return carry

    lax.fori_loop(0, n_used, block, 0)

    @pl.when(n_used >= 2)
    def _():
        y_copy(n_used - 2, n_used % 2).wait()

    y_copy(n_used - 1, (n_used - 1) % 2).wait()


def _gmm(blk_exp, n_used, xs, w_gate, w_up, w_down):
    n_rows = xs.shape[1]
    bm = GMM_BM
    nb = n_rows // bm

    assert blk_exp.shape == (nb,)
    any_spec = pl.BlockSpec(memory_space=pl.ANY)
    grid_spec = pltpu.PrefetchScalarGridSpec(
        num_scalar_prefetch=2,
        grid=(1,),
        in_specs=[any_spec, any_spec, any_spec, any_spec],
        out_specs=any_spec,
        scratch_shapes=[pltpu.VMEM((2, 2, bm, PLANE_W), I32), pltpu.VMEM((2, 2, bm, PLANE_W), I32),
                        pltpu.VMEM((2, D_MODEL, EXP_FF), F32), pltpu.VMEM((2, D_MODEL, EXP_FF), F32),
                        pltpu.VMEM((2, EXP_FF, D_MODEL), F32),
                        pltpu.VMEM((D_MODEL, 2 * EXP_FF), BF16), pltpu.VMEM((EXP_FF, D_MODEL), BF16),
                        pltpu.SemaphoreType.DMA((2,)), pltpu.SemaphoreType.DMA((2,)),
                        pltpu.SemaphoreType.DMA((2, 3)), pltpu.SMEM((1,), I32)],
    )
    return pl.pallas_call(
        _gmm_kernel,
        grid_spec=grid_spec,
        out_shape=jax.ShapeDtypeStruct((2, n_rows, PLANE_W), I32),
        compiler_params=pltpu.CompilerParams(dimension_semantics=("arbitrary",)),
        name="gmm",
    )(blk_exp, n_used, xs, w_gate, w_up, w_down)


def _final_kernel(xmid_ref, g2_ref, z_ref, w_ref, gfin_ref, *rest):
    y_ref = rest[-1]
    accs = [jnp.zeros((xmid_ref.shape[0], PLANE_W), F32) for _ in range(4)]
    w_cols = w_ref[...].T
    for k in range(TOP_K):
        wk = w_cols[:, k:k + 1]
        cols = _unpack_rows(z_ref[k, 0], z_ref[k, 1])
        accs = [a + wk * c.astype(F32) for a, c in zip(accs, cols)]
    acc = jnp.concatenate(accs, axis=1)
    y_ref[...] = _rms(xmid_ref[...] + g2_ref[0] * acc) * gfin_ref[...]


def _final(xmid, g2, z, w_t, gfin, tile, *, n_tiles, x_tile0, z_tile0, w_tile0, tiles_per_g2, y_prev=None):
    args = [xmid, g2, z, w_t, gfin]
    in_specs = [pl.BlockSpec((tile, D_MODEL), lambda i: (x_tile0 + i, 0)),
                pl.BlockSpec((1, g2.shape[1], D_MODEL), lambda i: ((x_tile0 + i) // tiles_per_g2, 0, 0)),
                pl.BlockSpec((TOP_K, 2, tile, PLANE_W), lambda i: (0, 0, z_tile0 + i, 0)),
                pl.BlockSpec((TOP_K, tile), lambda i: (0, w_tile0 + i)),
                pl.BlockSpec((1, D_MODEL), lambda i: (0, 0))]
    aliases = {}
    if y_prev is not None:
        args.append(y_prev)
        in_specs.append(pl.BlockSpec(memory_space=pl.ANY))
        aliases = {len(args) - 1: 0}
    return pl.pallas_call(
        _final_kernel,
        grid=(n_tiles,),
        in_specs=in_specs,
        out_specs=pl.BlockSpec((tile, D_MODEL), lambda i: (x_tile0 + i, 0)),
        out_shape=jax.ShapeDtypeStruct(xmid.shape, F32),
        input_output_aliases=aliases,
        compiler_params=pltpu.CompilerParams(dimension_semantics=("arbitrary",)),
        name="final",
    )(*args)


def kernel(x_prompt, x_sample, state_hgrn, state_conv, c_prompt, c_sample, w_ada, b_ada, norm_mix_g, norm_ffn_g, w_in, lb_logits, hgrn_norm_g, conv_w, conv_b, w_out_hgrn, w_out_conv, w_o, w_router, router_bias, w_exp_gate, w_exp_up, w_exp_down, w_sh_gate, w_sh_up, w_sh_down, final_norm_g):
    assert w_ada.shape[0] == 1 and lb_logits.shape[0] == 2
    bsz, seq, _ = x_prompt.shape
    n_smp = x_sample.shape[0]
    n_prompt = bsz * seq
    n_tok = n_prompt + n_smp

    w_in_b = w_in[0].astype(BF16)
    w_oh_b = w_out_hgrn[0].astype(BF16)
    w_oc_b = w_out_conv[0].astype(BF16)
    w_o_b = w_o[0].astype(BF16)
    wr_t = w_router[0].T
    wr_hi = wr_t.astype(BF16)
    wr_hl = jnp.concatenate([wr_hi, (wr_t - wr_hi.astype(F32)).astype(BF16)], axis=0)
    w_sgu = jnp.concatenate([w_sh_gate[0], w_sh_up[0]], axis=1).astype(BF16)
    w_sd = w_sh_down[0].astype(BF16)
    gmix = norm_mix_g[0].reshape(1, D_MODEL)
    gffn = norm_ffn_g[0].reshape(1, D_MODEL)
    hg = hgrn_norm_g[0].reshape(1, KEY_W)
    cw = conv_w[0]
    cb = conv_b[0].reshape(1, CONV_W)
    gfin = final_norm_g.reshape(1, D_MODEL)

    mod_p, mod_s = _ada(c_prompt, c_sample, w_ada[0], b_ada[0])

    xmid_p, h2_all, lgt_all, s_p, cv_p = _mix(
        x_prompt, mod_p.reshape(bsz, 6, D_MODEL), n_tok, gmix, gffn, w_in_b, lb_logits, hg, cw, cb, w_oh_b, w_oc_b, w_o_b, wr_hl, w_sgu, w_sd)

    xs2 = x_sample.reshape(n_smp, D_MODEL)
    f, kk, q, v, gate, yb, sga, sgb, cv_s = _smp1(
        xs2, mod_s, gmix, w_in_b, lb_logits, cw, cb, state_conv[0].reshape(n_smp, (CONV_K - 1) * CONV_W))
    s_s, o_s = _smp2(f, kk, q, v, state_hgrn[0])
    xmid_s, h2_all, lgt_all = _smp3(xs2, mod_s, o_s, gate, yb, sga, sgb, hg, gffn,
                                    w_oh_b, w_oc_b, w_o_b, wr_hl, w_sgu, w_sd,
                                    h2_all, lgt_all, n_prompt)

    idx, w_tok, rank, cnt = _route(lgt_all, router_bias[0].reshape(N_EXP, 1))

    bm = GMM_BM
    n_blocks = (n_tok * TOP_K + N_EXP * (bm - 1)) // bm
    n_rows = n_blocks * bm
    counts = cnt[:, 0]
    padded = (counts + bm - 1) // bm * bm
    pad_end = jnp.cumsum(padded)
    pad_start = pad_end - padded
    blk_row0 = jnp.arange(n_blocks, dtype=I32) * bm
    blk_exp = jnp.minimum(jnp.sum((pad_end[None, :] <= blk_row0[:, None]).astype(I32), axis=1), N_EXP - 1)
    n_used = (pad_end[-1:] // bm).astype(I32)
    dest = _dest(pad_start.astype(I32), idx, rank, n_rows)

    xs = _dispatch(h2_all.reshape(2 * n_tok, PLANE_W), dest, 2 * n_rows).reshape(2, n_rows, PLANE_W)
    ys = _gmm(blk_exp, n_used, xs, w_exp_gate[0], w_exp_up[0], w_exp_down[0])

    ys_flat = ys.reshape(2 * n_rows, PLANE_W)
    dest3 = dest.reshape(TOP_K, 2, n_tok)
    w_t = w_tok
    xmid_p2 = xmid_p.reshape(n_prompt, D_MODEL)
    g2_p = mod_p[:, 5 * D_MODEL:].reshape(bsz, 1, D_MODEL)
    g2_s = mod_s[:, 5 * D_MODEL:].reshape(1, n_smp, D_MODEL)
    chunk = n_prompt // FINAL_CHUNKS
    y_p = None
    for c in range(FINAL_CHUNKS):
        c0 = c * chunk
        c1 = n_tok if c == FINAL_CHUNKS - 1 else c0 + chunk
        z = _combine(ys_flat, dest3[:, :, c0:c1].reshape(-1)).reshape(TOP_K, 2, c1 - c0, PLANE_W)
        y_p = _final(xmid_p2, g2_p, z, w_t, gfin, FINAL_TILE, n_tiles=chunk // FINAL_TILE,
                     x_tile0=c0 // FINAL_TILE, z_tile0=0, w_tile0=c0 // FINAL_TILE,
                     tiles_per_g2=seq // FINAL_TILE, y_prev=y_p)
    y_s = _final(xmid_s, g2_s, z, w_t, gfin, n_smp, n_tiles=1, x_tile0=0, z_tile0=chunk // n_smp,
                 w_tile0=n_prompt // n_smp, tiles_per_g2=1)

    return (y_p.reshape(bsz, seq, D_MODEL), y_s.reshape(n_smp, 1, D_MODEL),
            s_p[None], cv_p[None], s_s[None], cv_s.reshape(1, n_smp, CONV_K - 1, CONV_W))
```

```python
import functools

import jax
import jax.numpy as jnp
from jax import lax
from jax.experimental import pallas as pl
from jax.experimental.pallas import tpu as pltpu
from jax.experimental.pallas import tpu_sc as plsc

F32 = jnp.float32
BF16 = jnp.bfloat16
I32 = jnp.int32

D_MODEL = 1024
HALF_D = D_MODEL // 2
HEADS = 4
DK = 128
KEY_W = HEADS * DK
CONV_W = 512
CONV_K = 3
IN_W = 2 * KEY_W + 2 * KEY_W + 3 * CONV_W + 2 * D_MODEL
N_EXP = 64
TOP_K = 8
N_GRP = 8
GRP_SZ = N_EXP // N_GRP
TOPK_GRP = 4
EXP_FF = 256
SH_FF = 256
ROUTED_SCALE = 2.5
EPS = 1e-6

C_Q, C_F, C_I, C_G = 0, 512, 1024, 1536
C_BB, C_CC, C_VB = 2048, 2560, 3072
C_MGA, C_MGB = 3584, 4608

MIX_TILE = 512
SUB = 256
CHUNK = 64
ROUTE_TILE = 384
GMM_BM = 512
GMM_NX = 4
GMM_NY = 3
FINAL_TILE = 256
FINAL_CHUNKS = 4
SMP_GROUP = 8
SC_WINDOW = 128
PLANE_W = HALF_D // 2
VMEM_LIMIT = 56 * 1024 * 1024


def _dot(a, b):
    return jnp.dot(a, b, preferred_element_type=F32)


def _dot_nt(a, b):
    return lax.dot_general(a, b, (((1,), (1,)), ((), ())), preferred_element_type=F32)


def _dot_tn(a, b):
    return lax.dot_general(a, b, (((0,), (0,)), ((), ())), preferred_element_type=F32)


def _sigmoid(x):
    return 0.5 * jnp.tanh(0.5 * x) + 0.5


def _silu(x):
    h = 0.5 * x
    return h * jnp.tanh(h) + h


def _rms(x):
    return x * lax.rsqrt(jnp.mean(x * x, axis=-1, keepdims=True) + EPS)


def _lower_bound(lbl):
    a, b = lbl[0:1], lbl[1:2]
    m = jnp.maximum(a, b)
    ea, eb = jnp.exp(a - m), jnp.exp(b - m)
    return ea / (ea + eb)


def _split3(x):
    hi = x.astype(BF16)
    r1 = x - hi.astype(F32)
    mid = r1.astype(BF16)
    lo = (r1 - mid.astype(F32)).astype(BF16)
    return hi, mid, lo


def _words(lo_b, hi_b):
    lo = lax.shift_right_logical(lax.bitcast_convert_type(lo_b.astype(F32), I32), 16)
    hi = lax.bitcast_convert_type(hi_b.astype(F32), I32) & jnp.int32(-65536)
    return lo | hi


def _halves(w):
    lo = lax.bitcast_convert_type(lax.shift_left(w, 16), F32)
    hi = lax.bitcast_convert_type(w & jnp.int32(-65536), F32)
    return lo.astype(BF16), hi.astype(BF16)


def _pack_rows(xb, out_ref):
    words = _words(xb[:, :HALF_D], xb[:, HALF_D:])
    out_ref[0] = words[:, :PLANE_W]
    out_ref[1] = words[:, PLANE_W:]


def _unpack_rows(p0, p1):
    c0, c2 = _halves(p0)
    c1, c3 = _halves(p1)
    return c0, c1, c2, c3


def _ada_kernel(cp_ref, cs_ref, w_ref, b_ref, op_ref, os_ref):
    w = w_ref[...].astype(BF16)
    for c_ref, o_ref in ((cp_ref, op_ref), (cs_ref, os_ref)):
        o_ref[...] = _dot(_silu(c_ref[...]).astype(BF16), w) + b_ref[...]


def _ada(c_prompt, c_sample, w_ada, b_ada):
    blk = 1024
    rows = lambda c: pl.BlockSpec((c.shape[0], D_MODEL), lambda j: (0, 0))
    cols = lambda c: pl.BlockSpec((c.shape[0], blk), lambda j: (0, j))
    return pl.pallas_call(
        _ada_kernel,
        grid=(6 * D_MODEL // blk,),
        in_specs=[rows(c_prompt), rows(c_sample),
                  pl.BlockSpec((D_MODEL, blk), lambda j: (0, j)),
                  pl.BlockSpec((1, blk), lambda j: (0, j))],
        out_specs=[cols(c_prompt), cols(c_sample)],
        out_shape=[jax.ShapeDtypeStruct((c.shape[0], 6 * D_MODEL), F32) for c in (c_prompt, c_sample)],
        name="ada",
    )(c_prompt, c_sample, w_ada, b_ada.reshape(1, -1))


def _ffn_pre(x1, mod_rows, gffn, w_sgu, w_sd, wr_hl):
    sh2, sc2, g2 = mod_rows
    h2 = _rms(x1) * gffn * (1.0 + sc2) + sh2
    h2b = h2.astype(BF16)
    gu = _dot(h2b, w_sgu)
    act = _silu(gu[:, :SH_FF]) * gu[:, SH_FF:]
    xmid = x1 + g2 * _dot(act.astype(BF16), w_sd)
    h2lo = (h2 - h2b.astype(F32)).astype(BF16)
    both = _dot_nt(wr_hl, h2b)
    lgt = both[:N_EXP] + both[N_EXP:] + _dot_nt(wr_hl[:N_EXP], h2lo)
    return xmid, h2b, lgt


def _mix_kernel(*refs, nt, n_tiles):
    i = pl.program_id(0)
    h2_ref, lgt_ref = refs[16], refs[17]

    @pl.when(i == n_tiles)
    def _():
        h2_ref[...] = jnp.zeros_like(h2_ref)
        lgt_ref[...] = jnp.zeros_like(lgt_ref)

    @pl.when(i < n_tiles)
    def _():
        _mix_tile(i % nt, nt, *refs)


def _mix_tile(t, nt, x_ref, mod_ref, gmix_ref, gffn_ref, w_in_ref, lbl_ref, hg_ref, cw_ref, cb_ref,
              w_oh_ref, w_oc_ref, w_o_ref, wr_hl_ref, w_sgu_ref, w_sd_ref,
              xmid_ref, h2_ref, lgt_ref, s_out_ref, cv_out_ref,
              proj_ref, st_ref, cbuf_ref, ya_ref):
    tt = x_ref.shape[1]

    @pl.when(t == 0)
    def _():
        st_ref[...] = jnp.zeros_like(st_ref)
        cbuf_ref[...] = jnp.zeros_like(cbuf_ref)

    x = x_ref[0]
    mod = mod_ref[0]
    sh1, sc1, g1 = mod[0:1], mod[1:2], mod[2:3]
    h = _rms(x) * gmix_ref[...] * (1.0 + sc1) + sh1
    hb = h.astype(BF16)
    for c in range(0, IN_W, 512):
        proj_ref[:, c:c + 512] = _dot(hb, w_in_ref[:, c:c + 512])

    lb = _lower_bound(lbl_ref[...])
    row = lax.broadcasted_iota(I32, (SUB, SUB), 0)
    col = lax.broadcasted_iota(I32, (SUB, SUB), 1)
    tri = (col <= row).astype(BF16)
    mask_d = (col <= row) & (row // CHUNK == col // CHUNK)
    mask_a = row // (2 * CHUNK) == col // (2 * CHUNK)
    n_ch = SUB // CHUNK

    def by_chunk(vals):
        return jnp.concatenate([jnp.zeros((CHUNK, DK), F32) if v is None
                                else jnp.broadcast_to(v, (CHUNK, DK)) for v in vals], axis=0)

    for s in range(tt // SUB):
        r0 = s * SUB
        f = lb + (1.0 - lb) * _sigmoid(proj_ref[r0:r0 + SUB, C_F:C_F + KEY_W])
        kk = 1.0 - f
        hi, mid, lo = _split3(jnp.log(f))
        bc = _dot(tri, hi) + _dot(tri, mid) + _dot(tri, lo)
        for hd in range(HEADS):
            hs = slice(hd * DK, (hd + 1) * DK)
            bh = bc[:, hs]
            at = lambda r: bh[r:r + 1]
            mids = [at(c * CHUNK + CHUNK // 2 - 1) for c in range(n_ch)]
            pair_mid = [at(CHUNK - 1), at(3 * CHUNK - 1)]
            step_mid, step_end = at(2 * CHUNK - 1), at(SUB - 1)
            arg = bh - by_chunk(mids)
            e_pos, e_neg = jnp.exp(arg), jnp.exp(-arg)
            q = _silu(proj_ref[r0:r0 + SUB, C_Q + hd * DK:C_Q + (hd + 1) * DK])
            v = proj_ref[r0:r0 + SUB, C_I + hd * DK:C_I + (hd + 1) * DK]
            qd = q * e_pos
            kd = kk[:, hs] * e_neg
            q_in = qd * by_chunk([jnp.exp(m) for m in mids])
            k_end = kd * by_chunk([jnp.exp(step_end - m) for m in mids])
            qa = qd * by_chunk([None, jnp.exp(mids[1] - pair_mid[0]), None, jnp.exp(mids[3] - pair_mid[1])])
            ka = kd * by_chunk([jnp.exp(pair_mid[0] - mids[0]), None, jnp.exp(pair_mid[1] - mids[2]), None])
            qb = qd * by_chunk([None, None, jnp.exp(mids[2] - step_mid), jnp.exp(mids[3] - step_mid)])
            kb = kd * by_chunk([jnp.exp(step_mid - mids[0]), jnp.exp(step_mid - mids[1]), None, None])
            att = jnp.where(mask_d, _dot_nt(qd.astype(BF16), kd.astype(BF16)), 0.0)
            att = att + jnp.where(mask_a, _dot_nt(qa.astype(BF16), ka.astype(BF16)), 0.0)
            att = att + _dot_nt(qb.astype(BF16), kb.astype(BF16))
            vb = v.astype(BF16)
            st = st_ref[hd]
            o = _dot(att.astype(BF16), vb) + _dot_nt(q_in.astype(BF16), st.astype(BF16))
            st_ref[hd] = st * jnp.exp(step_end) + _dot_tn(vb, k_end.astype(BF16))
            gate = _silu(proj_ref[r0:r0 + SUB, C_G + hd * DK:C_G + (hd + 1) * DK])
            ya_ref[r0:r0 + SUB, hs] = _rms(o) * hg_ref[:, hs] * gate

    u = proj_ref[:, C_CC:C_CC + CONV_W] * proj_ref[:, C_VB:C_VB + CONV_W]
    rows = lax.broadcasted_iota(I32, (tt, CONV_W), 0)
    c0, c1 = cbuf_ref[0:1], cbuf_ref[1:2]
    u1 = jnp.where(rows == 0, c1, pltpu.roll(u, 1, axis=0))
    u2 = jnp.where(rows == 0, c0, jnp.where(rows == 1, c1, pltpu.roll(u, 2, axis=0)))
    conv = cw_ref[0:1] * u2 + cw_ref[1:2] * u1 + cw_ref[2:3] * u + cb_ref[...]
    yb = proj_ref[:, C_BB:C_BB + CONV_W] * conv
    cbuf_ref[...] = u[tt - 2:tt]

    mixed = (_sigmoid(proj_ref[:, C_MGA:C_MGA + D_MODEL]) * _dot(ya_ref[...].astype(BF16), w_oh_ref[...])
             + _sigmoid(proj_ref[:, C_MGB:C_MGB + D_MODEL]) * _dot(yb.astype(BF16), w_oc_ref[...]))
    x1 = x + g1 * _dot(mixed.astype(BF16), w_o_ref[...])

    xmid, h2b, lgt = _ffn_pre(x1, (mod[3:4], mod[4:5], mod[5:6]), gffn_ref[...],
                              w_sgu_ref[...], w_sd_ref[...], wr_hl_ref[...])
    xmid_ref[0] = xmid
    _pack_rows(h2b, h2_ref)
    lgt_ref[...] = lgt

    @pl.when(t == nt - 1)
    def _():
        for hd in range(HEADS):
            s_out_ref[0, hd] = st_ref[hd].T
        cv_out_ref[0] = cbuf_ref[...]


def _const_spec(shape):
    nd = len(shape)
    return pl.BlockSpec(shape, lambda i, _nd=nd: (0,) * _nd, pipeline_mode=pl.Buffered(1))


def _mix(x, mod, n_tok, gmix, gffn, w_in, lbl, hg, cw, cb, w_oh, w_oc, w_o, wr_hl, w_sgu, w_sd):
    bsz, seq, _ = x.shape
    tt = MIX_TILE
    nt = seq // tt
    n_tiles = bsz * nt
    assert n_tiles * tt < n_tok <= (n_tiles + 1) * tt
    consts = [gmix, gffn, w_in, lbl, hg, cw, cb, w_oh, w_oc, w_o, wr_hl, w_sgu, w_sd]
    tile = lambda i: jnp.minimum(i, n_tiles - 1)
    return pl.pallas_call(
        functools.partial(_mix_kernel, nt=nt, n_tiles=n_tiles),
        grid=(n_tiles + 1,),
        in_specs=[pl.BlockSpec((1, tt, D_MODEL), lambda i: (tile(i) // nt, tile(i) % nt, 0)),
                  pl.BlockSpec((1, 6, D_MODEL), lambda i: (tile(i) // nt, 0, 0))]
                 + [_const_spec(a.shape) for a in consts],
        out_specs=[pl.BlockSpec((1, tt, D_MODEL), lambda i: (tile(i) // nt, tile(i) % nt, 0)),
                   pl.BlockSpec((2, tt, PLANE_W), lambda i: (0, i, 0)),
                   pl.BlockSpec((N_EXP, tt), lambda i: (0, i)),
                   pl.BlockSpec((1, HEADS, DK, DK), lambda i: (tile(i) // nt, 0, 0, 0)),
                   pl.BlockSpec((1, CONV_K - 1, CONV_W), lambda i: (tile(i) // nt, 0, 0))],
        out_shape=[jax.ShapeDtypeStruct((bsz, seq, D_MODEL), F32),
                   jax.ShapeDtypeStruct((2, n_tok, PLANE_W), I32),
                   jax.ShapeDtypeStruct((N_EXP, n_tok), F32),
                   jax.ShapeDtypeStruct((bsz, HEADS, DK, DK), F32),
                   jax.ShapeDtypeStruct((bsz, CONV_K - 1, CONV_W), F32)],
        scratch_shapes=[pltpu.VMEM((tt, IN_W), F32),
                        pltpu.VMEM((HEADS, DK, DK), F32),
                        pltpu.VMEM((CONV_K - 1, CONV_W), F32),
                        pltpu.VMEM((tt, KEY_W), F32)],
        compiler_params=pltpu.CompilerParams(
            dimension_semantics=("arbitrary",), vmem_limit_bytes=VMEM_LIMIT),
        name="mix",
    )(x, mod, *consts)


def _smp1_kernel(x_ref, mod_ref, gmix_ref, w_in_ref, lbl_ref, cw_ref, cb_ref, cst_ref,
                 f_ref, k_ref, q_ref, v_ref, gate_ref, yb_ref, sga_ref, sgb_ref, cv_out_ref):
    x = x_ref[...]
    sh1, sc1 = mod_ref[:, 0:D_MODEL], mod_ref[:, D_MODEL:2 * D_MODEL]
    h = _rms(x) * gmix_ref[...] * (1.0 + sc1) + sh1
    hb = h.astype(BF16)

    def proj(c, w):
        return _dot(hb, w_in_ref[:, c:c + w])

    lb = _lower_bound(lbl_ref[...])
    f = lb + (1.0 - lb) * _sigmoid(proj(C_F, KEY_W))
    f_ref[...] = f
    k_ref[...] = 1.0 - f
    q_ref[...] = _silu(proj(C_Q, KEY_W))
    v_ref[...] = proj(C_I, KEY_W)
    gate_ref[...] = _silu(proj(C_G, KEY_W))
    u = proj(C_CC, CONV_W) * proj(C_VB, CONV_W)
    c0, c1 = cst_ref[:, 0:CONV_W], cst_ref[:, CONV_W:2 * CONV_W]
    conv = cw_ref[0:1] * c0 + cw_ref[1:2] * c1 + cw_ref[2:3] * u + cb_ref[...]
    yb_ref[...] = proj(C_BB, CONV_W) * conv
    cv_out_ref[:, 0:CONV_W] = c1
    cv_out_ref[:, CONV_W:2 * CONV_W] = u
    sga_ref[...] = _sigmoid(proj(C_MGA, D_MODEL))
    sgb_ref[...] = _sigmoid(proj(C_MGB, D_MODEL))


def _smp1(x, mod, gmix, w_in, lbl, cw, cb, cst):
    n = x.shape[0]
    kw = jax.ShapeDtypeStruct((n, KEY_W), F32)
    dm = jax.ShapeDtypeStruct((n, D_MODEL), F32)
    return pl.pallas_call(
        _smp1_kernel,
        out_shape=[kw, kw, kw, kw, kw, kw, dm, dm,
                   jax.ShapeDtypeStruct((n, (CONV_K - 1) * CONV_W), F32)],
        compiler_params=pltpu.CompilerParams(vmem_limit_bytes=VMEM_LIMIT),
        name="smp1",
    )(x, mod, gmix, w_in, lbl, cw, cb, cst)


def _smp2_kernel(f_ref, k_ref, q_ref, v_ref, s_ref, s_out_ref, o_ref):
    g = f_ref.shape[0]
    for i in range(g):
        for hd in range(HEADS):
            hs = slice(hd * DK, (hd + 1) * DK)

            def col(ref):
                return jnp.broadcast_to(ref[i:i + 1, hs], (DK, DK)).T

            vrow = v_ref[i:i + 1, hs]
            s_new = col(f_ref) * s_ref[i, hd] + col(k_ref) * vrow
            s_out_ref[i, hd] = s_new
            o_ref[i:i + 1, hs] = jnp.sum(col(q_ref) * s_new, axis=0, keepdims=True)


def _smp2(f, k, q, v, state):
    n = f.shape[0]
    g = SMP_GROUP
    row_spec = pl.BlockSpec((g, KEY_W), lambda i: (i, 0))
    st_spec = pl.BlockSpec((g, HEADS, DK, DK), lambda i: (i, 0, 0, 0))
    return pl.pallas_call(
        _smp2_kernel,
        grid=(n // g,),
        in_specs=[row_spec, row_spec, row_spec, row_spec, st_spec],
        out_specs=[st_spec, row_spec],
        out_shape=[jax.ShapeDtypeStruct(state.shape, F32), jax.ShapeDtypeStruct((n, KEY_W), F32)],
        compiler_params=pltpu.CompilerParams(dimension_semantics=("arbitrary",)),
        name="smp2",
    )(f, k, q, v, state)


def _smp3_kernel(x_ref, mod_ref, o_ref, gate_ref, yb_ref, sga_ref, sgb_ref, hg_ref, gffn_ref,
                 w_oh_ref, w_oc_ref, w_o_ref, wr_hl_ref, w_sgu_ref, w_sd_ref,
                 h2_all_ref, lgt_all_ref, xmid_ref, h2_ref, lgt_ref):
    del h2_all_ref, lgt_all_ref
    parts = []
    for hd in range(HEADS):
        hs = slice(hd * DK, (hd + 1) * DK)
        parts.append(_rms(o_ref[:, hs]) * hg_ref[:, hs] * gate_ref[:, hs])
    ya = jnp.concatenate(parts, axis=1)
    mixed = (sga_ref[...] * _dot(ya.astype(BF16), w_oh_ref[...])
             + sgb_ref[...] * _dot(yb_ref[...].astype(BF16), w_oc_ref[...]))
    g1 = mod_ref[:, 2 * D_MODEL:3 * D_MODEL]
    x1 = x_ref[...] + g1 * _dot(mixed.astype(BF16), w_o_ref[...])
    mod_rows = tuple(mod_ref[:, j * D_MODEL:(j + 1) * D_MODEL] for j in (3, 4, 5))
    xmid, h2b, lgt = _ffn_pre(x1, mod_rows, gffn_ref[...], w_sgu_ref[...], w_sd_ref[...],
                              wr_hl_ref[...])
    xmid_ref[...] = xmid
    _pack_rows(h2b, h2_ref)
    lgt_ref[...] = lgt


def _smp3(x, mod, o, gate, yb, sga, sgb, hg, gffn, w_oh, w_oc, w_o, wr_hl, w_sgu, w_sd,
          h2_all, lgt_all, n_prompt):
    n = x.shape[0]
    vmem_args = [x, mod, o, gate, yb, sga, sgb, hg, gffn, w_oh, w_oc, w_o, wr_hl, w_sgu, w_sd]
    blk = n_prompt // n

    def full(a):
        nd = a.ndim
        return pl.BlockSpec(a.shape, lambda i, _nd=nd: (0,) * _nd)

    return pl.pallas_call(
        _smp3_kernel,
        grid=(1,),
        in_specs=[full(a) for a in vmem_args]
                 + [pl.BlockSpec(memory_space=pl.ANY), pl.BlockSpec(memory_space=pl.ANY)],
        out_specs=[pl.BlockSpec((n, D_MODEL), lambda i: (0, 0)),
                   pl.BlockSpec((2, n, PLANE_W), lambda i: (0, blk, 0)),
                   pl.BlockSpec((N_EXP, n), lambda i: (0, blk))],
        out_shape=[jax.ShapeDtypeStruct((n, D_MODEL), F32),
                   jax.ShapeDtypeStruct(h2_all.shape, h2_all.dtype),
                   jax.ShapeDtypeStruct(lgt_all.shape, lgt_all.dtype)],
        input_output_aliases={len(vmem_args): 1, len(vmem_args) + 1: 2},
        compiler_params=pltpu.CompilerParams(
            dimension_semantics=("arbitrary",), vmem_limit_bytes=VMEM_LIMIT),
        name="smp3",
    )(*vmem_args, h2_all, lgt_all)


def _route_kernel(lgt_ref, bias_ref, idx_ref, w_ref, rank_ref, cnt_ref):
    tr = ROUTE_TILE
    n_tiles = lgt_ref.shape[1] // tr

    def tile(i, carry):
        cols = pl.ds(pl.multiple_of(i * tr, tr), tr)
        picks, weights, ranks, carry = _route_tile(lgt_ref[:, cols], bias_ref[...], carry)
        for k in range(TOP_K):
            idx_ref[k:k + 1, cols] = picks[k]
            w_ref[k:k + 1, cols] = weights[k]
            rank_ref[k:k + 1, cols] = ranks[k]
        return carry

    total = lax.fori_loop(0, n_tiles, tile, jnp.zeros((N_EXP, 1), F32))
    cnt_ref[...] = jnp.broadcast_to(total, cnt_ref.shape).astype(I32)


def _route_tile(lgt, bias, carry):
    tr = lgt.shape[1]
    neg = -jnp.inf
    scores = _sigmoid(lgt)
    sel = scores + bias
    j8 = lax.broadcasted_iota(I32, (GRP_SZ, tr), 0)
    groups = [sel[g * GRP_SZ:(g + 1) * GRP_SZ] for g in range(N_GRP)]
    gscore = []
    for grp in groups:
        m1 = jnp.max(grp, axis=0, keepdims=True)
        i1 = jnp.min(jnp.where(grp == m1, j8, GRP_SZ), axis=0, keepdims=True)
        m2 = jnp.max(jnp.where(j8 == i1, neg, grp), axis=0, keepdims=True)
        gscore.append(m1 + m2)
    kept = []
    for g in range(N_GRP):
        beaten = jnp.zeros((1, tr), I32)
        for o in range(N_GRP):
            if o < g:
                beaten = beaten + (gscore[o] >= gscore[g]).astype(I32)
            elif o > g:
                beaten = beaten + (gscore[o] > gscore[g]).astype(I32)
        kept.append(jnp.where(beaten < TOPK_GRP, groups[g], neg))
    masked = jnp.concatenate(kept, axis=0)
    ei = lax.broadcasted_iota(I32, masked.shape, 0)
    chosen = jnp.zeros(masked.shape, jnp.bool_)
    picks, weights = [], []
    for _ in range(TOP_K):
        m = jnp.max(masked, axis=0, keepdims=True)
        pick = jnp.min(jnp.where(masked == m, ei, N_EXP), axis=0, keepdims=True)
        hit = ei == pick
        weights.append(jnp.sum(jnp.where(hit, scores, 0.0), axis=0, keepdims=True))
        picks.append(pick)
        chosen = chosen | hit
        masked = jnp.where(hit, neg, masked)
    wsum = weights[0]
    for w in weights[1:]:
        wsum = wsum + w
    sel01 = chosen.astype(F32)
    r = lax.broadcasted_iota(I32, (tr, tr), 0)
    c = lax.broadcasted_iota(I32, (tr, tr), 1)
    before = (r < c).astype(BF16)
    cnt = _dot(sel01.astype(BF16), before) + carry
    weights = [w / wsum * ROUTED_SCALE for w in weights]
    ranks = [jnp.sum(jnp.where(ei == p, cnt, 0.0), axis=0, keepdims=True).astype(I32) for p in picks]
    return picks, weights, ranks, carry + jnp.sum(sel01, axis=1, keepdims=True)


def _route(lgt, bias):
    n = lgt.shape[1]
    assert n % ROUTE_TILE == 0
    slot = lambda dt: jax.ShapeDtypeStruct((TOP_K, n), dt)
    return pl.pallas_call(
        _route_kernel,
        out_shape=[slot(I32), slot(F32), slot(I32),
                   jax.ShapeDtypeStruct((N_EXP, 128), I32)],
        name="route",
    )(lgt, bias)


def _dest_kernel(start_ref, idx_ref, rank_ref, dest_ref, *, n_rows):
    n_tok = idx_ref.shape[1]
    idx = idx_ref[...]
    acc = rank_ref[...]
    for e in range(N_EXP):
        acc = acc + jnp.where(idx == e, start_ref[e], 0)
    dest_ref[:, 0:n_tok] = acc
    dest_ref[:, n_tok:2 * n_tok] = acc + n_rows


def _dest(pad_start, idx, rank, n_rows):
    k, n_tok = idx.shape
    return pl.pallas_call(
        functools.partial(_dest_kernel, n_rows=n_rows),
        in_specs=[pl.BlockSpec(memory_space=pltpu.SMEM),
                  pl.BlockSpec(memory_space=pltpu.VMEM),
                  pl.BlockSpec(memory_space=pltpu.VMEM)],
        out_specs=pl.BlockSpec(memory_space=pltpu.VMEM),
        out_shape=jax.ShapeDtypeStruct((k, 2 * n_tok), I32),
        name="dest",
    )(pad_start, idx, rank)


def _sc_mesh():
    return plsc.VectorSubcoreMesh(core_axis_name="core", subcore_axis_name="subcore")


def _dispatch(rows, dest, n_out):
    n, width = rows.shape
    win = SC_WINDOW
    steps = n // win

    @pl.kernel(out_type=jax.ShapeDtypeStruct((n_out, width), rows.dtype), mesh=_sc_mesh(),
               scratch_types=[], name="dispatch")
    def run(x_hbm, *refs):
        i_hbms, o_hbm = refs[:TOP_K], refs[TOP_K]

        def body(x_vmem, *i_vmems):
            for i_vmem in i_vmems:
                pltpu.sync_copy(x_vmem, o_hbm.at[i_vmem.at[0]])

        pltpu.emit_pipeline(
            body,
            grid=(steps,),
            in_specs=[pl.BlockSpec((win, width), lambda i: (i, 0))]
                     + [pl.BlockSpec((1, win), lambda i, k=k: (0, k * steps + i)) for k in range(TOP_K)],
            out_specs=[],
            core_axis_name=("core", "subcore"),
            dimension_semantics=(pltpu.PARALLEL,),
        )(x_hbm, *i_hbms)

    dest_flat = dest.reshape(1, TOP_K * n)
    return run(rows, *([dest_flat] * TOP_K))


def _combine(rows, dest_flat):
    width = rows.shape[1]
    n = dest_flat.shape[0]
    win = SC_WINDOW

    @pl.kernel(out_type=jax.ShapeDtypeStruct((n, width), rows.dtype), mesh=_sc_mesh(),
               scratch_types=[], name="combine")
    def run(y_hbm, i_hbm, o_hbm):
        def body(i_vmem, o_vmem):
            pltpu.sync_copy(y_hbm.at[i_vmem.at[0]], o_vmem)

        pltpu.emit_pipeline(
            body,
            grid=(n // win,),
            in_specs=[pl.BlockSpec((1, win), lambda i: (0, i))],
            out_specs=[pl.BlockSpec((win, width), lambda i: (i, 0))],
            core_axis_name=("core", "subcore"),
            dimension_semantics=(pltpu.PARALLEL,),
        )(i_hbm, o_hbm)

    return run(rows, dest_flat.reshape(1, n))


def _gmm_kernel(blk_exp_ref, n_used_ref, xs_hbm, wg_hbm, wu_hbm, wd_hbm, ys_hbm,
                xbuf, ybuf, wg32, wu32, wd32, wgu_b, wd_b, xsem, ysem, wsem, run_ref):
    nx, ny, bm = xbuf.shape[0], ybuf.shape[0], xbuf.shape[2]
    n_used = n_used_ref[0]

    def x_copies(b):
        rows, slot = pl.ds(b * bm, bm), b % nx
        return [pltpu.make_async_copy(xs_hbm.at[p, rows, :], xbuf.at[slot, p], xsem.at[slot, p]) for p in range(2)]

    def y_copies(b):
        rows, slot = pl.ds(b * bm, bm), b % ny
        return [pltpu.make_async_copy(ybuf.at[slot, p], ys_hbm.at[p, rows, :], ysem.at[slot, p]) for p in range(2)]

    def start(copies):
        for c in copies:
            c.start()

    def wait(copies):
        for c in copies:
            c.wait()

    def w_copies(e, slot):
        return (pltpu.make_async_copy(wg_hbm.at[e], wg32.at[slot], wsem.at[slot, 0]),
                pltpu.make_async_copy(wu_hbm.at[e], wu32.at[slot], wsem.at[slot, 1]),
                pltpu.make_async_copy(wd_hbm.at[e], wd32.at[slot], wsem.at[slot, 2]))

    def run_end(b):
        return lax.while_loop(lambda j: (j < n_used) & (blk_exp_ref[jnp.minimum(j, n_used - 1)] == blk_exp_ref[b]),
                              lambda j: j + 1, b + 1)

    run_ref[0] = 0
    start(x_copies(0))
    start(w_copies(blk_exp_ref[0], 0))
    for j in range(1, nx - 1):
        @pl.when(j < n_used)
        def _():
            start(x_copies(j))

    def block(b, carry):
        @pl.when(b + nx - 1 < n_used)
        def _():
            start(x_copies(b + nx - 1))

        @pl.when((b == 0) | (blk_exp_ref[b] != blk_exp_ref[jnp.maximum(b - 1, 0)]))
        def _():
            wslot = run_ref[0] % 2
            run_ref[0] = run_ref[0] + 1
            wait(w_copies(blk_exp_ref[b], wslot))
            wgu_b[:, 0:EXP_FF] = wg32[wslot].astype(BF16)
            wgu_b[:, EXP_FF:2 * EXP_FF] = wu32[wslot].astype(BF16)
            wd_b[...] = wd32[wslot].astype(BF16)
            nxt = run_end(b)

            @pl.when(nxt < n_used)
            def _():
                start(w_copies(blk_exp_ref[jnp.minimum(nxt, n_used - 1)], 1 - wslot))

        wait(x_copies(b))

        @pl.when(b >= ny)
        def _():
            wait(y_copies(b - ny))

        xslot = b % nx
        xc = _unpack_rows(xbuf[xslot, 0], xbuf[xslot, 1])
        gu = sum(_dot(c, wgu_b[i * PLANE_W:(i + 1) * PLANE_W, :]) for i, c in enumerate(xc))
        act = (_silu(gu[:, :EXP_FF]) * gu[:, EXP_FF:]).astype(BF16)
        _pack_rows(_dot(act, wd_b[...]).astype(BF16), ybuf.at[b % ny])
        start(y_copies(b))
        return carry

    lax.fori_loop(0, n_used, block, 0)

    for j in range(ny, 0, -1):
        @pl.when(n_used >= j)
        def _():
            wait(y_copies(n_used - j))


def _gmm(blk_exp, n_used, xs, w_gate, w_up, w_down):
    n_rows = xs.shape[1]
    bm = GMM_BM
    nb = n_rows // bm

    assert blk_exp.shape == (nb,)
    any_spec = pl.BlockSpec(memory_space=pl.ANY)
    grid_spec = pltpu.PrefetchScalarGridSpec(
        num_scalar_prefetch=2,
        grid=(1,),
        in_specs=[any_spec, any_spec, any_spec, any_spec],
        out_specs=any_spec,
        scratch_shapes=[pltpu.VMEM((GMM_NX, 2, bm, PLANE_W), I32), pltpu.VMEM((GMM_NY, 2, bm, PLANE_W), I32),
                        pltpu.VMEM((2, D_MODEL, EXP_FF), F32), pltpu.VMEM((2, D_MODEL, EXP_FF), F32),
                        pltpu.VMEM((2, EXP_FF, D_MODEL), F32),
                        pltpu.VMEM((D_MODEL, 2 * EXP_FF), BF16), pltpu.VMEM((EXP_FF, D_MODEL), BF16),
                        pltpu.SemaphoreType.DMA((GMM_NX, 2)), pltpu.SemaphoreType.DMA((GMM_NY, 2)),
                        pltpu.SemaphoreType.DMA((2, 3)), pltpu.SMEM((1,), I32)],
    )
    return pl.pallas_call(
        _gmm_kernel,
        grid_spec=grid_spec,
        out_shape=jax.ShapeDtypeStruct((2, n_rows, PLANE_W), I32),
        compiler_params=pltpu.CompilerParams(dimension_semantics=("arbitrary",)),
        name="gmm",
    )(blk_exp, n_used, xs, w_gate, w_up, w_down)


def _final_kernel(xmid_ref, g2_ref, z_ref, w_ref, gfin_ref, *rest):
    y_ref = rest[-1]
    accs = [jnp.zeros((xmid_ref.shape[0], PLANE_W), F32) for _ in range(4)]
    w_cols = w_ref[...].T
    for k in range(TOP_K):
        wk = w_cols[:, k:k + 1]
        cols = _unpack_rows(z_ref[k, 0], z_ref[k, 1])
        accs = [a + wk * c.astype(F32) for a, c in zip(accs, cols)]
    acc = jnp.concatenate(accs, axis=1)
    y_ref[...] = _rms(xmid_ref[...] + g2_ref[0] * acc) * gfin_ref[...]


def _final(xmid, g2, z, w_t, gfin, tile, *, n_tiles, x_tile0, z_tile0, w_tile0, tiles_per_g2, y_prev=None):
    args = [xmid, g2, z, w_t, gfin]
    in_specs = [pl.BlockSpec((tile, D_MODEL), lambda i: (x_tile0 + i, 0)),
                pl.BlockSpec((1, g2.shape[1], D_MODEL), lambda i: ((x_tile0 + i) // tiles_per_g2, 0, 0)),
                pl.BlockSpec((TOP_K, 2, tile, PLANE_W), lambda i: (0, 0, z_tile0 + i, 0)),
                pl.BlockSpec((TOP_K, tile), lambda i: (0, w_tile0 + i)),
                pl.BlockSpec((1, D_MODEL), lambda i: (0, 0))]
    aliases = {}
    if y_prev is not None:
        args.append(y_prev)
        in_specs.append(pl.BlockSpec(memory_space=pl.ANY))
        aliases = {len(args) - 1: 0}
    return pl.pallas_call(
        _final_kernel,
        grid=(n_tiles,),
        in_specs=in_specs,
        out_specs=pl.BlockSpec((tile, D_MODEL), lambda i: (x_tile0 + i, 0)),
        out_shape=jax.ShapeDtypeStruct(xmid.shape, F32),
        input_output_aliases=aliases,
        compiler_params=pltpu.CompilerParams(dimension_semantics=("arbitrary",)),
        name="final",
    )(*args)


def kernel(x_prompt, x_sample, state_hgrn, state_conv, c_prompt, c_sample, w_ada, b_ada, norm_mix_g, norm_ffn_g, w_in, lb_logits, hgrn_norm_g, conv_w, conv_b, w_out_hgrn, w_out_conv, w_o, w_router, router_bias, w_exp_gate, w_exp_up, w_exp_down, w_sh_gate, w_sh_up, w_sh_down, final_norm_g):
    assert w_ada.shape[0] == 1 and lb_logits.shape[0] == 2
    bsz, seq, _ = x_prompt.shape
    n_smp = x_sample.shape[0]
    n_prompt = bsz * seq
    n_tok = n_prompt + n_smp

    w_in_b = w_in[0].astype(BF16)
    w_oh_b = w_out_hgrn[0].astype(BF16)
    w_oc_b = w_out_conv[0].astype(BF16)
    w_o_b = w_o[0].astype(BF16)
    wr_t = w_router[0].T
    wr_hi = wr_t.astype(BF16)
    wr_hl = jnp.concatenate([wr_hi, (wr_t - wr_hi.astype(F32)).astype(BF16)], axis=0)
    w_sgu = jnp.concatenate([w_sh_gate[0], w_sh_up[0]], axis=1).astype(BF16)
    w_sd = w_sh_down[0].astype(BF16)
    gmix = norm_mix_g[0].reshape(1, D_MODEL)
    gffn = norm_ffn_g[0].reshape(1, D_MODEL)
    hg = hgrn_norm_g[0].reshape(1, KEY_W)
    cw = conv_w[0]
    cb = conv_b[0].reshape(1, CONV_W)
    gfin = final_norm_g.reshape(1, D_MODEL)

    mod_p, mod_s = _ada(c_prompt, c_sample, w_ada[0], b_ada[0])

    xmid_p, h2_all, lgt_all, s_p, cv_p = _mix(
        x_prompt, mod_p.reshape(bsz, 6, D_MODEL), n_tok, gmix, gffn, w_in_b, lb_logits, hg, cw, cb, w_oh_b, w_oc_b, w_o_b, wr_hl, w_sgu, w_sd)

    xs2 = x_sample.reshape(n_smp, D_MODEL)
    f, kk, q, v, gate, yb, sga, sgb, cv_s = _smp1(
        xs2, mod_s, gmix, w_in_b, lb_logits, cw, cb, state_conv[0].reshape(n_smp, (CONV_K - 1) * CONV_W))
    s_s, o_s = _smp2(f, kk, q, v, state_hgrn[0])
    xmid_s, h2_all, lgt_all = _smp3(xs2, mod_s, o_s, gate, yb, sga, sgb, hg, gffn,
                                    w_oh_b, w_oc_b, w_o_b, wr_hl, w_sgu, w_sd,
                                    h2_all, lgt_all, n_prompt)

    idx, w_tok, rank, cnt = _route(lgt_all, router_bias[0].reshape(N_EXP, 1))

    bm = GMM_BM
    n_blocks = (n_tok * TOP_K + N_EXP * (bm - 1)) // bm
    n_rows = n_blocks * bm
    counts = cnt[:, 0]
    padded = (counts + bm - 1) // bm * bm
    pad_end = jnp.cumsum(padded)
    pad_start = pad_end - padded
    blk_row0 = jnp.arange(n_blocks, dtype=I32) * bm
    blk_exp = jnp.minimum(jnp.sum((pad_end[None, :] <= blk_row0[:, None]).astype(I32), axis=1), N_EXP - 1)
    n_used = (pad_end[-1:] // bm).astype(I32)
    dest = _dest(pad_start.astype(I32), idx, rank, n_rows)

    xs = _dispatch(h2_all.reshape(2 * n_tok, PLANE_W), dest, 2 * n_rows).reshape(2, n_rows, PLANE_W)
    ys = _gmm(blk_exp, n_used, xs, w_exp_gate[0], w_exp_up[0], w_exp_down[0])

    ys_flat = ys.reshape(2 * n_rows, PLANE_W)
    dest3 = dest.reshape(TOP_K, 2, n_tok)
    w_t = w_tok
    xmid_p2 = xmid_p.reshape(n_prompt, D_MODEL)
    g2_p = mod_p[:, 5 * D_MODEL:].reshape(bsz, 1, D_MODEL)
    g2_s = mod_s[:, 5 * D_MODEL:].reshape(1, n_smp, D_MODEL)
    chunk = n_prompt // FINAL_CHUNKS
    y_p = None
    for c in range(FINAL_CHUNKS):
        c0 = c * chunk
        c1 = n_tok if c == FINAL_CHUNKS - 1 else c0 + chunk
        z = _combine(ys_flat, dest3[:, :, c0:c1].reshape(-1)).reshape(TOP_K, 2, c1 - c0, PLANE_W)
        y_p = _final(xmid_p2, g2_p, z, w_t, gfin, FINAL_TILE, n_tiles=chunk // FINAL_TILE,
                     x_tile0=c0 // FINAL_TILE, z_tile0=0, w_tile0=c0 // FINAL_TILE,
                     tiles_per_g2=seq // FINAL_TILE, y_prev=y_p)
    y_s = _final(xmid_s, g2_s, z, w_t, gfin, n_smp, n_tiles=1, x_tile0=0, z_tile0=chunk // n_smp,
                 w_tile0=n_prompt // n_smp, tiles_per_g2=1)

    return (y_p.reshape(bsz, seq, D_MODEL), y_s.reshape(n_smp, 1, D_MODEL),
            s_p[None], cv_p[None], s_s[None], cv_s.reshape(1, n_smp, CONV_K - 1, CONV_W))
```

```python
import functools

import jax
import jax.numpy as jnp
from jax import lax
from jax.experimental import pallas as pl
from jax.experimental.pallas import tpu as pltpu
from jax.experimental.pallas import tpu_sc as plsc

F32 = jnp.float32
BF16 = jnp.bfloat16
I32 = jnp.int32

D_MODEL = 1024
HALF_D = D_MODEL // 2
HEADS = 4
DK = 128
KEY_W = HEADS * DK
CONV_W = 512
CONV_K = 3
IN_W = 2 * KEY_W + 2 * KEY_W + 3 * CONV_W + 2 * D_MODEL
N_EXP = 64
TOP_K = 8
N_GRP = 8
GRP_SZ = N_EXP // N_GRP
TOPK_GRP = 4
EXP_FF = 256
SH_FF = 256
ROUTED_SCALE = 2.5
EPS = 1e-6

C_Q, C_F, C_I, C_G = 0, 512, 1024, 1536
C_BB, C_CC, C_VB = 2048, 2560, 3072
C_MGA, C_MGB = 3584, 4608

MIX_TILE = 512
SUB = 256
CHUNK = 64
ROUTE_TILE = 384
GMM_BM = 512
GMM_NX = 4
GMM_NY = 3
FINAL_TILE = 256
FINAL_CHUNKS = 4
SMP_GROUP = 8
SC_WINDOW = 128
PLANE_W = HALF_D // 2
VMEM_LIMIT = 56 * 1024 * 1024


def _dot(a, b):
    return jnp.dot(a, b, preferred_element_type=F32)


def _dot_nt(a, b):
    return lax.dot_general(a, b, (((1,), (1,)), ((), ())), preferred_element_type=F32)


def _dot_tn(a, b):
    return lax.dot_general(a, b, (((0,), (0,)), ((), ())), preferred_element_type=F32)


def _sigmoid(x):
    return 0.5 * jnp.tanh(0.5 * x) + 0.5


def _silu(x):
    h = 0.5 * x
    return h * jnp.tanh(h) + h


def _rms(x):
    return x * lax.rsqrt(jnp.mean(x * x, axis=-1, keepdims=True) + EPS)


def _lower_bound(lbl):
    a, b = lbl[0:1], lbl[1:2]
    m = jnp.maximum(a, b)
    ea, eb = jnp.exp(a - m), jnp.exp(b - m)
    return ea / (ea + eb)


def _split3(x):
    hi = x.astype(BF16)
    r1 = x - hi.astype(F32)
    mid = r1.astype(BF16)
    lo = (r1 - mid.astype(F32)).astype(BF16)
    return hi, mid, lo


def _words(lo_b, hi_b):
    lo = lax.shift_right_logical(lax.bitcast_convert_type(lo_b.astype(F32), I32), 16)
    hi = lax.bitcast_convert_type(hi_b.astype(F32), I32) & jnp.int32(-65536)
    return lo | hi


def _halves(w):
    lo = lax.bitcast_convert_type(lax.shift_left(w, 16), F32)
    hi = lax.bitcast_convert_type(w & jnp.int32(-65536), F32)
    return lo.astype(BF16), hi.astype(BF16)


def _pack_rows(xb, out_ref):
    words = _words(xb[:, :HALF_D], xb[:, HALF_D:])
    out_ref[0] = words[:, :PLANE_W]
    out_ref[1] = words[:, PLANE_W:]


def _unpack_rows(p0, p1):
    c0, c2 = _halves(p0)
    c1, c3 = _halves(p1)
    return c0, c1, c2, c3


def _ada_kernel(cp_ref, cs_ref, w_ref, b_ref, op_ref, os_ref):
    w = w_ref[...].astype(BF16)
    for c_ref, o_ref in ((cp_ref, op_ref), (cs_ref, os_ref)):
        o_ref[...] = _dot(_silu(c_ref[...]).astype(BF16), w) + b_ref[...]


def _ada(c_prompt, c_sample, w_ada, b_ada):
    blk = 1024
    rows = lambda c: pl.BlockSpec((c.shape[0], D_MODEL), lambda j: (0, 0))
    cols = lambda c: pl.BlockSpec((c.shape[0], blk), lambda j: (0, j))
    return pl.pallas_call(
        _ada_kernel,
        grid=(6 * D_MODEL // blk,),
        in_specs=[rows(c_prompt), rows(c_sample),
                  pl.BlockSpec((D_MODEL, blk), lambda j: (0, j)),
                  pl.BlockSpec((1, blk), lambda j: (0, j))],
        out_specs=[cols(c_prompt), cols(c_sample)],
        out_shape=[jax.ShapeDtypeStruct((c.shape[0], 6 * D_MODEL), F32) for c in (c_prompt, c_sample)],
        name="ada",
    )(c_prompt, c_sample, w_ada, b_ada.reshape(1, -1))


def _ffn_pre(x1, mod_rows, gffn, w_sgu, w_sd, wr_hl):
    sh2, sc2, g2 = mod_rows
    h2 = _rms(x1) * gffn * (1.0 + sc2) + sh2
    h2b = h2.astype(BF16)
    gu = _dot(h2b, w_sgu)
    act = _silu(gu[:, :SH_FF]) * gu[:, SH_FF:]
    xmid = x1 + g2 * _dot(act.astype(BF16), w_sd)
    h2lo = (h2 - h2b.astype(F32)).astype(BF16)
    both = _dot_nt(wr_hl, h2b)
    lgt = both[:N_EXP] + both[N_EXP:] + _dot_nt(wr_hl[:N_EXP], h2lo)
    return xmid, h2b, lgt


def _mix_kernel(*refs, nt, n_tiles):
    i = pl.program_id(0)
    h2_ref, lgt_ref = refs[16], refs[17]

    @pl.when(i == n_tiles)
    def _():
        h2_ref[...] = jnp.zeros_like(h2_ref)
        lgt_ref[...] = jnp.zeros_like(lgt_ref)

    @pl.when(i < n_tiles)
    def _():
        _mix_tile(i % nt, nt, *refs)


def _mix_tile(t, nt, x_ref, mod_ref, gmix_ref, gffn_ref, w_in_ref, lbl_ref, hg_ref, cw_ref, cb_ref,
              w_oh_ref, w_oc_ref, w_o_ref, wr_hl_ref, w_sgu_ref, w_sd_ref,
              xmid_ref, h2_ref, lgt_ref, s_out_ref, cv_out_ref,
              proj_ref, st_ref, cbuf_ref, ya_ref):
    tt = x_ref.shape[1]

    @pl.when(t == 0)
    def _():
        st_ref[...] = jnp.zeros_like(st_ref)
        cbuf_ref[...] = jnp.zeros_like(cbuf_ref)

    x = x_ref[0]
    mod = mod_ref[0]
    sh1, sc1, g1 = mod[0:1], mod[1:2], mod[2:3]
    h = _rms(x) * gmix_ref[...] * (1.0 + sc1) + sh1
    hb = h.astype(BF16)
    for c in range(0, IN_W, 512):
        proj_ref[:, c:c + 512] = _dot(hb, w_in_ref[:, c:c + 512])

    lb = _lower_bound(lbl_ref[...])
    row = lax.broadcasted_iota(I32, (SUB, SUB), 0)
    col = lax.broadcasted_iota(I32, (SUB, SUB), 1)
    tri = (col <= row).astype(BF16)
    mask_d = (col <= row) & (row // CHUNK == col // CHUNK)
    mask_a = row // (2 * CHUNK) == col // (2 * CHUNK)
    n_ch = SUB // CHUNK

    def by_chunk(vals):
        return jnp.concatenate([jnp.zeros((CHUNK, DK), F32) if v is None
                                else jnp.broadcast_to(v, (CHUNK, DK)) for v in vals], axis=0)

    for s in range(tt // SUB):
        r0 = s * SUB
        f = lb + (1.0 - lb) * _sigmoid(proj_ref[r0:r0 + SUB, C_F:C_F + KEY_W])
        kk = 1.0 - f
        hi, mid, lo = _split3(jnp.log(f))
        bc = _dot(tri, hi) + _dot(tri, mid) + _dot(tri, lo)
        for hd in range(HEADS):
            hs = slice(hd * DK, (hd + 1) * DK)
            bh = bc[:, hs]
            at = lambda r: bh[r:r + 1]
            mids = [at(c * CHUNK + CHUNK // 2 - 1) for c in range(n_ch)]
            pair_mid = [at(CHUNK - 1), at(3 * CHUNK - 1)]
            step_mid, step_end = at(2 * CHUNK - 1), at(SUB - 1)
            arg = bh - by_chunk(mids)
            e_pos, e_neg = jnp.exp(arg), jnp.exp(-arg)
            q = _silu(proj_ref[r0:r0 + SUB, C_Q + hd * DK:C_Q + (hd + 1) * DK])
            v = proj_ref[r0:r0 + SUB, C_I + hd * DK:C_I + (hd + 1) * DK]
            qd = q * e_pos
            kd = kk[:, hs] * e_neg
            q_in = qd * by_chunk([jnp.exp(m) for m in mids])
            k_end = kd * by_chunk([jnp.exp(step_end - m) for m in mids])
            qa = qd * by_chunk([None, jnp.exp(mids[1] - pair_mid[0]), None, jnp.exp(mids[3] - pair_mid[1])])
            ka = kd * by_chunk([jnp.exp(pair_mid[0] - mids[0]), None, jnp.exp(pair_mid[1] - mids[2]), None])
            qb = qd * by_chunk([None, None, jnp.exp(mids[2] - step_mid), jnp.exp(mids[3] - step_mid)])
            kb = kd * by_chunk([jnp.exp(step_mid - mids[0]), jnp.exp(step_mid - mids[1]), None, None])
            att = jnp.where(mask_d, _dot_nt(qd.astype(BF16), kd.astype(BF16)), 0.0)
            att = att + jnp.where(mask_a, _dot_nt(qa.astype(BF16), ka.astype(BF16)), 0.0)
            att = att + _dot_nt(qb.astype(BF16), kb.astype(BF16))
            vb = v.astype(BF16)
            st = st_ref[hd]
            o = _dot(att.astype(BF16), vb) + _dot_nt(q_in.astype(BF16), st.astype(BF16))
            st_ref[hd] = st * jnp.exp(step_end) + _dot_tn(vb, k_end.astype(BF16))
            gate = _silu(proj_ref[r0:r0 + SUB, C_G + hd * DK:C_G + (hd + 1) * DK])
            ya_ref[r0:r0 + SUB, hs] = _rms(o) * hg_ref[:, hs] * gate

    u = proj_ref[:, C_CC:C_CC + CONV_W] * proj_ref[:, C_VB:C_VB + CONV_W]
    rows = lax.broadcasted_iota(I32, (tt, CONV_W), 0)
    c0, c1 = cbuf_ref[0:1], cbuf_ref[1:2]
    u1 = jnp.where(rows == 0, c1, pltpu.roll(u, 1, axis=0))
    u2 = jnp.where(rows == 0, c0, jnp.where(rows == 1, c1, pltpu.roll(u, 2, axis=0)))
    conv = cw_ref[0:1] * u2 + cw_ref[1:2] * u1 + cw_ref[2:3] * u + cb_ref[...]
    yb = proj_ref[:, C_BB:C_BB + CONV_W] * conv
    cbuf_ref[...] = u[tt - 2:tt]

    mixed = (_sigmoid(proj_ref[:, C_MGA:C_MGA + D_MODEL]) * _dot(ya_ref[...].astype(BF16), w_oh_ref[...])
             + _sigmoid(proj_ref[:, C_MGB:C_MGB + D_MODEL]) * _dot(yb.astype(BF16), w_oc_ref[...]))
    x1 = x + g1 * _dot(mixed.astype(BF16), w_o_ref[...])

    xmid, h2b, lgt = _ffn_pre(x1, (mod[3:4], mod[4:5], mod[5:6]), gffn_ref[...],
                              w_sgu_ref[...], w_sd_ref[...], wr_hl_ref[...])
    xmid_ref[0] = xmid
    _pack_rows(h2b, h2_ref)
    lgt_ref[...] = lgt

    @pl.when(t == nt - 1)
    def _():
        for hd in range(HEADS):
            s_out_ref[0, hd] = st_ref[hd].T
        cv_out_ref[0] = cbuf_ref[...]


def _const_spec(shape):
    nd = len(shape)
    return pl.BlockSpec(shape, lambda i, _nd=nd: (0,) * _nd, pipeline_mode=pl.Buffered(1))


def _mix(x, mod, n_tok, gmix, gffn, w_in, lbl, hg, cw, cb, w_oh, w_oc, w_o, wr_hl, w_sgu, w_sd):
    bsz, seq, _ = x.shape
    tt = MIX_TILE
    nt = seq // tt
    n_tiles = bsz * nt
    assert n_tiles * tt < n_tok <= (n_tiles + 1) * tt
    consts = [gmix, gffn, w_in, lbl, hg, cw, cb, w_oh, w_oc, w_o, wr_hl, w_sgu, w_sd]
    tile = lambda i: jnp.minimum(i, n_tiles - 1)
    return pl.pallas_call(
        functools.partial(_mix_kernel, nt=nt, n_tiles=n_tiles),
        grid=(n_tiles + 1,),
        in_specs=[pl.BlockSpec((1, tt, D_MODEL), lambda i: (tile(i) // nt, tile(i) % nt, 0)),
                  pl.BlockSpec((1, 6, D_MODEL), lambda i: (tile(i) // nt, 0, 0))]
                 + [_const_spec(a.shape) for a in consts],
        out_specs=[pl.BlockSpec((1, tt, D_MODEL), lambda i: (tile(i) // nt, tile(i) % nt, 0)),
                   pl.BlockSpec((2, tt, PLANE_W), lambda i: (0, i, 0)),
                   pl.BlockSpec((N_EXP, tt), lambda i: (0, i)),
                   pl.BlockSpec((1, HEADS, DK, DK), lambda i: (tile(i) // nt, 0, 0, 0)),
                   pl.BlockSpec((1, CONV_K - 1, CONV_W), lambda i: (tile(i) // nt, 0, 0))],
        out_shape=[jax.ShapeDtypeStruct((bsz, seq, D_MODEL), F32),
                   jax.ShapeDtypeStruct((2, n_tok, PLANE_W), I32),
                   jax.ShapeDtypeStruct((N_EXP, n_tok), F32),
                   jax.ShapeDtypeStruct((bsz, HEADS, DK, DK), F32),
                   jax.ShapeDtypeStruct((bsz, CONV_K - 1, CONV_W), F32)],
        scratch_shapes=[pltpu.VMEM((tt, IN_W), F32),
                        pltpu.VMEM((HEADS, DK, DK), F32),
                        pltpu.VMEM((CONV_K - 1, CONV_W), F32),
                        pltpu.VMEM((tt, KEY_W), F32)],
        compiler_params=pltpu.CompilerParams(
            dimension_semantics=("arbitrary",), vmem_limit_bytes=VMEM_LIMIT),
        name="mix",
    )(x, mod, *consts)


def _smp1_kernel(x_ref, mod_ref, gmix_ref, w_in_ref, lbl_ref, cw_ref, cb_ref, cst_ref,
                 f_ref, k_ref, q_ref, v_ref, gate_ref, yb_ref, sga_ref, sgb_ref, cv_out_ref):
    x = x_ref[...]
    sh1, sc1 = mod_ref[:, 0:D_MODEL], mod_ref[:, D_MODEL:2 * D_MODEL]
    h = _rms(x) * gmix_ref[...] * (1.0 + sc1) + sh1
    hb = h.astype(BF16)

    def proj(c, w):
        return _dot(hb, w_in_ref[:, c:c + w])

    lb = _lower_bound(lbl_ref[...])
    f = lb + (1.0 - lb) * _sigmoid(proj(C_F, KEY_W))
    f_ref[...] = f
    k_ref[...] = 1.0 - f
    q_ref[...] = _silu(proj(C_Q, KEY_W))
    v_ref[...] = proj(C_I, KEY_W)
    gate_ref[...] = _silu(proj(C_G, KEY_W))
    u = proj(C_CC, CONV_W) * proj(C_VB, CONV_W)
    c0, c1 = cst_ref[:, 0:CONV_W], cst_ref[:, CONV_W:2 * CONV_W]
    conv = cw_ref[0:1] * c0 + cw_ref[1:2] * c1 + cw_ref[2:3] * u + cb_ref[...]
    yb_ref[...] = proj(C_BB, CONV_W) * conv
    cv_out_ref[:, 0:CONV_W] = c1
    cv_out_ref[:, CONV_W:2 * CONV_W] = u
    sga_ref[...] = _sigmoid(proj(C_MGA, D_MODEL))
    sgb_ref[...] = _sigmoid(proj(C_MGB, D_MODEL))


def _smp1(x, mod, gmix, w_in, lbl, cw, cb, cst):
    n = x.shape[0]
    kw = jax.ShapeDtypeStruct((n, KEY_W), F32)
    dm = jax.ShapeDtypeStruct((n, D_MODEL), F32)
    return pl.pallas_call(
        _smp1_kernel,
        out_shape=[kw, kw, kw, kw, kw, kw, dm, dm,
                   jax.ShapeDtypeStruct((n, (CONV_K - 1) * CONV_W), F32)],
        compiler_params=pltpu.CompilerParams(vmem_limit_bytes=VMEM_LIMIT),
        name="smp1",
    )(x, mod, gmix, w_in, lbl, cw, cb, cst)


def _smp2_kernel(f_ref, k_ref, q_ref, v_ref, s_ref, s_out_ref, o_ref):
    g = f_ref.shape[0]
    for i in range(g):
        for hd in range(HEADS):
            hs = slice(hd * DK, (hd + 1) * DK)

            def col(ref):
                return jnp.broadcast_to(ref[i:i + 1, hs], (DK, DK)).T

            vrow = v_ref[i:i + 1, hs]
            s_new = col(f_ref) * s_ref[i, hd] + col(k_ref) * vrow
            s_out_ref[i, hd] = s_new
            o_ref[i:i + 1, hs] = jnp.sum(col(q_ref) * s_new, axis=0, keepdims=True)


def _smp2(f, k, q, v, state):
    n = f.shape[0]
    g = SMP_GROUP
    row_spec = pl.BlockSpec((g, KEY_W), lambda i: (i, 0))
    st_spec = pl.BlockSpec((g, HEADS, DK, DK), lambda i: (i, 0, 0, 0))
    return pl.pallas_call(
        _smp2_kernel,
        grid=(n // g,),
        in_specs=[row_spec, row_spec, row_spec, row_spec, st_spec],
        out_specs=[st_spec, row_spec],
        out_shape=[jax.ShapeDtypeStruct(state.shape, F32), jax.ShapeDtypeStruct((n, KEY_W), F32)],
        compiler_params=pltpu.CompilerParams(dimension_semantics=("arbitrary",)),
        name="smp2",
    )(f, k, q, v, state)


def _smp3_kernel(x_ref, mod_ref, o_ref, gate_ref, yb_ref, sga_ref, sgb_ref, hg_ref, gffn_ref,
                 w_oh_ref, w_oc_ref, w_o_ref, wr_hl_ref, w_sgu_ref, w_sd_ref,
                 h2_all_ref, lgt_all_ref, xmid_ref, h2_ref, lgt_ref):
    del h2_all_ref, lgt_all_ref
    parts = []
    for hd in range(HEADS):
        hs = slice(hd * DK, (hd + 1) * DK)
        parts.append(_rms(o_ref[:, hs]) * hg_ref[:, hs] * gate_ref[:, hs])
    ya = jnp.concatenate(parts, axis=1)
    mixed = (sga_ref[...] * _dot(ya.astype(BF16), w_oh_ref[...])
             + sgb_ref[...] * _dot(yb_ref[...].astype(BF16), w_oc_ref[...]))
    g1 = mod_ref[:, 2 * D_MODEL:3 * D_MODEL]
    x1 = x_ref[...] + g1 * _dot(mixed.astype(BF16), w_o_ref[...])
    mod_rows = tuple(mod_ref[:, j * D_MODEL:(j + 1) * D_MODEL] for j in (3, 4, 5))
    xmid, h2b, lgt = _ffn_pre(x1, mod_rows, gffn_ref[...], w_sgu_ref[...], w_sd_ref[...],
                              wr_hl_ref[...])
    xmid_ref[...] = xmid
    _pack_rows(h2b, h2_ref)
    lgt_ref[...] = lgt


def _smp3(x, mod, o, gate, yb, sga, sgb, hg, gffn, w_oh, w_oc, w_o, wr_hl, w_sgu, w_sd,
          h2_all, lgt_all, n_prompt):
    n = x.shape[0]
    vmem_args = [x, mod, o, gate, yb, sga, sgb, hg, gffn, w_oh, w_oc, w_o, wr_hl, w_sgu, w_sd]
    blk = n_prompt // n

    def full(a):
        nd = a.ndim
        return pl.BlockSpec(a.shape, lambda i, _nd=nd: (0,) * _nd)

    return pl.pallas_call(
        _smp3_kernel,
        grid=(1,),
        in_specs=[full(a) for a in vmem_args]
                 + [pl.BlockSpec(memory_space=pl.ANY), pl.BlockSpec(memory_space=pl.ANY)],
        out_specs=[pl.BlockSpec((n, D_MODEL), lambda i: (0, 0)),
                   pl.BlockSpec((2, n, PLANE_W), lambda i: (0, blk, 0)),
                   pl.BlockSpec((N_EXP, n), lambda i: (0, blk))],
        out_shape=[jax.ShapeDtypeStruct((n, D_MODEL), F32),
                   jax.ShapeDtypeStruct(h2_all.shape, h2_all.dtype),
                   jax.ShapeDtypeStruct(lgt_all.shape, lgt_all.dtype)],
        input_output_aliases={len(vmem_args): 1, len(vmem_args) + 1: 2},
        compiler_params=pltpu.CompilerParams(
            dimension_semantics=("arbitrary",), vmem_limit_bytes=VMEM_LIMIT),
        name="smp3",
    )(*vmem_args, h2_all, lgt_all)


def _route_kernel(lgt_ref, bias_ref, idx_ref, w_ref, rank_ref, cnt_ref):
    tr = ROUTE_TILE
    n_tiles = lgt_ref.shape[1] // tr

    def tile(i, carry):
        cols = pl.ds(pl.multiple_of(i * tr, tr), tr)
        picks, weights, ranks, carry = _route_tile(lgt_ref[:, cols], bias_ref[...], carry)
        for k in range(TOP_K):
            idx_ref[k:k + 1, cols] = picks[k]
            w_ref[k:k + 1, cols] = weights[k]
            rank_ref[k:k + 1, cols] = ranks[k]
        return carry

    total = lax.fori_loop(0, n_tiles, tile, jnp.zeros((N_EXP, 1), F32))
    cnt_ref[...] = jnp.broadcast_to(total, cnt_ref.shape).astype(I32)


def _route_tile(lgt, bias, carry):
    tr = lgt.shape[1]
    neg = -jnp.inf
    scores = _sigmoid(lgt)
    sel = scores + bias
    j8 = lax.broadcasted_iota(I32, (GRP_SZ, tr), 0)
    groups = [sel[g * GRP_SZ:(g + 1) * GRP_SZ] for g in range(N_GRP)]
    gscore = []
    for grp in groups:
        m1 = jnp.max(grp, axis=0, keepdims=True)
        i1 = jnp.min(jnp.where(grp == m1, j8, GRP_SZ), axis=0, keepdims=True)
        m2 = jnp.max(jnp.where(j8 == i1, neg, grp), axis=0, keepdims=True)
        gscore.append(m1 + m2)
    kept = []
    for g in range(N_GRP):
        beaten = jnp.zeros((1, tr), I32)
        for o in range(N_GRP):
            if o < g:
                beaten = beaten + (gscore[o] >= gscore[g]).astype(I32)
            elif o > g:
                beaten = beaten + (gscore[o] > gscore[g]).astype(I32)
        kept.append(jnp.where(beaten < TOPK_GRP, groups[g], neg))
    masked = jnp.concatenate(kept, axis=0)
    ei = lax.broadcasted_iota(I32, masked.shape, 0)
    chosen = jnp.zeros(masked.shape, jnp.bool_)
    picks, weights = [], []
    for _ in range(TOP_K):
        m = jnp.max(masked, axis=0, keepdims=True)
        pick = jnp.min(jnp.where(masked == m, ei, N_EXP), axis=0, keepdims=True)
        hit = ei == pick
        weights.append(jnp.sum(jnp.where(hit, scores, 0.0), axis=0, keepdims=True))
        picks.append(pick)
        chosen = chosen | hit
        masked = jnp.where(hit, neg, masked)
    wsum = weights[0]
    for w in weights[1:]:
        wsum = wsum + w
    sel01 = chosen.astype(F32)
    r = lax.broadcasted_iota(I32, (tr, tr), 0)
    c = lax.broadcasted_iota(I32, (tr, tr), 1)
    before = (r < c).astype(BF16)
    cnt = _dot(sel01.astype(BF16), before) + carry
    weights = [w / wsum * ROUTED_SCALE for w in weights]
    ranks = [jnp.sum(jnp.where(ei == p, cnt, 0.0), axis=0, keepdims=True).astype(I32) for p in picks]
    return picks, weights, ranks, carry + jnp.sum(sel01, axis=1, keepdims=True)


def _route(lgt, bias):
    n = lgt.shape[1]
    assert n % ROUTE_TILE == 0
    slot = lambda dt: jax.ShapeDtypeStruct((TOP_K, n), dt)
    return pl.pallas_call(
        _route_kernel,
        out_shape=[slot(I32), slot(F32), slot(I32),
                   jax.ShapeDtypeStruct((N_EXP, 128), I32)],
        name="route",
    )(lgt, bias)


def _dest_kernel(start_ref, idx_ref, rank_ref, dest_ref, *, n_rows):
    n_tok = idx_ref.shape[1]
    idx = idx_ref[...]
    acc = rank_ref[...]
    for e in range(N_EXP):
        acc = acc + jnp.where(idx == e, start_ref[e], 0)
    dest_ref[:, 0:n_tok] = acc
    dest_ref[:, n_tok:2 * n_tok] = acc + n_rows


def _dest(pad_start, idx, rank, n_rows):
    k, n_tok = idx.shape
    return pl.pallas_call(
        functools.partial(_dest_kernel, n_rows=n_rows),
        in_specs=[pl.BlockSpec(memory_space=pltpu.SMEM),
                  pl.BlockSpec(memory_space=pltpu.VMEM),
                  pl.BlockSpec(memory_space=pltpu.VMEM)],
        out_specs=pl.BlockSpec(memory_space=pltpu.VMEM),
        out_shape=jax.ShapeDtypeStruct((k, 2 * n_tok), I32),
        name="dest",
    )(pad_start, idx, rank)


def _sc_mesh():
    return plsc.VectorSubcoreMesh(core_axis_name="core", subcore_axis_name="subcore")


def _dispatch(rows, dest, n_out):
    n, width = rows.shape
    win = SC_WINDOW
    steps = n // win

    @pl.kernel(out_type=jax.ShapeDtypeStruct((n_out, width), rows.dtype), mesh=_sc_mesh(),
               scratch_types=[], name="dispatch")
    def run(x_hbm, *refs):
        i_hbms, o_hbm = refs[:TOP_K], refs[TOP_K]

        def body(x_vmem, *i_vmems):
            for i_vmem in i_vmems:
                pltpu.sync_copy(x_vmem, o_hbm.at[i_vmem.at[0]])

        pltpu.emit_pipeline(
            body,
            grid=(steps,),
            in_specs=[pl.BlockSpec((win, width), lambda i: (i, 0))]
                     + [pl.BlockSpec((1, win), lambda i, k=k: (0, k * steps + i)) for k in range(TOP_K)],
            out_specs=[],
            core_axis_name=("core", "subcore"),
            dimension_semantics=(pltpu.PARALLEL,),
        )(x_hbm, *i_hbms)

    dest_flat = dest.reshape(1, TOP_K * n)
    return run(rows, *([dest_flat] * TOP_K))


def _combine(rows, dest_flat):
    width = rows.shape[1]
    n = dest_flat.shape[0]
    win = SC_WINDOW

    @pl.kernel(out_type=jax.ShapeDtypeStruct((n, width), rows.dtype), mesh=_sc_mesh(),
               scratch_types=[], name="combine")
    def run(y_hbm, i_hbm, o_hbm):
        def body(i_vmem, o_vmem):
            pltpu.sync_copy(y_hbm.at[i_vmem.at[0]], o_vmem)

        pltpu.emit_pipeline(
            body,
            grid=(n // win,),
            in_specs=[pl.BlockSpec((1, win), lambda i: (0, i))],
            out_specs=[pl.BlockSpec((win, width), lambda i: (i, 0))],
            core_axis_name=("core", "subcore"),
            dimension_semantics=(pltpu.PARALLEL,),
        )(i_hbm, o_hbm)

    return run(rows, dest_flat.reshape(1, n))


def _gmm_kernel(blk_exp_ref, n_used_ref, xs_hbm, wg_hbm, wu_hbm, wd_hbm, ys_hbm,
                xbuf, ybuf, wg32, wu32, wd32, wgu_b, wd_b, xsem, ysem, wsem, run_ref):
    nx, ny, bm = xbuf.shape[0], ybuf.shape[0], xbuf.shape[2]
    n_used = n_used_ref[0]

    def x_copies(b):
        rows, slot = pl.ds(b * bm, bm), b % nx
        return [pltpu.make_async_copy(xs_hbm.at[p, rows, :], xbuf.at[slot, p], xsem.at[slot, p]) for p in range(2)]

    def y_copies(b):
        rows, slot = pl.ds(b * bm, bm), b % ny
        return [pltpu.make_async_copy(ybuf.at[slot, p], ys_hbm.at[p, rows, :], ysem.at[slot, p]) for p in range(2)]

    def start(copies):
        for c in copies:
            c.start()

    def wait(copies):
        for c in copies:
            c.wait()

    def w_copies(e, slot):
        return (pltpu.make_async_copy(wg_hbm.at[e], wg32.at[slot], wsem.at[slot, 0]),
                pltpu.make_async_copy(wu_hbm.at[e], wu32.at[slot], wsem.at[slot, 1]),
                pltpu.make_async_copy(wd_hbm.at[e], wd32.at[slot], wsem.at[slot, 2]))

    def run_end(b):
        return lax.while_loop(lambda j: (j < n_used) & (blk_exp_ref[jnp.minimum(j, n_used - 1)] == blk_exp_ref[b]),
                              lambda j: j + 1, b + 1)

    run_ref[0] = 0
    start(x_copies(0))
    start(w_copies(blk_exp_ref[0], 0))
    for j in range(1, nx - 1):
        @pl.when(j < n_used)
        def _():
            start(x_copies(j))

    def prefetch(j):
        @pl.when(j < n_used)
        def _():
            start(x_copies(j))

    def free_out_slot(b):
        @pl.when(b >= ny)
        def _():
            wait(y_copies(b - ny))

    def compute(b):
        xslot = b % nx
        xc = _unpack_rows(xbuf[xslot, 0], xbuf[xslot, 1])
        gu = sum(_dot(c, wgu_b[i * PLANE_W:(i + 1) * PLANE_W, :]) for i, c in enumerate(xc))
        act = (_silu(gu[:, :EXP_FF]) * gu[:, EXP_FF:]).astype(BF16)
        _pack_rows(_dot(act, wd_b[...]).astype(BF16), ybuf.at[b % ny])

    def step(b):
        prefetch(b + nx - 1)

        @pl.when((b == 0) | (blk_exp_ref[b] != blk_exp_ref[jnp.maximum(b - 1, 0)]))
        def _():
            wslot = run_ref[0] % 2
            run_ref[0] = run_ref[0] + 1
            wait(w_copies(blk_exp_ref[b], wslot))
            wgu_b[:, 0:EXP_FF] = wg32[wslot].astype(BF16)
            wgu_b[:, EXP_FF:2 * EXP_FF] = wu32[wslot].astype(BF16)
            wd_b[...] = wd32[wslot].astype(BF16)
            nxt = run_end(b)

            @pl.when(nxt < n_used)
            def _():
                start(w_copies(blk_exp_ref[jnp.minimum(nxt, n_used - 1)], 1 - wslot))

        nxt_b = jnp.minimum(b + 1, n_used - 1)
        pair = (b + 1 < n_used) & (blk_exp_ref[nxt_b] == blk_exp_ref[b])
        wait(x_copies(b))
        free_out_slot(b)

        def two():
            wait(x_copies(b + 1))
            free_out_slot(b + 1)
            compute(b)
            compute(b + 1)
            start(y_copies(b))
            start(y_copies(b + 1))
            prefetch(b + nx)
            return b + 2

        def one():
            compute(b)
            start(y_copies(b))
            return b + 1

        return lax.cond(pair, two, one)

    lax.while_loop(lambda b: b < n_used, step, jnp.int32(0))

    for j in range(ny, 0, -1):
        @pl.when(n_used >= j)
        def _():
            wait(y_copies(n_used - j))


def _gmm(blk_exp, n_used, xs, w_gate, w_up, w_down):
    n_rows = xs.shape[1]
    bm = GMM_BM
    nb = n_rows // bm

    assert blk_exp.shape == (nb,)
    any_spec = pl.BlockSpec(memory_space=pl.ANY)
    grid_spec = pltpu.PrefetchScalarGridSpec(
        num_scalar_prefetch=2,
        grid=(1,),
        in_specs=[any_spec, any_spec, any_spec, any_spec],
        out_specs=any_spec,
        scratch_shapes=[pltpu.VMEM((GMM_NX, 2, bm, PLANE_W), I32), pltpu.VMEM((GMM_NY, 2, bm, PLANE_W), I32),
                        pltpu.VMEM((2, D_MODEL, EXP_FF), F32), pltpu.VMEM((2, D_MODEL, EXP_FF), F32),
                        pltpu.VMEM((2, EXP_FF, D_MODEL), F32),
                        pltpu.VMEM((D_MODEL, 2 * EXP_FF), BF16), pltpu.VMEM((EXP_FF, D_MODEL), BF16),
                        pltpu.SemaphoreType.DMA((GMM_NX, 2)), pltpu.SemaphoreType.DMA((GMM_NY, 2)),
                        pltpu.SemaphoreType.DMA((2, 3)), pltpu.SMEM((1,), I32)],
    )
    return pl.pallas_call(
        _gmm_kernel,
        grid_spec=grid_spec,
        out_shape=jax.ShapeDtypeStruct((2, n_rows, PLANE_W), I32),
        compiler_params=pltpu.CompilerParams(dimension_semantics=("arbitrary",)),
        name="gmm",
    )(blk_exp, n_used, xs, w_gate, w_up, w_down)


def _final_kernel(xmid_ref, g2_ref, z_ref, w_ref, gfin_ref, *rest):
    y_ref = rest[-1]
    accs = [jnp.zeros((xmid_ref.shape[0], PLANE_W), F32) for _ in range(4)]
    w_cols = w_ref[...].T
    for k in range(TOP_K):
        wk = w_cols[:, k:k + 1]
        cols = _unpack_rows(z_ref[k, 0], z_ref[k, 1])
        accs = [a + wk * c.astype(F32) for a, c in zip(accs, cols)]
    acc = jnp.concatenate(accs, axis=1)
    y_ref[...] = _rms(xmid_ref[...] + g2_ref[0] * acc) * gfin_ref[...]


def _final(xmid, g2, z, w_t, gfin, tile, *, n_tiles, x_tile0, z_tile0, w_tile0, tiles_per_g2, y_prev=None):
    args = [xmid, g2, z, w_t, gfin]
    in_specs = [pl.BlockSpec((tile, D_MODEL), lambda i: (x_tile0 + i, 0)),
                pl.BlockSpec((1, g2.shape[1], D_MODEL), lambda i: ((x_tile0 + i) // tiles_per_g2, 0, 0)),
                pl.BlockSpec((TOP_K, 2, tile, PLANE_W), lambda i: (0, 0, z_tile0 + i, 0)),
                pl.BlockSpec((TOP_K, tile), lambda i: (0, w_tile0 + i)),
                pl.BlockSpec((1, D_MODEL), lambda i: (0, 0))]
    aliases = {}
    if y_prev is not None:
        args.append(y_prev)
        in_specs.append(pl.BlockSpec(memory_space=pl.ANY))
        aliases = {len(args) - 1: 0}
    return pl.pallas_call(
        _final_kernel,
        grid=(n_tiles,),
        in_specs=in_specs,
        out_specs=pl.BlockSpec((tile, D_MODEL), lambda i: (x_tile0 + i, 0)),
        out_shape=jax.ShapeDtypeStruct(xmid.shape, F32),
        input_output_aliases=aliases,
        compiler_params=pltpu.CompilerParams(dimension_semantics=("arbitrary",)),
        name="final",
    )(*args)


def kernel(x_prompt, x_sample, state_hgrn, state_conv, c_prompt, c_sample, w_ada, b_ada, norm_mix_g, norm_ffn_g, w_in, lb_logits, hgrn_norm_g, conv_w, conv_b, w_out_hgrn, w_out_conv, w_o, w_router, router_bias, w_exp_gate, w_exp_up, w_exp_down, w_sh_gate, w_sh_up, w_sh_down, final_norm_g):
    assert w_ada.shape[0] == 1 and lb_logits.shape[0] == 2
    bsz, seq, _ = x_prompt.shape
    n_smp = x_sample.shape[0]
    n_prompt = bsz * seq
    n_tok = n_prompt + n_smp

    w_in_b = w_in[0].astype(BF16)
    w_oh_b = w_out_hgrn[0].astype(BF16)
    w_oc_b = w_out_conv[0].astype(BF16)
    w_o_b = w_o[0].astype(BF16)
    wr_t = w_router[0].T
    wr_hi = wr_t.astype(BF16)
    wr_hl = jnp.concatenate([wr_hi, (wr_t - wr_hi.astype(F32)).astype(BF16)], axis=0)
    w_sgu = jnp.concatenate([w_sh_gate[0], w_sh_up[0]], axis=1).astype(BF16)
    w_sd = w_sh_down[0].astype(BF16)
    gmix = norm_mix_g[0].reshape(1, D_MODEL)
    gffn = norm_ffn_g[0].reshape(1, D_MODEL)
    hg = hgrn_norm_g[0].reshape(1, KEY_W)
    cw = conv_w[0]
    cb = conv_b[0].reshape(1, CONV_W)
    gfin = final_norm_g.reshape(1, D_MODEL)

    mod_p, mod_s = _ada(c_prompt, c_sample, w_ada[0], b_ada[0])

    xmid_p, h2_all, lgt_all, s_p, cv_p = _mix(
        x_prompt, mod_p.reshape(bsz, 6, D_MODEL), n_tok, gmix, gffn, w_in_b, lb_logits, hg, cw, cb, w_oh_b, w_oc_b, w_o_b, wr_hl, w_sgu, w_sd)

    xs2 = x_sample.reshape(n_smp, D_MODEL)
    f, kk, q, v, gate, yb, sga, sgb, cv_s = _smp1(
        xs2, mod_s, gmix, w_in_b, lb_logits, cw, cb, state_conv[0].reshape(n_smp, (CONV_K - 1) * CONV_W))
    s_s, o_s = _smp2(f, kk, q, v, state_hgrn[0])
    xmid_s, h2_all, lgt_all = _smp3(xs2, mod_s, o_s, gate, yb, sga, sgb, hg, gffn,
                                    w_oh_b, w_oc_b, w_o_b, wr_hl, w_sgu, w_sd,
                                    h2_all, lgt_all, n_prompt)

    idx, w_tok, rank, cnt = _route(lgt_all, router_bias[0].reshape(N_EXP, 1))

    bm = GMM_BM
    n_blocks = (n_tok * TOP_K + N_EXP * (bm - 1)) // bm
    n_rows = n_blocks * bm
    counts = cnt[:, 0]
    padded = (counts + bm - 1) // bm * bm
    pad_end = jnp.cumsum(padded)
    pad_start = pad_end - padded
    blk_row0 = jnp.arange(n_blocks, dtype=I32) * bm
    blk_exp = jnp.minimum(jnp.sum((pad_end[None, :] <= blk_row0[:, None]).astype(I32), axis=1), N_EXP - 1)
    n_used = (pad_end[-1:] // bm).astype(I32)
    dest = _dest(pad_start.astype(I32), idx, rank, n_rows)

    xs = _dispatch(h2_all.reshape(2 * n_tok, PLANE_W), dest, 2 * n_rows).reshape(2, n_rows, PLANE_W)
    ys = _gmm(blk_exp, n_used, xs, w_exp_gate[0], w_exp_up[0], w_exp_down[0])

    ys_flat = ys.reshape(2 * n_rows, PLANE_W)
    dest3 = dest.reshape(TOP_K, 2, n_tok)
    w_t = w_tok
    xmid_p2 = xmid_p.reshape(n_prompt, D_MODEL)
    g2_p = mod_p[:, 5 * D_MODEL:].reshape(bsz, 1, D_MODEL)
    g2_s = mod_s[:, 5 * D_MODEL:].reshape(1, n_smp, D_MODEL)
    chunk = n_prompt // FINAL_CHUNKS
    y_p = None
    for c in range(FINAL_CHUNKS):
        c0 = c * chunk
        c1 = n_tok if c == FINAL_CHUNKS - 1 else c0 + chunk
        z = _combine(ys_flat, dest3[:, :, c0:c1].reshape(-1)).reshape(TOP_K, 2, c1 - c0, PLANE_W)
        y_p = _final(xmid_p2, g2_p, z, w_t, gfin, FINAL_TILE, n_tiles=chunk // FINAL_TILE,
                     x_tile0=c0 // FINAL_TILE, z_tile0=0, w_tile0=c0 // FINAL_TILE,
                     tiles_per_g2=seq // FINAL_TILE, y_prev=y_p)
    y_s = _final(xmid_s, g2_s, z, w_t, gfin, n_smp, n_tiles=1, x_tile0=0, z_tile0=chunk // n_smp,
                 w_tile0=n_prompt // n_smp, tiles_per_g2=1)

    return (y_p.reshape(bsz, seq, D_MODEL), y_s.reshape(n_smp, 1, D_MODEL),
            s_p[None], cv_p[None], s_s[None], cv_s.reshape(1, n_smp, CONV_K - 1, CONV_W))
```

```python
import functools

import jax
import jax.numpy as jnp
from jax import lax
from jax.experimental import pallas as pl
from jax.experimental.pallas import tpu as pltpu
from jax.experimental.pallas import tpu_sc as plsc

F32 = jnp.float32
BF16 = jnp.bfloat16
I32 = jnp.int32

D_MODEL = 1024
HALF_D = D_MODEL // 2
HEADS = 4
DK = 128
KEY_W = HEADS * DK
CONV_W = 512
CONV_K = 3
IN_W = 2 * KEY_W + 2 * KEY_W + 3 * CONV_W + 2 * D_MODEL
N_EXP = 64
TOP_K = 8
N_GRP = 8
GRP_SZ = N_EXP // N_GRP
TOPK_GRP = 4
EXP_FF = 256
SH_FF = 256
ROUTED_SCALE = 2.5
EPS = 1e-6

C_Q, C_F, C_I, C_G = 0, 512, 1024, 1536
C_BB, C_CC, C_VB = 2048, 2560, 3072
C_MGA, C_MGB = 3584, 4608

MIX_TILE = 512
SUB = 256
CHUNK = 64
ROUTE_TILE = 384
GMM_BM = 512
GMM_NX = 4
GMM_NY = 3
FINAL_TILE = 512
FINAL_CHUNKS = 4
SMP_GROUP = 8
SC_WINDOW = 128
PLANE_W = HALF_D // 2
VMEM_LIMIT = 56 * 1024 * 1024


def _dot(a, b):
    return jnp.dot(a, b, preferred_element_type=F32)


def _dot_nt(a, b):
    return lax.dot_general(a, b, (((1,), (1,)), ((), ())), preferred_element_type=F32)


def _dot_tn(a, b):
    return lax.dot_general(a, b, (((0,), (0,)), ((), ())), preferred_element_type=F32)


def _sigmoid(x):
    return 0.5 * jnp.tanh(0.5 * x) + 0.5


def _silu(x):
    h = 0.5 * x
    return h * jnp.tanh(h) + h


def _rms(x):
    return x * lax.rsqrt(jnp.mean(x * x, axis=-1, keepdims=True) + EPS)


def _lower_bound(lbl):
    a, b = lbl[0:1], lbl[1:2]
    m = jnp.maximum(a, b)
    ea, eb = jnp.exp(a - m), jnp.exp(b - m)
    return ea / (ea + eb)


def _split3(x):
    hi = x.astype(BF16)
    r1 = x - hi.astype(F32)
    mid = r1.astype(BF16)
    lo = (r1 - mid.astype(F32)).astype(BF16)
    return hi, mid, lo


def _words(lo_b, hi_b):
    lo = lax.shift_right_logical(lax.bitcast_convert_type(lo_b.astype(F32), I32), 16)
    hi = lax.bitcast_convert_type(hi_b.astype(F32), I32) & jnp.int32(-65536)
    return lo | hi


def _halves(w):
    lo = lax.bitcast_convert_type(lax.shift_left(w, 16), F32)
    hi = lax.bitcast_convert_type(w & jnp.int32(-65536), F32)
    return lo.astype(BF16), hi.astype(BF16)


def _pack_rows(xb, out_ref):
    words = _words(xb[:, :HALF_D], xb[:, HALF_D:])
    out_ref[0] = words[:, :PLANE_W]
    out_ref[1] = words[:, PLANE_W:]


def _unpack_rows(p0, p1):
    c0, c2 = _halves(p0)
    c1, c3 = _halves(p1)
    return c0, c1, c2, c3


def _ada_kernel(cp_ref, cs_ref, w_ref, b_ref, op_ref, os_ref):
    w = w_ref[...].astype(BF16)
    for c_ref, o_ref in ((cp_ref, op_ref), (cs_ref, os_ref)):
        o_ref[...] = _dot(_silu(c_ref[...]).astype(BF16), w) + b_ref[...]


def _ada(c_prompt, c_sample, w_ada, b_ada):
    blk = 1024
    rows = lambda c: pl.BlockSpec((c.shape[0], D_MODEL), lambda j: (0, 0))
    cols = lambda c: pl.BlockSpec((c.shape[0], blk), lambda j: (0, j))
    return pl.pallas_call(
        _ada_kernel,
        grid=(6 * D_MODEL // blk,),
        in_specs=[rows(c_prompt), rows(c_sample),
                  pl.BlockSpec((D_MODEL, blk), lambda j: (0, j)),
                  pl.BlockSpec((1, blk), lambda j: (0, j))],
        out_specs=[cols(c_prompt), cols(c_sample)],
        out_shape=[jax.ShapeDtypeStruct((c.shape[0], 6 * D_MODEL), F32) for c in (c_prompt, c_sample)],
        name="ada",
    )(c_prompt, c_sample, w_ada, b_ada.reshape(1, -1))


def _ffn_pre(x1, mod_rows, gffn, w_sgu, w_sd, wr_hl):
    sh2, sc2, g2 = mod_rows
    h2 = _rms(x1) * gffn * (1.0 + sc2) + sh2
    h2b = h2.astype(BF16)
    gu = _dot(h2b, w_sgu)
    act = _silu(gu[:, :SH_FF]) * gu[:, SH_FF:]
    xmid = x1 + g2 * _dot(act.astype(BF16), w_sd)
    h2lo = (h2 - h2b.astype(F32)).astype(BF16)
    both = _dot_nt(wr_hl, h2b)
    lgt = both[:N_EXP] + both[N_EXP:] + _dot_nt(wr_hl[:N_EXP], h2lo)
    return xmid, h2b, lgt


def _mix_kernel(*refs, nt, n_tiles):
    i = pl.program_id(0)
    h2_ref, lgt_ref = refs[16], refs[17]

    @pl.when(i == n_tiles)
    def _():
        h2_ref[...] = jnp.zeros_like(h2_ref)
        lgt_ref[...] = jnp.zeros_like(lgt_ref)

    @pl.when(i < n_tiles)
    def _():
        _mix_tile(i % nt, nt, *refs)


def _mix_tile(t, nt, x_ref, mod_ref, gmix_ref, gffn_ref, w_in_ref, lbl_ref, hg_ref, cw_ref, cb_ref,
              w_oh_ref, w_oc_ref, w_o_ref, wr_hl_ref, w_sgu_ref, w_sd_ref,
              xmid_ref, h2_ref, lgt_ref, s_out_ref, cv_out_ref,
              proj_ref, st_ref, cbuf_ref, ya_ref):
    tt = x_ref.shape[1]

    @pl.when(t == 0)
    def _():
        st_ref[...] = jnp.zeros_like(st_ref)
        cbuf_ref[...] = jnp.zeros_like(cbuf_ref)

    x = x_ref[0]
    mod = mod_ref[0]
    sh1, sc1, g1 = mod[0:1], mod[1:2], mod[2:3]
    h = _rms(x) * gmix_ref[...] * (1.0 + sc1) + sh1
    hb = h.astype(BF16)
    for c in range(0, IN_W, 512):
        proj_ref[:, c:c + 512] = _dot(hb, w_in_ref[:, c:c + 512])

    lb = _lower_bound(lbl_ref[...])
    row = lax.broadcasted_iota(I32, (SUB, SUB), 0)
    col = lax.broadcasted_iota(I32, (SUB, SUB), 1)
    tri = (col <= row).astype(BF16)
    mask_d = (col <= row) & (row // CHUNK == col // CHUNK)
    mask_a = row // (2 * CHUNK) == col // (2 * CHUNK)
    n_ch = SUB // CHUNK

    def by_chunk(vals):
        return jnp.concatenate([jnp.zeros((CHUNK, DK), F32) if v is None
                                else jnp.broadcast_to(v, (CHUNK, DK)) for v in vals], axis=0)

    for s in range(tt // SUB):
        r0 = s * SUB
        f = lb + (1.0 - lb) * _sigmoid(proj_ref[r0:r0 + SUB, C_F:C_F + KEY_W])
        kk = 1.0 - f
        hi, mid, lo = _split3(jnp.log(f))
        bc = _dot(tri, hi) + _dot(tri, mid) + _dot(tri, lo)
        for hd in range(HEADS):
            hs = slice(hd * DK, (hd + 1) * DK)
            bh = bc[:, hs]
            at = lambda r: bh[r:r + 1]
            mids = [at(c * CHUNK + CHUNK // 2 - 1) for c in range(n_ch)]
            pair_mid = [at(CHUNK - 1), at(3 * CHUNK - 1)]
            step_mid, step_end = at(2 * CHUNK - 1), at(SUB - 1)
            arg = bh - by_chunk(mids)
            e_pos, e_neg = jnp.exp(arg), jnp.exp(-arg)
            q = _silu(proj_ref[r0:r0 + SUB, C_Q + hd * DK:C_Q + (hd + 1) * DK])
            v = proj_ref[r0:r0 + SUB, C_I + hd * DK:C_I + (hd + 1) * DK]
            qd = q * e_pos
            kd = kk[:, hs] * e_neg
            q_in = qd * by_chunk([jnp.exp(m) for m in mids])
            k_end = kd * by_chunk([jnp.exp(step_end - m) for m in mids])
            qa = qd * by_chunk([None, jnp.exp(mids[1] - pair_mid[0]), None, jnp.exp(mids[3] - pair_mid[1])])
            ka = kd * by_chunk([jnp.exp(pair_mid[0] - mids[0]), None, jnp.exp(pair_mid[1] - mids[2]), None])
            qb = qd * by_chunk([None, None, jnp.exp(mids[2] - step_mid), jnp.exp(mids[3] - step_mid)])
            kb = kd * by_chunk([jnp.exp(step_mid - mids[0]), jnp.exp(step_mid - mids[1]), None, None])
            att = jnp.where(mask_d, _dot_nt(qd.astype(BF16), kd.astype(BF16)), 0.0)
            att = att + jnp.where(mask_a, _dot_nt(qa.astype(BF16), ka.astype(BF16)), 0.0)
            att = att + _dot_nt(qb.astype(BF16), kb.astype(BF16))
            vb = v.astype(BF16)
            st = st_ref[hd]
            o = _dot(att.astype(BF16), vb) + _dot_nt(q_in.astype(BF16), st.astype(BF16))
            st_ref[hd] = st * jnp.exp(step_end) + _dot_tn(vb, k_end.astype(BF16))
            gate = _silu(proj_ref[r0:r0 + SUB, C_G + hd * DK:C_G + (hd + 1) * DK])
            ya_ref[r0:r0 + SUB, hs] = _rms(o) * hg_ref[:, hs] * gate

    u = proj_ref[:, C_CC:C_CC + CONV_W] * proj_ref[:, C_VB:C_VB + CONV_W]
    rows = lax.broadcasted_iota(I32, (tt, CONV_W), 0)
    c0, c1 = cbuf_ref[0:1], cbuf_ref[1:2]
    u1 = jnp.where(rows == 0, c1, pltpu.roll(u, 1, axis=0))
    u2 = jnp.where(rows == 0, c0, jnp.where(rows == 1, c1, pltpu.roll(u, 2, axis=0)))
    conv = cw_ref[0:1] * u2 + cw_ref[1:2] * u1 + cw_ref[2:3] * u + cb_ref[...]
    yb = proj_ref[:, C_BB:C_BB + CONV_W] * conv
    cbuf_ref[...] = u[tt - 2:tt]

    mixed = (_sigmoid(proj_ref[:, C_MGA:C_MGA + D_MODEL]) * _dot(ya_ref[...].astype(BF16), w_oh_ref[...])
             + _sigmoid(proj_ref[:, C_MGB:C_MGB + D_MODEL]) * _dot(yb.astype(BF16), w_oc_ref[...]))
    x1 = x + g1 * _dot(mixed.astype(BF16), w_o_ref[...])

    xmid, h2b, lgt = _ffn_pre(x1, (mod[3:4], mod[4:5], mod[5:6]), gffn_ref[...],
                              w_sgu_ref[...], w_sd_ref[...], wr_hl_ref[...])
    xmid_ref[0] = xmid
    _pack_rows(h2b, h2_ref)
    lgt_ref[...] = lgt

    @pl.when(t == nt - 1)
    def _():
        for hd in range(HEADS):
            s_out_ref[0, hd] = st_ref[hd].T
        cv_out_ref[0] = cbuf_ref[...]


def _const_spec(shape):
    nd = len(shape)
    return pl.BlockSpec(shape, lambda i, _nd=nd: (0,) * _nd, pipeline_mode=pl.Buffered(1))


def _mix(x, mod, n_tok, gmix, gffn, w_in, lbl, hg, cw, cb, w_oh, w_oc, w_o, wr_hl, w_sgu, w_sd):
    bsz, seq, _ = x.shape
    tt = MIX_TILE
    nt = seq // tt
    n_tiles = bsz * nt
    assert n_tiles * tt < n_tok <= (n_tiles + 1) * tt
    consts = [gmix, gffn, w_in, lbl, hg, cw, cb, w_oh, w_oc, w_o, wr_hl, w_sgu, w_sd]
    tile = lambda i: jnp.minimum(i, n_tiles - 1)
    return pl.pallas_call(
        functools.partial(_mix_kernel, nt=nt, n_tiles=n_tiles),
        grid=(n_tiles + 1,),
        in_specs=[pl.BlockSpec((1, tt, D_MODEL), lambda i: (tile(i) // nt, tile(i) % nt, 0)),
                  pl.BlockSpec((1, 6, D_MODEL), lambda i: (tile(i) // nt, 0, 0))]
                 + [_const_spec(a.shape) for a in consts],
        out_specs=[pl.BlockSpec((1, tt, D_MODEL), lambda i: (tile(i) // nt, tile(i) % nt, 0)),
                   pl.BlockSpec((2, tt, PLANE_W), lambda i: (0, i, 0)),
                   pl.BlockSpec((N_EXP, tt), lambda i: (0, i)),
                   pl.BlockSpec((1, HEADS, DK, DK), lambda i: (tile(i) // nt, 0, 0, 0)),
                   pl.BlockSpec((1, CONV_K - 1, CONV_W), lambda i: (tile(i) // nt, 0, 0))],
        out_shape=[jax.ShapeDtypeStruct((bsz, seq, D_MODEL), F32),
                   jax.ShapeDtypeStruct((2, n_tok, PLANE_W), I32),
                   jax.ShapeDtypeStruct((N_EXP, n_tok), F32),
                   jax.ShapeDtypeStruct((bsz, HEADS, DK, DK), F32),
                   jax.ShapeDtypeStruct((bsz, CONV_K - 1, CONV_W), F32)],
        scratch_shapes=[pltpu.VMEM((tt, IN_W), F32),
                        pltpu.VMEM((HEADS, DK, DK), F32),
                        pltpu.VMEM((CONV_K - 1, CONV_W), F32),
                        pltpu.VMEM((tt, KEY_W), F32)],
        compiler_params=pltpu.CompilerParams(
            dimension_semantics=("arbitrary",), vmem_limit_bytes=VMEM_LIMIT),
        name="mix",
    )(x, mod, *consts)


def _smp1_kernel(x_ref, mod_ref, gmix_ref, w_in_ref, lbl_ref, cw_ref, cb_ref, cst_ref,
                 f_ref, k_ref, q_ref, v_ref, gate_ref, yb_ref, sga_ref, sgb_ref, cv_out_ref):
    x = x_ref[...]
    sh1, sc1 = mod_ref[:, 0:D_MODEL], mod_ref[:, D_MODEL:2 * D_MODEL]
    h = _rms(x) * gmix_ref[...] * (1.0 + sc1) + sh1
    hb = h.astype(BF16)

    def proj(c, w):
        return _dot(hb, w_in_ref[:, c:c + w])

    lb = _lower_bound(lbl_ref[...])
    f = lb + (1.0 - lb) * _sigmoid(proj(C_F, KEY_W))
    f_ref[...] = f
    k_ref[...] = 1.0 - f
    q_ref[...] = _silu(proj(C_Q, KEY_W))
    v_ref[...] = proj(C_I, KEY_W)
    gate_ref[...] = _silu(proj(C_G, KEY_W))
    u = proj(C_CC, CONV_W) * proj(C_VB, CONV_W)
    c0, c1 = cst_ref[:, 0:CONV_W], cst_ref[:, CONV_W:2 * CONV_W]
    conv = cw_ref[0:1] * c0 + cw_ref[1:2] * c1 + cw_ref[2:3] * u + cb_ref[...]
    yb_ref[...] = proj(C_BB, CONV_W) * conv
    cv_out_ref[:, 0:CONV_W] = c1
    cv_out_ref[:, CONV_W:2 * CONV_W] = u
    sga_ref[...] = _sigmoid(proj(C_MGA, D_MODEL))
    sgb_ref[...] = _sigmoid(proj(C_MGB, D_MODEL))


def _smp1(x, mod, gmix, w_in, lbl, cw, cb, cst):
    n = x.shape[0]
    kw = jax.ShapeDtypeStruct((n, KEY_W), F32)
    dm = jax.ShapeDtypeStruct((n, D_MODEL), F32)
    return pl.pallas_call(
        _smp1_kernel,
        out_shape=[kw, kw, kw, kw, kw, kw, dm, dm,
                   jax.ShapeDtypeStruct((n, (CONV_K - 1) * CONV_W), F32)],
        compiler_params=pltpu.CompilerParams(vmem_limit_bytes=VMEM_LIMIT),
        name="smp1",
    )(x, mod, gmix, w_in, lbl, cw, cb, cst)


def _smp2_kernel(f_ref, k_ref, q_ref, v_ref, s_ref, s_out_ref, o_ref):
    g = f_ref.shape[0]
    for i in range(g):
        for hd in range(HEADS):
            hs = slice(hd * DK, (hd + 1) * DK)

            def col(ref):
                return jnp.broadcast_to(ref[i:i + 1, hs], (DK, DK)).T

            vrow = v_ref[i:i + 1, hs]
            s_new = col(f_ref) * s_ref[i, hd] + col(k_ref) * vrow
            s_out_ref[i, hd] = s_new
            o_ref[i:i + 1, hs] = jnp.sum(col(q_ref) * s_new, axis=0, keepdims=True)


def _smp2(f, k, q, v, state):
    n = f.shape[0]
    g = SMP_GROUP
    row_spec = pl.BlockSpec((g, KEY_W), lambda i: (i, 0))
    st_spec = pl.BlockSpec((g, HEADS, DK, DK), lambda i: (i, 0, 0, 0))
    return pl.pallas_call(
        _smp2_kernel,
        grid=(n // g,),
        in_specs=[row_spec, row_spec, row_spec, row_spec, st_spec],
        out_specs=[st_spec, row_spec],
        out_shape=[jax.ShapeDtypeStruct(state.shape, F32), jax.ShapeDtypeStruct((n, KEY_W), F32)],
        compiler_params=pltpu.CompilerParams(dimension_semantics=("arbitrary",)),
        name="smp2",
    )(f, k, q, v, state)


def _smp3_kernel(x_ref, mod_ref, o_ref, gate_ref, yb_ref, sga_ref, sgb_ref, hg_ref, gffn_ref,
                 w_oh_ref, w_oc_ref, w_o_ref, wr_hl_ref, w_sgu_ref, w_sd_ref,
                 h2_all_ref, lgt_all_ref, xmid_ref, h2_ref, lgt_ref):
    del h2_all_ref, lgt_all_ref
    parts = []
    for hd in range(HEADS):
        hs = slice(hd * DK, (hd + 1) * DK)
        parts.append(_rms(o_ref[:, hs]) * hg_ref[:, hs] * gate_ref[:, hs])
    ya = jnp.concatenate(parts, axis=1)
    mixed = (sga_ref[...] * _dot(ya.astype(BF16), w_oh_ref[...])
             + sgb_ref[...] * _dot(yb_ref[...].astype(BF16), w_oc_ref[...]))
    g1 = mod_ref[:, 2 * D_MODEL:3 * D_MODEL]
    x1 = x_ref[...] + g1 * _dot(mixed.astype(BF16), w_o_ref[...])
    mod_rows = tuple(mod_ref[:, j * D_MODEL:(j + 1) * D_MODEL] for j in (3, 4, 5))
    xmid, h2b, lgt = _ffn_pre(x1, mod_rows, gffn_ref[...], w_sgu_ref[...], w_sd_ref[...],
                              wr_hl_ref[...])
    xmid_ref[...] = xmid
    _pack_rows(h2b, h2_ref)
    lgt_ref[...] = lgt


def _smp3(x, mod, o, gate, yb, sga, sgb, hg, gffn, w_oh, w_oc, w_o, wr_hl, w_sgu, w_sd,
          h2_all, lgt_all, n_prompt):
    n = x.shape[0]
    vmem_args = [x, mod, o, gate, yb, sga, sgb, hg, gffn, w_oh, w_oc, w_o, wr_hl, w_sgu, w_sd]
    blk = n_prompt // n

    def full(a):
        nd = a.ndim
        return pl.BlockSpec(a.shape, lambda i, _nd=nd: (0,) * _nd)

    return pl.pallas_call(
        _smp3_kernel,
        grid=(1,),
        in_specs=[full(a) for a in vmem_args]
                 + [pl.BlockSpec(memory_space=pl.ANY), pl.BlockSpec(memory_space=pl.ANY)],
        out_specs=[pl.BlockSpec((n, D_MODEL), lambda i: (0, 0)),
                   pl.BlockSpec((2, n, PLANE_W), lambda i: (0, blk, 0)),
                   pl.BlockSpec((N_EXP, n), lambda i: (0, blk))],
        out_shape=[jax.ShapeDtypeStruct((n, D_MODEL), F32),
                   jax.ShapeDtypeStruct(h2_all.shape, h2_all.dtype),
                   jax.ShapeDtypeStruct(lgt_all.shape, lgt_all.dtype)],
        input_output_aliases={len(vmem_args): 1, len(vmem_args) + 1: 2},
        compiler_params=pltpu.CompilerParams(
            dimension_semantics=("arbitrary",), vmem_limit_bytes=VMEM_LIMIT),
        name="smp3",
    )(*vmem_args, h2_all, lgt_all)


def _route_kernel(lgt_ref, bias_ref, idx_ref, w_ref, rank_ref, cnt_ref):
    tr = ROUTE_TILE
    n_tiles = lgt_ref.shape[1] // tr

    def tile(i, carry):
        cols = pl.ds(pl.multiple_of(i * tr, tr), tr)
        picks, weights, ranks, carry = _route_tile(lgt_ref[:, cols], bias_ref[...], carry)
        for k in range(TOP_K):
            idx_ref[k:k + 1, cols] = picks[k]
            w_ref[k:k + 1, cols] = weights[k]
            rank_ref[k:k + 1, cols] = ranks[k]
        return carry

    total = lax.fori_loop(0, n_tiles, tile, jnp.zeros((N_EXP, 1), F32))
    cnt_ref[...] = jnp.broadcast_to(total, cnt_ref.shape).astype(I32)


def _route_tile(lgt, bias, carry):
    tr = lgt.shape[1]
    neg = -jnp.inf
    scores = _sigmoid(lgt)
    sel = scores + bias
    j8 = lax.broadcasted_iota(I32, (GRP_SZ, tr), 0)
    groups = [sel[g * GRP_SZ:(g + 1) * GRP_SZ] for g in range(N_GRP)]
    gscore = []
    for grp in groups:
        m1 = jnp.max(grp, axis=0, keepdims=True)
        i1 = jnp.min(jnp.where(grp == m1, j8, GRP_SZ), axis=0, keepdims=True)
        m2 = jnp.max(jnp.where(j8 == i1, neg, grp), axis=0, keepdims=True)
        gscore.append(m1 + m2)
    kept = []
    for g in range(N_GRP):
        beaten = jnp.zeros((1, tr), I32)
        for o in range(N_GRP):
            if o < g:
                beaten = beaten + (gscore[o] >= gscore[g]).astype(I32)
            elif o > g:
                beaten = beaten + (gscore[o] > gscore[g]).astype(I32)
        kept.append(jnp.where(beaten < TOPK_GRP, groups[g], neg))
    masked = jnp.concatenate(kept, axis=0)
    ei = lax.broadcasted_iota(I32, masked.shape, 0)
    chosen = jnp.zeros(masked.shape, jnp.bool_)
    picks, weights = [], []
    for _ in range(TOP_K):
        m = jnp.max(masked, axis=0, keepdims=True)
        pick = jnp.min(jnp.where(masked == m, ei, N_EXP), axis=0, keepdims=True)
        hit = ei == pick
        weights.append(jnp.sum(jnp.where(hit, scores, 0.0), axis=0, keepdims=True))
        picks.append(pick)
        chosen = chosen | hit
        masked = jnp.where(hit, neg, masked)
    wsum = weights[0]
    for w in weights[1:]:
        wsum = wsum + w
    sel01 = chosen.astype(F32)
    r = lax.broadcasted_iota(I32, (tr, tr), 0)
    c = lax.broadcasted_iota(I32, (tr, tr), 1)
    before = (r < c).astype(BF16)
    cnt = _dot(sel01.astype(BF16), before) + carry
    weights = [w / wsum * ROUTED_SCALE for w in weights]
    ranks = [jnp.sum(jnp.where(ei == p, cnt, 0.0), axis=0, keepdims=True).astype(I32) for p in picks]
    return picks, weights, ranks, carry + jnp.sum(sel01, axis=1, keepdims=True)


def _route(lgt, bias):
    n = lgt.shape[1]
    assert n % ROUTE_TILE == 0
    slot = lambda dt: jax.ShapeDtypeStruct((TOP_K, n), dt)
    return pl.pallas_call(
        _route_kernel,
        out_shape=[slot(I32), slot(F32), slot(I32),
                   jax.ShapeDtypeStruct((N_EXP, 128), I32)],
        name="route",
    )(lgt, bias)


def _dest_kernel(start_ref, idx_ref, rank_ref, all_ref, *chunk_refs, n_rows, chunks):
    n_tok = idx_ref.shape[1]
    idx = idx_ref[...]
    acc = rank_ref[...]
    for e in range(N_EXP):
        acc = acc + jnp.where(idx == e, start_ref[e], 0)
    for k in range(TOP_K):
        for p in range(2):
            row = acc[k:k + 1, :] + p * n_rows
            seg = 2 * k + p
            all_ref[:, seg * n_tok:(seg + 1) * n_tok] = row
            for (c0, c1), ref in zip(chunks, chunk_refs):
                ref[:, seg * (c1 - c0):(seg + 1) * (c1 - c0)] = row[:, c0:c1]


def _dest(pad_start, idx, rank, n_rows, chunks):
    k, n_tok = idx.shape
    vmem = pl.BlockSpec(memory_space=pltpu.VMEM)
    sizes = [n_tok] + [c1 - c0 for c0, c1 in chunks]
    return pl.pallas_call(
        functools.partial(_dest_kernel, n_rows=n_rows, chunks=chunks),
        in_specs=[pl.BlockSpec(memory_space=pltpu.SMEM), vmem, vmem],
        out_specs=[vmem] * len(sizes),
        out_shape=[jax.ShapeDtypeStruct((1, 2 * k * n), I32) for n in sizes],
        name="dest",
    )(pad_start, idx, rank)


def _sc_mesh():
    return plsc.VectorSubcoreMesh(core_axis_name="core", subcore_axis_name="subcore")


def _dispatch(rows, dest, n_out):
    n, width = rows.shape
    win = SC_WINDOW
    steps = n // win

    @pl.kernel(out_type=jax.ShapeDtypeStruct((n_out, width), rows.dtype), mesh=_sc_mesh(),
               scratch_types=[], name="dispatch")
    def run(x_hbm, *refs):
        i_hbms, o_hbm = refs[:TOP_K], refs[TOP_K]

        def body(x_vmem, *i_vmems):
            for i_vmem in i_vmems:
                pltpu.sync_copy(x_vmem, o_hbm.at[i_vmem.at[0]])

        pltpu.emit_pipeline(
            body,
            grid=(steps,),
            in_specs=[pl.BlockSpec((win, width), lambda i: (i, 0))]
                     + [pl.BlockSpec((1, win), lambda i, k=k: (0, k * steps + i)) for k in range(TOP_K)],
            out_specs=[],
            core_axis_name=("core", "subcore"),
            dimension_semantics=(pltpu.PARALLEL,),
        )(x_hbm, *i_hbms)

    assert dest.shape == (1, TOP_K * n)
    return run(rows, *([dest] * TOP_K))


def _combine(rows, dest_flat):
    width = rows.shape[1]
    n = dest_flat.shape[1]
    win = SC_WINDOW

    @pl.kernel(out_type=jax.ShapeDtypeStruct((n, width), rows.dtype), mesh=_sc_mesh(),
               scratch_types=[], name="combine")
    def run(y_hbm, i_hbm, o_hbm):
        def body(i_vmem, o_vmem):
            pltpu.sync_copy(y_hbm.at[i_vmem.at[0]], o_vmem)

        pltpu.emit_pipeline(
            body,
            grid=(n // win,),
            in_specs=[pl.BlockSpec((1, win), lambda i: (0, i))],
            out_specs=[pl.BlockSpec((win, width), lambda i: (i, 0))],
            core_axis_name=("core", "subcore"),
            dimension_semantics=(pltpu.PARALLEL,),
        )(i_hbm, o_hbm)

    return run(rows, dest_flat)


def _gmm_kernel(blk_exp_ref, n_used_ref, xs_hbm, wg_hbm, wu_hbm, wd_hbm, ys_hbm,
                xbuf, ybuf, wg32, wu32, wd32, wgu_b, wd_b, xsem, ysem, wsem, run_ref):
    nx, ny, bm = xbuf.shape[0], ybuf.shape[0], xbuf.shape[2]
    n_used = n_used_ref[0]

    def x_copies(b):
        rows, slot = pl.ds(b * bm, bm), b % nx
        return [pltpu.make_async_copy(xs_hbm.at[p, rows, :], xbuf.at[slot, p], xsem.at[slot, p]) for p in range(2)]

    def y_copies(b):
        rows, slot = pl.ds(b * bm, bm), b % ny
        return [pltpu.make_async_copy(ybuf.at[slot, p], ys_hbm.at[p, rows, :], ysem.at[slot, p]) for p in range(2)]

    def start(copies):
        for c in copies:
            c.start()

    def wait(copies):
        for c in copies:
            c.wait()

    def w_copies(e, slot):
        return (pltpu.make_async_copy(wg_hbm.at[e], wg32.at[slot], wsem.at[slot, 0]),
                pltpu.make_async_copy(wu_hbm.at[e], wu32.at[slot], wsem.at[slot, 1]),
                pltpu.make_async_copy(wd_hbm.at[e], wd32.at[slot], wsem.at[slot, 2]))

    def run_end(b):
        return lax.while_loop(lambda j: (j < n_used) & (blk_exp_ref[jnp.minimum(j, n_used - 1)] == blk_exp_ref[b]),
                              lambda j: j + 1, b + 1)

    run_ref[0] = 0
    start(x_copies(0))
    start(w_copies(blk_exp_ref[0], 0))
    for j in range(1, nx - 1):
        @pl.when(j < n_used)
        def _():
            start(x_copies(j))

    def block(b, carry):
        @pl.when(b + nx - 1 < n_used)
        def _():
            start(x_copies(b + nx - 1))

        @pl.when((b == 0) | (blk_exp_ref[b] != blk_exp_ref[jnp.maximum(b - 1, 0)]))
        def _():
            wslot = run_ref[0] % 2
            run_ref[0] = run_ref[0] + 1
            wait(w_copies(blk_exp_ref[b], wslot))
            wgu_b[:, 0:EXP_FF] = wg32[wslot].astype(BF16)
            wgu_b[:, EXP_FF:2 * EXP_FF] = wu32[wslot].astype(BF16)
            wd_b[...] = wd32[wslot].astype(BF16)
            nxt = run_end(b)

            @pl.when(nxt < n_used)
            def _():
                start(w_copies(blk_exp_ref[jnp.minimum(nxt, n_used - 1)], 1 - wslot))

        wait(x_copies(b))

        @pl.when(b >= ny)
        def _():
            wait(y_copies(b - ny))

        xslot = b % nx
        xc = _unpack_rows(xbuf[xslot, 0], xbuf[xslot, 1])
        gu = sum(_dot(c, wgu_b[i * PLANE_W:(i + 1) * PLANE_W, :]) for i, c in enumerate(xc))
        act = (_silu(gu[:, :EXP_FF]) * gu[:, EXP_FF:]).astype(BF16)
        _pack_rows(_dot(act, wd_b[...]).astype(BF16), ybuf.at[b % ny])
        start(y_copies(b))
        return carry

    lax.fori_loop(0, n_used, block, 0)

    for j in range(ny, 0, -1):
        @pl.when(n_used >= j)
        def _():
            wait(y_copies(n_used - j))


def _gmm(blk_exp, n_used, xs, w_gate, w_up, w_down):
    n_rows = xs.shape[1]
    bm = GMM_BM
    nb = n_rows // bm

    assert blk_exp.shape == (nb,)
    any_spec = pl.BlockSpec(memory_space=pl.ANY)
    grid_spec = pltpu.PrefetchScalarGridSpec(
        num_scalar_prefetch=2,
        grid=(1,),
        in_specs=[any_spec, any_spec, any_spec, any_spec],
        out_specs=any_spec,
        scratch_shapes=[pltpu.VMEM((GMM_NX, 2, bm, PLANE_W), I32), pltpu.VMEM((GMM_NY, 2, bm, PLANE_W), I32),
                        pltpu.VMEM((2, D_MODEL, EXP_FF), F32), pltpu.VMEM((2, D_MODEL, EXP_FF), F32),
                        pltpu.VMEM((2, EXP_FF, D_MODEL), F32),
                        pltpu.VMEM((D_MODEL, 2 * EXP_FF), BF16), pltpu.VMEM((EXP_FF, D_MODEL), BF16),
                        pltpu.SemaphoreType.DMA((GMM_NX, 2)), pltpu.SemaphoreType.DMA((GMM_NY, 2)),
                        pltpu.SemaphoreType.DMA((2, 3)), pltpu.SMEM((1,), I32)],
    )
    return pl.pallas_call(
        _gmm_kernel,
        grid_spec=grid_spec,
        out_shape=jax.ShapeDtypeStruct((2, n_rows, PLANE_W), I32),
        compiler_params=pltpu.CompilerParams(dimension_semantics=("arbitrary",)),
        name="gmm",
    )(blk_exp, n_used, xs, w_gate, w_up, w_down)


def _final_kernel(xmid_ref, g2_ref, z_ref, w_ref, gfin_ref, *rest):
    y_ref = rest[-1]
    accs = [jnp.zeros((xmid_ref.shape[0], PLANE_W), F32) for _ in range(4)]
    w_cols = w_ref[...].T
    for k in range(TOP_K):
        wk = w_cols[:, k:k + 1]
        cols = _unpack_rows(z_ref[k, 0], z_ref[k, 1])
        accs = [a + wk * c.astype(F32) for a, c in zip(accs, cols)]
    acc = jnp.concatenate(accs, axis=1)
    y_ref[...] = _rms(xmid_ref[...] + g2_ref[0] * acc) * gfin_ref[...]


def _final(xmid, g2, z, w_t, gfin, tile, *, n_tiles, x_tile0, z_tile0, w_tile0, tiles_per_g2, y_prev=None):
    args = [xmid, g2, z, w_t, gfin]
    in_specs = [pl.BlockSpec((tile, D_MODEL), lambda i: (x_tile0 + i, 0)),
                pl.BlockSpec((1, g2.shape[1], D_MODEL), lambda i: ((x_tile0 + i) // tiles_per_g2, 0, 0)),
                pl.BlockSpec((TOP_K, 2, tile, PLANE_W), lambda i: (0, 0, z_tile0 + i, 0)),
                pl.BlockSpec((TOP_K, tile), lambda i: (0, w_tile0 + i)),
                pl.BlockSpec((1, D_MODEL), lambda i: (0, 0))]
    aliases = {}
    if y_prev is not None:
        args.append(y_prev)
        in_specs.append(pl.BlockSpec(memory_space=pl.ANY))
        aliases = {len(args) - 1: 0}
    return pl.pallas_call(
        _final_kernel,
        grid=(n_tiles,),
        in_specs=in_specs,
        out_specs=pl.BlockSpec((tile, D_MODEL), lambda i: (x_tile0 + i, 0)),
        out_shape=jax.ShapeDtypeStruct(xmid.shape, F32),
        input_output_aliases=aliases,
        compiler_params=pltpu.CompilerParams(dimension_semantics=("arbitrary",)),
        name="final",
    )(*args)


def kernel(x_prompt, x_sample, state_hgrn, state_conv, c_prompt, c_sample, w_ada, b_ada, norm_mix_g, norm_ffn_g, w_in, lb_logits, hgrn_norm_g, conv_w, conv_b, w_out_hgrn, w_out_conv, w_o, w_router, router_bias, w_exp_gate, w_exp_up, w_exp_down, w_sh_gate, w_sh_up, w_sh_down, final_norm_g):
    assert w_ada.shape[0] == 1 and lb_logits.shape[0] == 2
    bsz, seq, _ = x_prompt.shape
    n_smp = x_sample.shape[0]
    n_prompt = bsz * seq
    n_tok = n_prompt + n_smp

    w_in_b = w_in[0].astype(BF16)
    w_oh_b = w_out_hgrn[0].astype(BF16)
    w_oc_b = w_out_conv[0].astype(BF16)
    w_o_b = w_o[0].astype(BF16)
    wr_t = w_router[0].T
    wr_hi = wr_t.astype(BF16)
    wr_hl = jnp.concatenate([wr_hi, (wr_t - wr_hi.astype(F32)).astype(BF16)], axis=0)
    w_sgu = jnp.concatenate([w_sh_gate[0], w_sh_up[0]], axis=1).astype(BF16)
    w_sd = w_sh_down[0].astype(BF16)
    gmix = norm_mix_g[0].reshape(1, D_MODEL)
    gffn = norm_ffn_g[0].reshape(1, D_MODEL)
    hg = hgrn_norm_g[0].reshape(1, KEY_W)
    cw = conv_w[0]
    cb = conv_b[0].reshape(1, CONV_W)
    gfin = final_norm_g.reshape(1, D_MODEL)

    mod_p, mod_s = _ada(c_prompt, c_sample, w_ada[0], b_ada[0])

    xmid_p, h2_all, lgt_all, s_p, cv_p = _mix(
        x_prompt, mod_p.reshape(bsz, 6, D_MODEL), n_tok, gmix, gffn, w_in_b, lb_logits, hg, cw, cb, w_oh_b, w_oc_b, w_o_b, wr_hl, w_sgu, w_sd)

    xs2 = x_sample.reshape(n_smp, D_MODEL)
    f, kk, q, v, gate, yb, sga, sgb, cv_s = _smp1(
        xs2, mod_s, gmix, w_in_b, lb_logits, cw, cb, state_conv[0].reshape(n_smp, (CONV_K - 1) * CONV_W))
    s_s, o_s = _smp2(f, kk, q, v, state_hgrn[0])
    xmid_s, h2_all, lgt_all = _smp3(xs2, mod_s, o_s, gate, yb, sga, sgb, hg, gffn,
                                    w_oh_b, w_oc_b, w_o_b, wr_hl, w_sgu, w_sd,
                                    h2_all, lgt_all, n_prompt)

    idx, w_tok, rank, cnt = _route(lgt_all, router_bias[0].reshape(N_EXP, 1))

    bm = GMM_BM
    n_blocks = (n_tok * TOP_K + N_EXP * (bm - 1)) // bm
    n_rows = n_blocks * bm
    counts = cnt[:, 0]
    padded = (counts + bm - 1) // bm * bm
    pad_end = jnp.cumsum(padded)
    pad_start = pad_end - padded
    blk_row0 = jnp.arange(n_blocks, dtype=I32) * bm
    blk_exp = jnp.minimum(jnp.sum((pad_end[None, :] <= blk_row0[:, None]).astype(I32), axis=1), N_EXP - 1)
    n_used = (pad_end[-1:] // bm).astype(I32)
    chunk = n_prompt // FINAL_CHUNKS
    chunks = tuple((c * chunk, n_tok if c == FINAL_CHUNKS - 1 else (c + 1) * chunk) for c in range(FINAL_CHUNKS))
    dest, *chunk_dest = _dest(pad_start.astype(I32), idx, rank, n_rows, chunks)

    xs = _dispatch(h2_all.reshape(2 * n_tok, PLANE_W), dest, 2 * n_rows).reshape(2, n_rows, PLANE_W)
    ys = _gmm(blk_exp, n_used, xs, w_exp_gate[0], w_exp_up[0], w_exp_down[0])

    ys_flat = ys.reshape(2 * n_rows, PLANE_W)
    w_t = w_tok
    xmid_p2 = xmid_p.reshape(n_prompt, D_MODEL)
    g2_p = mod_p[:, 5 * D_MODEL:].reshape(bsz, 1, D_MODEL)
    g2_s = mod_s[:, 5 * D_MODEL:].reshape(1, n_smp, D_MODEL)
    y_p = None
    for (c0, c1), dest_c in zip(chunks, chunk_dest):
        z = _combine(ys_flat, dest_c).reshape(TOP_K, 2, c1 - c0, PLANE_W)
        y_p = _final(xmid_p2, g2_p, z, w_t, gfin, FINAL_TILE, n_tiles=chunk // FINAL_TILE,
                     x_tile0=c0 // FINAL_TILE, z_tile0=0, w_tile0=c0 // FINAL_TILE,
                     tiles_per_g2=seq // FINAL_TILE, y_prev=y_p)
    y_s = _final(xmid_s, g2_s, z, w_t, gfin, n_smp, n_tiles=1, x_tile0=0, z_tile0=chunk // n_smp,
                 w_tile0=n_prompt // n_smp, tiles_per_g2=1)

    return (y_p.reshape(bsz, seq, D_MODEL), y_s.reshape(n_smp, 1, D_MODEL),
            s_p[None], cv_p[None], s_s[None], cv_s.reshape(1, n_smp, CONV_K - 1, CONV_W))
```

```python
import functools

import jax
import jax.numpy as jnp
from jax import lax
from jax.experimental import pallas as pl
from jax.experimental.pallas import tpu as pltpu
from jax.experimental.pallas import tpu_sc as plsc

F32 = jnp.float32
BF16 = jnp.bfloat16
I32 = jnp.int32

D_MODEL = 1024
HALF_D = D_MODEL // 2
HEADS = 4
DK = 128
KEY_W = HEADS * DK
CONV_W = 512
CONV_K = 3
IN_W = 2 * KEY_W + 2 * KEY_W + 3 * CONV_W + 2 * D_MODEL
N_EXP = 64
TOP_K = 8
N_GRP = 8
GRP_SZ = N_EXP // N_GRP
TOPK_GRP = 4
EXP_FF = 256
SH_FF = 256
ROUTED_SCALE = 2.5
EPS = 1e-6

C_Q, C_F, C_I, C_G = 0, 512, 1024, 1536
C_BB, C_CC, C_VB = 2048, 2560, 3072
C_MGA, C_MGB = 3584, 4608

MIX_TILE = 512
SUB = 256
CHUNK = 64
ROUTE_TILE = 384
GMM_BM = 512
GMM_NX = 4
GMM_NY = 3
FINAL_TILE = 512
FINAL_CHUNKS = 4
SMP_GROUP = 16
SC_WINDOW = 128
PLANE_W = HALF_D // 2
V7X_VMEM_BYTES = 64 * 1024 * 1024
VMEM_LIMIT = V7X_VMEM_BYTES - 8 * 1024 * 1024


def _dot(a, b):
    return jnp.dot(a, b, preferred_element_type=F32)


def _dot_nt(a, b):
    return lax.dot_general(a, b, (((1,), (1,)), ((), ())), preferred_element_type=F32)


def _dot_tn(a, b):
    return lax.dot_general(a, b, (((0,), (0,)), ((), ())), preferred_element_type=F32)


def _sigmoid(x):
    return 0.5 * jnp.tanh(0.5 * x) + 0.5


def _silu(x):
    h = 0.5 * x
    return h * jnp.tanh(h) + h


def _rms(x):
    return x * lax.rsqrt(jnp.mean(x * x, axis=-1, keepdims=True) + EPS)


def _lower_bound(lbl):
    a, b = lbl[0:1], lbl[1:2]
    m = jnp.maximum(a, b)
    ea, eb = jnp.exp(a - m), jnp.exp(b - m)
    return ea / (ea + eb)


def _split3(x):
    hi = x.astype(BF16)
    r1 = x - hi.astype(F32)
    mid = r1.astype(BF16)
    lo = (r1 - mid.astype(F32)).astype(BF16)
    return hi, mid, lo


def _words(lo_b, hi_b):
    lo = lax.shift_right_logical(lax.bitcast_convert_type(lo_b.astype(F32), I32), 16)
    hi = lax.bitcast_convert_type(hi_b.astype(F32), I32) & jnp.int32(-65536)
    return lo | hi


def _halves(w):
    lo = lax.bitcast_convert_type(lax.shift_left(w, 16), F32)
    hi = lax.bitcast_convert_type(w & jnp.int32(-65536), F32)
    return lo.astype(BF16), hi.astype(BF16)


def _pack_rows(xb, out_ref):
    words = _words(xb[:, :HALF_D], xb[:, HALF_D:])
    out_ref[0] = words[:, :PLANE_W]
    out_ref[1] = words[:, PLANE_W:]


def _unpack_rows(p0, p1):
    c0, c2 = _halves(p0)
    c1, c3 = _halves(p1)
    return c0, c1, c2, c3


def _ada_kernel(cp_ref, cs_ref, w_ref, b_ref, op_ref, os_ref):
    w = w_ref[...].astype(BF16)
    for c_ref, o_ref in ((cp_ref, op_ref), (cs_ref, os_ref)):
        o_ref[...] = _dot(_silu(c_ref[...]).astype(BF16), w) + b_ref[...]


def _ada(c_prompt, c_sample, w_ada, b_ada):
    blk = 1024
    rows = lambda c: pl.BlockSpec((c.shape[0], D_MODEL), lambda j: (0, 0))
    cols = lambda c: pl.BlockSpec((c.shape[0], blk), lambda j: (0, j))
    return pl.pallas_call(
        _ada_kernel,
        grid=(6 * D_MODEL // blk,),
        in_specs=[rows(c_prompt), rows(c_sample),
                  pl.BlockSpec((D_MODEL, blk), lambda j: (0, j)),
                  pl.BlockSpec((1, blk), lambda j: (0, j))],
        out_specs=[cols(c_prompt), cols(c_sample)],
        out_shape=[jax.ShapeDtypeStruct((c.shape[0], 6 * D_MODEL), F32) for c in (c_prompt, c_sample)],
        name="ada",
    )(c_prompt, c_sample, w_ada, b_ada.reshape(1, -1))


def _ffn_pre(x1, mod_rows, gffn, w_sgu, w_sd, wr_hl):
    sh2, sc2, g2 = mod_rows
    h2 = _rms(x1) * gffn * (1.0 + sc2) + sh2
    h2b = h2.astype(BF16)
    gu = _dot(h2b, w_sgu)
    act = _silu(gu[:, :SH_FF]) * gu[:, SH_FF:]
    xmid = x1 + g2 * _dot(act.astype(BF16), w_sd)
    h2lo = (h2 - h2b.astype(F32)).astype(BF16)
    both = _dot_nt(wr_hl, h2b)
    lgt = both[:N_EXP] + both[N_EXP:] + _dot_nt(wr_hl[:N_EXP], h2lo)
    return xmid, h2b, lgt


def _mix_kernel(*refs, nt, n_tiles):
    i = pl.program_id(0)
    h2_ref, lgt_ref = refs[16], refs[17]

    @pl.when(i == n_tiles)
    def _():
        h2_ref[...] = jnp.zeros_like(h2_ref)
        lgt_ref[...] = jnp.zeros_like(lgt_ref)

    @pl.when(i < n_tiles)
    def _():
        _mix_tile(i % nt, nt, *refs)


def _mix_tile(t, nt, x_ref, mod_ref, gmix_ref, gffn_ref, w_in_ref, lbl_ref, hg_ref, cw_ref, cb_ref,
              w_oh_ref, w_oc_ref, w_o_ref, wr_hl_ref, w_sgu_ref, w_sd_ref,
              xmid_ref, h2_ref, lgt_ref, s_out_ref, cv_out_ref,
              proj_ref, st_ref, cbuf_ref, ya_ref):
    tt = x_ref.shape[1]

    @pl.when(t == 0)
    def _():
        st_ref[...] = jnp.zeros_like(st_ref)
        cbuf_ref[...] = jnp.zeros_like(cbuf_ref)

    x = x_ref[0]
    mod = mod_ref[0]
    sh1, sc1, g1 = mod[0:1], mod[1:2], mod[2:3]
    h = _rms(x) * gmix_ref[...] * (1.0 + sc1) + sh1
    hb = h.astype(BF16)
    for c in range(0, IN_W, 512):
        proj_ref[:, c:c + 512] = _dot(hb, w_in_ref[:, c:c + 512])

    lb = _lower_bound(lbl_ref[...])
    row = lax.broadcasted_iota(I32, (SUB, SUB), 0)
    col = lax.broadcasted_iota(I32, (SUB, SUB), 1)
    tri = (col <= row).astype(BF16)
    mask_d = (col <= row) & (row // CHUNK == col // CHUNK)
    mask_a = row // (2 * CHUNK) == col // (2 * CHUNK)
    n_ch = SUB // CHUNK

    def by_chunk(vals):
        return jnp.concatenate([jnp.zeros((CHUNK, DK), F32) if v is None
                                else jnp.broadcast_to(v, (CHUNK, DK)) for v in vals], axis=0)

    for s in range(tt // SUB):
        r0 = s * SUB
        f = lb + (1.0 - lb) * _sigmoid(proj_ref[r0:r0 + SUB, C_F:C_F + KEY_W])
        kk = 1.0 - f
        hi, mid, lo = _split3(jnp.log(f))
        bc = _dot(tri, hi) + _dot(tri, mid) + _dot(tri, lo)
        for hd in range(HEADS):
            hs = slice(hd * DK, (hd + 1) * DK)
            bh = bc[:, hs]
            at = lambda r: bh[r:r + 1]
            mids = [at(c * CHUNK + CHUNK // 2 - 1) for c in range(n_ch)]
            pair_mid = [at(CHUNK - 1), at(3 * CHUNK - 1)]
            step_mid, step_end = at(2 * CHUNK - 1), at(SUB - 1)
            arg = bh - by_chunk(mids)
            e_pos, e_neg = jnp.exp(arg), jnp.exp(-arg)
            q = _silu(proj_ref[r0:r0 + SUB, C_Q + hd * DK:C_Q + (hd + 1) * DK])
            v = proj_ref[r0:r0 + SUB, C_I + hd * DK:C_I + (hd + 1) * DK]
            qd = q * e_pos
            kd = kk[:, hs] * e_neg
            q_in = qd * by_chunk([jnp.exp(m) for m in mids])
            k_end = kd * by_chunk([jnp.exp(step_end - m) for m in mids])
            qa = qd * by_chunk([None, jnp.exp(mids[1] - pair_mid[0]), None, jnp.exp(mids[3] - pair_mid[1])])
            ka = kd * by_chunk([jnp.exp(pair_mid[0] - mids[0]), None, jnp.exp(pair_mid[1] - mids[2]), None])
            qb = qd * by_chunk([None, None, jnp.exp(mids[2] - step_mid), jnp.exp(mids[3] - step_mid)])
            kb = kd * by_chunk([jnp.exp(step_mid - mids[0]), jnp.exp(step_mid - mids[1]), None, None])
            att = jnp.where(mask_d, _dot_nt(qd.astype(BF16), kd.astype(BF16)), 0.0)
            att = att + jnp.where(mask_a, _dot_nt(qa.astype(BF16), ka.astype(BF16)), 0.0)
            att = att + _dot_nt(qb.astype(BF16), kb.astype(BF16))
            vb = v.astype(BF16)
            st = st_ref[hd]
            o = _dot(att.astype(BF16), vb) + _dot_nt(q_in.astype(BF16), st.astype(BF16))
            st_ref[hd] = st * jnp.exp(step_end) + _dot_tn(vb, k_end.astype(BF16))
            gate = _silu(proj_ref[r0:r0 + SUB, C_G + hd * DK:C_G + (hd + 1) * DK])
            ya_ref[r0:r0 + SUB, hs] = _rms(o) * hg_ref[:, hs] * gate

    u = proj_ref[:, C_CC:C_CC + CONV_W] * proj_ref[:, C_VB:C_VB + CONV_W]
    rows = lax.broadcasted_iota(I32, (tt, CONV_W), 0)
    c0, c1 = cbuf_ref[0:1], cbuf_ref[1:2]
    u1 = jnp.where(rows == 0, c1, pltpu.roll(u, 1, axis=0))
    u2 = jnp.where(rows == 0, c0, jnp.where(rows == 1, c1, pltpu.roll(u, 2, axis=0)))
    conv = cw_ref[0:1] * u2 + cw_ref[1:2] * u1 + cw_ref[2:3] * u + cb_ref[...]
    yb = proj_ref[:, C_BB:C_BB + CONV_W] * conv
    cbuf_ref[...] = u[tt - 2:tt]

    mixed = (_sigmoid(proj_ref[:, C_MGA:C_MGA + D_MODEL]) * _dot(ya_ref[...].astype(BF16), w_oh_ref[...])
             + _sigmoid(proj_ref[:, C_MGB:C_MGB + D_MODEL]) * _dot(yb.astype(BF16), w_oc_ref[...]))
    x1 = x + g1 * _dot(mixed.astype(BF16), w_o_ref[...])

    xmid, h2b, lgt = _ffn_pre(x1, (mod[3:4], mod[4:5], mod[5:6]), gffn_ref[...],
                              w_sgu_ref[...], w_sd_ref[...], wr_hl_ref[...])
    xmid_ref[0] = xmid
    _pack_rows(h2b, h2_ref)
    lgt_ref[...] = lgt

    @pl.when(t == nt - 1)
    def _():
        for hd in range(HEADS):
            s_out_ref[0, hd] = st_ref[hd].T
        cv_out_ref[0] = cbuf_ref[...]


def _const_spec(shape):
    nd = len(shape)
    return pl.BlockSpec(shape, lambda i, _nd=nd: (0,) * _nd, pipeline_mode=pl.Buffered(1))


def _mix(x, mod, n_tok, gmix, gffn, w_in, lbl, hg, cw, cb, w_oh, w_oc, w_o, wr_hl, w_sgu, w_sd):
    bsz, seq, _ = x.shape
    tt = MIX_TILE
    nt = seq // tt
    n_tiles = bsz * nt
    assert n_tiles * tt < n_tok <= (n_tiles + 1) * tt
    consts = [gmix, gffn, w_in, lbl, hg, cw, cb, w_oh, w_oc, w_o, wr_hl, w_sgu, w_sd]
    tile = lambda i: jnp.minimum(i, n_tiles - 1)
    return pl.pallas_call(
        functools.partial(_mix_kernel, nt=nt, n_tiles=n_tiles),
        grid=(n_tiles + 1,),
        in_specs=[pl.BlockSpec((1, tt, D_MODEL), lambda i: (tile(i) // nt, tile(i) % nt, 0)),
                  pl.BlockSpec((1, 6, D_MODEL), lambda i: (tile(i) // nt, 0, 0))]
                 + [_const_spec(a.shape) for a in consts],
        out_specs=[pl.BlockSpec((1, tt, D_MODEL), lambda i: (tile(i) // nt, tile(i) % nt, 0)),
                   pl.BlockSpec((2, tt, PLANE_W), lambda i: (0, i, 0)),
                   pl.BlockSpec((N_EXP, tt), lambda i: (0, i)),
                   pl.BlockSpec((1, HEADS, DK, DK), lambda i: (tile(i) // nt, 0, 0, 0)),
                   pl.BlockSpec((1, CONV_K - 1, CONV_W), lambda i: (tile(i) // nt, 0, 0))],
        out_shape=[jax.ShapeDtypeStruct((bsz, seq, D_MODEL), F32),
                   jax.ShapeDtypeStruct((2, n_tok, PLANE_W), I32),
                   jax.ShapeDtypeStruct((N_EXP, n_tok), F32),
                   jax.ShapeDtypeStruct((bsz, HEADS, DK, DK), F32),
                   jax.ShapeDtypeStruct((bsz, CONV_K - 1, CONV_W), F32)],
        scratch_shapes=[pltpu.VMEM((tt, IN_W), F32),
                        pltpu.VMEM((HEADS, DK, DK), F32),
                        pltpu.VMEM((CONV_K - 1, CONV_W), F32),
                        pltpu.VMEM((tt, KEY_W), F32)],
        compiler_params=pltpu.CompilerParams(
            dimension_semantics=("arbitrary",), vmem_limit_bytes=VMEM_LIMIT),
        name="mix",
    )(x, mod, *consts)


def _smp1_kernel(x_ref, mod_ref, gmix_ref, w_in_ref, lbl_ref, cw_ref, cb_ref, cst_ref,
                 f_ref, k_ref, q_ref, v_ref, gate_ref, yb_ref, sga_ref, sgb_ref, cv_out_ref):
    x = x_ref[...]
    sh1, sc1 = mod_ref[:, 0:D_MODEL], mod_ref[:, D_MODEL:2 * D_MODEL]
    h = _rms(x) * gmix_ref[...] * (1.0 + sc1) + sh1
    hb = h.astype(BF16)

    def proj(c, w):
        return _dot(hb, w_in_ref[:, c:c + w])

    lb = _lower_bound(lbl_ref[...])
    f = lb + (1.0 - lb) * _sigmoid(proj(C_F, KEY_W))
    f_ref[...] = f
    k_ref[...] = 1.0 - f
    q_ref[...] = _silu(proj(C_Q, KEY_W))
    v_ref[...] = proj(C_I, KEY_W)
    gate_ref[...] = _silu(proj(C_G, KEY_W))
    u = proj(C_CC, CONV_W) * proj(C_VB, CONV_W)
    c0, c1 = cst_ref[:, 0:CONV_W], cst_ref[:, CONV_W:2 * CONV_W]
    conv = cw_ref[0:1] * c0 + cw_ref[1:2] * c1 + cw_ref[2:3] * u + cb_ref[...]
    yb_ref[...] = proj(C_BB, CONV_W) * conv
    cv_out_ref[:, 0:CONV_W] = c1
    cv_out_ref[:, CONV_W:2 * CONV_W] = u
    sga_ref[...] = _sigmoid(proj(C_MGA, D_MODEL))
    sgb_ref[...] = _sigmoid(proj(C_MGB, D_MODEL))


def _smp1(x, mod, gmix, w_in, lbl, cw, cb, cst):
    n = x.shape[0]
    kw = jax.ShapeDtypeStruct((n, KEY_W), F32)
    dm = jax.ShapeDtypeStruct((n, D_MODEL), F32)
    return pl.pallas_call(
        _smp1_kernel,
        out_shape=[kw, kw, kw, kw, kw, kw, dm, dm,
                   jax.ShapeDtypeStruct((n, (CONV_K - 1) * CONV_W), F32)],
        compiler_params=pltpu.CompilerParams(vmem_limit_bytes=VMEM_LIMIT),
        name="smp1",
    )(x, mod, gmix, w_in, lbl, cw, cb, cst)


def _smp2_kernel(f_ref, k_ref, q_ref, v_ref, s_ref, s_out_ref, o_ref):
    g = f_ref.shape[0]
    for i in range(g):
        for hd in range(HEADS):
            hs = slice(hd * DK, (hd + 1) * DK)

            def col(ref):
                return jnp.broadcast_to(ref[i:i + 1, hs], (DK, DK)).T

            vrow = v_ref[i:i + 1, hs]
            s_new = col(f_ref) * s_ref[i, hd] + col(k_ref) * vrow
            s_out_ref[i, hd] = s_new
            o_ref[i:i + 1, hs] = jnp.sum(col(q_ref) * s_new, axis=0, keepdims=True)


def _smp2(f, k, q, v, state):
    n = f.shape[0]
    g = SMP_GROUP
    row_spec = pl.BlockSpec((g, KEY_W), lambda i: (i, 0))
    st_spec = pl.BlockSpec((g, HEADS, DK, DK), lambda i: (i, 0, 0, 0))
    return pl.pallas_call(
        _smp2_kernel,
        grid=(n // g,),
        in_specs=[row_spec, row_spec, row_spec, row_spec, st_spec],
        out_specs=[st_spec, row_spec],
        out_shape=[jax.ShapeDtypeStruct(state.shape, F32), jax.ShapeDtypeStruct((n, KEY_W), F32)],
        compiler_params=pltpu.CompilerParams(dimension_semantics=("arbitrary",)),
        name="smp2",
    )(f, k, q, v, state)


def _smp3_kernel(x_ref, mod_ref, o_ref, gate_ref, yb_ref, sga_ref, sgb_ref, hg_ref, gffn_ref,
                 w_oh_ref, w_oc_ref, w_o_ref, wr_hl_ref, w_sgu_ref, w_sd_ref,
                 h2_all_ref, lgt_all_ref, xmid_ref, h2_ref, lgt_ref):
    del h2_all_ref, lgt_all_ref
    parts = []
    for hd in range(HEADS):
        hs = slice(hd * DK, (hd + 1) * DK)
        parts.append(_rms(o_ref[:, hs]) * hg_ref[:, hs] * gate_ref[:, hs])
    ya = jnp.concatenate(parts, axis=1)
    mixed = (sga_ref[...] * _dot(ya.astype(BF16), w_oh_ref[...])
             + sgb_ref[...] * _dot(yb_ref[...].astype(BF16), w_oc_ref[...]))
    g1 = mod_ref[:, 2 * D_MODEL:3 * D_MODEL]
    x1 = x_ref[...] + g1 * _dot(mixed.astype(BF16), w_o_ref[...])
    mod_rows = tuple(mod_ref[:, j * D_MODEL:(j + 1) * D_MODEL] for j in (3, 4, 5))
    xmid, h2b, lgt = _ffn_pre(x1, mod_rows, gffn_ref[...], w_sgu_ref[...], w_sd_ref[...],
                              wr_hl_ref[...])
    xmid_ref[...] = xmid
    _pack_rows(h2b, h2_ref)
    lgt_ref[...] = lgt


def _smp3(x, mod, o, gate, yb, sga, sgb, hg, gffn, w_oh, w_oc, w_o, wr_hl, w_sgu, w_sd,
          h2_all, lgt_all, n_prompt):
    n = x.shape[0]
    vmem_args = [x, mod, o, gate, yb, sga, sgb, hg, gffn, w_oh, w_oc, w_o, wr_hl, w_sgu, w_sd]
    blk = n_prompt // n

    def full(a):
        nd = a.ndim
        return pl.BlockSpec(a.shape, lambda i, _nd=nd: (0,) * _nd)

    return pl.pallas_call(
        _smp3_kernel,
        grid=(1,),
        in_specs=[full(a) for a in vmem_args]
                 + [pl.BlockSpec(memory_space=pl.ANY), pl.BlockSpec(memory_space=pl.ANY)],
        out_specs=[pl.BlockSpec((n, D_MODEL), lambda i: (0, 0)),
                   pl.BlockSpec((2, n, PLANE_W), lambda i: (0, blk, 0)),
                   pl.BlockSpec((N_EXP, n), lambda i: (0, blk))],
        out_shape=[jax.ShapeDtypeStruct((n, D_MODEL), F32),
                   jax.ShapeDtypeStruct(h2_all.shape, h2_all.dtype),
                   jax.ShapeDtypeStruct(lgt_all.shape, lgt_all.dtype)],
        input_output_aliases={len(vmem_args): 1, len(vmem_args) + 1: 2},
        compiler_params=pltpu.CompilerParams(
            dimension_semantics=("arbitrary",), vmem_limit_bytes=VMEM_LIMIT),
        name="smp3",
    )(*vmem_args, h2_all, lgt_all)


def _route_kernel(lgt_ref, bias_ref, idx_ref, w_ref, rank_ref, cnt_ref):
    tr = ROUTE_TILE
    n_tiles = lgt_ref.shape[1] // tr

    def tile(i, carry):
        cols = pl.ds(pl.multiple_of(i * tr, tr), tr)
        picks, weights, ranks, carry = _route_tile(lgt_ref[:, cols], bias_ref[...], carry)
        for k in range(TOP_K):
            idx_ref[k:k + 1, cols] = picks[k]
            w_ref[k:k + 1, cols] = weights[k]
            rank_ref[k:k + 1, cols] = ranks[k]
        return carry

    total = lax.fori_loop(0, n_tiles, tile, jnp.zeros((N_EXP, 1), F32))
    cnt_ref[...] = jnp.broadcast_to(total, cnt_ref.shape).astype(I32)


def _route_tile(lgt, bias, carry):
    tr = lgt.shape[1]
    neg = -jnp.inf
    scores = _sigmoid(lgt)
    sel = scores + bias
    j8 = lax.broadcasted_iota(I32, (GRP_SZ, tr), 0)
    groups = [sel[g * GRP_SZ:(g + 1) * GRP_SZ] for g in range(N_GRP)]
    gscore = []
    for grp in groups:
        m1 = jnp.max(grp, axis=0, keepdims=True)
        i1 = jnp.min(jnp.where(grp == m1, j8, GRP_SZ), axis=0, keepdims=True)
        m2 = jnp.max(jnp.where(j8 == i1, neg, grp), axis=0, keepdims=True)
        gscore.append(m1 + m2)
    kept = []
    for g in range(N_GRP):
        beaten = jnp.zeros((1, tr), I32)
        for o in range(N_GRP):
            if o < g:
                beaten = beaten + (gscore[o] >= gscore[g]).astype(I32)
            elif o > g:
                beaten = beaten + (gscore[o] > gscore[g]).astype(I32)
        kept.append(jnp.where(beaten < TOPK_GRP, groups[g], neg))
    masked = jnp.concatenate(kept, axis=0)
    ei = lax.broadcasted_iota(I32, masked.shape, 0)
    chosen = jnp.zeros(masked.shape, jnp.bool_)
    picks, weights = [], []
    for _ in range(TOP_K):
        m = jnp.max(masked, axis=0, keepdims=True)
        pick = jnp.min(jnp.where(masked == m, ei, N_EXP), axis=0, keepdims=True)
        hit = ei == pick
        weights.append(jnp.sum(jnp.where(hit, scores, 0.0), axis=0, keepdims=True))
        picks.append(pick)
        chosen = chosen | hit
        masked = jnp.where(hit, neg, masked)
    wsum = weights[0]
    for w in weights[1:]:
        wsum = wsum + w
    sel01 = chosen.astype(F32)
    r = lax.broadcasted_iota(I32, (tr, tr), 0)
    c = lax.broadcasted_iota(I32, (tr, tr), 1)
    before = (r < c).astype(BF16)
    cnt = _dot(sel01.astype(BF16), before) + carry
    weights = [w / wsum * ROUTED_SCALE for w in weights]
    ranks = [jnp.sum(jnp.where(ei == p, cnt, 0.0), axis=0, keepdims=True).astype(I32) for p in picks]
    return picks, weights, ranks, carry + jnp.sum(sel01, axis=1, keepdims=True)


def _route(lgt, bias):
    n = lgt.shape[1]
    assert n % ROUTE_TILE == 0
    slot = lambda dt: jax.ShapeDtypeStruct((TOP_K, n), dt)
    return pl.pallas_call(
        _route_kernel,
        out_shape=[slot(I32), slot(F32), slot(I32),
                   jax.ShapeDtypeStruct((N_EXP, 128), I32)],
        name="route",
    )(lgt, bias)


def _dest_kernel(start_ref, idx_ref, rank_ref, all_ref, *chunk_refs, n_rows, chunks):
    n_tok = idx_ref.shape[1]
    idx = idx_ref[...]
    acc = rank_ref[...]
    for e in range(N_EXP):
        acc = acc + jnp.where(idx == e, start_ref[e], 0)
    for k in range(TOP_K):
        for p in range(2):
            row = acc[k:k + 1, :] + p * n_rows
            seg = 2 * k + p
            all_ref[:, seg * n_tok:(seg + 1) * n_tok] = row
            for (c0, c1), ref in zip(chunks, chunk_refs):
                ref[:, seg * (c1 - c0):(seg + 1) * (c1 - c0)] = row[:, c0:c1]


def _dest(pad_start, idx, rank, n_rows, chunks):
    k, n_tok = idx.shape
    vmem = pl.BlockSpec(memory_space=pltpu.VMEM)
    sizes = [n_tok] + [c1 - c0 for c0, c1 in chunks]
    return pl.pallas_call(
        functools.partial(_dest_kernel, n_rows=n_rows, chunks=chunks),
        in_specs=[pl.BlockSpec(memory_space=pltpu.SMEM), vmem, vmem],
        out_specs=[vmem] * len(sizes),
        out_shape=[jax.ShapeDtypeStruct((1, 2 * k * n), I32) for n in sizes],
        name="dest",
    )(pad_start, idx, rank)


def _sc_mesh():
    return plsc.VectorSubcoreMesh(core_axis_name="core", subcore_axis_name="subcore")


def _dispatch(rows, dest, n_out):
    n, width = rows.shape
    win = SC_WINDOW
    steps = n // win

    @pl.kernel(out_type=jax.ShapeDtypeStruct((n_out, width), rows.dtype), mesh=_sc_mesh(),
               scratch_types=[], name="dispatch")
    def run(x_hbm, *refs):
        i_hbms, o_hbm = refs[:TOP_K], refs[TOP_K]

        def body(x_vmem, *i_vmems):
            for i_vmem in i_vmems:
                pltpu.sync_copy(x_vmem, o_hbm.at[i_vmem.at[0]])

        pltpu.emit_pipeline(
            body,
            grid=(steps,),
            in_specs=[pl.BlockSpec((win, width), lambda i: (i, 0))]
                     + [pl.BlockSpec((1, win), lambda i, k=k: (0, k * steps + i)) for k in range(TOP_K)],
            out_specs=[],
            core_axis_name=("core", "subcore"),
            dimension_semantics=(pltpu.PARALLEL,),
        )(x_hbm, *i_hbms)

    assert dest.shape == (1, TOP_K * n)
    return run(rows, *([dest] * TOP_K))


def _combine(rows, dest_flat):
    width = rows.shape[1]
    n = dest_flat.shape[1]
    win = SC_WINDOW

    @pl.kernel(out_type=jax.ShapeDtypeStruct((n, width), rows.dtype), mesh=_sc_mesh(),
               scratch_types=[], name="combine")
    def run(y_hbm, i_hbm, o_hbm):
        def body(i_vmem, o_vmem):
            pltpu.sync_copy(y_hbm.at[i_vmem.at[0]], o_vmem)

        pltpu.emit_pipeline(
            body,
            grid=(n // win,),
            in_specs=[pl.BlockSpec((1, win), lambda i: (0, i))],
            out_specs=[pl.BlockSpec((win, width), lambda i: (i, 0))],
            core_axis_name=("core", "subcore"),
            dimension_semantics=(pltpu.PARALLEL,),
        )(i_hbm, o_hbm)

    return run(rows, dest_flat)


def _gmm_kernel(blk_exp_ref, n_used_ref, xs_hbm, wg_hbm, wu_hbm, wd_hbm, ys_hbm,
                xbuf, ybuf, wg32, wu32, wd32, wgu_b, wd_b, xsem, ysem, wsem, run_ref):
    nx, ny, bm = xbuf.shape[0], ybuf.shape[0], xbuf.shape[2]
    n_used = n_used_ref[0]

    def x_copies(b):
        rows, slot = pl.ds(b * bm, bm), b % nx
        return [pltpu.make_async_copy(xs_hbm.at[p, rows, :], xbuf.at[slot, p], xsem.at[slot, p]) for p in range(2)]

    def y_copies(b):
        rows, slot = pl.ds(b * bm, bm), b % ny
        return [pltpu.make_async_copy(ybuf.at[slot, p], ys_hbm.at[p, rows, :], ysem.at[slot, p]) for p in range(2)]

    def start(copies):
        for c in copies:
            c.start()

    def wait(copies):
        for c in copies:
            c.wait()

    def w_copies(e, slot):
        return (pltpu.make_async_copy(wg_hbm.at[e], wg32.at[slot], wsem.at[slot, 0]),
                pltpu.make_async_copy(wu_hbm.at[e], wu32.at[slot], wsem.at[slot, 1]),
                pltpu.make_async_copy(wd_hbm.at[e], wd32.at[slot], wsem.at[slot, 2]))

    def run_end(b):
        return lax.while_loop(lambda j: (j < n_used) & (blk_exp_ref[jnp.minimum(j, n_used - 1)] == blk_exp_ref[b]),
                              lambda j: j + 1, b + 1)

    run_ref[0] = 0
    start(x_copies(0))
    start(w_copies(blk_exp_ref[0], 0))
    for j in range(1, nx - 1):
        @pl.when(j < n_used)
        def _():
            start(x_copies(j))

    def block(b, carry):
        @pl.when(b + nx - 1 < n_used)
        def _():
            start(x_copies(b + nx - 1))

        @pl.when((b == 0) | (blk_exp_ref[b] != blk_exp_ref[jnp.maximum(b - 1, 0)]))
        def _():
            wslot = run_ref[0] % 2
            run_ref[0] = run_ref[0] + 1
            wait(w_copies(blk_exp_ref[b], wslot))
            wgu_b[:, 0:EXP_FF] = wg32[wslot].astype(BF16)
            wgu_b[:, EXP_FF:2 * EXP_FF] = wu32[wslot].astype(BF16)
            wd_b[...] = wd32[wslot].astype(BF16)
            nxt = run_end(b)

            @pl.when(nxt < n_used)
            def _():
                start(w_copies(blk_exp_ref[jnp.minimum(nxt, n_used - 1)], 1 - wslot))

        wait(x_copies(b))

        @pl.when(b >= ny)
        def _():
            wait(y_copies(b - ny))

        xslot = b % nx
        xc = _unpack_rows(xbuf[xslot, 0], xbuf[xslot, 1])
        gu = sum(_dot(c, wgu_b[i * PLANE_W:(i + 1) * PLANE_W, :]) for i, c in enumerate(xc))
        act = (_silu(gu[:, :EXP_FF]) * gu[:, EXP_FF:]).astype(BF16)
        _pack_rows(_dot(act, wd_b[...]).astype(BF16), ybuf.at[b % ny])
        start(y_copies(b))
        return carry

    lax.fori_loop(0, n_used, block, 0)

    for j in range(ny, 0, -1):
        @pl.when(n_used >= j)
        def _():
            wait(y_copies(n_used - j))


def _gmm(blk_exp, n_used, xs, w_gate, w_up, w_down):
    n_rows = xs.shape[1]
    bm = GMM_BM
    nb = n_rows // bm

    assert blk_exp.shape == (nb,)
    any_spec = pl.BlockSpec(memory_space=pl.ANY)
    grid_spec = pltpu.PrefetchScalarGridSpec(
        num_scalar_prefetch=2,
        grid=(1,),
        in_specs=[any_spec, any_spec, any_spec, any_spec],
        out_specs=any_spec,
        scratch_shapes=[pltpu.VMEM((GMM_NX, 2, bm, PLANE_W), I32), pltpu.VMEM((GMM_NY, 2, bm, PLANE_W), I32),
                        pltpu.VMEM((2, D_MODEL, EXP_FF), F32), pltpu.VMEM((2, D_MODEL, EXP_FF), F32),
                        pltpu.VMEM((2, EXP_FF, D_MODEL), F32),
                        pltpu.VMEM((D_MODEL, 2 * EXP_FF), BF16), pltpu.VMEM((EXP_FF, D_MODEL), BF16),
                        pltpu.SemaphoreType.DMA((GMM_NX, 2)), pltpu.SemaphoreType.DMA((GMM_NY, 2)),
                        pltpu.SemaphoreType.DMA((2, 3)), pltpu.SMEM((1,), I32)],
    )
    return pl.pallas_call(
        _gmm_kernel,
        grid_spec=grid_spec,
        out_shape=jax.ShapeDtypeStruct((2, n_rows, PLANE_W), I32),
        compiler_params=pltpu.CompilerParams(dimension_semantics=("arbitrary",)),
        name="gmm",
    )(blk_exp, n_used, xs, w_gate, w_up, w_down)


def _final_kernel(xmid_ref, g2_ref, z_ref, w_ref, gfin_ref, *rest):
    y_ref = rest[-1]
    accs = [jnp.zeros((xmid_ref.shape[0], PLANE_W), F32) for _ in range(4)]
    w_cols = w_ref[...].T
    for k in range(TOP_K):
        wk = w_cols[:, k:k + 1]
        cols = _unpack_rows(z_ref[k, 0], z_ref[k, 1])
        accs = [a + wk * c.astype(F32) for a, c in zip(accs, cols)]
    acc = jnp.concatenate(accs, axis=1)
    y_ref[...] = _rms(xmid_ref[...] + g2_ref[0] * acc) * gfin_ref[...]


def _final(xmid, g2, z, w_t, gfin, tile, *, n_tiles, x_tile0, z_tile0, w_tile0, tiles_per_g2, y_prev=None):
    args = [xmid, g2, z, w_t, gfin]
    in_specs = [pl.BlockSpec((tile, D_MODEL), lambda i: (x_tile0 + i, 0)),
                pl.BlockSpec((1, g2.shape[1], D_MODEL), lambda i: ((x_tile0 + i) // tiles_per_g2, 0, 0)),
                pl.BlockSpec((TOP_K, 2, tile, PLANE_W), lambda i: (0, 0, z_tile0 + i, 0)),
                pl.BlockSpec((TOP_K, tile), lambda i: (0, w_tile0 + i)),
                pl.BlockSpec((1, D_MODEL), lambda i: (0, 0))]
    aliases = {}
    if y_prev is not None:
        args.append(y_prev)
        in_specs.append(pl.BlockSpec(memory_space=pl.ANY))
        aliases = {len(args) - 1: 0}
    return pl.pallas_call(
        _final_kernel,
        grid=(n_tiles,),
        in_specs=in_specs,
        out_specs=pl.BlockSpec((tile, D_MODEL), lambda i: (x_tile0 + i, 0)),
        out_shape=jax.ShapeDtypeStruct(xmid.shape, F32),
        input_output_aliases=aliases,
        compiler_params=pltpu.CompilerParams(dimension_semantics=("arbitrary",)),
        name="final",
    )(*args)


def kernel(x_prompt, x_sample, state_hgrn, state_conv, c_prompt, c_sample, w_ada, b_ada, norm_mix_g, norm_ffn_g, w_in, lb_logits, hgrn_norm_g, conv_w, conv_b, w_out_hgrn, w_out_conv, w_o, w_router, router_bias, w_exp_gate, w_exp_up, w_exp_down, w_sh_gate, w_sh_up, w_sh_down, final_norm_g):
    assert w_ada.shape[0] == 1 and lb_logits.shape[0] == 2
    bsz, seq, _ = x_prompt.shape
    n_smp = x_sample.shape[0]
    n_prompt = bsz * seq
    n_tok = n_prompt + n_smp

    w_in_b = w_in[0].astype(BF16)
    w_oh_b = w_out_hgrn[0].astype(BF16)
    w_oc_b = w_out_conv[0].astype(BF16)
    w_o_b = w_o[0].astype(BF16)
    wr_t = w_router[0].T
    wr_hi = wr_t.astype(BF16)
    wr_hl = jnp.concatenate([wr_hi, (wr_t - wr_hi.astype(F32)).astype(BF16)], axis=0)
    w_sgu = jnp.concatenate([w_sh_gate[0], w_sh_up[0]], axis=1).astype(BF16)
    w_sd = w_sh_down[0].astype(BF16)
    gmix = norm_mix_g[0].reshape(1, D_MODEL)
    gffn = norm_ffn_g[0].reshape(1, D_MODEL)
    hg = hgrn_norm_g[0].reshape(1, KEY_W)
    cw = conv_w[0]
    cb = conv_b[0].reshape(1, CONV_W)
    gfin = final_norm_g.reshape(1, D_MODEL)

    mod_p, mod_s = _ada(c_prompt, c_sample, w_ada[0], b_ada[0])

    xmid_p, h2_all, lgt_all, s_p, cv_p = _mix(
        x_prompt, mod_p.reshape(bsz, 6, D_MODEL), n_tok, gmix, gffn, w_in_b, lb_logits, hg, cw, cb, w_oh_b, w_oc_b, w_o_b, wr_hl, w_sgu, w_sd)

    xs2 = x_sample.reshape(n_smp, D_MODEL)
    f, kk, q, v, gate, yb, sga, sgb, cv_s = _smp1(
        xs2, mod_s, gmix, w_in_b, lb_logits, cw, cb, state_conv[0].reshape(n_smp, (CONV_K - 1) * CONV_W))
    s_s, o_s = _smp2(f, kk, q, v, state_hgrn[0])
    xmid_s, h2_all, lgt_all = _smp3(xs2, mod_s, o_s, gate, yb, sga, sgb, hg, gffn,
                                    w_oh_b, w_oc_b, w_o_b, wr_hl, w_sgu, w_sd,
                                    h2_all, lgt_all, n_prompt)

    idx, w_tok, rank, cnt = _route(lgt_all, router_bias[0].reshape(N_EXP, 1))

    bm = GMM_BM
    n_blocks = (n_tok * TOP_K + N_EXP * (bm - 1)) // bm
    n_rows = n_blocks * bm
    counts = cnt[:, 0]
    padded = (counts + bm - 1) // bm * bm
    pad_end = jnp.cumsum(padded)
    pad_start = pad_end - padded
    blk_row0 = jnp.arange(n_blocks, dtype=I32) * bm
    blk_exp = jnp.minimum(jnp.sum((pad_end[None, :] <= blk_row0[:, None]).astype(I32), axis=1), N_EXP - 1)
    n_used = (pad_end[-1:] // bm).astype(I32)
    chunk = n_prompt // FINAL_CHUNKS
    chunks = tuple((c * chunk, n_tok if c == FINAL_CHUNKS - 1 else (c + 1) * chunk) for c in range(FINAL_CHUNKS))
    dest, *chunk_dest = _dest(pad_start.astype(I32), idx, rank, n_rows, chunks)

    xs = _dispatch(h2_all.reshape(2 * n_tok, PLANE_W), dest, 2 * n_rows).reshape(2, n_rows, PLANE_W)
    ys = _gmm(blk_exp, n_used, xs, w_exp_gate[0], w_exp_up[0], w_exp_down[0])

    ys_flat = ys.reshape(2 * n_rows, PLANE_W)
    w_t = w_tok
    xmid_p2 = xmid_p.reshape(n_prompt, D_MODEL)
    g2_p = mod_p[:, 5 * D_MODEL:].reshape(bsz, 1, D_MODEL)
    g2_s = mod_s[:, 5 * D_MODEL:].reshape(1, n_smp, D_MODEL)
    y_p = None
    for (c0, c1), dest_c in zip(chunks, chunk_dest):
        z = _combine(ys_flat, dest_c).reshape(TOP_K, 2, c1 - c0, PLANE_W)
        y_p = _final(xmid_p2, g2_p, z, w_t, gfin, FINAL_TILE, n_tiles=chunk // FINAL_TILE,
                     x_tile0=c0 // FINAL_TILE, z_tile0=0, w_tile0=c0 // FINAL_TILE,
                     tiles_per_g2=seq // FINAL_TILE, y_prev=y_p)
    y_s = _final(xmid_s, g2_s, z, w_t, gfin, n_smp, n_tiles=1, x_tile0=0, z_tile0=chunk // n_smp,
                 w_tile0=n_prompt // n_smp, tiles_per_g2=1)

    return (y_p.reshape(bsz, seq, D_MODEL), y_s.reshape(n_smp, 1, D_MODEL),
            s_p[None], cv_p[None], s_s[None], cv_s.reshape(1, n_smp, CONV_K - 1, CONV_W))
```

```python
import functools

import jax
import jax.numpy as jnp
from jax import lax
from jax.experimental import pallas as pl
from jax.experimental.pallas import tpu as pltpu
from jax.experimental.pallas import tpu_sc as plsc

F32 = jnp.float32
BF16 = jnp.bfloat16
I32 = jnp.int32

D_MODEL = 1024
HALF_D = D_MODEL // 2
HEADS = 4
DK = 128
KEY_W = HEADS * DK
CONV_W = 512
CONV_K = 3
IN_W = 2 * KEY_W + 2 * KEY_W + 3 * CONV_W + 2 * D_MODEL
N_EXP = 64
TOP_K = 8
N_GRP = 8
GRP_SZ = N_EXP // N_GRP
TOPK_GRP = 4
EXP_FF = 256
SH_FF = 256
ROUTED_SCALE = 2.5
EPS = 1e-6

C_Q, C_F, C_I, C_G = 0, 512, 1024, 1536
C_BB, C_CC, C_VB = 2048, 2560, 3072
C_MGA, C_MGB = 3584, 4608

MIX_TILE = 512
SUB = 256
CHUNK = 64
ROUTE_TILE = 384
GMM_BM = 512
GMM_NX = 4
GMM_NY = 3
FINAL_TILE = 512
FINAL_CHUNKS = 4
SMP_GROUP = 16
SC_WINDOW = 128
PLANE_W = HALF_D // 2
VMEM_LIMIT = 56 * 1024 * 1024


def _dot(a, b):
    return jnp.dot(a, b, preferred_element_type=F32)


def _dot_nt(a, b):
    return lax.dot_general(a, b, (((1,), (1,)), ((), ())), preferred_element_type=F32)


def _dot_tn(a, b):
    return lax.dot_general(a, b, (((0,), (0,)), ((), ())), preferred_element_type=F32)


def _sigmoid(x):
    return 0.5 * jnp.tanh(0.5 * x) + 0.5


def _silu(x):
    h = 0.5 * x
    return h * jnp.tanh(h) + h


def _rms(x):
    return x * lax.rsqrt(jnp.mean(x * x, axis=-1, keepdims=True) + EPS)


def _lower_bound(lbl):
    a, b = lbl[0:1], lbl[1:2]
    m = jnp.maximum(a, b)
    ea, eb = jnp.exp(a - m), jnp.exp(b - m)
    return ea / (ea + eb)


def _split3(x):
    hi = x.astype(BF16)
    r1 = x - hi.astype(F32)
    mid = r1.astype(BF16)
    lo = (r1 - mid.astype(F32)).astype(BF16)
    return hi, mid, lo


def _words(lo_b, hi_b):
    lo = lax.shift_right_logical(lax.bitcast_convert_type(lo_b.astype(F32), I32), 16)
    hi = lax.bitcast_convert_type(hi_b.astype(F32), I32) & jnp.int32(-65536)
    return lo | hi


def _halves(w):
    lo = lax.bitcast_convert_type(lax.shift_left(w, 16), F32)
    hi = lax.bitcast_convert_type(w & jnp.int32(-65536), F32)
    return lo.astype(BF16), hi.astype(BF16)


def _pack_rows(xb, out_ref):
    words = _words(xb[:, :HALF_D], xb[:, HALF_D:])
    out_ref[0] = words[:, :PLANE_W]
    out_ref[1] = words[:, PLANE_W:]


def _unpack_rows(p0, p1):
    c0, c2 = _halves(p0)
    c1, c3 = _halves(p1)
    return c0, c1, c2, c3


def _ada_kernel(cp_ref, cs_ref, w_ref, b_ref, op_ref, os_ref):
    w = w_ref[...].astype(BF16)
    for c_ref, o_ref in ((cp_ref, op_ref), (cs_ref, os_ref)):
        o_ref[...] = _dot(_silu(c_ref[...]).astype(BF16), w) + b_ref[...]


def _ada(c_prompt, c_sample, w_ada, b_ada):
    blk = 1024
    rows = lambda c: pl.BlockSpec((c.shape[0], D_MODEL), lambda j: (0, 0))
    cols = lambda c: pl.BlockSpec((c.shape[0], blk), lambda j: (0, j))
    return pl.pallas_call(
        _ada_kernel,
        grid=(6 * D_MODEL // blk,),
        in_specs=[rows(c_prompt), rows(c_sample),
                  pl.BlockSpec((D_MODEL, blk), lambda j: (0, j)),
                  pl.BlockSpec((1, blk), lambda j: (0, j))],
        out_specs=[cols(c_prompt), cols(c_sample)],
        out_shape=[jax.ShapeDtypeStruct((c.shape[0], 6 * D_MODEL), F32) for c in (c_prompt, c_sample)],
        name="ada",
    )(c_prompt, c_sample, w_ada, b_ada.reshape(1, -1))


def _ffn_pre(x1, mod_rows, gffn, w_sgu, w_sd, wr_hl):
    sh2, sc2, g2 = mod_rows
    h2 = _rms(x1) * gffn * (1.0 + sc2) + sh2
    h2b = h2.astype(BF16)
    gu = _dot(h2b, w_sgu)
    act = _silu(gu[:, :SH_FF]) * gu[:, SH_FF:]
    xmid = x1 + g2 * _dot(act.astype(BF16), w_sd)
    h2lo = (h2 - h2b.astype(F32)).astype(BF16)
    both = _dot_nt(wr_hl, h2b)
    lgt = both[:N_EXP] + both[N_EXP:] + _dot_nt(wr_hl[:N_EXP], h2lo)
    return xmid, h2b, lgt


def _mix_kernel(*refs, nt, n_tiles):
    i = pl.program_id(0)
    h2_ref, lgt_ref = refs[16], refs[17]

    @pl.when(i == n_tiles)
    def _():
        h2_ref[...] = jnp.zeros_like(h2_ref)
        lgt_ref[...] = jnp.zeros_like(lgt_ref)

    @pl.when(i < n_tiles)
    def _():
        _mix_tile(i % nt, nt, *refs)


def _mix_tile(t, nt, x_ref, mod_ref, gmix_ref, gffn_ref, w_in_ref, lbl_ref, hg_ref, cw_ref, cb_ref,
              w_oh_ref, w_oc_ref, w_o_ref, wr_hl_ref, w_sgu_ref, w_sd_ref,
              xmid_ref, h2_ref, lgt_ref, s_out_ref, cv_out_ref,
              proj_ref, st_ref, cbuf_ref, ya_ref):
    tt = x_ref.shape[1]

    @pl.when(t == 0)
    def _():
        st_ref[...] = jnp.zeros_like(st_ref)
        cbuf_ref[...] = jnp.zeros_like(cbuf_ref)

    x = x_ref[0]
    mod = mod_ref[0]
    sh1, sc1, g1 = mod[0:1], mod[1:2], mod[2:3]
    h = _rms(x) * gmix_ref[...] * (1.0 + sc1) + sh1
    hb = h.astype(BF16)
    for c in range(0, IN_W, 512):
        proj_ref[:, c:c + 512] = _dot(hb, w_in_ref[:, c:c + 512])

    lb = _lower_bound(lbl_ref[...])
    row = lax.broadcasted_iota(I32, (SUB, SUB), 0)
    col = lax.broadcasted_iota(I32, (SUB, SUB), 1)
    tri = (col <= row).astype(BF16)
    mask_d = (col <= row) & (row // CHUNK == col // CHUNK)
    mask_a = row // (2 * CHUNK) == col // (2 * CHUNK)
    n_ch = SUB // CHUNK

    def by_chunk(vals):
        return jnp.concatenate([jnp.zeros((CHUNK, DK), F32) if v is None
                                else jnp.broadcast_to(v, (CHUNK, DK)) for v in vals], axis=0)

    for s in range(tt // SUB):
        r0 = s * SUB
        f = lb + (1.0 - lb) * _sigmoid(proj_ref[r0:r0 + SUB, C_F:C_F + KEY_W])
        kk = 1.0 - f
        hi, mid, lo = _split3(jnp.log(f))
        bc = _dot(tri, hi) + _dot(tri, mid) + _dot(tri, lo)
        for hd in range(HEADS):
            hs = slice(hd * DK, (hd + 1) * DK)
            bh = bc[:, hs]
            at = lambda r: bh[r:r + 1]
            mids = [at(c * CHUNK + CHUNK // 2 - 1) for c in range(n_ch)]
            pair_mid = [at(CHUNK - 1), at(3 * CHUNK - 1)]
            step_mid, step_end = at(2 * CHUNK - 1), at(SUB - 1)
            arg = bh - by_chunk(mids)
            e_pos, e_neg = jnp.exp(arg), jnp.exp(-arg)
            q = _silu(proj_ref[r0:r0 + SUB, C_Q + hd * DK:C_Q + (hd + 1) * DK])
            v = proj_ref[r0:r0 + SUB, C_I + hd * DK:C_I + (hd + 1) * DK]
            qd = q * e_pos
            kd = kk[:, hs] * e_neg
            q_in = qd * by_chunk([jnp.exp(m) for m in mids])
            k_end = kd * by_chunk([jnp.exp(step_end - m) for m in mids])
            qa = qd * by_chunk([None, jnp.exp(mids[1] - pair_mid[0]), None, jnp.exp(mids[3] - pair_mid[1])])
            ka = kd * by_chunk([jnp.exp(pair_mid[0] - mids[0]), None, jnp.exp(pair_mid[1] - mids[2]), None])
            qb = qd * by_chunk([None, None, jnp.exp(mids[2] - step_mid), jnp.exp(mids[3] - step_mid)])
            kb = kd * by_chunk([jnp.exp(step_mid - mids[0]), jnp.exp(step_mid - mids[1]), None, None])
            att = jnp.where(mask_d, _dot_nt(qd.astype(BF16), kd.astype(BF16)), 0.0)
            att = att + jnp.where(mask_a, _dot_nt(qa.astype(BF16), ka.astype(BF16)), 0.0)
            att = att + _dot_nt(qb.astype(BF16), kb.astype(BF16))
            vb = v.astype(BF16)
            st = st_ref[hd]
            o = _dot(att.astype(BF16), vb) + _dot_nt(q_in.astype(BF16), st.astype(BF16))
            st_ref[hd] = st * jnp.exp(step_end) + _dot_tn(vb, k_end.astype(BF16))
            gate = _silu(proj_ref[r0:r0 + SUB, C_G + hd * DK:C_G + (hd + 1) * DK])
            ya_ref[r0:r0 + SUB, hs] = _rms(o) * hg_ref[:, hs] * gate

    u = proj_ref[:, C_CC:C_CC + CONV_W] * proj_ref[:, C_VB:C_VB + CONV_W]
    rows = lax.broadcasted_iota(I32, (tt, CONV_W), 0)
    c0, c1 = cbuf_ref[0:1], cbuf_ref[1:2]
    u1 = jnp.where(rows == 0, c1, pltpu.roll(u, 1, axis=0))
    u2 = jnp.where(rows == 0, c0, jnp.where(rows == 1, c1, pltpu.roll(u, 2, axis=0)))
    conv = cw_ref[0:1] * u2 + cw_ref[1:2] * u1 + cw_ref[2:3] * u + cb_ref[...]
    yb = proj_ref[:, C_BB:C_BB + CONV_W] * conv
    cbuf_ref[...] = u[tt - 2:tt]

    mixed = (_sigmoid(proj_ref[:, C_MGA:C_MGA + D_MODEL]) * _dot(ya_ref[...].astype(BF16), w_oh_ref[...])
             + _sigmoid(proj_ref[:, C_MGB:C_MGB + D_MODEL]) * _dot(yb.astype(BF16), w_oc_ref[...]))
    x1 = x + g1 * _dot(mixed.astype(BF16), w_o_ref[...])

    xmid, h2b, lgt = _ffn_pre(x1, (mod[3:4], mod[4:5], mod[5:6]), gffn_ref[...],
                              w_sgu_ref[...], w_sd_ref[...], wr_hl_ref[...])
    xmid_ref[0] = xmid
    _pack_rows(h2b, h2_ref)
    lgt_ref[...] = lgt

    @pl.when(t == nt - 1)
    def _():
        for hd in range(HEADS):
            s_out_ref[0, hd] = st_ref[hd].T
        cv_out_ref[0] = cbuf_ref[...]


def _const_spec(shape):
    nd = len(shape)
    return pl.BlockSpec(shape, lambda i, _nd=nd: (0,) * _nd, pipeline_mode=pl.Buffered(1))


def _mix(x, mod, n_tok, gmix, gffn, w_in, lbl, hg, cw, cb, w_oh, w_oc, w_o, wr_hl, w_sgu, w_sd):
    bsz, seq, _ = x.shape
    tt = MIX_TILE
    nt = seq // tt
    n_tiles = bsz * nt
    assert n_tiles * tt < n_tok <= (n_tiles + 1) * tt
    consts = [gmix, gffn, w_in, lbl, hg, cw, cb, w_oh, w_oc, w_o, wr_hl, w_sgu, w_sd]
    tile = lambda i: jnp.minimum(i, n_tiles - 1)
    return pl.pallas_call(
        functools.partial(_mix_kernel, nt=nt, n_tiles=n_tiles),
        grid=(n_tiles + 1,),
        in_specs=[pl.BlockSpec((1, tt, D_MODEL), lambda i: (tile(i) // nt, tile(i) % nt, 0)),
                  pl.BlockSpec((1, 6, D_MODEL), lambda i: (tile(i) // nt, 0, 0))]
                 + [_const_spec(a.shape) for a in consts],
        out_specs=[pl.BlockSpec((1, tt, D_MODEL), lambda i: (tile(i) // nt, tile(i) % nt, 0)),
                   pl.BlockSpec((2, tt, PLANE_W), lambda i: (0, i, 0)),
                   pl.BlockSpec((N_EXP, tt), lambda i: (0, i)),
                   pl.BlockSpec((1, HEADS, DK, DK), lambda i: (tile(i) // nt, 0, 0, 0)),
                   pl.BlockSpec((1, CONV_K - 1, CONV_W), lambda i: (tile(i) // nt, 0, 0))],
        out_shape=[jax.ShapeDtypeStruct((bsz, seq, D_MODEL), F32),
                   jax.ShapeDtypeStruct((2, n_tok, PLANE_W), I32),
                   jax.ShapeDtypeStruct((N_EXP, n_tok), F32),
                   jax.ShapeDtypeStruct((bsz, HEADS, DK, DK), F32),
                   jax.ShapeDtypeStruct((bsz, CONV_K - 1, CONV_W), F32)],
        scratch_shapes=[pltpu.VMEM((tt, IN_W), F32),
                        pltpu.VMEM((HEADS, DK, DK), F32),
                        pltpu.VMEM((CONV_K - 1, CONV_W), F32),
                        pltpu.VMEM((tt, KEY_W), F32)],
        compiler_params=pltpu.CompilerParams(
            dimension_semantics=("arbitrary",), vmem_limit_bytes=VMEM_LIMIT),
        name="mix",
    )(x, mod, *consts)


def _smp1_kernel(x_ref, mod_ref, gmix_ref, w_in_ref, lbl_ref, cw_ref, cb_ref, cst_ref,
                 f_ref, k_ref, q_ref, v_ref, gate_ref, yb_ref, sga_ref, sgb_ref, cv_out_ref):
    x = x_ref[...]
    sh1, sc1 = mod_ref[:, 0:D_MODEL], mod_ref[:, D_MODEL:2 * D_MODEL]
    h = _rms(x) * gmix_ref[...] * (1.0 + sc1) + sh1
    hb = h.astype(BF16)

    def proj(c, w):
        return _dot(hb, w_in_ref[:, c:c + w])

    lb = _lower_bound(lbl_ref[...])
    f = lb + (1.0 - lb) * _sigmoid(proj(C_F, KEY_W))
    f_ref[...] = f
    k_ref[...] = 1.0 - f
    q_ref[...] = _silu(proj(C_Q, KEY_W))
    v_ref[...] = proj(C_I, KEY_W)
    gate_ref[...] = _silu(proj(C_G, KEY_W))
    u = proj(C_CC, CONV_W) * proj(C_VB, CONV_W)
    c0, c1 = cst_ref[:, 0:CONV_W], cst_ref[:, CONV_W:2 * CONV_W]
    conv = cw_ref[0:1] * c0 + cw_ref[1:2] * c1 + cw_ref[2:3] * u + cb_ref[...]
    yb_ref[...] = proj(C_BB, CONV_W) * conv
    cv_out_ref[:, 0:CONV_W] = c1
    cv_out_ref[:, CONV_W:2 * CONV_W] = u
    sga_ref[...] = _sigmoid(proj(C_MGA, D_MODEL))
    sgb_ref[...] = _sigmoid(proj(C_MGB, D_MODEL))


def _smp1(x, mod, gmix, w_in, lbl, cw, cb, cst):
    n = x.shape[0]
    kw = jax.ShapeDtypeStruct((n, KEY_W), F32)
    dm = jax.ShapeDtypeStruct((n, D_MODEL), F32)
    return pl.pallas_call(
        _smp1_kernel,
        out_shape=[kw, kw, kw, kw, kw, kw, dm, dm,
                   jax.ShapeDtypeStruct((n, (CONV_K - 1) * CONV_W), F32)],
        compiler_params=pltpu.CompilerParams(vmem_limit_bytes=VMEM_LIMIT),
        name="smp1",
    )(x, mod, gmix, w_in, lbl, cw, cb, cst)


def _smp2_kernel(f_ref, k_ref, q_ref, v_ref, s_ref, s_out_ref, o_ref):
    g = f_ref.shape[0]
    for i in range(g):
        for hd in range(HEADS):
            hs = slice(hd * DK, (hd + 1) * DK)

            def col(ref):
                return jnp.broadcast_to(ref[i:i + 1, hs], (DK, DK)).T

            vrow = v_ref[i:i + 1, hs]
            s_new = col(f_ref) * s_ref[i, hd] + col(k_ref) * vrow
            s_out_ref[i, hd] = s_new
            o_ref[i:i + 1, hs] = jnp.sum(col(q_ref) * s_new, axis=0, keepdims=True)


def _smp2(f, k, q, v, state):
    n = f.shape[0]
    g = SMP_GROUP
    row_spec = pl.BlockSpec((g, KEY_W), lambda i: (i, 0))
    st_spec = pl.BlockSpec((g, HEADS, DK, DK), lambda i: (i, 0, 0, 0))
    return pl.pallas_call(
        _smp2_kernel,
        grid=(n // g,),
        in_specs=[row_spec, row_spec, row_spec, row_spec, st_spec],
        out_specs=[st_spec, row_spec],
        out_shape=[jax.ShapeDtypeStruct(state.shape, F32), jax.ShapeDtypeStruct((n, KEY_W), F32)],
        compiler_params=pltpu.CompilerParams(dimension_semantics=("arbitrary",)),
        name="smp2",
    )(f, k, q, v, state)


def _smp2_sc(f, k, q, v, state):
    n = f.shape[0]
    info = plsc.get_sparse_core_info()
    lanes = info.num_lanes
    n_workers = info.num_cores * info.num_subcores
    tok_per = n // n_workers
    n_chunks = DK // lanes
    vec = lambda: pltpu.VMEM((DK,), F32)

    @pl.kernel(out_type=[jax.ShapeDtypeStruct(state.shape, F32), jax.ShapeDtypeStruct((n, KEY_W), F32)],
               mesh=_sc_mesh(), scratch_types=[pltpu.VMEM((DK, DK), F32), vec(), vec(), vec(), vec(), vec()],
               compiler_params=pltpu.CompilerParams(needs_layout_passes=False), name="smp2")
    def run(f_hbm, k_hbm, q_hbm, v_hbm, s_hbm, s_out_hbm, o_hbm, s_v, f_v, k_v, q_v, v_v, o_v):
        wid = lax.axis_index("subcore") * info.num_cores + lax.axis_index("core")

        def tile(j, carry):
            t = wid * tok_per + j // HEADS
            hcols = pl.ds((j % HEADS) * DK, DK)
            pltpu.sync_copy(s_hbm.at[t, j % HEADS], s_v)
            for src, dst in ((f_hbm, f_v), (k_hbm, k_v), (q_hbm, q_v), (v_hbm, v_v)):
                pltpu.sync_copy(src.at[t, hcols], dst)
            v_chunks = [v_v[pl.ds(c * lanes, lanes)] for c in range(n_chunks)]

            def row(d, acc):
                at = [jnp.zeros((lanes,), I32) + d]
                fd, kd, qd = (plsc.load_gather(r, at) for r in (f_v, k_v, q_v))
                out = []
                for c in range(n_chunks):
                    cols = pl.ds(c * lanes, lanes)
                    new = s_v[d, cols] * fd + kd * v_chunks[c]
                    s_v[d, cols] = new
                    out.append(acc[c] + qd * new)
                return tuple(out)

            acc = lax.fori_loop(0, DK, row, tuple(jnp.zeros((lanes,), F32) for _ in range(n_chunks)))
            for c in range(n_chunks):
                o_v[pl.ds(c * lanes, lanes)] = acc[c]
            pltpu.sync_copy(s_v, s_out_hbm.at[t, j % HEADS])
            pltpu.sync_copy(o_v, o_hbm.at[t, hcols])
            return carry

        lax.fori_loop(0, tok_per * HEADS, tile, 0)

    return run(f, k, q, v, state)


def _smp3_kernel(x_ref, mod_ref, o_ref, gate_ref, yb_ref, sga_ref, sgb_ref, hg_ref, gffn_ref,
                 w_oh_ref, w_oc_ref, w_o_ref, wr_hl_ref, w_sgu_ref, w_sd_ref,
                 h2_all_ref, lgt_all_ref, xmid_ref, h2_ref, lgt_ref):
    del h2_all_ref, lgt_all_ref
    parts = []
    for hd in range(HEADS):
        hs = slice(hd * DK, (hd + 1) * DK)
        parts.append(_rms(o_ref[:, hs]) * hg_ref[:, hs] * gate_ref[:, hs])
    ya = jnp.concatenate(parts, axis=1)
    mixed = (sga_ref[...] * _dot(ya.astype(BF16), w_oh_ref[...])
             + sgb_ref[...] * _dot(yb_ref[...].astype(BF16), w_oc_ref[...]))
    g1 = mod_ref[:, 2 * D_MODEL:3 * D_MODEL]
    x1 = x_ref[...] + g1 * _dot(mixed.astype(BF16), w_o_ref[...])
    mod_rows = tuple(mod_ref[:, j * D_MODEL:(j + 1) * D_MODEL] for j in (3, 4, 5))
    xmid, h2b, lgt = _ffn_pre(x1, mod_rows, gffn_ref[...], w_sgu_ref[...], w_sd_ref[...],
                              wr_hl_ref[...])
    xmid_ref[...] = xmid
    _pack_rows(h2b, h2_ref)
    lgt_ref[...] = lgt


def _smp3(x, mod, o, gate, yb, sga, sgb, hg, gffn, w_oh, w_oc, w_o, wr_hl, w_sgu, w_sd,
          h2_all, lgt_all, n_prompt):
    n = x.shape[0]
    vmem_args = [x, mod, o, gate, yb, sga, sgb, hg, gffn, w_oh, w_oc, w_o, wr_hl, w_sgu, w_sd]
    blk = n_prompt // n

    def full(a):
        nd = a.ndim
        return pl.BlockSpec(a.shape, lambda i, _nd=nd: (0,) * _nd)

    return pl.pallas_call(
        _smp3_kernel,
        grid=(1,),
        in_specs=[full(a) for a in vmem_args]
                 + [pl.BlockSpec(memory_space=pl.ANY), pl.BlockSpec(memory_space=pl.ANY)],
        out_specs=[pl.BlockSpec((n, D_MODEL), lambda i: (0, 0)),
                   pl.BlockSpec((2, n, PLANE_W), lambda i: (0, blk, 0)),
                   pl.BlockSpec((N_EXP, n), lambda i: (0, blk))],
        out_shape=[jax.ShapeDtypeStruct((n, D_MODEL), F32),
                   jax.ShapeDtypeStruct(h2_all.shape, h2_all.dtype),
                   jax.ShapeDtypeStruct(lgt_all.shape, lgt_all.dtype)],
        input_output_aliases={len(vmem_args): 1, len(vmem_args) + 1: 2},
        compiler_params=pltpu.CompilerParams(
            dimension_semantics=("arbitrary",), vmem_limit_bytes=VMEM_LIMIT),
        name="smp3",
    )(*vmem_args, h2_all, lgt_all)


def _route_kernel(lgt_ref, bias_ref, idx_ref, w_ref, rank_ref, cnt_ref):
    tr = ROUTE_TILE
    n_tiles = lgt_ref.shape[1] // tr

    def tile(i, carry):
        cols = pl.ds(pl.multiple_of(i * tr, tr), tr)
        picks, weights, ranks, carry = _route_tile(lgt_ref[:, cols], bias_ref[...], carry)
        for k in range(TOP_K):
            idx_ref[k:k + 1, cols] = picks[k]
            w_ref[k:k + 1, cols] = weights[k]
            rank_ref[k:k + 1, cols] = ranks[k]
        return carry

    total = lax.fori_loop(0, n_tiles, tile, jnp.zeros((N_EXP, 1), F32))
    cnt_ref[...] = jnp.broadcast_to(total, cnt_ref.shape).astype(I32)


def _route_tile(lgt, bias, carry):
    tr = lgt.shape[1]
    neg = -jnp.inf
    scores = _sigmoid(lgt)
    sel = scores + bias
    j8 = lax.broadcasted_iota(I32, (GRP_SZ, tr), 0)
    groups = [sel[g * GRP_SZ:(g + 1) * GRP_SZ] for g in range(N_GRP)]
    gscore = []
    for grp in groups:
        m1 = jnp.max(grp, axis=0, keepdims=True)
        i1 = jnp.min(jnp.where(grp == m1, j8, GRP_SZ), axis=0, keepdims=True)
        m2 = jnp.max(jnp.where(j8 == i1, neg, grp), axis=0, keepdims=True)
        gscore.append(m1 + m2)
    kept = []
    for g in range(N_GRP):
        beaten = jnp.zeros((1, tr), I32)
        for o in range(N_GRP):
            if o < g:
                beaten = beaten + (gscore[o] >= gscore[g]).astype(I32)
            elif o > g:
                beaten = beaten + (gscore[o] > gscore[g]).astype(I32)
        kept.append(jnp.where(beaten < TOPK_GRP, groups[g], neg))
    masked = jnp.concatenate(kept, axis=0)
    ei = lax.broadcasted_iota(I32, masked.shape, 0)
    chosen = jnp.zeros(masked.shape, jnp.bool_)
    picks, weights = [], []
    for _ in range(TOP_K):
        m = jnp.max(masked, axis=0, keepdims=True)
        pick = jnp.min(jnp.where(masked == m, ei, N_EXP), axis=0, keepdims=True)
        hit = ei == pick
        weights.append(jnp.sum(jnp.where(hit, scores, 0.0), axis=0, keepdims=True))
        picks.append(pick)
        chosen = chosen | hit
        masked = jnp.where(hit, neg, masked)
    wsum = weights[0]
    for w in weights[1:]:
        wsum = wsum + w
    sel01 = chosen.astype(F32)
    r = lax.broadcasted_iota(I32, (tr, tr), 0)
    c = lax.broadcasted_iota(I32, (tr, tr), 1)
    before = (r < c).astype(BF16)
    cnt = _dot(sel01.astype(BF16), before) + carry
    weights = [w / wsum * ROUTED_SCALE for w in weights]
    ranks = [jnp.sum(jnp.where(ei == p, cnt, 0.0), axis=0, keepdims=True).astype(I32) for p in picks]
    return picks, weights, ranks, carry + jnp.sum(sel01, axis=1, keepdims=True)


def _route(lgt, bias):
    n = lgt.shape[1]
    assert n % ROUTE_TILE == 0
    slot = lambda dt: jax.ShapeDtypeStruct((TOP_K, n), dt)
    return pl.pallas_call(
        _route_kernel,
        out_shape=[slot(I32), slot(F32), slot(I32),
                   jax.ShapeDtypeStruct((N_EXP, 128), I32)],
        name="route",
    )(lgt, bias)


def _dest_kernel(start_ref, idx_ref, rank_ref, all_ref, *chunk_refs, n_rows, chunks):
    n_tok = idx_ref.shape[1]
    idx = idx_ref[...]
    acc = rank_ref[...]
    for e in range(N_EXP):
        acc = acc + jnp.where(idx == e, start_ref[e], 0)
    for k in range(TOP_K):
        for p in range(2):
            row = acc[k:k + 1, :] + p * n_rows
            seg = 2 * k + p
            all_ref[:, seg * n_tok:(seg + 1) * n_tok] = row
            for (c0, c1), ref in zip(chunks, chunk_refs):
                ref[:, seg * (c1 - c0):(seg + 1) * (c1 - c0)] = row[:, c0:c1]


def _dest(pad_start, idx, rank, n_rows, chunks):
    k, n_tok = idx.shape
    vmem = pl.BlockSpec(memory_space=pltpu.VMEM)
    sizes = [n_tok] + [c1 - c0 for c0, c1 in chunks]
    return pl.pallas_call(
        functools.partial(_dest_kernel, n_rows=n_rows, chunks=chunks),
        in_specs=[pl.BlockSpec(memory_space=pltpu.SMEM), vmem, vmem],
        out_specs=[vmem] * len(sizes),
        out_shape=[jax.ShapeDtypeStruct((1, 2 * k * n), I32) for n in sizes],
        name="dest",
    )(pad_start, idx, rank)


def _sc_mesh():
    return plsc.VectorSubcoreMesh(core_axis_name="core", subcore_axis_name="subcore")


def _dispatch(rows, dest, n_out):
    n, width = rows.shape
    win = SC_WINDOW
    steps = n // win

    @pl.kernel(out_type=jax.ShapeDtypeStruct((n_out, width), rows.dtype), mesh=_sc_mesh(),
               scratch_types=[], name="dispatch")
    def run(x_hbm, *refs):
        i_hbms, o_hbm = refs[:TOP_K], refs[TOP_K]

        def body(x_vmem, *i_vmems):
            for i_vmem in i_vmems:
                pltpu.sync_copy(x_vmem, o_hbm.at[i_vmem.at[0]])

        pltpu.emit_pipeline(
            body,
            grid=(steps,),
            in_specs=[pl.BlockSpec((win, width), lambda i: (i, 0))]
                     + [pl.BlockSpec((1, win), lambda i, k=k: (0, k * steps + i)) for k in range(TOP_K)],
            out_specs=[],
            core_axis_name=("core", "subcore"),
            dimension_semantics=(pltpu.PARALLEL,),
        )(x_hbm, *i_hbms)

    assert dest.shape == (1, TOP_K * n)
    return run(rows, *([dest] * TOP_K))


def _combine(rows, dest_flat):
    width = rows.shape[1]
    n = dest_flat.shape[1]
    win = SC_WINDOW

    @pl.kernel(out_type=jax.ShapeDtypeStruct((n, width), rows.dtype), mesh=_sc_mesh(),
               scratch_types=[], name="combine")
    def run(y_hbm, i_hbm, o_hbm):
        def body(i_vmem, o_vmem):
            pltpu.sync_copy(y_hbm.at[i_vmem.at[0]], o_vmem)

        pltpu.emit_pipeline(
            body,
            grid=(n // win,),
            in_specs=[pl.BlockSpec((1, win), lambda i: (0, i))],
            out_specs=[pl.BlockSpec((win, width), lambda i: (i, 0))],
            core_axis_name=("core", "subcore"),
            dimension_semantics=(pltpu.PARALLEL,),
        )(i_hbm, o_hbm)

    return run(rows, dest_flat)


def _gmm_kernel(blk_exp_ref, n_used_ref, xs_hbm, wg_hbm, wu_hbm, wd_hbm, ys_hbm,
                xbuf, ybuf, wg32, wu32, wd32, wgu_b, wd_b, xsem, ysem, wsem, run_ref):
    nx, ny, bm = xbuf.shape[0], ybuf.shape[0], xbuf.shape[2]
    n_used = n_used_ref[0]

    def x_copies(b):
        rows, slot = pl.ds(b * bm, bm), b % nx
        return [pltpu.make_async_copy(xs_hbm.at[p, rows, :], xbuf.at[slot, p], xsem.at[slot, p]) for p in range(2)]

    def y_copies(b):
        rows, slot = pl.ds(b * bm, bm), b % ny
        return [pltpu.make_async_copy(ybuf.at[slot, p], ys_hbm.at[p, rows, :], ysem.at[slot, p]) for p in range(2)]

    def start(copies):
        for c in copies:
            c.start()

    def wait(copies):
        for c in copies:
            c.wait()

    def w_copies(e, slot):
        return (pltpu.make_async_copy(wg_hbm.at[e], wg32.at[slot], wsem.at[slot, 0]),
                pltpu.make_async_copy(wu_hbm.at[e], wu32.at[slot], wsem.at[slot, 1]),
                pltpu.make_async_copy(wd_hbm.at[e], wd32.at[slot], wsem.at[slot, 2]))

    def run_end(b):
        return lax.while_loop(lambda j: (j < n_used) & (blk_exp_ref[jnp.minimum(j, n_used - 1)] == blk_exp_ref[b]),
                              lambda j: j + 1, b + 1)

    run_ref[0] = 0
    start(x_copies(0))
    start(w_copies(blk_exp_ref[0], 0))
    for j in range(1, nx - 1):
        @pl.when(j < n_used)
        def _():
            start(x_copies(j))

    def block(b, carry):
        @pl.when(b + nx - 1 < n_used)
        def _():
            start(x_copies(b + nx - 1))

        @pl.when((b == 0) | (blk_exp_ref[b] != blk_exp_ref[jnp.maximum(b - 1, 0)]))
        def _():
            wslot = run_ref[0] % 2
            run_ref[0] = run_ref[0] + 1
            wait(w_copies(blk_exp_ref[b], wslot))
            wgu_b[:, 0:EXP_FF] = wg32[wslot].astype(BF16)
            wgu_b[:, EXP_FF:2 * EXP_FF] = wu32[wslot].astype(BF16)
            wd_b[...] = wd32[wslot].astype(BF16)
            nxt = run_end(b)

            @pl.when(nxt < n_used)
            def _():
                start(w_copies(blk_exp_ref[jnp.minimum(nxt, n_used - 1)], 1 - wslot))

        wait(x_copies(b))

        @pl.when(b >= ny)
        def _():
            wait(y_copies(b - ny))

        xslot = b % nx
        xc = _unpack_rows(xbuf[xslot, 0], xbuf[xslot, 1])
        gu = sum(_dot(c, wgu_b[i * PLANE_W:(i + 1) * PLANE_W, :]) for i, c in enumerate(xc))
        act = (_silu(gu[:, :EXP_FF]) * gu[:, EXP_FF:]).astype(BF16)
        _pack_rows(_dot(act, wd_b[...]).astype(BF16), ybuf.at[b % ny])
        start(y_copies(b))
        return carry

    lax.fori_loop(0, n_used, block, 0)

    for j in range(ny, 0, -1):
        @pl.when(n_used >= j)
        def _():
            wait(y_copies(n_used - j))


def _gmm(blk_exp, n_used, xs, w_gate, w_up, w_down):
    n_rows = xs.shape[1]
    bm = GMM_BM
    nb = n_rows // bm

    assert blk_exp.shape == (nb,)
    any_spec = pl.BlockSpec(memory_space=pl.ANY)
    grid_spec = pltpu.PrefetchScalarGridSpec(
        num_scalar_prefetch=2,
        grid=(1,),
        in_specs=[any_spec, any_spec, any_spec, any_spec],
        out_specs=any_spec,
        scratch_shapes=[pltpu.VMEM((GMM_NX, 2, bm, PLANE_W), I32), pltpu.VMEM((GMM_NY, 2, bm, PLANE_W), I32),
                        pltpu.VMEM((2, D_MODEL, EXP_FF), F32), pltpu.VMEM((2, D_MODEL, EXP_FF), F32),
                        pltpu.VMEM((2, EXP_FF, D_MODEL), F32),
                        pltpu.VMEM((D_MODEL, 2 * EXP_FF), BF16), pltpu.VMEM((EXP_FF, D_MODEL), BF16),
                        pltpu.SemaphoreType.DMA((GMM_NX, 2)), pltpu.SemaphoreType.DMA((GMM_NY, 2)),
                        pltpu.SemaphoreType.DMA((2, 3)), pltpu.SMEM((1,), I32)],
    )
    return pl.pallas_call(
        _gmm_kernel,
        grid_spec=grid_spec,
        out_shape=jax.ShapeDtypeStruct((2, n_rows, PLANE_W), I32),
        compiler_params=pltpu.CompilerParams(dimension_semantics=("arbitrary",)),
        name="gmm",
    )(blk_exp, n_used, xs, w_gate, w_up, w_down)


def _final_kernel(xmid_ref, g2_ref, z_ref, w_ref, gfin_ref, *rest):
    y_ref = rest[-1]
    accs = [jnp.zeros((xmid_ref.shape[0], PLANE_W), F32) for _ in range(4)]
    w_cols = w_ref[...].T
    for k in range(TOP_K):
        wk = w_cols[:, k:k + 1]
        cols = _unpack_rows(z_ref[k, 0], z_ref[k, 1])
        accs = [a + wk * c.astype(F32) for a, c in zip(accs, cols)]
    acc = jnp.concatenate(accs, axis=1)
    y_ref[...] = _rms(xmid_ref[...] + g2_ref[0] * acc) * gfin_ref[...]


def _final(xmid, g2, z, w_t, gfin, tile, *, n_tiles, x_tile0, z_tile0, w_tile0, tiles_per_g2, y_prev=None):
    args = [xmid, g2, z, w_t, gfin]
    in_specs = [pl.BlockSpec((tile, D_MODEL), lambda i: (x_tile0 + i, 0)),
                pl.BlockSpec((1, g2.shape[1], D_MODEL), lambda i: ((x_tile0 + i) // tiles_per_g2, 0, 0)),
                pl.BlockSpec((TOP_K, 2, tile, PLANE_W), lambda i: (0, 0, z_tile0 + i, 0)),
                pl.BlockSpec((TOP_K, tile), lambda i: (0, w_tile0 + i)),
                pl.BlockSpec((1, D_MODEL), lambda i: (0, 0))]
    aliases = {}
    if y_prev is not None:
        args.append(y_prev)
        in_specs.append(pl.BlockSpec(memory_space=pl.ANY))
        aliases = {len(args) - 1: 0}
    return pl.pallas_call(
        _final_kernel,
        grid=(n_tiles,),
        in_specs=in_specs,
        out_specs=pl.BlockSpec((tile, D_MODEL), lambda i: (x_tile0 + i, 0)),
        out_shape=jax.ShapeDtypeStruct(xmid.shape, F32),
        input_output_aliases=aliases,
        compiler_params=pltpu.CompilerParams(dimension_semantics=("arbitrary",)),
        name="final",
    )(*args)


def kernel(x_prompt, x_sample, state_hgrn, state_conv, c_prompt, c_sample, w_ada, b_ada, norm_mix_g, norm_ffn_g, w_in, lb_logits, hgrn_norm_g, conv_w, conv_b, w_out_hgrn, w_out_conv, w_o, w_router, router_bias, w_exp_gate, w_exp_up, w_exp_down, w_sh_gate, w_sh_up, w_sh_down, final_norm_g):
    assert w_ada.shape[0] == 1 and lb_logits.shape[0] == 2
    bsz, seq, _ = x_prompt.shape
    n_smp = x_sample.shape[0]
    n_prompt = bsz * seq
    n_tok = n_prompt + n_smp

    w_in_b = w_in[0].astype(BF16)
    w_oh_b = w_out_hgrn[0].astype(BF16)
    w_oc_b = w_out_conv[0].astype(BF16)
    w_o_b = w_o[0].astype(BF16)
    wr_t = w_router[0].T
    wr_hi = wr_t.astype(BF16)
    wr_hl = jnp.concatenate([wr_hi, (wr_t - wr_hi.astype(F32)).astype(BF16)], axis=0)
    w_sgu = jnp.concatenate([w_sh_gate[0], w_sh_up[0]], axis=1).astype(BF16)
    w_sd = w_sh_down[0].astype(BF16)
    gmix = norm_mix_g[0].reshape(1, D_MODEL)
    gffn = norm_ffn_g[0].reshape(1, D_MODEL)
    hg = hgrn_norm_g[0].reshape(1, KEY_W)
    cw = conv_w[0]
    cb = conv_b[0].reshape(1, CONV_W)
    gfin = final_norm_g.reshape(1, D_MODEL)

    mod_p, mod_s = _ada(c_prompt, c_sample, w_ada[0], b_ada[0])

    xmid_p, h2_all, lgt_all, s_p, cv_p = _mix(
        x_prompt, mod_p.reshape(bsz, 6, D_MODEL), n_tok, gmix, gffn, w_in_b, lb_logits, hg, cw, cb, w_oh_b, w_oc_b, w_o_b, wr_hl, w_sgu, w_sd)

    xs2 = x_sample.reshape(n_smp, D_MODEL)
    f, kk, q, v, gate, yb, sga, sgb, cv_s = _smp1(
        xs2, mod_s, gmix, w_in_b, lb_logits, cw, cb, state_conv[0].reshape(n_smp, (CONV_K - 1) * CONV_W))
    s_s, o_s = _smp2_sc(f, kk, q, v, state_hgrn[0])
    xmid_s, h2_all, lgt_all = _smp3(xs2, mod_s, o_s, gate, yb, sga, sgb, hg, gffn,
                                    w_oh_b, w_oc_b, w_o_b, wr_hl, w_sgu, w_sd,
                                    h2_all, lgt_all, n_prompt)

    idx, w_tok, rank, cnt = _route(lgt_all, router_bias[0].reshape(N_EXP, 1))

    bm = GMM_BM
    n_blocks = (n_tok * TOP_K + N_EXP * (bm - 1)) // bm
    n_rows = n_blocks * bm
    counts = cnt[:, 0]
    padded = (counts + bm - 1) // bm * bm
    pad_end = jnp.cumsum(padded)
    pad_start = pad_end - padded
    blk_row0 = jnp.arange(n_blocks, dtype=I32) * bm
    blk_exp = jnp.minimum(jnp.sum((pad_end[None, :] <= blk_row0[:, None]).astype(I32), axis=1), N_EXP - 1)
    n_used = (pad_end[-1:] // bm).astype(I32)
    chunk = n_prompt // FINAL_CHUNKS
    chunks = tuple((c * chunk, n_tok if c == FINAL_CHUNKS - 1 else (c + 1) * chunk) for c in range(FINAL_CHUNKS))
    dest, *chunk_dest = _dest(pad_start.astype(I32), idx, rank, n_rows, chunks)

    xs = _dispatch(h2_all.reshape(2 * n_tok, PLANE_W), dest, 2 * n_rows).reshape(2, n_rows, PLANE_W)
    ys = _gmm(blk_exp, n_used, xs, w_exp_gate[0], w_exp_up[0], w_exp_down[0])

    ys_flat = ys.reshape(2 * n_rows, PLANE_W)
    w_t = w_tok
    xmid_p2 = xmid_p.reshape(n_prompt, D_MODEL)
    g2_p = mod_p[:, 5 * D_MODEL:].reshape(bsz, 1, D_MODEL)
    g2_s = mod_s[:, 5 * D_MODEL:].reshape(1, n_smp, D_MODEL)
    y_p = None
    for (c0, c1), dest_c in zip(chunks, chunk_dest):
        z = _combine(ys_flat, dest_c).reshape(TOP_K, 2, c1 - c0, PLANE_W)
        y_p = _final(xmid_p2, g2_p, z, w_t, gfin, FINAL_TILE, n_tiles=chunk // FINAL_TILE,
                     x_tile0=c0 // FINAL_TILE, z_tile0=0, w_tile0=c0 // FINAL_TILE,
                     tiles_per_g2=seq // FINAL_TILE, y_prev=y_p)
    y_s = _final(xmid_s, g2_s, z, w_t, gfin, n_smp, n_tiles=1, x_tile0=0, z_tile0=chunk // n_smp,
                 w_tile0=n_prompt // n_smp, tiles_per_g2=1)

    return (y_p.reshape(bsz, seq, D_MODEL), y_s.reshape(n_smp, 1, D_MODEL),
            s_p[None], cv_p[None], s_s[None], cv_s.reshape(1, n_smp, CONV_K - 1, CONV_W))
```

```python
import functools

import jax
import jax.numpy as jnp
from jax import lax
from jax.experimental import pallas as pl
from jax.experimental.pallas import tpu as pltpu
from jax.experimental.pallas import tpu_sc as plsc

F32 = jnp.float32
BF16 = jnp.bfloat16
I32 = jnp.int32

D_MODEL = 1024
HALF_D = D_MODEL // 2
HEADS = 4
DK = 128
KEY_W = HEADS * DK
CONV_W = 512
CONV_K = 3
IN_W = 2 * KEY_W + 2 * KEY_W + 3 * CONV_W + 2 * D_MODEL
N_EXP = 64
TOP_K = 8
N_GRP = 8
GRP_SZ = N_EXP // N_GRP
TOPK_GRP = 4
EXP_FF = 256
SH_FF = 256
ROUTED_SCALE = 2.5
EPS = 1e-6

C_Q, C_F, C_I, C_G = 0, 512, 1024, 1536
C_BB, C_CC, C_VB = 2048, 2560, 3072
C_MGA, C_MGB = 3584, 4608

MIX_TILE = 512
SUB = 256
CHUNK = 64
ROUTE_TILE = 384
GMM_BM = 512
GMM_NX = 4
GMM_NY = 3
FINAL_TILE = 512
FINAL_CHUNKS = 4
SC_WINDOW = 128
PLANE_W = HALF_D // 2
VMEM_LIMIT = 56 * 1024 * 1024


def _dot(a, b):
    return jnp.dot(a, b, preferred_element_type=F32)


def _dot_nt(a, b):
    return lax.dot_general(a, b, (((1,), (1,)), ((), ())), preferred_element_type=F32)


def _dot_tn(a, b):
    return lax.dot_general(a, b, (((0,), (0,)), ((), ())), preferred_element_type=F32)


def _sigmoid(x):
    return 0.5 * jnp.tanh(0.5 * x) + 0.5


def _silu(x):
    h = 0.5 * x
    return h * jnp.tanh(h) + h


def _rms(x):
    return x * lax.rsqrt(jnp.mean(x * x, axis=-1, keepdims=True) + EPS)


def _lower_bound(lbl):
    a, b = lbl[0:1], lbl[1:2]
    m = jnp.maximum(a, b)
    ea, eb = jnp.exp(a - m), jnp.exp(b - m)
    return ea / (ea + eb)


def _split3(x):
    hi = x.astype(BF16)
    r1 = x - hi.astype(F32)
    mid = r1.astype(BF16)
    lo = (r1 - mid.astype(F32)).astype(BF16)
    return hi, mid, lo


def _words(lo_b, hi_b):
    lo = lax.shift_right_logical(lax.bitcast_convert_type(lo_b.astype(F32), I32), 16)
    hi = lax.bitcast_convert_type(hi_b.astype(F32), I32) & jnp.int32(-65536)
    return lo | hi


def _halves(w):
    lo = lax.bitcast_convert_type(lax.shift_left(w, 16), F32)
    hi = lax.bitcast_convert_type(w & jnp.int32(-65536), F32)
    return lo.astype(BF16), hi.astype(BF16)


def _pack_rows(xb, out_ref):
    words = _words(xb[:, :HALF_D], xb[:, HALF_D:])
    out_ref[0] = words[:, :PLANE_W]
    out_ref[1] = words[:, PLANE_W:]


def _unpack_rows(p0, p1):
    c0, c2 = _halves(p0)
    c1, c3 = _halves(p1)
    return c0, c1, c2, c3


def _ada_kernel(cp_ref, cs_ref, w_ref, b_ref, op_ref, os_ref):
    w = w_ref[...].astype(BF16)
    for c_ref, o_ref in ((cp_ref, op_ref), (cs_ref, os_ref)):
        o_ref[...] = _dot(_silu(c_ref[...]).astype(BF16), w) + b_ref[...]


def _ada(c_prompt, c_sample, w_ada, b_ada):
    blk = 1024
    rows = lambda c: pl.BlockSpec((c.shape[0], D_MODEL), lambda j: (0, 0))
    cols = lambda c: pl.BlockSpec((c.shape[0], blk), lambda j: (0, j))
    return pl.pallas_call(
        _ada_kernel,
        grid=(6 * D_MODEL // blk,),
        in_specs=[rows(c_prompt), rows(c_sample),
                  pl.BlockSpec((D_MODEL, blk), lambda j: (0, j)),
                  pl.BlockSpec((1, blk), lambda j: (0, j))],
        out_specs=[cols(c_prompt), cols(c_sample)],
        out_shape=[jax.ShapeDtypeStruct((c.shape[0], 6 * D_MODEL), F32) for c in (c_prompt, c_sample)],
        name="ada",
    )(c_prompt, c_sample, w_ada, b_ada.reshape(1, -1))


def _ffn_pre(x1, mod_rows, gffn, w_sgu, w_sd, wr_hl):
    sh2, sc2, g2 = mod_rows
    h2 = _rms(x1) * gffn * (1.0 + sc2) + sh2
    h2b = h2.astype(BF16)
    gu = _dot(h2b, w_sgu)
    act = _silu(gu[:, :SH_FF]) * gu[:, SH_FF:]
    xmid = x1 + g2 * _dot(act.astype(BF16), w_sd)
    h2lo = (h2 - h2b.astype(F32)).astype(BF16)
    both = _dot_nt(wr_hl, h2b)
    lgt = both[:N_EXP] + both[N_EXP:] + _dot_nt(wr_hl[:N_EXP], h2lo)
    return xmid, h2b, lgt


def _mix_kernel(*refs, nt, n_tiles):
    i = pl.program_id(0)
    h2_ref, lgt_ref = refs[16], refs[17]

    @pl.when(i == n_tiles)
    def _():
        h2_ref[...] = jnp.zeros_like(h2_ref)
        lgt_ref[...] = jnp.zeros_like(lgt_ref)

    @pl.when(i < n_tiles)
    def _():
        _mix_tile(i % nt, nt, *refs)


def _mix_tile(t, nt, x_ref, mod_ref, gmix_ref, gffn_ref, w_in_ref, lbl_ref, hg_ref, cw_ref, cb_ref,
              w_oh_ref, w_oc_ref, w_o_ref, wr_hl_ref, w_sgu_ref, w_sd_ref,
              xmid_ref, h2_ref, lgt_ref, s_out_ref, cv_out_ref,
              proj_ref, st_ref, cbuf_ref, ya_ref):
    tt = x_ref.shape[1]

    @pl.when(t == 0)
    def _():
        st_ref[...] = jnp.zeros_like(st_ref)
        cbuf_ref[...] = jnp.zeros_like(cbuf_ref)

    x = x_ref[0]
    mod = mod_ref[0]
    sh1, sc1, g1 = mod[0:1], mod[1:2], mod[2:3]
    h = _rms(x) * gmix_ref[...] * (1.0 + sc1) + sh1
    hb = h.astype(BF16)
    for c in range(0, IN_W, 512):
        proj_ref[:, c:c + 512] = _dot(hb, w_in_ref[:, c:c + 512])

    lb = _lower_bound(lbl_ref[...])
    row = lax.broadcasted_iota(I32, (SUB, SUB), 0)
    col = lax.broadcasted_iota(I32, (SUB, SUB), 1)
    tri = (col <= row).astype(BF16)
    mask_d = (col <= row) & (row // CHUNK == col // CHUNK)
    mask_a = row // (2 * CHUNK) == col // (2 * CHUNK)
    n_ch = SUB // CHUNK

    def by_chunk(vals):
        return jnp.concatenate([jnp.zeros((CHUNK, DK), F32) if v is None
                                else jnp.broadcast_to(v, (CHUNK, DK)) for v in vals], axis=0)

    for s in range(tt // SUB):
        r0 = s * SUB
        f = lb + (1.0 - lb) * _sigmoid(proj_ref[r0:r0 + SUB, C_F:C_F + KEY_W])
        kk = 1.0 - f
        hi, mid, lo = _split3(jnp.log(f))
        bc = _dot(tri, hi) + _dot(tri, mid) + _dot(tri, lo)
        for hd in range(HEADS):
            hs = slice(hd * DK, (hd + 1) * DK)
            bh = bc[:, hs]
            at = lambda r: bh[r:r + 1]
            mids = [at(c * CHUNK + CHUNK // 2 - 1) for c in range(n_ch)]
            pair_mid = [at(CHUNK - 1), at(3 * CHUNK - 1)]
            step_mid, step_end = at(2 * CHUNK - 1), at(SUB - 1)
            arg = bh - by_chunk(mids)
            e_pos, e_neg = jnp.exp(arg), jnp.exp(-arg)
            q = _silu(proj_ref[r0:r0 + SUB, C_Q + hd * DK:C_Q + (hd + 1) * DK])
            v = proj_ref[r0:r0 + SUB, C_I + hd * DK:C_I + (hd + 1) * DK]
            qd = q * e_pos
            kd = kk[:, hs] * e_neg
            q_in = qd * by_chunk([jnp.exp(m) for m in mids])
            k_end = kd * by_chunk([jnp.exp(step_end - m) for m in mids])
            qa = qd * by_chunk([None, jnp.exp(mids[1] - pair_mid[0]), None, jnp.exp(mids[3] - pair_mid[1])])
            ka = kd * by_chunk([jnp.exp(pair_mid[0] - mids[0]), None, jnp.exp(pair_mid[1] - mids[2]), None])
            qb = qd * by_chunk([None, None, jnp.exp(mids[2] - step_mid), jnp.exp(mids[3] - step_mid)])
            kb = kd * by_chunk([jnp.exp(step_mid - mids[0]), jnp.exp(step_mid - mids[1]), None, None])
            att = jnp.where(mask_d, _dot_nt(qd.astype(BF16), kd.astype(BF16)), 0.0)
            att = att + jnp.where(mask_a, _dot_nt(qa.astype(BF16), ka.astype(BF16)), 0.0)
            att = att + _dot_nt(qb.astype(BF16), kb.astype(BF16))
            vb = v.astype(BF16)
            st = st_ref[hd]
            o = _dot(att.astype(BF16), vb) + _dot_nt(q_in.astype(BF16), st.astype(BF16))
            st_ref[hd] = st * jnp.exp(step_end) + _dot_tn(vb, k_end.astype(BF16))
            gate = _silu(proj_ref[r0:r0 + SUB, C_G + hd * DK:C_G + (hd + 1) * DK])
            ya_ref[r0:r0 + SUB, hs] = _rms(o) * hg_ref[:, hs] * gate

    u = proj_ref[:, C_CC:C_CC + CONV_W] * proj_ref[:, C_VB:C_VB + CONV_W]
    rows = lax.broadcasted_iota(I32, (tt, CONV_W), 0)
    c0, c1 = cbuf_ref[0:1], cbuf_ref[1:2]
    u1 = jnp.where(rows == 0, c1, pltpu.roll(u, 1, axis=0))
    u2 = jnp.where(rows == 0, c0, jnp.where(rows == 1, c1, pltpu.roll(u, 2, axis=0)))
    conv = cw_ref[0:1] * u2 + cw_ref[1:2] * u1 + cw_ref[2:3] * u + cb_ref[...]
    yb = proj_ref[:, C_BB:C_BB + CONV_W] * conv
    cbuf_ref[...] = u[tt - 2:tt]

    mixed = (_sigmoid(proj_ref[:, C_MGA:C_MGA + D_MODEL]) * _dot(ya_ref[...].astype(BF16), w_oh_ref[...])
             + _sigmoid(proj_ref[:, C_MGB:C_MGB + D_MODEL]) * _dot(yb.astype(BF16), w_oc_ref[...]))
    x1 = x + g1 * _dot(mixed.astype(BF16), w_o_ref[...])

    xmid, h2b, lgt = _ffn_pre(x1, (mod[3:4], mod[4:5], mod[5:6]), gffn_ref[...],
                              w_sgu_ref[...], w_sd_ref[...], wr_hl_ref[...])
    xmid_ref[0] = xmid
    _pack_rows(h2b, h2_ref)
    lgt_ref[...] = lgt

    @pl.when(t == nt - 1)
    def _():
        for hd in range(HEADS):
            s_out_ref[0, hd] = st_ref[hd].T
        cv_out_ref[0] = cbuf_ref[...]


def _const_spec(shape):
    nd = len(shape)
    return pl.BlockSpec(shape, lambda i, _nd=nd: (0,) * _nd, pipeline_mode=pl.Buffered(1))


def _mix(x, mod, n_tok, gmix, gffn, w_in, lbl, hg, cw, cb, w_oh, w_oc, w_o, wr_hl, w_sgu, w_sd):
    bsz, seq, _ = x.shape
    tt = MIX_TILE
    nt = seq // tt
    n_tiles = bsz * nt
    assert n_tiles * tt < n_tok <= (n_tiles + 1) * tt
    consts = [gmix, gffn, w_in, lbl, hg, cw, cb, w_oh, w_oc, w_o, wr_hl, w_sgu, w_sd]
    tile = lambda i: jnp.minimum(i, n_tiles - 1)
    return pl.pallas_call(
        functools.partial(_mix_kernel, nt=nt, n_tiles=n_tiles),
        grid=(n_tiles + 1,),
        in_specs=[pl.BlockSpec((1, tt, D_MODEL), lambda i: (tile(i) // nt, tile(i) % nt, 0)),
                  pl.BlockSpec((1, 6, D_MODEL), lambda i: (tile(i) // nt, 0, 0))]
                 + [_const_spec(a.shape) for a in consts],
        out_specs=[pl.BlockSpec((1, tt, D_MODEL), lambda i: (tile(i) // nt, tile(i) % nt, 0)),
                   pl.BlockSpec((2, tt, PLANE_W), lambda i: (0, i, 0)),
                   pl.BlockSpec((N_EXP, tt), lambda i: (0, i)),
                   pl.BlockSpec((1, HEADS, DK, DK), lambda i: (tile(i) // nt, 0, 0, 0)),
                   pl.BlockSpec((1, CONV_K - 1, CONV_W), lambda i: (tile(i) // nt, 0, 0))],
        out_shape=[jax.ShapeDtypeStruct((bsz, seq, D_MODEL), F32),
                   jax.ShapeDtypeStruct((2, n_tok, PLANE_W), I32),
                   jax.ShapeDtypeStruct((N_EXP, n_tok), F32),
                   jax.ShapeDtypeStruct((bsz, HEADS, DK, DK), F32),
                   jax.ShapeDtypeStruct((bsz, CONV_K - 1, CONV_W), F32)],
        scratch_shapes=[pltpu.VMEM((tt, IN_W), F32),
                        pltpu.VMEM((HEADS, DK, DK), F32),
                        pltpu.VMEM((CONV_K - 1, CONV_W), F32),
                        pltpu.VMEM((tt, KEY_W), F32)],
        compiler_params=pltpu.CompilerParams(
            dimension_semantics=("arbitrary",), vmem_limit_bytes=VMEM_LIMIT),
        name="mix",
    )(x, mod, *consts)


def _smp1_kernel(x_ref, mod_ref, gmix_ref, w_in_ref, lbl_ref, cw_ref, cb_ref, cst_ref,
                 f_ref, k_ref, q_ref, v_ref, gate_ref, yb_ref, sga_ref, sgb_ref, cv_out_ref):
    x = x_ref[...]
    sh1, sc1 = mod_ref[:, 0:D_MODEL], mod_ref[:, D_MODEL:2 * D_MODEL]
    h = _rms(x) * gmix_ref[...] * (1.0 + sc1) + sh1
    hb = h.astype(BF16)

    def proj(c, w):
        return _dot(hb, w_in_ref[:, c:c + w])

    lb = _lower_bound(lbl_ref[...])
    f = lb + (1.0 - lb) * _sigmoid(proj(C_F, KEY_W))
    f_ref[...] = f
    k_ref[...] = 1.0 - f
    q_ref[...] = _silu(proj(C_Q, KEY_W))
    v_ref[...] = proj(C_I, KEY_W)
    gate_ref[...] = _silu(proj(C_G, KEY_W))
    u = proj(C_CC, CONV_W) * proj(C_VB, CONV_W)
    c0, c1 = cst_ref[:, 0:CONV_W], cst_ref[:, CONV_W:2 * CONV_W]
    conv = cw_ref[0:1] * c0 + cw_ref[1:2] * c1 + cw_ref[2:3] * u + cb_ref[...]
    yb_ref[...] = proj(C_BB, CONV_W) * conv
    cv_out_ref[:, 0:CONV_W] = c1
    cv_out_ref[:, CONV_W:2 * CONV_W] = u
    sga_ref[...] = _sigmoid(proj(C_MGA, D_MODEL))
    sgb_ref[...] = _sigmoid(proj(C_MGB, D_MODEL))


def _smp1(x, mod, gmix, w_in, lbl, cw, cb, cst):
    n = x.shape[0]
    kw = jax.ShapeDtypeStruct((n, KEY_W), F32)
    dm = jax.ShapeDtypeStruct((n, D_MODEL), F32)
    return pl.pallas_call(
        _smp1_kernel,
        out_shape=[kw, kw, kw, kw, kw, kw, dm, dm,
                   jax.ShapeDtypeStruct((n, (CONV_K - 1) * CONV_W), F32)],
        compiler_params=pltpu.CompilerParams(vmem_limit_bytes=VMEM_LIMIT),
        name="smp1",
    )(x, mod, gmix, w_in, lbl, cw, cb, cst)


def _smp2(f, k, q, v, state):
    n = f.shape[0]
    info = plsc.get_sparse_core_info()
    lanes = info.num_lanes
    n_workers = info.num_cores * info.num_subcores
    tok_per = n // n_workers
    n_chunks = DK // lanes
    vec = lambda: pltpu.VMEM((DK,), F32)

    @pl.kernel(out_type=[jax.ShapeDtypeStruct(state.shape, F32), jax.ShapeDtypeStruct((n, KEY_W), F32)],
               mesh=_sc_mesh(), scratch_types=[pltpu.VMEM((DK, DK), F32), vec(), vec(), vec(), vec(), vec()],
               compiler_params=pltpu.CompilerParams(needs_layout_passes=False), name="smp2")
    def run(f_hbm, k_hbm, q_hbm, v_hbm, s_hbm, s_out_hbm, o_hbm, s_v, f_v, k_v, q_v, v_v, o_v):
        wid = lax.axis_index("subcore") * info.num_cores + lax.axis_index("core")

        def tile(j, carry):
            t = wid * tok_per + j // HEADS
            hcols = pl.ds((j % HEADS) * DK, DK)
            pltpu.sync_copy(s_hbm.at[t, j % HEADS], s_v)
            for src, dst in ((f_hbm, f_v), (k_hbm, k_v), (q_hbm, q_v), (v_hbm, v_v)):
                pltpu.sync_copy(src.at[t, hcols], dst)
            v_chunks = [v_v[pl.ds(c * lanes, lanes)] for c in range(n_chunks)]

            def row(d, acc):
                at = [jnp.zeros((lanes,), I32) + d]
                fd, kd, qd = (plsc.load_gather(r, at) for r in (f_v, k_v, q_v))
                out = []
                for c in range(n_chunks):
                    cols = pl.ds(c * lanes, lanes)
                    new = s_v[d, cols] * fd + kd * v_chunks[c]
                    s_v[d, cols] = new
                    out.append(acc[c] + qd * new)
                return tuple(out)

            acc = lax.fori_loop(0, DK, row, tuple(jnp.zeros((lanes,), F32) for _ in range(n_chunks)))
            for c in range(n_chunks):
                o_v[pl.ds(c * lanes, lanes)] = acc[c]
            pltpu.sync_copy(s_v, s_out_hbm.at[t, j % HEADS])
            pltpu.sync_copy(o_v, o_hbm.at[t, hcols])
            return carry

        lax.fori_loop(0, tok_per * HEADS, tile, 0)

    return run(f, k, q, v, state)


def _smp3_kernel(x_ref, mod_ref, o_ref, gate_ref, yb_ref, sga_ref, sgb_ref, hg_ref, gffn_ref,
                 w_oh_ref, w_oc_ref, w_o_ref, wr_hl_ref, w_sgu_ref, w_sd_ref,
                 h2_all_ref, lgt_all_ref, xmid_ref, h2_ref, lgt_ref):
    del h2_all_ref, lgt_all_ref
    parts = []
    for hd in range(HEADS):
        hs = slice(hd * DK, (hd + 1) * DK)
        parts.append(_rms(o_ref[:, hs]) * hg_ref[:, hs] * gate_ref[:, hs])
    ya = jnp.concatenate(parts, axis=1)
    mixed = (sga_ref[...] * _dot(ya.astype(BF16), w_oh_ref[...])
             + sgb_ref[...] * _dot(yb_ref[...].astype(BF16), w_oc_ref[...]))
    g1 = mod_ref[:, 2 * D_MODEL:3 * D_MODEL]
    x1 = x_ref[...] + g1 * _dot(mixed.astype(BF16), w_o_ref[...])
    mod_rows = tuple(mod_ref[:, j * D_MODEL:(j + 1) * D_MODEL] for j in (3, 4, 5))
    xmid, h2b, lgt = _ffn_pre(x1, mod_rows, gffn_ref[...], w_sgu_ref[...], w_sd_ref[...],
                              wr_hl_ref[...])
    xmid_ref[...] = xmid
    _pack_rows(h2b, h2_ref)
    lgt_ref[...] = lgt


def _smp3(x, mod, o, gate, yb, sga, sgb, hg, gffn, w_oh, w_oc, w_o, wr_hl, w_sgu, w_sd,
          h2_all, lgt_all, n_prompt):
    n = x.shape[0]
    vmem_args = [x, mod, o, gate, yb, sga, sgb, hg, gffn, w_oh, w_oc, w_o, wr_hl, w_sgu, w_sd]
    blk = n_prompt // n

    def full(a):
        nd = a.ndim
        return pl.BlockSpec(a.shape, lambda i, _nd=nd: (0,) * _nd)

    return pl.pallas_call(
        _smp3_kernel,
        grid=(1,),
        in_specs=[full(a) for a in vmem_args]
                 + [pl.BlockSpec(memory_space=pl.ANY), pl.BlockSpec(memory_space=pl.ANY)],
        out_specs=[pl.BlockSpec((n, D_MODEL), lambda i: (0, 0)),
                   pl.BlockSpec((2, n, PLANE_W), lambda i: (0, blk, 0)),
                   pl.BlockSpec((N_EXP, n), lambda i: (0, blk))],
        out_shape=[jax.ShapeDtypeStruct((n, D_MODEL), F32),
                   jax.ShapeDtypeStruct(h2_all.shape, h2_all.dtype),
                   jax.ShapeDtypeStruct(lgt_all.shape, lgt_all.dtype)],
        input_output_aliases={len(vmem_args): 1, len(vmem_args) + 1: 2},
        compiler_params=pltpu.CompilerParams(
            dimension_semantics=("arbitrary",), vmem_limit_bytes=VMEM_LIMIT),
        name="smp3",
    )(*vmem_args, h2_all, lgt_all)


def _route_kernel(lgt_ref, bias_ref, idx_ref, w_ref, rank_ref, cnt_ref):
    tr = ROUTE_TILE
    n_tiles = lgt_ref.shape[1] // tr

    def tile(i, carry):
        cols = pl.ds(pl.multiple_of(i * tr, tr), tr)
        picks, weights, ranks, carry = _route_tile(lgt_ref[:, cols], bias_ref[...], carry)
        for k in range(TOP_K):
            idx_ref[k:k + 1, cols] = picks[k]
            w_ref[k:k + 1, cols] = weights[k]
            rank_ref[k:k + 1, cols] = ranks[k]
        return carry

    total = lax.fori_loop(0, n_tiles, tile, jnp.zeros((N_EXP, 1), F32))
    cnt_ref[...] = jnp.broadcast_to(total, cnt_ref.shape).astype(I32)


def _route_tile(lgt, bias, carry):
    tr = lgt.shape[1]
    neg = -jnp.inf
    scores = _sigmoid(lgt)
    sel = scores + bias
    j8 = lax.broadcasted_iota(I32, (GRP_SZ, tr), 0)
    groups = [sel[g * GRP_SZ:(g + 1) * GRP_SZ] for g in range(N_GRP)]
    gscore = []
    for grp in groups:
        m1 = jnp.max(grp, axis=0, keepdims=True)
        i1 = jnp.min(jnp.where(grp == m1, j8, GRP_SZ), axis=0, keepdims=True)
        m2 = jnp.max(jnp.where(j8 == i1, neg, grp), axis=0, keepdims=True)
        gscore.append(m1 + m2)
    kept = []
    for g in range(N_GRP):
        beaten = jnp.zeros((1, tr), I32)
        for o in range(N_GRP):
            if o < g:
                beaten = beaten + (gscore[o] >= gscore[g]).astype(I32)
            elif o > g:
                beaten = beaten + (gscore[o] > gscore[g]).astype(I32)
        kept.append(jnp.where(beaten < TOPK_GRP, groups[g], neg))
    masked = jnp.concatenate(kept, axis=0)
    ei = lax.broadcasted_iota(I32, masked.shape, 0)
    chosen = jnp.zeros(masked.shape, jnp.bool_)
    picks, weights = [], []
    for _ in range(TOP_K):
        m = jnp.max(masked, axis=0, keepdims=True)
        pick = jnp.min(jnp.where(masked == m, ei, N_EXP), axis=0, keepdims=True)
        hit = ei == pick
        weights.append(jnp.sum(jnp.where(hit, scores, 0.0), axis=0, keepdims=True))
        picks.append(pick)
        chosen = chosen | hit
        masked = jnp.where(hit, neg, masked)
    wsum = weights[0]
    for w in weights[1:]:
        wsum = wsum + w
    sel01 = chosen.astype(F32)
    r = lax.broadcasted_iota(I32, (tr, tr), 0)
    c = lax.broadcasted_iota(I32, (tr, tr), 1)
    before = (r < c).astype(BF16)
    cnt = _dot(sel01.astype(BF16), before) + carry
    weights = [w / wsum * ROUTED_SCALE for w in weights]
    ranks = [jnp.sum(jnp.where(ei == p, cnt, 0.0), axis=0, keepdims=True).astype(I32) for p in picks]
    return picks, weights, ranks, carry + jnp.sum(sel01, axis=1, keepdims=True)


def _route(lgt, bias):
    n = lgt.shape[1]
    assert n % ROUTE_TILE == 0
    slot = lambda dt: jax.ShapeDtypeStruct((TOP_K, n), dt)
    return pl.pallas_call(
        _route_kernel,
        out_shape=[slot(I32), slot(F32), slot(I32),
                   jax.ShapeDtypeStruct((N_EXP, 128), I32)],
        name="route",
    )(lgt, bias)


def _dest_kernel(start_ref, idx_ref, rank_ref, all_ref, *chunk_refs, n_rows, chunks):
    n_tok = idx_ref.shape[1]
    idx = idx_ref[...]
    acc = rank_ref[...]
    for e in range(N_EXP):
        acc = acc + jnp.where(idx == e, start_ref[e], 0)
    for k in range(TOP_K):
        for p in range(2):
            row = acc[k:k + 1, :] + p * n_rows
            seg = 2 * k + p
            all_ref[:, seg * n_tok:(seg + 1) * n_tok] = row
            for (c0, c1), ref in zip(chunks, chunk_refs):
                ref[:, seg * (c1 - c0):(seg + 1) * (c1 - c0)] = row[:, c0:c1]


def _dest(pad_start, idx, rank, n_rows, chunks):
    k, n_tok = idx.shape
    vmem = pl.BlockSpec(memory_space=pltpu.VMEM)
    sizes = [n_tok] + [c1 - c0 for c0, c1 in chunks]
    return pl.pallas_call(
        functools.partial(_dest_kernel, n_rows=n_rows, chunks=chunks),
        in_specs=[pl.BlockSpec(memory_space=pltpu.SMEM), vmem, vmem],
        out_specs=[vmem] * len(sizes),
        out_shape=[jax.ShapeDtypeStruct((1, 2 * k * n), I32) for n in sizes],
        name="dest",
    )(pad_start, idx, rank)


def _sc_mesh():
    return plsc.VectorSubcoreMesh(core_axis_name="core", subcore_axis_name="subcore")


def _dispatch(rows, dest, n_out):
    n, width = rows.shape
    win = SC_WINDOW
    steps = n // win

    @pl.kernel(out_type=jax.ShapeDtypeStruct((n_out, width), rows.dtype), mesh=_sc_mesh(),
               scratch_types=[], name="dispatch")
    def run(x_hbm, *refs):
        i_hbms, o_hbm = refs[:TOP_K], refs[TOP_K]

        def body(x_vmem, *i_vmems):
            for i_vmem in i_vmems:
                pltpu.sync_copy(x_vmem, o_hbm.at[i_vmem.at[0]])

        pltpu.emit_pipeline(
            body,
            grid=(steps,),
            in_specs=[pl.BlockSpec((win, width), lambda i: (i, 0))]
                     + [pl.BlockSpec((1, win), lambda i, k=k: (0, k * steps + i)) for k in range(TOP_K)],
            out_specs=[],
            core_axis_name=("core", "subcore"),
            dimension_semantics=(pltpu.PARALLEL,),
        )(x_hbm, *i_hbms)

    assert dest.shape == (1, TOP_K * n)
    return run(rows, *([dest] * TOP_K))


def _combine(rows, dest_flat):
    width = rows.shape[1]
    n = dest_flat.shape[1]
    win = SC_WINDOW

    @pl.kernel(out_type=jax.ShapeDtypeStruct((n, width), rows.dtype), mesh=_sc_mesh(),
               scratch_types=[], name="combine")
    def run(y_hbm, i_hbm, o_hbm):
        def body(i_vmem, o_vmem):
            pltpu.sync_copy(y_hbm.at[i_vmem.at[0]], o_vmem)

        pltpu.emit_pipeline(
            body,
            grid=(n // win,),
            in_specs=[pl.BlockSpec((1, win), lambda i: (0, i))],
            out_specs=[pl.BlockSpec((win, width), lambda i: (i, 0))],
            core_axis_name=("core", "subcore"),
            dimension_semantics=(pltpu.PARALLEL,),
        )(i_hbm, o_hbm)

    return run(rows, dest_flat)


def _gmm_kernel(blk_exp_ref, n_used_ref, xs_hbm, wg_hbm, wu_hbm, wd_hbm, ys_hbm,
                xbuf, ybuf, wg32, wu32, wd32, wgu_b, wd_b, xsem, ysem, wsem, run_ref):
    nx, ny, bm = xbuf.shape[0], ybuf.shape[0], xbuf.shape[2]
    n_used = n_used_ref[0]

    def x_copies(b):
        rows, slot = pl.ds(b * bm, bm), b % nx
        return [pltpu.make_async_copy(xs_hbm.at[p, rows, :], xbuf.at[slot, p], xsem.at[slot, p]) for p in range(2)]

    def y_copies(b):
        rows, slot = pl.ds(b * bm, bm), b % ny
        return [pltpu.make_async_copy(ybuf.at[slot, p], ys_hbm.at[p, rows, :], ysem.at[slot, p]) for p in range(2)]

    def start(copies):
        for c in copies:
            c.start()

    def wait(copies):
        for c in copies:
            c.wait()

    def w_copies(e, slot):
        return (pltpu.make_async_copy(wg_hbm.at[e], wg32.at[slot], wsem.at[slot, 0]),
                pltpu.make_async_copy(wu_hbm.at[e], wu32.at[slot], wsem.at[slot, 1]),
                pltpu.make_async_copy(wd_hbm.at[e], wd32.at[slot], wsem.at[slot, 2]))

    def run_end(b):
        return lax.while_loop(lambda j: (j < n_used) & (blk_exp_ref[jnp.minimum(j, n_used - 1)] == blk_exp_ref[b]),
                              lambda j: j + 1, b + 1)

    run_ref[0] = 0
    start(x_copies(0))
    start(w_copies(blk_exp_ref[0], 0))
    for j in range(1, nx - 1):
        @pl.when(j < n_used)
        def _():
            start(x_copies(j))

    def block(b, carry):
        @pl.when(b + nx - 1 < n_used)
        def _():
            start(x_copies(b + nx - 1))

        @pl.when((b == 0) | (blk_exp_ref[b] != blk_exp_ref[jnp.maximum(b - 1, 0)]))
        def _():
            wslot = run_ref[0] % 2
            run_ref[0] = run_ref[0] + 1
            wait(w_copies(blk_exp_ref[b], wslot))
            wgu_b[:, 0:EXP_FF] = wg32[wslot].astype(BF16)
            wgu_b[:, EXP_FF:2 * EXP_FF] = wu32[wslot].astype(BF16)
            wd_b[...] = wd32[wslot].astype(BF16)
            nxt = run_end(b)

            @pl.when(nxt < n_used)
            def _():
                start(w_copies(blk_exp_ref[jnp.minimum(nxt, n_used - 1)], 1 - wslot))

        wait(x_copies(b))

        @pl.when(b >= ny)
        def _():
            wait(y_copies(b - ny))

        xslot = b % nx
        xc = _unpack_rows(xbuf[xslot, 0], xbuf[xslot, 1])
        gu = sum(_dot(c, wgu_b[i * PLANE_W:(i + 1) * PLANE_W, :]) for i, c in enumerate(xc))
        act = (_silu(gu[:, :EXP_FF]) * gu[:, EXP_FF:]).astype(BF16)
        _pack_rows(_dot(act, wd_b[...]).astype(BF16), ybuf.at[b % ny])
        start(y_copies(b))
        return carry

    lax.fori_loop(0, n_used, block, 0)

    for j in range(ny, 0, -1):
        @pl.when(n_used >= j)
        def _():
            wait(y_copies(n_used - j))


def _gmm(blk_exp, n_used, xs, w_gate, w_up, w_down):
    n_rows = xs.shape[1]
    bm = GMM_BM
    nb = n_rows // bm

    assert blk_exp.shape == (nb,)
    any_spec = pl.BlockSpec(memory_space=pl.ANY)
    grid_spec = pltpu.PrefetchScalarGridSpec(
        num_scalar_prefetch=2,
        grid=(1,),
        in_specs=[any_spec, any_spec, any_spec, any_spec],
        out_specs=any_spec,
        scratch_shapes=[pltpu.VMEM((GMM_NX, 2, bm, PLANE_W), I32), pltpu.VMEM((GMM_NY, 2, bm, PLANE_W), I32),
                        pltpu.VMEM((2, D_MODEL, EXP_FF), F32), pltpu.VMEM((2, D_MODEL, EXP_FF), F32),
                        pltpu.VMEM((2, EXP_FF, D_MODEL), F32),
                        pltpu.VMEM((D_MODEL, 2 * EXP_FF), BF16), pltpu.VMEM((EXP_FF, D_MODEL), BF16),
                        pltpu.SemaphoreType.DMA((GMM_NX, 2)), pltpu.SemaphoreType.DMA((GMM_NY, 2)),
                        pltpu.SemaphoreType.DMA((2, 3)), pltpu.SMEM((1,), I32)],
    )
    return pl.pallas_call(
        _gmm_kernel,
        grid_spec=grid_spec,
        out_shape=jax.ShapeDtypeStruct((2, n_rows, PLANE_W), I32),
        compiler_params=pltpu.CompilerParams(dimension_semantics=("arbitrary",)),
        name="gmm",
    )(blk_exp, n_used, xs, w_gate, w_up, w_down)


def _final_kernel(xmid_ref, g2_ref, z_ref, w_ref, gfin_ref, *rest):
    y_ref = rest[-1]
    accs = [jnp.zeros((xmid_ref.shape[0], PLANE_W), F32) for _ in range(4)]
    w_cols = w_ref[...].T
    for k in range(TOP_K):
        wk = w_cols[:, k:k + 1]
        cols = _unpack_rows(z_ref[k, 0], z_ref[k, 1])
        accs = [a + wk * c.astype(F32) for a, c in zip(accs, cols)]
    acc = jnp.concatenate(accs, axis=1)
    y_ref[...] = _rms(xmid_ref[...] + g2_ref[0] * acc) * gfin_ref[...]


def _final(xmid, g2, z, w_t, gfin, tile, *, n_tiles, x_tile0, z_tile0, w_tile0, tiles_per_g2, y_prev=None):
    args = [xmid, g2, z, w_t, gfin]
    in_specs = [pl.BlockSpec((tile, D_MODEL), lambda i: (x_tile0 + i, 0)),
                pl.BlockSpec((1, g2.shape[1], D_MODEL), lambda i: ((x_tile0 + i) // tiles_per_g2, 0, 0)),
                pl.BlockSpec((TOP_K, 2, tile, PLANE_W), lambda i: (0, 0, z_tile0 + i, 0)),
                pl.BlockSpec((TOP_K, tile), lambda i: (0, w_tile0 + i)),
                pl.BlockSpec((1, D_MODEL), lambda i: (0, 0))]
    aliases = {}
    if y_prev is not None:
        args.append(y_prev)
        in_specs.append(pl.BlockSpec(memory_space=pl.ANY))
        aliases = {len(args) - 1: 0}
    return pl.pallas_call(
        _final_kernel,
        grid=(n_tiles,),
        in_specs=in_specs,
        out_specs=pl.BlockSpec((tile, D_MODEL), lambda i: (x_tile0 + i, 0)),
        out_shape=jax.ShapeDtypeStruct(xmid.shape, F32),
        input_output_aliases=aliases,
        compiler_params=pltpu.CompilerParams(dimension_semantics=("arbitrary",)),
        name="final",
    )(*args)


def kernel(x_prompt, x_sample, state_hgrn, state_conv, c_prompt, c_sample, w_ada, b_ada, norm_mix_g, norm_ffn_g, w_in, lb_logits, hgrn_norm_g, conv_w, conv_b, w_out_hgrn, w_out_conv, w_o, w_router, router_bias, w_exp_gate, w_exp_up, w_exp_down, w_sh_gate, w_sh_up, w_sh_down, final_norm_g):
    assert w_ada.shape[0] == 1 and lb_logits.shape[0] == 2
    bsz, seq, _ = x_prompt.shape
    n_smp = x_sample.shape[0]
    n_prompt = bsz * seq
    n_tok = n_prompt + n_smp

    w_in_b = w_in[0].astype(BF16)
    w_oh_b = w_out_hgrn[0].astype(BF16)
    w_oc_b = w_out_conv[0].astype(BF16)
    w_o_b = w_o[0].astype(BF16)
    wr_t = w_router[0].T
    wr_hi = wr_t.astype(BF16)
    wr_hl = jnp.concatenate([wr_hi, (wr_t - wr_hi.astype(F32)).astype(BF16)], axis=0)
    w_sgu = jnp.concatenate([w_sh_gate[0], w_sh_up[0]], axis=1).astype(BF16)
    w_sd = w_sh_down[0].astype(BF16)
    gmix = norm_mix_g[0].reshape(1, D_MODEL)
    gffn = norm_ffn_g[0].reshape(1, D_MODEL)
    hg = hgrn_norm_g[0].reshape(1, KEY_W)
    cw = conv_w[0]
    cb = conv_b[0].reshape(1, CONV_W)
    gfin = final_norm_g.reshape(1, D_MODEL)

    mod_p, mod_s = _ada(c_prompt, c_sample, w_ada[0], b_ada[0])

    xmid_p, h2_all, lgt_all, s_p, cv_p = _mix(
        x_prompt, mod_p.reshape(bsz, 6, D_MODEL), n_tok, gmix, gffn, w_in_b, lb_logits, hg, cw, cb, w_oh_b, w_oc_b, w_o_b, wr_hl, w_sgu, w_sd)

    xs2 = x_sample.reshape(n_smp, D_MODEL)
    f, kk, q, v, gate, yb, sga, sgb, cv_s = _smp1(
        xs2, mod_s, gmix, w_in_b, lb_logits, cw, cb, state_conv[0].reshape(n_smp, (CONV_K - 1) * CONV_W))
    s_s, o_s = _smp2(f, kk, q, v, state_hgrn[0])
    xmid_s, h2_all, lgt_all = _smp3(xs2, mod_s, o_s, gate, yb, sga, sgb, hg, gffn,
                                    w_oh_b, w_oc_b, w_o_b, wr_hl, w_sgu, w_sd,
                                    h2_all, lgt_all, n_prompt)

    idx, w_tok, rank, cnt = _route(lgt_all, router_bias[0].reshape(N_EXP, 1))

    bm = GMM_BM
    n_blocks = (n_tok * TOP_K + N_EXP * (bm - 1)) // bm
    n_rows = n_blocks * bm
    counts = cnt[:, 0]
    padded = (counts + bm - 1) // bm * bm
    pad_end = jnp.cumsum(padded)
    pad_start = pad_end - padded
    blk_row0 = jnp.arange(n_blocks, dtype=I32) * bm
    blk_exp = jnp.minimum(jnp.sum((pad_end[None, :] <= blk_row0[:, None]).astype(I32), axis=1), N_EXP - 1)
    n_used = (pad_end[-1:] // bm).astype(I32)
    chunk = n_prompt // FINAL_CHUNKS
    chunks = tuple((c * chunk, n_tok if c == FINAL_CHUNKS - 1 else (c + 1) * chunk) for c in range(FINAL_CHUNKS))
    dest, *chunk_dest = _dest(pad_start.astype(I32), idx, rank, n_rows, chunks)

    xs = _dispatch(h2_all.reshape(2 * n_tok, PLANE_W), dest, 2 * n_rows).reshape(2, n_rows, PLANE_W)
    ys = _gmm(blk_exp, n_used, xs, w_exp_gate[0], w_exp_up[0], w_exp_down[0])

    ys_flat = ys.reshape(2 * n_rows, PLANE_W)
    w_t = w_tok
    xmid_p2 = xmid_p.reshape(n_prompt, D_MODEL)
    g2_p = mod_p[:, 5 * D_MODEL:].reshape(bsz, 1, D_MODEL)
    g2_s = mod_s[:, 5 * D_MODEL:].reshape(1, n_smp, D_MODEL)
    y_p = None
    for (c0, c1), dest_c in zip(chunks, chunk_dest):
        z = _combine(ys_flat, dest_c).reshape(TOP_K, 2, c1 - c0, PLANE_W)
        y_p = _final(xmid_p2, g2_p, z, w_t, gfin, FINAL_TILE, n_tiles=chunk // FINAL_TILE,
                     x_tile0=c0 // FINAL_TILE, z_tile0=0, w_tile0=c0 // FINAL_TILE,
                     tiles_per_g2=seq // FINAL_TILE, y_prev=y_p)
    y_s = _final(xmid_s, g2_s, z, w_t, gfin, n_smp, n_tiles=1, x_tile0=0, z_tile0=chunk // n_smp,
                 w_tile0=n_prompt // n_smp, tiles_per_g2=1)

    return (y_p.reshape(bsz, seq, D_MODEL), y_s.reshape(n_smp, 1, D_MODEL),
            s_p[None], cv_p[None], s_s[None], cv_s.reshape(1, n_smp, CONV_K - 1, CONV_W))
```

```python
import functools

import jax
import jax.numpy as jnp
from jax import lax
from jax.experimental import pallas as pl
from jax.experimental.pallas import tpu as pltpu
from jax.experimental.pallas import tpu_sc as plsc

F32 = jnp.float32
BF16 = jnp.bfloat16
I32 = jnp.int32

D_MODEL = 1024
HALF_D = D_MODEL // 2
HEADS = 4
DK = 128
KEY_W = HEADS * DK
CONV_W = 512
CONV_K = 3
IN_W = 2 * KEY_W + 2 * KEY_W + 3 * CONV_W + 2 * D_MODEL
N_EXP = 64
TOP_K = 8
N_GRP = 8
GRP_SZ = N_EXP // N_GRP
TOPK_GRP = 4
EXP_FF = 256
SH_FF = 256
ROUTED_SCALE = 2.5
EPS = 1e-6

C_Q, C_F, C_I, C_G = 0, 512, 1024, 1536
C_BB, C_CC, C_VB = 2048, 2560, 3072
C_MGA, C_MGB = 3584, 4608

MIX_TILE = 512
SUB = 256
CHUNK = 64
ROUTE_TILE = 384
GMM_BM = 512
GMM_NX = 4
GMM_NY = 3
FINAL_TILE = 512
FINAL_SPLIT = (4, 12, 22, 32)
SC_WINDOW = 128
PLANE_W = HALF_D // 2
VMEM_LIMIT = 56 * 1024 * 1024


def _dot(a, b):
    return jnp.dot(a, b, preferred_element_type=F32)


def _dot_nt(a, b):
    return lax.dot_general(a, b, (((1,), (1,)), ((), ())), preferred_element_type=F32)


def _dot_tn(a, b):
    return lax.dot_general(a, b, (((0,), (0,)), ((), ())), preferred_element_type=F32)


def _sigmoid(x):
    return 0.5 * jnp.tanh(0.5 * x) + 0.5


def _silu(x):
    h = 0.5 * x
    return h * jnp.tanh(h) + h


def _rms(x):
    return x * lax.rsqrt(jnp.mean(x * x, axis=-1, keepdims=True) + EPS)


def _lower_bound(lbl):
    a, b = lbl[0:1], lbl[1:2]
    m = jnp.maximum(a, b)
    ea, eb = jnp.exp(a - m), jnp.exp(b - m)
    return ea / (ea + eb)


def _split3(x):
    hi = x.astype(BF16)
    r1 = x - hi.astype(F32)
    mid = r1.astype(BF16)
    lo = (r1 - mid.astype(F32)).astype(BF16)
    return hi, mid, lo


def _words(lo_b, hi_b):
    lo = lax.shift_right_logical(lax.bitcast_convert_type(lo_b.astype(F32), I32), 16)
    hi = lax.bitcast_convert_type(hi_b.astype(F32), I32) & jnp.int32(-65536)
    return lo | hi


def _halves(w):
    lo = lax.bitcast_convert_type(lax.shift_left(w, 16), F32)
    hi = lax.bitcast_convert_type(w & jnp.int32(-65536), F32)
    return lo.astype(BF16), hi.astype(BF16)


def _pack_rows(xb, out_ref):
    words = _words(xb[:, :HALF_D], xb[:, HALF_D:])
    out_ref[0] = words[:, :PLANE_W]
    out_ref[1] = words[:, PLANE_W:]


def _unpack_rows(p0, p1):
    c0, c2 = _halves(p0)
    c1, c3 = _halves(p1)
    return c0, c1, c2, c3


def _ada_kernel(cp_ref, cs_ref, w_ref, b_ref, op_ref, os_ref):
    w = w_ref[...].astype(BF16)
    for c_ref, o_ref in ((cp_ref, op_ref), (cs_ref, os_ref)):
        o_ref[...] = _dot(_silu(c_ref[...]).astype(BF16), w) + b_ref[...]


def _ada(c_prompt, c_sample, w_ada, b_ada):
    blk = 1024
    rows = lambda c: pl.BlockSpec((c.shape[0], D_MODEL), lambda j: (0, 0))
    cols = lambda c: pl.BlockSpec((c.shape[0], blk), lambda j: (0, j))
    return pl.pallas_call(
        _ada_kernel,
        grid=(6 * D_MODEL // blk,),
        in_specs=[rows(c_prompt), rows(c_sample),
                  pl.BlockSpec((D_MODEL, blk), lambda j: (0, j)),
                  pl.BlockSpec((1, blk), lambda j: (0, j))],
        out_specs=[cols(c_prompt), cols(c_sample)],
        out_shape=[jax.ShapeDtypeStruct((c.shape[0], 6 * D_MODEL), F32) for c in (c_prompt, c_sample)],
        name="ada",
    )(c_prompt, c_sample, w_ada, b_ada.reshape(1, -1))


def _ffn_pre(x1, mod_rows, gffn, w_sgu, w_sd, wr_hl):
    sh2, sc2, g2 = mod_rows
    h2 = _rms(x1) * gffn * (1.0 + sc2) + sh2
    h2b = h2.astype(BF16)
    gu = _dot(h2b, w_sgu)
    act = _silu(gu[:, :SH_FF]) * gu[:, SH_FF:]
    xmid = x1 + g2 * _dot(act.astype(BF16), w_sd)
    h2lo = (h2 - h2b.astype(F32)).astype(BF16)
    both = _dot_nt(wr_hl, h2b)
    lgt = both[:N_EXP] + both[N_EXP:] + _dot_nt(wr_hl[:N_EXP], h2lo)
    return xmid, h2b, lgt


def _mix_kernel(*refs, nt, n_tiles):
    i = pl.program_id(0)
    h2_ref, lgt_ref = refs[16], refs[17]

    @pl.when(i == n_tiles)
    def _():
        h2_ref[...] = jnp.zeros_like(h2_ref)
        lgt_ref[...] = jnp.zeros_like(lgt_ref)

    @pl.when(i < n_tiles)
    def _():
        _mix_tile(i % nt, nt, *refs)


def _mix_tile(t, nt, x_ref, mod_ref, gmix_ref, gffn_ref, w_in_ref, lbl_ref, hg_ref, cw_ref, cb_ref,
              w_oh_ref, w_oc_ref, w_o_ref, wr_hl_ref, w_sgu_ref, w_sd_ref,
              xmid_ref, h2_ref, lgt_ref, s_out_ref, cv_out_ref,
              proj_ref, st_ref, cbuf_ref, ya_ref):
    tt = x_ref.shape[1]

    @pl.when(t == 0)
    def _():
        st_ref[...] = jnp.zeros_like(st_ref)
        cbuf_ref[...] = jnp.zeros_like(cbuf_ref)

    x = x_ref[0]
    mod = mod_ref[0]
    sh1, sc1, g1 = mod[0:1], mod[1:2], mod[2:3]
    h = _rms(x) * gmix_ref[...] * (1.0 + sc1) + sh1
    hb = h.astype(BF16)
    for c in range(0, IN_W, 512):
        proj_ref[:, c:c + 512] = _dot(hb, w_in_ref[:, c:c + 512])

    lb = _lower_bound(lbl_ref[...])
    row = lax.broadcasted_iota(I32, (SUB, SUB), 0)
    col = lax.broadcasted_iota(I32, (SUB, SUB), 1)
    tri = (col <= row).astype(BF16)
    mask_d = (col <= row) & (row // CHUNK == col // CHUNK)
    mask_a = row // (2 * CHUNK) == col // (2 * CHUNK)
    n_ch = SUB // CHUNK

    def by_chunk(vals):
        return jnp.concatenate([jnp.zeros((CHUNK, DK), F32) if v is None
                                else jnp.broadcast_to(v, (CHUNK, DK)) for v in vals], axis=0)

    for s in range(tt // SUB):
        r0 = s * SUB
        f = lb + (1.0 - lb) * _sigmoid(proj_ref[r0:r0 + SUB, C_F:C_F + KEY_W])
        kk = 1.0 - f
        hi, mid, lo = _split3(jnp.log(f))
        bc = _dot(tri, hi) + _dot(tri, mid) + _dot(tri, lo)
        for hd in range(HEADS):
            hs = slice(hd * DK, (hd + 1) * DK)
            bh = bc[:, hs]
            at = lambda r: bh[r:r + 1]
            mids = [at(c * CHUNK + CHUNK // 2 - 1) for c in range(n_ch)]
            pair_mid = [at(CHUNK - 1), at(3 * CHUNK - 1)]
            step_mid, step_end = at(2 * CHUNK - 1), at(SUB - 1)
            arg = bh - by_chunk(mids)
            e_pos, e_neg = jnp.exp(arg), jnp.exp(-arg)
            q = _silu(proj_ref[r0:r0 + SUB, C_Q + hd * DK:C_Q + (hd + 1) * DK])
            v = proj_ref[r0:r0 + SUB, C_I + hd * DK:C_I + (hd + 1) * DK]
            qd = q * e_pos
            kd = kk[:, hs] * e_neg
            q_in = qd * by_chunk([jnp.exp(m) for m in mids])
            k_end = kd * by_chunk([jnp.exp(step_end - m) for m in mids])
            qa = qd * by_chunk([None, jnp.exp(mids[1] - pair_mid[0]), None, jnp.exp(mids[3] - pair_mid[1])])
            ka = kd * by_chunk([jnp.exp(pair_mid[0] - mids[0]), None, jnp.exp(pair_mid[1] - mids[2]), None])
            qb = qd * by_chunk([None, None, jnp.exp(mids[2] - step_mid), jnp.exp(mids[3] - step_mid)])
            kb = kd * by_chunk([jnp.exp(step_mid - mids[0]), jnp.exp(step_mid - mids[1]), None, None])
            att = jnp.where(mask_d, _dot_nt(qd.astype(BF16), kd.astype(BF16)), 0.0)
            att = att + jnp.where(mask_a, _dot_nt(qa.astype(BF16), ka.astype(BF16)), 0.0)
            att = att + _dot_nt(qb.astype(BF16), kb.astype(BF16))
            vb = v.astype(BF16)
            st = st_ref[hd]
            o = _dot(att.astype(BF16), vb) + _dot_nt(q_in.astype(BF16), st.astype(BF16))
            st_ref[hd] = st * jnp.exp(step_end) + _dot_tn(vb, k_end.astype(BF16))
            gate = _silu(proj_ref[r0:r0 + SUB, C_G + hd * DK:C_G + (hd + 1) * DK])
            ya_ref[r0:r0 + SUB, hs] = _rms(o) * hg_ref[:, hs] * gate

    u = proj_ref[:, C_CC:C_CC + CONV_W] * proj_ref[:, C_VB:C_VB + CONV_W]
    rows = lax.broadcasted_iota(I32, (tt, CONV_W), 0)
    c0, c1 = cbuf_ref[0:1], cbuf_ref[1:2]
    u1 = jnp.where(rows == 0, c1, pltpu.roll(u, 1, axis=0))
    u2 = jnp.where(rows == 0, c0, jnp.where(rows == 1, c1, pltpu.roll(u, 2, axis=0)))
    conv = cw_ref[0:1] * u2 + cw_ref[1:2] * u1 + cw_ref[2:3] * u + cb_ref[...]
    yb = proj_ref[:, C_BB:C_BB + CONV_W] * conv
    cbuf_ref[...] = u[tt - 2:tt]

    mixed = (_sigmoid(proj_ref[:, C_MGA:C_MGA + D_MODEL]) * _dot(ya_ref[...].astype(BF16), w_oh_ref[...])
             + _sigmoid(proj_ref[:, C_MGB:C_MGB + D_MODEL]) * _dot(yb.astype(BF16), w_oc_ref[...]))
    x1 = x + g1 * _dot(mixed.astype(BF16), w_o_ref[...])

    xmid, h2b, lgt = _ffn_pre(x1, (mod[3:4], mod[4:5], mod[5:6]), gffn_ref[...],
                              w_sgu_ref[...], w_sd_ref[...], wr_hl_ref[...])
    xmid_ref[0] = xmid
    _pack_rows(h2b, h2_ref)
    lgt_ref[...] = lgt

    @pl.when(t == nt - 1)
    def _():
        for hd in range(HEADS):
            s_out_ref[0, hd] = st_ref[hd].T
        cv_out_ref[0] = cbuf_ref[...]


def _const_spec(shape):
    nd = len(shape)
    return pl.BlockSpec(shape, lambda i, _nd=nd: (0,) * _nd, pipeline_mode=pl.Buffered(1))


def _mix(x, mod, n_tok, gmix, gffn, w_in, lbl, hg, cw, cb, w_oh, w_oc, w_o, wr_hl, w_sgu, w_sd):
    bsz, seq, _ = x.shape
    tt = MIX_TILE
    nt = seq // tt
    n_tiles = bsz * nt
    assert n_tiles * tt < n_tok <= (n_tiles + 1) * tt
    consts = [gmix, gffn, w_in, lbl, hg, cw, cb, w_oh, w_oc, w_o, wr_hl, w_sgu, w_sd]
    tile = lambda i: jnp.minimum(i, n_tiles - 1)
    return pl.pallas_call(
        functools.partial(_mix_kernel, nt=nt, n_tiles=n_tiles),
        grid=(n_tiles + 1,),
        in_specs=[pl.BlockSpec((1, tt, D_MODEL), lambda i: (tile(i) // nt, tile(i) % nt, 0)),
                  pl.BlockSpec((1, 6, D_MODEL), lambda i: (tile(i) // nt, 0, 0))]
                 + [_const_spec(a.shape) for a in consts],
        out_specs=[pl.BlockSpec((1, tt, D_MODEL), lambda i: (tile(i) // nt, tile(i) % nt, 0)),
                   pl.BlockSpec((2, tt, PLANE_W), lambda i: (0, i, 0)),
                   pl.BlockSpec((N_EXP, tt), lambda i: (0, i)),
                   pl.BlockSpec((1, HEADS, DK, DK), lambda i: (tile(i) // nt, 0, 0, 0)),
                   pl.BlockSpec((1, CONV_K - 1, CONV_W), lambda i: (tile(i) // nt, 0, 0))],
        out_shape=[jax.ShapeDtypeStruct((bsz, seq, D_MODEL), F32),
                   jax.ShapeDtypeStruct((2, n_tok, PLANE_W), I32),
                   jax.ShapeDtypeStruct((N_EXP, n_tok), F32),
                   jax.ShapeDtypeStruct((bsz, HEADS, DK, DK), F32),
                   jax.ShapeDtypeStruct((bsz, CONV_K - 1, CONV_W), F32)],
        scratch_shapes=[pltpu.VMEM((tt, IN_W), F32),
                        pltpu.VMEM((HEADS, DK, DK), F32),
                        pltpu.VMEM((CONV_K - 1, CONV_W), F32),
                        pltpu.VMEM((tt, KEY_W), F32)],
        compiler_params=pltpu.CompilerParams(
            dimension_semantics=("arbitrary",), vmem_limit_bytes=VMEM_LIMIT),
        name="mix",
    )(x, mod, *consts)


def _smp1_kernel(x_ref, mod_ref, gmix_ref, w_in_ref, lbl_ref, cw_ref, cb_ref, cst_ref,
                 f_ref, k_ref, q_ref, v_ref, gate_ref, yb_ref, sga_ref, sgb_ref, cv_out_ref):
    x = x_ref[...]
    sh1, sc1 = mod_ref[:, 0:D_MODEL], mod_ref[:, D_MODEL:2 * D_MODEL]
    h = _rms(x) * gmix_ref[...] * (1.0 + sc1) + sh1
    hb = h.astype(BF16)

    def proj(c, w):
        return _dot(hb, w_in_ref[:, c:c + w])

    lb = _lower_bound(lbl_ref[...])
    f = lb + (1.0 - lb) * _sigmoid(proj(C_F, KEY_W))
    f_ref[...] = f
    k_ref[...] = 1.0 - f
    q_ref[...] = _silu(proj(C_Q, KEY_W))
    v_ref[...] = proj(C_I, KEY_W)
    gate_ref[...] = _silu(proj(C_G, KEY_W))
    u = proj(C_CC, CONV_W) * proj(C_VB, CONV_W)
    c0, c1 = cst_ref[:, 0:CONV_W], cst_ref[:, CONV_W:2 * CONV_W]
    conv = cw_ref[0:1] * c0 + cw_ref[1:2] * c1 + cw_ref[2:3] * u + cb_ref[...]
    yb_ref[...] = proj(C_BB, CONV_W) * conv
    cv_out_ref[:, 0:CONV_W] = c1
    cv_out_ref[:, CONV_W:2 * CONV_W] = u
    sga_ref[...] = _sigmoid(proj(C_MGA, D_MODEL))
    sgb_ref[...] = _sigmoid(proj(C_MGB, D_MODEL))


def _smp1(x, mod, gmix, w_in, lbl, cw, cb, cst):
    n = x.shape[0]
    kw = jax.ShapeDtypeStruct((n, KEY_W), F32)
    dm = jax.ShapeDtypeStruct((n, D_MODEL), F32)
    return pl.pallas_call(
        _smp1_kernel,
        out_shape=[kw, kw, kw, kw, kw, kw, dm, dm,
                   jax.ShapeDtypeStruct((n, (CONV_K - 1) * CONV_W), F32)],
        compiler_params=pltpu.CompilerParams(vmem_limit_bytes=VMEM_LIMIT),
        name="smp1",
    )(x, mod, gmix, w_in, lbl, cw, cb, cst)


def _smp2(f, k, q, v, state):
    n = f.shape[0]
    info = plsc.get_sparse_core_info()
    lanes = info.num_lanes
    n_workers = info.num_cores * info.num_subcores
    tok_per = n // n_workers
    n_chunks = DK // lanes
    vec = lambda: pltpu.VMEM((DK,), F32)

    @pl.kernel(out_type=[jax.ShapeDtypeStruct(state.shape, F32), jax.ShapeDtypeStruct((n, KEY_W), F32)],
               mesh=_sc_mesh(), scratch_types=[pltpu.VMEM((DK, DK), F32), vec(), vec(), vec(), vec(), vec()],
               compiler_params=pltpu.CompilerParams(needs_layout_passes=False), name="smp2")
    def run(f_hbm, k_hbm, q_hbm, v_hbm, s_hbm, s_out_hbm, o_hbm, s_v, f_v, k_v, q_v, v_v, o_v):
        wid = lax.axis_index("subcore") * info.num_cores + lax.axis_index("core")

        def tile(j, carry):
            t = wid * tok_per + j // HEADS
            hcols = pl.ds((j % HEADS) * DK, DK)
            pltpu.sync_copy(s_hbm.at[t, j % HEADS], s_v)
            for src, dst in ((f_hbm, f_v), (k_hbm, k_v), (q_hbm, q_v), (v_hbm, v_v)):
                pltpu.sync_copy(src.at[t, hcols], dst)
            v_chunks = [v_v[pl.ds(c * lanes, lanes)] for c in range(n_chunks)]

            def row(d, acc):
                at = [jnp.zeros((lanes,), I32) + d]
                fd, kd, qd = (plsc.load_gather(r, at) for r in (f_v, k_v, q_v))
                out = []
                for c in range(n_chunks):
                    cols = pl.ds(c * lanes, lanes)
                    new = s_v[d, cols] * fd + kd * v_chunks[c]
                    s_v[d, cols] = new
                    out.append(acc[c] + qd * new)
                return tuple(out)

            acc = lax.fori_loop(0, DK, row, tuple(jnp.zeros((lanes,), F32) for _ in range(n_chunks)))
            for c in range(n_chunks):
                o_v[pl.ds(c * lanes, lanes)] = acc[c]
            pltpu.sync_copy(s_v, s_out_hbm.at[t, j % HEADS])
            pltpu.sync_copy(o_v, o_hbm.at[t, hcols])
            return carry

        lax.fori_loop(0, tok_per * HEADS, tile, 0)

    return run(f, k, q, v, state)


def _smp3_kernel(x_ref, mod_ref, o_ref, gate_ref, yb_ref, sga_ref, sgb_ref, hg_ref, gffn_ref,
                 w_oh_ref, w_oc_ref, w_o_ref, wr_hl_ref, w_sgu_ref, w_sd_ref,
                 h2_all_ref, lgt_all_ref, xmid_ref, h2_ref, lgt_ref):
    del h2_all_ref, lgt_all_ref
    parts = []
    for hd in range(HEADS):
        hs = slice(hd * DK, (hd + 1) * DK)
        parts.append(_rms(o_ref[:, hs]) * hg_ref[:, hs] * gate_ref[:, hs])
    ya = jnp.concatenate(parts, axis=1)
    mixed = (sga_ref[...] * _dot(ya.astype(BF16), w_oh_ref[...])
             + sgb_ref[...] * _dot(yb_ref[...].astype(BF16), w_oc_ref[...]))
    g1 = mod_ref[:, 2 * D_MODEL:3 * D_MODEL]
    x1 = x_ref[...] + g1 * _dot(mixed.astype(BF16), w_o_ref[...])
    mod_rows = tuple(mod_ref[:, j * D_MODEL:(j + 1) * D_MODEL] for j in (3, 4, 5))
    xmid, h2b, lgt = _ffn_pre(x1, mod_rows, gffn_ref[...], w_sgu_ref[...], w_sd_ref[...],
                              wr_hl_ref[...])
    xmid_ref[...] = xmid
    _pack_rows(h2b, h2_ref)
    lgt_ref[...] = lgt


def _smp3(x, mod, o, gate, yb, sga, sgb, hg, gffn, w_oh, w_oc, w_o, wr_hl, w_sgu, w_sd,
          h2_all, lgt_all, n_prompt):
    n = x.shape[0]
    vmem_args = [x, mod, o, gate, yb, sga, sgb, hg, gffn, w_oh, w_oc, w_o, wr_hl, w_sgu, w_sd]
    blk = n_prompt // n

    def full(a):
        nd = a.ndim
        return pl.BlockSpec(a.shape, lambda i, _nd=nd: (0,) * _nd)

    return pl.pallas_call(
        _smp3_kernel,
        grid=(1,),
        in_specs=[full(a) for a in vmem_args]
                 + [pl.BlockSpec(memory_space=pl.ANY), pl.BlockSpec(memory_space=pl.ANY)],
        out_specs=[pl.BlockSpec((n, D_MODEL), lambda i: (0, 0)),
                   pl.BlockSpec((2, n, PLANE_W), lambda i: (0, blk, 0)),
                   pl.BlockSpec((N_EXP, n), lambda i: (0, blk))],
        out_shape=[jax.ShapeDtypeStruct((n, D_MODEL), F32),
                   jax.ShapeDtypeStruct(h2_all.shape, h2_all.dtype),
                   jax.ShapeDtypeStruct(lgt_all.shape, lgt_all.dtype)],
        input_output_aliases={len(vmem_args): 1, len(vmem_args) + 1: 2},
        compiler_params=pltpu.CompilerParams(
            dimension_semantics=("arbitrary",), vmem_limit_bytes=VMEM_LIMIT),
        name="smp3",
    )(*vmem_args, h2_all, lgt_all)


def _route_kernel(lgt_ref, bias_ref, idx_ref, w_ref, rank_ref, cnt_ref):
    tr = ROUTE_TILE
    n_tiles = lgt_ref.shape[1] // tr

    def tile(i, carry):
        cols = pl.ds(pl.multiple_of(i * tr, tr), tr)
        picks, weights, ranks, carry = _route_tile(lgt_ref[:, cols], bias_ref[...], carry)
        for k in range(TOP_K):
            idx_ref[k:k + 1, cols] = picks[k]
            w_ref[k:k + 1, cols] = weights[k]
            rank_ref[k:k + 1, cols] = ranks[k]
        return carry

    total = lax.fori_loop(0, n_tiles, tile, jnp.zeros((N_EXP, 1), F32))
    cnt_ref[...] = jnp.broadcast_to(total, cnt_ref.shape).astype(I32)


def _route_tile(lgt, bias, carry):
    tr = lgt.shape[1]
    neg = -jnp.inf
    scores = _sigmoid(lgt)
    sel = scores + bias
    j8 = lax.broadcasted_iota(I32, (GRP_SZ, tr), 0)
    groups = [sel[g * GRP_SZ:(g + 1) * GRP_SZ] for g in range(N_GRP)]
    gscore = []
    for grp in groups:
        m1 = jnp.max(grp, axis=0, keepdims=True)
        i1 = jnp.min(jnp.where(grp == m1, j8, GRP_SZ), axis=0, keepdims=True)
        m2 = jnp.max(jnp.where(j8 == i1, neg, grp), axis=0, keepdims=True)
        gscore.append(m1 + m2)
    kept = []
    for g in range(N_GRP):
        beaten = jnp.zeros((1, tr), I32)
        for o in range(N_GRP):
            if o < g:
                beaten = beaten + (gscore[o] >= gscore[g]).astype(I32)
            elif o > g:
                beaten = beaten + (gscore[o] > gscore[g]).astype(I32)
        kept.append(jnp.where(beaten < TOPK_GRP, groups[g], neg))
    masked = jnp.concatenate(kept, axis=0)
    ei = lax.broadcasted_iota(I32, masked.shape, 0)
    chosen = jnp.zeros(masked.shape, jnp.bool_)
    picks, weights = [], []
    for _ in range(TOP_K):
        m = jnp.max(masked, axis=0, keepdims=True)
        pick = jnp.min(jnp.where(masked == m, ei, N_EXP), axis=0, keepdims=True)
        hit = ei == pick
        weights.append(jnp.sum(jnp.where(hit, scores, 0.0), axis=0, keepdims=True))
        picks.append(pick)
        chosen = chosen | hit
        masked = jnp.where(hit, neg, masked)
    wsum = weights[0]
    for w in weights[1:]:
        wsum = wsum + w
    sel01 = chosen.astype(F32)
    r = lax.broadcasted_iota(I32, (tr, tr), 0)
    c = lax.broadcasted_iota(I32, (tr, tr), 1)
    before = (r < c).astype(BF16)
    cnt = _dot(sel01.astype(BF16), before) + carry
    weights = [w / wsum * ROUTED_SCALE for w in weights]
    ranks = [jnp.sum(jnp.where(ei == p, cnt, 0.0), axis=0, keepdims=True).astype(I32) for p in picks]
    return picks, weights, ranks, carry + jnp.sum(sel01, axis=1, keepdims=True)


def _route(lgt, bias):
    n = lgt.shape[1]
    assert n % ROUTE_TILE == 0
    slot = lambda dt: jax.ShapeDtypeStruct((TOP_K, n), dt)
    return pl.pallas_call(
        _route_kernel,
        out_shape=[slot(I32), slot(F32), slot(I32),
                   jax.ShapeDtypeStruct((N_EXP, 128), I32)],
        name="route",
    )(lgt, bias)


def _dest_kernel(start_ref, idx_ref, rank_ref, all_ref, *chunk_refs, n_rows, chunks):
    n_tok = idx_ref.shape[1]
    idx = idx_ref[...]
    acc = rank_ref[...]
    for e in range(N_EXP):
        acc = acc + jnp.where(idx == e, start_ref[e], 0)
    for k in range(TOP_K):
        for p in range(2):
            row = acc[k:k + 1, :] + p * n_rows
            seg = 2 * k + p
            all_ref[:, seg * n_tok:(seg + 1) * n_tok] = row
            for (c0, c1), ref in zip(chunks, chunk_refs):
                ref[:, seg * (c1 - c0):(seg + 1) * (c1 - c0)] = row[:, c0:c1]


def _dest(pad_start, idx, rank, n_rows, chunks):
    k, n_tok = idx.shape
    vmem = pl.BlockSpec(memory_space=pltpu.VMEM)
    sizes = [n_tok] + [c1 - c0 for c0, c1 in chunks]
    return pl.pallas_call(
        functools.partial(_dest_kernel, n_rows=n_rows, chunks=chunks),
        in_specs=[pl.BlockSpec(memory_space=pltpu.SMEM), vmem, vmem],
        out_specs=[vmem] * len(sizes),
        out_shape=[jax.ShapeDtypeStruct((1, 2 * k * n), I32) for n in sizes],
        name="dest",
    )(pad_start, idx, rank)


def _sc_mesh():
    return plsc.VectorSubcoreMesh(core_axis_name="core", subcore_axis_name="subcore")


def _dispatch(rows, dest, n_out):
    n, width = rows.shape
    win = SC_WINDOW
    steps = n // win

    @pl.kernel(out_type=jax.ShapeDtypeStruct((n_out, width), rows.dtype), mesh=_sc_mesh(),
               scratch_types=[], name="dispatch")
    def run(x_hbm, *refs):
        i_hbms, o_hbm = refs[:TOP_K], refs[TOP_K]

        def body(x_vmem, *i_vmems):
            for i_vmem in i_vmems:
                pltpu.sync_copy(x_vmem, o_hbm.at[i_vmem.at[0]])

        pltpu.emit_pipeline(
            body,
            grid=(steps,),
            in_specs=[pl.BlockSpec((win, width), lambda i: (i, 0))]
                     + [pl.BlockSpec((1, win), lambda i, k=k: (0, k * steps + i)) for k in range(TOP_K)],
            out_specs=[],
            core_axis_name=("core", "subcore"),
            dimension_semantics=(pltpu.PARALLEL,),
        )(x_hbm, *i_hbms)

    assert dest.shape == (1, TOP_K * n)
    return run(rows, *([dest] * TOP_K))


def _combine(rows, dest_flat):
    width = rows.shape[1]
    n = dest_flat.shape[1]
    win = SC_WINDOW

    @pl.kernel(out_type=jax.ShapeDtypeStruct((n, width), rows.dtype), mesh=_sc_mesh(),
               scratch_types=[], name="combine")
    def run(y_hbm, i_hbm, o_hbm):
        def body(i_vmem, o_vmem):
            pltpu.sync_copy(y_hbm.at[i_vmem.at[0]], o_vmem)

        pltpu.emit_pipeline(
            body,
            grid=(n // win,),
            in_specs=[pl.BlockSpec((1, win), lambda i: (0, i))],
            out_specs=[pl.BlockSpec((win, width), lambda i: (i, 0))],
            core_axis_name=("core", "subcore"),
            dimension_semantics=(pltpu.PARALLEL,),
        )(i_hbm, o_hbm)

    return run(rows, dest_flat)


def _gmm_kernel(blk_exp_ref, n_used_ref, xs_hbm, wg_hbm, wu_hbm, wd_hbm, ys_hbm,
                xbuf, ybuf, wg32, wu32, wd32, wgu_b, wd_b, xsem, ysem, wsem, run_ref):
    nx, ny, bm = xbuf.shape[0], ybuf.shape[0], xbuf.shape[2]
    n_used = n_used_ref[0]

    def x_copies(b):
        rows, slot = pl.ds(b * bm, bm), b % nx
        return [pltpu.make_async_copy(xs_hbm.at[p, rows, :], xbuf.at[slot, p], xsem.at[slot, p]) for p in range(2)]

    def y_copies(b):
        rows, slot = pl.ds(b * bm, bm), b % ny
        return [pltpu.make_async_copy(ybuf.at[slot, p], ys_hbm.at[p, rows, :], ysem.at[slot, p]) for p in range(2)]

    def start(copies):
        for c in copies:
            c.start()

    def wait(copies):
        for c in copies:
            c.wait()

    def w_copies(e, slot):
        return (pltpu.make_async_copy(wg_hbm.at[e], wg32.at[slot], wsem.at[slot, 0]),
                pltpu.make_async_copy(wu_hbm.at[e], wu32.at[slot], wsem.at[slot, 1]),
                pltpu.make_async_copy(wd_hbm.at[e], wd32.at[slot], wsem.at[slot, 2]))

    def run_end(b):
        return lax.while_loop(lambda j: (j < n_used) & (blk_exp_ref[jnp.minimum(j, n_used - 1)] == blk_exp_ref[b]),
                              lambda j: j + 1, b + 1)

    run_ref[0] = 0
    start(x_copies(0))
    start(w_copies(blk_exp_ref[0], 0))
    for j in range(1, nx - 1):
        @pl.when(j < n_used)
        def _():
            start(x_copies(j))

    def block(b, carry):
        @pl.when(b + nx - 1 < n_used)
        def _():
            start(x_copies(b + nx - 1))

        @pl.when((b == 0) | (blk_exp_ref[b] != blk_exp_ref[jnp.maximum(b - 1, 0)]))
        def _():
            wslot = run_ref[0] % 2
            run_ref[0] = run_ref[0] + 1
            wait(w_copies(blk_exp_ref[b], wslot))
            wgu_b[:, 0:EXP_FF] = wg32[wslot].astype(BF16)
            wgu_b[:, EXP_FF:2 * EXP_FF] = wu32[wslot].astype(BF16)
            wd_b[...] = wd32[wslot].astype(BF16)
            nxt = run_end(b)

            @pl.when(nxt < n_used)
            def _():
                start(w_copies(blk_exp_ref[jnp.minimum(nxt, n_used - 1)], 1 - wslot))

        wait(x_copies(b))

        @pl.when(b >= ny)
        def _():
            wait(y_copies(b - ny))

        xslot = b % nx
        xc = _unpack_rows(xbuf[xslot, 0], xbuf[xslot, 1])
        gu = sum(_dot(c, wgu_b[i * PLANE_W:(i + 1) * PLANE_W, :]) for i, c in enumerate(xc))
        act = (_silu(gu[:, :EXP_FF]) * gu[:, EXP_FF:]).astype(BF16)
        _pack_rows(_dot(act, wd_b[...]).astype(BF16), ybuf.at[b % ny])
        start(y_copies(b))
        return carry

    lax.fori_loop(0, n_used, block, 0)

    for j in range(ny, 0, -1):
        @pl.when(n_used >= j)
        def _():
            wait(y_copies(n_used - j))


def _gmm(blk_exp, n_used, xs, w_gate, w_up, w_down):
    n_rows = xs.shape[1]
    bm = GMM_BM
    nb = n_rows // bm

    assert blk_exp.shape == (nb,)
    any_spec = pl.BlockSpec(memory_space=pl.ANY)
    grid_spec = pltpu.PrefetchScalarGridSpec(
        num_scalar_prefetch=2,
        grid=(1,),
        in_specs=[any_spec, any_spec, any_spec, any_spec],
        out_specs=any_spec,
        scratch_shapes=[pltpu.VMEM((GMM_NX, 2, bm, PLANE_W), I32), pltpu.VMEM((GMM_NY, 2, bm, PLANE_W), I32),
                        pltpu.VMEM((2, D_MODEL, EXP_FF), F32), pltpu.VMEM((2, D_MODEL, EXP_FF), F32),
                        pltpu.VMEM((2, EXP_FF, D_MODEL), F32),
                        pltpu.VMEM((D_MODEL, 2 * EXP_FF), BF16), pltpu.VMEM((EXP_FF, D_MODEL), BF16),
                        pltpu.SemaphoreType.DMA((GMM_NX, 2)), pltpu.SemaphoreType.DMA((GMM_NY, 2)),
                        pltpu.SemaphoreType.DMA((2, 3)), pltpu.SMEM((1,), I32)],
    )
    return pl.pallas_call(
        _gmm_kernel,
        grid_spec=grid_spec,
        out_shape=jax.ShapeDtypeStruct((2, n_rows, PLANE_W), I32),
        compiler_params=pltpu.CompilerParams(dimension_semantics=("arbitrary",)),
        name="gmm",
    )(blk_exp, n_used, xs, w_gate, w_up, w_down)


def _final_kernel(xmid_ref, g2_ref, z_ref, w_ref, gfin_ref, *rest):
    y_ref = rest[-1]
    accs = [jnp.zeros((xmid_ref.shape[0], PLANE_W), F32) for _ in range(4)]
    w_cols = w_ref[...].T
    for k in range(TOP_K):
        wk = w_cols[:, k:k + 1]
        cols = _unpack_rows(z_ref[k, 0], z_ref[k, 1])
        accs = [a + wk * c.astype(F32) for a, c in zip(accs, cols)]
    acc = jnp.concatenate(accs, axis=1)
    y_ref[...] = _rms(xmid_ref[...] + g2_ref[0] * acc) * gfin_ref[...]


def _final(xmid, g2, z, w_t, gfin, tile, *, n_tiles, x_tile0, z_tile0, w_tile0, tiles_per_g2, y_prev=None):
    args = [xmid, g2, z, w_t, gfin]
    in_specs = [pl.BlockSpec((tile, D_MODEL), lambda i: (x_tile0 + i, 0)),
                pl.BlockSpec((1, g2.shape[1], D_MODEL), lambda i: ((x_tile0 + i) // tiles_per_g2, 0, 0)),
                pl.BlockSpec((TOP_K, 2, tile, PLANE_W), lambda i: (0, 0, z_tile0 + i, 0)),
                pl.BlockSpec((TOP_K, tile), lambda i: (0, w_tile0 + i)),
                pl.BlockSpec((1, D_MODEL), lambda i: (0, 0))]
    aliases = {}
    if y_prev is not None:
        args.append(y_prev)
        in_specs.append(pl.BlockSpec(memory_space=pl.ANY))
        aliases = {len(args) - 1: 0}
    return pl.pallas_call(
        _final_kernel,
        grid=(n_tiles,),
        in_specs=in_specs,
        out_specs=pl.BlockSpec((tile, D_MODEL), lambda i: (x_tile0 + i, 0)),
        out_shape=jax.ShapeDtypeStruct(xmid.shape, F32),
        input_output_aliases=aliases,
        compiler_params=pltpu.CompilerParams(dimension_semantics=("arbitrary",)),
        name="final",
    )(*args)


def kernel(x_prompt, x_sample, state_hgrn, state_conv, c_prompt, c_sample, w_ada, b_ada, norm_mix_g, norm_ffn_g, w_in, lb_logits, hgrn_norm_g, conv_w, conv_b, w_out_hgrn, w_out_conv, w_o, w_router, router_bias, w_exp_gate, w_exp_up, w_exp_down, w_sh_gate, w_sh_up, w_sh_down, final_norm_g):
    assert w_ada.shape[0] == 1 and lb_logits.shape[0] == 2
    bsz, seq, _ = x_prompt.shape
    n_smp = x_sample.shape[0]
    n_prompt = bsz * seq
    n_tok = n_prompt + n_smp

    w_in_b = w_in[0].astype(BF16)
    w_oh_b = w_out_hgrn[0].astype(BF16)
    w_oc_b = w_out_conv[0].astype(BF16)
    w_o_b = w_o[0].astype(BF16)
    wr_t = w_router[0].T
    wr_hi = wr_t.astype(BF16)
    wr_hl = jnp.concatenate([wr_hi, (wr_t - wr_hi.astype(F32)).astype(BF16)], axis=0)
    w_sgu = jnp.concatenate([w_sh_gate[0], w_sh_up[0]], axis=1).astype(BF16)
    w_sd = w_sh_down[0].astype(BF16)
    gmix = norm_mix_g[0].reshape(1, D_MODEL)
    gffn = norm_ffn_g[0].reshape(1, D_MODEL)
    hg = hgrn_norm_g[0].reshape(1, KEY_W)
    cw = conv_w[0]
    cb = conv_b[0].reshape(1, CONV_W)
    gfin = final_norm_g.reshape(1, D_MODEL)

    mod_p, mod_s = _ada(c_prompt, c_sample, w_ada[0], b_ada[0])

    xmid_p, h2_all, lgt_all, s_p, cv_p = _mix(
        x_prompt, mod_p.reshape(bsz, 6, D_MODEL), n_tok, gmix, gffn, w_in_b, lb_logits, hg, cw, cb, w_oh_b, w_oc_b, w_o_b, wr_hl, w_sgu, w_sd)

    xs2 = x_sample.reshape(n_smp, D_MODEL)
    f, kk, q, v, gate, yb, sga, sgb, cv_s = _smp1(
        xs2, mod_s, gmix, w_in_b, lb_logits, cw, cb, state_conv[0].reshape(n_smp, (CONV_K - 1) * CONV_W))
    s_s, o_s = _smp2(f, kk, q, v, state_hgrn[0])
    xmid_s, h2_all, lgt_all = _smp3(xs2, mod_s, o_s, gate, yb, sga, sgb, hg, gffn,
                                    w_oh_b, w_oc_b, w_o_b, wr_hl, w_sgu, w_sd,
                                    h2_all, lgt_all, n_prompt)

    idx, w_tok, rank, cnt = _route(lgt_all, router_bias[0].reshape(N_EXP, 1))

    bm = GMM_BM
    n_blocks = (n_tok * TOP_K + N_EXP * (bm - 1)) // bm
    n_rows = n_blocks * bm
    counts = cnt[:, 0]
    padded = (counts + bm - 1) // bm * bm
    pad_end = jnp.cumsum(padded)
    pad_start = pad_end - padded
    blk_row0 = jnp.arange(n_blocks, dtype=I32) * bm
    blk_exp = jnp.minimum(jnp.sum((pad_end[None, :] <= blk_row0[:, None]).astype(I32), axis=1), N_EXP - 1)
    n_used = (pad_end[-1:] // bm).astype(I32)
    cuts = [0] + [n_prompt * f // FINAL_SPLIT[-1] for f in FINAL_SPLIT]
    assert all(c % FINAL_TILE == 0 for c in cuts) and cuts[-1] == n_prompt
    chunks = tuple((c0, n_tok if c1 == n_prompt else c1) for c0, c1 in zip(cuts[:-1], cuts[1:]))
    dest, *chunk_dest = _dest(pad_start.astype(I32), idx, rank, n_rows, chunks)

    xs = _dispatch(h2_all.reshape(2 * n_tok, PLANE_W), dest, 2 * n_rows).reshape(2, n_rows, PLANE_W)
    ys = _gmm(blk_exp, n_used, xs, w_exp_gate[0], w_exp_up[0], w_exp_down[0])

    ys_flat = ys.reshape(2 * n_rows, PLANE_W)
    w_t = w_tok
    xmid_p2 = xmid_p.reshape(n_prompt, D_MODEL)
    g2_p = mod_p[:, 5 * D_MODEL:].reshape(bsz, 1, D_MODEL)
    g2_s = mod_s[:, 5 * D_MODEL:].reshape(1, n_smp, D_MODEL)
    y_p = None
    for (c0, c1), dest_c in zip(chunks, chunk_dest):
        z = _combine(ys_flat, dest_c).reshape(TOP_K, 2, c1 - c0, PLANE_W)
        y_p = _final(xmid_p2, g2_p, z, w_t, gfin, FINAL_TILE, n_tiles=(min(c1, n_prompt) - c0) // FINAL_TILE,
                     x_tile0=c0 // FINAL_TILE, z_tile0=0, w_tile0=c0 // FINAL_TILE,
                     tiles_per_g2=seq // FINAL_TILE, y_prev=y_p)
    y_s = _final(xmid_s, g2_s, z, w_t, gfin, n_smp, n_tiles=1, x_tile0=0, z_tile0=(n_prompt - c0) // n_smp,
                 w_tile0=n_prompt // n_smp, tiles_per_g2=1)

    return (y_p.reshape(bsz, seq, D_MODEL), y_s.reshape(n_smp, 1, D_MODEL),
            s_p[None], cv_p[None], s_s[None], cv_s.reshape(1, n_smp, CONV_K - 1, CONV_W))
```

```python
import functools

import jax
import jax.numpy as jnp
from jax import lax
from jax.experimental import pallas as pl
from jax.experimental.pallas import tpu as pltpu
from jax.experimental.pallas import tpu_sc as plsc

F32 = jnp.float32
BF16 = jnp.bfloat16
I32 = jnp.int32

D_MODEL = 1024
HALF_D = D_MODEL // 2
HEADS = 4
DK = 128
KEY_W = HEADS * DK
CONV_W = 512
CONV_K = 3
IN_W = 2 * KEY_W + 2 * KEY_W + 3 * CONV_W + 2 * D_MODEL
N_EXP = 64
TOP_K = 8
N_GRP = 8
GRP_SZ = N_EXP // N_GRP
TOPK_GRP = 4
EXP_FF = 256
SH_FF = 256
ROUTED_SCALE = 2.5
EPS = 1e-6

C_Q, C_F, C_I, C_G = 0, 512, 1024, 1536
C_BB, C_CC, C_VB = 2048, 2560, 3072
C_MGA, C_MGB = 3584, 4608

MIX_TILE = 512
SUB = 256
CHUNK = 64
ROUTE_TILE = 384
GMM_BM = 512
GMM_NX = 4
GMM_NY = 3
FINAL_TILE = 512
FINAL_SPLIT = (4, 12, 22, 32)
SC_WINDOW = 128
PLANE_W = HALF_D // 2
VMEM_LIMIT = 56 * 1024 * 1024


def _dot(a, b):
    return jnp.dot(a, b, preferred_element_type=F32)


def _dot_nt(a, b):
    return lax.dot_general(a, b, (((1,), (1,)), ((), ())), preferred_element_type=F32)


def _dot_tn(a, b):
    return lax.dot_general(a, b, (((0,), (0,)), ((), ())), preferred_element_type=F32)


def _sigmoid(x):
    return 0.5 * jnp.tanh(0.5 * x) + 0.5


def _silu(x):
    h = 0.5 * x
    return h * jnp.tanh(h) + h


def _rms(x):
    return x * lax.rsqrt(jnp.mean(x * x, axis=-1, keepdims=True) + EPS)


def _lower_bound(lbl):
    a, b = lbl[0:1], lbl[1:2]
    m = jnp.maximum(a, b)
    ea, eb = jnp.exp(a - m), jnp.exp(b - m)
    return ea / (ea + eb)


def _split3(x):
    hi = x.astype(BF16)
    r1 = x - hi.astype(F32)
    mid = r1.astype(BF16)
    lo = (r1 - mid.astype(F32)).astype(BF16)
    return hi, mid, lo


def _words(lo_b, hi_b):
    lo = lax.shift_right_logical(lax.bitcast_convert_type(lo_b.astype(F32), I32), 16)
    hi = lax.bitcast_convert_type(hi_b.astype(F32), I32) & jnp.int32(-65536)
    return lo | hi


def _halves(w):
    lo = lax.bitcast_convert_type(lax.shift_left(w, 16), F32)
    hi = lax.bitcast_convert_type(w & jnp.int32(-65536), F32)
    return lo.astype(BF16), hi.astype(BF16)


def _pack_rows(xb, out_ref):
    words = _words(xb[:, :HALF_D], xb[:, HALF_D:])
    out_ref[0] = words[:, :PLANE_W]
    out_ref[1] = words[:, PLANE_W:]


def _unpack_rows(p0, p1):
    c0, c2 = _halves(p0)
    c1, c3 = _halves(p1)
    return c0, c1, c2, c3


def _ada_kernel(cp_ref, cs_ref, w_ref, b_ref, op_ref, os_ref):
    w = w_ref[...].astype(BF16)
    for c_ref, o_ref in ((cp_ref, op_ref), (cs_ref, os_ref)):
        o_ref[...] = _dot(_silu(c_ref[...]).astype(BF16), w) + b_ref[...]


def _ada(c_prompt, c_sample, w_ada, b_ada):
    blk = 1024
    rows = lambda c: pl.BlockSpec((c.shape[0], D_MODEL), lambda j: (0, 0))
    cols = lambda c: pl.BlockSpec((c.shape[0], blk), lambda j: (0, j))
    return pl.pallas_call(
        _ada_kernel,
        grid=(6 * D_MODEL // blk,),
        in_specs=[rows(c_prompt), rows(c_sample),
                  pl.BlockSpec((D_MODEL, blk), lambda j: (0, j)),
                  pl.BlockSpec((1, blk), lambda j: (0, j))],
        out_specs=[cols(c_prompt), cols(c_sample)],
        out_shape=[jax.ShapeDtypeStruct((c.shape[0], 6 * D_MODEL), F32) for c in (c_prompt, c_sample)],
        name="ada",
    )(c_prompt, c_sample, w_ada, b_ada.reshape(1, -1))


def _ffn_pre(x1, mod_rows, gffn, w_sgu, w_sd, wr_hl):
    sh2, sc2, g2 = mod_rows
    h2 = _rms(x1) * gffn * (1.0 + sc2) + sh2
    h2b = h2.astype(BF16)
    gu = _dot(h2b, w_sgu)
    act = _silu(gu[:, :SH_FF]) * gu[:, SH_FF:]
    xmid = x1 + g2 * _dot(act.astype(BF16), w_sd)
    h2lo = (h2 - h2b.astype(F32)).astype(BF16)
    both = _dot_nt(wr_hl, h2b)
    lgt = both[:N_EXP] + both[N_EXP:] + _dot_nt(wr_hl[:N_EXP], h2lo)
    return xmid, h2b, lgt


def _mix_kernel(*refs, nt, n_tiles):
    i = pl.program_id(0)
    h2_ref, lgt_ref = refs[16], refs[17]

    @pl.when(i == n_tiles)
    def _():
        h2_ref[...] = jnp.zeros_like(h2_ref)
        lgt_ref[...] = jnp.zeros_like(lgt_ref)

    @pl.when(i < n_tiles)
    def _():
        _mix_tile(i % nt, nt, *refs)


def _mix_tile(t, nt, x_ref, mod_ref, gmix_ref, gffn_ref, w_in_ref, lbl_ref, hg_ref, cw_ref, cb_ref,
              w_oh_ref, w_oc_ref, w_o_ref, wr_hl_ref, w_sgu_ref, w_sd_ref,
              xmid_ref, h2_ref, lgt_ref, s_out_ref, cv_out_ref,
              proj_ref, st_ref, cbuf_ref, ya_ref):
    tt = x_ref.shape[1]

    @pl.when(t == 0)
    def _():
        st_ref[...] = jnp.zeros_like(st_ref)
        cbuf_ref[...] = jnp.zeros_like(cbuf_ref)

    x = x_ref[0]
    mod = mod_ref[0]
    sh1, sc1, g1 = mod[0:1], mod[1:2], mod[2:3]
    h = _rms(x) * gmix_ref[...] * (1.0 + sc1) + sh1
    hb = h.astype(BF16)
    for c in range(0, IN_W, 512):
        proj_ref[:, c:c + 512] = _dot(hb, w_in_ref[:, c:c + 512])

    lb = _lower_bound(lbl_ref[...])
    row = lax.broadcasted_iota(I32, (SUB, SUB), 0)
    col = lax.broadcasted_iota(I32, (SUB, SUB), 1)
    tri = (col <= row).astype(BF16)
    mask_d = (col <= row) & (row // CHUNK == col // CHUNK)
    mask_a = row // (2 * CHUNK) == col // (2 * CHUNK)
    n_ch = SUB // CHUNK

    def by_chunk(vals):
        return jnp.concatenate([jnp.zeros((CHUNK, DK), F32) if v is None
                                else jnp.broadcast_to(v, (CHUNK, DK)) for v in vals], axis=0)

    for s in range(tt // SUB):
        r0 = s * SUB
        f = lb + (1.0 - lb) * _sigmoid(proj_ref[r0:r0 + SUB, C_F:C_F + KEY_W])
        kk = 1.0 - f
        hi, mid, lo = _split3(jnp.log(f))
        bc = _dot(tri, hi) + _dot(tri, mid) + _dot(tri, lo)
        for hd in range(HEADS):
            hs = slice(hd * DK, (hd + 1) * DK)
            bh = bc[:, hs]
            at = lambda r: bh[r:r + 1]
            mids = [at(c * CHUNK + CHUNK // 2 - 1) for c in range(n_ch)]
            pair_mid = [at(CHUNK - 1), at(3 * CHUNK - 1)]
            step_mid, step_end = at(2 * CHUNK - 1), at(SUB - 1)
            arg = bh - by_chunk(mids)
            e_pos, e_neg = jnp.exp(arg), jnp.exp(-arg)
            q = _silu(proj_ref[r0:r0 + SUB, C_Q + hd * DK:C_Q + (hd + 1) * DK])
            v = proj_ref[r0:r0 + SUB, C_I + hd * DK:C_I + (hd + 1) * DK]
            qd = q * e_pos
            kd = kk[:, hs] * e_neg
            q_in = qd * by_chunk([jnp.exp(m) for m in mids])
            k_end = kd * by_chunk([jnp.exp(step_end - m) for m in mids])
            qa = qd * by_chunk([None, jnp.exp(mids[1] - pair_mid[0]), None, jnp.exp(mids[3] - pair_mid[1])])
            ka = kd * by_chunk([jnp.exp(pair_mid[0] - mids[0]), None, jnp.exp(pair_mid[1] - mids[2]), None])
            qb = qd * by_chunk([None, None, jnp.exp(mids[2] - step_mid), jnp.exp(mids[3] - step_mid)])
            kb = kd * by_chunk([jnp.exp(step_mid - mids[0]), jnp.exp(step_mid - mids[1]), None, None])
            att = jnp.where(mask_d, _dot_nt(qd.astype(BF16), kd.astype(BF16)), 0.0)
            att = att + jnp.where(mask_a, _dot_nt(qa.astype(BF16), ka.astype(BF16)), 0.0)
            att = att + _dot_nt(qb.astype(BF16), kb.astype(BF16))
            vb = v.astype(BF16)
            st = st_ref[hd]
            o = _dot(att.astype(BF16), vb) + _dot_nt(q_in.astype(BF16), st.astype(BF16))
            st_ref[hd] = st * jnp.exp(step_end) + _dot_tn(vb, k_end.astype(BF16))
            gate = _silu(proj_ref[r0:r0 + SUB, C_G + hd * DK:C_G + (hd + 1) * DK])
            ya_ref[r0:r0 + SUB, hs] = _rms(o) * hg_ref[:, hs] * gate

    u = proj_ref[:, C_CC:C_CC + CONV_W] * proj_ref[:, C_VB:C_VB + CONV_W]
    rows = lax.broadcasted_iota(I32, (tt, CONV_W), 0)
    c0, c1 = cbuf_ref[0:1], cbuf_ref[1:2]
    u1 = jnp.where(rows == 0, c1, pltpu.roll(u, 1, axis=0))
    u2 = jnp.where(rows == 0, c0, jnp.where(rows == 1, c1, pltpu.roll(u, 2, axis=0)))
    conv = cw_ref[0:1] * u2 + cw_ref[1:2] * u1 + cw_ref[2:3] * u + cb_ref[...]
    yb = proj_ref[:, C_BB:C_BB + CONV_W] * conv
    cbuf_ref[...] = u[tt - 2:tt]

    mixed = (_sigmoid(proj_ref[:, C_MGA:C_MGA + D_MODEL]) * _dot(ya_ref[...].astype(BF16), w_oh_ref[...])
             + _sigmoid(proj_ref[:, C_MGB:C_MGB + D_MODEL]) * _dot(yb.astype(BF16), w_oc_ref[...]))
    x1 = x + g1 * _dot(mixed.astype(BF16), w_o_ref[...])

    xmid, h2b, lgt = _ffn_pre(x1, (mod[3:4], mod[4:5], mod[5:6]), gffn_ref[...],
                              w_sgu_ref[...], w_sd_ref[...], wr_hl_ref[...])
    xmid_ref[0] = xmid
    _pack_rows(h2b, h2_ref)
    lgt_ref[...] = lgt

    @pl.when(t == nt - 1)
    def _():
        for hd in range(HEADS):
            s_out_ref[0, hd] = st_ref[hd].T
        cv_out_ref[0] = cbuf_ref[...]


def _const_spec(shape):
    nd = len(shape)
    return pl.BlockSpec(shape, lambda i, _nd=nd: (0,) * _nd, pipeline_mode=pl.Buffered(1))


def _mix(x, mod, n_tok, gmix, gffn, w_in, lbl, hg, cw, cb, w_oh, w_oc, w_o, wr_hl, w_sgu, w_sd):
    bsz, seq, _ = x.shape
    tt = MIX_TILE
    nt = seq // tt
    n_tiles = bsz * nt
    assert n_tiles * tt < n_tok <= (n_tiles + 1) * tt
    consts = [gmix, gffn, w_in, lbl, hg, cw, cb, w_oh, w_oc, w_o, wr_hl, w_sgu, w_sd]
    tile = lambda i: jnp.minimum(i, n_tiles - 1)
    return pl.pallas_call(
        functools.partial(_mix_kernel, nt=nt, n_tiles=n_tiles),
        grid=(n_tiles + 1,),
        in_specs=[pl.BlockSpec((1, tt, D_MODEL), lambda i: (tile(i) // nt, tile(i) % nt, 0)),
                  pl.BlockSpec((1, 6, D_MODEL), lambda i: (tile(i) // nt, 0, 0))]
                 + [_const_spec(a.shape) for a in consts],
        out_specs=[pl.BlockSpec((1, tt, D_MODEL), lambda i: (tile(i) // nt, tile(i) % nt, 0)),
                   pl.BlockSpec((2, tt, PLANE_W), lambda i: (0, i, 0)),
                   pl.BlockSpec((N_EXP, tt), lambda i: (0, i)),
                   pl.BlockSpec((1, HEADS, DK, DK), lambda i: (tile(i) // nt, 0, 0, 0)),
                   pl.BlockSpec((1, CONV_K - 1, CONV_W), lambda i: (tile(i) // nt, 0, 0))],
        out_shape=[jax.ShapeDtypeStruct((bsz, seq, D_MODEL), F32),
                   jax.ShapeDtypeStruct((2, n_tok, PLANE_W), I32),
                   jax.ShapeDtypeStruct((N_EXP, n_tok), F32),
                   jax.ShapeDtypeStruct((bsz, HEADS, DK, DK), F32),
                   jax.ShapeDtypeStruct((bsz, CONV_K - 1, CONV_W), F32)],
        scratch_shapes=[pltpu.VMEM((tt, IN_W), F32),
                        pltpu.VMEM((HEADS, DK, DK), F32),
                        pltpu.VMEM((CONV_K - 1, CONV_W), F32),
                        pltpu.VMEM((tt, KEY_W), F32)],
        compiler_params=pltpu.CompilerParams(
            dimension_semantics=("arbitrary",), vmem_limit_bytes=VMEM_LIMIT),
        name="mix",
    )(x, mod, *consts)


def _smp1_kernel(x_ref, mod_ref, gmix_ref, w_in_ref, lbl_ref, cw_ref, cb_ref, cst_ref,
                 f_ref, k_ref, q_ref, v_ref, gate_ref, yb_ref, sga_ref, sgb_ref, cv_out_ref):
    x = x_ref[...]
    sh1, sc1 = mod_ref[:, 0:D_MODEL], mod_ref[:, D_MODEL:2 * D_MODEL]
    h = _rms(x) * gmix_ref[...] * (1.0 + sc1) + sh1
    hb = h.astype(BF16)

    def proj(c, w):
        return _dot(hb, w_in_ref[:, c:c + w])

    lb = _lower_bound(lbl_ref[...])
    f = lb + (1.0 - lb) * _sigmoid(proj(C_F, KEY_W))
    f_ref[...] = f
    k_ref[...] = 1.0 - f
    q_ref[...] = _silu(proj(C_Q, KEY_W))
    v_ref[...] = proj(C_I, KEY_W)
    gate_ref[...] = _silu(proj(C_G, KEY_W))
    u = proj(C_CC, CONV_W) * proj(C_VB, CONV_W)
    c0, c1 = cst_ref[:, 0:CONV_W], cst_ref[:, CONV_W:2 * CONV_W]
    conv = cw_ref[0:1] * c0 + cw_ref[1:2] * c1 + cw_ref[2:3] * u + cb_ref[...]
    yb_ref[...] = proj(C_BB, CONV_W) * conv
    cv_out_ref[:, 0:CONV_W] = c1
    cv_out_ref[:, CONV_W:2 * CONV_W] = u
    sga_ref[...] = _sigmoid(proj(C_MGA, D_MODEL))
    sgb_ref[...] = _sigmoid(proj(C_MGB, D_MODEL))


def _smp1(x, mod, gmix, w_in, lbl, cw, cb, cst):
    n = x.shape[0]
    kw = jax.ShapeDtypeStruct((n, KEY_W), F32)
    dm = jax.ShapeDtypeStruct((n, D_MODEL), F32)
    return pl.pallas_call(
        _smp1_kernel,
        out_shape=[kw, kw, kw, kw, kw, kw, dm, dm,
                   jax.ShapeDtypeStruct((n, (CONV_K - 1) * CONV_W), F32)],
        compiler_params=pltpu.CompilerParams(vmem_limit_bytes=VMEM_LIMIT),
        name="smp1",
    )(x, mod, gmix, w_in, lbl, cw, cb, cst)


def _smp2(f, k, q, v, state):
    n = f.shape[0]
    info = plsc.get_sparse_core_info()
    lanes = info.num_lanes
    n_workers = info.num_cores * info.num_subcores
    tok_per = n // n_workers
    n_chunks = DK // lanes
    vec = lambda: pltpu.VMEM((DK,), F32)

    @pl.kernel(out_type=[jax.ShapeDtypeStruct(state.shape, F32), jax.ShapeDtypeStruct((n, KEY_W), F32)],
               mesh=_sc_mesh(), scratch_types=[pltpu.VMEM((DK, DK), F32), vec(), vec(), vec(), vec(), vec()],
               compiler_params=pltpu.CompilerParams(needs_layout_passes=False), name="smp2")
    def run(f_hbm, k_hbm, q_hbm, v_hbm, s_hbm, s_out_hbm, o_hbm, s_v, f_v, k_v, q_v, v_v, o_v):
        wid = lax.axis_index("subcore") * info.num_cores + lax.axis_index("core")

        def tile(j, carry):
            t = wid * tok_per + j // HEADS
            hcols = pl.ds((j % HEADS) * DK, DK)
            pltpu.sync_copy(s_hbm.at[t, j % HEADS], s_v)
            for src, dst in ((f_hbm, f_v), (k_hbm, k_v), (q_hbm, q_v), (v_hbm, v_v)):
                pltpu.sync_copy(src.at[t, hcols], dst)
            v_chunks = [v_v[pl.ds(c * lanes, lanes)] for c in range(n_chunks)]

            def row(d, acc):
                at = [jnp.zeros((lanes,), I32) + d]
                fd, kd, qd = (plsc.load_gather(r, at) for r in (f_v, k_v, q_v))
                out = []
                for c in range(n_chunks):
                    cols = pl.ds(c * lanes, lanes)
                    new = s_v[d, cols] * fd + kd * v_chunks[c]
                    s_v[d, cols] = new
                    out.append(acc[c] + qd * new)
                return tuple(out)

            acc = lax.fori_loop(0, DK, row, tuple(jnp.zeros((lanes,), F32) for _ in range(n_chunks)))
            for c in range(n_chunks):
                o_v[pl.ds(c * lanes, lanes)] = acc[c]
            pltpu.sync_copy(s_v, s_out_hbm.at[t, j % HEADS])
            pltpu.sync_copy(o_v, o_hbm.at[t, hcols])
            return carry

        lax.fori_loop(0, tok_per * HEADS, tile, 0)

    return run(f, k, q, v, state)


def _smp3_kernel(x_ref, mod_ref, o_ref, gate_ref, yb_ref, sga_ref, sgb_ref, hg_ref, gffn_ref,
                 w_oh_ref, w_oc_ref, w_o_ref, wr_hl_ref, w_sgu_ref, w_sd_ref,
                 h2_all_ref, lgt_all_ref, xmid_ref, h2_ref, lgt_ref):
    del h2_all_ref, lgt_all_ref
    parts = []
    for hd in range(HEADS):
        hs = slice(hd * DK, (hd + 1) * DK)
        parts.append(_rms(o_ref[:, hs]) * hg_ref[:, hs] * gate_ref[:, hs])
    ya = jnp.concatenate(parts, axis=1)
    mixed = (sga_ref[...] * _dot(ya.astype(BF16), w_oh_ref[...])
             + sgb_ref[...] * _dot(yb_ref[...].astype(BF16), w_oc_ref[...]))
    g1 = mod_ref[:, 2 * D_MODEL:3 * D_MODEL]
    x1 = x_ref[...] + g1 * _dot(mixed.astype(BF16), w_o_ref[...])
    mod_rows = tuple(mod_ref[:, j * D_MODEL:(j + 1) * D_MODEL] for j in (3, 4, 5))
    xmid, h2b, lgt = _ffn_pre(x1, mod_rows, gffn_ref[...], w_sgu_ref[...], w_sd_ref[...],
                              wr_hl_ref[...])
    xmid_ref[...] = xmid
    _pack_rows(h2b, h2_ref)
    lgt_ref[...] = lgt


def _smp3(x, mod, o, gate, yb, sga, sgb, hg, gffn, w_oh, w_oc, w_o, wr_hl, w_sgu, w_sd,
          h2_all, lgt_all, n_prompt):
    n = x.shape[0]
    vmem_args = [x, mod, o, gate, yb, sga, sgb, hg, gffn, w_oh, w_oc, w_o, wr_hl, w_sgu, w_sd]
    blk = n_prompt // n

    def full(a):
        nd = a.ndim
        return pl.BlockSpec(a.shape, lambda i, _nd=nd: (0,) * _nd)

    return pl.pallas_call(
        _smp3_kernel,
        grid=(1,),
        in_specs=[full(a) for a in vmem_args]
                 + [pl.BlockSpec(memory_space=pl.ANY), pl.BlockSpec(memory_space=pl.ANY)],
        out_specs=[pl.BlockSpec((n, D_MODEL), lambda i: (0, 0)),
                   pl.BlockSpec((2, n, PLANE_W), lambda i: (0, blk, 0)),
                   pl.BlockSpec((N_EXP, n), lambda i: (0, blk))],
        out_shape=[jax.ShapeDtypeStruct((n, D_MODEL), F32),
                   jax.ShapeDtypeStruct(h2_all.shape, h2_all.dtype),
                   jax.ShapeDtypeStruct(lgt_all.shape, lgt_all.dtype)],
        input_output_aliases={len(vmem_args): 1, len(vmem_args) + 1: 2},
        compiler_params=pltpu.CompilerParams(
            dimension_semantics=("arbitrary",), vmem_limit_bytes=VMEM_LIMIT),
        name="smp3",
    )(*vmem_args, h2_all, lgt_all)


def _route_kernel(lgt_ref, bias_ref, idx_ref, w_ref, rank_ref, cnt_ref):
    tr = ROUTE_TILE
    n_tiles = lgt_ref.shape[1] // tr

    def tile(i, carry):
        cols = pl.ds(pl.multiple_of(i * tr, tr), tr)
        picks, weights, ranks, carry = _route_tile(lgt_ref[:, cols], bias_ref[...], carry)
        for k in range(TOP_K):
            idx_ref[k:k + 1, cols] = picks[k]
            w_ref[k:k + 1, cols] = weights[k]
            rank_ref[k:k + 1, cols] = ranks[k]
        return carry

    total = lax.fori_loop(0, n_tiles, tile, jnp.zeros((N_EXP, 1), F32))
    cnt_ref[...] = jnp.broadcast_to(total, cnt_ref.shape).astype(I32)


def _route_tile(lgt, bias, carry):
    tr = lgt.shape[1]
    neg = -jnp.inf
    scores = _sigmoid(lgt)
    sel = scores + bias
    j8 = lax.broadcasted_iota(I32, (GRP_SZ, tr), 0)
    groups = [sel[g * GRP_SZ:(g + 1) * GRP_SZ] for g in range(N_GRP)]
    gscore = []
    for grp in groups:
        m1 = jnp.max(grp, axis=0, keepdims=True)
        i1 = jnp.min(jnp.where(grp == m1, j8, GRP_SZ), axis=0, keepdims=True)
        m2 = jnp.max(jnp.where(j8 == i1, neg, grp), axis=0, keepdims=True)
        gscore.append(m1 + m2)
    kept = []
    for g in range(N_GRP):
        beaten = jnp.zeros((1, tr), I32)
        for o in range(N_GRP):
            if o < g:
                beaten = beaten + (gscore[o] >= gscore[g]).astype(I32)
            elif o > g:
                beaten = beaten + (gscore[o] > gscore[g]).astype(I32)
        kept.append(jnp.where(beaten < TOPK_GRP, groups[g], neg))
    masked = jnp.concatenate(kept, axis=0)
    ei = lax.broadcasted_iota(I32, masked.shape, 0)
    chosen = jnp.zeros(masked.shape, jnp.bool_)
    picks, weights, hits = [], [], []
    for _ in range(TOP_K):
        m = jnp.max(masked, axis=0, keepdims=True)
        pick = jnp.min(jnp.where((masked == m) & ~chosen, ei, N_EXP), axis=0, keepdims=True)
        hit = ei == pick
        weights.append(jnp.sum(jnp.where(hit, scores, 0.0), axis=0, keepdims=True))
        picks.append(pick)
        hits.append(hit)
        chosen = chosen | hit
        masked = jnp.where(hit, neg, masked)
    wsum = weights[0]
    for w in weights[1:]:
        wsum = wsum + w
    sel01 = chosen.astype(F32)
    r = lax.broadcasted_iota(I32, (tr, tr), 0)
    c = lax.broadcasted_iota(I32, (tr, tr), 1)
    before = (r < c).astype(BF16)
    cnt = _dot(sel01.astype(BF16), before) + carry
    weights = [w / wsum * ROUTED_SCALE for w in weights]
    ranks = [jnp.sum(jnp.where(h, cnt, 0.0), axis=0, keepdims=True).astype(I32) for h in hits]
    return picks, weights, ranks, carry + jnp.sum(sel01, axis=1, keepdims=True)


def _route(lgt, bias):
    n = lgt.shape[1]
    assert n % ROUTE_TILE == 0
    slot = lambda dt: jax.ShapeDtypeStruct((TOP_K, n), dt)
    return pl.pallas_call(
        _route_kernel,
        out_shape=[slot(I32), slot(F32), slot(I32),
                   jax.ShapeDtypeStruct((N_EXP, 128), I32)],
        name="route",
    )(lgt, bias)


def _dest_kernel(start_ref, idx_ref, rank_ref, all_ref, *chunk_refs, n_rows, chunks):
    n_tok = idx_ref.shape[1]
    idx = idx_ref[...]
    acc = rank_ref[...]
    for e in range(N_EXP):
        acc = acc + jnp.where(idx == e, start_ref[e], 0)
    for k in range(TOP_K):
        for p in range(2):
            row = acc[k:k + 1, :] + p * n_rows
            seg = 2 * k + p
            all_ref[:, seg * n_tok:(seg + 1) * n_tok] = row
            for (c0, c1), ref in zip(chunks, chunk_refs):
                ref[:, seg * (c1 - c0):(seg + 1) * (c1 - c0)] = row[:, c0:c1]


def _dest(pad_start, idx, rank, n_rows, chunks):
    k, n_tok = idx.shape
    vmem = pl.BlockSpec(memory_space=pltpu.VMEM)
    sizes = [n_tok] + [c1 - c0 for c0, c1 in chunks]
    return pl.pallas_call(
        functools.partial(_dest_kernel, n_rows=n_rows, chunks=chunks),
        in_specs=[pl.BlockSpec(memory_space=pltpu.SMEM), vmem, vmem],
        out_specs=[vmem] * len(sizes),
        out_shape=[jax.ShapeDtypeStruct((1, 2 * k * n), I32) for n in sizes],
        name="dest",
    )(pad_start, idx, rank)


def _sc_mesh():
    return plsc.VectorSubcoreMesh(core_axis_name="core", subcore_axis_name="subcore")


def _dispatch(rows, dest, n_out):
    n, width = rows.shape
    win = SC_WINDOW
    steps = n // win

    @pl.kernel(out_type=jax.ShapeDtypeStruct((n_out, width), rows.dtype), mesh=_sc_mesh(),
               scratch_types=[], name="dispatch")
    def run(x_hbm, *refs):
        i_hbms, o_hbm = refs[:TOP_K], refs[TOP_K]

        def body(x_vmem, *i_vmems):
            for i_vmem in i_vmems:
                pltpu.sync_copy(x_vmem, o_hbm.at[i_vmem.at[0]])

        pltpu.emit_pipeline(
            body,
            grid=(steps,),
            in_specs=[pl.BlockSpec((win, width), lambda i: (i, 0))]
                     + [pl.BlockSpec((1, win), lambda i, k=k: (0, k * steps + i)) for k in range(TOP_K)],
            out_specs=[],
            core_axis_name=("core", "subcore"),
            dimension_semantics=(pltpu.PARALLEL,),
        )(x_hbm, *i_hbms)

    assert dest.shape == (1, TOP_K * n)
    return run(rows, *([dest] * TOP_K))


def _combine(rows, dest_flat):
    width = rows.shape[1]
    n = dest_flat.shape[1]
    win = SC_WINDOW

    @pl.kernel(out_type=jax.ShapeDtypeStruct((n, width), rows.dtype), mesh=_sc_mesh(),
               scratch_types=[], name="combine")
    def run(y_hbm, i_hbm, o_hbm):
        def body(i_vmem, o_vmem):
            pltpu.sync_copy(y_hbm.at[i_vmem.at[0]], o_vmem)

        pltpu.emit_pipeline(
            body,
            grid=(n // win,),
            in_specs=[pl.BlockSpec((1, win), lambda i: (0, i))],
            out_specs=[pl.BlockSpec((win, width), lambda i: (i, 0))],
            core_axis_name=("core", "subcore"),
            dimension_semantics=(pltpu.PARALLEL,),
        )(i_hbm, o_hbm)

    return run(rows, dest_flat)


def _gmm_kernel(blk_exp_ref, n_used_ref, xs_hbm, wg_hbm, wu_hbm, wd_hbm, ys_hbm,
                xbuf, ybuf, wg32, wu32, wd32, wgu_b, wd_b, xsem, ysem, wsem, run_ref):
    nx, ny, bm = xbuf.shape[0], ybuf.shape[0], xbuf.shape[2]
    n_used = n_used_ref[0]

    def x_copies(b):
        rows, slot = pl.ds(b * bm, bm), b % nx
        return [pltpu.make_async_copy(xs_hbm.at[p, rows, :], xbuf.at[slot, p], xsem.at[slot, p]) for p in range(2)]

    def y_copies(b):
        rows, slot = pl.ds(b * bm, bm), b % ny
        return [pltpu.make_async_copy(ybuf.at[slot, p], ys_hbm.at[p, rows, :], ysem.at[slot, p]) for p in range(2)]

    def start(copies):
        for c in copies:
            c.start()

    def wait(copies):
        for c in copies:
            c.wait()

    def w_copies(e, slot):
        return (pltpu.make_async_copy(wg_hbm.at[e], wg32.at[slot], wsem.at[slot, 0]),
                pltpu.make_async_copy(wu_hbm.at[e], wu32.at[slot], wsem.at[slot, 1]),
                pltpu.make_async_copy(wd_hbm.at[e], wd32.at[slot], wsem.at[slot, 2]))

    def run_end(b):
        return lax.while_loop(lambda j: (j < n_used) & (blk_exp_ref[jnp.minimum(j, n_used - 1)] == blk_exp_ref[b]),
                              lambda j: j + 1, b + 1)

    run_ref[0] = 0
    start(x_copies(0))
    start(w_copies(blk_exp_ref[0], 0))
    for j in range(1, nx - 1):
        @pl.when(j < n_used)
        def _():
            start(x_copies(j))

    def block(b, carry):
        @pl.when(b + nx - 1 < n_used)
        def _():
            start(x_copies(b + nx - 1))

        @pl.when((b == 0) | (blk_exp_ref[b] != blk_exp_ref[jnp.maximum(b - 1, 0)]))
        def _():
            wslot = run_ref[0] % 2
            run_ref[0] = run_ref[0] + 1
            wait(w_copies(blk_exp_ref[b], wslot))
            wgu_b[:, 0:EXP_FF] = wg32[wslot].astype(BF16)
            wgu_b[:, EXP_FF:2 * EXP_FF] = wu32[wslot].astype(BF16)
            wd_b[...] = wd32[wslot].astype(BF16)
            nxt = run_end(b)

            @pl.when(nxt < n_used)
            def _():
                start(w_copies(blk_exp_ref[jnp.minimum(nxt, n_used - 1)], 1 - wslot))

        wait(x_copies(b))

        @pl.when(b >= ny)
        def _():
            wait(y_copies(b - ny))

        xslot = b % nx
        xc = _unpack_rows(xbuf[xslot, 0], xbuf[xslot, 1])
        gu = sum(_dot(c, wgu_b[i * PLANE_W:(i + 1) * PLANE_W, :]) for i, c in enumerate(xc))
        act = (_silu(gu[:, :EXP_FF]) * gu[:, EXP_FF:]).astype(BF16)
        _pack_rows(_dot(act, wd_b[...]).astype(BF16), ybuf.at[b % ny])
        start(y_copies(b))
        return carry

    lax.fori_loop(0, n_used, block, 0)

    for j in range(ny, 0, -1):
        @pl.when(n_used >= j)
        def _():
            wait(y_copies(n_used - j))


def _gmm(blk_exp, n_used, xs, w_gate, w_up, w_down):
    n_rows = xs.shape[1]
    bm = GMM_BM
    nb = n_rows // bm

    assert blk_exp.shape == (nb,)
    any_spec = pl.BlockSpec(memory_space=pl.ANY)
    grid_spec = pltpu.PrefetchScalarGridSpec(
        num_scalar_prefetch=2,
        grid=(1,),
        in_specs=[any_spec, any_spec, any_spec, any_spec],
        out_specs=any_spec,
        scratch_shapes=[pltpu.VMEM((GMM_NX, 2, bm, PLANE_W), I32), pltpu.VMEM((GMM_NY, 2, bm, PLANE_W), I32),
                        pltpu.VMEM((2, D_MODEL, EXP_FF), F32), pltpu.VMEM((2, D_MODEL, EXP_FF), F32),
                        pltpu.VMEM((2, EXP_FF, D_MODEL), F32),
                        pltpu.VMEM((D_MODEL, 2 * EXP_FF), BF16), pltpu.VMEM((EXP_FF, D_MODEL), BF16),
                        pltpu.SemaphoreType.DMA((GMM_NX, 2)), pltpu.SemaphoreType.DMA((GMM_NY, 2)),
                        pltpu.SemaphoreType.DMA((2, 3)), pltpu.SMEM((1,), I32)],
    )
    return pl.pallas_call(
        _gmm_kernel,
        grid_spec=grid_spec,
        out_shape=jax.ShapeDtypeStruct((2, n_rows, PLANE_W), I32),
        compiler_params=pltpu.CompilerParams(dimension_semantics=("arbitrary",)),
        name="gmm",
    )(blk_exp, n_used, xs, w_gate, w_up, w_down)


def _final_kernel(xmid_ref, g2_ref, z_ref, w_ref, gfin_ref, *rest):
    y_ref = rest[-1]
    accs = [jnp.zeros((xmid_ref.shape[0], PLANE_W), F32) for _ in range(4)]
    w_cols = w_ref[...].T
    for k in range(TOP_K):
        wk = w_cols[:, k:k + 1]
        cols = _unpack_rows(z_ref[k, 0], z_ref[k, 1])
        accs = [a + wk * c.astype(F32) for a, c in zip(accs, cols)]
    acc = jnp.concatenate(accs, axis=1)
    y_ref[...] = _rms(xmid_ref[...] + g2_ref[0] * acc) * gfin_ref[...]


def _final(xmid, g2, z, w_t, gfin, tile, *, n_tiles, x_tile0, z_tile0, w_tile0, tiles_per_g2, y_prev=None):
    args = [xmid, g2, z, w_t, gfin]
    in_specs = [pl.BlockSpec((tile, D_MODEL), lambda i: (x_tile0 + i, 0)),
                pl.BlockSpec((1, g2.shape[1], D_MODEL), lambda i: ((x_tile0 + i) // tiles_per_g2, 0, 0)),
                pl.BlockSpec((TOP_K, 2, tile, PLANE_W), lambda i: (0, 0, z_tile0 + i, 0)),
                pl.BlockSpec((TOP_K, tile), lambda i: (0, w_tile0 + i)),
                pl.BlockSpec((1, D_MODEL), lambda i: (0, 0))]
    aliases = {}
    if y_prev is not None:
        args.append(y_prev)
        in_specs.append(pl.BlockSpec(memory_space=pl.ANY))
        aliases = {len(args) - 1: 0}
    return pl.pallas_call(
        _final_kernel,
        grid=(n_tiles,),
        in_specs=in_specs,
        out_specs=pl.BlockSpec((tile, D_MODEL), lambda i: (x_tile0 + i, 0)),
        out_shape=jax.ShapeDtypeStruct(xmid.shape, F32),
        input_output_aliases=aliases,
        compiler_params=pltpu.CompilerParams(dimension_semantics=("arbitrary",)),
        name="final",
    )(*args)


def kernel(x_prompt, x_sample, state_hgrn, state_conv, c_prompt, c_sample, w_ada, b_ada, norm_mix_g, norm_ffn_g, w_in, lb_logits, hgrn_norm_g, conv_w, conv_b, w_out_hgrn, w_out_conv, w_o, w_router, router_bias, w_exp_gate, w_exp_up, w_exp_down, w_sh_gate, w_sh_up, w_sh_down, final_norm_g):
    assert w_ada.shape[0] == 1 and lb_logits.shape[0] == 2
    bsz, seq, _ = x_prompt.shape
    n_smp = x_sample.shape[0]
    n_prompt = bsz * seq
    n_tok = n_prompt + n_smp

    w_in_b = w_in[0].astype(BF16)
    w_oh_b = w_out_hgrn[0].astype(BF16)
    w_oc_b = w_out_conv[0].astype(BF16)
    w_o_b = w_o[0].astype(BF16)
    wr_t = w_router[0].T
    wr_hi = wr_t.astype(BF16)
    wr_hl = jnp.concatenate([wr_hi, (wr_t - wr_hi.astype(F32)).astype(BF16)], axis=0)
    w_sgu = jnp.concatenate([w_sh_gate[0], w_sh_up[0]], axis=1).astype(BF16)
    w_sd = w_sh_down[0].astype(BF16)
    gmix = norm_mix_g[0].reshape(1, D_MODEL)
    gffn = norm_ffn_g[0].reshape(1, D_MODEL)
    hg = hgrn_norm_g[0].reshape(1, KEY_W)
    cw = conv_w[0]
    cb = conv_b[0].reshape(1, CONV_W)
    gfin = final_norm_g.reshape(1, D_MODEL)

    mod_p, mod_s = _ada(c_prompt, c_sample, w_ada[0], b_ada[0])

    xmid_p, h2_all, lgt_all, s_p, cv_p = _mix(
        x_prompt, mod_p.reshape(bsz, 6, D_MODEL), n_tok, gmix, gffn, w_in_b, lb_logits, hg, cw, cb, w_oh_b, w_oc_b, w_o_b, wr_hl, w_sgu, w_sd)

    xs2 = x_sample.reshape(n_smp, D_MODEL)
    f, kk, q, v, gate, yb, sga, sgb, cv_s = _smp1(
        xs2, mod_s, gmix, w_in_b, lb_logits, cw, cb, state_conv[0].reshape(n_smp, (CONV_K - 1) * CONV_W))
    s_s, o_s = _smp2(f, kk, q, v, state_hgrn[0])
    xmid_s, h2_all, lgt_all = _smp3(xs2, mod_s, o_s, gate, yb, sga, sgb, hg, gffn,
                                    w_oh_b, w_oc_b, w_o_b, wr_hl, w_sgu, w_sd,
                                    h2_all, lgt_all, n_prompt)

    idx, w_tok, rank, cnt = _route(lgt_all, router_bias[0].reshape(N_EXP, 1))

    bm = GMM_BM
    n_blocks = (n_tok * TOP_K + N_EXP * (bm - 1)) // bm
    n_rows = n_blocks * bm
    counts = cnt[:, 0]
    padded = (counts + bm - 1) // bm * bm
    pad_end = jnp.cumsum(padded)
    pad_start = pad_end - padded
    blk_row0 = jnp.arange(n_blocks, dtype=I32) * bm
    blk_exp = jnp.minimum(jnp.sum((pad_end[None, :] <= blk_row0[:, None]).astype(I32), axis=1), N_EXP - 1)
    n_used = (pad_end[-1:] // bm).astype(I32)
    cuts = [0] + [n_prompt * f // FINAL_SPLIT[-1] for f in FINAL_SPLIT]
    assert all(c % FINAL_TILE == 0 for c in cuts) and cuts[-1] == n_prompt
    chunks = tuple((c0, n_tok if c1 == n_prompt else c1) for c0, c1 in zip(cuts[:-1], cuts[1:]))
    dest, *chunk_dest = _dest(pad_start.astype(I32), idx, rank, n_rows, chunks)

    xs = _dispatch(h2_all.reshape(2 * n_tok, PLANE_W), dest, 2 * n_rows).reshape(2, n_rows, PLANE_W)
    ys = _gmm(blk_exp, n_used, xs, w_exp_gate[0], w_exp_up[0], w_exp_down[0])

    ys_flat = ys.reshape(2 * n_rows, PLANE_W)
    w_t = w_tok
    xmid_p2 = xmid_p.reshape(n_prompt, D_MODEL)
    g2_p = mod_p[:, 5 * D_MODEL:].reshape(bsz, 1, D_MODEL)
    g2_s = mod_s[:, 5 * D_MODEL:].reshape(1, n_smp, D_MODEL)
    y_p = None
    for (c0, c1), dest_c in zip(chunks, chunk_dest):
        z = _combine(ys_flat, dest_c).reshape(TOP_K, 2, c1 - c0, PLANE_W)
        y_p = _final(xmid_p2, g2_p, z, w_t, gfin, FINAL_TILE, n_tiles=(min(c1, n_prompt) - c0) // FINAL_TILE,
                     x_tile0=c0 // FINAL_TILE, z_tile0=0, w_tile0=c0 // FINAL_TILE,
                     tiles_per_g2=seq // FINAL_TILE, y_prev=y_p)
    y_s = _final(xmid_s, g2_s, z, w_t, gfin, n_smp, n_tiles=1, x_tile0=0, z_tile0=(n_prompt - c0) // n_smp,
                 w_tile0=n_prompt // n_smp, tiles_per_g2=1)

    return (y_p.reshape(bsz, seq, D_MODEL), y_s.reshape(n_smp, 1, D_MODEL),
            s_p[None], cv_p[None], s_s[None], cv_s.reshape(1, n_smp, CONV_K - 1, CONV_W))
```

```python
import functools

import jax
import jax.numpy as jnp
from jax import lax
from jax.experimental import pallas as pl
from jax.experimental.pallas import tpu as pltpu
from jax.experimental.pallas import tpu_sc as plsc

F32 = jnp.float32
BF16 = jnp.bfloat16
I32 = jnp.int32

D_MODEL = 1024
HALF_D = D_MODEL // 2
HEADS = 4
DK = 128
KEY_W = HEADS * DK
CONV_W = 512
CONV_K = 3
IN_W = 2 * KEY_W + 2 * KEY_W + 3 * CONV_W + 2 * D_MODEL
N_EXP = 64
TOP_K = 8
N_GRP = 8
GRP_SZ = N_EXP // N_GRP
TOPK_GRP = 4
EXP_FF = 256
SH_FF = 256
ROUTED_SCALE = 2.5
EPS = 1e-6

C_Q, C_F, C_I, C_G = 0, 512, 1024, 1536
C_BB, C_CC, C_VB = 2048, 2560, 3072
C_MGA, C_MGB = 3584, 4608

MIX_TILE = 512
SUB = 256
CHUNK = 64
ROUTE_TILE = 384
GMM_BM = 512
GMM_NX = 4
GMM_NY = 3
FINAL_TILE = 512
FINAL_SPLIT = (4, 12, 22, 32)
SC_WINDOW = 128
PLANE_W = HALF_D // 2
VMEM_LIMIT = 56 * 1024 * 1024


def _dot(a, b):
    return jnp.dot(a, b, preferred_element_type=F32)


def _dot_nt(a, b):
    return lax.dot_general(a, b, (((1,), (1,)), ((), ())), preferred_element_type=F32)


def _dot_tn(a, b):
    return lax.dot_general(a, b, (((0,), (0,)), ((), ())), preferred_element_type=F32)


def _sigmoid(x):
    return 0.5 * jnp.tanh(0.5 * x) + 0.5


def _silu(x):
    h = 0.5 * x
    return h * jnp.tanh(h) + h


def _rms(x):
    return x * lax.rsqrt(jnp.mean(x * x, axis=-1, keepdims=True) + EPS)


def _lower_bound(lbl):
    a, b = lbl[0:1], lbl[1:2]
    m = jnp.maximum(a, b)
    ea, eb = jnp.exp(a - m), jnp.exp(b - m)
    return ea / (ea + eb)


def _split3(x):
    hi = x.astype(BF16)
    r1 = x - hi.astype(F32)
    mid = r1.astype(BF16)
    lo = (r1 - mid.astype(F32)).astype(BF16)
    return hi, mid, lo


def _words(lo_b, hi_b):
    lo = lax.shift_right_logical(lax.bitcast_convert_type(lo_b.astype(F32), I32), 16)
    hi = lax.bitcast_convert_type(hi_b.astype(F32), I32) & jnp.int32(-65536)
    return lo | hi


def _halves(w):
    lo = lax.bitcast_convert_type(lax.shift_left(w, 16), F32)
    hi = lax.bitcast_convert_type(w & jnp.int32(-65536), F32)
    return lo.astype(BF16), hi.astype(BF16)


def _pack_rows(xb, out_ref):
    words = _words(xb[:, :HALF_D], xb[:, HALF_D:])
    out_ref[0] = words[:, :PLANE_W]
    out_ref[1] = words[:, PLANE_W:]


def _unpack_rows(p0, p1):
    c0, c2 = _halves(p0)
    c1, c3 = _halves(p1)
    return c0, c1, c2, c3


def _ada_kernel(cp_ref, cs_ref, w_ref, b_ref, op_ref, os_ref):
    w = w_ref[...].astype(BF16)
    for c_ref, o_ref in ((cp_ref, op_ref), (cs_ref, os_ref)):
        o_ref[...] = _dot(_silu(c_ref[...]).astype(BF16), w) + b_ref[...]


def _ada(c_prompt, c_sample, w_ada, b_ada):
    blk = 1024
    rows = lambda c: pl.BlockSpec((c.shape[0], D_MODEL), lambda j: (0, 0))
    cols = lambda c: pl.BlockSpec((c.shape[0], blk), lambda j: (0, j))
    return pl.pallas_call(
        _ada_kernel,
        grid=(6 * D_MODEL // blk,),
        in_specs=[rows(c_prompt), rows(c_sample),
                  pl.BlockSpec((D_MODEL, blk), lambda j: (0, j)),
                  pl.BlockSpec((1, blk), lambda j: (0, j))],
        out_specs=[cols(c_prompt), cols(c_sample)],
        out_shape=[jax.ShapeDtypeStruct((c.shape[0], 6 * D_MODEL), F32) for c in (c_prompt, c_sample)],
        name="ada",
    )(c_prompt, c_sample, w_ada, b_ada.reshape(1, -1))


def _ffn_pre(x1, mod_rows, gffn, w_sgu, w_sd, wr_hl):
    sh2, sc2, g2 = mod_rows
    h2 = _rms(x1) * gffn * (1.0 + sc2) + sh2
    h2b = h2.astype(BF16)
    gu = _dot(h2b, w_sgu)
    act = _silu(gu[:, :SH_FF]) * gu[:, SH_FF:]
    xmid = x1 + g2 * _dot(act.astype(BF16), w_sd)
    h2lo = (h2 - h2b.astype(F32)).astype(BF16)
    both = _dot_nt(wr_hl, h2b)
    lgt = both[:N_EXP] + both[N_EXP:] + _dot_nt(wr_hl[:N_EXP], h2lo)
    return xmid, h2b, lgt


def _mix_kernel(*refs, nt, n_tiles):
    i = pl.program_id(0)
    h2_ref, lgt_ref = refs[16], refs[17]

    @pl.when(i == n_tiles)
    def _():
        h2_ref[...] = jnp.zeros_like(h2_ref)
        lgt_ref[...] = jnp.zeros_like(lgt_ref)

    @pl.when(i < n_tiles)
    def _():
        _mix_tile(i % nt, nt, *refs)


def _mix_tile(t, nt, x_ref, mod_ref, gmix_ref, gffn_ref, w_in_ref, lbl_ref, hg_ref, cw_ref, cb_ref,
              w_oh_ref, w_oc_ref, w_o_ref, wr_hl_ref, w_sgu_ref, w_sd_ref,
              xmid_ref, h2_ref, lgt_ref, s_out_ref, cv_out_ref,
              proj_ref, st_ref, cbuf_ref, ya_ref):
    tt = x_ref.shape[1]

    @pl.when(t == 0)
    def _():
        st_ref[...] = jnp.zeros_like(st_ref)
        cbuf_ref[...] = jnp.zeros_like(cbuf_ref)

    x = x_ref[0]
    mod = mod_ref[0]
    sh1, sc1, g1 = mod[0:1], mod[1:2], mod[2:3]
    h = _rms(x) * gmix_ref[...] * (1.0 + sc1) + sh1
    hb = h.astype(BF16)
    for c in range(0, IN_W, 512):
        proj_ref[:, c:c + 512] = _dot(hb, w_in_ref[:, c:c + 512])

    lb = _lower_bound(lbl_ref[...])
    row = lax.broadcasted_iota(I32, (SUB, SUB), 0)
    col = lax.broadcasted_iota(I32, (SUB, SUB), 1)
    tri = (col <= row).astype(BF16)
    mask_d = (col <= row) & (row // CHUNK == col // CHUNK)
    mask_a = row // (2 * CHUNK) == col // (2 * CHUNK)
    n_ch = SUB // CHUNK

    def by_chunk(vals):
        return jnp.concatenate([jnp.zeros((CHUNK, DK), F32) if v is None
                                else jnp.broadcast_to(v, (CHUNK, DK)) for v in vals], axis=0)

    for s in range(tt // SUB):
        r0 = s * SUB
        f = lb + (1.0 - lb) * _sigmoid(proj_ref[r0:r0 + SUB, C_F:C_F + KEY_W])
        kk = 1.0 - f
        hi, mid, lo = _split3(jnp.log(f))
        bc = _dot(tri, hi) + _dot(tri, mid) + _dot(tri, lo)
        for hd in range(HEADS):
            hs = slice(hd * DK, (hd + 1) * DK)
            bh = bc[:, hs]
            at = lambda r: bh[r:r + 1]
            mids = [at(c * CHUNK + CHUNK // 2 - 1) for c in range(n_ch)]
            pair_mid = [at(CHUNK - 1), at(3 * CHUNK - 1)]
            step_mid, step_end = at(2 * CHUNK - 1), at(SUB - 1)
            arg = bh - by_chunk(mids)
            e_pos, e_neg = jnp.exp(arg), jnp.exp(-arg)
            q = _silu(proj_ref[r0:r0 + SUB, C_Q + hd * DK:C_Q + (hd + 1) * DK])
            v = proj_ref[r0:r0 + SUB, C_I + hd * DK:C_I + (hd + 1) * DK]
            qd = q * e_pos
            kd = kk[:, hs] * e_neg
            q_in = qd * by_chunk([jnp.exp(m) for m in mids])
            k_end = kd * by_chunk([jnp.exp(step_end - m) for m in mids])
            qa = qd * by_chunk([None, jnp.exp(mids[1] - pair_mid[0]), None, jnp.exp(mids[3] - pair_mid[1])])
            ka = kd * by_chunk([jnp.exp(pair_mid[0] - mids[0]), None, jnp.exp(pair_mid[1] - mids[2]), None])
            qb = qd * by_chunk([None, None, jnp.exp(mids[2] - step_mid), jnp.exp(mids[3] - step_mid)])
            kb = kd * by_chunk([jnp.exp(step_mid - mids[0]), jnp.exp(step_mid - mids[1]), None, None])
            att = jnp.where(mask_d, _dot_nt(qd.astype(BF16), kd.astype(BF16)), 0.0)
            att = att + jnp.where(mask_a, _dot_nt(qa.astype(BF16), ka.astype(BF16)), 0.0)
            att = att + _dot_nt(qb.astype(BF16), kb.astype(BF16))
            vb = v.astype(BF16)
            st = st_ref[hd]
            o = _dot(att.astype(BF16), vb) + _dot_nt(q_in.astype(BF16), st.astype(BF16))
            st_ref[hd] = st * jnp.exp(step_end) + _dot_tn(vb, k_end.astype(BF16))
            gate = _silu(proj_ref[r0:r0 + SUB, C_G + hd * DK:C_G + (hd + 1) * DK])
            ya_ref[r0:r0 + SUB, hs] = _rms(o) * hg_ref[:, hs] * gate

    u = proj_ref[:, C_CC:C_CC + CONV_W] * proj_ref[:, C_VB:C_VB + CONV_W]
    rows = lax.broadcasted_iota(I32, (tt, CONV_W), 0)
    c0, c1 = cbuf_ref[0:1], cbuf_ref[1:2]
    u1 = jnp.where(rows == 0, c1, pltpu.roll(u, 1, axis=0))
    u2 = jnp.where(rows == 0, c0, jnp.where(rows == 1, c1, pltpu.roll(u, 2, axis=0)))
    conv = cw_ref[0:1] * u2 + cw_ref[1:2] * u1 + cw_ref[2:3] * u + cb_ref[...]
    yb = proj_ref[:, C_BB:C_BB + CONV_W] * conv
    cbuf_ref[...] = u[tt - 2:tt]

    mixed = (_sigmoid(proj_ref[:, C_MGA:C_MGA + D_MODEL]) * _dot(ya_ref[...].astype(BF16), w_oh_ref[...])
             + _sigmoid(proj_ref[:, C_MGB:C_MGB + D_MODEL]) * _dot(yb.astype(BF16), w_oc_ref[...]))
    x1 = x + g1 * _dot(mixed.astype(BF16), w_o_ref[...])

    xmid, h2b, lgt = _ffn_pre(x1, (mod[3:4], mod[4:5], mod[5:6]), gffn_ref[...],
                              w_sgu_ref[...], w_sd_ref[...], wr_hl_ref[...])
    xmid_ref[0] = xmid
    _pack_rows(h2b, h2_ref)
    lgt_ref[...] = lgt

    @pl.when(t == nt - 1)
    def _():
        for hd in range(HEADS):
            s_out_ref[0, hd] = st_ref[hd].T
        cv_out_ref[0] = cbuf_ref[...]


def _const_spec(shape):
    nd = len(shape)
    return pl.BlockSpec(shape, lambda i, _nd=nd: (0,) * _nd, pipeline_mode=pl.Buffered(1))


def _mix(x, mod, n_tok, gmix, gffn, w_in, lbl, hg, cw, cb, w_oh, w_oc, w_o, wr_hl, w_sgu, w_sd):
    bsz, seq, _ = x.shape
    tt = MIX_TILE
    nt = seq // tt
    n_tiles = bsz * nt
    assert n_tiles * tt < n_tok <= (n_tiles + 1) * tt
    consts = [gmix, gffn, w_in, lbl, hg, cw, cb, w_oh, w_oc, w_o, wr_hl, w_sgu, w_sd]
    tile = lambda i: jnp.minimum(i, n_tiles - 1)
    return pl.pallas_call(
        functools.partial(_mix_kernel, nt=nt, n_tiles=n_tiles),
        grid=(n_tiles + 1,),
        in_specs=[pl.BlockSpec((1, tt, D_MODEL), lambda i: (tile(i) // nt, tile(i) % nt, 0)),
                  pl.BlockSpec((1, 6, D_MODEL), lambda i: (tile(i) // nt, 0, 0))]
                 + [_const_spec(a.shape) for a in consts],
        out_specs=[pl.BlockSpec((1, tt, D_MODEL), lambda i: (tile(i) // nt, tile(i) % nt, 0)),
                   pl.BlockSpec((2, tt, PLANE_W), lambda i: (0, i, 0)),
                   pl.BlockSpec((N_EXP, tt), lambda i: (0, i)),
                   pl.BlockSpec((1, HEADS, DK, DK), lambda i: (tile(i) // nt, 0, 0, 0)),
                   pl.BlockSpec((1, CONV_K - 1, CONV_W), lambda i: (tile(i) // nt, 0, 0))],
        out_shape=[jax.ShapeDtypeStruct((bsz, seq, D_MODEL), F32),
                   jax.ShapeDtypeStruct((2, n_tok, PLANE_W), I32),
                   jax.ShapeDtypeStruct((N_EXP, n_tok), F32),
                   jax.ShapeDtypeStruct((bsz, HEADS, DK, DK), F32),
                   jax.ShapeDtypeStruct((bsz, CONV_K - 1, CONV_W), F32)],
        scratch_shapes=[pltpu.VMEM((tt, IN_W), F32),
                        pltpu.VMEM((HEADS, DK, DK), F32),
                        pltpu.VMEM((CONV_K - 1, CONV_W), F32),
                        pltpu.VMEM((tt, KEY_W), F32)],
        compiler_params=pltpu.CompilerParams(
            dimension_semantics=("arbitrary",), vmem_limit_bytes=VMEM_LIMIT),
        name="mix",
    )(x, mod, *consts)


def _smp1_kernel(x_ref, mod_ref, gmix_ref, w_in_ref, lbl_ref, cw_ref, cb_ref, cst_ref,
                 f_ref, k_ref, q_ref, v_ref, gate_ref, yb_ref, sga_ref, sgb_ref, cv_out_ref):
    x = x_ref[...]
    sh1, sc1 = mod_ref[:, 0:D_MODEL], mod_ref[:, D_MODEL:2 * D_MODEL]
    h = _rms(x) * gmix_ref[...] * (1.0 + sc1) + sh1
    hb = h.astype(BF16)

    def proj(c, w):
        return _dot(hb, w_in_ref[:, c:c + w])

    lb = _lower_bound(lbl_ref[...])
    f = lb + (1.0 - lb) * _sigmoid(proj(C_F, KEY_W))
    f_ref[...] = f
    k_ref[...] = 1.0 - f
    q_ref[...] = _silu(proj(C_Q, KEY_W))
    v_ref[...] = proj(C_I, KEY_W)
    gate_ref[...] = _silu(proj(C_G, KEY_W))
    u = proj(C_CC, CONV_W) * proj(C_VB, CONV_W)
    c0, c1 = cst_ref[:, 0:CONV_W], cst_ref[:, CONV_W:2 * CONV_W]
    conv = cw_ref[0:1] * c0 + cw_ref[1:2] * c1 + cw_ref[2:3] * u + cb_ref[...]
    yb_ref[...] = proj(C_BB, CONV_W) * conv
    cv_out_ref[:, 0:CONV_W] = c1
    cv_out_ref[:, CONV_W:2 * CONV_W] = u
    sga_ref[...] = _sigmoid(proj(C_MGA, D_MODEL))
    sgb_ref[...] = _sigmoid(proj(C_MGB, D_MODEL))


def _smp1(x, mod, gmix, w_in, lbl, cw, cb, cst):
    n = x.shape[0]
    kw = jax.ShapeDtypeStruct((n, KEY_W), F32)
    dm = jax.ShapeDtypeStruct((n, D_MODEL), F32)
    return pl.pallas_call(
        _smp1_kernel,
        out_shape=[kw, kw, kw, kw, kw, kw, dm, dm,
                   jax.ShapeDtypeStruct((n, (CONV_K - 1) * CONV_W), F32)],
        compiler_params=pltpu.CompilerParams(vmem_limit_bytes=VMEM_LIMIT),
        name="smp1",
    )(x, mod, gmix, w_in, lbl, cw, cb, cst)


def _smp2(f, k, q, v, state):
    n = f.shape[0]
    info = plsc.get_sparse_core_info()
    lanes = info.num_lanes
    n_workers = info.num_cores * info.num_subcores
    tok_per = n // n_workers
    n_chunks = DK // lanes
    vec = lambda: pltpu.VMEM((DK,), F32)

    @pl.kernel(out_type=[jax.ShapeDtypeStruct(state.shape, F32), jax.ShapeDtypeStruct((n, KEY_W), F32)],
               mesh=_sc_mesh(), scratch_types=[pltpu.VMEM((DK, DK), F32), vec(), vec(), vec(), vec(), vec()],
               compiler_params=pltpu.CompilerParams(needs_layout_passes=False), name="smp2")
    def run(f_hbm, k_hbm, q_hbm, v_hbm, s_hbm, s_out_hbm, o_hbm, s_v, f_v, k_v, q_v, v_v, o_v):
        wid = lax.axis_index("subcore") * info.num_cores + lax.axis_index("core")

        def tile(j, carry):
            t = wid * tok_per + j // HEADS
            hcols = pl.ds((j % HEADS) * DK, DK)
            pltpu.sync_copy(s_hbm.at[t, j % HEADS], s_v)
            for src, dst in ((f_hbm, f_v), (k_hbm, k_v), (q_hbm, q_v), (v_hbm, v_v)):
                pltpu.sync_copy(src.at[t, hcols], dst)
            v_chunks = [v_v[pl.ds(c * lanes, lanes)] for c in range(n_chunks)]

            def row(d, acc):
                at = [jnp.zeros((lanes,), I32) + d]
                fd, kd, qd = (plsc.load_gather(r, at) for r in (f_v, k_v, q_v))
                out = []
                for c in range(n_chunks):
                    cols = pl.ds(c * lanes, lanes)
                    new = s_v[d, cols] * fd + kd * v_chunks[c]
                    s_v[d, cols] = new
                    out.append(acc[c] + qd * new)
                return tuple(out)

            acc = lax.fori_loop(0, DK, row, tuple(jnp.zeros((lanes,), F32) for _ in range(n_chunks)))
            for c in range(n_chunks):
                o_v[pl.ds(c * lanes, lanes)] = acc[c]
            pltpu.sync_copy(s_v, s_out_hbm.at[t, j % HEADS])
            pltpu.sync_copy(o_v, o_hbm.at[t, hcols])
            return carry

        lax.fori_loop(0, tok_per * HEADS, tile, 0)

    return run(f, k, q, v, state)


def _smp3_kernel(x_ref, mod_ref, o_ref, gate_ref, yb_ref, sga_ref, sgb_ref, hg_ref, gffn_ref,
                 w_oh_ref, w_oc_ref, w_o_ref, wr_hl_ref, w_sgu_ref, w_sd_ref,
                 h2_all_ref, lgt_all_ref, xmid_ref, h2_ref, lgt_ref):
    del h2_all_ref, lgt_all_ref
    parts = []
    for hd in range(HEADS):
        hs = slice(hd * DK, (hd + 1) * DK)
        parts.append(_rms(o_ref[:, hs]) * hg_ref[:, hs] * gate_ref[:, hs])
    ya = jnp.concatenate(parts, axis=1)
    mixed = (sga_ref[...] * _dot(ya.astype(BF16), w_oh_ref[...])
             + sgb_ref[...] * _dot(yb_ref[...].astype(BF16), w_oc_ref[...]))
    g1 = mod_ref[:, 2 * D_MODEL:3 * D_MODEL]
    x1 = x_ref[...] + g1 * _dot(mixed.astype(BF16), w_o_ref[...])
    mod_rows = tuple(mod_ref[:, j * D_MODEL:(j + 1) * D_MODEL] for j in (3, 4, 5))
    xmid, h2b, lgt = _ffn_pre(x1, mod_rows, gffn_ref[...], w_sgu_ref[...], w_sd_ref[...],
                              wr_hl_ref[...])
    xmid_ref[...] = xmid
    _pack_rows(h2b, h2_ref)
    lgt_ref[...] = lgt


def _smp3(x, mod, o, gate, yb, sga, sgb, hg, gffn, w_oh, w_oc, w_o, wr_hl, w_sgu, w_sd,
          h2_all, lgt_all, n_prompt):
    n = x.shape[0]
    vmem_args = [x, mod, o, gate, yb, sga, sgb, hg, gffn, w_oh, w_oc, w_o, wr_hl, w_sgu, w_sd]
    blk = n_prompt // n

    def full(a):
        nd = a.ndim
        return pl.BlockSpec(a.shape, lambda i, _nd=nd: (0,) * _nd)

    return pl.pallas_call(
        _smp3_kernel,
        grid=(1,),
        in_specs=[full(a) for a in vmem_args]
                 + [pl.BlockSpec(memory_space=pl.ANY), pl.BlockSpec(memory_space=pl.ANY)],
        out_specs=[pl.BlockSpec((n, D_MODEL), lambda i: (0, 0)),
                   pl.BlockSpec((2, n, PLANE_W), lambda i: (0, blk, 0)),
                   pl.BlockSpec((N_EXP, n), lambda i: (0, blk))],
        out_shape=[jax.ShapeDtypeStruct((n, D_MODEL), F32),
                   jax.ShapeDtypeStruct(h2_all.shape, h2_all.dtype),
                   jax.ShapeDtypeStruct(lgt_all.shape, lgt_all.dtype)],
        input_output_aliases={len(vmem_args): 1, len(vmem_args) + 1: 2},
        compiler_params=pltpu.CompilerParams(
            dimension_semantics=("arbitrary",), vmem_limit_bytes=VMEM_LIMIT),
        name="smp3",
    )(*vmem_args, h2_all, lgt_all)


def _route_kernel(lgt_ref, bias_ref, idx_ref, w_ref, rank_ref, cnt_ref):
    tr = ROUTE_TILE
    n_tiles = lgt_ref.shape[1] // tr

    def tile(i, carry):
        cols = pl.ds(pl.multiple_of(i * tr, tr), tr)
        picks, weights, ranks, carry = _route_tile(lgt_ref[:, cols], bias_ref[...], carry)
        for k in range(TOP_K):
            idx_ref[k:k + 1, cols] = picks[k]
            w_ref[k:k + 1, cols] = weights[k]
            rank_ref[k:k + 1, cols] = ranks[k]
        return carry

    total = lax.fori_loop(0, n_tiles, tile, jnp.zeros((N_EXP, 1), F32))
    cnt_ref[...] = jnp.broadcast_to(total, cnt_ref.shape).astype(I32)


def _route_tile(lgt, bias, carry):
    tr = lgt.shape[1]
    neg = -jnp.inf
    scores = _sigmoid(lgt)
    sel = scores + bias
    j8 = lax.broadcasted_iota(I32, (GRP_SZ, tr), 0)
    groups = [sel[g * GRP_SZ:(g + 1) * GRP_SZ] for g in range(N_GRP)]
    gscore = []
    for grp in groups:
        m1 = jnp.max(grp, axis=0, keepdims=True)
        i1 = jnp.min(jnp.where(grp == m1, j8, GRP_SZ), axis=0, keepdims=True)
        m2 = jnp.max(jnp.where(j8 == i1, neg, grp), axis=0, keepdims=True)
        gscore.append(m1 + m2)
    kept = []
    for g in range(N_GRP):
        beaten = jnp.zeros((1, tr), I32)
        for o in range(N_GRP):
            if o < g:
                beaten = beaten + (gscore[o] >= gscore[g]).astype(I32)
            elif o > g:
                beaten = beaten + (gscore[o] > gscore[g]).astype(I32)
        kept.append(jnp.where(beaten < TOPK_GRP, groups[g], neg))
    masked = jnp.concatenate(kept, axis=0)
    ei = lax.broadcasted_iota(I32, masked.shape, 0)
    chosen = jnp.zeros(masked.shape, jnp.bool_)
    picks, weights = [], []
    for _ in range(TOP_K):
        m = jnp.max(masked, axis=0, keepdims=True)
        pick = jnp.min(jnp.where((masked == m) & ~chosen, ei, N_EXP), axis=0, keepdims=True)
        hit = ei == pick
        weights.append(jnp.sum(jnp.where(hit, scores, 0.0), axis=0, keepdims=True))
        picks.append(pick)
        chosen = chosen | hit
        masked = jnp.where(hit, neg, masked)
    wsum = weights[0]
    for w in weights[1:]:
        wsum = wsum + w
    sel01 = chosen.astype(F32)
    r = lax.broadcasted_iota(I32, (tr, tr), 0)
    c = lax.broadcasted_iota(I32, (tr, tr), 1)
    before = (r < c).astype(BF16)
    cnt = _dot(sel01.astype(BF16), before) + carry
    weights = [w / wsum * ROUTED_SCALE for w in weights]
    ranks = [jnp.sum(jnp.where(ei == p, cnt, 0.0), axis=0, keepdims=True).astype(I32) for p in picks]
    return picks, weights, ranks, carry + jnp.sum(sel01, axis=1, keepdims=True)


def _route(lgt, bias):
    n = lgt.shape[1]
    assert n % ROUTE_TILE == 0
    slot = lambda dt: jax.ShapeDtypeStruct((TOP_K, n), dt)
    return pl.pallas_call(
        _route_kernel,
        out_shape=[slot(I32), slot(F32), slot(I32),
                   jax.ShapeDtypeStruct((N_EXP, 128), I32)],
        name="route",
    )(lgt, bias)


def _dest_kernel(start_ref, idx_ref, rank_ref, all_ref, *chunk_refs, n_rows, chunks):
    n_tok = idx_ref.shape[1]
    idx = idx_ref[...]
    acc = rank_ref[...]
    for e in range(N_EXP):
        acc = acc + jnp.where(idx == e, start_ref[e], 0)
    for k in range(TOP_K):
        for p in range(2):
            row = acc[k:k + 1, :] + p * n_rows
            seg = 2 * k + p
            all_ref[:, seg * n_tok:(seg + 1) * n_tok] = row
            for (c0, c1), ref in zip(chunks, chunk_refs):
                ref[:, seg * (c1 - c0):(seg + 1) * (c1 - c0)] = row[:, c0:c1]


def _dest(pad_start, idx, rank, n_rows, chunks):
    k, n_tok = idx.shape
    vmem = pl.BlockSpec(memory_space=pltpu.VMEM)
    sizes = [n_tok] + [c1 - c0 for c0, c1 in chunks]
    return pl.pallas_call(
        functools.partial(_dest_kernel, n_rows=n_rows, chunks=chunks),
        in_specs=[pl.BlockSpec(memory_space=pltpu.SMEM), vmem, vmem],
        out_specs=[vmem] * len(sizes),
        out_shape=[jax.ShapeDtypeStruct((1, 2 * k * n), I32) for n in sizes],
        name="dest",
    )(pad_start, idx, rank)


def _sc_mesh():
    return plsc.VectorSubcoreMesh(core_axis_name="core", subcore_axis_name="subcore")


def _dispatch(rows, dest, n_out):
    n, width = rows.shape
    win = SC_WINDOW
    steps = n // win

    @pl.kernel(out_type=jax.ShapeDtypeStruct((n_out, width), rows.dtype), mesh=_sc_mesh(),
               scratch_types=[], name="dispatch")
    def run(x_hbm, *refs):
        i_hbms, o_hbm = refs[:TOP_K], refs[TOP_K]

        def body(x_vmem, *i_vmems):
            for i_vmem in i_vmems:
                pltpu.sync_copy(x_vmem, o_hbm.at[i_vmem.at[0]])

        pltpu.emit_pipeline(
            body,
            grid=(steps,),
            in_specs=[pl.BlockSpec((win, width), lambda i: (i, 0))]
                     + [pl.BlockSpec((1, win), lambda i, k=k: (0, k * steps + i)) for k in range(TOP_K)],
            out_specs=[],
            core_axis_name=("core", "subcore"),
            dimension_semantics=(pltpu.PARALLEL,),
        )(x_hbm, *i_hbms)

    assert dest.shape == (1, TOP_K * n)
    return run(rows, *([dest] * TOP_K))


def _combine(rows, dest_flat):
    width = rows.shape[1]
    n = dest_flat.shape[1]
    win = SC_WINDOW

    @pl.kernel(out_type=jax.ShapeDtypeStruct((n, width), rows.dtype), mesh=_sc_mesh(),
               scratch_types=[], name="combine")
    def run(y_hbm, i_hbm, o_hbm):
        def body(i_vmem, o_vmem):
            pltpu.sync_copy(y_hbm.at[i_vmem.at[0]], o_vmem)

        pltpu.emit_pipeline(
            body,
            grid=(n // win,),
            in_specs=[pl.BlockSpec((1, win), lambda i: (0, i))],
            out_specs=[pl.BlockSpec((win, width), lambda i: (i, 0))],
            core_axis_name=("core", "subcore"),
            dimension_semantics=(pltpu.PARALLEL,),
        )(i_hbm, o_hbm)

    return run(rows, dest_flat)


def _gmm_kernel(blk_exp_ref, n_used_ref, xs_hbm, wg_hbm, wu_hbm, wd_hbm, ys_hbm,
                xbuf, ybuf, wg32, wu32, wd32, wgu_b, wd_b, xsem, ysem, wsem, run_ref):
    nx, ny, bm = xbuf.shape[0], ybuf.shape[0], xbuf.shape[2]
    n_used = n_used_ref[0]

    def x_copies(b):
        rows, slot = pl.ds(b * bm, bm), b % nx
        return [pltpu.make_async_copy(xs_hbm.at[p, rows, :], xbuf.at[slot, p], xsem.at[slot, p]) for p in range(2)]

    def y_copies(b):
        rows, slot = pl.ds(b * bm, bm), b % ny
        return [pltpu.make_async_copy(ybuf.at[slot, p], ys_hbm.at[p, rows, :], ysem.at[slot, p]) for p in range(2)]

    def start(copies):
        for c in copies:
            c.start()

    def wait(copies):
        for c in copies:
            c.wait()

    def w_copies(e, slot):
        return (pltpu.make_async_copy(wg_hbm.at[e], wg32.at[slot], wsem.at[slot, 0]),
                pltpu.make_async_copy(wu_hbm.at[e], wu32.at[slot], wsem.at[slot, 1]),
                pltpu.make_async_copy(wd_hbm.at[e], wd32.at[slot], wsem.at[slot, 2]))

    def run_end(b):
        return lax.while_loop(lambda j: (j < n_used) & (blk_exp_ref[jnp.minimum(j, n_used - 1)] == blk_exp_ref[b]),
                              lambda j: j + 1, b + 1)

    run_ref[0] = 0
    start(x_copies(0))
    start(w_copies(blk_exp_ref[0], 0))
    for j in range(1, nx - 1):
        @pl.when(j < n_used)
        def _():
            start(x_copies(j))

    def block(b, carry):
        @pl.when(b + nx - 1 < n_used)
        def _():
            start(x_copies(b + nx - 1))

        @pl.when((b == 0) | (blk_exp_ref[b] != blk_exp_ref[jnp.maximum(b - 1, 0)]))
        def _():
            wslot = run_ref[0] % 2
            run_ref[0] = run_ref[0] + 1
            wait(w_copies(blk_exp_ref[b], wslot))
            wgu_b[:, 0:EXP_FF] = wg32[wslot].astype(BF16)
            wgu_b[:, EXP_FF:2 * EXP_FF] = wu32[wslot].astype(BF16)
            wd_b[...] = wd32[wslot].astype(BF16)
            nxt = run_end(b)

            @pl.when(nxt < n_used)
            def _():
                start(w_copies(blk_exp_ref[jnp.minimum(nxt, n_used - 1)], 1 - wslot))

        wait(x_copies(b))

        @pl.when(b >= ny)
        def _():
            wait(y_copies(b - ny))

        xslot = b % nx
        xc = _unpack_rows(xbuf[xslot, 0], xbuf[xslot, 1])
        gu = sum(_dot(c, wgu_b[i * PLANE_W:(i + 1) * PLANE_W, :]) for i, c in enumerate(xc))
        act = (_silu(gu[:, :EXP_FF]) * gu[:, EXP_FF:]).astype(BF16)
        _pack_rows(_dot(act, wd_b[...]).astype(BF16), ybuf.at[b % ny])
        start(y_copies(b))
        return carry

    lax.fori_loop(0, n_used, block, 0)

    for j in range(ny, 0, -1):
        @pl.when(n_used >= j)
        def _():
            wait(y_copies(n_used - j))


def _gmm(blk_exp, n_used, xs, w_gate, w_up, w_down):
    n_rows = xs.shape[1]
    bm = GMM_BM
    nb = n_rows // bm

    assert blk_exp.shape == (nb,)
    any_spec = pl.BlockSpec(memory_space=pl.ANY)
    grid_spec = pltpu.PrefetchScalarGridSpec(
        num_scalar_prefetch=2,
        grid=(1,),
        in_specs=[any_spec, any_spec, any_spec, any_spec],
        out_specs=any_spec,
        scratch_shapes=[pltpu.VMEM((GMM_NX, 2, bm, PLANE_W), I32), pltpu.VMEM((GMM_NY, 2, bm, PLANE_W), I32),
                        pltpu.VMEM((2, D_MODEL, EXP_FF), F32), pltpu.VMEM((2, D_MODEL, EXP_FF), F32),
                        pltpu.VMEM((2, EXP_FF, D_MODEL), F32),
                        pltpu.VMEM((D_MODEL, 2 * EXP_FF), BF16), pltpu.VMEM((EXP_FF, D_MODEL), BF16),
                        pltpu.SemaphoreType.DMA((GMM_NX, 2)), pltpu.SemaphoreType.DMA((GMM_NY, 2)),
                        pltpu.SemaphoreType.DMA((2, 3)), pltpu.SMEM((1,), I32)],
    )
    return pl.pallas_call(
        _gmm_kernel,
        grid_spec=grid_spec,
        out_shape=jax.ShapeDtypeStruct((2, n_rows, PLANE_W), I32),
        compiler_params=pltpu.CompilerParams(dimension_semantics=("arbitrary",)),
        name="gmm",
    )(blk_exp, n_used, xs, w_gate, w_up, w_down)


def _final_kernel(xmid_ref, g2_ref, z_ref, w_ref, gfin_ref, *rest):
    y_ref = rest[-1]
    accs = [jnp.zeros((xmid_ref.shape[0], PLANE_W), F32) for _ in range(4)]
    w_cols = w_ref[...].T
    for k in range(TOP_K):
        wk = w_cols[:, k:k + 1]
        cols = _unpack_rows(z_ref[k, 0], z_ref[k, 1])
        accs = [a + wk * c.astype(F32) for a, c in zip(accs, cols)]
    acc = jnp.concatenate(accs, axis=1)
    y_ref[...] = _rms(xmid_ref[...] + g2_ref[0] * acc) * gfin_ref[...]


def _final(xmid, g2, z, w_t, gfin, tile, *, n_tiles, x_tile0, z_tile0, w_tile0, tiles_per_g2, y_prev=None):
    args = [xmid, g2, z, w_t, gfin]
    in_specs = [pl.BlockSpec((tile, D_MODEL), lambda i: (x_tile0 + i, 0)),
                pl.BlockSpec((1, g2.shape[1], D_MODEL), lambda i: ((x_tile0 + i) // tiles_per_g2, 0, 0)),
                pl.BlockSpec((TOP_K, 2, tile, PLANE_W), lambda i: (0, 0, z_tile0 + i, 0)),
                pl.BlockSpec((TOP_K, tile), lambda i: (0, w_tile0 + i)),
                pl.BlockSpec((1, D_MODEL), lambda i: (0, 0))]
    aliases = {}
    if y_prev is not None:
        args.append(y_prev)
        in_specs.append(pl.BlockSpec(memory_space=pl.ANY))
        aliases = {len(args) - 1: 0}
    return pl.pallas_call(
        _final_kernel,
        grid=(n_tiles,),
        in_specs=in_specs,
        out_specs=pl.BlockSpec((tile, D_MODEL), lambda i: (x_tile0 + i, 0)),
        out_shape=jax.ShapeDtypeStruct(xmid.shape, F32),
        input_output_aliases=aliases,
        compiler_params=pltpu.CompilerParams(dimension_semantics=("arbitrary",)),
        name="final",
    )(*args)


def kernel(x_prompt, x_sample, state_hgrn, state_conv, c_prompt, c_sample, w_ada, b_ada, norm_mix_g, norm_ffn_g, w_in, lb_logits, hgrn_norm_g, conv_w, conv_b, w_out_hgrn, w_out_conv, w_o, w_router, router_bias, w_exp_gate, w_exp_up, w_exp_down, w_sh_gate, w_sh_up, w_sh_down, final_norm_g):
    assert w_ada.shape[0] == 1 and lb_logits.shape[0] == 2
    bsz, seq, _ = x_prompt.shape
    n_smp = x_sample.shape[0]
    n_prompt = bsz * seq
    n_tok = n_prompt + n_smp

    w_in_b = w_in[0].astype(BF16)
    w_oh_b = w_out_hgrn[0].astype(BF16)
    w_oc_b = w_out_conv[0].astype(BF16)
    w_o_b = w_o[0].astype(BF16)
    wr_t = w_router[0].T
    wr_hi = wr_t.astype(BF16)
    wr_hl = jnp.concatenate([wr_hi, (wr_t - wr_hi.astype(F32)).astype(BF16)], axis=0)
    w_sgu = jnp.concatenate([w_sh_gate[0], w_sh_up[0]], axis=1).astype(BF16)
    w_sd = w_sh_down[0].astype(BF16)
    gmix = norm_mix_g[0].reshape(1, D_MODEL)
    gffn = norm_ffn_g[0].reshape(1, D_MODEL)
    hg = hgrn_norm_g[0].reshape(1, KEY_W)
    cw = conv_w[0]
    cb = conv_b[0].reshape(1, CONV_W)
    gfin = final_norm_g.reshape(1, D_MODEL)

    mod_p, mod_s = _ada(c_prompt, c_sample, w_ada[0], b_ada[0])

    xmid_p, h2_all, lgt_all, s_p, cv_p = _mix(
        x_prompt, mod_p.reshape(bsz, 6, D_MODEL), n_tok, gmix, gffn, w_in_b, lb_logits, hg, cw, cb, w_oh_b, w_oc_b, w_o_b, wr_hl, w_sgu, w_sd)

    xs2 = x_sample.reshape(n_smp, D_MODEL)
    f, kk, q, v, gate, yb, sga, sgb, cv_s = _smp1(
        xs2, mod_s, gmix, w_in_b, lb_logits, cw, cb, state_conv[0].reshape(n_smp, (CONV_K - 1) * CONV_W))
    s_s, o_s = _smp2(f, kk, q, v, state_hgrn[0])
    xmid_s, h2_all, lgt_all = _smp3(xs2, mod_s, o_s, gate, yb, sga, sgb, hg, gffn,
                                    w_oh_b, w_oc_b, w_o_b, wr_hl, w_sgu, w_sd,
                                    h2_all, lgt_all, n_prompt)

    idx, w_tok, rank, cnt = _route(lgt_all, router_bias[0].reshape(N_EXP, 1))

    bm = GMM_BM
    n_blocks = (n_tok * TOP_K + N_EXP * (bm - 1)) // bm
    n_rows = n_blocks * bm
    counts = cnt[:, 0]
    padded = (counts + bm - 1) // bm * bm
    pad_end = jnp.cumsum(padded)
    pad_start = pad_end - padded
    blk_row0 = jnp.arange(n_blocks, dtype=I32) * bm
    blk_exp = jnp.minimum(jnp.sum((pad_end[None, :] <= blk_row0[:, None]).astype(I32), axis=1), N_EXP - 1)
    n_used = (pad_end[-1:] // bm).astype(I32)
    cuts = [0] + [n_prompt * f // FINAL_SPLIT[-1] for f in FINAL_SPLIT]
    assert all(c % FINAL_TILE == 0 for c in cuts) and cuts[-1] == n_prompt
    chunks = tuple((c0, n_tok if c1 == n_prompt else c1) for c0, c1 in zip(cuts[:-1], cuts[1:]))
    dest, *chunk_dest = _dest(pad_start.astype(I32), idx, rank, n_rows, chunks)

    xs = _dispatch(h2_all.reshape(2 * n_tok, PLANE_W), dest, 2 * n_rows).reshape(2, n_rows, PLANE_W)
    ys = _gmm(blk_exp, n_used, xs, w_exp_gate[0], w_exp_up[0], w_exp_down[0])

    ys_flat = ys.reshape(2 * n_rows, PLANE_W)
    w_t = w_tok
    xmid_p2 = xmid_p.reshape(n_prompt, D_MODEL)
    g2_p = mod_p[:, 5 * D_MODEL:].reshape(bsz, 1, D_MODEL)
    g2_s = mod_s[:, 5 * D_MODEL:].reshape(1, n_smp, D_MODEL)
    y_p = None
    for (c0, c1), dest_c in zip(chunks, chunk_dest):
        z = _combine(ys_flat, dest_c).reshape(TOP_K, 2, c1 - c0, PLANE_W)
        y_p = _final(xmid_p2, g2_p, z, w_t, gfin, FINAL_TILE, n_tiles=(min(c1, n_prompt) - c0) // FINAL_TILE,
                     x_tile0=c0 // FINAL_TILE, z_tile0=0, w_tile0=c0 // FINAL_TILE,
                     tiles_per_g2=seq // FINAL_TILE, y_prev=y_p)
    y_s = _final(xmid_s, g2_s, z, w_t, gfin, n_smp, n_tiles=1, x_tile0=0, z_tile0=(n_prompt - c0) // n_smp,
                 w_tile0=n_prompt // n_smp, tiles_per_g2=1)

    return (y_p.reshape(bsz, seq, D_MODEL), y_s.reshape(n_smp, 1, D_MODEL),
            s_p[None], cv_p[None], s_s[None], cv_s.reshape(1, n_smp, CONV_K - 1, CONV_W))
```

```python
import functools

import jax
import jax.numpy as jnp
from jax import lax
from jax.experimental import pallas as pl
from jax.experimental.pallas import tpu as pltpu
from jax.experimental.pallas import tpu_sc as plsc

F32 = jnp.float32
BF16 = jnp.bfloat16
I32 = jnp.int32

D_MODEL = 1024
HALF_D = D_MODEL // 2
HEADS = 4
DK = 128
KEY_W = HEADS * DK
CONV_W = 512
CONV_K = 3
IN_W = 2 * KEY_W + 2 * KEY_W + 3 * CONV_W + 2 * D_MODEL
N_EXP = 64
TOP_K = 8
N_GRP = 8
GRP_SZ = N_EXP // N_GRP
TOPK_GRP = 4
EXP_FF = 256
SH_FF = 256
ROUTED_SCALE = 2.5
EPS = 1e-6

C_Q, C_F, C_I, C_G = 0, 512, 1024, 1536
C_BB, C_CC, C_VB = 2048, 2560, 3072
C_MGA, C_MGB = 3584, 4608

MIX_TILE = 512
SUB = 256
CHUNK = 64
ROUTE_TILE = 384
GMM_BM = 512
GMM_NX = 4
GMM_NY = 3
FINAL_TILE = 512
FINAL_SPLIT = (4, 12, 22, 32)
SC_WINDOW = 128
PLANE_W = HALF_D // 2
VMEM_LIMIT = 56 * 1024 * 1024


def _dot(a, b):
    return jnp.dot(a, b, preferred_element_type=F32)


def _dot_nt(a, b):
    return lax.dot_general(a, b, (((1,), (1,)), ((), ())), preferred_element_type=F32)


def _dot_tn(a, b):
    return lax.dot_general(a, b, (((0,), (0,)), ((), ())), preferred_element_type=F32)


def _sigmoid(x):
    return 0.5 * jnp.tanh(0.5 * x) + 0.5


def _silu(x):
    h = 0.5 * x
    return h * jnp.tanh(h) + h


def _rms(x):
    return x * lax.rsqrt(jnp.mean(x * x, axis=-1, keepdims=True) + EPS)


def _lower_bound(lbl):
    a, b = lbl[0:1], lbl[1:2]
    m = jnp.maximum(a, b)
    ea, eb = jnp.exp(a - m), jnp.exp(b - m)
    return ea / (ea + eb)


def _split3(x):
    hi = x.astype(BF16)
    r1 = x - hi.astype(F32)
    mid = r1.astype(BF16)
    lo = (r1 - mid.astype(F32)).astype(BF16)
    return hi, mid, lo


def _words(lo_b, hi_b):
    lo = lax.shift_right_logical(lax.bitcast_convert_type(lo_b.astype(F32), I32), 16)
    hi = lax.bitcast_convert_type(hi_b.astype(F32), I32) & jnp.int32(-65536)
    return lo | hi


def _halves(w):
    lo = lax.bitcast_convert_type(lax.shift_left(w, 16), F32)
    hi = lax.bitcast_convert_type(w & jnp.int32(-65536), F32)
    return lo.astype(BF16), hi.astype(BF16)


def _pack_rows(xb, out_ref):
    words = _words(xb[:, :HALF_D], xb[:, HALF_D:])
    out_ref[0] = words[:, :PLANE_W]
    out_ref[1] = words[:, PLANE_W:]


def _unpack_rows(p0, p1):
    c0, c2 = _halves(p0)
    c1, c3 = _halves(p1)
    return c0, c1, c2, c3


def _ada_kernel(cp_ref, cs_ref, w_ref, b_ref, op_ref, os_ref):
    w = w_ref[...].astype(BF16)
    for c_ref, o_ref in ((cp_ref, op_ref), (cs_ref, os_ref)):
        o_ref[...] = _dot(_silu(c_ref[...]).astype(BF16), w) + b_ref[...]


def _ada(c_prompt, c_sample, w_ada, b_ada):
    blk = 1024
    rows = lambda c: pl.BlockSpec((c.shape[0], D_MODEL), lambda j: (0, 0))
    cols = lambda c: pl.BlockSpec((c.shape[0], blk), lambda j: (0, j))
    return pl.pallas_call(
        _ada_kernel,
        grid=(6 * D_MODEL // blk,),
        in_specs=[rows(c_prompt), rows(c_sample),
                  pl.BlockSpec((D_MODEL, blk), lambda j: (0, j)),
                  pl.BlockSpec((1, blk), lambda j: (0, j))],
        out_specs=[cols(c_prompt), cols(c_sample)],
        out_shape=[jax.ShapeDtypeStruct((c.shape[0], 6 * D_MODEL), F32) for c in (c_prompt, c_sample)],
        name="ada",
    )(c_prompt, c_sample, w_ada, b_ada.reshape(1, -1))


def _ffn_pre(x1, mod_rows, gffn, w_sgu, w_sd, wr_hl):
    sh2, sc2, g2 = mod_rows
    h2 = _rms(x1) * gffn * (1.0 + sc2) + sh2
    h2b = h2.astype(BF16)
    gu = _dot(h2b, w_sgu)
    act = _silu(gu[:, :SH_FF]) * gu[:, SH_FF:]
    xmid = x1 + g2 * _dot(act.astype(BF16), w_sd)
    h2lo = (h2 - h2b.astype(F32)).astype(BF16)
    both = _dot_nt(wr_hl, h2b)
    lgt = both[:N_EXP] + both[N_EXP:] + _dot_nt(wr_hl[:N_EXP], h2lo)
    return xmid, h2b, lgt


def _mix_kernel(*refs, nt, n_tiles):
    i = pl.program_id(0)
    h2_ref, lgt_ref = refs[16], refs[17]

    @pl.when(i == n_tiles)
    def _():
        h2_ref[...] = jnp.zeros_like(h2_ref)
        lgt_ref[...] = jnp.zeros_like(lgt_ref)

    @pl.when(i < n_tiles)
    def _():
        _mix_tile(i % nt, nt, *refs)


def _mix_tile(t, nt, x_ref, mod_ref, gmix_ref, gffn_ref, w_in_ref, lbl_ref, hg_ref, cw_ref, cb_ref,
              w_oh_ref, w_oc_ref, w_o_ref, wr_hl_ref, w_sgu_ref, w_sd_ref,
              xmid_ref, h2_ref, lgt_ref, s_out_ref, cv_out_ref,
              proj_ref, st_ref, cbuf_ref, ya_ref):
    tt = x_ref.shape[1]

    @pl.when(t == 0)
    def _():
        st_ref[...] = jnp.zeros_like(st_ref)
        cbuf_ref[...] = jnp.zeros_like(cbuf_ref)

    x = x_ref[0]
    mod = mod_ref[0]
    sh1, sc1, g1 = mod[0:1], mod[1:2], mod[2:3]
    h = _rms(x) * gmix_ref[...] * (1.0 + sc1) + sh1
    hb = h.astype(BF16)
    for c in range(0, IN_W, 512):
        proj_ref[:, c:c + 512] = _dot(hb, w_in_ref[:, c:c + 512])

    lb = _lower_bound(lbl_ref[...])
    row = lax.broadcasted_iota(I32, (SUB, SUB), 0)
    col = lax.broadcasted_iota(I32, (SUB, SUB), 1)
    tri = (col <= row).astype(BF16)
    mask_d = (col <= row) & (row // CHUNK == col // CHUNK)
    mask_a = row // (2 * CHUNK) == col // (2 * CHUNK)
    n_ch = SUB // CHUNK

    def by_chunk(vals):
        return jnp.concatenate([jnp.zeros((CHUNK, DK), F32) if v is None
                                else jnp.broadcast_to(v, (CHUNK, DK)) for v in vals], axis=0)

    for s in range(tt // SUB):
        r0 = s * SUB
        f = lb + (1.0 - lb) * _sigmoid(proj_ref[r0:r0 + SUB, C_F:C_F + KEY_W])
        kk = 1.0 - f
        hi, mid, lo = _split3(jnp.log(f))
        bc = _dot(tri, hi) + _dot(tri, mid) + _dot(tri, lo)
        for hd in range(HEADS):
            hs = slice(hd * DK, (hd + 1) * DK)
            bh = bc[:, hs]
            at = lambda r: bh[r:r + 1]
            mids = [at(c * CHUNK + CHUNK // 2 - 1) for c in range(n_ch)]
            pair_mid = [at(CHUNK - 1), at(3 * CHUNK - 1)]
            step_mid, step_end = at(2 * CHUNK - 1), at(SUB - 1)
            arg = bh - by_chunk(mids)
            e_pos, e_neg = jnp.exp(arg), jnp.exp(-arg)
            q = _silu(proj_ref[r0:r0 + SUB, C_Q + hd * DK:C_Q + (hd + 1) * DK])
            v = proj_ref[r0:r0 + SUB, C_I + hd * DK:C_I + (hd + 1) * DK]
            qd = q * e_pos
            kd = kk[:, hs] * e_neg
            q_in = qd * by_chunk([jnp.exp(m) for m in mids])
            k_end = kd * by_chunk([jnp.exp(step_end - m) for m in mids])
            qa = qd * by_chunk([None, jnp.exp(mids[1] - pair_mid[0]), None, jnp.exp(mids[3] - pair_mid[1])])
            ka = kd * by_chunk([jnp.exp(pair_mid[0] - mids[0]), None, jnp.exp(pair_mid[1] - mids[2]), None])
            qb = qd * by_chunk([None, None, jnp.exp(mids[2] - step_mid), jnp.exp(mids[3] - step_mid)])
            kb = kd * by_chunk([jnp.exp(step_mid - mids[0]), jnp.exp(step_mid - mids[1]), None, None])
            att = jnp.where(mask_d, _dot_nt(qd.astype(BF16), kd.astype(BF16)), 0.0)
            att = att + jnp.where(mask_a, _dot_nt(qa.astype(BF16), ka.astype(BF16)), 0.0)
            att = att + _dot_nt(qb.astype(BF16), kb.astype(BF16))
            vb = v.astype(BF16)
            st = st_ref[hd]
            o = _dot(att.astype(BF16), vb) + _dot_nt(q_in.astype(BF16), st.astype(BF16))
            st_ref[hd] = st * jnp.exp(step_end) + _dot_tn(vb, k_end.astype(BF16))
            gate = _silu(proj_ref[r0:r0 + SUB, C_G + hd * DK:C_G + (hd + 1) * DK])
            ya_ref[r0:r0 + SUB, hs] = _rms(o) * hg_ref[:, hs] * gate

    u = proj_ref[:, C_CC:C_CC + CONV_W] * proj_ref[:, C_VB:C_VB + CONV_W]
    rows = lax.broadcasted_iota(I32, (tt, CONV_W), 0)
    c0, c1 = cbuf_ref[0:1], cbuf_ref[1:2]
    u1 = jnp.where(rows == 0, c1, pltpu.roll(u, 1, axis=0))
    u2 = jnp.where(rows == 0, c0, jnp.where(rows == 1, c1, pltpu.roll(u, 2, axis=0)))
    conv = cw_ref[0:1] * u2 + cw_ref[1:2] * u1 + cw_ref[2:3] * u + cb_ref[...]
    yb = proj_ref[:, C_BB:C_BB + CONV_W] * conv
    cbuf_ref[...] = u[tt - 2:tt]

    mixed = (_sigmoid(proj_ref[:, C_MGA:C_MGA + D_MODEL]) * _dot(ya_ref[...].astype(BF16), w_oh_ref[...])
             + _sigmoid(proj_ref[:, C_MGB:C_MGB + D_MODEL]) * _dot(yb.astype(BF16), w_oc_ref[...]))
    x1 = x + g1 * _dot(mixed.astype(BF16), w_o_ref[...])

    xmid, h2b, lgt = _ffn_pre(x1, (mod[3:4], mod[4:5], mod[5:6]), gffn_ref[...],
                              w_sgu_ref[...], w_sd_ref[...], wr_hl_ref[...])
    xmid_ref[0] = xmid
    _pack_rows(h2b, h2_ref)
    lgt_ref[...] = lgt

    @pl.when(t == nt - 1)
    def _():
        for hd in range(HEADS):
            s_out_ref[0, hd] = st_ref[hd].T
        cv_out_ref[0] = cbuf_ref[...]


def _const_spec(shape):
    nd = len(shape)
    return pl.BlockSpec(shape, lambda i, _nd=nd: (0,) * _nd, pipeline_mode=pl.Buffered(1))


def _mix(x, mod, n_tok, gmix, gffn, w_in, lbl, hg, cw, cb, w_oh, w_oc, w_o, wr_hl, w_sgu, w_sd):
    bsz, seq, _ = x.shape
    tt = MIX_TILE
    nt = seq // tt
    n_tiles = bsz * nt
    assert n_tiles * tt < n_tok <= (n_tiles + 1) * tt
    consts = [gmix, gffn, w_in, lbl, hg, cw, cb, w_oh, w_oc, w_o, wr_hl, w_sgu, w_sd]
    tile = lambda i: jnp.minimum(i, n_tiles - 1)
    return pl.pallas_call(
        functools.partial(_mix_kernel, nt=nt, n_tiles=n_tiles),
        grid=(n_tiles + 1,),
        in_specs=[pl.BlockSpec((1, tt, D_MODEL), lambda i: (tile(i) // nt, tile(i) % nt, 0)),
                  pl.BlockSpec((1, 6, D_MODEL), lambda i: (tile(i) // nt, 0, 0))]
                 + [_const_spec(a.shape) for a in consts],
        out_specs=[pl.BlockSpec((1, tt, D_MODEL), lambda i: (tile(i) // nt, tile(i) % nt, 0)),
                   pl.BlockSpec((2, tt, PLANE_W), lambda i: (0, i, 0)),
                   pl.BlockSpec((N_EXP, tt), lambda i: (0, i)),
                   pl.BlockSpec((1, HEADS, DK, DK), lambda i: (tile(i) // nt, 0, 0, 0)),
                   pl.BlockSpec((1, CONV_K - 1, CONV_W), lambda i: (tile(i) // nt, 0, 0))],
        out_shape=[jax.ShapeDtypeStruct((bsz, seq, D_MODEL), F32),
                   jax.ShapeDtypeStruct((2, n_tok, PLANE_W), I32),
                   jax.ShapeDtypeStruct((N_EXP, n_tok), F32),
                   jax.ShapeDtypeStruct((bsz, HEADS, DK, DK), F32),
                   jax.ShapeDtypeStruct((bsz, CONV_K - 1, CONV_W), F32)],
        scratch_shapes=[pltpu.VMEM((tt, IN_W), F32),
                        pltpu.VMEM((HEADS, DK, DK), F32),
                        pltpu.VMEM((CONV_K - 1, CONV_W), F32),
                        pltpu.VMEM((tt, KEY_W), F32)],
        compiler_params=pltpu.CompilerParams(
            dimension_semantics=("arbitrary",), vmem_limit_bytes=VMEM_LIMIT),
        name="mix",
    )(x, mod, *consts)


def _smp1_kernel(x_ref, mod_ref, gmix_ref, w_in_ref, lbl_ref, cw_ref, cb_ref, cst_ref,
                 f_ref, k_ref, q_ref, v_ref, gate_ref, yb_ref, sga_ref, sgb_ref, cv_out_ref):
    x = x_ref[...]
    sh1, sc1 = mod_ref[:, 0:D_MODEL], mod_ref[:, D_MODEL:2 * D_MODEL]
    h = _rms(x) * gmix_ref[...] * (1.0 + sc1) + sh1
    hb = h.astype(BF16)

    def proj(c, w):
        return _dot(hb, w_in_ref[:, c:c + w])

    lb = _lower_bound(lbl_ref[...])
    f = lb + (1.0 - lb) * _sigmoid(proj(C_F, KEY_W))
    f_ref[...] = f
    k_ref[...] = 1.0 - f
    q_ref[...] = _silu(proj(C_Q, KEY_W))
    v_ref[...] = proj(C_I, KEY_W)
    gate_ref[...] = _silu(proj(C_G, KEY_W))
    u = proj(C_CC, CONV_W) * proj(C_VB, CONV_W)
    c0, c1 = cst_ref[:, 0:CONV_W], cst_ref[:, CONV_W:2 * CONV_W]
    conv = cw_ref[0:1] * c0 + cw_ref[1:2] * c1 + cw_ref[2:3] * u + cb_ref[...]
    yb_ref[...] = proj(C_BB, CONV_W) * conv
    cv_out_ref[:, 0:CONV_W] = c1
    cv_out_ref[:, CONV_W:2 * CONV_W] = u
    sga_ref[...] = _sigmoid(proj(C_MGA, D_MODEL))
    sgb_ref[...] = _sigmoid(proj(C_MGB, D_MODEL))


def _smp1(x, mod, gmix, w_in, lbl, cw, cb, cst):
    n = x.shape[0]
    kw = jax.ShapeDtypeStruct((n, KEY_W), F32)
    dm = jax.ShapeDtypeStruct((n, D_MODEL), F32)
    return pl.pallas_call(
        _smp1_kernel,
        out_shape=[kw, kw, kw, kw, kw, kw, dm, dm,
                   jax.ShapeDtypeStruct((n, (CONV_K - 1) * CONV_W), F32)],
        compiler_params=pltpu.CompilerParams(vmem_limit_bytes=VMEM_LIMIT),
        name="smp1",
    )(x, mod, gmix, w_in, lbl, cw, cb, cst)


def _smp2(f, k, q, v, state):
    n = f.shape[0]
    info = plsc.get_sparse_core_info()
    lanes = info.num_lanes
    n_workers = info.num_cores * info.num_subcores
    tok_per = n // n_workers
    n_chunks = DK // lanes
    vec = lambda: pltpu.VMEM((DK,), F32)

    @pl.kernel(out_type=[jax.ShapeDtypeStruct(state.shape, F32), jax.ShapeDtypeStruct((n, KEY_W), F32)],
               mesh=_sc_mesh(), scratch_types=[pltpu.VMEM((DK, DK), F32), vec(), vec(), vec(), vec(), vec()],
               compiler_params=pltpu.CompilerParams(needs_layout_passes=False), name="smp2")
    def run(f_hbm, k_hbm, q_hbm, v_hbm, s_hbm, s_out_hbm, o_hbm, s_v, f_v, k_v, q_v, v_v, o_v):
        wid = lax.axis_index("subcore") * info.num_cores + lax.axis_index("core")

        def tile(j, carry):
            t = wid * tok_per + j // HEADS
            hcols = pl.ds((j % HEADS) * DK, DK)
            pltpu.sync_copy(s_hbm.at[t, j % HEADS], s_v)
            for src, dst in ((f_hbm, f_v), (k_hbm, k_v), (q_hbm, q_v), (v_hbm, v_v)):
                pltpu.sync_copy(src.at[t, hcols], dst)
            v_chunks = [v_v[pl.ds(c * lanes, lanes)] for c in range(n_chunks)]

            def row(d, acc):
                at = [jnp.zeros((lanes,), I32) + d]
                fd, kd, qd = (plsc.load_gather(r, at) for r in (f_v, k_v, q_v))
                out = []
                for c in range(n_chunks):
                    cols = pl.ds(c * lanes, lanes)
                    new = s_v[d, cols] * fd + kd * v_chunks[c]
                    s_v[d, cols] = new
                    out.append(acc[c] + qd * new)
                return tuple(out)

            acc = lax.fori_loop(0, DK, row, tuple(jnp.zeros((lanes,), F32) for _ in range(n_chunks)))
            for c in range(n_chunks):
                o_v[pl.ds(c * lanes, lanes)] = acc[c]
            pltpu.sync_copy(s_v, s_out_hbm.at[t, j % HEADS])
            pltpu.sync_copy(o_v, o_hbm.at[t, hcols])
            return carry

        lax.fori_loop(0, tok_per * HEADS, tile, 0)

    return run(f, k, q, v, state)


def _smp3_kernel(x_ref, mod_ref, o_ref, gate_ref, yb_ref, sga_ref, sgb_ref, hg_ref, gffn_ref,
                 w_oh_ref, w_oc_ref, w_o_ref, wr_hl_ref, w_sgu_ref, w_sd_ref,
                 h2_all_ref, lgt_all_ref, xmid_ref, h2_ref, lgt_ref):
    del h2_all_ref, lgt_all_ref
    parts = []
    for hd in range(HEADS):
        hs = slice(hd * DK, (hd + 1) * DK)
        parts.append(_rms(o_ref[:, hs]) * hg_ref[:, hs] * gate_ref[:, hs])
    ya = jnp.concatenate(parts, axis=1)
    mixed = (sga_ref[...] * _dot(ya.astype(BF16), w_oh_ref[...])
             + sgb_ref[...] * _dot(yb_ref[...].astype(BF16), w_oc_ref[...]))
    g1 = mod_ref[:, 2 * D_MODEL:3 * D_MODEL]
    x1 = x_ref[...] + g1 * _dot(mixed.astype(BF16), w_o_ref[...])
    mod_rows = tuple(mod_ref[:, j * D_MODEL:(j + 1) * D_MODEL] for j in (3, 4, 5))
    xmid, h2b, lgt = _ffn_pre(x1, mod_rows, gffn_ref[...], w_sgu_ref[...], w_sd_ref[...],
                              wr_hl_ref[...])
    xmid_ref[...] = xmid
    _pack_rows(h2b, h2_ref)
    lgt_ref[...] = lgt


def _smp3(x, mod, o, gate, yb, sga, sgb, hg, gffn, w_oh, w_oc, w_o, wr_hl, w_sgu, w_sd,
          h2_all, lgt_all, n_prompt):
    n = x.shape[0]
    vmem_args = [x, mod, o, gate, yb, sga, sgb, hg, gffn, w_oh, w_oc, w_o, wr_hl, w_sgu, w_sd]
    blk = n_prompt // n

    def full(a):
        nd = a.ndim
        return pl.BlockSpec(a.shape, lambda i, _nd=nd: (0,) * _nd)

    return pl.pallas_call(
        _smp3_kernel,
        grid=(1,),
        in_specs=[full(a) for a in vmem_args]
                 + [pl.BlockSpec(memory_space=pl.ANY), pl.BlockSpec(memory_space=pl.ANY)],
        out_specs=[pl.BlockSpec((n, D_MODEL), lambda i: (0, 0)),
                   pl.BlockSpec((2, n, PLANE_W), lambda i: (0, blk, 0)),
                   pl.BlockSpec((N_EXP, n), lambda i: (0, blk))],
        out_shape=[jax.ShapeDtypeStruct((n, D_MODEL), F32),
                   jax.ShapeDtypeStruct(h2_all.shape, h2_all.dtype),
                   jax.ShapeDtypeStruct(lgt_all.shape, lgt_all.dtype)],
        input_output_aliases={len(vmem_args): 1, len(vmem_args) + 1: 2},
        compiler_params=pltpu.CompilerParams(
            dimension_semantics=("arbitrary",), vmem_limit_bytes=VMEM_LIMIT),
        name="smp3",
    )(*vmem_args, h2_all, lgt_all)


def _route_kernel(lgt_ref, bias_ref, idx_ref, w_ref, rank_ref, cnt_ref):
    tr = ROUTE_TILE
    n_tiles = lgt_ref.shape[1] // tr

    def tile(i, carry):
        cols = pl.ds(pl.multiple_of(i * tr, tr), tr)
        picks, weights, ranks, carry = _route_tile(lgt_ref[:, cols], bias_ref[...], carry)
        for k in range(TOP_K):
            idx_ref[k:k + 1, cols] = picks[k]
            w_ref[k:k + 1, cols] = weights[k]
            rank_ref[k:k + 1, cols] = ranks[k]
        return carry

    total = lax.fori_loop(0, n_tiles, tile, jnp.zeros((N_EXP, 1), F32))
    cnt_ref[...] = jnp.broadcast_to(total, cnt_ref.shape).astype(I32)


def _route_tile(lgt, bias, carry):
    tr = lgt.shape[1]
    neg = -jnp.inf
    scores = _sigmoid(lgt)
    sel = scores + bias
    j8 = lax.broadcasted_iota(I32, (GRP_SZ, tr), 0)
    groups = [sel[g * GRP_SZ:(g + 1) * GRP_SZ] for g in range(N_GRP)]
    gscore = []
    for grp in groups:
        m1 = jnp.max(grp, axis=0, keepdims=True)
        i1 = jnp.min(jnp.where(grp == m1, j8, GRP_SZ), axis=0, keepdims=True)
        m2 = jnp.max(jnp.where(j8 == i1, neg, grp), axis=0, keepdims=True)
        gscore.append(m1 + m2)
    kept = []
    for g in range(N_GRP):
        beaten = jnp.zeros((1, tr), I32)
        for o in range(N_GRP):
            if o < g:
                beaten = beaten + (gscore[o] >= gscore[g]).astype(I32)
            elif o > g:
                beaten = beaten + (gscore[o] > gscore[g]).astype(I32)
        kept.append(jnp.where(beaten < TOPK_GRP, groups[g], neg))
    masked = jnp.concatenate(kept, axis=0)
    ei = lax.broadcasted_iota(I32, masked.shape, 0)
    chosen = jnp.zeros(masked.shape, jnp.bool_)
    picks, weights = [], []
    for _ in range(TOP_K):
        m = jnp.max(masked, axis=0, keepdims=True)
        pick = jnp.min(jnp.where((masked == m) & ~chosen, ei, N_EXP), axis=0, keepdims=True)
        hit = ei == pick
        weights.append(jnp.sum(jnp.where(hit, scores, 0.0), axis=0, keepdims=True))
        picks.append(pick)
        chosen = chosen | hit
        masked = jnp.where(hit, neg, masked)
    wsum = weights[0]
    for w in weights[1:]:
        wsum = wsum + w
    sel01 = chosen.astype(F32)
    r = lax.broadcasted_iota(I32, (tr, tr), 0)
    c = lax.broadcasted_iota(I32, (tr, tr), 1)
    before = (r < c).astype(BF16)
    cnt = _dot(sel01.astype(BF16), before) + carry
    weights = [w / wsum * ROUTED_SCALE for w in weights]
    ranks = [jnp.sum(jnp.where(ei == p, cnt, 0.0), axis=0, keepdims=True).astype(I32) for p in picks]
    return picks, weights, ranks, carry + jnp.sum(sel01, axis=1, keepdims=True)


def _route(lgt, bias):
    n = lgt.shape[1]
    assert n % ROUTE_TILE == 0
    slot = lambda dt: jax.ShapeDtypeStruct((TOP_K, n), dt)
    return pl.pallas_call(
        _route_kernel,
        out_shape=[slot(I32), slot(F32), slot(I32),
                   jax.ShapeDtypeStruct((N_EXP, 128), I32)],
        name="route",
    )(lgt, bias)


def _dest_kernel(start_ref, idx_ref, rank_ref, all_ref, *chunk_refs, n_rows, chunks):
    n_tok = idx_ref.shape[1]
    idx = idx_ref[...]
    acc = rank_ref[...]
    for e in range(N_EXP):
        acc = acc + jnp.where(idx == e, start_ref[e], 0)
    for k in range(TOP_K):
        for p in range(2):
            row = acc[k:k + 1, :] + p * n_rows
            seg = 2 * k + p
            all_ref[:, seg * n_tok:(seg + 1) * n_tok] = row
            for (c0, c1), ref in zip(chunks, chunk_refs):
                ref[:, seg * (c1 - c0):(seg + 1) * (c1 - c0)] = row[:, c0:c1]


def _dest(pad_start, idx, rank, n_rows, chunks):
    k, n_tok = idx.shape
    vmem = pl.BlockSpec(memory_space=pltpu.VMEM)
    sizes = [n_tok] + [c1 - c0 for c0, c1 in chunks]
    return pl.pallas_call(
        functools.partial(_dest_kernel, n_rows=n_rows, chunks=chunks),
        in_specs=[pl.BlockSpec(memory_space=pltpu.SMEM), vmem, vmem],
        out_specs=[vmem] * len(sizes),
        out_shape=[jax.ShapeDtypeStruct((1, 2 * k * n), I32) for n in sizes],
        name="dest",
    )(pad_start, idx, rank)


def _sc_mesh():
    return plsc.VectorSubcoreMesh(core_axis_name="core", subcore_axis_name="subcore")


def _dispatch(rows, dest, n_out):
    n, width = rows.shape
    win = SC_WINDOW
    steps = n // win

    @pl.kernel(out_type=jax.ShapeDtypeStruct((n_out, width), rows.dtype), mesh=_sc_mesh(),
               scratch_types=[], name="dispatch")
    def run(x_hbm, *refs):
        i_hbms, o_hbm = refs[:TOP_K], refs[TOP_K]

        def body(x_vmem, *i_vmems):
            for i_vmem in i_vmems:
                pltpu.sync_copy(x_vmem, o_hbm.at[i_vmem.at[0]])

        pltpu.emit_pipeline(
            body,
            grid=(steps,),
            in_specs=[pl.BlockSpec((win, width), lambda i: (i, 0))]
                     + [pl.BlockSpec((1, win), lambda i, k=k: (0, k * steps + i)) for k in range(TOP_K)],
            out_specs=[],
            core_axis_name=("core", "subcore"),
            dimension_semantics=(pltpu.PARALLEL,),
        )(x_hbm, *i_hbms)

    assert dest.shape == (1, TOP_K * n)
    return run(rows, *([dest] * TOP_K))


def _combine(rows, dest_flat):
    width = rows.shape[1]
    n = dest_flat.shape[1]
    win = SC_WINDOW

    @pl.kernel(out_type=jax.ShapeDtypeStruct((n, width), rows.dtype), mesh=_sc_mesh(),
               scratch_types=[], name="combine")
    def run(y_hbm, i_hbm, o_hbm):
        def body(i_vmem, o_vmem):
            pltpu.sync_copy(y_hbm.at[i_vmem.at[0]], o_vmem)

        pltpu.emit_pipeline(
            body,
            grid=(n // win,),
            in_specs=[pl.BlockSpec((1, win), lambda i: (0, i))],
            out_specs=[pl.BlockSpec((win, width), lambda i: (i, 0))],
            core_axis_name=("core", "subcore"),
            dimension_semantics=(pltpu.PARALLEL,),
        )(i_hbm, o_hbm)

    return run(rows, dest_flat)


def _gmm_kernel(blk_exp_ref, n_used_ref, half_ref, xs_hbm, wg_hbm, wu_hbm, wd_hbm, ys_hbm,
                xbuf, ybuf, wg32, wu32, wd32, wgu_b, wd_b, xsem, ysem, wsem, run_ref):
    nx, ny, bm = xbuf.shape[0], ybuf.shape[0], xbuf.shape[2]
    n_used = n_used_ref[0]

    def x_copies(b):
        rows, slot = pl.ds(b * bm, bm), b % nx
        return [pltpu.make_async_copy(xs_hbm.at[p, rows, :], xbuf.at[slot, p], xsem.at[slot, p]) for p in range(2)]

    def y_copies(b):
        rows, slot = pl.ds(b * bm, bm), b % ny
        return [pltpu.make_async_copy(ybuf.at[slot, p], ys_hbm.at[p, rows, :], ysem.at[slot, p]) for p in range(2)]

    def start(copies):
        for c in copies:
            c.start()

    def wait(copies):
        for c in copies:
            c.wait()

    def w_copies(e, slot):
        return (pltpu.make_async_copy(wg_hbm.at[e], wg32.at[slot], wsem.at[slot, 0]),
                pltpu.make_async_copy(wu_hbm.at[e], wu32.at[slot], wsem.at[slot, 1]),
                pltpu.make_async_copy(wd_hbm.at[e], wd32.at[slot], wsem.at[slot, 2]))

    def run_end(b):
        return lax.while_loop(lambda j: (j < n_used) & (blk_exp_ref[jnp.minimum(j, n_used - 1)] == blk_exp_ref[b]),
                              lambda j: j + 1, b + 1)

    run_ref[0] = 0
    start(x_copies(0))
    start(w_copies(blk_exp_ref[0], 0))
    for j in range(1, nx - 1):
        @pl.when(j < n_used)
        def _():
            start(x_copies(j))

    def block(b, carry):
        @pl.when(b + nx - 1 < n_used)
        def _():
            start(x_copies(b + nx - 1))

        @pl.when((b == 0) | (blk_exp_ref[b] != blk_exp_ref[jnp.maximum(b - 1, 0)]))
        def _():
            wslot = run_ref[0] % 2
            run_ref[0] = run_ref[0] + 1
            wait(w_copies(blk_exp_ref[b], wslot))
            wgu_b[:, 0:EXP_FF] = wg32[wslot].astype(BF16)
            wgu_b[:, EXP_FF:2 * EXP_FF] = wu32[wslot].astype(BF16)
            wd_b[...] = wd32[wslot].astype(BF16)
            nxt = run_end(b)

            @pl.when(nxt < n_used)
            def _():
                start(w_copies(blk_exp_ref[jnp.minimum(nxt, n_used - 1)], 1 - wslot))

        wait(x_copies(b))

        @pl.when(b >= ny)
        def _():
            wait(y_copies(b - ny))

        xslot, yslot = b % nx, b % ny

        def compute(n):
            xc = _unpack_rows(xbuf[xslot, 0, 0:n], xbuf[xslot, 1, 0:n])
            gu = sum(_dot(c, wgu_b[i * PLANE_W:(i + 1) * PLANE_W, :]) for i, c in enumerate(xc))
            act = (_silu(gu[:, :EXP_FF]) * gu[:, EXP_FF:]).astype(BF16)
            y = _dot(act, wd_b[...]).astype(BF16)
            words = _words(y[:, :HALF_D], y[:, HALF_D:])
            ybuf[yslot, 0, 0:n] = words[:, :PLANE_W]
            ybuf[yslot, 1, 0:n] = words[:, PLANE_W:]

        lax.cond(half_ref[b] == 1, lambda: compute(bm // 2), lambda: compute(bm))
        start(y_copies(b))
        return carry

    lax.fori_loop(0, n_used, block, 0)

    for j in range(ny, 0, -1):
        @pl.when(n_used >= j)
        def _():
            wait(y_copies(n_used - j))


def _gmm(blk_exp, n_used, blk_half, xs, w_gate, w_up, w_down):
    n_rows = xs.shape[1]
    bm = GMM_BM
    nb = n_rows // bm

    assert blk_exp.shape == (nb,) and blk_half.shape == (nb,)
    any_spec = pl.BlockSpec(memory_space=pl.ANY)
    grid_spec = pltpu.PrefetchScalarGridSpec(
        num_scalar_prefetch=3,
        grid=(1,),
        in_specs=[any_spec, any_spec, any_spec, any_spec],
        out_specs=any_spec,
        scratch_shapes=[pltpu.VMEM((GMM_NX, 2, bm, PLANE_W), I32), pltpu.VMEM((GMM_NY, 2, bm, PLANE_W), I32),
                        pltpu.VMEM((2, D_MODEL, EXP_FF), F32), pltpu.VMEM((2, D_MODEL, EXP_FF), F32),
                        pltpu.VMEM((2, EXP_FF, D_MODEL), F32),
                        pltpu.VMEM((D_MODEL, 2 * EXP_FF), BF16), pltpu.VMEM((EXP_FF, D_MODEL), BF16),
                        pltpu.SemaphoreType.DMA((GMM_NX, 2)), pltpu.SemaphoreType.DMA((GMM_NY, 2)),
                        pltpu.SemaphoreType.DMA((2, 3)), pltpu.SMEM((1,), I32)],
    )
    return pl.pallas_call(
        _gmm_kernel,
        grid_spec=grid_spec,
        out_shape=jax.ShapeDtypeStruct((2, n_rows, PLANE_W), I32),
        compiler_params=pltpu.CompilerParams(dimension_semantics=("arbitrary",)),
        name="gmm",
    )(blk_exp, n_used, blk_half, xs, w_gate, w_up, w_down)


def _final_kernel(xmid_ref, g2_ref, z_ref, w_ref, gfin_ref, *rest):
    y_ref = rest[-1]
    accs = [jnp.zeros((xmid_ref.shape[0], PLANE_W), F32) for _ in range(4)]
    w_cols = w_ref[...].T
    for k in range(TOP_K):
        wk = w_cols[:, k:k + 1]
        cols = _unpack_rows(z_ref[k, 0], z_ref[k, 1])
        accs = [a + wk * c.astype(F32) for a, c in zip(accs, cols)]
    acc = jnp.concatenate(accs, axis=1)
    y_ref[...] = _rms(xmid_ref[...] + g2_ref[0] * acc) * gfin_ref[...]


def _final(xmid, g2, z, w_t, gfin, tile, *, n_tiles, x_tile0, z_tile0, w_tile0, tiles_per_g2, y_prev=None):
    args = [xmid, g2, z, w_t, gfin]
    in_specs = [pl.BlockSpec((tile, D_MODEL), lambda i: (x_tile0 + i, 0)),
                pl.BlockSpec((1, g2.shape[1], D_MODEL), lambda i: ((x_tile0 + i) // tiles_per_g2, 0, 0)),
                pl.BlockSpec((TOP_K, 2, tile, PLANE_W), lambda i: (0, 0, z_tile0 + i, 0)),
                pl.BlockSpec((TOP_K, tile), lambda i: (0, w_tile0 + i)),
                pl.BlockSpec((1, D_MODEL), lambda i: (0, 0))]
    aliases = {}
    if y_prev is not None:
        args.append(y_prev)
        in_specs.append(pl.BlockSpec(memory_space=pl.ANY))
        aliases = {len(args) - 1: 0}
    return pl.pallas_call(
        _final_kernel,
        grid=(n_tiles,),
        in_specs=in_specs,
        out_specs=pl.BlockSpec((tile, D_MODEL), lambda i: (x_tile0 + i, 0)),
        out_shape=jax.ShapeDtypeStruct(xmid.shape, F32),
        input_output_aliases=aliases,
        compiler_params=pltpu.CompilerParams(dimension_semantics=("arbitrary",)),
        name="final",
    )(*args)


def kernel(x_prompt, x_sample, state_hgrn, state_conv, c_prompt, c_sample, w_ada, b_ada, norm_mix_g, norm_ffn_g, w_in, lb_logits, hgrn_norm_g, conv_w, conv_b, w_out_hgrn, w_out_conv, w_o, w_router, router_bias, w_exp_gate, w_exp_up, w_exp_down, w_sh_gate, w_sh_up, w_sh_down, final_norm_g):
    assert w_ada.shape[0] == 1 and lb_logits.shape[0] == 2
    bsz, seq, _ = x_prompt.shape
    n_smp = x_sample.shape[0]
    n_prompt = bsz * seq
    n_tok = n_prompt + n_smp

    w_in_b = w_in[0].astype(BF16)
    w_oh_b = w_out_hgrn[0].astype(BF16)
    w_oc_b = w_out_conv[0].astype(BF16)
    w_o_b = w_o[0].astype(BF16)
    wr_t = w_router[0].T
    wr_hi = wr_t.astype(BF16)
    wr_hl = jnp.concatenate([wr_hi, (wr_t - wr_hi.astype(F32)).astype(BF16)], axis=0)
    w_sgu = jnp.concatenate([w_sh_gate[0], w_sh_up[0]], axis=1).astype(BF16)
    w_sd = w_sh_down[0].astype(BF16)
    gmix = norm_mix_g[0].reshape(1, D_MODEL)
    gffn = norm_ffn_g[0].reshape(1, D_MODEL)
    hg = hgrn_norm_g[0].reshape(1, KEY_W)
    cw = conv_w[0]
    cb = conv_b[0].reshape(1, CONV_W)
    gfin = final_norm_g.reshape(1, D_MODEL)

    mod_p, mod_s = _ada(c_prompt, c_sample, w_ada[0], b_ada[0])

    xmid_p, h2_all, lgt_all, s_p, cv_p = _mix(
        x_prompt, mod_p.reshape(bsz, 6, D_MODEL), n_tok, gmix, gffn, w_in_b, lb_logits, hg, cw, cb, w_oh_b, w_oc_b, w_o_b, wr_hl, w_sgu, w_sd)

    xs2 = x_sample.reshape(n_smp, D_MODEL)
    f, kk, q, v, gate, yb, sga, sgb, cv_s = _smp1(
        xs2, mod_s, gmix, w_in_b, lb_logits, cw, cb, state_conv[0].reshape(n_smp, (CONV_K - 1) * CONV_W))
    s_s, o_s = _smp2(f, kk, q, v, state_hgrn[0])
    xmid_s, h2_all, lgt_all = _smp3(xs2, mod_s, o_s, gate, yb, sga, sgb, hg, gffn,
                                    w_oh_b, w_oc_b, w_o_b, wr_hl, w_sgu, w_sd,
                                    h2_all, lgt_all, n_prompt)

    idx, w_tok, rank, cnt = _route(lgt_all, router_bias[0].reshape(N_EXP, 1))

    bm = GMM_BM
    n_blocks = (n_tok * TOP_K + N_EXP * (bm - 1)) // bm
    n_rows = n_blocks * bm
    counts = cnt[:, 0]
    padded = (counts + bm - 1) // bm * bm
    pad_end = jnp.cumsum(padded)
    pad_start = pad_end - padded
    blk_row0 = jnp.arange(n_blocks, dtype=I32) * bm
    blk_exp = jnp.minimum(jnp.sum((pad_end[None, :] <= blk_row0[:, None]).astype(I32), axis=1), N_EXP - 1)
    n_used = (pad_end[-1:] // bm).astype(I32)
    of_blk = (blk_exp[:, None] == jnp.arange(N_EXP, dtype=I32)[None, :]).astype(I32)
    blk_rows = jnp.sum(of_blk * (counts + pad_start)[None, :], axis=1) - blk_row0
    blk_half = (blk_rows <= bm // 2).astype(I32)
    cuts = [0] + [n_prompt * f // FINAL_SPLIT[-1] for f in FINAL_SPLIT]
    assert all(c % FINAL_TILE == 0 for c in cuts) and cuts[-1] == n_prompt
    chunks = tuple((c0, n_tok if c1 == n_prompt else c1) for c0, c1 in zip(cuts[:-1], cuts[1:]))
    dest, *chunk_dest = _dest(pad_start.astype(I32), idx, rank, n_rows, chunks)

    xs = _dispatch(h2_all.reshape(2 * n_tok, PLANE_W), dest, 2 * n_rows).reshape(2, n_rows, PLANE_W)
    ys = _gmm(blk_exp, n_used, blk_half, xs, w_exp_gate[0], w_exp_up[0], w_exp_down[0])

    ys_flat = ys.reshape(2 * n_rows, PLANE_W)
    w_t = w_tok
    xmid_p2 = xmid_p.reshape(n_prompt, D_MODEL)
    g2_p = mod_p[:, 5 * D_MODEL:].reshape(bsz, 1, D_MODEL)
    g2_s = mod_s[:, 5 * D_MODEL:].reshape(1, n_smp, D_MODEL)
    y_p = None
    for (c0, c1), dest_c in zip(chunks, chunk_dest):
        z = _combine(ys_flat, dest_c).reshape(TOP_K, 2, c1 - c0, PLANE_W)
        y_p = _final(xmid_p2, g2_p, z, w_t, gfin, FINAL_TILE, n_tiles=(min(c1, n_prompt) - c0) // FINAL_TILE,
                     x_tile0=c0 // FINAL_TILE, z_tile0=0, w_tile0=c0 // FINAL_TILE,
                     tiles_per_g2=seq // FINAL_TILE, y_prev=y_p)
    y_s = _final(xmid_s, g2_s, z, w_t, gfin, n_smp, n_tiles=1, x_tile0=0, z_tile0=(n_prompt - c0) // n_smp,
                 w_tile0=n_prompt // n_smp, tiles_per_g2=1)

    return (y_p.reshape(bsz, seq, D_MODEL), y_s.reshape(n_smp, 1, D_MODEL),
            s_p[None], cv_p[None], s_s[None], cv_s.reshape(1, n_smp, CONV_K - 1, CONV_W))
```

```python
import functools

import jax
import jax.numpy as jnp
from jax import lax
from jax.experimental import pallas as pl
from jax.experimental.pallas import tpu as pltpu
from jax.experimental.pallas import tpu_sc as plsc

F32 = jnp.float32
BF16 = jnp.bfloat16
I32 = jnp.int32

D_MODEL = 1024
HALF_D = D_MODEL // 2
HEADS = 4
DK = 128
KEY_W = HEADS * DK
CONV_W = 512
CONV_K = 3
IN_W = 2 * KEY_W + 2 * KEY_W + 3 * CONV_W + 2 * D_MODEL
N_EXP = 64
TOP_K = 8
N_GRP = 8
GRP_SZ = N_EXP // N_GRP
TOPK_GRP = 4
EXP_FF = 256
SH_FF = 256
ROUTED_SCALE = 2.5
EPS = 1e-6

C_Q, C_F, C_I, C_G = 0, 512, 1024, 1536
C_BB, C_CC, C_VB = 2048, 2560, 3072
C_MGA, C_MGB = 3584, 4608

MIX_TILE = 512
SUB = 256
CHUNK = 64
ROUTE_TILE = 384
GMM_BM = 512
GMM_NX = 4
GMM_NY = 3
FINAL_TILE = 512
FINAL_SPLIT = (4, 12, 22, 32)
SC_WINDOW = 128
PLANE_W = HALF_D // 2
VMEM_LIMIT = 56 * 1024 * 1024


def _dot(a, b):
    return jnp.dot(a, b, preferred_element_type=F32)


def _dot_nt(a, b):
    return lax.dot_general(a, b, (((1,), (1,)), ((), ())), preferred_element_type=F32)


def _dot_tn(a, b):
    return lax.dot_general(a, b, (((0,), (0,)), ((), ())), preferred_element_type=F32)


def _sigmoid(x):
    return 0.5 * jnp.tanh(0.5 * x) + 0.5


def _silu(x):
    h = 0.5 * x
    return h * jnp.tanh(h) + h


def _rms(x):
    return x * lax.rsqrt(jnp.mean(x * x, axis=-1, keepdims=True) + EPS)


def _lower_bound(lbl):
    a, b = lbl[0:1], lbl[1:2]
    m = jnp.maximum(a, b)
    ea, eb = jnp.exp(a - m), jnp.exp(b - m)
    return ea / (ea + eb)


def _split3(x):
    hi = x.astype(BF16)
    r1 = x - hi.astype(F32)
    mid = r1.astype(BF16)
    lo = (r1 - mid.astype(F32)).astype(BF16)
    return hi, mid, lo


def _words(lo_b, hi_b):
    lo = lax.shift_right_logical(lax.bitcast_convert_type(lo_b.astype(F32), I32), 16)
    hi = lax.bitcast_convert_type(hi_b.astype(F32), I32) & jnp.int32(-65536)
    return lo | hi


def _halves(w):
    lo = lax.bitcast_convert_type(lax.shift_left(w, 16), F32)
    hi = lax.bitcast_convert_type(w & jnp.int32(-65536), F32)
    return lo.astype(BF16), hi.astype(BF16)


def _pack_rows(xb, out_ref):
    words = _words(xb[:, :HALF_D], xb[:, HALF_D:])
    out_ref[0] = words[:, :PLANE_W]
    out_ref[1] = words[:, PLANE_W:]


def _unpack_rows(p0, p1):
    c0, c2 = _halves(p0)
    c1, c3 = _halves(p1)
    return c0, c1, c2, c3


def _ada_kernel(cp_ref, cs_ref, w_ref, b_ref, op_ref, os_ref):
    w = w_ref[...].astype(BF16)
    for c_ref, o_ref in ((cp_ref, op_ref), (cs_ref, os_ref)):
        o_ref[...] = _dot(_silu(c_ref[...]).astype(BF16), w) + b_ref[...]


def _ada(c_prompt, c_sample, w_ada, b_ada):
    blk = 1024
    rows = lambda c: pl.BlockSpec((c.shape[0], D_MODEL), lambda j: (0, 0))
    cols = lambda c: pl.BlockSpec((c.shape[0], blk), lambda j: (0, j))
    return pl.pallas_call(
        _ada_kernel,
        grid=(6 * D_MODEL // blk,),
        in_specs=[rows(c_prompt), rows(c_sample),
                  pl.BlockSpec((D_MODEL, blk), lambda j: (0, j)),
                  pl.BlockSpec((1, blk), lambda j: (0, j))],
        out_specs=[cols(c_prompt), cols(c_sample)],
        out_shape=[jax.ShapeDtypeStruct((c.shape[0], 6 * D_MODEL), F32) for c in (c_prompt, c_sample)],
        name="ada",
    )(c_prompt, c_sample, w_ada, b_ada.reshape(1, -1))


def _ffn_pre(x1, mod_rows, gffn, w_sgu, w_sd, wr_hl):
    sh2, sc2, g2 = mod_rows
    h2 = _rms(x1) * gffn * (1.0 + sc2) + sh2
    h2b = h2.astype(BF16)
    gu = _dot(h2b, w_sgu)
    act = _silu(gu[:, :SH_FF]) * gu[:, SH_FF:]
    xmid = x1 + g2 * _dot(act.astype(BF16), w_sd)
    h2lo = (h2 - h2b.astype(F32)).astype(BF16)
    both = _dot_nt(wr_hl, h2b)
    lgt = both[:N_EXP] + both[N_EXP:] + _dot_nt(wr_hl[:N_EXP], h2lo)
    return xmid, h2b, lgt


def _mix_kernel(*refs, nt, n_tiles):
    i = pl.program_id(0)
    h2_ref, lgt_ref = refs[16], refs[17]

    @pl.when(i == n_tiles)
    def _():
        h2_ref[...] = jnp.zeros_like(h2_ref)
        lgt_ref[...] = jnp.zeros_like(lgt_ref)

    @pl.when(i < n_tiles)
    def _():
        _mix_tile(i % nt, nt, *refs)


def _mix_tile(t, nt, x_ref, mod_ref, gmix_ref, gffn_ref, w_in_ref, lbl_ref, hg_ref, cw_ref, cb_ref,
              w_oh_ref, w_oc_ref, w_o_ref, wr_hl_ref, w_sgu_ref, w_sd_ref,
              xmid_ref, h2_ref, lgt_ref, s_out_ref, cv_out_ref,
              proj_ref, st_ref, cbuf_ref, ya_ref):
    tt = x_ref.shape[1]

    @pl.when(t == 0)
    def _():
        st_ref[...] = jnp.zeros_like(st_ref)
        cbuf_ref[...] = jnp.zeros_like(cbuf_ref)

    x = x_ref[0]
    mod = mod_ref[0]
    sh1, sc1, g1 = mod[0:1], mod[1:2], mod[2:3]
    h = _rms(x) * gmix_ref[...] * (1.0 + sc1) + sh1
    hb = h.astype(BF16)
    for c in range(0, IN_W, 512):
        proj_ref[:, c:c + 512] = _dot(hb, w_in_ref[:, c:c + 512])

    lb = _lower_bound(lbl_ref[...])
    row = lax.broadcasted_iota(I32, (SUB, SUB), 0)
    col = lax.broadcasted_iota(I32, (SUB, SUB), 1)
    tri = (col <= row).astype(BF16)
    mask_d = (col <= row) & (row // CHUNK == col // CHUNK)
    mask_a = row // (2 * CHUNK) == col // (2 * CHUNK)
    n_ch = SUB // CHUNK

    def by_chunk(vals):
        return jnp.concatenate([jnp.zeros((CHUNK, DK), F32) if v is None
                                else jnp.broadcast_to(v, (CHUNK, DK)) for v in vals], axis=0)

    for s in range(tt // SUB):
        r0 = s * SUB
        f = lb + (1.0 - lb) * _sigmoid(proj_ref[r0:r0 + SUB, C_F:C_F + KEY_W])
        kk = 1.0 - f
        hi, mid, lo = _split3(jnp.log(f))
        bc = _dot(tri, hi) + _dot(tri, mid) + _dot(tri, lo)
        for hd in range(HEADS):
            hs = slice(hd * DK, (hd + 1) * DK)
            bh = bc[:, hs]
            at = lambda r: bh[r:r + 1]
            mids = [at(c * CHUNK + CHUNK // 2 - 1) for c in range(n_ch)]
            pair_mid = [at(CHUNK - 1), at(3 * CHUNK - 1)]
            step_mid, step_end = at(2 * CHUNK - 1), at(SUB - 1)
            arg = bh - by_chunk(mids)
            e_pos, e_neg = jnp.exp(arg), jnp.exp(-arg)
            q = _silu(proj_ref[r0:r0 + SUB, C_Q + hd * DK:C_Q + (hd + 1) * DK])
            v = proj_ref[r0:r0 + SUB, C_I + hd * DK:C_I + (hd + 1) * DK]
            qd = q * e_pos
            kd = kk[:, hs] * e_neg
            q_in = qd * by_chunk([jnp.exp(m) for m in mids])
            k_end = kd * by_chunk([jnp.exp(step_end - m) for m in mids])
            qa = qd * by_chunk([None, jnp.exp(mids[1] - pair_mid[0]), None, jnp.exp(mids[3] - pair_mid[1])])
            ka = kd * by_chunk([jnp.exp(pair_mid[0] - mids[0]), None, jnp.exp(pair_mid[1] - mids[2]), None])
            qb = qd * by_chunk([None, None, jnp.exp(mids[2] - step_mid), jnp.exp(mids[3] - step_mid)])
            kb = kd * by_chunk([jnp.exp(step_mid - mids[0]), jnp.exp(step_mid - mids[1]), None, None])
            att = jnp.where(mask_d, _dot_nt(qd.astype(BF16), kd.astype(BF16)), 0.0)
            att = att + jnp.where(mask_a, _dot_nt(qa.astype(BF16), ka.astype(BF16)), 0.0)
            att = att + _dot_nt(qb.astype(BF16), kb.astype(BF16))
            vb = v.astype(BF16)
            st = st_ref[hd]
            o = _dot(att.astype(BF16), vb) + _dot_nt(q_in.astype(BF16), st.astype(BF16))
            st_ref[hd] = st * jnp.exp(step_end) + _dot_tn(vb, k_end.astype(BF16))
            gate = _silu(proj_ref[r0:r0 + SUB, C_G + hd * DK:C_G + (hd + 1) * DK])
            ya_ref[r0:r0 + SUB, hs] = _rms(o) * hg_ref[:, hs] * gate

    u = proj_ref[:, C_CC:C_CC + CONV_W] * proj_ref[:, C_VB:C_VB + CONV_W]
    rows = lax.broadcasted_iota(I32, (tt, CONV_W), 0)
    c0, c1 = cbuf_ref[0:1], cbuf_ref[1:2]
    u1 = jnp.where(rows == 0, c1, pltpu.roll(u, 1, axis=0))
    u2 = jnp.where(rows == 0, c0, jnp.where(rows == 1, c1, pltpu.roll(u, 2, axis=0)))
    conv = cw_ref[0:1] * u2 + cw_ref[1:2] * u1 + cw_ref[2:3] * u + cb_ref[...]
    yb = proj_ref[:, C_BB:C_BB + CONV_W] * conv
    cbuf_ref[...] = u[tt - 2:tt]

    mixed = (_sigmoid(proj_ref[:, C_MGA:C_MGA + D_MODEL]) * _dot(ya_ref[...].astype(BF16), w_oh_ref[...])
             + _sigmoid(proj_ref[:, C_MGB:C_MGB + D_MODEL]) * _dot(yb.astype(BF16), w_oc_ref[...]))
    x1 = x + g1 * _dot(mixed.astype(BF16), w_o_ref[...])

    xmid, h2b, lgt = _ffn_pre(x1, (mod[3:4], mod[4:5], mod[5:6]), gffn_ref[...],
                              w_sgu_ref[...], w_sd_ref[...], wr_hl_ref[...])
    xmid_ref[0] = xmid
    _pack_rows(h2b, h2_ref)
    lgt_ref[...] = lgt

    @pl.when(t == nt - 1)
    def _():
        for hd in range(HEADS):
            s_out_ref[0, hd] = st_ref[hd].T
        cv_out_ref[0] = cbuf_ref[...]


def _const_spec(shape):
    nd = len(shape)
    return pl.BlockSpec(shape, lambda i, _nd=nd: (0,) * _nd, pipeline_mode=pl.Buffered(1))


def _mix(x, mod, n_tok, gmix, gffn, w_in, lbl, hg, cw, cb, w_oh, w_oc, w_o, wr_hl, w_sgu, w_sd):
    bsz, seq, _ = x.shape
    tt = MIX_TILE
    nt = seq // tt
    n_tiles = bsz * nt
    assert n_tiles * tt < n_tok <= (n_tiles + 1) * tt
    consts = [gmix, gffn, w_in, lbl, hg, cw, cb, w_oh, w_oc, w_o, wr_hl, w_sgu, w_sd]
    tile = lambda i: jnp.minimum(i, n_tiles - 1)
    return pl.pallas_call(
        functools.partial(_mix_kernel, nt=nt, n_tiles=n_tiles),
        grid=(n_tiles + 1,),
        in_specs=[pl.BlockSpec((1, tt, D_MODEL), lambda i: (tile(i) // nt, tile(i) % nt, 0)),
                  pl.BlockSpec((1, 6, D_MODEL), lambda i: (tile(i) // nt, 0, 0))]
                 + [_const_spec(a.shape) for a in consts],
        out_specs=[pl.BlockSpec((1, tt, D_MODEL), lambda i: (tile(i) // nt, tile(i) % nt, 0)),
                   pl.BlockSpec((2, tt, PLANE_W), lambda i: (0, i, 0)),
                   pl.BlockSpec((N_EXP, tt), lambda i: (0, i)),
                   pl.BlockSpec((1, HEADS, DK, DK), lambda i: (tile(i) // nt, 0, 0, 0)),
                   pl.BlockSpec((1, CONV_K - 1, CONV_W), lambda i: (tile(i) // nt, 0, 0))],
        out_shape=[jax.ShapeDtypeStruct((bsz, seq, D_MODEL), F32),
                   jax.ShapeDtypeStruct((2, n_tok, PLANE_W), I32),
                   jax.ShapeDtypeStruct((N_EXP, n_tok), F32),
                   jax.ShapeDtypeStruct((bsz, HEADS, DK, DK), F32),
                   jax.ShapeDtypeStruct((bsz, CONV_K - 1, CONV_W), F32)],
        scratch_shapes=[pltpu.VMEM((tt, IN_W), F32),
                        pltpu.VMEM((HEADS, DK, DK), F32),
                        pltpu.VMEM((CONV_K - 1, CONV_W), F32),
                        pltpu.VMEM((tt, KEY_W), F32)],
        compiler_params=pltpu.CompilerParams(
            dimension_semantics=("arbitrary",), vmem_limit_bytes=VMEM_LIMIT),
        name="mix",
    )(x, mod, *consts)


def _smp1_kernel(x_ref, mod_ref, gmix_ref, w_in_ref, lbl_ref, cw_ref, cb_ref, cst_ref,
                 f_ref, k_ref, q_ref, v_ref, gate_ref, yb_ref, sga_ref, sgb_ref, cv_out_ref):
    x = x_ref[...]
    sh1, sc1 = mod_ref[:, 0:D_MODEL], mod_ref[:, D_MODEL:2 * D_MODEL]
    h = _rms(x) * gmix_ref[...] * (1.0 + sc1) + sh1
    hb = h.astype(BF16)

    def proj(c, w):
        return _dot(hb, w_in_ref[:, c:c + w])

    lb = _lower_bound(lbl_ref[...])
    f = lb + (1.0 - lb) * _sigmoid(proj(C_F, KEY_W))
    f_ref[...] = f
    k_ref[...] = 1.0 - f
    q_ref[...] = _silu(proj(C_Q, KEY_W))
    v_ref[...] = proj(C_I, KEY_W)
    gate_ref[...] = _silu(proj(C_G, KEY_W))
    u = proj(C_CC, CONV_W) * proj(C_VB, CONV_W)
    c0, c1 = cst_ref[:, 0:CONV_W], cst_ref[:, CONV_W:2 * CONV_W]
    conv = cw_ref[0:1] * c0 + cw_ref[1:2] * c1 + cw_ref[2:3] * u + cb_ref[...]
    yb_ref[...] = proj(C_BB, CONV_W) * conv
    cv_out_ref[:, 0:CONV_W] = c1
    cv_out_ref[:, CONV_W:2 * CONV_W] = u
    sga_ref[...] = _sigmoid(proj(C_MGA, D_MODEL))
    sgb_ref[...] = _sigmoid(proj(C_MGB, D_MODEL))


def _smp1(x, mod, gmix, w_in, lbl, cw, cb, cst):
    n = x.shape[0]
    kw = jax.ShapeDtypeStruct((n, KEY_W), F32)
    dm = jax.ShapeDtypeStruct((n, D_MODEL), F32)
    return pl.pallas_call(
        _smp1_kernel,
        out_shape=[kw, kw, kw, kw, kw, kw, dm, dm,
                   jax.ShapeDtypeStruct((n, (CONV_K - 1) * CONV_W), F32)],
        compiler_params=pltpu.CompilerParams(vmem_limit_bytes=VMEM_LIMIT),
        name="smp1",
    )(x, mod, gmix, w_in, lbl, cw, cb, cst)


def _smp2(f, k, q, v, state):
    n = f.shape[0]
    info = plsc.get_sparse_core_info()
    lanes = info.num_lanes
    n_workers = info.num_cores * info.num_subcores
    tok_per = n // n_workers
    n_chunks = DK // lanes
    vec = lambda: pltpu.VMEM((DK,), F32)

    @pl.kernel(out_type=[jax.ShapeDtypeStruct(state.shape, F32), jax.ShapeDtypeStruct((n, KEY_W), F32)],
               mesh=_sc_mesh(), scratch_types=[pltpu.VMEM((DK, DK), F32), vec(), vec(), vec(), vec(), vec()],
               compiler_params=pltpu.CompilerParams(needs_layout_passes=False), name="smp2")
    def run(f_hbm, k_hbm, q_hbm, v_hbm, s_hbm, s_out_hbm, o_hbm, s_v, f_v, k_v, q_v, v_v, o_v):
        wid = lax.axis_index("subcore") * info.num_cores + lax.axis_index("core")

        def tile(j, carry):
            t = wid * tok_per + j // HEADS
            hcols = pl.ds((j % HEADS) * DK, DK)
            pltpu.sync_copy(s_hbm.at[t, j % HEADS], s_v)
            for src, dst in ((f_hbm, f_v), (k_hbm, k_v), (q_hbm, q_v), (v_hbm, v_v)):
                pltpu.sync_copy(src.at[t, hcols], dst)
            v_chunks = [v_v[pl.ds(c * lanes, lanes)] for c in range(n_chunks)]

            def row(d, acc):
                at = [jnp.zeros((lanes,), I32) + d]
                fd, kd, qd = (plsc.load_gather(r, at) for r in (f_v, k_v, q_v))
                out = []
                for c in range(n_chunks):
                    cols = pl.ds(c * lanes, lanes)
                    new = s_v[d, cols] * fd + kd * v_chunks[c]
                    s_v[d, cols] = new
                    out.append(acc[c] + qd * new)
                return tuple(out)

            acc = lax.fori_loop(0, DK, row, tuple(jnp.zeros((lanes,), F32) for _ in range(n_chunks)))
            for c in range(n_chunks):
                o_v[pl.ds(c * lanes, lanes)] = acc[c]
            pltpu.sync_copy(s_v, s_out_hbm.at[t, j % HEADS])
            pltpu.sync_copy(o_v, o_hbm.at[t, hcols])
            return carry

        lax.fori_loop(0, tok_per * HEADS, tile, 0)

    return run(f, k, q, v, state)


def _smp3_kernel(x_ref, mod_ref, o_ref, gate_ref, yb_ref, sga_ref, sgb_ref, hg_ref, gffn_ref,
                 w_oh_ref, w_oc_ref, w_o_ref, wr_hl_ref, w_sgu_ref, w_sd_ref,
                 h2_all_ref, lgt_all_ref, xmid_ref, h2_ref, lgt_ref):
    del h2_all_ref, lgt_all_ref
    parts = []
    for hd in range(HEADS):
        hs = slice(hd * DK, (hd + 1) * DK)
        parts.append(_rms(o_ref[:, hs]) * hg_ref[:, hs] * gate_ref[:, hs])
    ya = jnp.concatenate(parts, axis=1)
    mixed = (sga_ref[...] * _dot(ya.astype(BF16), w_oh_ref[...])
             + sgb_ref[...] * _dot(yb_ref[...].astype(BF16), w_oc_ref[...]))
    g1 = mod_ref[:, 2 * D_MODEL:3 * D_MODEL]
    x1 = x_ref[...] + g1 * _dot(mixed.astype(BF16), w_o_ref[...])
    mod_rows = tuple(mod_ref[:, j * D_MODEL:(j + 1) * D_MODEL] for j in (3, 4, 5))
    xmid, h2b, lgt = _ffn_pre(x1, mod_rows, gffn_ref[...], w_sgu_ref[...], w_sd_ref[...],
                              wr_hl_ref[...])
    xmid_ref[...] = xmid
    _pack_rows(h2b, h2_ref)
    lgt_ref[...] = lgt


def _smp3(x, mod, o, gate, yb, sga, sgb, hg, gffn, w_oh, w_oc, w_o, wr_hl, w_sgu, w_sd,
          h2_all, lgt_all, n_prompt):
    n = x.shape[0]
    vmem_args = [x, mod, o, gate, yb, sga, sgb, hg, gffn, w_oh, w_oc, w_o, wr_hl, w_sgu, w_sd]
    blk = n_prompt // n

    def full(a):
        nd = a.ndim
        return pl.BlockSpec(a.shape, lambda i, _nd=nd: (0,) * _nd)

    return pl.pallas_call(
        _smp3_kernel,
        grid=(1,),
        in_specs=[full(a) for a in vmem_args]
                 + [pl.BlockSpec(memory_space=pl.ANY), pl.BlockSpec(memory_space=pl.ANY)],
        out_specs=[pl.BlockSpec((n, D_MODEL), lambda i: (0, 0)),
                   pl.BlockSpec((2, n, PLANE_W), lambda i: (0, blk, 0)),
                   pl.BlockSpec((N_EXP, n), lambda i: (0, blk))],
        out_shape=[jax.ShapeDtypeStruct((n, D_MODEL), F32),
                   jax.ShapeDtypeStruct(h2_all.shape, h2_all.dtype),
                   jax.ShapeDtypeStruct(lgt_all.shape, lgt_all.dtype)],
        input_output_aliases={len(vmem_args): 1, len(vmem_args) + 1: 2},
        compiler_params=pltpu.CompilerParams(
            dimension_semantics=("arbitrary",), vmem_limit_bytes=VMEM_LIMIT),
        name="smp3",
    )(*vmem_args, h2_all, lgt_all)


def _route_kernel(lgt_ref, bias_ref, idx_ref, w_ref, rank_ref, cnt_ref):
    tr = ROUTE_TILE
    n_tiles = lgt_ref.shape[1] // tr

    def tile(i, carry):
        cols = pl.ds(pl.multiple_of(i * tr, tr), tr)
        picks, weights, ranks, carry = _route_tile(lgt_ref[:, cols], bias_ref[...], carry)
        for k in range(TOP_K):
            idx_ref[k:k + 1, cols] = picks[k]
            w_ref[k:k + 1, cols] = weights[k]
            rank_ref[k:k + 1, cols] = ranks[k]
        return carry

    total = lax.fori_loop(0, n_tiles, tile, jnp.zeros((N_EXP, 1), F32))
    cnt_ref[...] = jnp.broadcast_to(total, cnt_ref.shape).astype(I32)


def _route_tile(lgt, bias, carry):
    tr = lgt.shape[1]
    neg = -jnp.inf
    scores = _sigmoid(lgt)
    sel = scores + bias
    j8 = lax.broadcasted_iota(I32, (GRP_SZ, tr), 0)
    groups = [sel[g * GRP_SZ:(g + 1) * GRP_SZ] for g in range(N_GRP)]
    gscore = []
    for grp in groups:
        m1 = jnp.max(grp, axis=0, keepdims=True)
        i1 = jnp.min(jnp.where(grp == m1, j8, GRP_SZ), axis=0, keepdims=True)
        m2 = jnp.max(jnp.where(j8 == i1, neg, grp), axis=0, keepdims=True)
        gscore.append(m1 + m2)
    kept = []
    for g in range(N_GRP):
        beaten = jnp.zeros((1, tr), I32)
        for o in range(N_GRP):
            if o < g:
                beaten = beaten + (gscore[o] >= gscore[g]).astype(I32)
            elif o > g:
                beaten = beaten + (gscore[o] > gscore[g]).astype(I32)
        kept.append(jnp.where(beaten < TOPK_GRP, groups[g], neg))
    masked = jnp.concatenate(kept, axis=0)
    ei = lax.broadcasted_iota(I32, masked.shape, 0)
    chosen = jnp.zeros(masked.shape, jnp.bool_)
    picks, weights = [], []
    for _ in range(TOP_K):
        m = jnp.max(masked, axis=0, keepdims=True)
        pick = jnp.min(jnp.where((masked == m) & ~chosen, ei, N_EXP), axis=0, keepdims=True)
        hit = ei == pick
        weights.append(jnp.sum(jnp.where(hit, scores, 0.0), axis=0, keepdims=True))
        picks.append(pick)
        chosen = chosen | hit
        masked = jnp.where(hit, neg, masked)
    wsum = weights[0]
    for w in weights[1:]:
        wsum = wsum + w
    sel01 = chosen.astype(F32)
    r = lax.broadcasted_iota(I32, (tr, tr), 0)
    c = lax.broadcasted_iota(I32, (tr, tr), 1)
    before = (r < c).astype(BF16)
    cnt = _dot(sel01.astype(BF16), before) + carry
    weights = [w / wsum * ROUTED_SCALE for w in weights]
    ranks = [jnp.sum(jnp.where(ei == p, cnt, 0.0), axis=0, keepdims=True).astype(I32) for p in picks]
    return picks, weights, ranks, carry + jnp.sum(sel01, axis=1, keepdims=True)


def _route(lgt, bias):
    n = lgt.shape[1]
    assert n % ROUTE_TILE == 0
    slot = lambda dt: jax.ShapeDtypeStruct((TOP_K, n), dt)
    return pl.pallas_call(
        _route_kernel,
        out_shape=[slot(I32), slot(F32), slot(I32),
                   jax.ShapeDtypeStruct((N_EXP, 128), I32)],
        name="route",
    )(lgt, bias)


def _dest_kernel(start_ref, idx_ref, rank_ref, all_ref, *chunk_refs, n_rows, chunks):
    n_tok = idx_ref.shape[1]
    idx = idx_ref[...]
    acc = rank_ref[...]
    for e in range(N_EXP):
        acc = acc + jnp.where(idx == e, start_ref[e], 0)
    for k in range(TOP_K):
        for p in range(2):
            row = acc[k:k + 1, :] + p * n_rows
            seg = 2 * k + p
            all_ref[:, seg * n_tok:(seg + 1) * n_tok] = row
            for (c0, c1), ref in zip(chunks, chunk_refs):
                ref[:, seg * (c1 - c0):(seg + 1) * (c1 - c0)] = row[:, c0:c1]


def _dest(pad_start, idx, rank, n_rows, chunks):
    k, n_tok = idx.shape
    vmem = pl.BlockSpec(memory_space=pltpu.VMEM)
    sizes = [n_tok] + [c1 - c0 for c0, c1 in chunks]
    return pl.pallas_call(
        functools.partial(_dest_kernel, n_rows=n_rows, chunks=chunks),
        in_specs=[pl.BlockSpec(memory_space=pltpu.SMEM), vmem, vmem],
        out_specs=[vmem] * len(sizes),
        out_shape=[jax.ShapeDtypeStruct((1, 2 * k * n), I32) for n in sizes],
        name="dest",
    )(pad_start, idx, rank)


def _sc_mesh():
    return plsc.VectorSubcoreMesh(core_axis_name="core", subcore_axis_name="subcore")


def _dispatch(rows, dest, n_out):
    n, width = rows.shape
    win = SC_WINDOW
    steps = n // win

    @pl.kernel(out_type=jax.ShapeDtypeStruct((n_out, width), rows.dtype), mesh=_sc_mesh(),
               scratch_types=[pltpu.SemaphoreType.DMA], name="dispatch")
    def run(x_hbm, *refs):
        i_hbms, o_hbm, sem = refs[:TOP_K], refs[TOP_K], refs[TOP_K + 1]

        def body(x_vmem, *i_vmems):
            copies = [pltpu.async_copy(x_vmem, o_hbm.at[i_vmem.at[0]], sem) for i_vmem in i_vmems]
            for c in copies:
                c.wait()

        pltpu.emit_pipeline(
            body,
            grid=(steps,),
            in_specs=[pl.BlockSpec((win, width), lambda i: (i, 0))]
                     + [pl.BlockSpec((1, win), lambda i, k=k: (0, k * steps + i)) for k in range(TOP_K)],
            out_specs=[],
            core_axis_name=("core", "subcore"),
            dimension_semantics=(pltpu.PARALLEL,),
        )(x_hbm, *i_hbms)

    assert dest.shape == (1, TOP_K * n)
    return run(rows, *([dest] * TOP_K))


def _combine(rows, dest_flat):
    width = rows.shape[1]
    n = dest_flat.shape[1]
    win = SC_WINDOW

    @pl.kernel(out_type=jax.ShapeDtypeStruct((n, width), rows.dtype), mesh=_sc_mesh(),
               scratch_types=[], name="combine")
    def run(y_hbm, i_hbm, o_hbm):
        def body(i_vmem, o_vmem):
            pltpu.sync_copy(y_hbm.at[i_vmem.at[0]], o_vmem)

        pltpu.emit_pipeline(
            body,
            grid=(n // win,),
            in_specs=[pl.BlockSpec((1, win), lambda i: (0, i))],
            out_specs=[pl.BlockSpec((win, width), lambda i: (i, 0))],
            core_axis_name=("core", "subcore"),
            dimension_semantics=(pltpu.PARALLEL,),
        )(i_hbm, o_hbm)

    return run(rows, dest_flat)


def _gmm_kernel(blk_exp_ref, n_used_ref, xs_hbm, wg_hbm, wu_hbm, wd_hbm, ys_hbm,
                xbuf, ybuf, wg32, wu32, wd32, wgu_b, wd_b, xsem, ysem, wsem, run_ref):
    nx, ny, bm = xbuf.shape[0], ybuf.shape[0], xbuf.shape[2]
    n_used = n_used_ref[0]

    def x_copies(b):
        rows, slot = pl.ds(b * bm, bm), b % nx
        return [pltpu.make_async_copy(xs_hbm.at[p, rows, :], xbuf.at[slot, p], xsem.at[slot, p]) for p in range(2)]

    def y_copies(b):
        rows, slot = pl.ds(b * bm, bm), b % ny
        return [pltpu.make_async_copy(ybuf.at[slot, p], ys_hbm.at[p, rows, :], ysem.at[slot, p]) for p in range(2)]

    def start(copies):
        for c in copies:
            c.start()

    def wait(copies):
        for c in copies:
            c.wait()

    def w_copies(e, slot):
        return (pltpu.make_async_copy(wg_hbm.at[e], wg32.at[slot], wsem.at[slot, 0]),
                pltpu.make_async_copy(wu_hbm.at[e], wu32.at[slot], wsem.at[slot, 1]),
                pltpu.make_async_copy(wd_hbm.at[e], wd32.at[slot], wsem.at[slot, 2]))

    def run_end(b):
        return lax.while_loop(lambda j: (j < n_used) & (blk_exp_ref[jnp.minimum(j, n_used - 1)] == blk_exp_ref[b]),
                              lambda j: j + 1, b + 1)

    run_ref[0] = 0
    start(x_copies(0))
    start(w_copies(blk_exp_ref[0], 0))
    for j in range(1, nx - 1):
        @pl.when(j < n_used)
        def _():
            start(x_copies(j))

    def block(b, carry):
        @pl.when(b + nx - 1 < n_used)
        def _():
            start(x_copies(b + nx - 1))

        @pl.when((b == 0) | (blk_exp_ref[b] != blk_exp_ref[jnp.maximum(b - 1, 0)]))
        def _():
            wslot = run_ref[0] % 2
            run_ref[0] = run_ref[0] + 1
            wait(w_copies(blk_exp_ref[b], wslot))
            wgu_b[:, 0:EXP_FF] = wg32[wslot].astype(BF16)
            wgu_b[:, EXP_FF:2 * EXP_FF] = wu32[wslot].astype(BF16)
            wd_b[...] = wd32[wslot].astype(BF16)
            nxt = run_end(b)

            @pl.when(nxt < n_used)
            def _():
                start(w_copies(blk_exp_ref[jnp.minimum(nxt, n_used - 1)], 1 - wslot))

        wait(x_copies(b))

        @pl.when(b >= ny)
        def _():
            wait(y_copies(b - ny))

        xslot = b % nx
        xc = _unpack_rows(xbuf[xslot, 0], xbuf[xslot, 1])
        gu = sum(_dot(c, wgu_b[i * PLANE_W:(i + 1) * PLANE_W, :]) for i, c in enumerate(xc))
        act = (_silu(gu[:, :EXP_FF]) * gu[:, EXP_FF:]).astype(BF16)
        _pack_rows(_dot(act, wd_b[...]).astype(BF16), ybuf.at[b % ny])
        start(y_copies(b))
        return carry

    lax.fori_loop(0, n_used, block, 0)

    for j in range(ny, 0, -1):
        @pl.when(n_used >= j)
        def _():
            wait(y_copies(n_used - j))


def _gmm(blk_exp, n_used, xs, w_gate, w_up, w_down):
    n_rows = xs.shape[1]
    bm = GMM_BM
    nb = n_rows // bm

    assert blk_exp.shape == (nb,)
    any_spec = pl.BlockSpec(memory_space=pl.ANY)
    grid_spec = pltpu.PrefetchScalarGridSpec(
        num_scalar_prefetch=2,
        grid=(1,),
        in_specs=[any_spec, any_spec, any_spec, any_spec],
        out_specs=any_spec,
        scratch_shapes=[pltpu.VMEM((GMM_NX, 2, bm, PLANE_W), I32), pltpu.VMEM((GMM_NY, 2, bm, PLANE_W), I32),
                        pltpu.VMEM((2, D_MODEL, EXP_FF), F32), pltpu.VMEM((2, D_MODEL, EXP_FF), F32),
                        pltpu.VMEM((2, EXP_FF, D_MODEL), F32),
                        pltpu.VMEM((D_MODEL, 2 * EXP_FF), BF16), pltpu.VMEM((EXP_FF, D_MODEL), BF16),
                        pltpu.SemaphoreType.DMA((GMM_NX, 2)), pltpu.SemaphoreType.DMA((GMM_NY, 2)),
                        pltpu.SemaphoreType.DMA((2, 3)), pltpu.SMEM((1,), I32)],
    )
    return pl.pallas_call(
        _gmm_kernel,
        grid_spec=grid_spec,
        out_shape=jax.ShapeDtypeStruct((2, n_rows, PLANE_W), I32),
        compiler_params=pltpu.CompilerParams(dimension_semantics=("arbitrary",)),
        name="gmm",
    )(blk_exp, n_used, xs, w_gate, w_up, w_down)


def _final_kernel(xmid_ref, g2_ref, z_ref, w_ref, gfin_ref, *rest):
    y_ref = rest[-1]
    accs = [jnp.zeros((xmid_ref.shape[0], PLANE_W), F32) for _ in range(4)]
    w_cols = w_ref[...].T
    for k in range(TOP_K):
        wk = w_cols[:, k:k + 1]
        cols = _unpack_rows(z_ref[k, 0], z_ref[k, 1])
        accs = [a + wk * c.astype(F32) for a, c in zip(accs, cols)]
    acc = jnp.concatenate(accs, axis=1)
    y_ref[...] = _rms(xmid_ref[...] + g2_ref[0] * acc) * gfin_ref[...]


def _final(xmid, g2, z, w_t, gfin, tile, *, n_tiles, x_tile0, z_tile0, w_tile0, tiles_per_g2, y_prev=None):
    args = [xmid, g2, z, w_t, gfin]
    in_specs = [pl.BlockSpec((tile, D_MODEL), lambda i: (x_tile0 + i, 0)),
                pl.BlockSpec((1, g2.shape[1], D_MODEL), lambda i: ((x_tile0 + i) // tiles_per_g2, 0, 0)),
                pl.BlockSpec((TOP_K, 2, tile, PLANE_W), lambda i: (0, 0, z_tile0 + i, 0)),
                pl.BlockSpec((TOP_K, tile), lambda i: (0, w_tile0 + i)),
                pl.BlockSpec((1, D_MODEL), lambda i: (0, 0))]
    aliases = {}
    if y_prev is not None:
        args.append(y_prev)
        in_specs.append(pl.BlockSpec(memory_space=pl.ANY))
        aliases = {len(args) - 1: 0}
    return pl.pallas_call(
        _final_kernel,
        grid=(n_tiles,),
        in_specs=in_specs,
        out_specs=pl.BlockSpec((tile, D_MODEL), lambda i: (x_tile0 + i, 0)),
        out_shape=jax.ShapeDtypeStruct(xmid.shape, F32),
        input_output_aliases=aliases,
        compiler_params=pltpu.CompilerParams(dimension_semantics=("arbitrary",)),
        name="final",
    )(*args)


def kernel(x_prompt, x_sample, state_hgrn, state_conv, c_prompt, c_sample, w_ada, b_ada, norm_mix_g, norm_ffn_g, w_in, lb_logits, hgrn_norm_g, conv_w, conv_b, w_out_hgrn, w_out_conv, w_o, w_router, router_bias, w_exp_gate, w_exp_up, w_exp_down, w_sh_gate, w_sh_up, w_sh_down, final_norm_g):
    assert w_ada.shape[0] == 1 and lb_logits.shape[0] == 2
    bsz, seq, _ = x_prompt.shape
    n_smp = x_sample.shape[0]
    n_prompt = bsz * seq
    n_tok = n_prompt + n_smp

    w_in_b = w_in[0].astype(BF16)
    w_oh_b = w_out_hgrn[0].astype(BF16)
    w_oc_b = w_out_conv[0].astype(BF16)
    w_o_b = w_o[0].astype(BF16)
    wr_t = w_router[0].T
    wr_hi = wr_t.astype(BF16)
    wr_hl = jnp.concatenate([wr_hi, (wr_t - wr_hi.astype(F32)).astype(BF16)], axis=0)
    w_sgu = jnp.concatenate([w_sh_gate[0], w_sh_up[0]], axis=1).astype(BF16)
    w_sd = w_sh_down[0].astype(BF16)
    gmix = norm_mix_g[0].reshape(1, D_MODEL)
    gffn = norm_ffn_g[0].reshape(1, D_MODEL)
    hg = hgrn_norm_g[0].reshape(1, KEY_W)
    cw = conv_w[0]
    cb = conv_b[0].reshape(1, CONV_W)
    gfin = final_norm_g.reshape(1, D_MODEL)

    mod_p, mod_s = _ada(c_prompt, c_sample, w_ada[0], b_ada[0])

    xmid_p, h2_all, lgt_all, s_p, cv_p = _mix(
        x_prompt, mod_p.reshape(bsz, 6, D_MODEL), n_tok, gmix, gffn, w_in_b, lb_logits, hg, cw, cb, w_oh_b, w_oc_b, w_o_b, wr_hl, w_sgu, w_sd)

    xs2 = x_sample.reshape(n_smp, D_MODEL)
    f, kk, q, v, gate, yb, sga, sgb, cv_s = _smp1(
        xs2, mod_s, gmix, w_in_b, lb_logits, cw, cb, state_conv[0].reshape(n_smp, (CONV_K - 1) * CONV_W))
    s_s, o_s = _smp2(f, kk, q, v, state_hgrn[0])
    xmid_s, h2_all, lgt_all = _smp3(xs2, mod_s, o_s, gate, yb, sga, sgb, hg, gffn,
                                    w_oh_b, w_oc_b, w_o_b, wr_hl, w_sgu, w_sd,
                                    h2_all, lgt_all, n_prompt)

    idx, w_tok, rank, cnt = _route(lgt_all, router_bias[0].reshape(N_EXP, 1))

    bm = GMM_BM
    n_blocks = (n_tok * TOP_K + N_EXP * (bm - 1)) // bm
    n_rows = n_blocks * bm
    counts = cnt[:, 0]
    padded = (counts + bm - 1) // bm * bm
    pad_end = jnp.cumsum(padded)
    pad_start = pad_end - padded
    blk_row0 = jnp.arange(n_blocks, dtype=I32) * bm
    blk_exp = jnp.minimum(jnp.sum((pad_end[None, :] <= blk_row0[:, None]).astype(I32), axis=1), N_EXP - 1)
    n_used = (pad_end[-1:] // bm).astype(I32)
    cuts = [0] + [n_prompt * f // FINAL_SPLIT[-1] for f in FINAL_SPLIT]
    assert all(c % FINAL_TILE == 0 for c in cuts) and cuts[-1] == n_prompt
    chunks = tuple((c0, n_tok if c1 == n_prompt else c1) for c0, c1 in zip(cuts[:-1], cuts[1:]))
    dest, *chunk_dest = _dest(pad_start.astype(I32), idx, rank, n_rows, chunks)

    xs = _dispatch(h2_all.reshape(2 * n_tok, PLANE_W), dest, 2 * n_rows).reshape(2, n_rows, PLANE_W)
    ys = _gmm(blk_exp, n_used, xs, w_exp_gate[0], w_exp_up[0], w_exp_down[0])

    ys_flat = ys.reshape(2 * n_rows, PLANE_W)
    w_t = w_tok
    xmid_p2 = xmid_p.reshape(n_prompt, D_MODEL)
    g2_p = mod_p[:, 5 * D_MODEL:].reshape(bsz, 1, D_MODEL)
    g2_s = mod_s[:, 5 * D_MODEL:].reshape(1, n_smp, D_MODEL)
    y_p = None
    for (c0, c1), dest_c in zip(chunks, chunk_dest):
        z = _combine(ys_flat, dest_c).reshape(TOP_K, 2, c1 - c0, PLANE_W)
        y_p = _final(xmid_p2, g2_p, z, w_t, gfin, FINAL_TILE, n_tiles=(min(c1, n_prompt) - c0) // FINAL_TILE,
                     x_tile0=c0 // FINAL_TILE, z_tile0=0, w_tile0=c0 // FINAL_TILE,
                     tiles_per_g2=seq // FINAL_TILE, y_prev=y_p)
    y_s = _final(xmid_s, g2_s, z, w_t, gfin, n_smp, n_tiles=1, x_tile0=0, z_tile0=(n_prompt - c0) // n_smp,
                 w_tile0=n_prompt // n_smp, tiles_per_g2=1)

    return (y_p.reshape(bsz, seq, D_MODEL), y_s.reshape(n_smp, 1, D_MODEL),
            s_p[None], cv_p[None], s_s[None], cv_s.reshape(1, n_smp, CONV_K - 1, CONV_W))
```

```python
import functools

import jax
import jax.numpy as jnp
from jax import lax
from jax.experimental import pallas as pl
from jax.experimental.pallas import tpu as pltpu
from jax.experimental.pallas import tpu_sc as plsc

F32 = jnp.float32
BF16 = jnp.bfloat16
I32 = jnp.int32

D_MODEL = 1024
HALF_D = D_MODEL // 2
HEADS = 4
DK = 128
KEY_W = HEADS * DK
CONV_W = 512
CONV_K = 3
IN_W = 2 * KEY_W + 2 * KEY_W + 3 * CONV_W + 2 * D_MODEL
N_EXP = 64
TOP_K = 8
N_GRP = 8
GRP_SZ = N_EXP // N_GRP
TOPK_GRP = 4
EXP_FF = 256
SH_FF = 256
ROUTED_SCALE = 2.5
EPS = 1e-6

C_Q, C_F, C_I, C_G = 0, 512, 1024, 1536
C_BB, C_CC, C_VB = 2048, 2560, 3072
C_MGA, C_MGB = 3584, 4608

MIX_TILE = 512
SUB = 256
CHUNK = 64
ROUTE_TILE = 384
GMM_BM = 512
GMM_NX = 4
GMM_NY = 3
FINAL_TILE = 512
FINAL_SPLIT = (4, 12, 22, 32)
SC_WINDOW = 128
PLANE_W = HALF_D // 2
VMEM_LIMIT = 56 * 1024 * 1024


def _dot(a, b):
    return jnp.dot(a, b, preferred_element_type=F32)


def _dot_nt(a, b):
    return lax.dot_general(a, b, (((1,), (1,)), ((), ())), preferred_element_type=F32)


def _dot_tn(a, b):
    return lax.dot_general(a, b, (((0,), (0,)), ((), ())), preferred_element_type=F32)


def _sigmoid(x):
    return 0.5 * jnp.tanh(0.5 * x) + 0.5


def _silu(x):
    h = 0.5 * x
    return h * jnp.tanh(h) + h


def _rms(x):
    return x * lax.rsqrt(jnp.mean(x * x, axis=-1, keepdims=True) + EPS)


def _lower_bound(lbl):
    a, b = lbl[0:1], lbl[1:2]
    m = jnp.maximum(a, b)
    ea, eb = jnp.exp(a - m), jnp.exp(b - m)
    return ea / (ea + eb)


def _split3(x):
    hi = x.astype(BF16)
    r1 = x - hi.astype(F32)
    mid = r1.astype(BF16)
    lo = (r1 - mid.astype(F32)).astype(BF16)
    return hi, mid, lo


def _words(lo_b, hi_b):
    lo = lax.shift_right_logical(lax.bitcast_convert_type(lo_b.astype(F32), I32), 16)
    hi = lax.bitcast_convert_type(hi_b.astype(F32), I32) & jnp.int32(-65536)
    return lo | hi


def _halves(w):
    lo = lax.bitcast_convert_type(lax.shift_left(w, 16), F32)
    hi = lax.bitcast_convert_type(w & jnp.int32(-65536), F32)
    return lo.astype(BF16), hi.astype(BF16)


def _pack_rows(xb, out_ref):
    words = _words(xb[:, :HALF_D], xb[:, HALF_D:])
    out_ref[0] = words[:, :PLANE_W]
    out_ref[1] = words[:, PLANE_W:]


def _unpack_rows(p0, p1):
    c0, c2 = _halves(p0)
    c1, c3 = _halves(p1)
    return c0, c1, c2, c3


def _ada_kernel(cp_ref, cs_ref, w_ref, b_ref, op_ref, os_ref):
    w = w_ref[...].astype(BF16)
    for c_ref, o_ref in ((cp_ref, op_ref), (cs_ref, os_ref)):
        o_ref[...] = _dot(_silu(c_ref[...]).astype(BF16), w) + b_ref[...]


def _ada(c_prompt, c_sample, w_ada, b_ada):
    blk = 1024
    rows = lambda c: pl.BlockSpec((c.shape[0], D_MODEL), lambda j: (0, 0))
    cols = lambda c: pl.BlockSpec((c.shape[0], blk), lambda j: (0, j))
    return pl.pallas_call(
        _ada_kernel,
        grid=(6 * D_MODEL // blk,),
        in_specs=[rows(c_prompt), rows(c_sample),
                  pl.BlockSpec((D_MODEL, blk), lambda j: (0, j)),
                  pl.BlockSpec((1, blk), lambda j: (0, j))],
        out_specs=[cols(c_prompt), cols(c_sample)],
        out_shape=[jax.ShapeDtypeStruct((c.shape[0], 6 * D_MODEL), F32) for c in (c_prompt, c_sample)],
        name="ada",
    )(c_prompt, c_sample, w_ada, b_ada.reshape(1, -1))


def _ffn_pre(x1, mod_rows, gffn, w_sgu, w_sd, wr_hl):
    sh2, sc2, g2 = mod_rows
    h2 = _rms(x1) * gffn * (1.0 + sc2) + sh2
    h2b = h2.astype(BF16)
    gu = _dot(h2b, w_sgu)
    act = _silu(gu[:, :SH_FF]) * gu[:, SH_FF:]
    xmid = x1 + g2 * _dot(act.astype(BF16), w_sd)
    h2lo = (h2 - h2b.astype(F32)).astype(BF16)
    both = _dot_nt(wr_hl, h2b)
    lgt = both[:N_EXP] + both[N_EXP:] + _dot_nt(wr_hl[:N_EXP], h2lo)
    return xmid, h2b, lgt


def _mix_kernel(*refs, nt, n_tiles):
    i = pl.program_id(0)
    h2_ref, lgt_ref = refs[16], refs[17]

    @pl.when(i == n_tiles)
    def _():
        h2_ref[...] = jnp.zeros_like(h2_ref)
        lgt_ref[...] = jnp.zeros_like(lgt_ref)

    @pl.when(i < n_tiles)
    def _():
        _mix_tile(i % nt, nt, *refs)


def _mix_tile(t, nt, x_ref, mod_ref, gmix_ref, gffn_ref, w_in_ref, lbl_ref, hg_ref, cw_ref, cb_ref,
              w_oh_ref, w_oc_ref, w_o_ref, wr_hl_ref, w_sgu_ref, w_sd_ref,
              xmid_ref, h2_ref, lgt_ref, s_out_ref, cv_out_ref,
              proj_ref, st_ref, cbuf_ref, ya_ref):
    tt = x_ref.shape[1]

    @pl.when(t == 0)
    def _():
        st_ref[...] = jnp.zeros_like(st_ref)
        cbuf_ref[...] = jnp.zeros_like(cbuf_ref)

    x = x_ref[0]
    mod = mod_ref[0]
    sh1, sc1, g1 = mod[0:1], mod[1:2], mod[2:3]
    h = _rms(x) * gmix_ref[...] * (1.0 + sc1) + sh1
    hb = h.astype(BF16)
    for c in range(0, IN_W, 512):
        proj_ref[:, c:c + 512] = _dot(hb, w_in_ref[:, c:c + 512])

    lb = _lower_bound(lbl_ref[...])
    row = lax.broadcasted_iota(I32, (SUB, SUB), 0)
    col = lax.broadcasted_iota(I32, (SUB, SUB), 1)
    tri = (col <= row).astype(BF16)
    mask_d = (col <= row) & (row // CHUNK == col // CHUNK)
    mask_a = row // (2 * CHUNK) == col // (2 * CHUNK)
    n_ch = SUB // CHUNK

    def by_chunk(vals):
        return jnp.concatenate([jnp.zeros((CHUNK, DK), F32) if v is None
                                else jnp.broadcast_to(v, (CHUNK, DK)) for v in vals], axis=0)

    for s in range(tt // SUB):
        r0 = s * SUB
        f = lb + (1.0 - lb) * _sigmoid(proj_ref[r0:r0 + SUB, C_F:C_F + KEY_W])
        kk = 1.0 - f
        hi, mid, lo = _split3(jnp.log(f))
        bc = _dot(tri, hi) + _dot(tri, mid) + _dot(tri, lo)
        for hd in range(HEADS):
            hs = slice(hd * DK, (hd + 1) * DK)
            bh = bc[:, hs]
            at = lambda r: bh[r:r + 1]
            mids = [at(c * CHUNK + CHUNK // 2 - 1) for c in range(n_ch)]
            pair_mid = [at(CHUNK - 1), at(3 * CHUNK - 1)]
            step_mid, step_end = at(2 * CHUNK - 1), at(SUB - 1)
            arg = bh - by_chunk(mids)
            e_pos, e_neg = jnp.exp(arg), jnp.exp(-arg)
            q = _silu(proj_ref[r0:r0 + SUB, C_Q + hd * DK:C_Q + (hd + 1) * DK])
            v = proj_ref[r0:r0 + SUB, C_I + hd * DK:C_I + (hd + 1) * DK]
            qd = q * e_pos
            kd = kk[:, hs] * e_neg
            q_in = qd * by_chunk([jnp.exp(m) for m in mids])
            k_end = kd * by_chunk([jnp.exp(step_end - m) for m in mids])
            qa = qd * by_chunk([None, jnp.exp(mids[1] - pair_mid[0]), None, jnp.exp(mids[3] - pair_mid[1])])
            ka = kd * by_chunk([jnp.exp(pair_mid[0] - mids[0]), None, jnp.exp(pair_mid[1] - mids[2]), None])
            qb = qd * by_chunk([None, None, jnp.exp(mids[2] - step_mid), jnp.exp(mids[3] - step_mid)])
            kb = kd * by_chunk([jnp.exp(step_mid - mids[0]), jnp.exp(step_mid - mids[1]), None, None])
            att = jnp.where(mask_d, _dot_nt(qd.astype(BF16), kd.astype(BF16)), 0.0)
            att = att + jnp.where(mask_a, _dot_nt(qa.astype(BF16), ka.astype(BF16)), 0.0)
            att = att + _dot_nt(qb.astype(BF16), kb.astype(BF16))
            vb = v.astype(BF16)
            st = st_ref[hd]
            o = _dot(att.astype(BF16), vb) + _dot_nt(q_in.astype(BF16), st.astype(BF16))
            st_ref[hd] = st * jnp.exp(step_end) + _dot_tn(vb, k_end.astype(BF16))
            gate = _silu(proj_ref[r0:r0 + SUB, C_G + hd * DK:C_G + (hd + 1) * DK])
            ya_ref[r0:r0 + SUB, hs] = _rms(o) * hg_ref[:, hs] * gate

    u = proj_ref[:, C_CC:C_CC + CONV_W] * proj_ref[:, C_VB:C_VB + CONV_W]
    rows = lax.broadcasted_iota(I32, (tt, CONV_W), 0)
    c0, c1 = cbuf_ref[0:1], cbuf_ref[1:2]
    u1 = jnp.where(rows == 0, c1, pltpu.roll(u, 1, axis=0))
    u2 = jnp.where(rows == 0, c0, jnp.where(rows == 1, c1, pltpu.roll(u, 2, axis=0)))
    conv = cw_ref[0:1] * u2 + cw_ref[1:2] * u1 + cw_ref[2:3] * u + cb_ref[...]
    yb = proj_ref[:, C_BB:C_BB + CONV_W] * conv
    cbuf_ref[...] = u[tt - 2:tt]

    mixed = (_sigmoid(proj_ref[:, C_MGA:C_MGA + D_MODEL]) * _dot(ya_ref[...].astype(BF16), w_oh_ref[...])
             + _sigmoid(proj_ref[:, C_MGB:C_MGB + D_MODEL]) * _dot(yb.astype(BF16), w_oc_ref[...]))
    x1 = x + g1 * _dot(mixed.astype(BF16), w_o_ref[...])

    xmid, h2b, lgt = _ffn_pre(x1, (mod[3:4], mod[4:5], mod[5:6]), gffn_ref[...],
                              w_sgu_ref[...], w_sd_ref[...], wr_hl_ref[...])
    xmid_ref[0] = xmid
    _pack_rows(h2b, h2_ref)
    lgt_ref[...] = lgt

    @pl.when(t == nt - 1)
    def _():
        for hd in range(HEADS):
            s_out_ref[0, hd] = st_ref[hd].T
        cv_out_ref[0] = cbuf_ref[...]


def _const_spec(shape):
    nd = len(shape)
    return pl.BlockSpec(shape, lambda i, _nd=nd: (0,) * _nd, pipeline_mode=pl.Buffered(1))


def _mix(x, mod, n_tok, gmix, gffn, w_in, lbl, hg, cw, cb, w_oh, w_oc, w_o, wr_hl, w_sgu, w_sd):
    bsz, seq, _ = x.shape
    tt = MIX_TILE
    nt = seq // tt
    n_tiles = bsz * nt
    assert n_tiles * tt < n_tok <= (n_tiles + 1) * tt
    consts = [gmix, gffn, w_in, lbl, hg, cw, cb, w_oh, w_oc, w_o, wr_hl, w_sgu, w_sd]
    tile = lambda i: jnp.minimum(i, n_tiles - 1)
    return pl.pallas_call(
        functools.partial(_mix_kernel, nt=nt, n_tiles=n_tiles),
        grid=(n_tiles + 1,),
        in_specs=[pl.BlockSpec((1, tt, D_MODEL), lambda i: (tile(i) // nt, tile(i) % nt, 0)),
                  pl.BlockSpec((1, 6, D_MODEL), lambda i: (tile(i) // nt, 0, 0))]
                 + [_const_spec(a.shape) for a in consts],
        out_specs=[pl.BlockSpec((1, tt, D_MODEL), lambda i: (tile(i) // nt, tile(i) % nt, 0)),
                   pl.BlockSpec((2, tt, PLANE_W), lambda i: (0, i, 0)),
                   pl.BlockSpec((N_EXP, tt), lambda i: (0, i)),
                   pl.BlockSpec((1, HEADS, DK, DK), lambda i: (tile(i) // nt, 0, 0, 0)),
                   pl.BlockSpec((1, CONV_K - 1, CONV_W), lambda i: (tile(i) // nt, 0, 0))],
        out_shape=[jax.ShapeDtypeStruct((bsz, seq, D_MODEL), F32),
                   jax.ShapeDtypeStruct((2, n_tok, PLANE_W), I32),
                   jax.ShapeDtypeStruct((N_EXP, n_tok), F32),
                   jax.ShapeDtypeStruct((bsz, HEADS, DK, DK), F32),
                   jax.ShapeDtypeStruct((bsz, CONV_K - 1, CONV_W), F32)],
        scratch_shapes=[pltpu.VMEM((tt, IN_W), F32),
                        pltpu.VMEM((HEADS, DK, DK), F32),
                        pltpu.VMEM((CONV_K - 1, CONV_W), F32),
                        pltpu.VMEM((tt, KEY_W), F32)],
        compiler_params=pltpu.CompilerParams(
            dimension_semantics=("arbitrary",), vmem_limit_bytes=VMEM_LIMIT),
        name="mix",
    )(x, mod, *consts)


def _smp1_kernel(x_ref, mod_ref, gmix_ref, w_in_ref, lbl_ref, cw_ref, cb_ref, cst_ref,
                 f_ref, k_ref, q_ref, v_ref, gate_ref, yb_ref, sga_ref, sgb_ref, cv_out_ref):
    x = x_ref[...]
    sh1, sc1 = mod_ref[:, 0:D_MODEL], mod_ref[:, D_MODEL:2 * D_MODEL]
    h = _rms(x) * gmix_ref[...] * (1.0 + sc1) + sh1
    hb = h.astype(BF16)

    def proj(c, w):
        return _dot(hb, w_in_ref[:, c:c + w])

    lb = _lower_bound(lbl_ref[...])
    f = lb + (1.0 - lb) * _sigmoid(proj(C_F, KEY_W))
    f_ref[...] = f
    k_ref[...] = 1.0 - f
    q_ref[...] = _silu(proj(C_Q, KEY_W))
    v_ref[...] = proj(C_I, KEY_W)
    gate_ref[...] = _silu(proj(C_G, KEY_W))
    u = proj(C_CC, CONV_W) * proj(C_VB, CONV_W)
    c0, c1 = cst_ref[:, 0:CONV_W], cst_ref[:, CONV_W:2 * CONV_W]
    conv = cw_ref[0:1] * c0 + cw_ref[1:2] * c1 + cw_ref[2:3] * u + cb_ref[...]
    yb_ref[...] = proj(C_BB, CONV_W) * conv
    cv_out_ref[:, 0:CONV_W] = c1
    cv_out_ref[:, CONV_W:2 * CONV_W] = u
    sga_ref[...] = _sigmoid(proj(C_MGA, D_MODEL))
    sgb_ref[...] = _sigmoid(proj(C_MGB, D_MODEL))


def _smp1(x, mod, gmix, w_in, lbl, cw, cb, cst):
    n = x.shape[0]
    kw = jax.ShapeDtypeStruct((n, KEY_W), F32)
    dm = jax.ShapeDtypeStruct((n, D_MODEL), F32)
    return pl.pallas_call(
        _smp1_kernel,
        out_shape=[kw, kw, kw, kw, kw, kw, dm, dm,
                   jax.ShapeDtypeStruct((n, (CONV_K - 1) * CONV_W), F32)],
        compiler_params=pltpu.CompilerParams(vmem_limit_bytes=VMEM_LIMIT),
        name="smp1",
    )(x, mod, gmix, w_in, lbl, cw, cb, cst)


def _smp2(f, k, q, v, state):
    n = f.shape[0]
    info = plsc.get_sparse_core_info()
    lanes = info.num_lanes
    n_workers = info.num_cores * info.num_subcores
    tok_per = n // n_workers
    n_chunks = DK // lanes
    vec = lambda: pltpu.VMEM((DK,), F32)

    @pl.kernel(out_type=[jax.ShapeDtypeStruct(state.shape, F32), jax.ShapeDtypeStruct((n, KEY_W), F32)],
               mesh=_sc_mesh(), scratch_types=[pltpu.VMEM((DK, DK), F32), vec(), vec(), vec(), vec(), vec()],
               compiler_params=pltpu.CompilerParams(needs_layout_passes=False), name="smp2")
    def run(f_hbm, k_hbm, q_hbm, v_hbm, s_hbm, s_out_hbm, o_hbm, s_v, f_v, k_v, q_v, v_v, o_v):
        wid = lax.axis_index("subcore") * info.num_cores + lax.axis_index("core")

        def tile(j, carry):
            t = wid * tok_per + j // HEADS
            hcols = pl.ds((j % HEADS) * DK, DK)
            pltpu.sync_copy(s_hbm.at[t, j % HEADS], s_v)
            for src, dst in ((f_hbm, f_v), (k_hbm, k_v), (q_hbm, q_v), (v_hbm, v_v)):
                pltpu.sync_copy(src.at[t, hcols], dst)
            v_chunks = [v_v[pl.ds(c * lanes, lanes)] for c in range(n_chunks)]

            def row(d, acc):
                at = [jnp.zeros((lanes,), I32) + d]
                fd, kd, qd = (plsc.load_gather(r, at) for r in (f_v, k_v, q_v))
                out = []
                for c in range(n_chunks):
                    cols = pl.ds(c * lanes, lanes)
                    new = s_v[d, cols] * fd + kd * v_chunks[c]
                    s_v[d, cols] = new
                    out.append(acc[c] + qd * new)
                return tuple(out)

            acc = lax.fori_loop(0, DK, row, tuple(jnp.zeros((lanes,), F32) for _ in range(n_chunks)))
            for c in range(n_chunks):
                o_v[pl.ds(c * lanes, lanes)] = acc[c]
            pltpu.sync_copy(s_v, s_out_hbm.at[t, j % HEADS])
            pltpu.sync_copy(o_v, o_hbm.at[t, hcols])
            return carry

        lax.fori_loop(0, tok_per * HEADS, tile, 0)

    return run(f, k, q, v, state)


def _smp3_kernel(x_ref, mod_ref, o_ref, gate_ref, yb_ref, sga_ref, sgb_ref, hg_ref, gffn_ref,
                 w_oh_ref, w_oc_ref, w_o_ref, wr_hl_ref, w_sgu_ref, w_sd_ref,
                 h2_all_ref, lgt_all_ref, xmid_ref, h2_ref, lgt_ref):
    del h2_all_ref, lgt_all_ref
    parts = []
    for hd in range(HEADS):
        hs = slice(hd * DK, (hd + 1) * DK)
        parts.append(_rms(o_ref[:, hs]) * hg_ref[:, hs] * gate_ref[:, hs])
    ya = jnp.concatenate(parts, axis=1)
    mixed = (sga_ref[...] * _dot(ya.astype(BF16), w_oh_ref[...])
             + sgb_ref[...] * _dot(yb_ref[...].astype(BF16), w_oc_ref[...]))
    g1 = mod_ref[:, 2 * D_MODEL:3 * D_MODEL]
    x1 = x_ref[...] + g1 * _dot(mixed.astype(BF16), w_o_ref[...])
    mod_rows = tuple(mod_ref[:, j * D_MODEL:(j + 1) * D_MODEL] for j in (3, 4, 5))
    xmid, h2b, lgt = _ffn_pre(x1, mod_rows, gffn_ref[...], w_sgu_ref[...], w_sd_ref[...],
                              wr_hl_ref[...])
    xmid_ref[...] = xmid
    _pack_rows(h2b, h2_ref)
    lgt_ref[...] = lgt


def _smp3(x, mod, o, gate, yb, sga, sgb, hg, gffn, w_oh, w_oc, w_o, wr_hl, w_sgu, w_sd,
          h2_all, lgt_all, n_prompt):
    n = x.shape[0]
    vmem_args = [x, mod, o, gate, yb, sga, sgb, hg, gffn, w_oh, w_oc, w_o, wr_hl, w_sgu, w_sd]
    blk = n_prompt // n

    def full(a):
        nd = a.ndim
        return pl.BlockSpec(a.shape, lambda i, _nd=nd: (0,) * _nd)

    return pl.pallas_call(
        _smp3_kernel,
        grid=(1,),
        in_specs=[full(a) for a in vmem_args]
                 + [pl.BlockSpec(memory_space=pl.ANY), pl.BlockSpec(memory_space=pl.ANY)],
        out_specs=[pl.BlockSpec((n, D_MODEL), lambda i: (0, 0)),
                   pl.BlockSpec((2, n, PLANE_W), lambda i: (0, blk, 0)),
                   pl.BlockSpec((N_EXP, n), lambda i: (0, blk))],
        out_shape=[jax.ShapeDtypeStruct((n, D_MODEL), F32),
                   jax.ShapeDtypeStruct(h2_all.shape, h2_all.dtype),
                   jax.ShapeDtypeStruct(lgt_all.shape, lgt_all.dtype)],
        input_output_aliases={len(vmem_args): 1, len(vmem_args) + 1: 2},
        compiler_params=pltpu.CompilerParams(
            dimension_semantics=("arbitrary",), vmem_limit_bytes=VMEM_LIMIT),
        name="smp3",
    )(*vmem_args, h2_all, lgt_all)


def _route_kernel(lgt_ref, bias_ref, idx_ref, w_ref, rank_ref, cnt_ref):
    tr = ROUTE_TILE
    n_tiles = lgt_ref.shape[1] // tr

    def tile(i, carry):
        cols = pl.ds(pl.multiple_of(i * tr, tr), tr)
        picks, weights, ranks, carry = _route_tile(lgt_ref[:, cols], bias_ref[...], carry)
        for k in range(TOP_K):
            idx_ref[k:k + 1, cols] = picks[k]
            w_ref[k:k + 1, cols] = weights[k]
            rank_ref[k:k + 1, cols] = ranks[k]
        return carry

    total = lax.fori_loop(0, n_tiles, tile, jnp.zeros((N_EXP, 1), F32))
    cnt_ref[...] = jnp.broadcast_to(total, cnt_ref.shape).astype(I32)


def _route_tile(lgt, bias, carry):
    tr = lgt.shape[1]
    neg = -jnp.inf
    scores = _sigmoid(lgt)
    sel = scores + bias
    j8 = lax.broadcasted_iota(I32, (GRP_SZ, tr), 0)
    groups = [sel[g * GRP_SZ:(g + 1) * GRP_SZ] for g in range(N_GRP)]
    gscore = []
    for grp in groups:
        m1 = jnp.max(grp, axis=0, keepdims=True)
        i1 = jnp.min(jnp.where(grp == m1, j8, GRP_SZ), axis=0, keepdims=True)
        m2 = jnp.max(jnp.where(j8 == i1, neg, grp), axis=0, keepdims=True)
        gscore.append(m1 + m2)
    kept = []
    for g in range(N_GRP):
        beaten = jnp.zeros((1, tr), I32)
        for o in range(N_GRP):
            if o < g:
                beaten = beaten + (gscore[o] >= gscore[g]).astype(I32)
            elif o > g:
                beaten = beaten + (gscore[o] > gscore[g]).astype(I32)
        kept.append(jnp.where(beaten < TOPK_GRP, groups[g], neg))
    masked = jnp.concatenate(kept, axis=0)
    ei = lax.broadcasted_iota(I32, masked.shape, 0)
    chosen = jnp.zeros(masked.shape, jnp.bool_)
    picks, weights = [], []
    for _ in range(TOP_K):
        m = jnp.max(masked, axis=0, keepdims=True)
        pick = jnp.min(jnp.where((masked == m) & ~chosen, ei, N_EXP), axis=0, keepdims=True)
        hit = ei == pick
        weights.append(jnp.sum(jnp.where(hit, scores, 0.0), axis=0, keepdims=True))
        picks.append(pick)
        chosen = chosen | hit
        masked = jnp.where(hit, neg, masked)
    wsum = weights[0]
    for w in weights[1:]:
        wsum = wsum + w
    sel01 = chosen.astype(F32)
    r = lax.broadcasted_iota(I32, (tr, tr), 0)
    c = lax.broadcasted_iota(I32, (tr, tr), 1)
    before = (r < c).astype(BF16)
    cnt = _dot(sel01.astype(BF16), before) + carry
    weights = [w / wsum * ROUTED_SCALE for w in weights]
    ranks = [jnp.sum(jnp.where(ei == p, cnt, 0.0), axis=0, keepdims=True).astype(I32) for p in picks]
    return picks, weights, ranks, carry + jnp.sum(sel01, axis=1, keepdims=True)


def _route(lgt, bias):
    n = lgt.shape[1]
    assert n % ROUTE_TILE == 0
    slot = lambda dt: jax.ShapeDtypeStruct((TOP_K, n), dt)
    return pl.pallas_call(
        _route_kernel,
        out_shape=[slot(I32), slot(F32), slot(I32),
                   jax.ShapeDtypeStruct((N_EXP, 128), I32)],
        name="route",
    )(lgt, bias)


def _dest_kernel(start_ref, idx_ref, rank_ref, all_ref, *chunk_refs, n_rows, chunks):
    n_tok = idx_ref.shape[1]
    idx = idx_ref[...]
    acc = rank_ref[...]
    for e in range(N_EXP):
        acc = acc + jnp.where(idx == e, start_ref[e], 0)
    for k in range(TOP_K):
        for p in range(2):
            row = acc[k:k + 1, :] + p * n_rows
            seg = 2 * k + p
            all_ref[:, seg * n_tok:(seg + 1) * n_tok] = row
            for (c0, c1), ref in zip(chunks, chunk_refs):
                ref[:, seg * (c1 - c0):(seg + 1) * (c1 - c0)] = row[:, c0:c1]


def _dest(pad_start, idx, rank, n_rows, chunks):
    k, n_tok = idx.shape
    vmem = pl.BlockSpec(memory_space=pltpu.VMEM)
    sizes = [n_tok] + [c1 - c0 for c0, c1 in chunks]
    return pl.pallas_call(
        functools.partial(_dest_kernel, n_rows=n_rows, chunks=chunks),
        in_specs=[pl.BlockSpec(memory_space=pltpu.SMEM), vmem, vmem],
        out_specs=[vmem] * len(sizes),
        out_shape=[jax.ShapeDtypeStruct((1, 2 * k * n), I32) for n in sizes],
        name="dest",
    )(pad_start, idx, rank)


def _sc_mesh():
    return plsc.VectorSubcoreMesh(core_axis_name="core", subcore_axis_name="subcore")


def _dispatch(rows, dest, n_out):
    n, width = rows.shape
    win = SC_WINDOW
    steps = n // win

    @pl.kernel(out_type=jax.ShapeDtypeStruct((n_out, width), rows.dtype), mesh=_sc_mesh(),
               scratch_types=[], name="dispatch")
    def run(x_hbm, *refs):
        i_hbms, o_hbm = refs[:TOP_K], refs[TOP_K]

        def body(x_vmem, *i_vmems):
            for i_vmem in i_vmems:
                pltpu.sync_copy(x_vmem, o_hbm.at[i_vmem.at[0]])

        pltpu.emit_pipeline(
            body,
            grid=(steps,),
            in_specs=[pl.BlockSpec((win, width), lambda i: (i, 0))]
                     + [pl.BlockSpec((1, win), lambda i, k=k: (0, k * steps + i)) for k in range(TOP_K)],
            out_specs=[],
            core_axis_name=("core", "subcore"),
            dimension_semantics=(pltpu.PARALLEL,),
        )(x_hbm, *i_hbms)

    assert dest.shape == (1, TOP_K * n)
    return run(rows, *([dest] * TOP_K))


def _combine(rows, dest_flat):
    width = rows.shape[1]
    n = dest_flat.shape[1]
    win = SC_WINDOW

    @pl.kernel(out_type=jax.ShapeDtypeStruct((n, width), rows.dtype), mesh=_sc_mesh(),
               scratch_types=[], name="combine")
    def run(y_hbm, i_hbm, o_hbm):
        def body(i_vmem, o_vmem):
            pltpu.sync_copy(y_hbm.at[i_vmem.at[0]], o_vmem)

        pltpu.emit_pipeline(
            body,
            grid=(n // win,),
            in_specs=[pl.BlockSpec((1, win), lambda i: (0, i))],
            out_specs=[pl.BlockSpec((win, width), lambda i: (i, 0))],
            core_axis_name=("core", "subcore"),
            dimension_semantics=(pltpu.PARALLEL,),
        )(i_hbm, o_hbm)

    return run(rows, dest_flat)


def _wcast(w_gate, w_up, w_down):
    info = plsc.get_sparse_core_info()
    lanes = info.num_lanes
    n_workers = info.num_cores * info.num_subcores
    per = N_EXP // n_workers
    rows, cols = 128, 256
    words = lambda w: jax.ShapeDtypeStruct((w.shape[0], w.shape[1] // 2, w.shape[2]), I32)

    @pl.kernel(out_type=[words(w_gate), words(w_up), words(w_down)], mesh=_sc_mesh(),
               scratch_types=[pltpu.VMEM((rows, cols), F32), pltpu.VMEM((rows // 2, cols), I32)],
               compiler_params=pltpu.CompilerParams(needs_layout_passes=False), name="wcast")
    def run(g_hbm, u_hbm, d_hbm, go_hbm, uo_hbm, do_hbm, in_v, out_v):
        wid = lax.axis_index("subcore") * info.num_cores + lax.axis_index("core")

        def cast_tile(src, dst):
            pltpu.sync_copy(src, in_v)

            def pair(i, carry):
                for c in range(cols // lanes):
                    at = pl.ds(c * lanes, lanes)
                    both = plsc.pack(in_v[2 * i, at], in_v[2 * i + 1, at], format=plsc.PackFormat.INTERLEAVED)
                    out_v[i, at] = plsc.bitcast(both, I32)
                return carry

            lax.fori_loop(0, rows // 2, pair, 0)
            pltpu.sync_copy(out_v, dst)

        def tiles(src, dst, e):
            n_c = src.shape[2] // cols

            def one(t, carry):
                r, c = t // n_c, t % n_c
                cast_tile(src.at[e, pl.ds(r * rows, rows), pl.ds(c * cols, cols)],
                          dst.at[e, pl.ds(r * (rows // 2), rows // 2), pl.ds(c * cols, cols)])
                return carry

            lax.fori_loop(0, (src.shape[1] // rows) * n_c, one, 0)

        def expert(j, carry):
            e = wid * per + j
            tiles(g_hbm, go_hbm, e)
            tiles(u_hbm, uo_hbm, e)
            tiles(d_hbm, do_hbm, e)
            return carry

        lax.fori_loop(0, per, expert, 0)

    return run(w_gate, w_up, w_down)


def _gmm_kernel(blk_exp_ref, n_used_ref, xs_hbm, wg_hbm, wu_hbm, wd_hbm, ys_hbm,
                xbuf, ybuf, wgu_w, wd_w, xsem, ysem, wsem, run_ref):
    nx, ny, bm = xbuf.shape[0], ybuf.shape[0], xbuf.shape[2]
    n_used = n_used_ref[0]

    def x_copies(b):
        rows, slot = pl.ds(b * bm, bm), b % nx
        return [pltpu.make_async_copy(xs_hbm.at[p, rows, :], xbuf.at[slot, p], xsem.at[slot, p]) for p in range(2)]

    def y_copies(b):
        rows, slot = pl.ds(b * bm, bm), b % ny
        return [pltpu.make_async_copy(ybuf.at[slot, p], ys_hbm.at[p, rows, :], ysem.at[slot, p]) for p in range(2)]

    def start(copies):
        for c in copies:
            c.start()

    def wait(copies):
        for c in copies:
            c.wait()

    def w_copies(e, slot):
        return (pltpu.make_async_copy(wg_hbm.at[e], wgu_w.at[slot, :, pl.ds(0, EXP_FF)], wsem.at[slot, 0]),
                pltpu.make_async_copy(wu_hbm.at[e], wgu_w.at[slot, :, pl.ds(EXP_FF, EXP_FF)], wsem.at[slot, 1]),
                pltpu.make_async_copy(wd_hbm.at[e], wd_w.at[slot], wsem.at[slot, 2]))

    def run_end(b):
        return lax.while_loop(lambda j: (j < n_used) & (blk_exp_ref[jnp.minimum(j, n_used - 1)] == blk_exp_ref[b]),
                              lambda j: j + 1, b + 1)

    run_ref[0] = 0
    start(x_copies(0))
    start(w_copies(blk_exp_ref[0], 0))
    for j in range(1, nx - 1):
        @pl.when(j < n_used)
        def _():
            start(x_copies(j))

    def block(b, carry):
        @pl.when(b + nx - 1 < n_used)
        def _():
            start(x_copies(b + nx - 1))

        @pl.when((b == 0) | (blk_exp_ref[b] != blk_exp_ref[jnp.maximum(b - 1, 0)]))
        def _():
            wslot = run_ref[0] % 2
            run_ref[0] = run_ref[0] + 1
            wait(w_copies(blk_exp_ref[b], wslot))
            nxt = run_end(b)

            @pl.when(nxt < n_used)
            def _():
                start(w_copies(blk_exp_ref[jnp.minimum(nxt, n_used - 1)], 1 - wslot))

        wait(x_copies(b))

        @pl.when(b >= ny)
        def _():
            wait(y_copies(b - ny))

        xslot, wslot = b % nx, (run_ref[0] + 1) % 2
        wgu = pltpu.bitcast(wgu_w[wslot], BF16)
        wd = pltpu.bitcast(wd_w[wslot], BF16)
        xc = _unpack_rows(xbuf[xslot, 0], xbuf[xslot, 1])
        gu = sum(_dot(c, wgu[i * PLANE_W:(i + 1) * PLANE_W, :]) for i, c in enumerate(xc))
        act = (_silu(gu[:, :EXP_FF]) * gu[:, EXP_FF:]).astype(BF16)
        _pack_rows(_dot(act, wd).astype(BF16), ybuf.at[b % ny])
        start(y_copies(b))
        return carry

    lax.fori_loop(0, n_used, block, 0)

    for j in range(ny, 0, -1):
        @pl.when(n_used >= j)
        def _():
            wait(y_copies(n_used - j))


def _gmm(blk_exp, n_used, xs, w_gate, w_up, w_down):
    n_rows = xs.shape[1]
    bm = GMM_BM
    nb = n_rows // bm

    assert blk_exp.shape == (nb,)
    any_spec = pl.BlockSpec(memory_space=pl.ANY)
    grid_spec = pltpu.PrefetchScalarGridSpec(
        num_scalar_prefetch=2,
        grid=(1,),
        in_specs=[any_spec, any_spec, any_spec, any_spec],
        out_specs=any_spec,
        scratch_shapes=[pltpu.VMEM((GMM_NX, 2, bm, PLANE_W), I32), pltpu.VMEM((GMM_NY, 2, bm, PLANE_W), I32),
                        pltpu.VMEM((2, D_MODEL // 2, 2 * EXP_FF), I32), pltpu.VMEM((2, EXP_FF // 2, D_MODEL), I32),
                        pltpu.SemaphoreType.DMA((GMM_NX, 2)), pltpu.SemaphoreType.DMA((GMM_NY, 2)),
                        pltpu.SemaphoreType.DMA((2, 3)), pltpu.SMEM((1,), I32)],
    )
    return pl.pallas_call(
        _gmm_kernel,
        grid_spec=grid_spec,
        out_shape=jax.ShapeDtypeStruct((2, n_rows, PLANE_W), I32),
        compiler_params=pltpu.CompilerParams(dimension_semantics=("arbitrary",)),
        name="gmm",
    )(blk_exp, n_used, xs, w_gate, w_up, w_down)


def _final_kernel(xmid_ref, g2_ref, z_ref, w_ref, gfin_ref, *rest):
    y_ref = rest[-1]
    accs = [jnp.zeros((xmid_ref.shape[0], PLANE_W), F32) for _ in range(4)]
    w_cols = w_ref[...].T
    for k in range(TOP_K):
        wk = w_cols[:, k:k + 1]
        cols = _unpack_rows(z_ref[k, 0], z_ref[k, 1])
        accs = [a + wk * c.astype(F32) for a, c in zip(accs, cols)]
    acc = jnp.concatenate(accs, axis=1)
    y_ref[...] = _rms(xmid_ref[...] + g2_ref[0] * acc) * gfin_ref[...]


def _final(xmid, g2, z, w_t, gfin, tile, *, n_tiles, x_tile0, z_tile0, w_tile0, tiles_per_g2, y_prev=None):
    args = [xmid, g2, z, w_t, gfin]
    in_specs = [pl.BlockSpec((tile, D_MODEL), lambda i: (x_tile0 + i, 0)),
                pl.BlockSpec((1, g2.shape[1], D_MODEL), lambda i: ((x_tile0 + i) // tiles_per_g2, 0, 0)),
                pl.BlockSpec((TOP_K, 2, tile, PLANE_W), lambda i: (0, 0, z_tile0 + i, 0)),
                pl.BlockSpec((TOP_K, tile), lambda i: (0, w_tile0 + i)),
                pl.BlockSpec((1, D_MODEL), lambda i: (0, 0))]
    aliases = {}
    if y_prev is not None:
        args.append(y_prev)
        in_specs.append(pl.BlockSpec(memory_space=pl.ANY))
        aliases = {len(args) - 1: 0}
    return pl.pallas_call(
        _final_kernel,
        grid=(n_tiles,),
        in_specs=in_specs,
        out_specs=pl.BlockSpec((tile, D_MODEL), lambda i: (x_tile0 + i, 0)),
        out_shape=jax.ShapeDtypeStruct(xmid.shape, F32),
        input_output_aliases=aliases,
        compiler_params=pltpu.CompilerParams(dimension_semantics=("arbitrary",)),
        name="final",
    )(*args)


def kernel(x_prompt, x_sample, state_hgrn, state_conv, c_prompt, c_sample, w_ada, b_ada, norm_mix_g, norm_ffn_g, w_in, lb_logits, hgrn_norm_g, conv_w, conv_b, w_out_hgrn, w_out_conv, w_o, w_router, router_bias, w_exp_gate, w_exp_up, w_exp_down, w_sh_gate, w_sh_up, w_sh_down, final_norm_g):
    assert w_ada.shape[0] == 1 and lb_logits.shape[0] == 2
    bsz, seq, _ = x_prompt.shape
    n_smp = x_sample.shape[0]
    n_prompt = bsz * seq
    n_tok = n_prompt + n_smp

    w_in_b = w_in[0].astype(BF16)
    w_oh_b = w_out_hgrn[0].astype(BF16)
    w_oc_b = w_out_conv[0].astype(BF16)
    w_o_b = w_o[0].astype(BF16)
    wr_t = w_router[0].T
    wr_hi = wr_t.astype(BF16)
    wr_hl = jnp.concatenate([wr_hi, (wr_t - wr_hi.astype(F32)).astype(BF16)], axis=0)
    w_sgu = jnp.concatenate([w_sh_gate[0], w_sh_up[0]], axis=1).astype(BF16)
    w_sd = w_sh_down[0].astype(BF16)
    gmix = norm_mix_g[0].reshape(1, D_MODEL)
    gffn = norm_ffn_g[0].reshape(1, D_MODEL)
    hg = hgrn_norm_g[0].reshape(1, KEY_W)
    cw = conv_w[0]
    cb = conv_b[0].reshape(1, CONV_W)
    gfin = final_norm_g.reshape(1, D_MODEL)

    mod_p, mod_s = _ada(c_prompt, c_sample, w_ada[0], b_ada[0])

    xmid_p, h2_all, lgt_all, s_p, cv_p = _mix(
        x_prompt, mod_p.reshape(bsz, 6, D_MODEL), n_tok, gmix, gffn, w_in_b, lb_logits, hg, cw, cb, w_oh_b, w_oc_b, w_o_b, wr_hl, w_sgu, w_sd)

    xs2 = x_sample.reshape(n_smp, D_MODEL)
    f, kk, q, v, gate, yb, sga, sgb, cv_s = _smp1(
        xs2, mod_s, gmix, w_in_b, lb_logits, cw, cb, state_conv[0].reshape(n_smp, (CONV_K - 1) * CONV_W))
    s_s, o_s = _smp2(f, kk, q, v, state_hgrn[0])
    xmid_s, h2_all, lgt_all = _smp3(xs2, mod_s, o_s, gate, yb, sga, sgb, hg, gffn,
                                    w_oh_b, w_oc_b, w_o_b, wr_hl, w_sgu, w_sd,
                                    h2_all, lgt_all, n_prompt)

    idx, w_tok, rank, cnt = _route(lgt_all, router_bias[0].reshape(N_EXP, 1))

    bm = GMM_BM
    n_blocks = (n_tok * TOP_K + N_EXP * (bm - 1)) // bm
    n_rows = n_blocks * bm
    counts = cnt[:, 0]
    padded = (counts + bm - 1) // bm * bm
    pad_end = jnp.cumsum(padded)
    pad_start = pad_end - padded
    blk_row0 = jnp.arange(n_blocks, dtype=I32) * bm
    blk_exp = jnp.minimum(jnp.sum((pad_end[None, :] <= blk_row0[:, None]).astype(I32), axis=1), N_EXP - 1)
    n_used = (pad_end[-1:] // bm).astype(I32)
    cuts = [0] + [n_prompt * f // FINAL_SPLIT[-1] for f in FINAL_SPLIT]
    assert all(c % FINAL_TILE == 0 for c in cuts) and cuts[-1] == n_prompt
    chunks = tuple((c0, n_tok if c1 == n_prompt else c1) for c0, c1 in zip(cuts[:-1], cuts[1:]))
    dest, *chunk_dest = _dest(pad_start.astype(I32), idx, rank, n_rows, chunks)

    xs = _dispatch(h2_all.reshape(2 * n_tok, PLANE_W), dest, 2 * n_rows).reshape(2, n_rows, PLANE_W)
    ys = _gmm(blk_exp, n_used, xs, *_wcast(w_exp_gate[0], w_exp_up[0], w_exp_down[0]))

    ys_flat = ys.reshape(2 * n_rows, PLANE_W)
    w_t = w_tok
    xmid_p2 = xmid_p.reshape(n_prompt, D_MODEL)
    g2_p = mod_p[:, 5 * D_MODEL:].reshape(bsz, 1, D_MODEL)
    g2_s = mod_s[:, 5 * D_MODEL:].reshape(1, n_smp, D_MODEL)
    y_p = None
    for (c0, c1), dest_c in zip(chunks, chunk_dest):
        z = _combine(ys_flat, dest_c).reshape(TOP_K, 2, c1 - c0, PLANE_W)
        y_p = _final(xmid_p2, g2_p, z, w_t, gfin, FINAL_TILE, n_tiles=(min(c1, n_prompt) - c0) // FINAL_TILE,
                     x_tile0=c0 // FINAL_TILE, z_tile0=0, w_tile0=c0 // FINAL_TILE,
                     tiles_per_g2=seq // FINAL_TILE, y_prev=y_p)
    y_s = _final(xmid_s, g2_s, z, w_t, gfin, n_smp, n_tiles=1, x_tile0=0, z_tile0=(n_prompt - c0) // n_smp,
                 w_tile0=n_prompt // n_smp, tiles_per_g2=1)

    return (y_p.reshape(bsz, seq, D_MODEL), y_s.reshape(n_smp, 1, D_MODEL),
            s_p[None], cv_p[None], s_s[None], cv_s.reshape(1, n_smp, CONV_K - 1, CONV_W))
```

```python
import functools

import jax
import jax.numpy as jnp
from jax import lax
from jax.experimental import pallas as pl
from jax.experimental.pallas import tpu as pltpu
from jax.experimental.pallas import tpu_sc as plsc

F32 = jnp.float32
BF16 = jnp.bfloat16
I32 = jnp.int32

D_MODEL = 1024
HALF_D = D_MODEL // 2
HEADS = 4
DK = 128
KEY_W = HEADS * DK
CONV_W = 512
CONV_K = 3
IN_W = 2 * KEY_W + 2 * KEY_W + 3 * CONV_W + 2 * D_MODEL
N_EXP = 64
TOP_K = 8
N_GRP = 8
GRP_SZ = N_EXP // N_GRP
TOPK_GRP = 4
EXP_FF = 256
SH_FF = 256
ROUTED_SCALE = 2.5
EPS = 1e-6

C_Q, C_F, C_I, C_G = 0, 512, 1024, 1536
C_BB, C_CC, C_VB = 2048, 2560, 3072
C_MGA, C_MGB = 3584, 4608

MIX_TILE = 512
SUB = 256
CHUNK = 64
ROUTE_TILE = 128
GMM_BM = 512
GMM_NX = 4
GMM_NY = 3
FINAL_TILE = 512
FINAL_SPLIT = (4, 12, 22, 32)
SC_WINDOW = 128
PLANE_W = HALF_D // 2
VMEM_LIMIT = 56 * 1024 * 1024


def _dot(a, b):
    return jnp.dot(a, b, preferred_element_type=F32)


def _dot_nt(a, b):
    return lax.dot_general(a, b, (((1,), (1,)), ((), ())), preferred_element_type=F32)


def _dot_tn(a, b):
    return lax.dot_general(a, b, (((0,), (0,)), ((), ())), preferred_element_type=F32)


def _sigmoid(x):
    return 0.5 * jnp.tanh(0.5 * x) + 0.5


def _silu(x):
    h = 0.5 * x
    return h * jnp.tanh(h) + h


def _rms(x):
    return x * lax.rsqrt(jnp.mean(x * x, axis=-1, keepdims=True) + EPS)


def _lower_bound(lbl):
    a, b = lbl[0:1], lbl[1:2]
    m = jnp.maximum(a, b)
    ea, eb = jnp.exp(a - m), jnp.exp(b - m)
    return ea / (ea + eb)


def _split3(x):
    hi = x.astype(BF16)
    r1 = x - hi.astype(F32)
    mid = r1.astype(BF16)
    lo = (r1 - mid.astype(F32)).astype(BF16)
    return hi, mid, lo


def _words(lo_b, hi_b):
    lo = lax.shift_right_logical(lax.bitcast_convert_type(lo_b.astype(F32), I32), 16)
    hi = lax.bitcast_convert_type(hi_b.astype(F32), I32) & jnp.int32(-65536)
    return lo | hi


def _halves(w):
    lo = lax.bitcast_convert_type(lax.shift_left(w, 16), F32)
    hi = lax.bitcast_convert_type(w & jnp.int32(-65536), F32)
    return lo.astype(BF16), hi.astype(BF16)


def _pack_rows(xb, out_ref):
    words = _words(xb[:, :HALF_D], xb[:, HALF_D:])
    out_ref[0] = words[:, :PLANE_W]
    out_ref[1] = words[:, PLANE_W:]


def _unpack_rows(p0, p1):
    c0, c2 = _halves(p0)
    c1, c3 = _halves(p1)
    return c0, c1, c2, c3


def _ada_kernel(cp_ref, cs_ref, w_ref, b_ref, op_ref, os_ref):
    w = w_ref[...].astype(BF16)
    for c_ref, o_ref in ((cp_ref, op_ref), (cs_ref, os_ref)):
        o_ref[...] = _dot(_silu(c_ref[...]).astype(BF16), w) + b_ref[...]


def _ada(c_prompt, c_sample, w_ada, b_ada):
    blk = 1024
    rows = lambda c: pl.BlockSpec((c.shape[0], D_MODEL), lambda j: (0, 0))
    cols = lambda c: pl.BlockSpec((c.shape[0], blk), lambda j: (0, j))
    return pl.pallas_call(
        _ada_kernel,
        grid=(6 * D_MODEL // blk,),
        in_specs=[rows(c_prompt), rows(c_sample),
                  pl.BlockSpec((D_MODEL, blk), lambda j: (0, j)),
                  pl.BlockSpec((1, blk), lambda j: (0, j))],
        out_specs=[cols(c_prompt), cols(c_sample)],
        out_shape=[jax.ShapeDtypeStruct((c.shape[0], 6 * D_MODEL), F32) for c in (c_prompt, c_sample)],
        name="ada",
    )(c_prompt, c_sample, w_ada, b_ada.reshape(1, -1))


def _ffn_pre(x1, mod_rows, gffn, w_sgu, w_sd, wr_hl):
    sh2, sc2, g2 = mod_rows
    h2 = _rms(x1) * gffn * (1.0 + sc2) + sh2
    h2b = h2.astype(BF16)
    gu = _dot(h2b, w_sgu)
    act = _silu(gu[:, :SH_FF]) * gu[:, SH_FF:]
    xmid = x1 + g2 * _dot(act.astype(BF16), w_sd)
    h2lo = (h2 - h2b.astype(F32)).astype(BF16)
    both = _dot_nt(wr_hl, h2b)
    lgt = both[:N_EXP] + both[N_EXP:] + _dot_nt(wr_hl[:N_EXP], h2lo)
    return xmid, h2b, lgt


def _mix_kernel(*refs, nt, n_tiles):
    i = pl.program_id(0)
    rbias_ref, h2_ref = refs[15], refs[17]
    idx_ref, w_ref, rank_ref, cnt_ref, lgt_ref = refs[18], refs[19], refs[20], refs[21], refs[-1]

    @pl.when(i == 0)
    def _():
        lgt_ref[...] = jnp.zeros_like(lgt_ref)
        cnt_ref[...] = jnp.zeros_like(cnt_ref)

    def route_previous(keep):
        carry = cnt_ref[:, 0:1]
        for c0 in range(0, lgt_ref.shape[1], ROUTE_TILE):
            cols = slice(c0, c0 + ROUTE_TILE)
            picks, weights, ranks, carry = _route_tile(lgt_ref[:, cols], rbias_ref[...], carry)
            for k in range(TOP_K):
                idx_ref[k:k + 1, cols] = picks[k]
                w_ref[k:k + 1, cols] = weights[k]
                rank_ref[k:k + 1, cols] = ranks[k]
            yield
        cnt_ref[...] = jnp.broadcast_to(jnp.where(keep, carry, 0.0), cnt_ref.shape)

    @pl.when(i == n_tiles)
    def _():
        h2_ref[...] = jnp.zeros_like(h2_ref)
        for _ in route_previous(True):
            pass

    @pl.when(i < n_tiles)
    def _():
        _mix_tile(i % nt, nt, route_previous(i > 0), *refs)


def _mix_tile(t, nt, routing, x_ref, mod_ref, gmix_ref, gffn_ref, w_in_ref, lbl_ref, hg_ref, cw_ref, cb_ref,
              w_oh_ref, w_oc_ref, w_o_ref, wr_hl_ref, w_sgu_ref, w_sd_ref, rbias_ref,
              xmid_ref, h2_ref, idx_ref, w_ref, rank_ref, cnt_ref, s_out_ref, cv_out_ref,
              proj_ref, st_ref, cbuf_ref, ya_ref, lgt_ref):
    del rbias_ref, idx_ref, w_ref, rank_ref, cnt_ref
    tt = x_ref.shape[1]

    @pl.when(t == 0)
    def _():
        st_ref[...] = jnp.zeros_like(st_ref)
        cbuf_ref[...] = jnp.zeros_like(cbuf_ref)

    x = x_ref[0]
    mod = mod_ref[0]
    sh1, sc1, g1 = mod[0:1], mod[1:2], mod[2:3]
    h = _rms(x) * gmix_ref[...] * (1.0 + sc1) + sh1
    hb = h.astype(BF16)
    for c in range(0, IN_W, 512):
        proj_ref[:, c:c + 512] = _dot(hb, w_in_ref[:, c:c + 512])
        if c % 1024 == 512:
            next(routing, None)
    for _ in routing:
        pass

    lb = _lower_bound(lbl_ref[...])
    row = lax.broadcasted_iota(I32, (SUB, SUB), 0)
    col = lax.broadcasted_iota(I32, (SUB, SUB), 1)
    tri = (col <= row).astype(BF16)
    mask_d = (col <= row) & (row // CHUNK == col // CHUNK)
    mask_a = row // (2 * CHUNK) == col // (2 * CHUNK)
    n_ch = SUB // CHUNK

    def by_chunk(vals):
        return jnp.concatenate([jnp.zeros((CHUNK, DK), F32) if v is None
                                else jnp.broadcast_to(v, (CHUNK, DK)) for v in vals], axis=0)

    for s in range(tt // SUB):
        r0 = s * SUB
        f = lb + (1.0 - lb) * _sigmoid(proj_ref[r0:r0 + SUB, C_F:C_F + KEY_W])
        kk = 1.0 - f
        hi, mid, lo = _split3(jnp.log(f))
        bc = _dot(tri, hi) + _dot(tri, mid) + _dot(tri, lo)
        for hd in range(HEADS):
            hs = slice(hd * DK, (hd + 1) * DK)
            bh = bc[:, hs]
            at = lambda r: bh[r:r + 1]
            mids = [at(c * CHUNK + CHUNK // 2 - 1) for c in range(n_ch)]
            pair_mid = [at(CHUNK - 1), at(3 * CHUNK - 1)]
            step_mid, step_end = at(2 * CHUNK - 1), at(SUB - 1)
            arg = bh - by_chunk(mids)
            e_pos, e_neg = jnp.exp(arg), jnp.exp(-arg)
            q = _silu(proj_ref[r0:r0 + SUB, C_Q + hd * DK:C_Q + (hd + 1) * DK])
            v = proj_ref[r0:r0 + SUB, C_I + hd * DK:C_I + (hd + 1) * DK]
            qd = q * e_pos
            kd = kk[:, hs] * e_neg
            q_in = qd * by_chunk([jnp.exp(m) for m in mids])
            k_end = kd * by_chunk([jnp.exp(step_end - m) for m in mids])
            qa = qd * by_chunk([None, jnp.exp(mids[1] - pair_mid[0]), None, jnp.exp(mids[3] - pair_mid[1])])
            ka = kd * by_chunk([jnp.exp(pair_mid[0] - mids[0]), None, jnp.exp(pair_mid[1] - mids[2]), None])
            qb = qd * by_chunk([None, None, jnp.exp(mids[2] - step_mid), jnp.exp(mids[3] - step_mid)])
            kb = kd * by_chunk([jnp.exp(step_mid - mids[0]), jnp.exp(step_mid - mids[1]), None, None])
            att = jnp.where(mask_d, _dot_nt(qd.astype(BF16), kd.astype(BF16)), 0.0)
            att = att + jnp.where(mask_a, _dot_nt(qa.astype(BF16), ka.astype(BF16)), 0.0)
            att = att + _dot_nt(qb.astype(BF16), kb.astype(BF16))
            vb = v.astype(BF16)
            st = st_ref[hd]
            o = _dot(att.astype(BF16), vb) + _dot_nt(q_in.astype(BF16), st.astype(BF16))
            st_ref[hd] = st * jnp.exp(step_end) + _dot_tn(vb, k_end.astype(BF16))
            gate = _silu(proj_ref[r0:r0 + SUB, C_G + hd * DK:C_G + (hd + 1) * DK])
            ya_ref[r0:r0 + SUB, hs] = _rms(o) * hg_ref[:, hs] * gate

    u = proj_ref[:, C_CC:C_CC + CONV_W] * proj_ref[:, C_VB:C_VB + CONV_W]
    rows = lax.broadcasted_iota(I32, (tt, CONV_W), 0)
    c0, c1 = cbuf_ref[0:1], cbuf_ref[1:2]
    u1 = jnp.where(rows == 0, c1, pltpu.roll(u, 1, axis=0))
    u2 = jnp.where(rows == 0, c0, jnp.where(rows == 1, c1, pltpu.roll(u, 2, axis=0)))
    conv = cw_ref[0:1] * u2 + cw_ref[1:2] * u1 + cw_ref[2:3] * u + cb_ref[...]
    yb = proj_ref[:, C_BB:C_BB + CONV_W] * conv
    cbuf_ref[...] = u[tt - 2:tt]

    mixed = (_sigmoid(proj_ref[:, C_MGA:C_MGA + D_MODEL]) * _dot(ya_ref[...].astype(BF16), w_oh_ref[...])
             + _sigmoid(proj_ref[:, C_MGB:C_MGB + D_MODEL]) * _dot(yb.astype(BF16), w_oc_ref[...]))
    x1 = x + g1 * _dot(mixed.astype(BF16), w_o_ref[...])

    xmid, h2b, lgt = _ffn_pre(x1, (mod[3:4], mod[4:5], mod[5:6]), gffn_ref[...],
                              w_sgu_ref[...], w_sd_ref[...], wr_hl_ref[...])
    xmid_ref[0] = xmid
    _pack_rows(h2b, h2_ref)
    lgt_ref[...] = lgt

    @pl.when(t == nt - 1)
    def _():
        for hd in range(HEADS):
            s_out_ref[0, hd] = st_ref[hd].T
        cv_out_ref[0] = cbuf_ref[...]


def _const_spec(shape):
    nd = len(shape)
    return pl.BlockSpec(shape, lambda i, _nd=nd: (0,) * _nd, pipeline_mode=pl.Buffered(1))


def _mix(x, mod, n_tok, gmix, gffn, w_in, lbl, hg, cw, cb, w_oh, w_oc, w_o, wr_hl, w_sgu, w_sd, rbias):
    bsz, seq, _ = x.shape
    tt = MIX_TILE
    nt = seq // tt
    n_tiles = bsz * nt
    assert n_tiles * tt < n_tok <= (n_tiles + 1) * tt and tt % ROUTE_TILE == 0
    consts = [gmix, gffn, w_in, lbl, hg, cw, cb, w_oh, w_oc, w_o, wr_hl, w_sgu, w_sd, rbias]
    tile = lambda i: jnp.minimum(i, n_tiles - 1)
    routed = pl.BlockSpec((TOP_K, tt), lambda i: (0, jnp.maximum(i - 1, 0)))
    slot = lambda dt: jax.ShapeDtypeStruct((TOP_K, n_tiles * tt), dt)
    return pl.pallas_call(
        functools.partial(_mix_kernel, nt=nt, n_tiles=n_tiles),
        grid=(n_tiles + 1,),
        in_specs=[pl.BlockSpec((1, tt, D_MODEL), lambda i: (tile(i) // nt, tile(i) % nt, 0)),
                  pl.BlockSpec((1, 6, D_MODEL), lambda i: (tile(i) // nt, 0, 0))]
                 + [_const_spec(a.shape) for a in consts],
        out_specs=[pl.BlockSpec((1, tt, D_MODEL), lambda i: (tile(i) // nt, tile(i) % nt, 0)),
                   pl.BlockSpec((2, tt, PLANE_W), lambda i: (0, i, 0)),
                   routed, routed, routed,
                   pl.BlockSpec((N_EXP, 128), lambda i: (0, 0)),
                   pl.BlockSpec((1, HEADS, DK, DK), lambda i: (tile(i) // nt, 0, 0, 0)),
                   pl.BlockSpec((1, CONV_K - 1, CONV_W), lambda i: (tile(i) // nt, 0, 0))],
        out_shape=[jax.ShapeDtypeStruct((bsz, seq, D_MODEL), F32),
                   jax.ShapeDtypeStruct((2, n_tok, PLANE_W), I32),
                   slot(I32), slot(F32), slot(I32),
                   jax.ShapeDtypeStruct((N_EXP, 128), F32),
                   jax.ShapeDtypeStruct((bsz, HEADS, DK, DK), F32),
                   jax.ShapeDtypeStruct((bsz, CONV_K - 1, CONV_W), F32)],
        scratch_shapes=[pltpu.VMEM((tt, IN_W), F32),
                        pltpu.VMEM((HEADS, DK, DK), F32),
                        pltpu.VMEM((CONV_K - 1, CONV_W), F32),
                        pltpu.VMEM((tt, KEY_W), F32),
                        pltpu.VMEM((N_EXP, tt), F32)],
        compiler_params=pltpu.CompilerParams(
            dimension_semantics=("arbitrary",), vmem_limit_bytes=VMEM_LIMIT),
        name="mix",
    )(x, mod, *consts)


def _smp1_kernel(x_ref, mod_ref, gmix_ref, w_in_ref, lbl_ref, cw_ref, cb_ref, cst_ref,
                 f_ref, k_ref, q_ref, v_ref, gate_ref, yb_ref, sga_ref, sgb_ref, cv_out_ref):
    x = x_ref[...]
    sh1, sc1 = mod_ref[:, 0:D_MODEL], mod_ref[:, D_MODEL:2 * D_MODEL]
    h = _rms(x) * gmix_ref[...] * (1.0 + sc1) + sh1
    hb = h.astype(BF16)

    def proj(c, w):
        return _dot(hb, w_in_ref[:, c:c + w])

    lb = _lower_bound(lbl_ref[...])
    f = lb + (1.0 - lb) * _sigmoid(proj(C_F, KEY_W))
    f_ref[...] = f
    k_ref[...] = 1.0 - f
    q_ref[...] = _silu(proj(C_Q, KEY_W))
    v_ref[...] = proj(C_I, KEY_W)
    gate_ref[...] = _silu(proj(C_G, KEY_W))
    u = proj(C_CC, CONV_W) * proj(C_VB, CONV_W)
    c0, c1 = cst_ref[:, 0:CONV_W], cst_ref[:, CONV_W:2 * CONV_W]
    conv = cw_ref[0:1] * c0 + cw_ref[1:2] * c1 + cw_ref[2:3] * u + cb_ref[...]
    yb_ref[...] = proj(C_BB, CONV_W) * conv
    cv_out_ref[:, 0:CONV_W] = c1
    cv_out_ref[:, CONV_W:2 * CONV_W] = u
    sga_ref[...] = _sigmoid(proj(C_MGA, D_MODEL))
    sgb_ref[...] = _sigmoid(proj(C_MGB, D_MODEL))


def _smp1(x, mod, gmix, w_in, lbl, cw, cb, cst):
    n = x.shape[0]
    kw = jax.ShapeDtypeStruct((n, KEY_W), F32)
    dm = jax.ShapeDtypeStruct((n, D_MODEL), F32)
    return pl.pallas_call(
        _smp1_kernel,
        out_shape=[kw, kw, kw, kw, kw, kw, dm, dm,
                   jax.ShapeDtypeStruct((n, (CONV_K - 1) * CONV_W), F32)],
        compiler_params=pltpu.CompilerParams(vmem_limit_bytes=VMEM_LIMIT),
        name="smp1",
    )(x, mod, gmix, w_in, lbl, cw, cb, cst)


def _smp2(f, k, q, v, state):
    n = f.shape[0]
    info = plsc.get_sparse_core_info()
    lanes = info.num_lanes
    n_workers = info.num_cores * info.num_subcores
    tok_per = n // n_workers
    n_chunks = DK // lanes
    vec = lambda: pltpu.VMEM((DK,), F32)

    @pl.kernel(out_type=[jax.ShapeDtypeStruct(state.shape, F32), jax.ShapeDtypeStruct((n, KEY_W), F32)],
               mesh=_sc_mesh(), scratch_types=[pltpu.VMEM((DK, DK), F32), vec(), vec(), vec(), vec(), vec()],
               compiler_params=pltpu.CompilerParams(needs_layout_passes=False), name="smp2")
    def run(f_hbm, k_hbm, q_hbm, v_hbm, s_hbm, s_out_hbm, o_hbm, s_v, f_v, k_v, q_v, v_v, o_v):
        wid = lax.axis_index("subcore") * info.num_cores + lax.axis_index("core")

        def tile(j, carry):
            t = wid * tok_per + j // HEADS
            hcols = pl.ds((j % HEADS) * DK, DK)
            pltpu.sync_copy(s_hbm.at[t, j % HEADS], s_v)
            for src, dst in ((f_hbm, f_v), (k_hbm, k_v), (q_hbm, q_v), (v_hbm, v_v)):
                pltpu.sync_copy(src.at[t, hcols], dst)
            v_chunks = [v_v[pl.ds(c * lanes, lanes)] for c in range(n_chunks)]

            def row(d, acc):
                at = [jnp.zeros((lanes,), I32) + d]
                fd, kd, qd = (plsc.load_gather(r, at) for r in (f_v, k_v, q_v))
                out = []
                for c in range(n_chunks):
                    cols = pl.ds(c * lanes, lanes)
                    new = s_v[d, cols] * fd + kd * v_chunks[c]
                    s_v[d, cols] = new
                    out.append(acc[c] + qd * new)
                return tuple(out)

            acc = lax.fori_loop(0, DK, row, tuple(jnp.zeros((lanes,), F32) for _ in range(n_chunks)))
            for c in range(n_chunks):
                o_v[pl.ds(c * lanes, lanes)] = acc[c]
            pltpu.sync_copy(s_v, s_out_hbm.at[t, j % HEADS])
            pltpu.sync_copy(o_v, o_hbm.at[t, hcols])
            return carry

        lax.fori_loop(0, tok_per * HEADS, tile, 0)

    return run(f, k, q, v, state)


def _smp3_kernel(x_ref, mod_ref, o_ref, gate_ref, yb_ref, sga_ref, sgb_ref, hg_ref, gffn_ref,
                 w_oh_ref, w_oc_ref, w_o_ref, wr_hl_ref, w_sgu_ref, w_sd_ref,
                 rbias_ref, cnt_p_ref, idx_p_ref, w_p_ref, rank_p_ref, h2_all_ref,
                 xmid_ref, h2_ref, idx_ref, w_ref, rank_ref, cnt_ref):
    del h2_all_ref
    parts = []
    for hd in range(HEADS):
        hs = slice(hd * DK, (hd + 1) * DK)
        parts.append(_rms(o_ref[:, hs]) * hg_ref[:, hs] * gate_ref[:, hs])
    ya = jnp.concatenate(parts, axis=1)
    mixed = (sga_ref[...] * _dot(ya.astype(BF16), w_oh_ref[...])
             + sgb_ref[...] * _dot(yb_ref[...].astype(BF16), w_oc_ref[...]))
    g1 = mod_ref[:, 2 * D_MODEL:3 * D_MODEL]
    x1 = x_ref[...] + g1 * _dot(mixed.astype(BF16), w_o_ref[...])
    mod_rows = tuple(mod_ref[:, j * D_MODEL:(j + 1) * D_MODEL] for j in (3, 4, 5))
    xmid, h2b, lgt = _ffn_pre(x1, mod_rows, gffn_ref[...], w_sgu_ref[...], w_sd_ref[...],
                              wr_hl_ref[...])
    xmid_ref[...] = xmid
    _pack_rows(h2b, h2_ref)
    n_prompt = idx_p_ref.shape[1]
    picks, weights, ranks, total = _route_tile(lgt, rbias_ref[...], cnt_p_ref[:, 0:1])
    for src, dst, new in ((idx_p_ref, idx_ref, picks), (w_p_ref, w_ref, weights), (rank_p_ref, rank_ref, ranks)):
        dst[:, :n_prompt] = src[...]
        for k in range(TOP_K):
            dst[k:k + 1, n_prompt:] = new[k]
    cnt_ref[...] = jnp.broadcast_to(total, cnt_ref.shape).astype(I32)


def _smp3(x, mod, o, gate, yb, sga, sgb, hg, gffn, w_oh, w_oc, w_o, wr_hl, w_sgu, w_sd,
          rbias, cnt_p, idx_p, w_p, rank_p, h2_all):
    n = x.shape[0]
    vmem_args = [x, mod, o, gate, yb, sga, sgb, hg, gffn, w_oh, w_oc, w_o, wr_hl, w_sgu, w_sd,
                 rbias, cnt_p, idx_p, w_p, rank_p]
    n_prompt = idx_p.shape[1]
    blk = n_prompt // n
    slot = lambda dt: jax.ShapeDtypeStruct((TOP_K, n_prompt + n), dt)

    def full(a):
        nd = a.ndim
        return pl.BlockSpec(a.shape, lambda i, _nd=nd: (0,) * _nd)

    return pl.pallas_call(
        _smp3_kernel,
        grid=(1,),
        in_specs=[full(a) for a in vmem_args] + [pl.BlockSpec(memory_space=pl.ANY)],
        out_specs=[pl.BlockSpec((n, D_MODEL), lambda i: (0, 0)),
                   pl.BlockSpec((2, n, PLANE_W), lambda i: (0, blk, 0)),
                   pl.BlockSpec((TOP_K, n_prompt + n), lambda i: (0, 0)),
                   pl.BlockSpec((TOP_K, n_prompt + n), lambda i: (0, 0)),
                   pl.BlockSpec((TOP_K, n_prompt + n), lambda i: (0, 0)),
                   pl.BlockSpec((N_EXP, 128), lambda i: (0, 0))],
        out_shape=[jax.ShapeDtypeStruct((n, D_MODEL), F32),
                   jax.ShapeDtypeStruct(h2_all.shape, h2_all.dtype),
                   slot(I32), slot(F32), slot(I32),
                   jax.ShapeDtypeStruct((N_EXP, 128), I32)],
        input_output_aliases={len(vmem_args): 1},
        compiler_params=pltpu.CompilerParams(
            dimension_semantics=("arbitrary",), vmem_limit_bytes=VMEM_LIMIT),
        name="smp3",
    )(*vmem_args, h2_all)


def _route_tile(lgt, bias, carry):
    tr = lgt.shape[1]
    neg = -jnp.inf
    scores = _sigmoid(lgt)
    sel = scores + bias
    j8 = lax.broadcasted_iota(I32, (GRP_SZ, tr), 0)
    groups = [sel[g * GRP_SZ:(g + 1) * GRP_SZ] for g in range(N_GRP)]
    gscore = []
    for grp in groups:
        m1 = jnp.max(grp, axis=0, keepdims=True)
        i1 = jnp.min(jnp.where(grp == m1, j8, GRP_SZ), axis=0, keepdims=True)
        m2 = jnp.max(jnp.where(j8 == i1, neg, grp), axis=0, keepdims=True)
        gscore.append(m1 + m2)
    kept = []
    for g in range(N_GRP):
        beaten = jnp.zeros((1, tr), I32)
        for o in range(N_GRP):
            if o < g:
                beaten = beaten + (gscore[o] >= gscore[g]).astype(I32)
            elif o > g:
                beaten = beaten + (gscore[o] > gscore[g]).astype(I32)
        kept.append(jnp.where(beaten < TOPK_GRP, groups[g], neg))
    masked = jnp.concatenate(kept, axis=0)
    ei = lax.broadcasted_iota(I32, masked.shape, 0)
    chosen = jnp.zeros(masked.shape, jnp.bool_)
    picks, weights = [], []
    for _ in range(TOP_K):
        m = jnp.max(masked, axis=0, keepdims=True)
        pick = jnp.min(jnp.where((masked == m) & ~chosen, ei, N_EXP), axis=0, keepdims=True)
        hit = ei == pick
        weights.append(jnp.sum(jnp.where(hit, scores, 0.0), axis=0, keepdims=True))
        picks.append(pick)
        chosen = chosen | hit
        masked = jnp.where(hit, neg, masked)
    wsum = weights[0]
    for w in weights[1:]:
        wsum = wsum + w
    sel01 = chosen.astype(F32)
    r = lax.broadcasted_iota(I32, (tr, tr), 0)
    c = lax.broadcasted_iota(I32, (tr, tr), 1)
    before = (r < c).astype(BF16)
    cnt = _dot(sel01.astype(BF16), before) + carry
    weights = [w / wsum * ROUTED_SCALE for w in weights]
    ranks = [jnp.sum(jnp.where(ei == p, cnt, 0.0), axis=0, keepdims=True).astype(I32) for p in picks]
    return picks, weights, ranks, carry + jnp.sum(sel01, axis=1, keepdims=True)


def _dest_kernel(start_ref, idx_ref, rank_ref, all_ref, *chunk_refs, n_rows, chunks):
    n_tok = idx_ref.shape[1]
    idx = idx_ref[...]
    acc = rank_ref[...]
    for e in range(N_EXP):
        acc = acc + jnp.where(idx == e, start_ref[e], 0)
    for k in range(TOP_K):
        for p in range(2):
            row = acc[k:k + 1, :] + p * n_rows
            seg = 2 * k + p
            all_ref[:, seg * n_tok:(seg + 1) * n_tok] = row
            for (c0, c1), ref in zip(chunks, chunk_refs):
                ref[:, seg * (c1 - c0):(seg + 1) * (c1 - c0)] = row[:, c0:c1]


def _dest(pad_start, idx, rank, n_rows, chunks):
    k, n_tok = idx.shape
    vmem = pl.BlockSpec(memory_space=pltpu.VMEM)
    sizes = [n_tok] + [c1 - c0 for c0, c1 in chunks]
    return pl.pallas_call(
        functools.partial(_dest_kernel, n_rows=n_rows, chunks=chunks),
        in_specs=[pl.BlockSpec(memory_space=pltpu.SMEM), vmem, vmem],
        out_specs=[vmem] * len(sizes),
        out_shape=[jax.ShapeDtypeStruct((1, 2 * k * n), I32) for n in sizes],
        name="dest",
    )(pad_start, idx, rank)


def _sc_mesh():
    return plsc.VectorSubcoreMesh(core_axis_name="core", subcore_axis_name="subcore")


def _dispatch(rows, dest, n_out):
    n, width = rows.shape
    win = SC_WINDOW
    steps = n // win

    @pl.kernel(out_type=jax.ShapeDtypeStruct((n_out, width), rows.dtype), mesh=_sc_mesh(),
               scratch_types=[], name="dispatch")
    def run(x_hbm, *refs):
        i_hbms, o_hbm = refs[:TOP_K], refs[TOP_K]

        def body(x_vmem, *i_vmems):
            for i_vmem in i_vmems:
                pltpu.sync_copy(x_vmem, o_hbm.at[i_vmem.at[0]])

        pltpu.emit_pipeline(
            body,
            grid=(steps,),
            in_specs=[pl.BlockSpec((win, width), lambda i: (i, 0))]
                     + [pl.BlockSpec((1, win), lambda i, k=k: (0, k * steps + i)) for k in range(TOP_K)],
            out_specs=[],
            core_axis_name=("core", "subcore"),
            dimension_semantics=(pltpu.PARALLEL,),
        )(x_hbm, *i_hbms)

    assert dest.shape == (1, TOP_K * n)
    return run(rows, *([dest] * TOP_K))


def _combine(rows, dest_flat):
    width = rows.shape[1]
    n = dest_flat.shape[1]
    win = SC_WINDOW

    @pl.kernel(out_type=jax.ShapeDtypeStruct((n, width), rows.dtype), mesh=_sc_mesh(),
               scratch_types=[], name="combine")
    def run(y_hbm, i_hbm, o_hbm):
        def body(i_vmem, o_vmem):
            pltpu.sync_copy(y_hbm.at[i_vmem.at[0]], o_vmem)

        pltpu.emit_pipeline(
            body,
            grid=(n // win,),
            in_specs=[pl.BlockSpec((1, win), lambda i: (0, i))],
            out_specs=[pl.BlockSpec((win, width), lambda i: (i, 0))],
            core_axis_name=("core", "subcore"),
            dimension_semantics=(pltpu.PARALLEL,),
        )(i_hbm, o_hbm)

    return run(rows, dest_flat)


def _gmm_kernel(blk_exp_ref, n_used_ref, xs_hbm, wg_hbm, wu_hbm, wd_hbm, ys_hbm,
                xbuf, ybuf, wg32, wu32, wd32, wgu_b, wd_b, xsem, ysem, wsem, run_ref):
    nx, ny, bm = xbuf.shape[0], ybuf.shape[0], xbuf.shape[2]
    n_used = n_used_ref[0]

    def x_copies(b):
        rows, slot = pl.ds(b * bm, bm), b % nx
        return [pltpu.make_async_copy(xs_hbm.at[p, rows, :], xbuf.at[slot, p], xsem.at[slot, p]) for p in range(2)]

    def y_copies(b):
        rows, slot = pl.ds(b * bm, bm), b % ny
        return [pltpu.make_async_copy(ybuf.at[slot, p], ys_hbm.at[p, rows, :], ysem.at[slot, p]) for p in range(2)]

    def start(copies):
        for c in copies:
            c.start()

    def wait(copies):
        for c in copies:
            c.wait()

    def w_copies(e, slot):
        return (pltpu.make_async_copy(wg_hbm.at[e], wg32.at[slot], wsem.at[slot, 0]),
                pltpu.make_async_copy(wu_hbm.at[e], wu32.at[slot], wsem.at[slot, 1]),
                pltpu.make_async_copy(wd_hbm.at[e], wd32.at[slot], wsem.at[slot, 2]))

    def run_end(b):
        return lax.while_loop(lambda j: (j < n_used) & (blk_exp_ref[jnp.minimum(j, n_used - 1)] == blk_exp_ref[b]),
                              lambda j: j + 1, b + 1)

    run_ref[0] = 0
    start(x_copies(0))
    start(w_copies(blk_exp_ref[0], 0))
    for j in range(1, nx - 1):
        @pl.when(j < n_used)
        def _():
            start(x_copies(j))

    def block(b, carry):
        @pl.when(b + nx - 1 < n_used)
        def _():
            start(x_copies(b + nx - 1))

        @pl.when((b == 0) | (blk_exp_ref[b] != blk_exp_ref[jnp.maximum(b - 1, 0)]))
        def _():
            wslot = run_ref[0] % 2
            run_ref[0] = run_ref[0] + 1
            wait(w_copies(blk_exp_ref[b], wslot))
            wgu_b[:, 0:EXP_FF] = wg32[wslot].astype(BF16)
            wgu_b[:, EXP_FF:2 * EXP_FF] = wu32[wslot].astype(BF16)
            wd_b[...] = wd32[wslot].astype(BF16)
            nxt = run_end(b)

            @pl.when(nxt < n_used)
            def _():
                start(w_copies(blk_exp_ref[jnp.minimum(nxt, n_used - 1)], 1 - wslot))

        wait(x_copies(b))

        @pl.when(b >= ny)
        def _():
            wait(y_copies(b - ny))

        xslot = b % nx
        xc = _unpack_rows(xbuf[xslot, 0], xbuf[xslot, 1])
        gu = sum(_dot(c, wgu_b[i * PLANE_W:(i + 1) * PLANE_W, :]) for i, c in enumerate(xc))
        act = (_silu(gu[:, :EXP_FF]) * gu[:, EXP_FF:]).astype(BF16)
        _pack_rows(_dot(act, wd_b[...]).astype(BF16), ybuf.at[b % ny])
        start(y_copies(b))
        return carry

    lax.fori_loop(0, n_used, block, 0)

    for j in range(ny, 0, -1):
        @pl.when(n_used >= j)
        def _():
            wait(y_copies(n_used - j))


def _gmm(blk_exp, n_used, xs, w_gate, w_up, w_down):
    n_rows = xs.shape[1]
    bm = GMM_BM
    nb = n_rows // bm

    assert blk_exp.shape == (nb,)
    any_spec = pl.BlockSpec(memory_space=pl.ANY)
    grid_spec = pltpu.PrefetchScalarGridSpec(
        num_scalar_prefetch=2,
        grid=(1,),
        in_specs=[any_spec, any_spec, any_spec, any_spec],
        out_specs=any_spec,
        scratch_shapes=[pltpu.VMEM((GMM_NX, 2, bm, PLANE_W), I32), pltpu.VMEM((GMM_NY, 2, bm, PLANE_W), I32),
                        pltpu.VMEM((2, D_MODEL, EXP_FF), F32), pltpu.VMEM((2, D_MODEL, EXP_FF), F32),
                        pltpu.VMEM((2, EXP_FF, D_MODEL), F32),
                        pltpu.VMEM((D_MODEL, 2 * EXP_FF), BF16), pltpu.VMEM((EXP_FF, D_MODEL), BF16),
                        pltpu.SemaphoreType.DMA((GMM_NX, 2)), pltpu.SemaphoreType.DMA((GMM_NY, 2)),
                        pltpu.SemaphoreType.DMA((2, 3)), pltpu.SMEM((1,), I32)],
    )
    return pl.pallas_call(
        _gmm_kernel,
        grid_spec=grid_spec,
        out_shape=jax.ShapeDtypeStruct((2, n_rows, PLANE_W), I32),
        compiler_params=pltpu.CompilerParams(dimension_semantics=("arbitrary",)),
        name="gmm",
    )(blk_exp, n_used, xs, w_gate, w_up, w_down)


def _final_kernel(xmid_ref, g2_ref, z_ref, w_ref, gfin_ref, *rest):
    y_ref = rest[-1]
    accs = [jnp.zeros((xmid_ref.shape[0], PLANE_W), F32) for _ in range(4)]
    w_cols = w_ref[...].T
    for k in range(TOP_K):
        wk = w_cols[:, k:k + 1]
        cols = _unpack_rows(z_ref[k, 0], z_ref[k, 1])
        accs = [a + wk * c.astype(F32) for a, c in zip(accs, cols)]
    acc = jnp.concatenate(accs, axis=1)
    y_ref[...] = _rms(xmid_ref[...] + g2_ref[0] * acc) * gfin_ref[...]


def _final(xmid, g2, z, w_t, gfin, tile, *, n_tiles, x_tile0, z_tile0, w_tile0, tiles_per_g2, y_prev=None):
    args = [xmid, g2, z, w_t, gfin]
    in_specs = [pl.BlockSpec((tile, D_MODEL), lambda i: (x_tile0 + i, 0)),
                pl.BlockSpec((1, g2.shape[1], D_MODEL), lambda i: ((x_tile0 + i) // tiles_per_g2, 0, 0)),
                pl.BlockSpec((TOP_K, 2, tile, PLANE_W), lambda i: (0, 0, z_tile0 + i, 0)),
                pl.BlockSpec((TOP_K, tile), lambda i: (0, w_tile0 + i)),
                pl.BlockSpec((1, D_MODEL), lambda i: (0, 0))]
    aliases = {}
    if y_prev is not None:
        args.append(y_prev)
        in_specs.append(pl.BlockSpec(memory_space=pl.ANY))
        aliases = {len(args) - 1: 0}
    return pl.pallas_call(
        _final_kernel,
        grid=(n_tiles,),
        in_specs=in_specs,
        out_specs=pl.BlockSpec((tile, D_MODEL), lambda i: (x_tile0 + i, 0)),
        out_shape=jax.ShapeDtypeStruct(xmid.shape, F32),
        input_output_aliases=aliases,
        compiler_params=pltpu.CompilerParams(dimension_semantics=("arbitrary",)),
        name="final",
    )(*args)


def kernel(x_prompt, x_sample, state_hgrn, state_conv, c_prompt, c_sample, w_ada, b_ada, norm_mix_g, norm_ffn_g, w_in, lb_logits, hgrn_norm_g, conv_w, conv_b, w_out_hgrn, w_out_conv, w_o, w_router, router_bias, w_exp_gate, w_exp_up, w_exp_down, w_sh_gate, w_sh_up, w_sh_down, final_norm_g):
    assert w_ada.shape[0] == 1 and lb_logits.shape[0] == 2
    bsz, seq, _ = x_prompt.shape
    n_smp = x_sample.shape[0]
    n_prompt = bsz * seq
    n_tok = n_prompt + n_smp

    w_in_b = w_in[0].astype(BF16)
    w_oh_b = w_out_hgrn[0].astype(BF16)
    w_oc_b = w_out_conv[0].astype(BF16)
    w_o_b = w_o[0].astype(BF16)
    wr_t = w_router[0].T
    wr_hi = wr_t.astype(BF16)
    wr_hl = jnp.concatenate([wr_hi, (wr_t - wr_hi.astype(F32)).astype(BF16)], axis=0)
    w_sgu = jnp.concatenate([w_sh_gate[0], w_sh_up[0]], axis=1).astype(BF16)
    w_sd = w_sh_down[0].astype(BF16)
    gmix = norm_mix_g[0].reshape(1, D_MODEL)
    gffn = norm_ffn_g[0].reshape(1, D_MODEL)
    hg = hgrn_norm_g[0].reshape(1, KEY_W)
    cw = conv_w[0]
    cb = conv_b[0].reshape(1, CONV_W)
    gfin = final_norm_g.reshape(1, D_MODEL)

    mod_p, mod_s = _ada(c_prompt, c_sample, w_ada[0], b_ada[0])

    rbias = router_bias[0].reshape(N_EXP, 1)
    xmid_p, h2_all, idx_p, w_p, rank_p, cnt_p, s_p, cv_p = _mix(
        x_prompt, mod_p.reshape(bsz, 6, D_MODEL), n_tok, gmix, gffn, w_in_b, lb_logits, hg, cw, cb, w_oh_b, w_oc_b, w_o_b, wr_hl, w_sgu, w_sd, rbias)

    xs2 = x_sample.reshape(n_smp, D_MODEL)
    f, kk, q, v, gate, yb, sga, sgb, cv_s = _smp1(
        xs2, mod_s, gmix, w_in_b, lb_logits, cw, cb, state_conv[0].reshape(n_smp, (CONV_K - 1) * CONV_W))
    s_s, o_s = _smp2(f, kk, q, v, state_hgrn[0])
    xmid_s, h2_all, idx, w_tok, rank, cnt = _smp3(xs2, mod_s, o_s, gate, yb, sga, sgb, hg, gffn,
                                                  w_oh_b, w_oc_b, w_o_b, wr_hl, w_sgu, w_sd,
                                                  rbias, cnt_p, idx_p, w_p, rank_p, h2_all)

    bm = GMM_BM
    n_blocks = (n_tok * TOP_K + N_EXP * (bm - 1)) // bm
    n_rows = n_blocks * bm
    counts = cnt[:, 0]
    padded = (counts + bm - 1) // bm * bm
    pad_end = jnp.cumsum(padded)
    pad_start = pad_end - padded
    blk_row0 = jnp.arange(n_blocks, dtype=I32) * bm
    blk_exp = jnp.minimum(jnp.sum((pad_end[None, :] <= blk_row0[:, None]).astype(I32), axis=1), N_EXP - 1)
    n_used = (pad_end[-1:] // bm).astype(I32)
    cuts = [0] + [n_prompt * f // FINAL_SPLIT[-1] for f in FINAL_SPLIT]
    assert all(c % FINAL_TILE == 0 for c in cuts) and cuts[-1] == n_prompt
    chunks = tuple((c0, n_tok if c1 == n_prompt else c1) for c0, c1 in zip(cuts[:-1], cuts[1:]))
    dest, *chunk_dest = _dest(pad_start.astype(I32), idx, rank, n_rows, chunks)

    xs = _dispatch(h2_all.reshape(2 * n_tok, PLANE_W), dest, 2 * n_rows).reshape(2, n_rows, PLANE_W)
    ys = _gmm(blk_exp, n_used, xs, w_exp_gate[0], w_exp_up[0], w_exp_down[0])

    ys_flat = ys.reshape(2 * n_rows, PLANE_W)
    w_t = w_tok
    xmid_p2 = xmid_p.reshape(n_prompt, D_MODEL)
    g2_p = mod_p[:, 5 * D_MODEL:].reshape(bsz, 1, D_MODEL)
    g2_s = mod_s[:, 5 * D_MODEL:].reshape(1, n_smp, D_MODEL)
    y_p = None
    for (c0, c1), dest_c in zip(chunks, chunk_dest):
        z = _combine(ys_flat, dest_c).reshape(TOP_K, 2, c1 - c0, PLANE_W)
        y_p = _final(xmid_p2, g2_p, z, w_t, gfin, FINAL_TILE, n_tiles=(min(c1, n_prompt) - c0) // FINAL_TILE,
                     x_tile0=c0 // FINAL_TILE, z_tile0=0, w_tile0=c0 // FINAL_TILE,
                     tiles_per_g2=seq // FINAL_TILE, y_prev=y_p)
    y_s = _final(xmid_s, g2_s, z, w_t, gfin, n_smp, n_tiles=1, x_tile0=0, z_tile0=(n_prompt - c0) // n_smp,
                 w_tile0=n_prompt // n_smp, tiles_per_g2=1)

    return (y_p.reshape(bsz, seq, D_MODEL), y_s.reshape(n_smp, 1, D_MODEL),
            s_p[None], cv_p[None], s_s[None], cv_s.reshape(1, n_smp, CONV_K - 1, CONV_W))
```

```python
import functools

import jax
import jax.numpy as jnp
from jax import lax
from jax.experimental import pallas as pl
from jax.experimental.pallas import tpu as pltpu
from jax.experimental.pallas import tpu_sc as plsc

F32 = jnp.float32
BF16 = jnp.bfloat16
I32 = jnp.int32

D_MODEL = 1024
HALF_D = D_MODEL // 2
HEADS = 4
DK = 128
KEY_W = HEADS * DK
CONV_W = 512
CONV_K = 3
IN_W = 2 * KEY_W + 2 * KEY_W + 3 * CONV_W + 2 * D_MODEL
N_EXP = 64
TOP_K = 8
N_GRP = 8
GRP_SZ = N_EXP // N_GRP
TOPK_GRP = 4
EXP_FF = 256
SH_FF = 256
ROUTED_SCALE = 2.5
EPS = 1e-6

C_Q, C_F, C_I, C_G = 0, 512, 1024, 1536
C_BB, C_CC, C_VB = 2048, 2560, 3072
C_MGA, C_MGB = 3584, 4608

MIX_TILE = 512
SUB = 256
CHUNK = 64
ROUTE_TILE = 256
GMM_BM = 512
GMM_NX = 4
GMM_NY = 3
FINAL_TILE = 512
FINAL_SPLIT = (4, 12, 22, 32)
SC_WINDOW = 128
PLANE_W = HALF_D // 2
VMEM_LIMIT = 56 * 1024 * 1024


def _dot(a, b):
    return jnp.dot(a, b, preferred_element_type=F32)


def _dot_nt(a, b):
    return lax.dot_general(a, b, (((1,), (1,)), ((), ())), preferred_element_type=F32)


def _dot_tn(a, b):
    return lax.dot_general(a, b, (((0,), (0,)), ((), ())), preferred_element_type=F32)


def _sigmoid(x):
    return 0.5 * jnp.tanh(0.5 * x) + 0.5


def _silu(x):
    h = 0.5 * x
    return h * jnp.tanh(h) + h


def _rms(x):
    return x * lax.rsqrt(jnp.mean(x * x, axis=-1, keepdims=True) + EPS)


def _lower_bound(lbl):
    a, b = lbl[0:1], lbl[1:2]
    m = jnp.maximum(a, b)
    ea, eb = jnp.exp(a - m), jnp.exp(b - m)
    return ea / (ea + eb)


def _split3(x):
    hi = x.astype(BF16)
    r1 = x - hi.astype(F32)
    mid = r1.astype(BF16)
    lo = (r1 - mid.astype(F32)).astype(BF16)
    return hi, mid, lo


def _words(lo_b, hi_b):
    lo = lax.shift_right_logical(lax.bitcast_convert_type(lo_b.astype(F32), I32), 16)
    hi = lax.bitcast_convert_type(hi_b.astype(F32), I32) & jnp.int32(-65536)
    return lo | hi


def _halves(w):
    lo = lax.bitcast_convert_type(lax.shift_left(w, 16), F32)
    hi = lax.bitcast_convert_type(w & jnp.int32(-65536), F32)
    return lo.astype(BF16), hi.astype(BF16)


def _pack_rows(xb, out_ref):
    words = _words(xb[:, :HALF_D], xb[:, HALF_D:])
    out_ref[0] = words[:, :PLANE_W]
    out_ref[1] = words[:, PLANE_W:]


def _unpack_rows(p0, p1):
    c0, c2 = _halves(p0)
    c1, c3 = _halves(p1)
    return c0, c1, c2, c3


def _ada_kernel(cp_ref, cs_ref, w_ref, b_ref, op_ref, os_ref):
    w = w_ref[...].astype(BF16)
    for c_ref, o_ref in ((cp_ref, op_ref), (cs_ref, os_ref)):
        o_ref[...] = _dot(_silu(c_ref[...]).astype(BF16), w) + b_ref[...]


def _ada(c_prompt, c_sample, w_ada, b_ada):
    blk = 1024
    rows = lambda c: pl.BlockSpec((c.shape[0], D_MODEL), lambda j: (0, 0))
    cols = lambda c: pl.BlockSpec((c.shape[0], blk), lambda j: (0, j))
    return pl.pallas_call(
        _ada_kernel,
        grid=(6 * D_MODEL // blk,),
        in_specs=[rows(c_prompt), rows(c_sample),
                  pl.BlockSpec((D_MODEL, blk), lambda j: (0, j)),
                  pl.BlockSpec((1, blk), lambda j: (0, j))],
        out_specs=[cols(c_prompt), cols(c_sample)],
        out_shape=[jax.ShapeDtypeStruct((c.shape[0], 6 * D_MODEL), F32) for c in (c_prompt, c_sample)],
        name="ada",
    )(c_prompt, c_sample, w_ada, b_ada.reshape(1, -1))


def _ffn_pre(x1, mod_rows, gffn, w_sgu, w_sd, wr_hl):
    sh2, sc2, g2 = mod_rows
    h2 = _rms(x1) * gffn * (1.0 + sc2) + sh2
    h2b = h2.astype(BF16)
    gu = _dot(h2b, w_sgu)
    act = _silu(gu[:, :SH_FF]) * gu[:, SH_FF:]
    xmid = x1 + g2 * _dot(act.astype(BF16), w_sd)
    h2lo = (h2 - h2b.astype(F32)).astype(BF16)
    both = _dot_nt(wr_hl, h2b)
    lgt = both[:N_EXP] + both[N_EXP:] + _dot_nt(wr_hl[:N_EXP], h2lo)
    return xmid, h2b, lgt


def _mix_kernel(*refs, nt, n_tiles):
    i = pl.program_id(0)
    rbias_ref, h2_ref = refs[15], refs[17]
    idx_ref, w_ref, rank_ref, cnt_ref, lgt_ref = refs[18], refs[19], refs[20], refs[21], refs[-1]

    @pl.when(i == 0)
    def _():
        lgt_ref[...] = jnp.zeros_like(lgt_ref)
        cnt_ref[...] = jnp.zeros_like(cnt_ref)

    def route_previous(keep):
        carry = cnt_ref[:, 0:1]
        for c0 in range(0, lgt_ref.shape[1], ROUTE_TILE):
            cols = slice(c0, c0 + ROUTE_TILE)
            picks, weights, ranks, carry = _route_tile(lgt_ref[:, cols], rbias_ref[...], carry)
            for k in range(TOP_K):
                idx_ref[k:k + 1, cols] = picks[k]
                w_ref[k:k + 1, cols] = weights[k]
                rank_ref[k:k + 1, cols] = ranks[k]
            yield
        cnt_ref[...] = jnp.broadcast_to(jnp.where(keep, carry, 0.0), cnt_ref.shape)

    @pl.when(i == n_tiles)
    def _():
        h2_ref[...] = jnp.zeros_like(h2_ref)
        for _ in route_previous(True):
            pass

    @pl.when(i < n_tiles)
    def _():
        _mix_tile(i % nt, nt, route_previous(i > 0), *refs)


def _mix_tile(t, nt, routing, x_ref, mod_ref, gmix_ref, gffn_ref, w_in_ref, lbl_ref, hg_ref, cw_ref, cb_ref,
              w_oh_ref, w_oc_ref, w_o_ref, wr_hl_ref, w_sgu_ref, w_sd_ref, rbias_ref,
              xmid_ref, h2_ref, idx_ref, w_ref, rank_ref, cnt_ref, s_out_ref, cv_out_ref,
              proj_ref, st_ref, cbuf_ref, ya_ref, lgt_ref):
    del rbias_ref, idx_ref, w_ref, rank_ref, cnt_ref
    tt = x_ref.shape[1]

    @pl.when(t == 0)
    def _():
        st_ref[...] = jnp.zeros_like(st_ref)
        cbuf_ref[...] = jnp.zeros_like(cbuf_ref)

    x = x_ref[0]
    mod = mod_ref[0]
    sh1, sc1, g1 = mod[0:1], mod[1:2], mod[2:3]
    h = _rms(x) * gmix_ref[...] * (1.0 + sc1) + sh1
    hb = h.astype(BF16)
    for c in range(0, IN_W, 512):
        proj_ref[:, c:c + 512] = _dot(hb, w_in_ref[:, c:c + 512])
        if c % 2048 == 512:
            next(routing, None)
    for _ in routing:
        pass

    lb = _lower_bound(lbl_ref[...])
    row = lax.broadcasted_iota(I32, (SUB, SUB), 0)
    col = lax.broadcasted_iota(I32, (SUB, SUB), 1)
    tri = (col <= row).astype(BF16)
    mask_d = (col <= row) & (row // CHUNK == col // CHUNK)
    mask_a = row // (2 * CHUNK) == col // (2 * CHUNK)
    n_ch = SUB // CHUNK

    def by_chunk(vals):
        return jnp.concatenate([jnp.zeros((CHUNK, DK), F32) if v is None
                                else jnp.broadcast_to(v, (CHUNK, DK)) for v in vals], axis=0)

    for s in range(tt // SUB):
        r0 = s * SUB
        f = lb + (1.0 - lb) * _sigmoid(proj_ref[r0:r0 + SUB, C_F:C_F + KEY_W])
        kk = 1.0 - f
        hi, mid, lo = _split3(jnp.log(f))
        bc = _dot(tri, hi) + _dot(tri, mid) + _dot(tri, lo)
        for hd in range(HEADS):
            hs = slice(hd * DK, (hd + 1) * DK)
            bh = bc[:, hs]
            at = lambda r: bh[r:r + 1]
            mids = [at(c * CHUNK + CHUNK // 2 - 1) for c in range(n_ch)]
            pair_mid = [at(CHUNK - 1), at(3 * CHUNK - 1)]
            step_mid, step_end = at(2 * CHUNK - 1), at(SUB - 1)
            arg = bh - by_chunk(mids)
            e_pos, e_neg = jnp.exp(arg), jnp.exp(-arg)
            q = _silu(proj_ref[r0:r0 + SUB, C_Q + hd * DK:C_Q + (hd + 1) * DK])
            v = proj_ref[r0:r0 + SUB, C_I + hd * DK:C_I + (hd + 1) * DK]
            qd = q * e_pos
            kd = kk[:, hs] * e_neg
            q_in = qd * by_chunk([jnp.exp(m) for m in mids])
            k_end = kd * by_chunk([jnp.exp(step_end - m) for m in mids])
            qa = qd * by_chunk([None, jnp.exp(mids[1] - pair_mid[0]), None, jnp.exp(mids[3] - pair_mid[1])])
            ka = kd * by_chunk([jnp.exp(pair_mid[0] - mids[0]), None, jnp.exp(pair_mid[1] - mids[2]), None])
            qb = qd * by_chunk([None, None, jnp.exp(mids[2] - step_mid), jnp.exp(mids[3] - step_mid)])
            kb = kd * by_chunk([jnp.exp(step_mid - mids[0]), jnp.exp(step_mid - mids[1]), None, None])
            att = jnp.where(mask_d, _dot_nt(qd.astype(BF16), kd.astype(BF16)), 0.0)
            att = att + jnp.where(mask_a, _dot_nt(qa.astype(BF16), ka.astype(BF16)), 0.0)
            att = att + _dot_nt(qb.astype(BF16), kb.astype(BF16))
            vb = v.astype(BF16)
            st = st_ref[hd]
            o = _dot(att.astype(BF16), vb) + _dot_nt(q_in.astype(BF16), st.astype(BF16))
            st_ref[hd] = st * jnp.exp(step_end) + _dot_tn(vb, k_end.astype(BF16))
            gate = _silu(proj_ref[r0:r0 + SUB, C_G + hd * DK:C_G + (hd + 1) * DK])
            ya_ref[r0:r0 + SUB, hs] = _rms(o) * hg_ref[:, hs] * gate

    u = proj_ref[:, C_CC:C_CC + CONV_W] * proj_ref[:, C_VB:C_VB + CONV_W]
    rows = lax.broadcasted_iota(I32, (tt, CONV_W), 0)
    c0, c1 = cbuf_ref[0:1], cbuf_ref[1:2]
    u1 = jnp.where(rows == 0, c1, pltpu.roll(u, 1, axis=0))
    u2 = jnp.where(rows == 0, c0, jnp.where(rows == 1, c1, pltpu.roll(u, 2, axis=0)))
    conv = cw_ref[0:1] * u2 + cw_ref[1:2] * u1 + cw_ref[2:3] * u + cb_ref[...]
    yb = proj_ref[:, C_BB:C_BB + CONV_W] * conv
    cbuf_ref[...] = u[tt - 2:tt]

    mixed = (_sigmoid(proj_ref[:, C_MGA:C_MGA + D_MODEL]) * _dot(ya_ref[...].astype(BF16), w_oh_ref[...])
             + _sigmoid(proj_ref[:, C_MGB:C_MGB + D_MODEL]) * _dot(yb.astype(BF16), w_oc_ref[...]))
    x1 = x + g1 * _dot(mixed.astype(BF16), w_o_ref[...])

    xmid, h2b, lgt = _ffn_pre(x1, (mod[3:4], mod[4:5], mod[5:6]), gffn_ref[...],
                              w_sgu_ref[...], w_sd_ref[...], wr_hl_ref[...])
    xmid_ref[0] = xmid
    _pack_rows(h2b, h2_ref)
    lgt_ref[...] = lgt

    @pl.when(t == nt - 1)
    def _():
        for hd in range(HEADS):
            s_out_ref[0, hd] = st_ref[hd].T
        cv_out_ref[0] = cbuf_ref[...]


def _const_spec(shape):
    nd = len(shape)
    return pl.BlockSpec(shape, lambda i, _nd=nd: (0,) * _nd, pipeline_mode=pl.Buffered(1))


def _mix(x, mod, n_tok, gmix, gffn, w_in, lbl, hg, cw, cb, w_oh, w_oc, w_o, wr_hl, w_sgu, w_sd, rbias):
    bsz, seq, _ = x.shape
    tt = MIX_TILE
    nt = seq // tt
    n_tiles = bsz * nt
    assert n_tiles * tt < n_tok <= (n_tiles + 1) * tt and tt % ROUTE_TILE == 0
    consts = [gmix, gffn, w_in, lbl, hg, cw, cb, w_oh, w_oc, w_o, wr_hl, w_sgu, w_sd, rbias]
    tile = lambda i: jnp.minimum(i, n_tiles - 1)
    routed = pl.BlockSpec((TOP_K, tt), lambda i: (0, jnp.maximum(i - 1, 0)))
    slot = lambda dt: jax.ShapeDtypeStruct((TOP_K, n_tiles * tt), dt)
    return pl.pallas_call(
        functools.partial(_mix_kernel, nt=nt, n_tiles=n_tiles),
        grid=(n_tiles + 1,),
        in_specs=[pl.BlockSpec((1, tt, D_MODEL), lambda i: (tile(i) // nt, tile(i) % nt, 0)),
                  pl.BlockSpec((1, 6, D_MODEL), lambda i: (tile(i) // nt, 0, 0))]
                 + [_const_spec(a.shape) for a in consts],
        out_specs=[pl.BlockSpec((1, tt, D_MODEL), lambda i: (tile(i) // nt, tile(i) % nt, 0)),
                   pl.BlockSpec((2, tt, PLANE_W), lambda i: (0, i, 0)),
                   routed, routed, routed,
                   pl.BlockSpec((N_EXP, 128), lambda i: (0, 0)),
                   pl.BlockSpec((1, HEADS, DK, DK), lambda i: (tile(i) // nt, 0, 0, 0)),
                   pl.BlockSpec((1, CONV_K - 1, CONV_W), lambda i: (tile(i) // nt, 0, 0))],
        out_shape=[jax.ShapeDtypeStruct((bsz, seq, D_MODEL), F32),
                   jax.ShapeDtypeStruct((2, n_tok, PLANE_W), I32),
                   slot(I32), slot(F32), slot(I32),
                   jax.ShapeDtypeStruct((N_EXP, 128), F32),
                   jax.ShapeDtypeStruct((bsz, HEADS, DK, DK), F32),
                   jax.ShapeDtypeStruct((bsz, CONV_K - 1, CONV_W), F32)],
        scratch_shapes=[pltpu.VMEM((tt, IN_W), F32),
                        pltpu.VMEM((HEADS, DK, DK), F32),
                        pltpu.VMEM((CONV_K - 1, CONV_W), F32),
                        pltpu.VMEM((tt, KEY_W), F32),
                        pltpu.VMEM((N_EXP, tt), F32)],
        compiler_params=pltpu.CompilerParams(
            dimension_semantics=("arbitrary",), vmem_limit_bytes=VMEM_LIMIT),
        name="mix",
    )(x, mod, *consts)


def _smp1_kernel(x_ref, mod_ref, gmix_ref, w_in_ref, lbl_ref, cw_ref, cb_ref, cst_ref,
                 f_ref, k_ref, q_ref, v_ref, gate_ref, yb_ref, sga_ref, sgb_ref, cv_out_ref):
    x = x_ref[...]
    sh1, sc1 = mod_ref[:, 0:D_MODEL], mod_ref[:, D_MODEL:2 * D_MODEL]
    h = _rms(x) * gmix_ref[...] * (1.0 + sc1) + sh1
    hb = h.astype(BF16)

    def proj(c, w):
        return _dot(hb, w_in_ref[:, c:c + w])

    lb = _lower_bound(lbl_ref[...])
    f = lb + (1.0 - lb) * _sigmoid(proj(C_F, KEY_W))
    f_ref[...] = f
    k_ref[...] = 1.0 - f
    q_ref[...] = _silu(proj(C_Q, KEY_W))
    v_ref[...] = proj(C_I, KEY_W)
    gate_ref[...] = _silu(proj(C_G, KEY_W))
    u = proj(C_CC, CONV_W) * proj(C_VB, CONV_W)
    c0, c1 = cst_ref[:, 0:CONV_W], cst_ref[:, CONV_W:2 * CONV_W]
    conv = cw_ref[0:1] * c0 + cw_ref[1:2] * c1 + cw_ref[2:3] * u + cb_ref[...]
    yb_ref[...] = proj(C_BB, CONV_W) * conv
    cv_out_ref[:, 0:CONV_W] = c1
    cv_out_ref[:, CONV_W:2 * CONV_W] = u
    sga_ref[...] = _sigmoid(proj(C_MGA, D_MODEL))
    sgb_ref[...] = _sigmoid(proj(C_MGB, D_MODEL))


def _smp1(x, mod, gmix, w_in, lbl, cw, cb, cst):
    n = x.shape[0]
    kw = jax.ShapeDtypeStruct((n, KEY_W), F32)
    dm = jax.ShapeDtypeStruct((n, D_MODEL), F32)
    return pl.pallas_call(
        _smp1_kernel,
        out_shape=[kw, kw, kw, kw, kw, kw, dm, dm,
                   jax.ShapeDtypeStruct((n, (CONV_K - 1) * CONV_W), F32)],
        compiler_params=pltpu.CompilerParams(vmem_limit_bytes=VMEM_LIMIT),
        name="smp1",
    )(x, mod, gmix, w_in, lbl, cw, cb, cst)


def _smp2(f, k, q, v, state):
    n = f.shape[0]
    info = plsc.get_sparse_core_info()
    lanes = info.num_lanes
    n_workers = info.num_cores * info.num_subcores
    tok_per = n // n_workers
    n_chunks = DK // lanes
    vec = lambda: pltpu.VMEM((DK,), F32)

    @pl.kernel(out_type=[jax.ShapeDtypeStruct(state.shape, F32), jax.ShapeDtypeStruct((n, KEY_W), F32)],
               mesh=_sc_mesh(), scratch_types=[pltpu.VMEM((DK, DK), F32), vec(), vec(), vec(), vec(), vec()],
               compiler_params=pltpu.CompilerParams(needs_layout_passes=False), name="smp2")
    def run(f_hbm, k_hbm, q_hbm, v_hbm, s_hbm, s_out_hbm, o_hbm, s_v, f_v, k_v, q_v, v_v, o_v):
        wid = lax.axis_index("subcore") * info.num_cores + lax.axis_index("core")

        def tile(j, carry):
            t = wid * tok_per + j // HEADS
            hcols = pl.ds((j % HEADS) * DK, DK)
            pltpu.sync_copy(s_hbm.at[t, j % HEADS], s_v)
            for src, dst in ((f_hbm, f_v), (k_hbm, k_v), (q_hbm, q_v), (v_hbm, v_v)):
                pltpu.sync_copy(src.at[t, hcols], dst)
            v_chunks = [v_v[pl.ds(c * lanes, lanes)] for c in range(n_chunks)]

            def row(d, acc):
                at = [jnp.zeros((lanes,), I32) + d]
                fd, kd, qd = (plsc.load_gather(r, at) for r in (f_v, k_v, q_v))
                out = []
                for c in range(n_chunks):
                    cols = pl.ds(c * lanes, lanes)
                    new = s_v[d, cols] * fd + kd * v_chunks[c]
                    s_v[d, cols] = new
                    out.append(acc[c] + qd * new)
                return tuple(out)

            acc = lax.fori_loop(0, DK, row, tuple(jnp.zeros((lanes,), F32) for _ in range(n_chunks)))
            for c in range(n_chunks):
                o_v[pl.ds(c * lanes, lanes)] = acc[c]
            pltpu.sync_copy(s_v, s_out_hbm.at[t, j % HEADS])
            pltpu.sync_copy(o_v, o_hbm.at[t, hcols])
            return carry

        lax.fori_loop(0, tok_per * HEADS, tile, 0)

    return run(f, k, q, v, state)


def _smp3_kernel(x_ref, mod_ref, o_ref, gate_ref, yb_ref, sga_ref, sgb_ref, hg_ref, gffn_ref,
                 w_oh_ref, w_oc_ref, w_o_ref, wr_hl_ref, w_sgu_ref, w_sd_ref,
                 rbias_ref, cnt_p_ref, idx_p_ref, w_p_ref, rank_p_ref, h2_all_ref,
                 xmid_ref, h2_ref, idx_ref, w_ref, rank_ref, cnt_ref):
    del h2_all_ref
    parts = []
    for hd in range(HEADS):
        hs = slice(hd * DK, (hd + 1) * DK)
        parts.append(_rms(o_ref[:, hs]) * hg_ref[:, hs] * gate_ref[:, hs])
    ya = jnp.concatenate(parts, axis=1)
    mixed = (sga_ref[...] * _dot(ya.astype(BF16), w_oh_ref[...])
             + sgb_ref[...] * _dot(yb_ref[...].astype(BF16), w_oc_ref[...]))
    g1 = mod_ref[:, 2 * D_MODEL:3 * D_MODEL]
    x1 = x_ref[...] + g1 * _dot(mixed.astype(BF16), w_o_ref[...])
    mod_rows = tuple(mod_ref[:, j * D_MODEL:(j + 1) * D_MODEL] for j in (3, 4, 5))
    xmid, h2b, lgt = _ffn_pre(x1, mod_rows, gffn_ref[...], w_sgu_ref[...], w_sd_ref[...],
                              wr_hl_ref[...])
    xmid_ref[...] = xmid
    _pack_rows(h2b, h2_ref)
    n_prompt = idx_p_ref.shape[1]
    picks, weights, ranks, total = _route_tile(lgt, rbias_ref[...], cnt_p_ref[:, 0:1])
    for src, dst, new in ((idx_p_ref, idx_ref, picks), (w_p_ref, w_ref, weights), (rank_p_ref, rank_ref, ranks)):
        dst[:, :n_prompt] = src[...]
        for k in range(TOP_K):
            dst[k:k + 1, n_prompt:] = new[k]
    cnt_ref[...] = jnp.broadcast_to(total, cnt_ref.shape).astype(I32)


def _smp3(x, mod, o, gate, yb, sga, sgb, hg, gffn, w_oh, w_oc, w_o, wr_hl, w_sgu, w_sd,
          rbias, cnt_p, idx_p, w_p, rank_p, h2_all):
    n = x.shape[0]
    vmem_args = [x, mod, o, gate, yb, sga, sgb, hg, gffn, w_oh, w_oc, w_o, wr_hl, w_sgu, w_sd,
                 rbias, cnt_p, idx_p, w_p, rank_p]
    n_prompt = idx_p.shape[1]
    blk = n_prompt // n
    slot = lambda dt: jax.ShapeDtypeStruct((TOP_K, n_prompt + n), dt)

    def full(a):
        nd = a.ndim
        return pl.BlockSpec(a.shape, lambda i, _nd=nd: (0,) * _nd)

    return pl.pallas_call(
        _smp3_kernel,
        grid=(1,),
        in_specs=[full(a) for a in vmem_args] + [pl.BlockSpec(memory_space=pl.ANY)],
        out_specs=[pl.BlockSpec((n, D_MODEL), lambda i: (0, 0)),
                   pl.BlockSpec((2, n, PLANE_W), lambda i: (0, blk, 0)),
                   pl.BlockSpec((TOP_K, n_prompt + n), lambda i: (0, 0)),
                   pl.BlockSpec((TOP_K, n_prompt + n), lambda i: (0, 0)),
                   pl.BlockSpec((TOP_K, n_prompt + n), lambda i: (0, 0)),
                   pl.BlockSpec((N_EXP, 128), lambda i: (0, 0))],
        out_shape=[jax.ShapeDtypeStruct((n, D_MODEL), F32),
                   jax.ShapeDtypeStruct(h2_all.shape, h2_all.dtype),
                   slot(I32), slot(F32), slot(I32),
                   jax.ShapeDtypeStruct((N_EXP, 128), I32)],
        input_output_aliases={len(vmem_args): 1},
        compiler_params=pltpu.CompilerParams(
            dimension_semantics=("arbitrary",), vmem_limit_bytes=VMEM_LIMIT),
        name="smp3",
    )(*vmem_args, h2_all)


def _route_tile(lgt, bias, carry):
    tr = lgt.shape[1]
    neg = -jnp.inf
    scores = _sigmoid(lgt)
    sel = scores + bias
    j8 = lax.broadcasted_iota(I32, (GRP_SZ, tr), 0)
    groups = [sel[g * GRP_SZ:(g + 1) * GRP_SZ] for g in range(N_GRP)]
    gscore = []
    for grp in groups:
        m1 = jnp.max(grp, axis=0, keepdims=True)
        i1 = jnp.min(jnp.where(grp == m1, j8, GRP_SZ), axis=0, keepdims=True)
        m2 = jnp.max(jnp.where(j8 == i1, neg, grp), axis=0, keepdims=True)
        gscore.append(m1 + m2)
    kept = []
    for g in range(N_GRP):
        beaten = jnp.zeros((1, tr), I32)
        for o in range(N_GRP):
            if o < g:
                beaten = beaten + (gscore[o] >= gscore[g]).astype(I32)
            elif o > g:
                beaten = beaten + (gscore[o] > gscore[g]).astype(I32)
        kept.append(jnp.where(beaten < TOPK_GRP, groups[g], neg))
    masked = jnp.concatenate(kept, axis=0)
    ei = lax.broadcasted_iota(I32, masked.shape, 0)
    chosen = jnp.zeros(masked.shape, jnp.bool_)
    picks, weights = [], []
    for _ in range(TOP_K):
        m = jnp.max(masked, axis=0, keepdims=True)
        pick = jnp.min(jnp.where((masked == m) & ~chosen, ei, N_EXP), axis=0, keepdims=True)
        hit = ei == pick
        weights.append(jnp.sum(jnp.where(hit, scores, 0.0), axis=0, keepdims=True))
        picks.append(pick)
        chosen = chosen | hit
        masked = jnp.where(hit, neg, masked)
    wsum = weights[0]
    for w in weights[1:]:
        wsum = wsum + w
    sel01 = chosen.astype(F32)
    r = lax.broadcasted_iota(I32, (tr, tr), 0)
    c = lax.broadcasted_iota(I32, (tr, tr), 1)
    before = (r < c).astype(BF16)
    cnt = _dot(sel01.astype(BF16), before) + carry
    weights = [w / wsum * ROUTED_SCALE for w in weights]
    ranks = [jnp.sum(jnp.where(ei == p, cnt, 0.0), axis=0, keepdims=True).astype(I32) for p in picks]
    return picks, weights, ranks, carry + jnp.sum(sel01, axis=1, keepdims=True)


def _dest_kernel(start_ref, idx_ref, rank_ref, all_ref, *chunk_refs, n_rows, chunks):
    n_tok = idx_ref.shape[1]
    idx = idx_ref[...]
    acc = rank_ref[...]
    for e in range(N_EXP):
        acc = acc + jnp.where(idx == e, start_ref[e], 0)
    for k in range(TOP_K):
        for p in range(2):
            row = acc[k:k + 1, :] + p * n_rows
            seg = 2 * k + p
            all_ref[:, seg * n_tok:(seg + 1) * n_tok] = row
            for (c0, c1), ref in zip(chunks, chunk_refs):
                ref[:, seg * (c1 - c0):(seg + 1) * (c1 - c0)] = row[:, c0:c1]


def _dest(pad_start, idx, rank, n_rows, chunks):
    k, n_tok = idx.shape
    vmem = pl.BlockSpec(memory_space=pltpu.VMEM)
    sizes = [n_tok] + [c1 - c0 for c0, c1 in chunks]
    return pl.pallas_call(
        functools.partial(_dest_kernel, n_rows=n_rows, chunks=chunks),
        in_specs=[pl.BlockSpec(memory_space=pltpu.SMEM), vmem, vmem],
        out_specs=[vmem] * len(sizes),
        out_shape=[jax.ShapeDtypeStruct((1, 2 * k * n), I32) for n in sizes],
        name="dest",
    )(pad_start, idx, rank)


def _sc_mesh():
    return plsc.VectorSubcoreMesh(core_axis_name="core", subcore_axis_name="subcore")


def _dispatch(rows, dest, n_out):
    n, width = rows.shape
    win = SC_WINDOW
    steps = n // win

    @pl.kernel(out_type=jax.ShapeDtypeStruct((n_out, width), rows.dtype), mesh=_sc_mesh(),
               scratch_types=[], name="dispatch")
    def run(x_hbm, *refs):
        i_hbms, o_hbm = refs[:TOP_K], refs[TOP_K]

        def body(x_vmem, *i_vmems):
            for i_vmem in i_vmems:
                pltpu.sync_copy(x_vmem, o_hbm.at[i_vmem.at[0]])

        pltpu.emit_pipeline(
            body,
            grid=(steps,),
            in_specs=[pl.BlockSpec((win, width), lambda i: (i, 0))]
                     + [pl.BlockSpec((1, win), lambda i, k=k: (0, k * steps + i)) for k in range(TOP_K)],
            out_specs=[],
            core_axis_name=("core", "subcore"),
            dimension_semantics=(pltpu.PARALLEL,),
        )(x_hbm, *i_hbms)

    assert dest.shape == (1, TOP_K * n)
    return run(rows, *([dest] * TOP_K))


def _combine(rows, dest_flat):
    width = rows.shape[1]
    n = dest_flat.shape[1]
    win = SC_WINDOW

    @pl.kernel(out_type=jax.ShapeDtypeStruct((n, width), rows.dtype), mesh=_sc_mesh(),
               scratch_types=[], name="combine")
    def run(y_hbm, i_hbm, o_hbm):
        def body(i_vmem, o_vmem):
            pltpu.sync_copy(y_hbm.at[i_vmem.at[0]], o_vmem)

        pltpu.emit_pipeline(
            body,
            grid=(n // win,),
            in_specs=[pl.BlockSpec((1, win), lambda i: (0, i))],
            out_specs=[pl.BlockSpec((win, width), lambda i: (i, 0))],
            core_axis_name=("core", "subcore"),
            dimension_semantics=(pltpu.PARALLEL,),
        )(i_hbm, o_hbm)

    return run(rows, dest_flat)


def _gmm_kernel(blk_exp_ref, n_used_ref, xs_hbm, wg_hbm, wu_hbm, wd_hbm, ys_hbm,
                xbuf, ybuf, wg32, wu32, wd32, wgu_b, wd_b, xsem, ysem, wsem, run_ref):
    nx, ny, bm = xbuf.shape[0], ybuf.shape[0], xbuf.shape[2]
    n_used = n_used_ref[0]

    def x_copies(b):
        rows, slot = pl.ds(b * bm, bm), b % nx
        return [pltpu.make_async_copy(xs_hbm.at[p, rows, :], xbuf.at[slot, p], xsem.at[slot, p]) for p in range(2)]

    def y_copies(b):
        rows, slot = pl.ds(b * bm, bm), b % ny
        return [pltpu.make_async_copy(ybuf.at[slot, p], ys_hbm.at[p, rows, :], ysem.at[slot, p]) for p in range(2)]

    def start(copies):
        for c in copies:
            c.start()

    def wait(copies):
        for c in copies:
            c.wait()

    def w_copies(e, slot):
        return (pltpu.make_async_copy(wg_hbm.at[e], wg32.at[slot], wsem.at[slot, 0]),
                pltpu.make_async_copy(wu_hbm.at[e], wu32.at[slot], wsem.at[slot, 1]),
                pltpu.make_async_copy(wd_hbm.at[e], wd32.at[slot], wsem.at[slot, 2]))

    def run_end(b):
        return lax.while_loop(lambda j: (j < n_used) & (blk_exp_ref[jnp.minimum(j, n_used - 1)] == blk_exp_ref[b]),
                              lambda j: j + 1, b + 1)

    run_ref[0] = 0
    start(x_copies(0))
    start(w_copies(blk_exp_ref[0], 0))
    for j in range(1, nx - 1):
        @pl.when(j < n_used)
        def _():
            start(x_copies(j))

    def block(b, carry):
        @pl.when(b + nx - 1 < n_used)
        def _():
            start(x_copies(b + nx - 1))

        @pl.when((b == 0) | (blk_exp_ref[b] != blk_exp_ref[jnp.maximum(b - 1, 0)]))
        def _():
            wslot = run_ref[0] % 2
            run_ref[0] = run_ref[0] + 1
            wait(w_copies(blk_exp_ref[b], wslot))
            wgu_b[:, 0:EXP_FF] = wg32[wslot].astype(BF16)
            wgu_b[:, EXP_FF:2 * EXP_FF] = wu32[wslot].astype(BF16)
            wd_b[...] = wd32[wslot].astype(BF16)
            nxt = run_end(b)

            @pl.when(nxt < n_used)
            def _():
                start(w_copies(blk_exp_ref[jnp.minimum(nxt, n_used - 1)], 1 - wslot))

        wait(x_copies(b))

        @pl.when(b >= ny)
        def _():
            wait(y_copies(b - ny))

        xslot = b % nx
        xc = _unpack_rows(xbuf[xslot, 0], xbuf[xslot, 1])
        gu = sum(_dot(c, wgu_b[i * PLANE_W:(i + 1) * PLANE_W, :]) for i, c in enumerate(xc))
        act = (_silu(gu[:, :EXP_FF]) * gu[:, EXP_FF:]).astype(BF16)
        _pack_rows(_dot(act, wd_b[...]).astype(BF16), ybuf.at[b % ny])
        start(y_copies(b))
        return carry

    lax.fori_loop(0, n_used, block, 0)

    for j in range(ny, 0, -1):
        @pl.when(n_used >= j)
        def _():
            wait(y_copies(n_used - j))


def _gmm(blk_exp, n_used, xs, w_gate, w_up, w_down):
    n_rows = xs.shape[1]
    bm = GMM_BM
    nb = n_rows // bm

    assert blk_exp.shape == (nb,)
    any_spec = pl.BlockSpec(memory_space=pl.ANY)
    grid_spec = pltpu.PrefetchScalarGridSpec(
        num_scalar_prefetch=2,
        grid=(1,),
        in_specs=[any_spec, any_spec, any_spec, any_spec],
        out_specs=any_spec,
        scratch_shapes=[pltpu.VMEM((GMM_NX, 2, bm, PLANE_W), I32), pltpu.VMEM((GMM_NY, 2, bm, PLANE_W), I32),
                        pltpu.VMEM((2, D_MODEL, EXP_FF), F32), pltpu.VMEM((2, D_MODEL, EXP_FF), F32),
                        pltpu.VMEM((2, EXP_FF, D_MODEL), F32),
                        pltpu.VMEM((D_MODEL, 2 * EXP_FF), BF16), pltpu.VMEM((EXP_FF, D_MODEL), BF16),
                        pltpu.SemaphoreType.DMA((GMM_NX, 2)), pltpu.SemaphoreType.DMA((GMM_NY, 2)),
                        pltpu.SemaphoreType.DMA((2, 3)), pltpu.SMEM((1,), I32)],
    )
    return pl.pallas_call(
        _gmm_kernel,
        grid_spec=grid_spec,
        out_shape=jax.ShapeDtypeStruct((2, n_rows, PLANE_W), I32),
        compiler_params=pltpu.CompilerParams(dimension_semantics=("arbitrary",)),
        name="gmm",
    )(blk_exp, n_used, xs, w_gate, w_up, w_down)


def _final_kernel(xmid_ref, g2_ref, z_ref, w_ref, gfin_ref, *rest):
    y_ref = rest[-1]
    accs = [jnp.zeros((xmid_ref.shape[0], PLANE_W), F32) for _ in range(4)]
    w_cols = w_ref[...].T
    for k in range(TOP_K):
        wk = w_cols[:, k:k + 1]
        cols = _unpack_rows(z_ref[k, 0], z_ref[k, 1])
        accs = [a + wk * c.astype(F32) for a, c in zip(accs, cols)]
    acc = jnp.concatenate(accs, axis=1)
    y_ref[...] = _rms(xmid_ref[...] + g2_ref[0] * acc) * gfin_ref[...]


def _final(xmid, g2, z, w_t, gfin, tile, *, n_tiles, x_tile0, z_tile0, w_tile0, tiles_per_g2, y_prev=None):
    args = [xmid, g2, z, w_t, gfin]
    in_specs = [pl.BlockSpec((tile, D_MODEL), lambda i: (x_tile0 + i, 0)),
                pl.BlockSpec((1, g2.shape[1], D_MODEL), lambda i: ((x_tile0 + i) // tiles_per_g2, 0, 0)),
                pl.BlockSpec((TOP_K, 2, tile, PLANE_W), lambda i: (0, 0, z_tile0 + i, 0)),
                pl.BlockSpec((TOP_K, tile), lambda i: (0, w_tile0 + i)),
                pl.BlockSpec((1, D_MODEL), lambda i: (0, 0))]
    aliases = {}
    if y_prev is not None:
        args.append(y_prev)
        in_specs.append(pl.BlockSpec(memory_space=pl.ANY))
        aliases = {len(args) - 1: 0}
    return pl.pallas_call(
        _final_kernel,
        grid=(n_tiles,),
        in_specs=in_specs,
        out_specs=pl.BlockSpec((tile, D_MODEL), lambda i: (x_tile0 + i, 0)),
        out_shape=jax.ShapeDtypeStruct(xmid.shape, F32),
        input_output_aliases=aliases,
        compiler_params=pltpu.CompilerParams(dimension_semantics=("arbitrary",)),
        name="final",
    )(*args)


def kernel(x_prompt, x_sample, state_hgrn, state_conv, c_prompt, c_sample, w_ada, b_ada, norm_mix_g, norm_ffn_g, w_in, lb_logits, hgrn_norm_g, conv_w, conv_b, w_out_hgrn, w_out_conv, w_o, w_router, router_bias, w_exp_gate, w_exp_up, w_exp_down, w_sh_gate, w_sh_up, w_sh_down, final_norm_g):
    assert w_ada.shape[0] == 1 and lb_logits.shape[0] == 2
    bsz, seq, _ = x_prompt.shape
    n_smp = x_sample.shape[0]
    n_prompt = bsz * seq
    n_tok = n_prompt + n_smp

    w_in_b = w_in[0].astype(BF16)
    w_oh_b = w_out_hgrn[0].astype(BF16)
    w_oc_b = w_out_conv[0].astype(BF16)
    w_o_b = w_o[0].astype(BF16)
    wr_t = w_router[0].T
    wr_hi = wr_t.astype(BF16)
    wr_hl = jnp.concatenate([wr_hi, (wr_t - wr_hi.astype(F32)).astype(BF16)], axis=0)
    w_sgu = jnp.concatenate([w_sh_gate[0], w_sh_up[0]], axis=1).astype(BF16)
    w_sd = w_sh_down[0].astype(BF16)
    gmix = norm_mix_g[0].reshape(1, D_MODEL)
    gffn = norm_ffn_g[0].reshape(1, D_MODEL)
    hg = hgrn_norm_g[0].reshape(1, KEY_W)
    cw = conv_w[0]
    cb = conv_b[0].reshape(1, CONV_W)
    gfin = final_norm_g.reshape(1, D_MODEL)

    mod_p, mod_s = _ada(c_prompt, c_sample, w_ada[0], b_ada[0])

    rbias = router_bias[0].reshape(N_EXP, 1)
    xmid_p, h2_all, idx_p, w_p, rank_p, cnt_p, s_p, cv_p = _mix(
        x_prompt, mod_p.reshape(bsz, 6, D_MODEL), n_tok, gmix, gffn, w_in_b, lb_logits, hg, cw, cb, w_oh_b, w_oc_b, w_o_b, wr_hl, w_sgu, w_sd, rbias)

    xs2 = x_sample.reshape(n_smp, D_MODEL)
    f, kk, q, v, gate, yb, sga, sgb, cv_s = _smp1(
        xs2, mod_s, gmix, w_in_b, lb_logits, cw, cb, state_conv[0].reshape(n_smp, (CONV_K - 1) * CONV_W))
    s_s, o_s = _smp2(f, kk, q, v, state_hgrn[0])
    xmid_s, h2_all, idx, w_tok, rank, cnt = _smp3(xs2, mod_s, o_s, gate, yb, sga, sgb, hg, gffn,
                                                  w_oh_b, w_oc_b, w_o_b, wr_hl, w_sgu, w_sd,
                                                  rbias, cnt_p, idx_p, w_p, rank_p, h2_all)

    bm = GMM_BM
    n_blocks = (n_tok * TOP_K + N_EXP * (bm - 1)) // bm
    n_rows = n_blocks * bm
    counts = cnt[:, 0]
    padded = (counts + bm - 1) // bm * bm
    pad_end = jnp.cumsum(padded)
    pad_start = pad_end - padded
    blk_row0 = jnp.arange(n_blocks, dtype=I32) * bm
    blk_exp = jnp.minimum(jnp.sum((pad_end[None, :] <= blk_row0[:, None]).astype(I32), axis=1), N_EXP - 1)
    n_used = (pad_end[-1:] // bm).astype(I32)
    cuts = [0] + [n_prompt * f // FINAL_SPLIT[-1] for f in FINAL_SPLIT]
    assert all(c % FINAL_TILE == 0 for c in cuts) and cuts[-1] == n_prompt
    chunks = tuple((c0, n_tok if c1 == n_prompt else c1) for c0, c1 in zip(cuts[:-1], cuts[1:]))
    dest, *chunk_dest = _dest(pad_start.astype(I32), idx, rank, n_rows, chunks)

    xs = _dispatch(h2_all.reshape(2 * n_tok, PLANE_W), dest, 2 * n_rows).reshape(2, n_rows, PLANE_W)
    ys = _gmm(blk_exp, n_used, xs, w_exp_gate[0], w_exp_up[0], w_exp_down[0])

    ys_flat = ys.reshape(2 * n_rows, PLANE_W)
    w_t = w_tok
    xmid_p2 = xmid_p.reshape(n_prompt, D_MODEL)
    g2_p = mod_p[:, 5 * D_MODEL:].reshape(bsz, 1, D_MODEL)
    g2_s = mod_s[:, 5 * D_MODEL:].reshape(1, n_smp, D_MODEL)
    y_p = None
    for (c0, c1), dest_c in zip(chunks, chunk_dest):
        z = _combine(ys_flat, dest_c).reshape(TOP_K, 2, c1 - c0, PLANE_W)
        y_p = _final(xmid_p2, g2_p, z, w_t, gfin, FINAL_TILE, n_tiles=(min(c1, n_prompt) - c0) // FINAL_TILE,
                     x_tile0=c0 // FINAL_TILE, z_tile0=0, w_tile0=c0 // FINAL_TILE,
                     tiles_per_g2=seq // FINAL_TILE, y_prev=y_p)
    y_s = _final(xmid_s, g2_s, z, w_t, gfin, n_smp, n_tiles=1, x_tile0=0, z_tile0=(n_prompt - c0) // n_smp,
                 w_tile0=n_prompt // n_smp, tiles_per_g2=1)

    return (y_p.reshape(bsz, seq, D_MODEL), y_s.reshape(n_smp, 1, D_MODEL),
            s_p[None], cv_p[None], s_s[None], cv_s.reshape(1, n_smp, CONV_K - 1, CONV_W))
```

```python
import functools

import jax
import jax.numpy as jnp
from jax import lax
from jax.experimental import pallas as pl
from jax.experimental.pallas import tpu as pltpu
from jax.experimental.pallas import tpu_sc as plsc

F32 = jnp.float32
BF16 = jnp.bfloat16
I32 = jnp.int32

D_MODEL = 1024
HALF_D = D_MODEL // 2
HEADS = 4
DK = 128
KEY_W = HEADS * DK
CONV_W = 512
CONV_K = 3
IN_W = 2 * KEY_W + 2 * KEY_W + 3 * CONV_W + 2 * D_MODEL
N_EXP = 64
TOP_K = 8
N_GRP = 8
GRP_SZ = N_EXP // N_GRP
TOPK_GRP = 4
EXP_FF = 256
SH_FF = 256
ROUTED_SCALE = 2.5
EPS = 1e-6

C_Q, C_F, C_I, C_G = 0, 512, 1024, 1536
C_BB, C_CC, C_VB = 2048, 2560, 3072
C_MGA, C_MGB = 3584, 4608

MIX_TILE = 512
SUB = 256
CHUNK = 64
ROUTE_TILE = 128
GMM_BM = 512
GMM_NX = 4
GMM_NY = 3
FINAL_TILE = 512
FINAL_SPLIT = (4, 12, 22, 32)
SC_WINDOW = 128
PLANE_W = HALF_D // 2
VMEM_LIMIT = 56 * 1024 * 1024


def _dot(a, b):
    return jnp.dot(a, b, preferred_element_type=F32)


def _dot_nt(a, b):
    return lax.dot_general(a, b, (((1,), (1,)), ((), ())), preferred_element_type=F32)


def _dot_tn(a, b):
    return lax.dot_general(a, b, (((0,), (0,)), ((), ())), preferred_element_type=F32)


def _sigmoid(x):
    return 0.5 * jnp.tanh(0.5 * x) + 0.5


def _silu(x):
    h = 0.5 * x
    return h * jnp.tanh(h) + h


def _rms(x):
    return x * lax.rsqrt(jnp.mean(x * x, axis=-1, keepdims=True) + EPS)


def _lower_bound(lbl):
    a, b = lbl[0:1], lbl[1:2]
    m = jnp.maximum(a, b)
    ea, eb = jnp.exp(a - m), jnp.exp(b - m)
    return ea / (ea + eb)


def _split3(x):
    hi = x.astype(BF16)
    r1 = x - hi.astype(F32)
    mid = r1.astype(BF16)
    lo = (r1 - mid.astype(F32)).astype(BF16)
    return hi, mid, lo


def _words(lo_b, hi_b):
    lo = lax.shift_right_logical(lax.bitcast_convert_type(lo_b.astype(F32), I32), 16)
    hi = lax.bitcast_convert_type(hi_b.astype(F32), I32) & jnp.int32(-65536)
    return lo | hi


def _halves(w):
    lo = lax.bitcast_convert_type(lax.shift_left(w, 16), F32)
    hi = lax.bitcast_convert_type(w & jnp.int32(-65536), F32)
    return lo.astype(BF16), hi.astype(BF16)


def _pack_rows(xb, out_ref):
    words = _words(xb[:, :HALF_D], xb[:, HALF_D:])
    out_ref[0] = words[:, :PLANE_W]
    out_ref[1] = words[:, PLANE_W:]


def _unpack_rows(p0, p1):
    c0, c2 = _halves(p0)
    c1, c3 = _halves(p1)
    return c0, c1, c2, c3


def _ada_kernel(cp_ref, cs_ref, w_ref, b_ref, op_ref, os_ref):
    w = w_ref[...].astype(BF16)
    for c_ref, o_ref in ((cp_ref, op_ref), (cs_ref, os_ref)):
        o_ref[...] = _dot(_silu(c_ref[...]).astype(BF16), w) + b_ref[...]


def _ada(c_prompt, c_sample, w_ada, b_ada):
    blk = 1024
    rows = lambda c: pl.BlockSpec((c.shape[0], D_MODEL), lambda j: (0, 0))
    cols = lambda c: pl.BlockSpec((c.shape[0], blk), lambda j: (0, j))
    return pl.pallas_call(
        _ada_kernel,
        grid=(6 * D_MODEL // blk,),
        in_specs=[rows(c_prompt), rows(c_sample),
                  pl.BlockSpec((D_MODEL, blk), lambda j: (0, j)),
                  pl.BlockSpec((1, blk), lambda j: (0, j))],
        out_specs=[cols(c_prompt), cols(c_sample)],
        out_shape=[jax.ShapeDtypeStruct((c.shape[0], 6 * D_MODEL), F32) for c in (c_prompt, c_sample)],
        name="ada",
    )(c_prompt, c_sample, w_ada, b_ada.reshape(1, -1))


def _ffn_pre(x1, mod_rows, gffn, w_sgu, w_sd, wr_hl):
    sh2, sc2, g2 = mod_rows
    h2 = _rms(x1) * gffn * (1.0 + sc2) + sh2
    h2b = h2.astype(BF16)
    gu = _dot(h2b, w_sgu)
    act = _silu(gu[:, :SH_FF]) * gu[:, SH_FF:]
    xmid = x1 + g2 * _dot(act.astype(BF16), w_sd)
    h2lo = (h2 - h2b.astype(F32)).astype(BF16)
    both = _dot_nt(wr_hl, h2b)
    lgt = both[:N_EXP] + both[N_EXP:] + _dot_nt(wr_hl[:N_EXP], h2lo)
    return xmid, h2b, lgt


def _mix_kernel(*refs, nt, n_tiles):
    i = pl.program_id(0)
    rbias_ref, h2_ref = refs[15], refs[17]
    idx_ref, w_ref, rank_ref, cnt_ref, lgt_ref = refs[18], refs[19], refs[20], refs[21], refs[-1]

    @pl.when(i == 0)
    def _():
        lgt_ref[...] = jnp.zeros_like(lgt_ref)
        cnt_ref[...] = jnp.zeros_like(cnt_ref)

    def route_previous(keep):
        carry = cnt_ref[:, 0:1]
        for c0 in range(0, lgt_ref.shape[1], ROUTE_TILE):
            cols = slice(c0, c0 + ROUTE_TILE)
            picks, weights, ranks, carry = _route_tile(lgt_ref[:, cols], rbias_ref[...], carry)
            for k in range(TOP_K):
                idx_ref[k:k + 1, cols] = picks[k]
                w_ref[k:k + 1, cols] = weights[k]
                rank_ref[k:k + 1, cols] = ranks[k]
            yield
        cnt_ref[...] = jnp.broadcast_to(jnp.where(keep, carry, 0.0), cnt_ref.shape)

    @pl.when(i == n_tiles)
    def _():
        h2_ref[...] = jnp.zeros_like(h2_ref)
        for _ in route_previous(True):
            pass

    @pl.when(i < n_tiles)
    def _():
        _mix_tile(i % nt, nt, route_previous(i > 0), *refs)


def _mix_tile(t, nt, routing, x_ref, mod_ref, gmix_ref, gffn_ref, w_in_ref, lbl_ref, hg_ref, cw_ref, cb_ref,
              w_oh_ref, w_oc_ref, w_o_ref, wr_hl_ref, w_sgu_ref, w_sd_ref, rbias_ref,
              xmid_ref, h2_ref, idx_ref, w_ref, rank_ref, cnt_ref, s_out_ref, cv_out_ref,
              proj_ref, st_ref, cbuf_ref, ya_ref, lgt_ref):
    del rbias_ref, idx_ref, w_ref, rank_ref, cnt_ref
    tt = x_ref.shape[1]

    @pl.when(t == 0)
    def _():
        st_ref[...] = jnp.zeros_like(st_ref)
        cbuf_ref[...] = jnp.zeros_like(cbuf_ref)

    x = x_ref[0]
    mod = mod_ref[0]
    sh1, sc1, g1 = mod[0:1], mod[1:2], mod[2:3]
    h = _rms(x) * gmix_ref[...] * (1.0 + sc1) + sh1
    hb = h.astype(BF16)
    for c in range(0, IN_W, 512):
        proj_ref[:, c:c + 512] = _dot(hb, w_in_ref[:, c:c + 512])
        if c % 1024 == 512:
            next(routing, None)
    for _ in routing:
        pass

    lb = _lower_bound(lbl_ref[...])
    row = lax.broadcasted_iota(I32, (SUB, SUB), 0)
    col = lax.broadcasted_iota(I32, (SUB, SUB), 1)
    tri = (col <= row).astype(BF16)
    mask_d = (col <= row) & (row // CHUNK == col // CHUNK)
    mask_a = row // (2 * CHUNK) == col // (2 * CHUNK)
    n_ch = SUB // CHUNK

    def by_chunk(vals):
        return jnp.concatenate([jnp.zeros((CHUNK, DK), F32) if v is None
                                else jnp.broadcast_to(v, (CHUNK, DK)) for v in vals], axis=0)

    for s in range(tt // SUB):
        r0 = s * SUB
        f = lb + (1.0 - lb) * _sigmoid(proj_ref[r0:r0 + SUB, C_F:C_F + KEY_W])
        kk = 1.0 - f
        hi, mid, lo = _split3(jnp.log(f))
        bc = _dot(tri, hi) + _dot(tri, mid) + _dot(tri, lo)
        for hd in range(HEADS):
            hs = slice(hd * DK, (hd + 1) * DK)
            bh = bc[:, hs]
            at = lambda r: bh[r:r + 1]
            mids = [at(c * CHUNK + CHUNK // 2 - 1) for c in range(n_ch)]
            pair_mid = [at(CHUNK - 1), at(3 * CHUNK - 1)]
            step_mid, step_end = at(2 * CHUNK - 1), at(SUB - 1)
            arg = bh - by_chunk(mids)
            e_pos, e_neg = jnp.exp(arg), jnp.exp(-arg)
            q = _silu(proj_ref[r0:r0 + SUB, C_Q + hd * DK:C_Q + (hd + 1) * DK])
            v = proj_ref[r0:r0 + SUB, C_I + hd * DK:C_I + (hd + 1) * DK]
            qd = q * e_pos
            kd = kk[:, hs] * e_neg
            q_in = qd * by_chunk([jnp.exp(m) for m in mids])
            k_end = kd * by_chunk([jnp.exp(step_end - m) for m in mids])
            qa = qd * by_chunk([None, jnp.exp(mids[1] - pair_mid[0]), None, jnp.exp(mids[3] - pair_mid[1])])
            ka = kd * by_chunk([jnp.exp(pair_mid[0] - mids[0]), None, jnp.exp(pair_mid[1] - mids[2]), None])
            qb = qd * by_chunk([None, None, jnp.exp(mids[2] - step_mid), jnp.exp(mids[3] - step_mid)])
            kb = kd * by_chunk([jnp.exp(step_mid - mids[0]), jnp.exp(step_mid - mids[1]), None, None])
            att = jnp.where(mask_d, _dot_nt(qd.astype(BF16), kd.astype(BF16)), 0.0)
            att = att + jnp.where(mask_a, _dot_nt(qa.astype(BF16), ka.astype(BF16)), 0.0)
            att = att + _dot_nt(qb.astype(BF16), kb.astype(BF16))
            vb = v.astype(BF16)
            st = st_ref[hd]
            o = _dot(att.astype(BF16), vb) + _dot_nt(q_in.astype(BF16), st.astype(BF16))
            st_ref[hd] = st * jnp.exp(step_end) + _dot_tn(vb, k_end.astype(BF16))
            gate = _silu(proj_ref[r0:r0 + SUB, C_G + hd * DK:C_G + (hd + 1) * DK])
            ya_ref[r0:r0 + SUB, hs] = _rms(o) * hg_ref[:, hs] * gate

    u = proj_ref[:, C_CC:C_CC + CONV_W] * proj_ref[:, C_VB:C_VB + CONV_W]
    rows = lax.broadcasted_iota(I32, (tt, CONV_W), 0)
    c0, c1 = cbuf_ref[0:1], cbuf_ref[1:2]
    u1 = jnp.where(rows == 0, c1, pltpu.roll(u, 1, axis=0))
    u2 = jnp.where(rows == 0, c0, jnp.where(rows == 1, c1, pltpu.roll(u, 2, axis=0)))
    conv = cw_ref[0:1] * u2 + cw_ref[1:2] * u1 + cw_ref[2:3] * u + cb_ref[...]
    yb = proj_ref[:, C_BB:C_BB + CONV_W] * conv
    cbuf_ref[...] = u[tt - 2:tt]

    mixed = (_sigmoid(proj_ref[:, C_MGA:C_MGA + D_MODEL]) * _dot(ya_ref[...].astype(BF16), w_oh_ref[...])
             + _sigmoid(proj_ref[:, C_MGB:C_MGB + D_MODEL]) * _dot(yb.astype(BF16), w_oc_ref[...]))
    x1 = x + g1 * _dot(mixed.astype(BF16), w_o_ref[...])

    xmid, h2b, lgt = _ffn_pre(x1, (mod[3:4], mod[4:5], mod[5:6]), gffn_ref[...],
                              w_sgu_ref[...], w_sd_ref[...], wr_hl_ref[...])
    xmid_ref[0] = xmid
    _pack_rows(h2b, h2_ref)
    lgt_ref[...] = lgt

    @pl.when(t == nt - 1)
    def _():
        for hd in range(HEADS):
            s_out_ref[0, hd] = st_ref[hd].T
        cv_out_ref[0] = cbuf_ref[...]


def _const_spec(shape):
    nd = len(shape)
    return pl.BlockSpec(shape, lambda i, _nd=nd: (0,) * _nd, pipeline_mode=pl.Buffered(1))


def _mix(x, mod, n_tok, gmix, gffn, w_in, lbl, hg, cw, cb, w_oh, w_oc, w_o, wr_hl, w_sgu, w_sd, rbias):
    bsz, seq, _ = x.shape
    tt = MIX_TILE
    nt = seq // tt
    n_tiles = bsz * nt
    assert n_tiles * tt < n_tok <= (n_tiles + 1) * tt and tt % ROUTE_TILE == 0
    consts = [gmix, gffn, w_in, lbl, hg, cw, cb, w_oh, w_oc, w_o, wr_hl, w_sgu, w_sd, rbias]
    tile = lambda i: jnp.minimum(i, n_tiles - 1)
    routed = pl.BlockSpec((TOP_K, tt), lambda i: (0, jnp.maximum(i - 1, 0)))
    slot = lambda dt: jax.ShapeDtypeStruct((TOP_K, n_tiles * tt), dt)
    return pl.pallas_call(
        functools.partial(_mix_kernel, nt=nt, n_tiles=n_tiles),
        grid=(n_tiles + 1,),
        in_specs=[pl.BlockSpec((1, tt, D_MODEL), lambda i: (tile(i) // nt, tile(i) % nt, 0)),
                  pl.BlockSpec((1, 6, D_MODEL), lambda i: (tile(i) // nt, 0, 0))]
                 + [_const_spec(a.shape) for a in consts],
        out_specs=[pl.BlockSpec((1, tt, D_MODEL), lambda i: (tile(i) // nt, tile(i) % nt, 0)),
                   pl.BlockSpec((2, tt, PLANE_W), lambda i: (0, i, 0)),
                   routed, routed, routed,
                   pl.BlockSpec((N_EXP, 128), lambda i: (0, 0)),
                   pl.BlockSpec((1, HEADS, DK, DK), lambda i: (tile(i) // nt, 0, 0, 0)),
                   pl.BlockSpec((1, CONV_K - 1, CONV_W), lambda i: (tile(i) // nt, 0, 0))],
        out_shape=[jax.ShapeDtypeStruct((bsz, seq, D_MODEL), F32),
                   jax.ShapeDtypeStruct((2, n_tok, PLANE_W), I32),
                   slot(I32), slot(F32), slot(I32),
                   jax.ShapeDtypeStruct((N_EXP, 128), F32),
                   jax.ShapeDtypeStruct((bsz, HEADS, DK, DK), F32),
                   jax.ShapeDtypeStruct((bsz, CONV_K - 1, CONV_W), F32)],
        scratch_shapes=[pltpu.VMEM((tt, IN_W), F32),
                        pltpu.VMEM((HEADS, DK, DK), F32),
                        pltpu.VMEM((CONV_K - 1, CONV_W), F32),
                        pltpu.VMEM((tt, KEY_W), F32),
                        pltpu.VMEM((N_EXP, tt), F32)],
        compiler_params=pltpu.CompilerParams(
            dimension_semantics=("arbitrary",), vmem_limit_bytes=VMEM_LIMIT),
        name="mix",
    )(x, mod, *consts)


def _smp1_kernel(x_ref, mod_ref, gmix_ref, w_in_ref, lbl_ref, cw_ref, cb_ref, cst_ref,
                 f_ref, k_ref, q_ref, v_ref, gate_ref, yb_ref, sga_ref, sgb_ref, cv_out_ref):
    x = x_ref[...]
    sh1, sc1 = mod_ref[:, 0:D_MODEL], mod_ref[:, D_MODEL:2 * D_MODEL]
    h = _rms(x) * gmix_ref[...] * (1.0 + sc1) + sh1
    hb = h.astype(BF16)

    def proj(c, w):
        return _dot(hb, w_in_ref[:, c:c + w])

    lb = _lower_bound(lbl_ref[...])
    f = lb + (1.0 - lb) * _sigmoid(proj(C_F, KEY_W))
    f_ref[...] = f
    k_ref[...] = 1.0 - f
    q_ref[...] = _silu(proj(C_Q, KEY_W))
    v_ref[...] = proj(C_I, KEY_W)
    gate_ref[...] = _silu(proj(C_G, KEY_W))
    u = proj(C_CC, CONV_W) * proj(C_VB, CONV_W)
    c0, c1 = cst_ref[:, 0:CONV_W], cst_ref[:, CONV_W:2 * CONV_W]
    conv = cw_ref[0:1] * c0 + cw_ref[1:2] * c1 + cw_ref[2:3] * u + cb_ref[...]
    yb_ref[...] = proj(C_BB, CONV_W) * conv
    cv_out_ref[:, 0:CONV_W] = c1
    cv_out_ref[:, CONV_W:2 * CONV_W] = u
    sga_ref[...] = _sigmoid(proj(C_MGA, D_MODEL))
    sgb_ref[...] = _sigmoid(proj(C_MGB, D_MODEL))


def _smp1(x, mod, gmix, w_in, lbl, cw, cb, cst):
    n = x.shape[0]
    kw = jax.ShapeDtypeStruct((n, KEY_W), F32)
    dm = jax.ShapeDtypeStruct((n, D_MODEL), F32)
    return pl.pallas_call(
        _smp1_kernel,
        out_shape=[kw, kw, kw, kw, kw, kw, dm, dm,
                   jax.ShapeDtypeStruct((n, (CONV_K - 1) * CONV_W), F32)],
        compiler_params=pltpu.CompilerParams(vmem_limit_bytes=VMEM_LIMIT),
        name="smp1",
    )(x, mod, gmix, w_in, lbl, cw, cb, cst)


def _smp2(f, k, q, v, state):
    n = f.shape[0]
    info = plsc.get_sparse_core_info()
    lanes = info.num_lanes
    n_workers = info.num_cores * info.num_subcores
    tok_per = n // n_workers
    n_chunks = DK // lanes
    vec = lambda: pltpu.VMEM((DK,), F32)

    @pl.kernel(out_type=[jax.ShapeDtypeStruct(state.shape, F32), jax.ShapeDtypeStruct((n, KEY_W), F32)],
               mesh=_sc_mesh(), scratch_types=[pltpu.VMEM((DK, DK), F32), vec(), vec(), vec(), vec(), vec()],
               compiler_params=pltpu.CompilerParams(needs_layout_passes=False), name="smp2")
    def run(f_hbm, k_hbm, q_hbm, v_hbm, s_hbm, s_out_hbm, o_hbm, s_v, f_v, k_v, q_v, v_v, o_v):
        wid = lax.axis_index("subcore") * info.num_cores + lax.axis_index("core")

        def tile(j, carry):
            t = wid * tok_per + j // HEADS
            hcols = pl.ds((j % HEADS) * DK, DK)
            pltpu.sync_copy(s_hbm.at[t, j % HEADS], s_v)
            for src, dst in ((f_hbm, f_v), (k_hbm, k_v), (q_hbm, q_v), (v_hbm, v_v)):
                pltpu.sync_copy(src.at[t, hcols], dst)
            v_chunks = [v_v[pl.ds(c * lanes, lanes)] for c in range(n_chunks)]

            def row(d, acc):
                at = [jnp.zeros((lanes,), I32) + d]
                fd, kd, qd = (plsc.load_gather(r, at) for r in (f_v, k_v, q_v))
                out = []
                for c in range(n_chunks):
                    cols = pl.ds(c * lanes, lanes)
                    new = s_v[d, cols] * fd + kd * v_chunks[c]
                    s_v[d, cols] = new
                    out.append(acc[c] + qd * new)
                return tuple(out)

            acc = lax.fori_loop(0, DK, row, tuple(jnp.zeros((lanes,), F32) for _ in range(n_chunks)))
            for c in range(n_chunks):
                o_v[pl.ds(c * lanes, lanes)] = acc[c]
            pltpu.sync_copy(s_v, s_out_hbm.at[t, j % HEADS])
            pltpu.sync_copy(o_v, o_hbm.at[t, hcols])
            return carry

        lax.fori_loop(0, tok_per * HEADS, tile, 0)

    return run(f, k, q, v, state)


def _smp3_kernel(x_ref, mod_ref, o_ref, gate_ref, yb_ref, sga_ref, sgb_ref, hg_ref, gffn_ref,
                 w_oh_ref, w_oc_ref, w_o_ref, wr_hl_ref, w_sgu_ref, w_sd_ref,
                 rbias_ref, cnt_p_ref, idx_p_ref, w_p_ref, rank_p_ref, h2_all_ref,
                 xmid_ref, h2_ref, idx_ref, w_ref, rank_ref, cnt_ref):
    del h2_all_ref
    parts = []
    for hd in range(HEADS):
        hs = slice(hd * DK, (hd + 1) * DK)
        parts.append(_rms(o_ref[:, hs]) * hg_ref[:, hs] * gate_ref[:, hs])
    ya = jnp.concatenate(parts, axis=1)
    mixed = (sga_ref[...] * _dot(ya.astype(BF16), w_oh_ref[...])
             + sgb_ref[...] * _dot(yb_ref[...].astype(BF16), w_oc_ref[...]))
    g1 = mod_ref[:, 2 * D_MODEL:3 * D_MODEL]
    x1 = x_ref[...] + g1 * _dot(mixed.astype(BF16), w_o_ref[...])
    mod_rows = tuple(mod_ref[:, j * D_MODEL:(j + 1) * D_MODEL] for j in (3, 4, 5))
    xmid, h2b, lgt = _ffn_pre(x1, mod_rows, gffn_ref[...], w_sgu_ref[...], w_sd_ref[...],
                              wr_hl_ref[...])
    xmid_ref[...] = xmid
    _pack_rows(h2b, h2_ref)
    n_prompt = idx_p_ref.shape[1]
    picks, weights, ranks, total = _route_tile(lgt, rbias_ref[...], cnt_p_ref[:, 0:1])
    for src, dst, new in ((idx_p_ref, idx_ref, picks), (w_p_ref, w_ref, weights), (rank_p_ref, rank_ref, ranks)):
        dst[:, :n_prompt] = src[...]
        for k in range(TOP_K):
            dst[k:k + 1, n_prompt:] = new[k]
    cnt_ref[...] = jnp.broadcast_to(total, cnt_ref.shape).astype(I32)


def _smp3(x, mod, o, gate, yb, sga, sgb, hg, gffn, w_oh, w_oc, w_o, wr_hl, w_sgu, w_sd,
          rbias, cnt_p, idx_p, w_p, rank_p, h2_all):
    n = x.shape[0]
    vmem_args = [x, mod, o, gate, yb, sga, sgb, hg, gffn, w_oh, w_oc, w_o, wr_hl, w_sgu, w_sd,
                 rbias, cnt_p, idx_p, w_p, rank_p]
    n_prompt = idx_p.shape[1]
    blk = n_prompt // n
    slot = lambda dt: jax.ShapeDtypeStruct((TOP_K, n_prompt + n), dt)

    def full(a):
        nd = a.ndim
        return pl.BlockSpec(a.shape, lambda i, _nd=nd: (0,) * _nd)

    return pl.pallas_call(
        _smp3_kernel,
        grid=(1,),
        in_specs=[full(a) for a in vmem_args] + [pl.BlockSpec(memory_space=pl.ANY)],
        out_specs=[pl.BlockSpec((n, D_MODEL), lambda i: (0, 0)),
                   pl.BlockSpec((2, n, PLANE_W), lambda i: (0, blk, 0)),
                   pl.BlockSpec((TOP_K, n_prompt + n), lambda i: (0, 0)),
                   pl.BlockSpec((TOP_K, n_prompt + n), lambda i: (0, 0)),
                   pl.BlockSpec((TOP_K, n_prompt + n), lambda i: (0, 0)),
                   pl.BlockSpec((N_EXP, 128), lambda i: (0, 0))],
        out_shape=[jax.ShapeDtypeStruct((n, D_MODEL), F32),
                   jax.ShapeDtypeStruct(h2_all.shape, h2_all.dtype),
                   slot(I32), slot(F32), slot(I32),
                   jax.ShapeDtypeStruct((N_EXP, 128), I32)],
        input_output_aliases={len(vmem_args): 1},
        compiler_params=pltpu.CompilerParams(
            dimension_semantics=("arbitrary",), vmem_limit_bytes=VMEM_LIMIT),
        name="smp3",
    )(*vmem_args, h2_all)


def _route_tile(lgt, bias, carry):
    tr = lgt.shape[1]
    neg = -jnp.inf
    scores = _sigmoid(lgt)
    sel = scores + bias
    j8 = lax.broadcasted_iota(I32, (GRP_SZ, tr), 0)
    groups = [sel[g * GRP_SZ:(g + 1) * GRP_SZ] for g in range(N_GRP)]
    gscore = []
    for grp in groups:
        m1 = jnp.max(grp, axis=0, keepdims=True)
        i1 = jnp.min(jnp.where(grp == m1, j8, GRP_SZ), axis=0, keepdims=True)
        m2 = jnp.max(jnp.where(j8 == i1, neg, grp), axis=0, keepdims=True)
        gscore.append(m1 + m2)
    kept = []
    for g in range(N_GRP):
        beaten = jnp.zeros((1, tr), I32)
        for o in range(N_GRP):
            if o < g:
                beaten = beaten + (gscore[o] >= gscore[g]).astype(I32)
            elif o > g:
                beaten = beaten + (gscore[o] > gscore[g]).astype(I32)
        kept.append(jnp.where(beaten < TOPK_GRP, groups[g], neg))
    masked = jnp.concatenate(kept, axis=0)
    ei = lax.broadcasted_iota(I32, masked.shape, 0)
    chosen = jnp.zeros(masked.shape, jnp.bool_)
    picks, weights = [], []
    for _ in range(TOP_K):
        m = jnp.max(masked, axis=0, keepdims=True)
        pick = jnp.min(jnp.where((masked == m) & ~chosen, ei, N_EXP), axis=0, keepdims=True)
        hit = ei == pick
        weights.append(jnp.sum(jnp.where(hit, scores, 0.0), axis=0, keepdims=True))
        picks.append(pick)
        chosen = chosen | hit
        masked = jnp.where(hit, neg, masked)
    wsum = weights[0]
    for w in weights[1:]:
        wsum = wsum + w
    sel01 = chosen.astype(F32)
    r = lax.broadcasted_iota(I32, (tr, tr), 0)
    c = lax.broadcasted_iota(I32, (tr, tr), 1)
    before = (r < c).astype(BF16)
    cnt = _dot(sel01.astype(BF16), before) + carry
    weights = [w / wsum * ROUTED_SCALE for w in weights]
    ranks = [jnp.sum(jnp.where(ei == p, cnt, 0.0), axis=0, keepdims=True).astype(I32) for p in picks]
    return picks, weights, ranks, carry + jnp.sum(sel01, axis=1, keepdims=True)


def _dest_kernel(start_ref, idx_ref, rank_ref, all_ref, *chunk_refs, n_rows, chunks):
    n_tok = idx_ref.shape[1]
    idx = idx_ref[...]
    acc = rank_ref[...]
    for e in range(N_EXP):
        acc = acc + jnp.where(idx == e, start_ref[e], 0)
    for k in range(TOP_K):
        for p in range(2):
            row = acc[k:k + 1, :] + p * n_rows
            seg = 2 * k + p
            all_ref[:, seg * n_tok:(seg + 1) * n_tok] = row
            for (c0, c1), ref in zip(chunks, chunk_refs):
                ref[:, seg * (c1 - c0):(seg + 1) * (c1 - c0)] = row[:, c0:c1]


def _dest(pad_start, idx, rank, n_rows, chunks):
    k, n_tok = idx.shape
    vmem = pl.BlockSpec(memory_space=pltpu.VMEM)
    sizes = [n_tok] + [c1 - c0 for c0, c1 in chunks]
    return pl.pallas_call(
        functools.partial(_dest_kernel, n_rows=n_rows, chunks=chunks),
        in_specs=[pl.BlockSpec(memory_space=pltpu.SMEM), vmem, vmem],
        out_specs=[vmem] * len(sizes),
        out_shape=[jax.ShapeDtypeStruct((1, 2 * k * n), I32) for n in sizes],
        name="dest",
    )(pad_start, idx, rank)


def _sc_mesh():
    return plsc.VectorSubcoreMesh(core_axis_name="core", subcore_axis_name="subcore")


def _dispatch(rows, dest, n_out):
    n, width = rows.shape
    win = SC_WINDOW
    steps = n // win

    @pl.kernel(out_type=jax.ShapeDtypeStruct((n_out, width), rows.dtype), mesh=_sc_mesh(),
               scratch_types=[], name="dispatch")
    def run(x_hbm, *refs):
        i_hbms, o_hbm = refs[:TOP_K], refs[TOP_K]

        def body(x_vmem, *i_vmems):
            for i_vmem in i_vmems:
                pltpu.sync_copy(x_vmem, o_hbm.at[i_vmem.at[0]])

        pltpu.emit_pipeline(
            body,
            grid=(steps,),
            in_specs=[pl.BlockSpec((win, width), lambda i: (i, 0))]
                     + [pl.BlockSpec((1, win), lambda i, k=k: (0, k * steps + i)) for k in range(TOP_K)],
            out_specs=[],
            core_axis_name=("core", "subcore"),
            dimension_semantics=(pltpu.PARALLEL,),
        )(x_hbm, *i_hbms)

    assert dest.shape == (1, TOP_K * n)
    return run(rows, *([dest] * TOP_K))


def _combine(rows, dest_flat):
    width = rows.shape[1]
    n = dest_flat.shape[1]
    win = SC_WINDOW

    @pl.kernel(out_type=jax.ShapeDtypeStruct((n, width), rows.dtype), mesh=_sc_mesh(),
               scratch_types=[], name="combine")
    def run(y_hbm, i_hbm, o_hbm):
        def body(i_vmem, o_vmem):
            pltpu.sync_copy(y_hbm.at[i_vmem.at[0]], o_vmem)

        pltpu.emit_pipeline(
            body,
            grid=(n // win,),
            in_specs=[pl.BlockSpec((1, win), lambda i: (0, i))],
            out_specs=[pl.BlockSpec((win, width), lambda i: (i, 0))],
            core_axis_name=("core", "subcore"),
            dimension_semantics=(pltpu.PARALLEL,),
        )(i_hbm, o_hbm)

    return run(rows, dest_flat)


def _gmm_kernel(blk_exp_ref, n_used_ref, xs_hbm, wg_hbm, wu_hbm, wd_hbm, ys_hbm,
                xbuf, ybuf, wg32, wu32, wd32, wgu_b, wd_b, xsem, ysem, wsem, run_ref):
    nx, ny, bm = xbuf.shape[0], ybuf.shape[0], xbuf.shape[2]
    n_used = n_used_ref[0]

    def x_copies(b):
        rows, slot = pl.ds(b * bm, bm), b % nx
        return [pltpu.make_async_copy(xs_hbm.at[p, rows, :], xbuf.at[slot, p], xsem.at[slot, p]) for p in range(2)]

    def y_copies(b):
        rows, slot = pl.ds(b * bm, bm), b % ny
        return [pltpu.make_async_copy(ybuf.at[slot, p], ys_hbm.at[p, rows, :], ysem.at[slot, p]) for p in range(2)]

    def start(copies):
        for c in copies:
            c.start()

    def wait(copies):
        for c in copies:
            c.wait()

    def w_copies(e, slot):
        return (pltpu.make_async_copy(wg_hbm.at[e], wg32.at[slot], wsem.at[slot, 0]),
                pltpu.make_async_copy(wu_hbm.at[e], wu32.at[slot], wsem.at[slot, 1]),
                pltpu.make_async_copy(wd_hbm.at[e], wd32.at[slot], wsem.at[slot, 2]))

    def run_end(b):
        return lax.while_loop(lambda j: (j < n_used) & (blk_exp_ref[jnp.minimum(j, n_used - 1)] == blk_exp_ref[b]),
                              lambda j: j + 1, b + 1)

    run_ref[0] = 0
    start(x_copies(0))
    start(w_copies(blk_exp_ref[0], 0))
    for j in range(1, nx - 1):
        @pl.when(j < n_used)
        def _():
            start(x_copies(j))

    def block(b, carry):
        @pl.when(b + nx - 1 < n_used)
        def _():
            start(x_copies(b + nx - 1))

        @pl.when((b == 0) | (blk_exp_ref[b] != blk_exp_ref[jnp.maximum(b - 1, 0)]))
        def _():
            wslot = run_ref[0] % 2
            run_ref[0] = run_ref[0] + 1
            wait(w_copies(blk_exp_ref[b], wslot))
            wgu_b[:, 0:EXP_FF] = wg32[wslot].astype(BF16)
            wgu_b[:, EXP_FF:2 * EXP_FF] = wu32[wslot].astype(BF16)
            wd_b[...] = wd32[wslot].astype(BF16)
            nxt = run_end(b)

            @pl.when(nxt < n_used)
            def _():
                start(w_copies(blk_exp_ref[jnp.minimum(nxt, n_used - 1)], 1 - wslot))

        wait(x_copies(b))

        @pl.when(b >= ny)
        def _():
            wait(y_copies(b - ny))

        xslot = b % nx
        xc = _unpack_rows(xbuf[xslot, 0], xbuf[xslot, 1])
        gu = sum(_dot(c, wgu_b[i * PLANE_W:(i + 1) * PLANE_W, :]) for i, c in enumerate(xc))
        act = (_silu(gu[:, :EXP_FF]) * gu[:, EXP_FF:]).astype(BF16)
        _pack_rows(_dot(act, wd_b[...]).astype(BF16), ybuf.at[b % ny])
        start(y_copies(b))
        return carry

    lax.fori_loop(0, n_used, block, 0)

    for j in range(ny, 0, -1):
        @pl.when(n_used >= j)
        def _():
            wait(y_copies(n_used - j))


def _gmm(blk_exp, n_used, xs, w_gate, w_up, w_down):
    n_rows = xs.shape[1]
    bm = GMM_BM
    nb = n_rows // bm

    assert blk_exp.shape == (nb,)
    any_spec = pl.BlockSpec(memory_space=pl.ANY)
    grid_spec = pltpu.PrefetchScalarGridSpec(
        num_scalar_prefetch=2,
        grid=(1,),
        in_specs=[any_spec, any_spec, any_spec, any_spec],
        out_specs=any_spec,
        scratch_shapes=[pltpu.VMEM((GMM_NX, 2, bm, PLANE_W), I32), pltpu.VMEM((GMM_NY, 2, bm, PLANE_W), I32),
                        pltpu.VMEM((2, D_MODEL, EXP_FF), F32), pltpu.VMEM((2, D_MODEL, EXP_FF), F32),
                        pltpu.VMEM((2, EXP_FF, D_MODEL), F32),
                        pltpu.VMEM((D_MODEL, 2 * EXP_FF), BF16), pltpu.VMEM((EXP_FF, D_MODEL), BF16),
                        pltpu.SemaphoreType.DMA((GMM_NX, 2)), pltpu.SemaphoreType.DMA((GMM_NY, 2)),
                        pltpu.SemaphoreType.DMA((2, 3)), pltpu.SMEM((1,), I32)],
    )
    return pl.pallas_call(
        _gmm_kernel,
        grid_spec=grid_spec,
        out_shape=jax.ShapeDtypeStruct((2, n_rows, PLANE_W), I32),
        compiler_params=pltpu.CompilerParams(dimension_semantics=("arbitrary",)),
        name="gmm",
    )(blk_exp, n_used, xs, w_gate, w_up, w_down)


def _final_kernel(xmid_ref, g2_ref, z_ref, w_ref, gfin_ref, *rest, g2_row):
    y_ref = rest[-1]
    g2 = g2_ref[0] if g2_row is None else g2_ref[0, g2_row:g2_row + 1]
    accs = [jnp.zeros((xmid_ref.shape[0], PLANE_W), F32) for _ in range(4)]
    w_cols = w_ref[...].T
    for k in range(TOP_K):
        wk = w_cols[:, k:k + 1]
        cols = _unpack_rows(z_ref[k, 0], z_ref[k, 1])
        accs = [a + wk * c.astype(F32) for a, c in zip(accs, cols)]
    acc = jnp.concatenate(accs, axis=1)
    y_ref[...] = _rms(xmid_ref[...] + g2 * acc) * gfin_ref[...]


def _final(xmid, g2, z, w_t, gfin, tile, *, n_tiles, x_tile0, z_tile0, w_tile0, tiles_per_g2,
           g2_row=None, g2_col=0, y_prev=None):
    args = [xmid, g2, z, w_t, gfin]
    in_specs = [pl.BlockSpec((tile, D_MODEL), lambda i: (x_tile0 + i, 0)),
                pl.BlockSpec((1, g2.shape[1], D_MODEL), lambda i: ((x_tile0 + i) // tiles_per_g2, 0, g2_col)),
                pl.BlockSpec((TOP_K, 2, tile, PLANE_W), lambda i: (0, 0, z_tile0 + i, 0)),
                pl.BlockSpec((TOP_K, tile), lambda i: (0, w_tile0 + i)),
                pl.BlockSpec((1, D_MODEL), lambda i: (0, 0))]
    aliases = {}
    if y_prev is not None:
        args.append(y_prev)
        in_specs.append(pl.BlockSpec(memory_space=pl.ANY))
        aliases = {len(args) - 1: 0}
    return pl.pallas_call(
        functools.partial(_final_kernel, g2_row=g2_row),
        grid=(n_tiles,),
        in_specs=in_specs,
        out_specs=pl.BlockSpec((tile, D_MODEL), lambda i: (x_tile0 + i, 0)),
        out_shape=jax.ShapeDtypeStruct(xmid.shape, F32),
        input_output_aliases=aliases,
        compiler_params=pltpu.CompilerParams(dimension_semantics=("arbitrary",)),
        name="final",
    )(*args)


def kernel(x_prompt, x_sample, state_hgrn, state_conv, c_prompt, c_sample, w_ada, b_ada, norm_mix_g, norm_ffn_g, w_in, lb_logits, hgrn_norm_g, conv_w, conv_b, w_out_hgrn, w_out_conv, w_o, w_router, router_bias, w_exp_gate, w_exp_up, w_exp_down, w_sh_gate, w_sh_up, w_sh_down, final_norm_g):
    assert w_ada.shape[0] == 1 and lb_logits.shape[0] == 2
    bsz, seq, _ = x_prompt.shape
    n_smp = x_sample.shape[0]
    n_prompt = bsz * seq
    n_tok = n_prompt + n_smp

    w_in_b = w_in[0].astype(BF16)
    w_oh_b = w_out_hgrn[0].astype(BF16)
    w_oc_b = w_out_conv[0].astype(BF16)
    w_o_b = w_o[0].astype(BF16)
    wr_t = w_router[0].T
    wr_hi = wr_t.astype(BF16)
    wr_hl = jnp.concatenate([wr_hi, (wr_t - wr_hi.astype(F32)).astype(BF16)], axis=0)
    w_sgu = jnp.concatenate([w_sh_gate[0], w_sh_up[0]], axis=1).astype(BF16)
    w_sd = w_sh_down[0].astype(BF16)
    gmix = norm_mix_g[0].reshape(1, D_MODEL)
    gffn = norm_ffn_g[0].reshape(1, D_MODEL)
    hg = hgrn_norm_g[0].reshape(1, KEY_W)
    cw = conv_w[0]
    cb = conv_b[0].reshape(1, CONV_W)
    gfin = final_norm_g.reshape(1, D_MODEL)

    mod_p, mod_s = _ada(c_prompt, c_sample, w_ada[0], b_ada[0])

    rbias = router_bias[0].reshape(N_EXP, 1)
    mod_p3 = mod_p.reshape(bsz, 6, D_MODEL)
    xmid_p, h2_all, idx_p, w_p, rank_p, cnt_p, s_p, cv_p = _mix(
        x_prompt, mod_p3, n_tok, gmix, gffn, w_in_b, lb_logits, hg, cw, cb, w_oh_b, w_oc_b, w_o_b, wr_hl, w_sgu, w_sd, rbias)

    xs2 = x_sample.reshape(n_smp, D_MODEL)
    f, kk, q, v, gate, yb, sga, sgb, cv_s = _smp1(
        xs2, mod_s, gmix, w_in_b, lb_logits, cw, cb, state_conv[0].reshape(n_smp, (CONV_K - 1) * CONV_W))
    s_s, o_s = _smp2(f, kk, q, v, state_hgrn[0])
    xmid_s, h2_all, idx, w_tok, rank, cnt = _smp3(xs2, mod_s, o_s, gate, yb, sga, sgb, hg, gffn,
                                                  w_oh_b, w_oc_b, w_o_b, wr_hl, w_sgu, w_sd,
                                                  rbias, cnt_p, idx_p, w_p, rank_p, h2_all)

    bm = GMM_BM
    n_blocks = (n_tok * TOP_K + N_EXP * (bm - 1)) // bm
    n_rows = n_blocks * bm
    counts = cnt[:, 0]
    padded = (counts + bm - 1) // bm * bm
    pad_end = jnp.cumsum(padded)
    pad_start = pad_end - padded
    blk_row0 = jnp.arange(n_blocks, dtype=I32) * bm
    blk_exp = jnp.minimum(jnp.sum((pad_end[None, :] <= blk_row0[:, None]).astype(I32), axis=1), N_EXP - 1)
    n_used = (pad_end[-1:] // bm).astype(I32)
    cuts = [0] + [n_prompt * f // FINAL_SPLIT[-1] for f in FINAL_SPLIT]
    assert all(c % FINAL_TILE == 0 for c in cuts) and cuts[-1] == n_prompt
    chunks = tuple((c0, n_tok if c1 == n_prompt else c1) for c0, c1 in zip(cuts[:-1], cuts[1:]))
    dest, *chunk_dest = _dest(pad_start.astype(I32), idx, rank, n_rows, chunks)

    xs = _dispatch(h2_all.reshape(2 * n_tok, PLANE_W), dest, 2 * n_rows).reshape(2, n_rows, PLANE_W)
    ys = _gmm(blk_exp, n_used, xs, w_exp_gate[0], w_exp_up[0], w_exp_down[0])

    ys_flat = ys.reshape(2 * n_rows, PLANE_W)
    w_t = w_tok
    xmid_p2 = xmid_p.reshape(n_prompt, D_MODEL)
    y_p = None
    for (c0, c1), dest_c in zip(chunks, chunk_dest):
        z = _combine(ys_flat, dest_c).reshape(TOP_K, 2, c1 - c0, PLANE_W)
        y_p = _final(xmid_p2, mod_p3, z, w_t, gfin, FINAL_TILE, n_tiles=(min(c1, n_prompt) - c0) // FINAL_TILE,
                     x_tile0=c0 // FINAL_TILE, z_tile0=0, w_tile0=c0 // FINAL_TILE,
                     tiles_per_g2=seq // FINAL_TILE, g2_row=5, y_prev=y_p)
    y_s = _final(xmid_s, mod_s.reshape(1, n_smp, 6 * D_MODEL), z, w_t, gfin, n_smp, n_tiles=1, x_tile0=0,
                 z_tile0=(n_prompt - c0) // n_smp, w_tile0=n_prompt // n_smp, tiles_per_g2=1, g2_col=5)

    return (y_p.reshape(bsz, seq, D_MODEL), y_s.reshape(n_smp, 1, D_MODEL),
            s_p[None], cv_p[None], s_s[None], cv_s.reshape(1, n_smp, CONV_K - 1, CONV_W))
```

```python
import functools

import jax
import jax.numpy as jnp
from jax import lax
from jax.experimental import pallas as pl
from jax.experimental.pallas import tpu as pltpu
from jax.experimental.pallas import tpu_sc as plsc

F32 = jnp.float32
BF16 = jnp.bfloat16
I32 = jnp.int32

D_MODEL = 1024
HALF_D = D_MODEL // 2
HEADS = 4
DK = 128
KEY_W = HEADS * DK
CONV_W = 512
CONV_K = 3
IN_W = 2 * KEY_W + 2 * KEY_W + 3 * CONV_W + 2 * D_MODEL
N_EXP = 64
TOP_K = 8
N_GRP = 8
GRP_SZ = N_EXP // N_GRP
TOPK_GRP = 4
EXP_FF = 256
SH_FF = 256
ROUTED_SCALE = 2.5
EPS = 1e-6

C_Q, C_F, C_I, C_G = 0, 512, 1024, 1536
C_BB, C_CC, C_VB = 2048, 2560, 3072
C_MGA, C_MGB = 3584, 4608

MIX_TILE = 512
SUB = 256
CHUNK = 64
ROUTE_TILE = 128
GMM_BM = 512
GMM_NX = 6
GMM_NY = 4
FINAL_TILE = 512
FINAL_SPLIT = (4, 12, 22, 32)
SC_WINDOW = 128
PLANE_W = HALF_D // 2
VMEM_LIMIT = 56 * 1024 * 1024


def _dot(a, b):
    return jnp.dot(a, b, preferred_element_type=F32)


def _dot_nt(a, b):
    return lax.dot_general(a, b, (((1,), (1,)), ((), ())), preferred_element_type=F32)


def _dot_tn(a, b):
    return lax.dot_general(a, b, (((0,), (0,)), ((), ())), preferred_element_type=F32)


def _sigmoid(x):
    return 0.5 * jnp.tanh(0.5 * x) + 0.5


def _silu(x):
    h = 0.5 * x
    return h * jnp.tanh(h) + h


def _rms(x):
    return x * lax.rsqrt(jnp.mean(x * x, axis=-1, keepdims=True) + EPS)


def _lower_bound(lbl):
    a, b = lbl[0:1], lbl[1:2]
    m = jnp.maximum(a, b)
    ea, eb = jnp.exp(a - m), jnp.exp(b - m)
    return ea / (ea + eb)


def _split3(x):
    hi = x.astype(BF16)
    r1 = x - hi.astype(F32)
    mid = r1.astype(BF16)
    lo = (r1 - mid.astype(F32)).astype(BF16)
    return hi, mid, lo


def _words(lo_b, hi_b):
    lo = lax.shift_right_logical(lax.bitcast_convert_type(lo_b.astype(F32), I32), 16)
    hi = lax.bitcast_convert_type(hi_b.astype(F32), I32) & jnp.int32(-65536)
    return lo | hi


def _halves(w):
    lo = lax.bitcast_convert_type(lax.shift_left(w, 16), F32)
    hi = lax.bitcast_convert_type(w & jnp.int32(-65536), F32)
    return lo.astype(BF16), hi.astype(BF16)


def _pack_rows(xb, out_ref):
    words = _words(xb[:, :HALF_D], xb[:, HALF_D:])
    out_ref[0] = words[:, :PLANE_W]
    out_ref[1] = words[:, PLANE_W:]


def _unpack_rows(p0, p1):
    c0, c2 = _halves(p0)
    c1, c3 = _halves(p1)
    return c0, c1, c2, c3


def _ada_kernel(cp_ref, cs_ref, w_ref, b_ref, op_ref, os_ref):
    w = w_ref[...].astype(BF16)
    for c_ref, o_ref in ((cp_ref, op_ref), (cs_ref, os_ref)):
        o_ref[...] = _dot(_silu(c_ref[...]).astype(BF16), w) + b_ref[...]


def _ada(c_prompt, c_sample, w_ada, b_ada):
    blk = 1024
    rows = lambda c: pl.BlockSpec((c.shape[0], D_MODEL), lambda j: (0, 0))
    cols = lambda c: pl.BlockSpec((c.shape[0], blk), lambda j: (0, j))
    return pl.pallas_call(
        _ada_kernel,
        grid=(6 * D_MODEL // blk,),
        in_specs=[rows(c_prompt), rows(c_sample),
                  pl.BlockSpec((D_MODEL, blk), lambda j: (0, j)),
                  pl.BlockSpec((1, blk), lambda j: (0, j))],
        out_specs=[cols(c_prompt), cols(c_sample)],
        out_shape=[jax.ShapeDtypeStruct((c.shape[0], 6 * D_MODEL), F32) for c in (c_prompt, c_sample)],
        name="ada",
    )(c_prompt, c_sample, w_ada, b_ada.reshape(1, -1))


def _ffn_pre(x1, mod_rows, gffn, w_sgu, w_sd, wr_hl):
    sh2, sc2, g2 = mod_rows
    h2 = _rms(x1) * gffn * (1.0 + sc2) + sh2
    h2b = h2.astype(BF16)
    gu = _dot(h2b, w_sgu)
    act = _silu(gu[:, :SH_FF]) * gu[:, SH_FF:]
    xmid = x1 + g2 * _dot(act.astype(BF16), w_sd)
    h2lo = (h2 - h2b.astype(F32)).astype(BF16)
    both = _dot_nt(wr_hl, h2b)
    lgt = both[:N_EXP] + both[N_EXP:] + _dot_nt(wr_hl[:N_EXP], h2lo)
    return xmid, h2b, lgt


def _mix_kernel(*refs, nt, n_tiles):
    i = pl.program_id(0)
    rbias_ref, h2_ref = refs[15], refs[17]
    idx_ref, w_ref, rank_ref, cnt_ref, lgt_ref = refs[18], refs[19], refs[20], refs[21], refs[-1]

    @pl.when(i == 0)
    def _():
        lgt_ref[...] = jnp.zeros_like(lgt_ref)
        cnt_ref[...] = jnp.zeros_like(cnt_ref)

    def route_previous(keep):
        carry = cnt_ref[:, 0:1]
        for c0 in range(0, lgt_ref.shape[1], ROUTE_TILE):
            cols = slice(c0, c0 + ROUTE_TILE)
            picks, weights, ranks, carry = _route_tile(lgt_ref[:, cols], rbias_ref[...], carry)
            for k in range(TOP_K):
                idx_ref[k:k + 1, cols] = picks[k]
                w_ref[k:k + 1, cols] = weights[k]
                rank_ref[k:k + 1, cols] = ranks[k]
            yield
        cnt_ref[...] = jnp.broadcast_to(jnp.where(keep, carry, 0.0), cnt_ref.shape)

    @pl.when(i == n_tiles)
    def _():
        h2_ref[...] = jnp.zeros_like(h2_ref)
        for _ in route_previous(True):
            pass

    @pl.when(i < n_tiles)
    def _():
        _mix_tile(i % nt, nt, route_previous(i > 0), *refs)


def _mix_tile(t, nt, routing, x_ref, mod_ref, gmix_ref, gffn_ref, w_in_ref, lbl_ref, hg_ref, cw_ref, cb_ref,
              w_oh_ref, w_oc_ref, w_o_ref, wr_hl_ref, w_sgu_ref, w_sd_ref, rbias_ref,
              xmid_ref, h2_ref, idx_ref, w_ref, rank_ref, cnt_ref, s_out_ref, cv_out_ref,
              proj_ref, st_ref, cbuf_ref, ya_ref, lgt_ref):
    del rbias_ref, idx_ref, w_ref, rank_ref, cnt_ref
    tt = x_ref.shape[1]

    @pl.when(t == 0)
    def _():
        st_ref[...] = jnp.zeros_like(st_ref)
        cbuf_ref[...] = jnp.zeros_like(cbuf_ref)

    x = x_ref[0]
    mod = mod_ref[0]
    sh1, sc1, g1 = mod[0:1], mod[1:2], mod[2:3]
    h = _rms(x) * gmix_ref[...] * (1.0 + sc1) + sh1
    hb = h.astype(BF16)
    for c in range(0, IN_W, 512):
        proj_ref[:, c:c + 512] = _dot(hb, w_in_ref[:, c:c + 512])
        if c % 1024 == 512:
            next(routing, None)
    for _ in routing:
        pass

    lb = _lower_bound(lbl_ref[...])
    row = lax.broadcasted_iota(I32, (SUB, SUB), 0)
    col = lax.broadcasted_iota(I32, (SUB, SUB), 1)
    tri = (col <= row).astype(BF16)
    mask_d = (col <= row) & (row // CHUNK == col // CHUNK)
    mask_a = row // (2 * CHUNK) == col // (2 * CHUNK)
    n_ch = SUB // CHUNK

    def by_chunk(vals):
        return jnp.concatenate([jnp.zeros((CHUNK, DK), F32) if v is None
                                else jnp.broadcast_to(v, (CHUNK, DK)) for v in vals], axis=0)

    for s in range(tt // SUB):
        r0 = s * SUB
        f = lb + (1.0 - lb) * _sigmoid(proj_ref[r0:r0 + SUB, C_F:C_F + KEY_W])
        kk = 1.0 - f
        hi, mid, lo = _split3(jnp.log(f))
        bc = _dot(tri, hi) + _dot(tri, mid) + _dot(tri, lo)
        for hd in range(HEADS):
            hs = slice(hd * DK, (hd + 1) * DK)
            bh = bc[:, hs]
            at = lambda r: bh[r:r + 1]
            mids = [at(c * CHUNK + CHUNK // 2 - 1) for c in range(n_ch)]
            pair_mid = [at(CHUNK - 1), at(3 * CHUNK - 1)]
            step_mid, step_end = at(2 * CHUNK - 1), at(SUB - 1)
            arg = bh - by_chunk(mids)
            e_pos, e_neg = jnp.exp(arg), jnp.exp(-arg)
            q = _silu(proj_ref[r0:r0 + SUB, C_Q + hd * DK:C_Q + (hd + 1) * DK])
            v = proj_ref[r0:r0 + SUB, C_I + hd * DK:C_I + (hd + 1) * DK]
            qd = q * e_pos
            kd = kk[:, hs] * e_neg
            q_in = qd * by_chunk([jnp.exp(m) for m in mids])
            k_end = kd * by_chunk([jnp.exp(step_end - m) for m in mids])
            qa = qd * by_chunk([None, jnp.exp(mids[1] - pair_mid[0]), None, jnp.exp(mids[3] - pair_mid[1])])
            ka = kd * by_chunk([jnp.exp(pair_mid[0] - mids[0]), None, jnp.exp(pair_mid[1] - mids[2]), None])
            qb = qd * by_chunk([None, None, jnp.exp(mids[2] - step_mid), jnp.exp(mids[3] - step_mid)])
            kb = kd * by_chunk([jnp.exp(step_mid - mids[0]), jnp.exp(step_mid - mids[1]), None, None])
            att = jnp.where(mask_d, _dot_nt(qd.astype(BF16), kd.astype(BF16)), 0.0)
            att = att + jnp.where(mask_a, _dot_nt(qa.astype(BF16), ka.astype(BF16)), 0.0)
            att = att + _dot_nt(qb.astype(BF16), kb.astype(BF16))
            vb = v.astype(BF16)
            st = st_ref[hd]
            o = _dot(att.astype(BF16), vb) + _dot_nt(q_in.astype(BF16), st.astype(BF16))
            st_ref[hd] = st * jnp.exp(step_end) + _dot_tn(vb, k_end.astype(BF16))
            gate = _silu(proj_ref[r0:r0 + SUB, C_G + hd * DK:C_G + (hd + 1) * DK])
            ya_ref[r0:r0 + SUB, hs] = _rms(o) * hg_ref[:, hs] * gate

    u = proj_ref[:, C_CC:C_CC + CONV_W] * proj_ref[:, C_VB:C_VB + CONV_W]
    rows = lax.broadcasted_iota(I32, (tt, CONV_W), 0)
    c0, c1 = cbuf_ref[0:1], cbuf_ref[1:2]
    u1 = jnp.where(rows == 0, c1, pltpu.roll(u, 1, axis=0))
    u2 = jnp.where(rows == 0, c0, jnp.where(rows == 1, c1, pltpu.roll(u, 2, axis=0)))
    conv = cw_ref[0:1] * u2 + cw_ref[1:2] * u1 + cw_ref[2:3] * u + cb_ref[...]
    yb = proj_ref[:, C_BB:C_BB + CONV_W] * conv
    cbuf_ref[...] = u[tt - 2:tt]

    mixed = (_sigmoid(proj_ref[:, C_MGA:C_MGA + D_MODEL]) * _dot(ya_ref[...].astype(BF16), w_oh_ref[...])
             + _sigmoid(proj_ref[:, C_MGB:C_MGB + D_MODEL]) * _dot(yb.astype(BF16), w_oc_ref[...]))
    x1 = x + g1 * _dot(mixed.astype(BF16), w_o_ref[...])

    xmid, h2b, lgt = _ffn_pre(x1, (mod[3:4], mod[4:5], mod[5:6]), gffn_ref[...],
                              w_sgu_ref[...], w_sd_ref[...], wr_hl_ref[...])
    xmid_ref[0] = xmid
    _pack_rows(h2b, h2_ref)
    lgt_ref[...] = lgt

    @pl.when(t == nt - 1)
    def _():
        for hd in range(HEADS):
            s_out_ref[0, hd] = st_ref[hd].T
        cv_out_ref[0] = cbuf_ref[...]


def _const_spec(shape):
    nd = len(shape)
    return pl.BlockSpec(shape, lambda i, _nd=nd: (0,) * _nd, pipeline_mode=pl.Buffered(1))


def _mix(x, mod, n_tok, gmix, gffn, w_in, lbl, hg, cw, cb, w_oh, w_oc, w_o, wr_hl, w_sgu, w_sd, rbias):
    bsz, seq, _ = x.shape
    tt = MIX_TILE
    nt = seq // tt
    n_tiles = bsz * nt
    assert n_tiles * tt < n_tok <= (n_tiles + 1) * tt and tt % ROUTE_TILE == 0
    consts = [gmix, gffn, w_in, lbl, hg, cw, cb, w_oh, w_oc, w_o, wr_hl, w_sgu, w_sd, rbias]
    tile = lambda i: jnp.minimum(i, n_tiles - 1)
    routed = pl.BlockSpec((TOP_K, tt), lambda i: (0, jnp.maximum(i - 1, 0)))
    slot = lambda dt: jax.ShapeDtypeStruct((TOP_K, n_tiles * tt), dt)
    return pl.pallas_call(
        functools.partial(_mix_kernel, nt=nt, n_tiles=n_tiles),
        grid=(n_tiles + 1,),
        in_specs=[pl.BlockSpec((1, tt, D_MODEL), lambda i: (tile(i) // nt, tile(i) % nt, 0)),
                  pl.BlockSpec((1, 6, D_MODEL), lambda i: (tile(i) // nt, 0, 0))]
                 + [_const_spec(a.shape) for a in consts],
        out_specs=[pl.BlockSpec((1, tt, D_MODEL), lambda i: (tile(i) // nt, tile(i) % nt, 0)),
                   pl.BlockSpec((2, tt, PLANE_W), lambda i: (0, i, 0)),
                   routed, routed, routed,
                   pl.BlockSpec((N_EXP, 128), lambda i: (0, 0)),
                   pl.BlockSpec((1, HEADS, DK, DK), lambda i: (tile(i) // nt, 0, 0, 0)),
                   pl.BlockSpec((1, CONV_K - 1, CONV_W), lambda i: (tile(i) // nt, 0, 0))],
        out_shape=[jax.ShapeDtypeStruct((bsz, seq, D_MODEL), F32),
                   jax.ShapeDtypeStruct((2, n_tok, PLANE_W), I32),
                   slot(I32), slot(F32), slot(I32),
                   jax.ShapeDtypeStruct((N_EXP, 128), F32),
                   jax.ShapeDtypeStruct((bsz, HEADS, DK, DK), F32),
                   jax.ShapeDtypeStruct((bsz, CONV_K - 1, CONV_W), F32)],
        scratch_shapes=[pltpu.VMEM((tt, IN_W), F32),
                        pltpu.VMEM((HEADS, DK, DK), F32),
                        pltpu.VMEM((CONV_K - 1, CONV_W), F32),
                        pltpu.VMEM((tt, KEY_W), F32),
                        pltpu.VMEM((N_EXP, tt), F32)],
        compiler_params=pltpu.CompilerParams(
            dimension_semantics=("arbitrary",), vmem_limit_bytes=VMEM_LIMIT),
        name="mix",
    )(x, mod, *consts)


def _smp1_kernel(x_ref, mod_ref, gmix_ref, w_in_ref, lbl_ref, cw_ref, cb_ref, cst_ref,
                 f_ref, k_ref, q_ref, v_ref, gate_ref, yb_ref, sga_ref, sgb_ref, cv_out_ref):
    x = x_ref[...]
    sh1, sc1 = mod_ref[:, 0:D_MODEL], mod_ref[:, D_MODEL:2 * D_MODEL]
    h = _rms(x) * gmix_ref[...] * (1.0 + sc1) + sh1
    hb = h.astype(BF16)

    def proj(c, w):
        return _dot(hb, w_in_ref[:, c:c + w])

    lb = _lower_bound(lbl_ref[...])
    f = lb + (1.0 - lb) * _sigmoid(proj(C_F, KEY_W))
    f_ref[...] = f
    k_ref[...] = 1.0 - f
    q_ref[...] = _silu(proj(C_Q, KEY_W))
    v_ref[...] = proj(C_I, KEY_W)
    gate_ref[...] = _silu(proj(C_G, KEY_W))
    u = proj(C_CC, CONV_W) * proj(C_VB, CONV_W)
    c0, c1 = cst_ref[:, 0:CONV_W], cst_ref[:, CONV_W:2 * CONV_W]
    conv = cw_ref[0:1] * c0 + cw_ref[1:2] * c1 + cw_ref[2:3] * u + cb_ref[...]
    yb_ref[...] = proj(C_BB, CONV_W) * conv
    cv_out_ref[:, 0:CONV_W] = c1
    cv_out_ref[:, CONV_W:2 * CONV_W] = u
    sga_ref[...] = _sigmoid(proj(C_MGA, D_MODEL))
    sgb_ref[...] = _sigmoid(proj(C_MGB, D_MODEL))


def _smp1(x, mod, gmix, w_in, lbl, cw, cb, cst):
    n = x.shape[0]
    kw = jax.ShapeDtypeStruct((n, KEY_W), F32)
    dm = jax.ShapeDtypeStruct((n, D_MODEL), F32)
    return pl.pallas_call(
        _smp1_kernel,
        out_shape=[kw, kw, kw, kw, kw, kw, dm, dm,
                   jax.ShapeDtypeStruct((n, (CONV_K - 1) * CONV_W), F32)],
        compiler_params=pltpu.CompilerParams(vmem_limit_bytes=VMEM_LIMIT),
        name="smp1",
    )(x, mod, gmix, w_in, lbl, cw, cb, cst)


def _smp2(f, k, q, v, state):
    n = f.shape[0]
    info = plsc.get_sparse_core_info()
    lanes = info.num_lanes
    n_workers = info.num_cores * info.num_subcores
    tok_per = n // n_workers
    n_chunks = DK // lanes
    vec = lambda: pltpu.VMEM((DK,), F32)

    @pl.kernel(out_type=[jax.ShapeDtypeStruct(state.shape, F32), jax.ShapeDtypeStruct((n, KEY_W), F32)],
               mesh=_sc_mesh(), scratch_types=[pltpu.VMEM((DK, DK), F32), vec(), vec(), vec(), vec(), vec()],
               compiler_params=pltpu.CompilerParams(needs_layout_passes=False), name="smp2")
    def run(f_hbm, k_hbm, q_hbm, v_hbm, s_hbm, s_out_hbm, o_hbm, s_v, f_v, k_v, q_v, v_v, o_v):
        wid = lax.axis_index("subcore") * info.num_cores + lax.axis_index("core")

        def tile(j, carry):
            t = wid * tok_per + j // HEADS
            hcols = pl.ds((j % HEADS) * DK, DK)
            pltpu.sync_copy(s_hbm.at[t, j % HEADS], s_v)
            for src, dst in ((f_hbm, f_v), (k_hbm, k_v), (q_hbm, q_v), (v_hbm, v_v)):
                pltpu.sync_copy(src.at[t, hcols], dst)
            v_chunks = [v_v[pl.ds(c * lanes, lanes)] for c in range(n_chunks)]

            def row(d, acc):
                at = [jnp.zeros((lanes,), I32) + d]
                fd, kd, qd = (plsc.load_gather(r, at) for r in (f_v, k_v, q_v))
                out = []
                for c in range(n_chunks):
                    cols = pl.ds(c * lanes, lanes)
                    new = s_v[d, cols] * fd + kd * v_chunks[c]
                    s_v[d, cols] = new
                    out.append(acc[c] + qd * new)
                return tuple(out)

            acc = lax.fori_loop(0, DK, row, tuple(jnp.zeros((lanes,), F32) for _ in range(n_chunks)))
            for c in range(n_chunks):
                o_v[pl.ds(c * lanes, lanes)] = acc[c]
            pltpu.sync_copy(s_v, s_out_hbm.at[t, j % HEADS])
            pltpu.sync_copy(o_v, o_hbm.at[t, hcols])
            return carry

        lax.fori_loop(0, tok_per * HEADS, tile, 0)

    return run(f, k, q, v, state)


def _smp3_kernel(x_ref, mod_ref, o_ref, gate_ref, yb_ref, sga_ref, sgb_ref, hg_ref, gffn_ref,
                 w_oh_ref, w_oc_ref, w_o_ref, wr_hl_ref, w_sgu_ref, w_sd_ref,
                 rbias_ref, cnt_p_ref, idx_p_ref, w_p_ref, rank_p_ref, h2_all_ref,
                 xmid_ref, h2_ref, idx_ref, w_ref, rank_ref, cnt_ref):
    del h2_all_ref
    parts = []
    for hd in range(HEADS):
        hs = slice(hd * DK, (hd + 1) * DK)
        parts.append(_rms(o_ref[:, hs]) * hg_ref[:, hs] * gate_ref[:, hs])
    ya = jnp.concatenate(parts, axis=1)
    mixed = (sga_ref[...] * _dot(ya.astype(BF16), w_oh_ref[...])
             + sgb_ref[...] * _dot(yb_ref[...].astype(BF16), w_oc_ref[...]))
    g1 = mod_ref[:, 2 * D_MODEL:3 * D_MODEL]
    x1 = x_ref[...] + g1 * _dot(mixed.astype(BF16), w_o_ref[...])
    mod_rows = tuple(mod_ref[:, j * D_MODEL:(j + 1) * D_MODEL] for j in (3, 4, 5))
    xmid, h2b, lgt = _ffn_pre(x1, mod_rows, gffn_ref[...], w_sgu_ref[...], w_sd_ref[...],
                              wr_hl_ref[...])
    xmid_ref[...] = xmid
    _pack_rows(h2b, h2_ref)
    n_prompt = idx_p_ref.shape[1]
    picks, weights, ranks, total = _route_tile(lgt, rbias_ref[...], cnt_p_ref[:, 0:1])
    for src, dst, new in ((idx_p_ref, idx_ref, picks), (w_p_ref, w_ref, weights), (rank_p_ref, rank_ref, ranks)):
        dst[:, :n_prompt] = src[...]
        for k in range(TOP_K):
            dst[k:k + 1, n_prompt:] = new[k]
    cnt_ref[...] = jnp.broadcast_to(total, cnt_ref.shape).astype(I32)


def _smp3(x, mod, o, gate, yb, sga, sgb, hg, gffn, w_oh, w_oc, w_o, wr_hl, w_sgu, w_sd,
          rbias, cnt_p, idx_p, w_p, rank_p, h2_all):
    n = x.shape[0]
    vmem_args = [x, mod, o, gate, yb, sga, sgb, hg, gffn, w_oh, w_oc, w_o, wr_hl, w_sgu, w_sd,
                 rbias, cnt_p, idx_p, w_p, rank_p]
    n_prompt = idx_p.shape[1]
    blk = n_prompt // n
    slot = lambda dt: jax.ShapeDtypeStruct((TOP_K, n_prompt + n), dt)

    def full(a):
        nd = a.ndim
        return pl.BlockSpec(a.shape, lambda i, _nd=nd: (0,) * _nd)

    return pl.pallas_call(
        _smp3_kernel,
        grid=(1,),
        in_specs=[full(a) for a in vmem_args] + [pl.BlockSpec(memory_space=pl.ANY)],
        out_specs=[pl.BlockSpec((n, D_MODEL), lambda i: (0, 0)),
                   pl.BlockSpec((2, n, PLANE_W), lambda i: (0, blk, 0)),
                   pl.BlockSpec((TOP_K, n_prompt + n), lambda i: (0, 0)),
                   pl.BlockSpec((TOP_K, n_prompt + n), lambda i: (0, 0)),
                   pl.BlockSpec((TOP_K, n_prompt + n), lambda i: (0, 0)),
                   pl.BlockSpec((N_EXP, 128), lambda i: (0, 0))],
        out_shape=[jax.ShapeDtypeStruct((n, D_MODEL), F32),
                   jax.ShapeDtypeStruct(h2_all.shape, h2_all.dtype),
                   slot(I32), slot(F32), slot(I32),
                   jax.ShapeDtypeStruct((N_EXP, 128), I32)],
        input_output_aliases={len(vmem_args): 1},
        compiler_params=pltpu.CompilerParams(
            dimension_semantics=("arbitrary",), vmem_limit_bytes=VMEM_LIMIT),
        name="smp3",
    )(*vmem_args, h2_all)


def _route_tile(lgt, bias, carry):
    tr = lgt.shape[1]
    neg = -jnp.inf
    scores = _sigmoid(lgt)
    sel = scores + bias
    j8 = lax.broadcasted_iota(I32, (GRP_SZ, tr), 0)
    groups = [sel[g * GRP_SZ:(g + 1) * GRP_SZ] for g in range(N_GRP)]
    gscore = []
    for grp in groups:
        m1 = jnp.max(grp, axis=0, keepdims=True)
        i1 = jnp.min(jnp.where(grp == m1, j8, GRP_SZ), axis=0, keepdims=True)
        m2 = jnp.max(jnp.where(j8 == i1, neg, grp), axis=0, keepdims=True)
        gscore.append(m1 + m2)
    kept = []
    for g in range(N_GRP):
        beaten = jnp.zeros((1, tr), I32)
        for o in range(N_GRP):
            if o < g:
                beaten = beaten + (gscore[o] >= gscore[g]).astype(I32)
            elif o > g:
                beaten = beaten + (gscore[o] > gscore[g]).astype(I32)
        kept.append(jnp.where(beaten < TOPK_GRP, groups[g], neg))
    masked = jnp.concatenate(kept, axis=0)
    ei = lax.broadcasted_iota(I32, masked.shape, 0)
    chosen = jnp.zeros(masked.shape, jnp.bool_)
    picks, weights = [], []
    for _ in range(TOP_K):
        m = jnp.max(masked, axis=0, keepdims=True)
        pick = jnp.min(jnp.where((masked == m) & ~chosen, ei, N_EXP), axis=0, keepdims=True)
        hit = ei == pick
        weights.append(jnp.sum(jnp.where(hit, scores, 0.0), axis=0, keepdims=True))
        picks.append(pick)
        chosen = chosen | hit
        masked = jnp.where(hit, neg, masked)
    wsum = weights[0]
    for w in weights[1:]:
        wsum = wsum + w
    sel01 = chosen.astype(F32)
    r = lax.broadcasted_iota(I32, (tr, tr), 0)
    c = lax.broadcasted_iota(I32, (tr, tr), 1)
    before = (r < c).astype(BF16)
    cnt = _dot(sel01.astype(BF16), before) + carry
    weights = [w / wsum * ROUTED_SCALE for w in weights]
    ranks = [jnp.sum(jnp.where(ei == p, cnt, 0.0), axis=0, keepdims=True).astype(I32) for p in picks]
    return picks, weights, ranks, carry + jnp.sum(sel01, axis=1, keepdims=True)


def _dest_kernel(start_ref, idx_ref, rank_ref, all_ref, *chunk_refs, n_rows, chunks):
    n_tok = idx_ref.shape[1]
    idx = idx_ref[...]
    acc = rank_ref[...]
    for e in range(N_EXP):
        acc = acc + jnp.where(idx == e, start_ref[e], 0)
    for k in range(TOP_K):
        for p in range(2):
            row = acc[k:k + 1, :] + p * n_rows
            seg = 2 * k + p
            all_ref[:, seg * n_tok:(seg + 1) * n_tok] = row
            for (c0, c1), ref in zip(chunks, chunk_refs):
                ref[:, seg * (c1 - c0):(seg + 1) * (c1 - c0)] = row[:, c0:c1]


def _dest(pad_start, idx, rank, n_rows, chunks):
    k, n_tok = idx.shape
    vmem = pl.BlockSpec(memory_space=pltpu.VMEM)
    sizes = [n_tok] + [c1 - c0 for c0, c1 in chunks]
    return pl.pallas_call(
        functools.partial(_dest_kernel, n_rows=n_rows, chunks=chunks),
        in_specs=[pl.BlockSpec(memory_space=pltpu.SMEM), vmem, vmem],
        out_specs=[vmem] * len(sizes),
        out_shape=[jax.ShapeDtypeStruct((1, 2 * k * n), I32) for n in sizes],
        name="dest",
    )(pad_start, idx, rank)


def _sc_mesh():
    return plsc.VectorSubcoreMesh(core_axis_name="core", subcore_axis_name="subcore")


def _dispatch(rows, dest, n_out):
    n, width = rows.shape
    win = SC_WINDOW
    steps = n // win

    @pl.kernel(out_type=jax.ShapeDtypeStruct((n_out, width), rows.dtype), mesh=_sc_mesh(),
               scratch_types=[], name="dispatch")
    def run(x_hbm, *refs):
        i_hbms, o_hbm = refs[:TOP_K], refs[TOP_K]

        def body(x_vmem, *i_vmems):
            for i_vmem in i_vmems:
                pltpu.sync_copy(x_vmem, o_hbm.at[i_vmem.at[0]])

        pltpu.emit_pipeline(
            body,
            grid=(steps,),
            in_specs=[pl.BlockSpec((win, width), lambda i: (i, 0))]
                     + [pl.BlockSpec((1, win), lambda i, k=k: (0, k * steps + i)) for k in range(TOP_K)],
            out_specs=[],
            core_axis_name=("core", "subcore"),
            dimension_semantics=(pltpu.PARALLEL,),
        )(x_hbm, *i_hbms)

    assert dest.shape == (1, TOP_K * n)
    return run(rows, *([dest] * TOP_K))


def _combine(rows, dest_flat):
    width = rows.shape[1]
    n = dest_flat.shape[1]
    win = SC_WINDOW

    @pl.kernel(out_type=jax.ShapeDtypeStruct((n, width), rows.dtype), mesh=_sc_mesh(),
               scratch_types=[], name="combine")
    def run(y_hbm, i_hbm, o_hbm):
        def body(i_vmem, o_vmem):
            pltpu.sync_copy(y_hbm.at[i_vmem.at[0]], o_vmem)

        pltpu.emit_pipeline(
            body,
            grid=(n // win,),
            in_specs=[pl.BlockSpec((1, win), lambda i: (0, i))],
            out_specs=[pl.BlockSpec((win, width), lambda i: (i, 0))],
            core_axis_name=("core", "subcore"),
            dimension_semantics=(pltpu.PARALLEL,),
        )(i_hbm, o_hbm)

    return run(rows, dest_flat)


def _gmm_kernel(blk_exp_ref, n_used_ref, xs_hbm, wg_hbm, wu_hbm, wd_hbm, ys_hbm,
                xbuf, ybuf, wg32, wu32, wd32, wgu_b, wd_b, xsem, ysem, wsem, run_ref):
    nx, ny, bm = xbuf.shape[0], ybuf.shape[0], xbuf.shape[2]
    n_used = n_used_ref[0]

    def x_copies(b):
        rows, slot = pl.ds(b * bm, bm), b % nx
        return [pltpu.make_async_copy(xs_hbm.at[p, rows, :], xbuf.at[slot, p], xsem.at[slot, p]) for p in range(2)]

    def y_copies(b):
        rows, slot = pl.ds(b * bm, bm), b % ny
        return [pltpu.make_async_copy(ybuf.at[slot, p], ys_hbm.at[p, rows, :], ysem.at[slot, p]) for p in range(2)]

    def start(copies):
        for c in copies:
            c.start()

    def wait(copies):
        for c in copies:
            c.wait()

    def w_copies(e, slot):
        return (pltpu.make_async_copy(wg_hbm.at[e], wg32.at[slot], wsem.at[slot, 0]),
                pltpu.make_async_copy(wu_hbm.at[e], wu32.at[slot], wsem.at[slot, 1]),
                pltpu.make_async_copy(wd_hbm.at[e], wd32.at[slot], wsem.at[slot, 2]))

    def run_end(b):
        return lax.while_loop(lambda j: (j < n_used) & (blk_exp_ref[jnp.minimum(j, n_used - 1)] == blk_exp_ref[b]),
                              lambda j: j + 1, b + 1)

    run_ref[0] = 0
    start(x_copies(0))
    start(w_copies(blk_exp_ref[0], 0))
    for j in range(1, nx - 1):
        @pl.when(j < n_used)
        def _():
            start(x_copies(j))

    def block(b, carry):
        @pl.when(b + nx - 1 < n_used)
        def _():
            start(x_copies(b + nx - 1))

        @pl.when((b == 0) | (blk_exp_ref[b] != blk_exp_ref[jnp.maximum(b - 1, 0)]))
        def _():
            wslot = run_ref[0] % 2
            run_ref[0] = run_ref[0] + 1
            wait(w_copies(blk_exp_ref[b], wslot))
            wgu_b[:, 0:EXP_FF] = wg32[wslot].astype(BF16)
            wgu_b[:, EXP_FF:2 * EXP_FF] = wu32[wslot].astype(BF16)
            wd_b[...] = wd32[wslot].astype(BF16)
            nxt = run_end(b)

            @pl.when(nxt < n_used)
            def _():
                start(w_copies(blk_exp_ref[jnp.minimum(nxt, n_used - 1)], 1 - wslot))

        wait(x_copies(b))

        @pl.when(b >= ny)
        def _():
            wait(y_copies(b - ny))

        xslot = b % nx
        xc = _unpack_rows(xbuf[xslot, 0], xbuf[xslot, 1])
        gu = sum(_dot(c, wgu_b[i * PLANE_W:(i + 1) * PLANE_W, :]) for i, c in enumerate(xc))
        act = (_silu(gu[:, :EXP_FF]) * gu[:, EXP_FF:]).astype(BF16)
        _pack_rows(_dot(act, wd_b[...]).astype(BF16), ybuf.at[b % ny])
        start(y_copies(b))
        return carry

    lax.fori_loop(0, n_used, block, 0)

    for j in range(ny, 0, -1):
        @pl.when(n_used >= j)
        def _():
            wait(y_copies(n_used - j))


def _gmm(blk_exp, n_used, xs, w_gate, w_up, w_down):
    n_rows = xs.shape[1]
    bm = GMM_BM
    nb = n_rows // bm

    assert blk_exp.shape == (nb,)
    any_spec = pl.BlockSpec(memory_space=pl.ANY)
    grid_spec = pltpu.PrefetchScalarGridSpec(
        num_scalar_prefetch=2,
        grid=(1,),
        in_specs=[any_spec, any_spec, any_spec, any_spec],
        out_specs=any_spec,
        scratch_shapes=[pltpu.VMEM((GMM_NX, 2, bm, PLANE_W), I32), pltpu.VMEM((GMM_NY, 2, bm, PLANE_W), I32),
                        pltpu.VMEM((2, D_MODEL, EXP_FF), F32), pltpu.VMEM((2, D_MODEL, EXP_FF), F32),
                        pltpu.VMEM((2, EXP_FF, D_MODEL), F32),
                        pltpu.VMEM((D_MODEL, 2 * EXP_FF), BF16), pltpu.VMEM((EXP_FF, D_MODEL), BF16),
                        pltpu.SemaphoreType.DMA((GMM_NX, 2)), pltpu.SemaphoreType.DMA((GMM_NY, 2)),
                        pltpu.SemaphoreType.DMA((2, 3)), pltpu.SMEM((1,), I32)],
    )
    return pl.pallas_call(
        _gmm_kernel,
        grid_spec=grid_spec,
        out_shape=jax.ShapeDtypeStruct((2, n_rows, PLANE_W), I32),
        compiler_params=pltpu.CompilerParams(
            dimension_semantics=("arbitrary",), vmem_limit_bytes=VMEM_LIMIT),
        name="gmm",
    )(blk_exp, n_used, xs, w_gate, w_up, w_down)


def _final_kernel(xmid_ref, g2_ref, z_ref, w_ref, gfin_ref, *rest):
    y_ref = rest[-1]
    accs = [jnp.zeros((xmid_ref.shape[0], PLANE_W), F32) for _ in range(4)]
    w_cols = w_ref[...].T
    for k in range(TOP_K):
        wk = w_cols[:, k:k + 1]
        cols = _unpack_rows(z_ref[k, 0], z_ref[k, 1])
        accs = [a + wk * c.astype(F32) for a, c in zip(accs, cols)]
    acc = jnp.concatenate(accs, axis=1)
    y_ref[...] = _rms(xmid_ref[...] + g2_ref[0] * acc) * gfin_ref[...]


def _final(xmid, g2, z, w_t, gfin, tile, *, n_tiles, x_tile0, z_tile0, w_tile0, tiles_per_g2, y_prev=None):
    args = [xmid, g2, z, w_t, gfin]
    in_specs = [pl.BlockSpec((tile, D_MODEL), lambda i: (x_tile0 + i, 0)),
                pl.BlockSpec((1, g2.shape[1], D_MODEL), lambda i: ((x_tile0 + i) // tiles_per_g2, 0, 0)),
                pl.BlockSpec((TOP_K, 2, tile, PLANE_W), lambda i: (0, 0, z_tile0 + i, 0)),
                pl.BlockSpec((TOP_K, tile), lambda i: (0, w_tile0 + i)),
                pl.BlockSpec((1, D_MODEL), lambda i: (0, 0))]
    aliases = {}
    if y_prev is not None:
        args.append(y_prev)
        in_specs.append(pl.BlockSpec(memory_space=pl.ANY))
        aliases = {len(args) - 1: 0}
    return pl.pallas_call(
        _final_kernel,
        grid=(n_tiles,),
        in_specs=in_specs,
        out_specs=pl.BlockSpec((tile, D_MODEL), lambda i: (x_tile0 + i, 0)),
        out_shape=jax.ShapeDtypeStruct(xmid.shape, F32),
        input_output_aliases=aliases,
        compiler_params=pltpu.CompilerParams(dimension_semantics=("arbitrary",)),
        name="final",
    )(*args)


def kernel(x_prompt, x_sample, state_hgrn, state_conv, c_prompt, c_sample, w_ada, b_ada, norm_mix_g, norm_ffn_g, w_in, lb_logits, hgrn_norm_g, conv_w, conv_b, w_out_hgrn, w_out_conv, w_o, w_router, router_bias, w_exp_gate, w_exp_up, w_exp_down, w_sh_gate, w_sh_up, w_sh_down, final_norm_g):
    assert w_ada.shape[0] == 1 and lb_logits.shape[0] == 2
    bsz, seq, _ = x_prompt.shape
    n_smp = x_sample.shape[0]
    n_prompt = bsz * seq
    n_tok = n_prompt + n_smp

    w_in_b = w_in[0].astype(BF16)
    w_oh_b = w_out_hgrn[0].astype(BF16)
    w_oc_b = w_out_conv[0].astype(BF16)
    w_o_b = w_o[0].astype(BF16)
    wr_t = w_router[0].T
    wr_hi = wr_t.astype(BF16)
    wr_hl = jnp.concatenate([wr_hi, (wr_t - wr_hi.astype(F32)).astype(BF16)], axis=0)
    w_sgu = jnp.concatenate([w_sh_gate[0], w_sh_up[0]], axis=1).astype(BF16)
    w_sd = w_sh_down[0].astype(BF16)
    gmix = norm_mix_g[0].reshape(1, D_MODEL)
    gffn = norm_ffn_g[0].reshape(1, D_MODEL)
    hg = hgrn_norm_g[0].reshape(1, KEY_W)
    cw = conv_w[0]
    cb = conv_b[0].reshape(1, CONV_W)
    gfin = final_norm_g.reshape(1, D_MODEL)

    mod_p, mod_s = _ada(c_prompt, c_sample, w_ada[0], b_ada[0])

    rbias = router_bias[0].reshape(N_EXP, 1)
    xmid_p, h2_all, idx_p, w_p, rank_p, cnt_p, s_p, cv_p = _mix(
        x_prompt, mod_p.reshape(bsz, 6, D_MODEL), n_tok, gmix, gffn, w_in_b, lb_logits, hg, cw, cb, w_oh_b, w_oc_b, w_o_b, wr_hl, w_sgu, w_sd, rbias)

    xs2 = x_sample.reshape(n_smp, D_MODEL)
    f, kk, q, v, gate, yb, sga, sgb, cv_s = _smp1(
        xs2, mod_s, gmix, w_in_b, lb_logits, cw, cb, state_conv[0].reshape(n_smp, (CONV_K - 1) * CONV_W))
    s_s, o_s = _smp2(f, kk, q, v, state_hgrn[0])
    xmid_s, h2_all, idx, w_tok, rank, cnt = _smp3(xs2, mod_s, o_s, gate, yb, sga, sgb, hg, gffn,
                                                  w_oh_b, w_oc_b, w_o_b, wr_hl, w_sgu, w_sd,
                                                  rbias, cnt_p, idx_p, w_p, rank_p, h2_all)

    bm = GMM_BM
    n_blocks = (n_tok * TOP_K + N_EXP * (bm - 1)) // bm
    n_rows = n_blocks * bm
    counts = cnt[:, 0]
    padded = (counts + bm - 1) // bm * bm
    pad_end = jnp.cumsum(padded)
    pad_start = pad_end - padded
    blk_row0 = jnp.arange(n_blocks, dtype=I32) * bm
    blk_exp = jnp.minimum(jnp.sum((pad_end[None, :] <= blk_row0[:, None]).astype(I32), axis=1), N_EXP - 1)
    n_used = (pad_end[-1:] // bm).astype(I32)
    cuts = [0] + [n_prompt * f // FINAL_SPLIT[-1] for f in FINAL_SPLIT]
    assert all(c % FINAL_TILE == 0 for c in cuts) and cuts[-1] == n_prompt
    chunks = tuple((c0, n_tok if c1 == n_prompt else c1) for c0, c1 in zip(cuts[:-1], cuts[1:]))
    dest, *chunk_dest = _dest(pad_start.astype(I32), idx, rank, n_rows, chunks)

    xs = _dispatch(h2_all.reshape(2 * n_tok, PLANE_W), dest, 2 * n_rows).reshape(2, n_rows, PLANE_W)
    ys = _gmm(blk_exp, n_used, xs, w_exp_gate[0], w_exp_up[0], w_exp_down[0])

    ys_flat = ys.reshape(2 * n_rows, PLANE_W)
    w_t = w_tok
    xmid_p2 = xmid_p.reshape(n_prompt, D_MODEL)
    g2_p = mod_p[:, 5 * D_MODEL:].reshape(bsz, 1, D_MODEL)
    g2_s = mod_s[:, 5 * D_MODEL:].reshape(1, n_smp, D_MODEL)
    y_p = None
    for (c0, c1), dest_c in zip(chunks, chunk_dest):
        z = _combine(ys_flat, dest_c).reshape(TOP_K, 2, c1 - c0, PLANE_W)
        y_p = _final(xmid_p2, g2_p, z, w_t, gfin, FINAL_TILE, n_tiles=(min(c1, n_prompt) - c0) // FINAL_TILE,
                     x_tile0=c0 // FINAL_TILE, z_tile0=0, w_tile0=c0 // FINAL_TILE,
                     tiles_per_g2=seq // FINAL_TILE, y_prev=y_p)
    y_s = _final(xmid_s, g2_s, z, w_t, gfin, n_smp, n_tiles=1, x_tile0=0, z_tile0=(n_prompt - c0) // n_smp,
                 w_tile0=n_prompt // n_smp, tiles_per_g2=1)

    return (y_p.reshape(bsz, seq, D_MODEL), y_s.reshape(n_smp, 1, D_MODEL),
            s_p[None], cv_p[None], s_s[None], cv_s.reshape(1, n_smp, CONV_K - 1, CONV_W))
```

```python
import functools

import jax
import jax.numpy as jnp
from jax import lax
from jax.experimental import pallas as pl
from jax.experimental.pallas import tpu as pltpu
from jax.experimental.pallas import tpu_sc as plsc

F32 = jnp.float32
BF16 = jnp.bfloat16
I32 = jnp.int32

D_MODEL = 1024
HALF_D = D_MODEL // 2
HEADS = 4
DK = 128
KEY_W = HEADS * DK
CONV_W = 512
CONV_K = 3
IN_W = 2 * KEY_W + 2 * KEY_W + 3 * CONV_W + 2 * D_MODEL
N_EXP = 64
TOP_K = 8
N_GRP = 8
GRP_SZ = N_EXP // N_GRP
TOPK_GRP = 4
EXP_FF = 256
SH_FF = 256
ROUTED_SCALE = 2.5
EPS = 1e-6

C_Q, C_F, C_I, C_G = 0, 512, 1024, 1536
C_BB, C_CC, C_VB = 2048, 2560, 3072
C_MGA, C_MGB = 3584, 4608

MIX_TILE = 512
SUB = 256
CHUNK = 64
ROUTE_TILE = 128
GMM_BM = 512
GMM_NX = 4
GMM_NY = 3
FINAL_TILE = 512
FINAL_SPLIT = (4, 12, 22, 32)
SC_WINDOW = 128
PLANE_W = HALF_D // 2
VMEM_LIMIT = 56 * 1024 * 1024


def _dot(a, b):
    return jnp.dot(a, b, preferred_element_type=F32)


def _dot_nt(a, b):
    return lax.dot_general(a, b, (((1,), (1,)), ((), ())), preferred_element_type=F32)


def _dot_tn(a, b):
    return lax.dot_general(a, b, (((0,), (0,)), ((), ())), preferred_element_type=F32)


def _sigmoid(x):
    return 0.5 * jnp.tanh(0.5 * x) + 0.5


def _silu(x):
    h = 0.5 * x
    return h * jnp.tanh(h) + h


def _rms(x):
    return x * lax.rsqrt(jnp.mean(x * x, axis=-1, keepdims=True) + EPS)


def _lower_bound(lbl):
    a, b = lbl[0:1], lbl[1:2]
    m = jnp.maximum(a, b)
    ea, eb = jnp.exp(a - m), jnp.exp(b - m)
    return ea / (ea + eb)


def _split3(x):
    hi = x.astype(BF16)
    r1 = x - hi.astype(F32)
    mid = r1.astype(BF16)
    lo = (r1 - mid.astype(F32)).astype(BF16)
    return hi, mid, lo


def _words(lo_b, hi_b):
    lo = lax.shift_right_logical(lax.bitcast_convert_type(lo_b.astype(F32), I32), 16)
    hi = lax.bitcast_convert_type(hi_b.astype(F32), I32) & jnp.int32(-65536)
    return lo | hi


def _halves(w):
    lo = lax.bitcast_convert_type(lax.shift_left(w, 16), F32)
    hi = lax.bitcast_convert_type(w & jnp.int32(-65536), F32)
    return lo.astype(BF16), hi.astype(BF16)


def _pack_rows(xb, out_ref):
    words = _words(xb[:, :HALF_D], xb[:, HALF_D:])
    out_ref[0] = words[:, :PLANE_W]
    out_ref[1] = words[:, PLANE_W:]


def _unpack_rows(p0, p1):
    c0, c2 = _halves(p0)
    c1, c3 = _halves(p1)
    return c0, c1, c2, c3


def _ada_kernel(cp_ref, cs_ref, w_ref, b_ref, op_ref, os_ref):
    w = w_ref[...].astype(BF16)
    for c_ref, o_ref in ((cp_ref, op_ref), (cs_ref, os_ref)):
        o_ref[...] = _dot(_silu(c_ref[...]).astype(BF16), w) + b_ref[...]


def _ada(c_prompt, c_sample, w_ada, b_ada):
    blk = 1024
    rows = lambda c: pl.BlockSpec((c.shape[0], D_MODEL), lambda j: (0, 0))
    cols = lambda c: pl.BlockSpec((c.shape[0], blk), lambda j: (0, j))
    return pl.pallas_call(
        _ada_kernel,
        grid=(6 * D_MODEL // blk,),
        in_specs=[rows(c_prompt), rows(c_sample),
                  pl.BlockSpec((D_MODEL, blk), lambda j: (0, j)),
                  pl.BlockSpec((1, blk), lambda j: (0, j))],
        out_specs=[cols(c_prompt), cols(c_sample)],
        out_shape=[jax.ShapeDtypeStruct((c.shape[0], 6 * D_MODEL), F32) for c in (c_prompt, c_sample)],
        name="ada",
    )(c_prompt, c_sample, w_ada, b_ada.reshape(1, -1))


def _ffn_pre(x1, mod_rows, gffn, w_sgu, w_sd, wr_hl):
    sh2, sc2, g2 = mod_rows
    h2 = _rms(x1) * gffn * (1.0 + sc2) + sh2
    h2b = h2.astype(BF16)
    gu = _dot(h2b, w_sgu)
    act = _silu(gu[:, :SH_FF]) * gu[:, SH_FF:]
    xmid = x1 + g2 * _dot(act.astype(BF16), w_sd)
    h2lo = (h2 - h2b.astype(F32)).astype(BF16)
    both = _dot_nt(wr_hl, h2b)
    lgt = both[:N_EXP] + both[N_EXP:] + _dot_nt(wr_hl[:N_EXP], h2lo)
    return xmid, h2b, lgt


def _mix_kernel(*refs, nt, n_tiles):
    i = pl.program_id(0)
    rbias_ref, h2_ref = refs[15], refs[17]
    idx_ref, w_ref, rank_ref, cnt_ref, lgt_ref = refs[18], refs[19], refs[20], refs[21], refs[-1]

    @pl.when(i == 0)
    def _():
        lgt_ref[...] = jnp.zeros_like(lgt_ref)
        cnt_ref[...] = jnp.zeros_like(cnt_ref)

    def route_previous(keep):
        carry = cnt_ref[:, 0:1]
        for c0 in range(0, lgt_ref.shape[1], ROUTE_TILE):
            cols = slice(c0, c0 + ROUTE_TILE)
            picks, weights, ranks, carry = _route_tile(lgt_ref[:, cols], rbias_ref[...], carry)
            for k in range(TOP_K):
                idx_ref[k:k + 1, cols] = picks[k]
                w_ref[k:k + 1, cols] = weights[k]
                rank_ref[k:k + 1, cols] = ranks[k]
            yield
        cnt_ref[...] = jnp.broadcast_to(jnp.where(keep, carry, 0.0), cnt_ref.shape)

    @pl.when(i == n_tiles)
    def _():
        h2_ref[...] = jnp.zeros_like(h2_ref)
        for _ in route_previous(True):
            pass

    @pl.when(i < n_tiles)
    def _():
        _mix_tile(i % nt, nt, route_previous(i > 0), *refs)


def _mix_tile(t, nt, routing, x_ref, mod_ref, gmix_ref, gffn_ref, w_in_ref, lbl_ref, hg_ref, cw_ref, cb_ref,
              w_oh_ref, w_oc_ref, w_o_ref, wr_hl_ref, w_sgu_ref, w_sd_ref, rbias_ref,
              xmid_ref, h2_ref, idx_ref, w_ref, rank_ref, cnt_ref, s_out_ref, cv_out_ref,
              proj_ref, st_ref, cbuf_ref, ya_ref, lgt_ref):
    del rbias_ref, idx_ref, w_ref, rank_ref, cnt_ref
    tt = x_ref.shape[1]

    @pl.when(t == 0)
    def _():
        st_ref[...] = jnp.zeros_like(st_ref)
        cbuf_ref[...] = jnp.zeros_like(cbuf_ref)

    x = x_ref[0]
    mod = mod_ref[0]
    sh1, sc1, g1 = mod[0:1], mod[1:2], mod[2:3]
    h = _rms(x) * gmix_ref[...] * (1.0 + sc1) + sh1
    hb = h.astype(BF16)
    for c in range(0, IN_W, 512):
        proj_ref[:, c:c + 512] = _dot(hb, w_in_ref[:, c:c + 512])
        if c % 1024 == 512:
            next(routing, None)
    for _ in routing:
        pass

    lb = _lower_bound(lbl_ref[...])
    row = lax.broadcasted_iota(I32, (SUB, SUB), 0)
    col = lax.broadcasted_iota(I32, (SUB, SUB), 1)
    tri = (col <= row).astype(BF16)
    mask_d = (col <= row) & (row // CHUNK == col // CHUNK)
    mask_a = row // (2 * CHUNK) == col // (2 * CHUNK)
    n_ch = SUB // CHUNK

    def by_chunk(vals):
        return jnp.concatenate([jnp.zeros((CHUNK, DK), F32) if v is None
                                else jnp.broadcast_to(v, (CHUNK, DK)) for v in vals], axis=0)

    for s in range(tt // SUB):
        r0 = s * SUB
        f = lb + (1.0 - lb) * _sigmoid(proj_ref[r0:r0 + SUB, C_F:C_F + KEY_W])
        kk = 1.0 - f
        hi, mid, lo = _split3(jnp.log(f))
        bc = _dot(tri, hi) + _dot(tri, mid) + _dot(tri, lo)
        for hd in range(HEADS):
            hs = slice(hd * DK, (hd + 1) * DK)
            bh = bc[:, hs]
            at = lambda r: bh[r:r + 1]
            mids = [at(c * CHUNK + CHUNK // 2 - 1) for c in range(n_ch)]
            pair_mid = [at(CHUNK - 1), at(3 * CHUNK - 1)]
            step_mid, step_end = at(2 * CHUNK - 1), at(SUB - 1)
            arg = bh - by_chunk(mids)
            e_pos, e_neg = jnp.exp(arg), jnp.exp(-arg)
            q = _silu(proj_ref[r0:r0 + SUB, C_Q + hd * DK:C_Q + (hd + 1) * DK])
            v = proj_ref[r0:r0 + SUB, C_I + hd * DK:C_I + (hd + 1) * DK]
            qd = q * e_pos
            kd = kk[:, hs] * e_neg
            q_in = qd * by_chunk([jnp.exp(m) for m in mids])
            k_end = kd * by_chunk([jnp.exp(step_end - m) for m in mids])
            qa = qd * by_chunk([None, jnp.exp(mids[1] - pair_mid[0]), None, jnp.exp(mids[3] - pair_mid[1])])
            ka = kd * by_chunk([jnp.exp(pair_mid[0] - mids[0]), None, jnp.exp(pair_mid[1] - mids[2]), None])
            qb = qd * by_chunk([None, None, jnp.exp(mids[2] - step_mid), jnp.exp(mids[3] - step_mid)])
            kb = kd * by_chunk([jnp.exp(step_mid - mids[0]), jnp.exp(step_mid - mids[1]), None, None])
            att = jnp.where(mask_d, _dot_nt(qd.astype(BF16), kd.astype(BF16)), 0.0)
            att = att + jnp.where(mask_a, _dot_nt(qa.astype(BF16), ka.astype(BF16)), 0.0)
            att = att + _dot_nt(qb.astype(BF16), kb.astype(BF16))
            vb = v.astype(BF16)
            st = st_ref[hd]
            o = _dot(att.astype(BF16), vb) + _dot_nt(q_in.astype(BF16), st.astype(BF16))
            st_ref[hd] = st * jnp.exp(step_end) + _dot_tn(vb, k_end.astype(BF16))
            gate = _silu(proj_ref[r0:r0 + SUB, C_G + hd * DK:C_G + (hd + 1) * DK])
            ya_ref[r0:r0 + SUB, hs] = _rms(o) * hg_ref[:, hs] * gate

    u = proj_ref[:, C_CC:C_CC + CONV_W] * proj_ref[:, C_VB:C_VB + CONV_W]
    rows = lax.broadcasted_iota(I32, (tt, CONV_W), 0)
    c0, c1 = cbuf_ref[0:1], cbuf_ref[1:2]
    u1 = jnp.where(rows == 0, c1, pltpu.roll(u, 1, axis=0))
    u2 = jnp.where(rows == 0, c0, jnp.where(rows == 1, c1, pltpu.roll(u, 2, axis=0)))
    conv = cw_ref[0:1] * u2 + cw_ref[1:2] * u1 + cw_ref[2:3] * u + cb_ref[...]
    yb = proj_ref[:, C_BB:C_BB + CONV_W] * conv
    cbuf_ref[...] = u[tt - 2:tt]

    mixed = (_sigmoid(proj_ref[:, C_MGA:C_MGA + D_MODEL]) * _dot(ya_ref[...].astype(BF16), w_oh_ref[...])
             + _sigmoid(proj_ref[:, C_MGB:C_MGB + D_MODEL]) * _dot(yb.astype(BF16), w_oc_ref[...]))
    x1 = x + g1 * _dot(mixed.astype(BF16), w_o_ref[...])

    xmid, h2b, lgt = _ffn_pre(x1, (mod[3:4], mod[4:5], mod[5:6]), gffn_ref[...],
                              w_sgu_ref[...], w_sd_ref[...], wr_hl_ref[...])
    xmid_ref[0] = xmid
    _pack_rows(h2b, h2_ref)
    lgt_ref[...] = lgt

    @pl.when(t == nt - 1)
    def _():
        for hd in range(HEADS):
            s_out_ref[0, hd] = st_ref[hd].T
        cv_out_ref[0] = cbuf_ref[...]


def _const_spec(shape):
    nd = len(shape)
    return pl.BlockSpec(shape, lambda i, _nd=nd: (0,) * _nd, pipeline_mode=pl.Buffered(1))


def _mix(x, mod, n_tok, gmix, gffn, w_in, lbl, hg, cw, cb, w_oh, w_oc, w_o, wr_hl, w_sgu, w_sd, rbias):
    bsz, seq, _ = x.shape
    tt = MIX_TILE
    nt = seq // tt
    n_tiles = bsz * nt
    assert n_tiles * tt < n_tok <= (n_tiles + 1) * tt and tt % ROUTE_TILE == 0
    consts = [gmix, gffn, w_in, lbl, hg, cw, cb, w_oh, w_oc, w_o, wr_hl, w_sgu, w_sd, rbias]
    tile = lambda i: jnp.minimum(i, n_tiles - 1)
    routed = pl.BlockSpec((TOP_K, tt), lambda i: (0, jnp.maximum(i - 1, 0)))
    slot = lambda dt: jax.ShapeDtypeStruct((TOP_K, n_tiles * tt), dt)
    return pl.pallas_call(
        functools.partial(_mix_kernel, nt=nt, n_tiles=n_tiles),
        grid=(n_tiles + 1,),
        in_specs=[pl.BlockSpec((1, tt, D_MODEL), lambda i: (tile(i) // nt, tile(i) % nt, 0)),
                  pl.BlockSpec((1, 6, D_MODEL), lambda i: (tile(i) // nt, 0, 0))]
                 + [_const_spec(a.shape) for a in consts],
        out_specs=[pl.BlockSpec((1, tt, D_MODEL), lambda i: (tile(i) // nt, tile(i) % nt, 0)),
                   pl.BlockSpec((2, tt, PLANE_W), lambda i: (0, i, 0)),
                   routed, routed, routed,
                   pl.BlockSpec((N_EXP, 128), lambda i: (0, 0)),
                   pl.BlockSpec((1, HEADS, DK, DK), lambda i: (tile(i) // nt, 0, 0, 0)),
                   pl.BlockSpec((1, CONV_K - 1, CONV_W), lambda i: (tile(i) // nt, 0, 0))],
        out_shape=[jax.ShapeDtypeStruct((bsz, seq, D_MODEL), F32),
                   jax.ShapeDtypeStruct((2, n_tok, PLANE_W), I32),
                   slot(I32), slot(F32), slot(I32),
                   jax.ShapeDtypeStruct((N_EXP, 128), F32),
                   jax.ShapeDtypeStruct((bsz, HEADS, DK, DK), F32),
                   jax.ShapeDtypeStruct((bsz, CONV_K - 1, CONV_W), F32)],
        scratch_shapes=[pltpu.VMEM((tt, IN_W), F32),
                        pltpu.VMEM((HEADS, DK, DK), F32),
                        pltpu.VMEM((CONV_K - 1, CONV_W), F32),
                        pltpu.VMEM((tt, KEY_W), F32),
                        pltpu.VMEM((N_EXP, tt), F32)],
        compiler_params=pltpu.CompilerParams(
            dimension_semantics=("arbitrary",), vmem_limit_bytes=VMEM_LIMIT),
        name="mix",
    )(x, mod, *consts)


def _smp1_kernel(x_ref, mod_ref, gmix_ref, w_in_ref, lbl_ref, cw_ref, cb_ref, cst_ref,
                 f_ref, k_ref, q_ref, v_ref, gate_ref, yb_ref, sga_ref, sgb_ref, cv_out_ref):
    x = x_ref[...]
    sh1, sc1 = mod_ref[:, 0:D_MODEL], mod_ref[:, D_MODEL:2 * D_MODEL]
    h = _rms(x) * gmix_ref[...] * (1.0 + sc1) + sh1
    hb = h.astype(BF16)

    def proj(c, w):
        return _dot(hb, w_in_ref[:, c:c + w])

    lb = _lower_bound(lbl_ref[...])
    f = lb + (1.0 - lb) * _sigmoid(proj(C_F, KEY_W))
    f_ref[...] = f
    k_ref[...] = 1.0 - f
    q_ref[...] = _silu(proj(C_Q, KEY_W))
    v_ref[...] = proj(C_I, KEY_W)
    gate_ref[...] = _silu(proj(C_G, KEY_W))
    u = proj(C_CC, CONV_W) * proj(C_VB, CONV_W)
    c0, c1 = cst_ref[:, 0:CONV_W], cst_ref[:, CONV_W:2 * CONV_W]
    conv = cw_ref[0:1] * c0 + cw_ref[1:2] * c1 + cw_ref[2:3] * u + cb_ref[...]
    yb_ref[...] = proj(C_BB, CONV_W) * conv
    cv_out_ref[:, 0:CONV_W] = c1
    cv_out_ref[:, CONV_W:2 * CONV_W] = u
    sga_ref[...] = _sigmoid(proj(C_MGA, D_MODEL))
    sgb_ref[...] = _sigmoid(proj(C_MGB, D_MODEL))


def _smp1(x, mod, gmix, w_in, lbl, cw, cb, cst):
    n = x.shape[0]
    kw = jax.ShapeDtypeStruct((n, KEY_W), F32)
    dm = jax.ShapeDtypeStruct((n, D_MODEL), F32)
    return pl.pallas_call(
        _smp1_kernel,
        out_shape=[kw, kw, kw, kw, kw, kw, dm, dm,
                   jax.ShapeDtypeStruct((n, (CONV_K - 1) * CONV_W), F32)],
        compiler_params=pltpu.CompilerParams(vmem_limit_bytes=VMEM_LIMIT),
        name="smp1",
    )(x, mod, gmix, w_in, lbl, cw, cb, cst)


def _smp2(f, k, q, v, state):
    n = f.shape[0]
    info = plsc.get_sparse_core_info()
    lanes = info.num_lanes
    n_workers = info.num_cores * info.num_subcores
    tok_per = n // n_workers
    n_chunks = DK // lanes
    vec = lambda: pltpu.VMEM((DK,), F32)

    @pl.kernel(out_type=[jax.ShapeDtypeStruct(state.shape, F32), jax.ShapeDtypeStruct((n, KEY_W), F32)],
               mesh=_sc_mesh(), scratch_types=[pltpu.VMEM((DK, DK), F32), vec(), vec(), vec(), vec(), vec()],
               compiler_params=pltpu.CompilerParams(needs_layout_passes=False), name="smp2")
    def run(f_hbm, k_hbm, q_hbm, v_hbm, s_hbm, s_out_hbm, o_hbm, s_v, f_v, k_v, q_v, v_v, o_v):
        wid = lax.axis_index("subcore") * info.num_cores + lax.axis_index("core")

        def tile(j, carry):
            t = wid * tok_per + j // HEADS
            hcols = pl.ds((j % HEADS) * DK, DK)
            pltpu.sync_copy(s_hbm.at[t, j % HEADS], s_v)
            for src, dst in ((f_hbm, f_v), (k_hbm, k_v), (q_hbm, q_v), (v_hbm, v_v)):
                pltpu.sync_copy(src.at[t, hcols], dst)
            v_chunks = [v_v[pl.ds(c * lanes, lanes)] for c in range(n_chunks)]

            def row(d, acc):
                at = [jnp.zeros((lanes,), I32) + d]
                fd, kd, qd = (plsc.load_gather(r, at) for r in (f_v, k_v, q_v))
                out = []
                for c in range(n_chunks):
                    cols = pl.ds(c * lanes, lanes)
                    new = s_v[d, cols] * fd + kd * v_chunks[c]
                    s_v[d, cols] = new
                    out.append(acc[c] + qd * new)
                return tuple(out)

            acc = lax.fori_loop(0, DK, row, tuple(jnp.zeros((lanes,), F32) for _ in range(n_chunks)))
            for c in range(n_chunks):
                o_v[pl.ds(c * lanes, lanes)] = acc[c]
            pltpu.sync_copy(s_v, s_out_hbm.at[t, j % HEADS])
            pltpu.sync_copy(o_v, o_hbm.at[t, hcols])
            return carry

        lax.fori_loop(0, tok_per * HEADS, tile, 0)

    return run(f, k, q, v, state)


def _smp3_kernel(x_ref, mod_ref, o_ref, gate_ref, yb_ref, sga_ref, sgb_ref, hg_ref, gffn_ref,
                 w_oh_ref, w_oc_ref, w_o_ref, wr_hl_ref, w_sgu_ref, w_sd_ref,
                 rbias_ref, cnt_p_ref, idx_p_ref, w_p_ref, rank_p_ref, h2_all_ref,
                 xmid_ref, h2_ref, idx_ref, w_ref, rank_ref, cnt_ref):
    del h2_all_ref
    parts = []
    for hd in range(HEADS):
        hs = slice(hd * DK, (hd + 1) * DK)
        parts.append(_rms(o_ref[:, hs]) * hg_ref[:, hs] * gate_ref[:, hs])
    ya = jnp.concatenate(parts, axis=1)
    mixed = (sga_ref[...] * _dot(ya.astype(BF16), w_oh_ref[...])
             + sgb_ref[...] * _dot(yb_ref[...].astype(BF16), w_oc_ref[...]))
    g1 = mod_ref[:, 2 * D_MODEL:3 * D_MODEL]
    x1 = x_ref[...] + g1 * _dot(mixed.astype(BF16), w_o_ref[...])
    mod_rows = tuple(mod_ref[:, j * D_MODEL:(j + 1) * D_MODEL] for j in (3, 4, 5))
    xmid, h2b, lgt = _ffn_pre(x1, mod_rows, gffn_ref[...], w_sgu_ref[...], w_sd_ref[...],
                              wr_hl_ref[...])
    xmid_ref[...] = xmid
    _pack_rows(h2b, h2_ref)
    n_prompt = idx_p_ref.shape[1]
    picks, weights, ranks, total = _route_tile(lgt, rbias_ref[...], cnt_p_ref[:, 0:1])
    for src, dst, new in ((idx_p_ref, idx_ref, picks), (w_p_ref, w_ref, weights), (rank_p_ref, rank_ref, ranks)):
        dst[:, :n_prompt] = src[...]
        for k in range(TOP_K):
            dst[k:k + 1, n_prompt:] = new[k]
    cnt_ref[...] = jnp.broadcast_to(total, cnt_ref.shape).astype(I32)


def _smp3(x, mod, o, gate, yb, sga, sgb, hg, gffn, w_oh, w_oc, w_o, wr_hl, w_sgu, w_sd,
          rbias, cnt_p, idx_p, w_p, rank_p, h2_all):
    n = x.shape[0]
    vmem_args = [x, mod, o, gate, yb, sga, sgb, hg, gffn, w_oh, w_oc, w_o, wr_hl, w_sgu, w_sd,
                 rbias, cnt_p, idx_p, w_p, rank_p]
    n_prompt = idx_p.shape[1]
    blk = n_prompt // n
    slot = lambda dt: jax.ShapeDtypeStruct((TOP_K, n_prompt + n), dt)

    def full(a):
        nd = a.ndim
        return pl.BlockSpec(a.shape, lambda i, _nd=nd: (0,) * _nd)

    return pl.pallas_call(
        _smp3_kernel,
        grid=(1,),
        in_specs=[full(a) for a in vmem_args] + [pl.BlockSpec(memory_space=pl.ANY)],
        out_specs=[pl.BlockSpec((n, D_MODEL), lambda i: (0, 0)),
                   pl.BlockSpec((2, n, PLANE_W), lambda i: (0, blk, 0)),
                   pl.BlockSpec((TOP_K, n_prompt + n), lambda i: (0, 0)),
                   pl.BlockSpec((TOP_K, n_prompt + n), lambda i: (0, 0)),
                   pl.BlockSpec((TOP_K, n_prompt + n), lambda i: (0, 0)),
                   pl.BlockSpec((N_EXP, 128), lambda i: (0, 0))],
        out_shape=[jax.ShapeDtypeStruct((n, D_MODEL), F32),
                   jax.ShapeDtypeStruct(h2_all.shape, h2_all.dtype),
                   slot(I32), slot(F32), slot(I32),
                   jax.ShapeDtypeStruct((N_EXP, 128), I32)],
        input_output_aliases={len(vmem_args): 1},
        compiler_params=pltpu.CompilerParams(
            dimension_semantics=("arbitrary",), vmem_limit_bytes=VMEM_LIMIT),
        name="smp3",
    )(*vmem_args, h2_all)


def _route_tile(lgt, bias, carry):
    tr = lgt.shape[1]
    neg = -jnp.inf
    scores = _sigmoid(lgt)
    sel = scores + bias
    j8 = lax.broadcasted_iota(I32, (GRP_SZ, tr), 0)
    groups = [sel[g * GRP_SZ:(g + 1) * GRP_SZ] for g in range(N_GRP)]
    gscore = []
    for grp in groups:
        m1 = jnp.max(grp, axis=0, keepdims=True)
        i1 = jnp.min(jnp.where(grp == m1, j8, GRP_SZ), axis=0, keepdims=True)
        m2 = jnp.max(jnp.where(j8 == i1, neg, grp), axis=0, keepdims=True)
        gscore.append(m1 + m2)
    kept = []
    for g in range(N_GRP):
        beaten = jnp.zeros((1, tr), I32)
        for o in range(N_GRP):
            if o < g:
                beaten = beaten + (gscore[o] >= gscore[g]).astype(I32)
            elif o > g:
                beaten = beaten + (gscore[o] > gscore[g]).astype(I32)
        kept.append(jnp.where(beaten < TOPK_GRP, groups[g], neg))
    masked = jnp.concatenate(kept, axis=0)
    ei = lax.broadcasted_iota(I32, masked.shape, 0)
    chosen = jnp.zeros(masked.shape, jnp.bool_)
    picks, weights = [], []
    for _ in range(TOP_K):
        m = jnp.max(masked, axis=0, keepdims=True)
        pick = jnp.min(jnp.where((masked == m) & ~chosen, ei, N_EXP), axis=0, keepdims=True)
        hit = ei == pick
        weights.append(jnp.sum(jnp.where(hit, scores, 0.0), axis=0, keepdims=True))
        picks.append(pick)
        chosen = chosen | hit
        masked = jnp.where(hit, neg, masked)
    wsum = weights[0]
    for w in weights[1:]:
        wsum = wsum + w
    sel01 = chosen.astype(F32)
    r = lax.broadcasted_iota(I32, (tr, tr), 0)
    c = lax.broadcasted_iota(I32, (tr, tr), 1)
    before = (r < c).astype(BF16)
    cnt = _dot(sel01.astype(BF16), before) + carry
    weights = [w / wsum * ROUTED_SCALE for w in weights]
    ranks = [jnp.sum(jnp.where(ei == p, cnt, 0.0), axis=0, keepdims=True).astype(I32) for p in picks]
    return picks, weights, ranks, carry + jnp.sum(sel01, axis=1, keepdims=True)


def _dest_kernel(start_ref, idx_ref, rank_ref, all_ref, *chunk_refs, n_rows, chunks):
    n_tok = idx_ref.shape[1]
    idx = idx_ref[...]
    acc = rank_ref[...]
    for e in range(N_EXP):
        acc = acc + jnp.where(idx == e, start_ref[e], 0)
    for k in range(TOP_K):
        for p in range(2):
            row = acc[k:k + 1, :] + p * n_rows
            seg = 2 * k + p
            all_ref[:, seg * n_tok:(seg + 1) * n_tok] = row
            for (c0, c1), ref in zip(chunks, chunk_refs):
                ref[:, seg * (c1 - c0):(seg + 1) * (c1 - c0)] = row[:, c0:c1]


def _dest(pad_start, idx, rank, n_rows, chunks):
    k, n_tok = idx.shape
    vmem = pl.BlockSpec(memory_space=pltpu.VMEM)
    sizes = [n_tok] + [c1 - c0 for c0, c1 in chunks]
    return pl.pallas_call(
        functools.partial(_dest_kernel, n_rows=n_rows, chunks=chunks),
        in_specs=[pl.BlockSpec(memory_space=pltpu.SMEM), vmem, vmem],
        out_specs=[vmem] * len(sizes),
        out_shape=[jax.ShapeDtypeStruct((1, 2 * k * n), I32) for n in sizes],
        name="dest",
    )(pad_start, idx, rank)


def _sc_mesh():
    return plsc.VectorSubcoreMesh(core_axis_name="core", subcore_axis_name="subcore")


def _dispatch(rows, dest, n_out):
    n, width = rows.shape
    win = SC_WINDOW
    steps = n // win

    @pl.kernel(out_type=jax.ShapeDtypeStruct((n_out, width), rows.dtype), mesh=_sc_mesh(),
               scratch_types=[], name="dispatch")
    def run(x_hbm, *refs):
        i_hbms, o_hbm = refs[:TOP_K], refs[TOP_K]

        def body(x_vmem, *i_vmems):
            for i_vmem in i_vmems:
                pltpu.sync_copy(x_vmem, o_hbm.at[i_vmem.at[0]])

        pltpu.emit_pipeline(
            body,
            grid=(steps,),
            in_specs=[pl.BlockSpec((win, width), lambda i: (i, 0))]
                     + [pl.BlockSpec((1, win), lambda i, k=k: (0, k * steps + i)) for k in range(TOP_K)],
            out_specs=[],
            core_axis_name=("core", "subcore"),
            dimension_semantics=(pltpu.PARALLEL,),
        )(x_hbm, *i_hbms)

    assert dest.shape == (1, TOP_K * n)
    return run(rows, *([dest] * TOP_K))


def _combine(rows, dest_flat):
    width = rows.shape[1]
    n = dest_flat.shape[1]
    win = SC_WINDOW

    @pl.kernel(out_type=jax.ShapeDtypeStruct((n, width), rows.dtype), mesh=_sc_mesh(),
               scratch_types=[], name="combine")
    def run(y_hbm, i_hbm, o_hbm):
        def body(i_vmem, o_vmem):
            pltpu.sync_copy(y_hbm.at[i_vmem.at[0]], o_vmem)

        pltpu.emit_pipeline(
            body,
            grid=(n // win,),
            in_specs=[pl.BlockSpec((1, win), lambda i: (0, i))],
            out_specs=[pl.BlockSpec((win, width), lambda i: (i, 0))],
            core_axis_name=("core", "subcore"),
            dimension_semantics=(pltpu.PARALLEL,),
        )(i_hbm, o_hbm)

    return run(rows, dest_flat)


def _gmm_kernel(blk_exp_ref, n_used_ref, xs_hbm, wg_hbm, wu_hbm, wd_hbm, ys_hbm,
                xbuf, ybuf, wg32, wu32, wd32, wgu_b, wd_b, xsem, ysem, wsem, run_ref):
    nx, ny, bm = xbuf.shape[0], ybuf.shape[0], xbuf.shape[2]
    n_used = n_used_ref[0]

    def x_copies(b):
        rows, slot = pl.ds(b * bm, bm), b % nx
        return [pltpu.make_async_copy(xs_hbm.at[p, rows, :], xbuf.at[slot, p], xsem.at[slot, p]) for p in range(2)]

    def y_copies(b):
        rows, slot = pl.ds(b * bm, bm), b % ny
        return [pltpu.make_async_copy(ybuf.at[slot, p], ys_hbm.at[p, rows, :], ysem.at[slot, p]) for p in range(2)]

    def start(copies):
        for c in copies:
            c.start()

    def wait(copies):
        for c in copies:
            c.wait()

    def w_copies(e, slot):
        return (pltpu.make_async_copy(wg_hbm.at[e], wg32.at[slot], wsem.at[slot, 0]),
                pltpu.make_async_copy(wu_hbm.at[e], wu32.at[slot], wsem.at[slot, 1]),
                pltpu.make_async_copy(wd_hbm.at[e], wd32.at[slot], wsem.at[slot, 2]))

    def run_end(b):
        return lax.while_loop(lambda j: (j < n_used) & (blk_exp_ref[jnp.minimum(j, n_used - 1)] == blk_exp_ref[b]),
                              lambda j: j + 1, b + 1)

    run_ref[0] = 0
    start(x_copies(0))
    start(w_copies(blk_exp_ref[0], 0))
    for j in range(1, nx - 1):
        @pl.when(j < n_used)
        def _():
            start(x_copies(j))

    def block(b, carry):
        @pl.when(b + nx - 1 < n_used)
        def _():
            start(x_copies(b + nx - 1))

        @pl.when((b == 0) | (blk_exp_ref[b] != blk_exp_ref[jnp.maximum(b - 1, 0)]))
        def _():
            wslot = run_ref[0] % 2
            run_ref[0] = run_ref[0] + 1
            wait(w_copies(blk_exp_ref[b], wslot))
            wgu_b[:, 0:EXP_FF] = wg32[wslot].astype(BF16)
            wgu_b[:, EXP_FF:2 * EXP_FF] = wu32[wslot].astype(BF16)
            wd_b[...] = wd32[wslot].astype(BF16)
            nxt = run_end(b)

            @pl.when(nxt < n_used)
            def _():
                start(w_copies(blk_exp_ref[jnp.minimum(nxt, n_used - 1)], 1 - wslot))

        wait(x_copies(b))

        @pl.when(b >= ny)
        def _():
            wait(y_copies(b - ny))

        xslot = b % nx
        xc = _unpack_rows(xbuf[xslot, 0], xbuf[xslot, 1])
        gu = sum(_dot(c, wgu_b[i * PLANE_W:(i + 1) * PLANE_W, :]) for i, c in enumerate(xc))
        act = (_silu(gu[:, :EXP_FF]) * gu[:, EXP_FF:]).astype(BF16)
        _pack_rows(_dot(act, wd_b[...]).astype(BF16), ybuf.at[b % ny])
        start(y_copies(b))
        return carry

    lax.fori_loop(0, n_used, block, 0)

    for j in range(ny, 0, -1):
        @pl.when(n_used >= j)
        def _():
            wait(y_copies(n_used - j))


def _gmm(blk_exp, n_used, xs, w_gate, w_up, w_down):
    n_rows = xs.shape[1]
    bm = GMM_BM
    nb = n_rows // bm

    assert blk_exp.shape == (nb,)
    any_spec = pl.BlockSpec(memory_space=pl.ANY)
    grid_spec = pltpu.PrefetchScalarGridSpec(
        num_scalar_prefetch=2,
        grid=(1,),
        in_specs=[any_spec, any_spec, any_spec, any_spec],
        out_specs=any_spec,
        scratch_shapes=[pltpu.VMEM((GMM_NX, 2, bm, PLANE_W), I32), pltpu.VMEM((GMM_NY, 2, bm, PLANE_W), I32),
                        pltpu.VMEM((2, D_MODEL, EXP_FF), F32), pltpu.VMEM((2, D_MODEL, EXP_FF), F32),
                        pltpu.VMEM((2, EXP_FF, D_MODEL), F32),
                        pltpu.VMEM((D_MODEL, 2 * EXP_FF), BF16), pltpu.VMEM((EXP_FF, D_MODEL), BF16),
                        pltpu.SemaphoreType.DMA((GMM_NX, 2)), pltpu.SemaphoreType.DMA((GMM_NY, 2)),
                        pltpu.SemaphoreType.DMA((2, 3)), pltpu.SMEM((1,), I32)],
    )
    return pl.pallas_call(
        _gmm_kernel,
        grid_spec=grid_spec,
        out_shape=jax.ShapeDtypeStruct((2, n_rows, PLANE_W), I32),
        compiler_params=pltpu.CompilerParams(dimension_semantics=("arbitrary",)),
        name="gmm",
    )(blk_exp, n_used, xs, w_gate, w_up, w_down)


def _final_kernel(xmid_ref, g2_ref, z_ref, w_ref, gfin_ref, *rest):
    y_ref = rest[-1]
    accs = [jnp.zeros((xmid_ref.shape[0], PLANE_W), F32) for _ in range(4)]
    w_cols = w_ref[...].T
    for k in range(TOP_K):
        wk = w_cols[:, k:k + 1]
        cols = _unpack_rows(z_ref[k, 0], z_ref[k, 1])
        accs = [a + wk * c.astype(F32) for a, c in zip(accs, cols)]
    acc = jnp.concatenate(accs, axis=1)
    y_ref[...] = _rms(xmid_ref[...] + g2_ref[0] * acc) * gfin_ref[...]


def _final(xmid, g2, z, w_t, gfin, tile, *, n_tiles, x_tile0, z_tile0, w_tile0, tiles_per_g2, y_prev=None):
    args = [xmid, g2, z, w_t, gfin]
    in_specs = [pl.BlockSpec((tile, D_MODEL), lambda i: (x_tile0 + i, 0)),
                pl.BlockSpec((1, g2.shape[1], D_MODEL), lambda i: ((x_tile0 + i) // tiles_per_g2, 0, 0)),
                pl.BlockSpec((TOP_K, 2, tile, PLANE_W), lambda i: (0, 0, z_tile0 + i, 0)),
                pl.BlockSpec((TOP_K, tile), lambda i: (0, w_tile0 + i)),
                pl.BlockSpec((1, D_MODEL), lambda i: (0, 0))]
    aliases = {}
    if y_prev is not None:
        args.append(y_prev)
        in_specs.append(pl.BlockSpec(memory_space=pl.ANY))
        aliases = {len(args) - 1: 0}
    return pl.pallas_call(
        _final_kernel,
        grid=(n_tiles,),
        in_specs=in_specs,
        out_specs=pl.BlockSpec((tile, D_MODEL), lambda i: (x_tile0 + i, 0)),
        out_shape=jax.ShapeDtypeStruct(xmid.shape, F32),
        input_output_aliases=aliases,
        compiler_params=pltpu.CompilerParams(dimension_semantics=("arbitrary",)),
        name="final",
    )(*args)


def kernel(x_prompt, x_sample, state_hgrn, state_conv, c_prompt, c_sample, w_ada, b_ada, norm_mix_g, norm_ffn_g, w_in, lb_logits, hgrn_norm_g, conv_w, conv_b, w_out_hgrn, w_out_conv, w_o, w_router, router_bias, w_exp_gate, w_exp_up, w_exp_down, w_sh_gate, w_sh_up, w_sh_down, final_norm_g):
    assert w_ada.shape[0] == 1 and lb_logits.shape[0] == 2
    bsz, seq, _ = x_prompt.shape
    n_smp = x_sample.shape[0]
    n_prompt = bsz * seq
    n_tok = n_prompt + n_smp

    w_in_b = w_in[0].astype(BF16)
    w_oh_b = w_out_hgrn[0].astype(BF16)
    w_oc_b = w_out_conv[0].astype(BF16)
    w_o_b = w_o[0].astype(BF16)
    wr_t = w_router[0].T
    wr_hi = wr_t.astype(BF16)
    wr_hl = jnp.concatenate([wr_hi, (wr_t - wr_hi.astype(F32)).astype(BF16)], axis=0)
    w_sgu = jnp.concatenate([w_sh_gate[0], w_sh_up[0]], axis=1).astype(BF16)
    w_sd = w_sh_down[0].astype(BF16)
    gmix = norm_mix_g[0].reshape(1, D_MODEL)
    gffn = norm_ffn_g[0].reshape(1, D_MODEL)
    hg = hgrn_norm_g[0].reshape(1, KEY_W)
    cw = conv_w[0]
    cb = conv_b[0].reshape(1, CONV_W)
    gfin = final_norm_g.reshape(1, D_MODEL)

    mod_p, mod_s = _ada(c_prompt, c_sample, w_ada[0], b_ada[0])

    rbias = router_bias[0].reshape(N_EXP, 1)
    xmid_p, h2_all, idx_p, w_p, rank_p, cnt_p, s_p, cv_p = _mix(
        x_prompt, mod_p.reshape(bsz, 6, D_MODEL), n_tok, gmix, gffn, w_in_b, lb_logits, hg, cw, cb, w_oh_b, w_oc_b, w_o_b, wr_hl, w_sgu, w_sd, rbias)

    xs2 = x_sample.reshape(n_smp, D_MODEL)
    f, kk, q, v, gate, yb, sga, sgb, cv_s = _smp1(
        xs2, mod_s, gmix, w_in_b, lb_logits, cw, cb, state_conv[0].reshape(n_smp, (CONV_K - 1) * CONV_W))
    s_s, o_s = _smp2(f, kk, q, v, state_hgrn[0])
    xmid_s, h2_all, idx, w_tok, rank, cnt = _smp3(xs2, mod_s, o_s, gate, yb, sga, sgb, hg, gffn,
                                                  w_oh_b, w_oc_b, w_o_b, wr_hl, w_sgu, w_sd,
                                                  rbias, cnt_p, idx_p, w_p, rank_p, h2_all)

    bm = GMM_BM
    n_blocks = (n_tok * TOP_K + N_EXP * (bm - 1)) // bm
    n_rows = n_blocks * bm
    counts = cnt[:, 0]
    padded = (counts + bm - 1) // bm * bm
    pad_end = jnp.cumsum(padded)
    pad_start = pad_end - padded
    blk_row0 = jnp.arange(n_blocks, dtype=I32) * bm
    blk_exp = jnp.minimum(jnp.sum((pad_end[None, :] <= blk_row0[:, None]).astype(I32), axis=1), N_EXP - 1)
    n_used = (pad_end[-1:] // bm).astype(I32)
    cuts = [0] + [n_prompt * f // FINAL_SPLIT[-1] for f in FINAL_SPLIT]
    assert all(c % FINAL_TILE == 0 for c in cuts) and cuts[-1] == n_prompt
    chunks = tuple((c0, n_tok if c1 == n_prompt else c1) for c0, c1 in zip(cuts[:-1], cuts[1:]))
    dest, *chunk_dest = _dest(pad_start.astype(I32), idx, rank, n_rows, chunks)

    xs = _dispatch(h2_all.reshape(2 * n_tok, PLANE_W), dest, 2 * n_rows).reshape(2, n_rows, PLANE_W)
    ys = _gmm(blk_exp, n_used, xs, w_exp_gate[0], w_exp_up[0], w_exp_down[0])

    ys_flat = ys.reshape(2 * n_rows, PLANE_W)
    w_t = w_tok
    xmid_p2 = xmid_p.reshape(n_prompt, D_MODEL)
    g2_p = mod_p[:, 5 * D_MODEL:].reshape(bsz, 1, D_MODEL)
    g2_s = mod_s[:, 5 * D_MODEL:].reshape(1, n_smp, D_MODEL)
    y_p = None
    for (c0, c1), dest_c in zip(chunks, chunk_dest):
        z = _combine(ys_flat, dest_c).reshape(TOP_K, 2, c1 - c0, PLANE_W)
        y_p = _final(xmid_p2, g2_p, z, w_t, gfin, FINAL_TILE, n_tiles=(min(c1, n_prompt) - c0) // FINAL_TILE,
                     x_tile0=c0 // FINAL_TILE, z_tile0=0, w_tile0=c0 // FINAL_TILE,
                     tiles_per_g2=seq // FINAL_TILE, y_prev=y_p)
    y_s = _final(xmid_s, g2_s, z, w_t, gfin, n_smp, n_tiles=1, x_tile0=0, z_tile0=(n_prompt - c0) // n_smp,
                 w_tile0=n_prompt // n_smp, tiles_per_g2=1)

    return (y_p.reshape(bsz, seq, D_MODEL), y_s.reshape(n_smp, 1, D_MODEL),
            s_p[None], cv_p[None], s_s[None], cv_s.reshape(1, n_smp, CONV_K - 1, CONV_W))
```

```python
import functools

import jax
import jax.numpy as jnp
from jax import lax
from jax.experimental import pallas as pl
from jax.experimental.pallas import tpu as pltpu
from jax.experimental.pallas import tpu_sc as plsc

F32 = jnp.float32
BF16 = jnp.bfloat16
I32 = jnp.int32

D_MODEL = 1024
HALF_D = D_MODEL // 2
HEADS = 4
DK = 128
KEY_W = HEADS * DK
CONV_W = 512
CONV_K = 3
IN_W = 2 * KEY_W + 2 * KEY_W + 3 * CONV_W + 2 * D_MODEL
N_EXP = 64
TOP_K = 8
N_GRP = 8
GRP_SZ = N_EXP // N_GRP
TOPK_GRP = 4
EXP_FF = 256
SH_FF = 256
ROUTED_SCALE = 2.5
EPS = 1e-6

C_Q, C_F, C_I, C_G = 0, 512, 1024, 1536
C_BB, C_CC, C_VB = 2048, 2560, 3072
C_MGA, C_MGB = 3584, 4608

MIX_TILE = 512
SUB = 256
CHUNK = 64
ROUTE_TILE = 128
GMM_BM = 512
GMM_NX = 4
GMM_NY = 3
FINAL_TILE = 512
FINAL_SPLIT = (4, 12, 22, 32)
SC_WINDOW = 128
PLANE_W = HALF_D // 2
VMEM_LIMIT = 56 * 1024 * 1024


def _dot(a, b):
    return jnp.dot(a, b, preferred_element_type=F32)


def _dot_nt(a, b):
    return lax.dot_general(a, b, (((1,), (1,)), ((), ())), preferred_element_type=F32)


def _dot_tn(a, b):
    return lax.dot_general(a, b, (((0,), (0,)), ((), ())), preferred_element_type=F32)


def _sigmoid(x):
    return 0.5 * jnp.tanh(0.5 * x) + 0.5


def _silu(x):
    h = 0.5 * x
    return h * jnp.tanh(h) + h


def _rms(x):
    return x * lax.rsqrt(jnp.mean(x * x, axis=-1, keepdims=True) + EPS)


def _lower_bound(lbl):
    a, b = lbl[0:1], lbl[1:2]
    m = jnp.maximum(a, b)
    ea, eb = jnp.exp(a - m), jnp.exp(b - m)
    return ea / (ea + eb)


def _split3(x):
    hi = x.astype(BF16)
    r1 = x - hi.astype(F32)
    mid = r1.astype(BF16)
    lo = (r1 - mid.astype(F32)).astype(BF16)
    return hi, mid, lo


def _words(lo_b, hi_b):
    lo = lax.shift_right_logical(lax.bitcast_convert_type(lo_b.astype(F32), I32), 16)
    hi = lax.bitcast_convert_type(hi_b.astype(F32), I32) & jnp.int32(-65536)
    return lo | hi


def _halves(w):
    lo = lax.bitcast_convert_type(lax.shift_left(w, 16), F32)
    hi = lax.bitcast_convert_type(w & jnp.int32(-65536), F32)
    return lo.astype(BF16), hi.astype(BF16)


def _pack_rows(xb, out_ref):
    words = _words(xb[:, :HALF_D], xb[:, HALF_D:])
    out_ref[0] = words[:, :PLANE_W]
    out_ref[1] = words[:, PLANE_W:]


def _unpack_rows(p0, p1):
    c0, c2 = _halves(p0)
    c1, c3 = _halves(p1)
    return c0, c1, c2, c3


def _ada_kernel(cp_ref, cs_ref, w_ref, b_ref, op_ref, os_ref):
    w = w_ref[...].astype(BF16)
    for c_ref, o_ref in ((cp_ref, op_ref), (cs_ref, os_ref)):
        o_ref[...] = _dot(_silu(c_ref[...]).astype(BF16), w) + b_ref[...]


def _ada(c_prompt, c_sample, w_ada, b_ada):
    blk = 1024
    rows = lambda c: pl.BlockSpec((c.shape[0], D_MODEL), lambda j: (0, 0))
    cols = lambda c: pl.BlockSpec((c.shape[0], blk), lambda j: (0, j))
    return pl.pallas_call(
        _ada_kernel,
        grid=(6 * D_MODEL // blk,),
        in_specs=[rows(c_prompt), rows(c_sample),
                  pl.BlockSpec((D_MODEL, blk), lambda j: (0, j)),
                  pl.BlockSpec((1, blk), lambda j: (0, j))],
        out_specs=[cols(c_prompt), cols(c_sample)],
        out_shape=[jax.ShapeDtypeStruct((c.shape[0], 6 * D_MODEL), F32) for c in (c_prompt, c_sample)],
        name="ada",
    )(c_prompt, c_sample, w_ada, b_ada.reshape(1, -1))


def _shared_expert(h2_cols, w_sgu, w_sd):
    starts = [sum(c.shape[1] for c in h2_cols[:i]) for i in range(len(h2_cols))]
    gu = sum(_dot(c, w_sgu[s:s + c.shape[1], :]) for s, c in zip(starts, h2_cols))
    act = _silu(gu[:, :SH_FF]) * gu[:, SH_FF:]
    return _dot(act.astype(BF16), w_sd)


def _ffn_pre(x1, mod_rows, gffn, wr_hl, shared=None):
    sh2, sc2, g2 = mod_rows
    h2 = _rms(x1) * gffn * (1.0 + sc2) + sh2
    h2b = h2.astype(BF16)
    xmid = x1 if shared is None else x1 + g2 * _shared_expert([h2b], *shared)
    h2lo = (h2 - h2b.astype(F32)).astype(BF16)
    both = _dot_nt(wr_hl, h2b)
    lgt = both[:N_EXP] + both[N_EXP:] + _dot_nt(wr_hl[:N_EXP], h2lo)
    return xmid, h2b, lgt


def _mix_kernel(*refs, nt, n_tiles):
    i = pl.program_id(0)
    rbias_ref, h2_ref = refs[13], refs[15]
    idx_ref, w_ref, rank_ref, cnt_ref, lgt_ref = refs[16], refs[17], refs[18], refs[19], refs[-1]

    @pl.when(i == 0)
    def _():
        lgt_ref[...] = jnp.zeros_like(lgt_ref)
        cnt_ref[...] = jnp.zeros_like(cnt_ref)

    def route_previous(keep):
        carry = cnt_ref[:, 0:1]
        for c0 in range(0, lgt_ref.shape[1], ROUTE_TILE):
            cols = slice(c0, c0 + ROUTE_TILE)
            picks, weights, ranks, carry = _route_tile(lgt_ref[:, cols], rbias_ref[...], carry)
            for k in range(TOP_K):
                idx_ref[k:k + 1, cols] = picks[k]
                w_ref[k:k + 1, cols] = weights[k]
                rank_ref[k:k + 1, cols] = ranks[k]
            yield
        cnt_ref[...] = jnp.broadcast_to(jnp.where(keep, carry, 0.0), cnt_ref.shape)

    @pl.when(i == n_tiles)
    def _():
        h2_ref[...] = jnp.zeros_like(h2_ref)
        for _ in route_previous(True):
            pass

    @pl.when(i < n_tiles)
    def _():
        _mix_tile(i % nt, nt, route_previous(i > 0), *refs)


def _mix_tile(t, nt, routing, x_ref, mod_ref, gmix_ref, gffn_ref, w_in_ref, lbl_ref, hg_ref, cw_ref, cb_ref,
              w_oh_ref, w_oc_ref, w_o_ref, wr_hl_ref, rbias_ref,
              xmid_ref, h2_ref, idx_ref, w_ref, rank_ref, cnt_ref, s_out_ref, cv_out_ref,
              proj_ref, st_ref, cbuf_ref, ya_ref, lgt_ref):
    del rbias_ref, idx_ref, w_ref, rank_ref, cnt_ref
    tt = x_ref.shape[1]

    @pl.when(t == 0)
    def _():
        st_ref[...] = jnp.zeros_like(st_ref)
        cbuf_ref[...] = jnp.zeros_like(cbuf_ref)

    x = x_ref[0]
    mod = mod_ref[0]
    sh1, sc1, g1 = mod[0:1], mod[1:2], mod[2:3]
    h = _rms(x) * gmix_ref[...] * (1.0 + sc1) + sh1
    hb = h.astype(BF16)
    for c in range(0, IN_W, 512):
        proj_ref[:, c:c + 512] = _dot(hb, w_in_ref[:, c:c + 512])
        if c % 1024 == 512:
            next(routing, None)
    for _ in routing:
        pass

    lb = _lower_bound(lbl_ref[...])
    row = lax.broadcasted_iota(I32, (SUB, SUB), 0)
    col = lax.broadcasted_iota(I32, (SUB, SUB), 1)
    tri = (col <= row).astype(BF16)
    mask_d = (col <= row) & (row // CHUNK == col // CHUNK)
    mask_a = row // (2 * CHUNK) == col // (2 * CHUNK)
    n_ch = SUB // CHUNK

    def by_chunk(vals):
        return jnp.concatenate([jnp.zeros((CHUNK, DK), F32) if v is None
                                else jnp.broadcast_to(v, (CHUNK, DK)) for v in vals], axis=0)

    for s in range(tt // SUB):
        r0 = s * SUB
        f = lb + (1.0 - lb) * _sigmoid(proj_ref[r0:r0 + SUB, C_F:C_F + KEY_W])
        kk = 1.0 - f
        hi, mid, lo = _split3(jnp.log(f))
        bc = _dot(tri, hi) + _dot(tri, mid) + _dot(tri, lo)
        for hd in range(HEADS):
            hs = slice(hd * DK, (hd + 1) * DK)
            bh = bc[:, hs]
            at = lambda r: bh[r:r + 1]
            mids = [at(c * CHUNK + CHUNK // 2 - 1) for c in range(n_ch)]
            pair_mid = [at(CHUNK - 1), at(3 * CHUNK - 1)]
            step_mid, step_end = at(2 * CHUNK - 1), at(SUB - 1)
            arg = bh - by_chunk(mids)
            e_pos, e_neg = jnp.exp(arg), jnp.exp(-arg)
            q = _silu(proj_ref[r0:r0 + SUB, C_Q + hd * DK:C_Q + (hd + 1) * DK])
            v = proj_ref[r0:r0 + SUB, C_I + hd * DK:C_I + (hd + 1) * DK]
            qd = q * e_pos
            kd = kk[:, hs] * e_neg
            q_in = qd * by_chunk([jnp.exp(m) for m in mids])
            k_end = kd * by_chunk([jnp.exp(step_end - m) for m in mids])
            qa = qd * by_chunk([None, jnp.exp(mids[1] - pair_mid[0]), None, jnp.exp(mids[3] - pair_mid[1])])
            ka = kd * by_chunk([jnp.exp(pair_mid[0] - mids[0]), None, jnp.exp(pair_mid[1] - mids[2]), None])
            qb = qd * by_chunk([None, None, jnp.exp(mids[2] - step_mid), jnp.exp(mids[3] - step_mid)])
            kb = kd * by_chunk([jnp.exp(step_mid - mids[0]), jnp.exp(step_mid - mids[1]), None, None])
            att = jnp.where(mask_d, _dot_nt(qd.astype(BF16), kd.astype(BF16)), 0.0)
            att = att + jnp.where(mask_a, _dot_nt(qa.astype(BF16), ka.astype(BF16)), 0.0)
            att = att + _dot_nt(qb.astype(BF16), kb.astype(BF16))
            vb = v.astype(BF16)
            st = st_ref[hd]
            o = _dot(att.astype(BF16), vb) + _dot_nt(q_in.astype(BF16), st.astype(BF16))
            st_ref[hd] = st * jnp.exp(step_end) + _dot_tn(vb, k_end.astype(BF16))
            gate = _silu(proj_ref[r0:r0 + SUB, C_G + hd * DK:C_G + (hd + 1) * DK])
            ya_ref[r0:r0 + SUB, hs] = _rms(o) * hg_ref[:, hs] * gate

    u = proj_ref[:, C_CC:C_CC + CONV_W] * proj_ref[:, C_VB:C_VB + CONV_W]
    rows = lax.broadcasted_iota(I32, (tt, CONV_W), 0)
    c0, c1 = cbuf_ref[0:1], cbuf_ref[1:2]
    u1 = jnp.where(rows == 0, c1, pltpu.roll(u, 1, axis=0))
    u2 = jnp.where(rows == 0, c0, jnp.where(rows == 1, c1, pltpu.roll(u, 2, axis=0)))
    conv = cw_ref[0:1] * u2 + cw_ref[1:2] * u1 + cw_ref[2:3] * u + cb_ref[...]
    yb = proj_ref[:, C_BB:C_BB + CONV_W] * conv
    cbuf_ref[...] = u[tt - 2:tt]

    mixed = (_sigmoid(proj_ref[:, C_MGA:C_MGA + D_MODEL]) * _dot(ya_ref[...].astype(BF16), w_oh_ref[...])
             + _sigmoid(proj_ref[:, C_MGB:C_MGB + D_MODEL]) * _dot(yb.astype(BF16), w_oc_ref[...]))
    x1 = x + g1 * _dot(mixed.astype(BF16), w_o_ref[...])

    xmid, h2b, lgt = _ffn_pre(x1, (mod[3:4], mod[4:5], mod[5:6]), gffn_ref[...], wr_hl_ref[...])
    xmid_ref[0] = xmid
    _pack_rows(h2b, h2_ref)
    lgt_ref[...] = lgt

    @pl.when(t == nt - 1)
    def _():
        for hd in range(HEADS):
            s_out_ref[0, hd] = st_ref[hd].T
        cv_out_ref[0] = cbuf_ref[...]


def _const_spec(shape):
    nd = len(shape)
    return pl.BlockSpec(shape, lambda i, _nd=nd: (0,) * _nd, pipeline_mode=pl.Buffered(1))


def _mix(x, mod, n_tok, gmix, gffn, w_in, lbl, hg, cw, cb, w_oh, w_oc, w_o, wr_hl, rbias):
    bsz, seq, _ = x.shape
    tt = MIX_TILE
    nt = seq // tt
    n_tiles = bsz * nt
    assert n_tiles * tt < n_tok <= (n_tiles + 1) * tt and tt % ROUTE_TILE == 0
    consts = [gmix, gffn, w_in, lbl, hg, cw, cb, w_oh, w_oc, w_o, wr_hl, rbias]
    tile = lambda i: jnp.minimum(i, n_tiles - 1)
    routed = pl.BlockSpec((TOP_K, tt), lambda i: (0, jnp.maximum(i - 1, 0)))
    slot = lambda dt: jax.ShapeDtypeStruct((TOP_K, n_tiles * tt), dt)
    return pl.pallas_call(
        functools.partial(_mix_kernel, nt=nt, n_tiles=n_tiles),
        grid=(n_tiles + 1,),
        in_specs=[pl.BlockSpec((1, tt, D_MODEL), lambda i: (tile(i) // nt, tile(i) % nt, 0)),
                  pl.BlockSpec((1, 6, D_MODEL), lambda i: (tile(i) // nt, 0, 0))]
                 + [_const_spec(a.shape) for a in consts],
        out_specs=[pl.BlockSpec((1, tt, D_MODEL), lambda i: (tile(i) // nt, tile(i) % nt, 0)),
                   pl.BlockSpec((2, tt, PLANE_W), lambda i: (0, i, 0)),
                   routed, routed, routed,
                   pl.BlockSpec((N_EXP, 128), lambda i: (0, 0)),
                   pl.BlockSpec((1, HEADS, DK, DK), lambda i: (tile(i) // nt, 0, 0, 0)),
                   pl.BlockSpec((1, CONV_K - 1, CONV_W), lambda i: (tile(i) // nt, 0, 0))],
        out_shape=[jax.ShapeDtypeStruct((bsz, seq, D_MODEL), F32),
                   jax.ShapeDtypeStruct((2, n_tok, PLANE_W), I32),
                   slot(I32), slot(F32), slot(I32),
                   jax.ShapeDtypeStruct((N_EXP, 128), F32),
                   jax.ShapeDtypeStruct((bsz, HEADS, DK, DK), F32),
                   jax.ShapeDtypeStruct((bsz, CONV_K - 1, CONV_W), F32)],
        scratch_shapes=[pltpu.VMEM((tt, IN_W), F32),
                        pltpu.VMEM((HEADS, DK, DK), F32),
                        pltpu.VMEM((CONV_K - 1, CONV_W), F32),
                        pltpu.VMEM((tt, KEY_W), F32),
                        pltpu.VMEM((N_EXP, tt), F32)],
        compiler_params=pltpu.CompilerParams(
            dimension_semantics=("arbitrary",), vmem_limit_bytes=VMEM_LIMIT),
        name="mix",
    )(x, mod, *consts)


def _smp1_kernel(x_ref, mod_ref, gmix_ref, w_in_ref, lbl_ref, cw_ref, cb_ref, cst_ref,
                 f_ref, k_ref, q_ref, v_ref, gate_ref, yb_ref, sga_ref, sgb_ref, cv_out_ref):
    x = x_ref[...]
    sh1, sc1 = mod_ref[:, 0:D_MODEL], mod_ref[:, D_MODEL:2 * D_MODEL]
    h = _rms(x) * gmix_ref[...] * (1.0 + sc1) + sh1
    hb = h.astype(BF16)

    def proj(c, w):
        return _dot(hb, w_in_ref[:, c:c + w])

    lb = _lower_bound(lbl_ref[...])
    f = lb + (1.0 - lb) * _sigmoid(proj(C_F, KEY_W))
    f_ref[...] = f
    k_ref[...] = 1.0 - f
    q_ref[...] = _silu(proj(C_Q, KEY_W))
    v_ref[...] = proj(C_I, KEY_W)
    gate_ref[...] = _silu(proj(C_G, KEY_W))
    u = proj(C_CC, CONV_W) * proj(C_VB, CONV_W)
    c0, c1 = cst_ref[:, 0:CONV_W], cst_ref[:, CONV_W:2 * CONV_W]
    conv = cw_ref[0:1] * c0 + cw_ref[1:2] * c1 + cw_ref[2:3] * u + cb_ref[...]
    yb_ref[...] = proj(C_BB, CONV_W) * conv
    cv_out_ref[:, 0:CONV_W] = c1
    cv_out_ref[:, CONV_W:2 * CONV_W] = u
    sga_ref[...] = _sigmoid(proj(C_MGA, D_MODEL))
    sgb_ref[...] = _sigmoid(proj(C_MGB, D_MODEL))


def _smp1(x, mod, gmix, w_in, lbl, cw, cb, cst):
    n = x.shape[0]
    kw = jax.ShapeDtypeStruct((n, KEY_W), F32)
    dm = jax.ShapeDtypeStruct((n, D_MODEL), F32)
    return pl.pallas_call(
        _smp1_kernel,
        out_shape=[kw, kw, kw, kw, kw, kw, dm, dm,
                   jax.ShapeDtypeStruct((n, (CONV_K - 1) * CONV_W), F32)],
        compiler_params=pltpu.CompilerParams(vmem_limit_bytes=VMEM_LIMIT),
        name="smp1",
    )(x, mod, gmix, w_in, lbl, cw, cb, cst)


def _smp2(f, k, q, v, state):
    n = f.shape[0]
    info = plsc.get_sparse_core_info()
    lanes = info.num_lanes
    n_workers = info.num_cores * info.num_subcores
    tok_per = n // n_workers
    n_chunks = DK // lanes
    vec = lambda: pltpu.VMEM((DK,), F32)

    @pl.kernel(out_type=[jax.ShapeDtypeStruct(state.shape, F32), jax.ShapeDtypeStruct((n, KEY_W), F32)],
               mesh=_sc_mesh(), scratch_types=[pltpu.VMEM((DK, DK), F32), vec(), vec(), vec(), vec(), vec()],
               compiler_params=pltpu.CompilerParams(needs_layout_passes=False), name="smp2")
    def run(f_hbm, k_hbm, q_hbm, v_hbm, s_hbm, s_out_hbm, o_hbm, s_v, f_v, k_v, q_v, v_v, o_v):
        wid = lax.axis_index("subcore") * info.num_cores + lax.axis_index("core")

        def tile(j, carry):
            t = wid * tok_per + j // HEADS
            hcols = pl.ds((j % HEADS) * DK, DK)
            pltpu.sync_copy(s_hbm.at[t, j % HEADS], s_v)
            for src, dst in ((f_hbm, f_v), (k_hbm, k_v), (q_hbm, q_v), (v_hbm, v_v)):
                pltpu.sync_copy(src.at[t, hcols], dst)
            v_chunks = [v_v[pl.ds(c * lanes, lanes)] for c in range(n_chunks)]

            def row(d, acc):
                at = [jnp.zeros((lanes,), I32) + d]
                fd, kd, qd = (plsc.load_gather(r, at) for r in (f_v, k_v, q_v))
                out = []
                for c in range(n_chunks):
                    cols = pl.ds(c * lanes, lanes)
                    new = s_v[d, cols] * fd + kd * v_chunks[c]
                    s_v[d, cols] = new
                    out.append(acc[c] + qd * new)
                return tuple(out)

            acc = lax.fori_loop(0, DK, row, tuple(jnp.zeros((lanes,), F32) for _ in range(n_chunks)))
            for c in range(n_chunks):
                o_v[pl.ds(c * lanes, lanes)] = acc[c]
            pltpu.sync_copy(s_v, s_out_hbm.at[t, j % HEADS])
            pltpu.sync_copy(o_v, o_hbm.at[t, hcols])
            return carry

        lax.fori_loop(0, tok_per * HEADS, tile, 0)

    return run(f, k, q, v, state)


def _smp3_kernel(x_ref, mod_ref, o_ref, gate_ref, yb_ref, sga_ref, sgb_ref, hg_ref, gffn_ref,
                 w_oh_ref, w_oc_ref, w_o_ref, wr_hl_ref, w_sgu_ref, w_sd_ref,
                 rbias_ref, cnt_p_ref, idx_p_ref, w_p_ref, rank_p_ref, h2_all_ref,
                 xmid_ref, h2_ref, idx_ref, w_ref, rank_ref, cnt_ref):
    del h2_all_ref
    parts = []
    for hd in range(HEADS):
        hs = slice(hd * DK, (hd + 1) * DK)
        parts.append(_rms(o_ref[:, hs]) * hg_ref[:, hs] * gate_ref[:, hs])
    ya = jnp.concatenate(parts, axis=1)
    mixed = (sga_ref[...] * _dot(ya.astype(BF16), w_oh_ref[...])
             + sgb_ref[...] * _dot(yb_ref[...].astype(BF16), w_oc_ref[...]))
    g1 = mod_ref[:, 2 * D_MODEL:3 * D_MODEL]
    x1 = x_ref[...] + g1 * _dot(mixed.astype(BF16), w_o_ref[...])
    mod_rows = tuple(mod_ref[:, j * D_MODEL:(j + 1) * D_MODEL] for j in (3, 4, 5))
    xmid, h2b, lgt = _ffn_pre(x1, mod_rows, gffn_ref[...], wr_hl_ref[...],
                              shared=(w_sgu_ref[...], w_sd_ref[...]))
    xmid_ref[...] = xmid
    _pack_rows(h2b, h2_ref)
    n_prompt = idx_p_ref.shape[1]
    picks, weights, ranks, total = _route_tile(lgt, rbias_ref[...], cnt_p_ref[:, 0:1])
    for src, dst, new in ((idx_p_ref, idx_ref, picks), (w_p_ref, w_ref, weights), (rank_p_ref, rank_ref, ranks)):
        dst[:, :n_prompt] = src[...]
        for k in range(TOP_K):
            dst[k:k + 1, n_prompt:] = new[k]
    cnt_ref[...] = jnp.broadcast_to(total, cnt_ref.shape).astype(I32)


def _smp3(x, mod, o, gate, yb, sga, sgb, hg, gffn, w_oh, w_oc, w_o, wr_hl, w_sgu, w_sd,
          rbias, cnt_p, idx_p, w_p, rank_p, h2_all):
    n = x.shape[0]
    vmem_args = [x, mod, o, gate, yb, sga, sgb, hg, gffn, w_oh, w_oc, w_o, wr_hl, w_sgu, w_sd,
                 rbias, cnt_p, idx_p, w_p, rank_p]
    n_prompt = idx_p.shape[1]
    blk = n_prompt // n
    slot = lambda dt: jax.ShapeDtypeStruct((TOP_K, n_prompt + n), dt)

    def full(a):
        nd = a.ndim
        return pl.BlockSpec(a.shape, lambda i, _nd=nd: (0,) * _nd)

    return pl.pallas_call(
        _smp3_kernel,
        grid=(1,),
        in_specs=[full(a) for a in vmem_args] + [pl.BlockSpec(memory_space=pl.ANY)],
        out_specs=[pl.BlockSpec((n, D_MODEL), lambda i: (0, 0)),
                   pl.BlockSpec((2, n, PLANE_W), lambda i: (0, blk, 0)),
                   pl.BlockSpec((TOP_K, n_prompt + n), lambda i: (0, 0)),
                   pl.BlockSpec((TOP_K, n_prompt + n), lambda i: (0, 0)),
                   pl.BlockSpec((TOP_K, n_prompt + n), lambda i: (0, 0)),
                   pl.BlockSpec((N_EXP, 128), lambda i: (0, 0))],
        out_shape=[jax.ShapeDtypeStruct((n, D_MODEL), F32),
                   jax.ShapeDtypeStruct(h2_all.shape, h2_all.dtype),
                   slot(I32), slot(F32), slot(I32),
                   jax.ShapeDtypeStruct((N_EXP, 128), I32)],
        input_output_aliases={len(vmem_args): 1},
        compiler_params=pltpu.CompilerParams(
            dimension_semantics=("arbitrary",), vmem_limit_bytes=VMEM_LIMIT),
        name="smp3",
    )(*vmem_args, h2_all)


def _route_tile(lgt, bias, carry):
    tr = lgt.shape[1]
    neg = -jnp.inf
    scores = _sigmoid(lgt)
    sel = scores + bias
    j8 = lax.broadcasted_iota(I32, (GRP_SZ, tr), 0)
    groups = [sel[g * GRP_SZ:(g + 1) * GRP_SZ] for g in range(N_GRP)]
    gscore = []
    for grp in groups:
        m1 = jnp.max(grp, axis=0, keepdims=True)
        i1 = jnp.min(jnp.where(grp == m1, j8, GRP_SZ), axis=0, keepdims=True)
        m2 = jnp.max(jnp.where(j8 == i1, neg, grp), axis=0, keepdims=True)
        gscore.append(m1 + m2)
    kept = []
    for g in range(N_GRP):
        beaten = jnp.zeros((1, tr), I32)
        for o in range(N_GRP):
            if o < g:
                beaten = beaten + (gscore[o] >= gscore[g]).astype(I32)
            elif o > g:
                beaten = beaten + (gscore[o] > gscore[g]).astype(I32)
        kept.append(jnp.where(beaten < TOPK_GRP, groups[g], neg))
    masked = jnp.concatenate(kept, axis=0)
    ei = lax.broadcasted_iota(I32, masked.shape, 0)
    chosen = jnp.zeros(masked.shape, jnp.bool_)
    picks, weights = [], []
    for _ in range(TOP_K):
        m = jnp.max(masked, axis=0, keepdims=True)
        pick = jnp.min(jnp.where((masked == m) & ~chosen, ei, N_EXP), axis=0, keepdims=True)
        hit = ei == pick
        weights.append(jnp.sum(jnp.where(hit, scores, 0.0), axis=0, keepdims=True))
        picks.append(pick)
        chosen = chosen | hit
        masked = jnp.where(hit, neg, masked)
    wsum = weights[0]
    for w in weights[1:]:
        wsum = wsum + w
    sel01 = chosen.astype(F32)
    r = lax.broadcasted_iota(I32, (tr, tr), 0)
    c = lax.broadcasted_iota(I32, (tr, tr), 1)
    before = (r < c).astype(BF16)
    cnt = _dot(sel01.astype(BF16), before) + carry
    weights = [w / wsum * ROUTED_SCALE for w in weights]
    ranks = [jnp.sum(jnp.where(ei == p, cnt, 0.0), axis=0, keepdims=True).astype(I32) for p in picks]
    return picks, weights, ranks, carry + jnp.sum(sel01, axis=1, keepdims=True)


def _dest_kernel(start_ref, idx_ref, rank_ref, all_ref, *chunk_refs, n_rows, chunks):
    n_tok = idx_ref.shape[1]
    idx = idx_ref[...]
    acc = rank_ref[...]
    for e in range(N_EXP):
        acc = acc + jnp.where(idx == e, start_ref[e], 0)
    for k in range(TOP_K):
        for p in range(2):
            row = acc[k:k + 1, :] + p * n_rows
            seg = 2 * k + p
            all_ref[:, seg * n_tok:(seg + 1) * n_tok] = row
            for (c0, c1), ref in zip(chunks, chunk_refs):
                ref[:, seg * (c1 - c0):(seg + 1) * (c1 - c0)] = row[:, c0:c1]


def _dest(pad_start, idx, rank, n_rows, chunks):
    k, n_tok = idx.shape
    vmem = pl.BlockSpec(memory_space=pltpu.VMEM)
    sizes = [n_tok] + [c1 - c0 for c0, c1 in chunks]
    return pl.pallas_call(
        functools.partial(_dest_kernel, n_rows=n_rows, chunks=chunks),
        in_specs=[pl.BlockSpec(memory_space=pltpu.SMEM), vmem, vmem],
        out_specs=[vmem] * len(sizes),
        out_shape=[jax.ShapeDtypeStruct((1, 2 * k * n), I32) for n in sizes],
        name="dest",
    )(pad_start, idx, rank)


def _sc_mesh():
    return plsc.VectorSubcoreMesh(core_axis_name="core", subcore_axis_name="subcore")


def _dispatch(rows, dest, n_out):
    n, width = rows.shape
    win = SC_WINDOW
    steps = n // win

    @pl.kernel(out_type=jax.ShapeDtypeStruct((n_out, width), rows.dtype), mesh=_sc_mesh(),
               scratch_types=[], name="dispatch")
    def run(x_hbm, *refs):
        i_hbms, o_hbm = refs[:TOP_K], refs[TOP_K]

        def body(x_vmem, *i_vmems):
            for i_vmem in i_vmems:
                pltpu.sync_copy(x_vmem, o_hbm.at[i_vmem.at[0]])

        pltpu.emit_pipeline(
            body,
            grid=(steps,),
            in_specs=[pl.BlockSpec((win, width), lambda i: (i, 0))]
                     + [pl.BlockSpec((1, win), lambda i, k=k: (0, k * steps + i)) for k in range(TOP_K)],
            out_specs=[],
            core_axis_name=("core", "subcore"),
            dimension_semantics=(pltpu.PARALLEL,),
        )(x_hbm, *i_hbms)

    assert dest.shape == (1, TOP_K * n)
    return run(rows, *([dest] * TOP_K))


def _combine(rows, dest_flat):
    width = rows.shape[1]
    n = dest_flat.shape[1]
    win = SC_WINDOW

    @pl.kernel(out_type=jax.ShapeDtypeStruct((n, width), rows.dtype), mesh=_sc_mesh(),
               scratch_types=[], name="combine")
    def run(y_hbm, i_hbm, o_hbm):
        def body(i_vmem, o_vmem):
            pltpu.sync_copy(y_hbm.at[i_vmem.at[0]], o_vmem)

        pltpu.emit_pipeline(
            body,
            grid=(n // win,),
            in_specs=[pl.BlockSpec((1, win), lambda i: (0, i))],
            out_specs=[pl.BlockSpec((win, width), lambda i: (i, 0))],
            core_axis_name=("core", "subcore"),
            dimension_semantics=(pltpu.PARALLEL,),
        )(i_hbm, o_hbm)

    return run(rows, dest_flat)


def _gmm_kernel(blk_exp_ref, n_used_ref, xs_hbm, wg_hbm, wu_hbm, wd_hbm, ys_hbm,
                xbuf, ybuf, wg32, wu32, wd32, wgu_b, wd_b, xsem, ysem, wsem, run_ref):
    nx, ny, bm = xbuf.shape[0], ybuf.shape[0], xbuf.shape[2]
    n_used = n_used_ref[0]

    def x_copies(b):
        rows, slot = pl.ds(b * bm, bm), b % nx
        return [pltpu.make_async_copy(xs_hbm.at[p, rows, :], xbuf.at[slot, p], xsem.at[slot, p]) for p in range(2)]

    def y_copies(b):
        rows, slot = pl.ds(b * bm, bm), b % ny
        return [pltpu.make_async_copy(ybuf.at[slot, p], ys_hbm.at[p, rows, :], ysem.at[slot, p]) for p in range(2)]

    def start(copies):
        for c in copies:
            c.start()

    def wait(copies):
        for c in copies:
            c.wait()

    def w_copies(e, slot):
        return (pltpu.make_async_copy(wg_hbm.at[e], wg32.at[slot], wsem.at[slot, 0]),
                pltpu.make_async_copy(wu_hbm.at[e], wu32.at[slot], wsem.at[slot, 1]),
                pltpu.make_async_copy(wd_hbm.at[e], wd32.at[slot], wsem.at[slot, 2]))

    def run_end(b):
        return lax.while_loop(lambda j: (j < n_used) & (blk_exp_ref[jnp.minimum(j, n_used - 1)] == blk_exp_ref[b]),
                              lambda j: j + 1, b + 1)

    run_ref[0] = 0
    start(x_copies(0))
    start(w_copies(blk_exp_ref[0], 0))
    for j in range(1, nx - 1):
        @pl.when(j < n_used)
        def _():
            start(x_copies(j))

    def block(b, carry):
        @pl.when(b + nx - 1 < n_used)
        def _():
            start(x_copies(b + nx - 1))

        @pl.when((b == 0) | (blk_exp_ref[b] != blk_exp_ref[jnp.maximum(b - 1, 0)]))
        def _():
            wslot = run_ref[0] % 2
            run_ref[0] = run_ref[0] + 1
            wait(w_copies(blk_exp_ref[b], wslot))
            wgu_b[:, 0:EXP_FF] = wg32[wslot].astype(BF16)
            wgu_b[:, EXP_FF:2 * EXP_FF] = wu32[wslot].astype(BF16)
            wd_b[...] = wd32[wslot].astype(BF16)
            nxt = run_end(b)

            @pl.when(nxt < n_used)
            def _():
                start(w_copies(blk_exp_ref[jnp.minimum(nxt, n_used - 1)], 1 - wslot))

        wait(x_copies(b))

        @pl.when(b >= ny)
        def _():
            wait(y_copies(b - ny))

        xslot = b % nx
        xc = _unpack_rows(xbuf[xslot, 0], xbuf[xslot, 1])
        gu = sum(_dot(c, wgu_b[i * PLANE_W:(i + 1) * PLANE_W, :]) for i, c in enumerate(xc))
        act = (_silu(gu[:, :EXP_FF]) * gu[:, EXP_FF:]).astype(BF16)
        _pack_rows(_dot(act, wd_b[...]).astype(BF16), ybuf.at[b % ny])
        start(y_copies(b))
        return carry

    lax.fori_loop(0, n_used, block, 0)

    for j in range(ny, 0, -1):
        @pl.when(n_used >= j)
        def _():
            wait(y_copies(n_used - j))


def _gmm(blk_exp, n_used, xs, w_gate, w_up, w_down):
    n_rows = xs.shape[1]
    bm = GMM_BM
    nb = n_rows // bm

    assert blk_exp.shape == (nb,)
    any_spec = pl.BlockSpec(memory_space=pl.ANY)
    grid_spec = pltpu.PrefetchScalarGridSpec(
        num_scalar_prefetch=2,
        grid=(1,),
        in_specs=[any_spec, any_spec, any_spec, any_spec],
        out_specs=any_spec,
        scratch_shapes=[pltpu.VMEM((GMM_NX, 2, bm, PLANE_W), I32), pltpu.VMEM((GMM_NY, 2, bm, PLANE_W), I32),
                        pltpu.VMEM((2, D_MODEL, EXP_FF), F32), pltpu.VMEM((2, D_MODEL, EXP_FF), F32),
                        pltpu.VMEM((2, EXP_FF, D_MODEL), F32),
                        pltpu.VMEM((D_MODEL, 2 * EXP_FF), BF16), pltpu.VMEM((EXP_FF, D_MODEL), BF16),
                        pltpu.SemaphoreType.DMA((GMM_NX, 2)), pltpu.SemaphoreType.DMA((GMM_NY, 2)),
                        pltpu.SemaphoreType.DMA((2, 3)), pltpu.SMEM((1,), I32)],
    )
    return pl.pallas_call(
        _gmm_kernel,
        grid_spec=grid_spec,
        out_shape=jax.ShapeDtypeStruct((2, n_rows, PLANE_W), I32),
        compiler_params=pltpu.CompilerParams(dimension_semantics=("arbitrary",)),
        name="gmm",
    )(blk_exp, n_used, xs, w_gate, w_up, w_down)


def _final_kernel(xmid_ref, g2_ref, z_ref, w_ref, gfin_ref, *rest, with_shared):
    y_ref = rest[-1]
    accs = [jnp.zeros((xmid_ref.shape[0], PLANE_W), F32) for _ in range(4)]
    w_cols = w_ref[...].T
    for k in range(TOP_K):
        wk = w_cols[:, k:k + 1]
        cols = _unpack_rows(z_ref[k, 0], z_ref[k, 1])
        accs = [a + wk * c.astype(F32) for a, c in zip(accs, cols)]
    acc = jnp.concatenate(accs, axis=1)
    if with_shared:
        h2_ref, w_sgu_ref, w_sd_ref = rest[:3]
        acc = acc + _shared_expert(_unpack_rows(h2_ref[0], h2_ref[1]), w_sgu_ref[...], w_sd_ref[...])
    y_ref[...] = _rms(xmid_ref[...] + g2_ref[0] * acc) * gfin_ref[...]


def _final(xmid, g2, z, w_t, gfin, tile, *, n_tiles, x_tile0, z_tile0, w_tile0, tiles_per_g2,
           shared=None, y_prev=None):
    args = [xmid, g2, z, w_t, gfin]
    in_specs = [pl.BlockSpec((tile, D_MODEL), lambda i: (x_tile0 + i, 0)),
                pl.BlockSpec((1, g2.shape[1], D_MODEL), lambda i: ((x_tile0 + i) // tiles_per_g2, 0, 0)),
                pl.BlockSpec((TOP_K, 2, tile, PLANE_W), lambda i: (0, 0, z_tile0 + i, 0)),
                pl.BlockSpec((TOP_K, tile), lambda i: (0, w_tile0 + i)),
                pl.BlockSpec((1, D_MODEL), lambda i: (0, 0))]
    if shared is not None:
        h2_all, w_sgu, w_sd = shared
        args += [h2_all, w_sgu, w_sd]
        in_specs += [pl.BlockSpec((2, tile, PLANE_W), lambda i: (0, x_tile0 + i, 0)),
                     pl.BlockSpec(w_sgu.shape, lambda i: (0, 0)), pl.BlockSpec(w_sd.shape, lambda i: (0, 0))]
    aliases = {}
    if y_prev is not None:
        args.append(y_prev)
        in_specs.append(pl.BlockSpec(memory_space=pl.ANY))
        aliases = {len(args) - 1: 0}
    return pl.pallas_call(
        functools.partial(_final_kernel, with_shared=shared is not None),
        grid=(n_tiles,),
        in_specs=in_specs,
        out_specs=pl.BlockSpec((tile, D_MODEL), lambda i: (x_tile0 + i, 0)),
        out_shape=jax.ShapeDtypeStruct(xmid.shape, F32),
        input_output_aliases=aliases,
        compiler_params=pltpu.CompilerParams(
            dimension_semantics=("arbitrary",), vmem_limit_bytes=VMEM_LIMIT),
        name="final",
    )(*args)


def kernel(x_prompt, x_sample, state_hgrn, state_conv, c_prompt, c_sample, w_ada, b_ada, norm_mix_g, norm_ffn_g, w_in, lb_logits, hgrn_norm_g, conv_w, conv_b, w_out_hgrn, w_out_conv, w_o, w_router, router_bias, w_exp_gate, w_exp_up, w_exp_down, w_sh_gate, w_sh_up, w_sh_down, final_norm_g):
    assert w_ada.shape[0] == 1 and lb_logits.shape[0] == 2
    bsz, seq, _ = x_prompt.shape
    n_smp = x_sample.shape[0]
    n_prompt = bsz * seq
    n_tok = n_prompt + n_smp

    w_in_b = w_in[0].astype(BF16)
    w_oh_b = w_out_hgrn[0].astype(BF16)
    w_oc_b = w_out_conv[0].astype(BF16)
    w_o_b = w_o[0].astype(BF16)
    wr_t = w_router[0].T
    wr_hi = wr_t.astype(BF16)
    wr_hl = jnp.concatenate([wr_hi, (wr_t - wr_hi.astype(F32)).astype(BF16)], axis=0)
    w_sgu = jnp.concatenate([w_sh_gate[0], w_sh_up[0]], axis=1).astype(BF16)
    w_sd = w_sh_down[0].astype(BF16)
    gmix = norm_mix_g[0].reshape(1, D_MODEL)
    gffn = norm_ffn_g[0].reshape(1, D_MODEL)
    hg = hgrn_norm_g[0].reshape(1, KEY_W)
    cw = conv_w[0]
    cb = conv_b[0].reshape(1, CONV_W)
    gfin = final_norm_g.reshape(1, D_MODEL)

    mod_p, mod_s = _ada(c_prompt, c_sample, w_ada[0], b_ada[0])

    rbias = router_bias[0].reshape(N_EXP, 1)
    xmid_p, h2_all, idx_p, w_p, rank_p, cnt_p, s_p, cv_p = _mix(
        x_prompt, mod_p.reshape(bsz, 6, D_MODEL), n_tok, gmix, gffn, w_in_b, lb_logits, hg, cw, cb, w_oh_b, w_oc_b, w_o_b, wr_hl, rbias)

    xs2 = x_sample.reshape(n_smp, D_MODEL)
    f, kk, q, v, gate, yb, sga, sgb, cv_s = _smp1(
        xs2, mod_s, gmix, w_in_b, lb_logits, cw, cb, state_conv[0].reshape(n_smp, (CONV_K - 1) * CONV_W))
    s_s, o_s = _smp2(f, kk, q, v, state_hgrn[0])
    xmid_s, h2_all, idx, w_tok, rank, cnt = _smp3(xs2, mod_s, o_s, gate, yb, sga, sgb, hg, gffn,
                                                  w_oh_b, w_oc_b, w_o_b, wr_hl, w_sgu, w_sd,
                                                  rbias, cnt_p, idx_p, w_p, rank_p, h2_all)

    bm = GMM_BM
    n_blocks = (n_tok * TOP_K + N_EXP * (bm - 1)) // bm
    n_rows = n_blocks * bm
    counts = cnt[:, 0]
    padded = (counts + bm - 1) // bm * bm
    pad_end = jnp.cumsum(padded)
    pad_start = pad_end - padded
    blk_row0 = jnp.arange(n_blocks, dtype=I32) * bm
    blk_exp = jnp.minimum(jnp.sum((pad_end[None, :] <= blk_row0[:, None]).astype(I32), axis=1), N_EXP - 1)
    n_used = (pad_end[-1:] // bm).astype(I32)
    cuts = [0] + [n_prompt * f // FINAL_SPLIT[-1] for f in FINAL_SPLIT]
    assert all(c % FINAL_TILE == 0 for c in cuts) and cuts[-1] == n_prompt
    chunks = tuple((c0, n_tok if c1 == n_prompt else c1) for c0, c1 in zip(cuts[:-1], cuts[1:]))
    dest, *chunk_dest = _dest(pad_start.astype(I32), idx, rank, n_rows, chunks)

    xs = _dispatch(h2_all.reshape(2 * n_tok, PLANE_W), dest, 2 * n_rows).reshape(2, n_rows, PLANE_W)
    ys = _gmm(blk_exp, n_used, xs, w_exp_gate[0], w_exp_up[0], w_exp_down[0])

    ys_flat = ys.reshape(2 * n_rows, PLANE_W)
    w_t = w_tok
    xmid_p2 = xmid_p.reshape(n_prompt, D_MODEL)
    g2_p = mod_p[:, 5 * D_MODEL:].reshape(bsz, 1, D_MODEL)
    g2_s = mod_s[:, 5 * D_MODEL:].reshape(1, n_smp, D_MODEL)
    y_p = None
    for (c0, c1), dest_c in zip(chunks, chunk_dest):
        z = _combine(ys_flat, dest_c).reshape(TOP_K, 2, c1 - c0, PLANE_W)
        y_p = _final(xmid_p2, g2_p, z, w_t, gfin, FINAL_TILE, n_tiles=(min(c1, n_prompt) - c0) // FINAL_TILE,
                     x_tile0=c0 // FINAL_TILE, z_tile0=0, w_tile0=c0 // FINAL_TILE,
                     tiles_per_g2=seq // FINAL_TILE, shared=(h2_all, w_sgu, w_sd), y_prev=y_p)
    y_s = _final(xmid_s, g2_s, z, w_t, gfin, n_smp, n_tiles=1, x_tile0=0, z_tile0=(n_prompt - c0) // n_smp,
                 w_tile0=n_prompt // n_smp, tiles_per_g2=1)

    return (y_p.reshape(bsz, seq, D_MODEL), y_s.reshape(n_smp, 1, D_MODEL),
            s_p[None], cv_p[None], s_s[None], cv_s.reshape(1, n_smp, CONV_K - 1, CONV_W))
```

```python
import functools

import jax
import jax.numpy as jnp
from jax import lax
from jax.experimental import pallas as pl
from jax.experimental.pallas import tpu as pltpu
from jax.experimental.pallas import tpu_sc as plsc

F32 = jnp.float32
BF16 = jnp.bfloat16
I32 = jnp.int32

D_MODEL = 1024
HALF_D = D_MODEL // 2
HEADS = 4
DK = 128
KEY_W = HEADS * DK
CONV_W = 512
CONV_K = 3
IN_W = 2 * KEY_W + 2 * KEY_W + 3 * CONV_W + 2 * D_MODEL
N_EXP = 64
TOP_K = 8
N_GRP = 8
GRP_SZ = N_EXP // N_GRP
TOPK_GRP = 4
EXP_FF = 256
SH_FF = 256
ROUTED_SCALE = 2.5
EPS = 1e-6

C_Q, C_F, C_I, C_G = 0, 512, 1024, 1536
C_BB, C_CC, C_VB = 2048, 2560, 3072
C_MGA, C_MGB = 3584, 4608

MIX_TILE = 512
SUB = 256
CHUNK = 64
ROUTE_TILE = 128
GMM_BM = 512
GMM_NX = 4
GMM_NY = 3
FINAL_TILE = 512
FINAL_SPLIT = (4, 12, 22, 32)
SC_WINDOW = 128
PLANE_W = HALF_D // 2
VMEM_LIMIT = 56 * 1024 * 1024


def _dot(a, b):
    return jnp.dot(a, b, preferred_element_type=F32)


def _dot_nt(a, b):
    return lax.dot_general(a, b, (((1,), (1,)), ((), ())), preferred_element_type=F32)


def _dot_tn(a, b):
    return lax.dot_general(a, b, (((0,), (0,)), ((), ())), preferred_element_type=F32)


def _sigmoid(x):
    return 0.5 * jnp.tanh(0.5 * x) + 0.5


def _silu(x):
    h = 0.5 * x
    return h * jnp.tanh(h) + h


def _rms(x):
    return x * lax.rsqrt(jnp.mean(x * x, axis=-1, keepdims=True) + EPS)


def _lower_bound(lbl):
    a, b = lbl[0:1], lbl[1:2]
    m = jnp.maximum(a, b)
    ea, eb = jnp.exp(a - m), jnp.exp(b - m)
    return ea / (ea + eb)


def _split3(x):
    hi = x.astype(BF16)
    r1 = x - hi.astype(F32)
    mid = r1.astype(BF16)
    lo = (r1 - mid.astype(F32)).astype(BF16)
    return hi, mid, lo


def _words(lo_b, hi_b):
    lo = lax.shift_right_logical(lax.bitcast_convert_type(lo_b.astype(F32), I32), 16)
    hi = lax.bitcast_convert_type(hi_b.astype(F32), I32) & jnp.int32(-65536)
    return lo | hi


def _halves(w):
    lo = lax.bitcast_convert_type(lax.shift_left(w, 16), F32)
    hi = lax.bitcast_convert_type(w & jnp.int32(-65536), F32)
    return lo.astype(BF16), hi.astype(BF16)


def _pack_rows(xb, out_ref):
    words = _words(xb[:, :HALF_D], xb[:, HALF_D:])
    out_ref[0] = words[:, :PLANE_W]
    out_ref[1] = words[:, PLANE_W:]


def _unpack_rows(p0, p1):
    c0, c2 = _halves(p0)
    c1, c3 = _halves(p1)
    return c0, c1, c2, c3


def _ada_kernel(cp_ref, cs_ref, w_ref, b_ref, op_ref, os_ref):
    w = w_ref[...].astype(BF16)
    for c_ref, o_ref in ((cp_ref, op_ref), (cs_ref, os_ref)):
        o_ref[...] = _dot(_silu(c_ref[...]).astype(BF16), w) + b_ref[...]


def _ada(c_prompt, c_sample, w_ada, b_ada):
    blk = 1024
    rows = lambda c: pl.BlockSpec((c.shape[0], D_MODEL), lambda j: (0, 0))
    cols = lambda c: pl.BlockSpec((c.shape[0], blk), lambda j: (0, j))
    return pl.pallas_call(
        _ada_kernel,
        grid=(6 * D_MODEL // blk,),
        in_specs=[rows(c_prompt), rows(c_sample),
                  pl.BlockSpec((D_MODEL, blk), lambda j: (0, j)),
                  pl.BlockSpec((1, blk), lambda j: (0, j))],
        out_specs=[cols(c_prompt), cols(c_sample)],
        out_shape=[jax.ShapeDtypeStruct((c.shape[0], 6 * D_MODEL), F32) for c in (c_prompt, c_sample)],
        name="ada",
    )(c_prompt, c_sample, w_ada, b_ada.reshape(1, -1))


def _shared_expert(h2_cols, w_sgu, w_sd):
    starts = [sum(c.shape[1] for c in h2_cols[:i]) for i in range(len(h2_cols))]
    gu = sum(_dot(c, w_sgu[s:s + c.shape[1], :]) for s, c in zip(starts, h2_cols))
    act = _silu(gu[:, :SH_FF]) * gu[:, SH_FF:]
    return _dot(act.astype(BF16), w_sd)


def _ffn_pre(x1, mod_rows, gffn, wr_hl, shared=None):
    sh2, sc2, g2 = mod_rows
    h2 = _rms(x1) * gffn * (1.0 + sc2) + sh2
    h2b = h2.astype(BF16)
    xmid = x1 if shared is None else x1 + g2 * _shared_expert([h2b], *shared)
    h2lo = (h2 - h2b.astype(F32)).astype(BF16)
    both = _dot_nt(wr_hl, h2b)
    lgt = both[:N_EXP] + both[N_EXP:] + _dot_nt(wr_hl[:N_EXP], h2lo)
    return xmid, h2b, lgt


def _mix_kernel(*refs, nt, n_tiles):
    i = pl.program_id(0)
    rbias_ref, h2_ref = refs[13], refs[15]
    idx_ref, w_ref, rank_ref, cnt_ref, lgt_ref = refs[16], refs[17], refs[18], refs[19], refs[-1]

    @pl.when(i == 0)
    def _():
        lgt_ref[...] = jnp.zeros_like(lgt_ref)
        cnt_ref[...] = jnp.zeros_like(cnt_ref)

    def route_previous(keep):
        carry = cnt_ref[:, 0:1]
        for c0 in range(0, lgt_ref.shape[1], ROUTE_TILE):
            cols = slice(c0, c0 + ROUTE_TILE)
            picks, weights, ranks, carry = _route_tile(lgt_ref[:, cols], rbias_ref[...], carry)
            for k in range(TOP_K):
                idx_ref[k:k + 1, cols] = picks[k]
                w_ref[k:k + 1, cols] = weights[k]
                rank_ref[k:k + 1, cols] = ranks[k]
            yield
        cnt_ref[...] = jnp.broadcast_to(jnp.where(keep, carry, 0.0), cnt_ref.shape)

    @pl.when(i == n_tiles)
    def _():
        h2_ref[...] = jnp.zeros_like(h2_ref)
        for _ in route_previous(True):
            pass

    @pl.when(i < n_tiles)
    def _():
        _mix_tile(i % nt, nt, route_previous(i > 0), *refs)


def _mix_tile(t, nt, routing, x_ref, mod_ref, gmix_ref, gffn_ref, w_in_ref, lbl_ref, hg_ref, cw_ref, cb_ref,
              w_oh_ref, w_oc_ref, w_o_ref, wr_hl_ref, rbias_ref,
              xmid_ref, h2_ref, idx_ref, w_ref, rank_ref, cnt_ref, s_out_ref, cv_out_ref,
              proj_ref, st_ref, cbuf_ref, ya_ref, lgt_ref):
    del rbias_ref, idx_ref, w_ref, rank_ref, cnt_ref
    tt = x_ref.shape[1]

    @pl.when(t == 0)
    def _():
        st_ref[...] = jnp.zeros_like(st_ref)
        cbuf_ref[...] = jnp.zeros_like(cbuf_ref)

    x = x_ref[0]
    mod = mod_ref[0]
    sh1, sc1, g1 = mod[0:1], mod[1:2], mod[2:3]
    h = _rms(x) * gmix_ref[...] * (1.0 + sc1) + sh1
    hb = h.astype(BF16)
    for c in range(0, IN_W, 512):
        proj_ref[:, c:c + 512] = _dot(hb, w_in_ref[:, c:c + 512])
        if c % 1024 == 512:
            next(routing, None)
    for _ in routing:
        pass

    lb = _lower_bound(lbl_ref[...])
    row = lax.broadcasted_iota(I32, (SUB, SUB), 0)
    col = lax.broadcasted_iota(I32, (SUB, SUB), 1)
    tri = (col <= row).astype(BF16)
    mask_d = (col <= row) & (row // CHUNK == col // CHUNK)
    mask_a = row // (2 * CHUNK) == col // (2 * CHUNK)
    n_ch = SUB // CHUNK

    def by_chunk(vals):
        return jnp.concatenate([jnp.zeros((CHUNK, DK), F32) if v is None
                                else jnp.broadcast_to(v, (CHUNK, DK)) for v in vals], axis=0)

    for s in range(tt // SUB):
        r0 = s * SUB
        f = lb + (1.0 - lb) * _sigmoid(proj_ref[r0:r0 + SUB, C_F:C_F + KEY_W])
        kk = 1.0 - f
        hi, mid, lo = _split3(jnp.log(f))
        bc = _dot(tri, hi) + _dot(tri, mid) + _dot(tri, lo)
        for hd in range(HEADS):
            hs = slice(hd * DK, (hd + 1) * DK)
            bh = bc[:, hs]
            at = lambda r: bh[r:r + 1]
            mids = [at(c * CHUNK + CHUNK // 2 - 1) for c in range(n_ch)]
            pair_mid = [at(CHUNK - 1), at(3 * CHUNK - 1)]
            step_mid, step_end = at(2 * CHUNK - 1), at(SUB - 1)
            arg = bh - by_chunk(mids)
            e_pos, e_neg = jnp.exp(arg), jnp.exp(-arg)
            q = _silu(proj_ref[r0:r0 + SUB, C_Q + hd * DK:C_Q + (hd + 1) * DK])
            v = proj_ref[r0:r0 + SUB, C_I + hd * DK:C_I + (hd + 1) * DK]
            qd = q * e_pos
            kd = kk[:, hs] * e_neg
            q_in = qd * by_chunk([jnp.exp(m) for m in mids])
            k_end = kd * by_chunk([jnp.exp(step_end - m) for m in mids])
            qa = qd * by_chunk([None, jnp.exp(mids[1] - pair_mid[0]), None, jnp.exp(mids[3] - pair_mid[1])])
            ka = kd * by_chunk([jnp.exp(pair_mid[0] - mids[0]), None, jnp.exp(pair_mid[1] - mids[2]), None])
            qb = qd * by_chunk([None, None, jnp.exp(mids[2] - step_mid), jnp.exp(mids[3] - step_mid)])
            kb = kd * by_chunk([jnp.exp(step_mid - mids[0]), jnp.exp(step_mid - mids[1]), None, None])
            att = jnp.where(mask_d, _dot_nt(qd.astype(BF16), kd.astype(BF16)), 0.0)
            att = att + jnp.where(mask_a, _dot_nt(qa.astype(BF16), ka.astype(BF16)), 0.0)
            att = att + _dot_nt(qb.astype(BF16), kb.astype(BF16))
            vb = v.astype(BF16)
            st = st_ref[hd]
            o = _dot(att.astype(BF16), vb) + _dot_nt(q_in.astype(BF16), st.astype(BF16))
            st_ref[hd] = st * jnp.exp(step_end) + _dot_tn(vb, k_end.astype(BF16))
            gate = _silu(proj_ref[r0:r0 + SUB, C_G + hd * DK:C_G + (hd + 1) * DK])
            ya_ref[r0:r0 + SUB, hs] = _rms(o) * hg_ref[:, hs] * gate

    u = proj_ref[:, C_CC:C_CC + CONV_W] * proj_ref[:, C_VB:C_VB + CONV_W]
    rows = lax.broadcasted_iota(I32, (tt, CONV_W), 0)
    c0, c1 = cbuf_ref[0:1], cbuf_ref[1:2]
    u1 = jnp.where(rows == 0, c1, pltpu.roll(u, 1, axis=0))
    u2 = jnp.where(rows == 0, c0, jnp.where(rows == 1, c1, pltpu.roll(u, 2, axis=0)))
    conv = cw_ref[0:1] * u2 + cw_ref[1:2] * u1 + cw_ref[2:3] * u + cb_ref[...]
    yb = proj_ref[:, C_BB:C_BB + CONV_W] * conv
    cbuf_ref[...] = u[tt - 2:tt]

    mixed = (_sigmoid(proj_ref[:, C_MGA:C_MGA + D_MODEL]) * _dot(ya_ref[...].astype(BF16), w_oh_ref[...])
             + _sigmoid(proj_ref[:, C_MGB:C_MGB + D_MODEL]) * _dot(yb.astype(BF16), w_oc_ref[...]))
    x1 = x + g1 * _dot(mixed.astype(BF16), w_o_ref[...])

    xmid, h2b, lgt = _ffn_pre(x1, (mod[3:4], mod[4:5], mod[5:6]), gffn_ref[...], wr_hl_ref[...])
    xmid_ref[0] = xmid
    _pack_rows(h2b, h2_ref)
    lgt_ref[...] = lgt

    @pl.when(t == nt - 1)
    def _():
        for hd in range(HEADS):
            s_out_ref[0, hd] = st_ref[hd].T
        cv_out_ref[0] = cbuf_ref[...]


def _const_spec(shape):
    nd = len(shape)
    return pl.BlockSpec(shape, lambda i, _nd=nd: (0,) * _nd, pipeline_mode=pl.Buffered(1))


def _mix(x, mod, n_tok, gmix, gffn, w_in, lbl, hg, cw, cb, w_oh, w_oc, w_o, wr_hl, rbias):
    bsz, seq, _ = x.shape
    tt = MIX_TILE
    nt = seq // tt
    n_tiles = bsz * nt
    assert n_tiles * tt < n_tok <= (n_tiles + 1) * tt and tt % ROUTE_TILE == 0
    consts = [gmix, gffn, w_in, lbl, hg, cw, cb, w_oh, w_oc, w_o, wr_hl, rbias]
    tile = lambda i: jnp.minimum(i, n_tiles - 1)
    routed = pl.BlockSpec((TOP_K, tt), lambda i: (0, jnp.maximum(i - 1, 0)))
    slot = lambda dt: jax.ShapeDtypeStruct((TOP_K, n_tiles * tt), dt)
    return pl.pallas_call(
        functools.partial(_mix_kernel, nt=nt, n_tiles=n_tiles),
        grid=(n_tiles + 1,),
        in_specs=[pl.BlockSpec((1, tt, D_MODEL), lambda i: (tile(i) // nt, tile(i) % nt, 0)),
                  pl.BlockSpec((1, 6, D_MODEL), lambda i: (tile(i) // nt, 0, 0))]
                 + [_const_spec(a.shape) for a in consts],
        out_specs=[pl.BlockSpec((1, tt, D_MODEL), lambda i: (tile(i) // nt, tile(i) % nt, 0)),
                   pl.BlockSpec((2, tt, PLANE_W), lambda i: (0, i, 0)),
                   routed, routed, routed,
                   pl.BlockSpec((N_EXP, 128), lambda i: (0, 0)),
                   pl.BlockSpec((1, HEADS, DK, DK), lambda i: (tile(i) // nt, 0, 0, 0)),
                   pl.BlockSpec((1, CONV_K - 1, CONV_W), lambda i: (tile(i) // nt, 0, 0))],
        out_shape=[jax.ShapeDtypeStruct((bsz, seq, D_MODEL), F32),
                   jax.ShapeDtypeStruct((2, n_tok, PLANE_W), I32),
                   slot(I32), slot(F32), slot(I32),
                   jax.ShapeDtypeStruct((N_EXP, 128), F32),
                   jax.ShapeDtypeStruct((bsz, HEADS, DK, DK), F32),
                   jax.ShapeDtypeStruct((bsz, CONV_K - 1, CONV_W), F32)],
        scratch_shapes=[pltpu.VMEM((tt, IN_W), F32),
                        pltpu.VMEM((HEADS, DK, DK), F32),
                        pltpu.VMEM((CONV_K - 1, CONV_W), F32),
                        pltpu.VMEM((tt, KEY_W), F32),
                        pltpu.VMEM((N_EXP, tt), F32)],
        compiler_params=pltpu.CompilerParams(
            dimension_semantics=("arbitrary",), vmem_limit_bytes=VMEM_LIMIT),
        name="mix",
    )(x, mod, *consts)


def _smp1_kernel(x_ref, mod_ref, gmix_ref, w_in_ref, lbl_ref, cw_ref, cb_ref, cst_ref,
                 f_ref, k_ref, q_ref, v_ref, gate_ref, yb_ref, sga_ref, sgb_ref, cv_out_ref):
    x = x_ref[...]
    sh1, sc1 = mod_ref[:, 0:D_MODEL], mod_ref[:, D_MODEL:2 * D_MODEL]
    h = _rms(x) * gmix_ref[...] * (1.0 + sc1) + sh1
    hb = h.astype(BF16)

    def proj(c, w):
        return _dot(hb, w_in_ref[:, c:c + w])

    lb = _lower_bound(lbl_ref[...])
    f = lb + (1.0 - lb) * _sigmoid(proj(C_F, KEY_W))
    f_ref[...] = f
    k_ref[...] = 1.0 - f
    q_ref[...] = _silu(proj(C_Q, KEY_W))
    v_ref[...] = proj(C_I, KEY_W)
    gate_ref[...] = _silu(proj(C_G, KEY_W))
    u = proj(C_CC, CONV_W) * proj(C_VB, CONV_W)
    c0, c1 = cst_ref[:, 0:CONV_W], cst_ref[:, CONV_W:2 * CONV_W]
    conv = cw_ref[0:1] * c0 + cw_ref[1:2] * c1 + cw_ref[2:3] * u + cb_ref[...]
    yb_ref[...] = proj(C_BB, CONV_W) * conv
    cv_out_ref[:, 0:CONV_W] = c1
    cv_out_ref[:, CONV_W:2 * CONV_W] = u
    sga_ref[...] = _sigmoid(proj(C_MGA, D_MODEL))
    sgb_ref[...] = _sigmoid(proj(C_MGB, D_MODEL))


def _smp1(x, mod, gmix, w_in, lbl, cw, cb, cst):
    n = x.shape[0]
    kw = jax.ShapeDtypeStruct((n, KEY_W), F32)
    dm = jax.ShapeDtypeStruct((n, D_MODEL), F32)
    return pl.pallas_call(
        _smp1_kernel,
        out_shape=[kw, kw, kw, kw, kw, kw, dm, dm,
                   jax.ShapeDtypeStruct((n, (CONV_K - 1) * CONV_W), F32)],
        compiler_params=pltpu.CompilerParams(vmem_limit_bytes=VMEM_LIMIT),
        name="smp1",
    )(x, mod, gmix, w_in, lbl, cw, cb, cst)


def _smp2(f, k, q, v, state):
    n = f.shape[0]
    info = plsc.get_sparse_core_info()
    lanes = info.num_lanes
    n_workers = info.num_cores * info.num_subcores
    tok_per = n // n_workers
    n_chunks = DK // lanes
    vec = lambda: pltpu.VMEM((DK,), F32)

    @pl.kernel(out_type=[jax.ShapeDtypeStruct(state.shape, F32), jax.ShapeDtypeStruct((n, KEY_W), F32)],
               mesh=_sc_mesh(), scratch_types=[pltpu.VMEM((DK, DK), F32), vec(), vec(), vec(), vec(), vec()],
               compiler_params=pltpu.CompilerParams(needs_layout_passes=False), name="smp2")
    def run(f_hbm, k_hbm, q_hbm, v_hbm, s_hbm, s_out_hbm, o_hbm, s_v, f_v, k_v, q_v, v_v, o_v):
        wid = lax.axis_index("subcore") * info.num_cores + lax.axis_index("core")

        def tile(j, carry):
            t = wid * tok_per + j // HEADS
            hcols = pl.ds((j % HEADS) * DK, DK)
            pltpu.sync_copy(s_hbm.at[t, j % HEADS], s_v)
            for src, dst in ((f_hbm, f_v), (k_hbm, k_v), (q_hbm, q_v), (v_hbm, v_v)):
                pltpu.sync_copy(src.at[t, hcols], dst)
            v_chunks = [v_v[pl.ds(c * lanes, lanes)] for c in range(n_chunks)]

            def row(d, acc):
                at = [jnp.zeros((lanes,), I32) + d]
                fd, kd, qd = (plsc.load_gather(r, at) for r in (f_v, k_v, q_v))
                out = []
                for c in range(n_chunks):
                    cols = pl.ds(c * lanes, lanes)
                    new = s_v[d, cols] * fd + kd * v_chunks[c]
                    s_v[d, cols] = new
                    out.append(acc[c] + qd * new)
                return tuple(out)

            acc = lax.fori_loop(0, DK, row, tuple(jnp.zeros((lanes,), F32) for _ in range(n_chunks)))
            for c in range(n_chunks):
                o_v[pl.ds(c * lanes, lanes)] = acc[c]
            pltpu.sync_copy(s_v, s_out_hbm.at[t, j % HEADS])
            pltpu.sync_copy(o_v, o_hbm.at[t, hcols])
            return carry

        lax.fori_loop(0, tok_per * HEADS, tile, 0)

    return run(f, k, q, v, state)


def _smp3_kernel(x_ref, mod_ref, o_ref, gate_ref, yb_ref, sga_ref, sgb_ref, hg_ref, gffn_ref,
                 w_oh_ref, w_oc_ref, w_o_ref, wr_hl_ref, w_sgu_ref, w_sd_ref,
                 rbias_ref, cnt_p_ref, idx_p_ref, w_p_ref, rank_p_ref, h2_all_ref,
                 xmid_ref, h2_ref, idx_ref, w_ref, rank_ref, cnt_ref):
    del h2_all_ref
    parts = []
    for hd in range(HEADS):
        hs = slice(hd * DK, (hd + 1) * DK)
        parts.append(_rms(o_ref[:, hs]) * hg_ref[:, hs] * gate_ref[:, hs])
    ya = jnp.concatenate(parts, axis=1)
    mixed = (sga_ref[...] * _dot(ya.astype(BF16), w_oh_ref[...])
             + sgb_ref[...] * _dot(yb_ref[...].astype(BF16), w_oc_ref[...]))
    g1 = mod_ref[:, 2 * D_MODEL:3 * D_MODEL]
    x1 = x_ref[...] + g1 * _dot(mixed.astype(BF16), w_o_ref[...])
    mod_rows = tuple(mod_ref[:, j * D_MODEL:(j + 1) * D_MODEL] for j in (3, 4, 5))
    xmid, h2b, lgt = _ffn_pre(x1, mod_rows, gffn_ref[...], wr_hl_ref[...],
                              shared=(w_sgu_ref[...], w_sd_ref[...]))
    xmid_ref[...] = xmid
    _pack_rows(h2b, h2_ref)
    n_prompt = idx_p_ref.shape[1]
    picks, weights, ranks, total = _route_tile(lgt, rbias_ref[...], cnt_p_ref[:, 0:1])
    for src, dst, new in ((idx_p_ref, idx_ref, picks), (w_p_ref, w_ref, weights), (rank_p_ref, rank_ref, ranks)):
        dst[:, :n_prompt] = src[...]
        for k in range(TOP_K):
            dst[k:k + 1, n_prompt:] = new[k]
    cnt_ref[...] = jnp.broadcast_to(total, cnt_ref.shape).astype(I32)


def _smp3(x, mod, o, gate, yb, sga, sgb, hg, gffn, w_oh, w_oc, w_o, wr_hl, w_sgu, w_sd,
          rbias, cnt_p, idx_p, w_p, rank_p, h2_all):
    n = x.shape[0]
    vmem_args = [x, mod, o, gate, yb, sga, sgb, hg, gffn, w_oh, w_oc, w_o, wr_hl, w_sgu, w_sd,
                 rbias, cnt_p, idx_p, w_p, rank_p]
    n_prompt = idx_p.shape[1]
    blk = n_prompt // n
    slot = lambda dt: jax.ShapeDtypeStruct((TOP_K, n_prompt + n), dt)

    def full(a):
        nd = a.ndim
        return pl.BlockSpec(a.shape, lambda i, _nd=nd: (0,) * _nd)

    return pl.pallas_call(
        _smp3_kernel,
        grid=(1,),
        in_specs=[full(a) for a in vmem_args] + [pl.BlockSpec(memory_space=pl.ANY)],
        out_specs=[pl.BlockSpec((n, D_MODEL), lambda i: (0, 0)),
                   pl.BlockSpec((2, n, PLANE_W), lambda i: (0, blk, 0)),
                   pl.BlockSpec((TOP_K, n_prompt + n), lambda i: (0, 0)),
                   pl.BlockSpec((TOP_K, n_prompt + n), lambda i: (0, 0)),
                   pl.BlockSpec((TOP_K, n_prompt + n), lambda i: (0, 0)),
                   pl.BlockSpec((N_EXP, 128), lambda i: (0, 0))],
        out_shape=[jax.ShapeDtypeStruct((n, D_MODEL), F32),
                   jax.ShapeDtypeStruct(h2_all.shape, h2_all.dtype),
                   slot(I32), slot(F32), slot(I32),
                   jax.ShapeDtypeStruct((N_EXP, 128), I32)],
        input_output_aliases={len(vmem_args): 1},
        compiler_params=pltpu.CompilerParams(
            dimension_semantics=("arbitrary",), vmem_limit_bytes=VMEM_LIMIT),
        name="smp3",
    )(*vmem_args, h2_all)


def _route_tile(lgt, bias, carry):
    tr = lgt.shape[1]
    neg = -jnp.inf
    scores = _sigmoid(lgt)
    sel = scores + bias
    j8 = lax.broadcasted_iota(I32, (GRP_SZ, tr), 0)
    groups = [sel[g * GRP_SZ:(g + 1) * GRP_SZ] for g in range(N_GRP)]
    gscore = []
    for grp in groups:
        m1 = jnp.max(grp, axis=0, keepdims=True)
        i1 = jnp.min(jnp.where(grp == m1, j8, GRP_SZ), axis=0, keepdims=True)
        m2 = jnp.max(jnp.where(j8 == i1, neg, grp), axis=0, keepdims=True)
        gscore.append(m1 + m2)
    kept = []
    for g in range(N_GRP):
        beaten = jnp.zeros((1, tr), I32)
        for o in range(N_GRP):
            if o < g:
                beaten = beaten + (gscore[o] >= gscore[g]).astype(I32)
            elif o > g:
                beaten = beaten + (gscore[o] > gscore[g]).astype(I32)
        kept.append(jnp.where(beaten < TOPK_GRP, groups[g], neg))
    masked = jnp.concatenate(kept, axis=0)
    ei = lax.broadcasted_iota(I32, masked.shape, 0)
    chosen = jnp.zeros(masked.shape, jnp.bool_)
    picks, weights = [], []
    for _ in range(TOP_K):
        m = jnp.max(masked, axis=0, keepdims=True)
        pick = jnp.min(jnp.where((masked == m) & ~chosen, ei, N_EXP), axis=0, keepdims=True)
        hit = ei == pick
        weights.append(jnp.sum(jnp.where(hit, scores, 0.0), axis=0, keepdims=True))
        picks.append(pick)
        chosen = chosen | hit
        masked = jnp.where(hit, neg, masked)
    wsum = weights[0]
    for w in weights[1:]:
        wsum = wsum + w
    sel01 = chosen.astype(F32)
    r = lax.broadcasted_iota(I32, (tr, tr), 0)
    c = lax.broadcasted_iota(I32, (tr, tr), 1)
    before = (r < c).astype(BF16)
    cnt = _dot(sel01.astype(BF16), before) + carry
    weights = [w / wsum * ROUTED_SCALE for w in weights]
    ranks = [jnp.sum(jnp.where(ei == p, cnt, 0.0), axis=0, keepdims=True).astype(I32) for p in picks]
    return picks, weights, ranks, carry + jnp.sum(sel01, axis=1, keepdims=True)


def _dest_kernel(start_ref, idx_ref, rank_ref, all_ref, *chunk_refs, n_rows, chunks):
    n_tok = idx_ref.shape[1]
    idx = idx_ref[...]
    acc = rank_ref[...]
    for e in range(N_EXP):
        acc = acc + jnp.where(idx == e, start_ref[e], 0)
    for k in range(TOP_K):
        for p in range(2):
            row = acc[k:k + 1, :] + p * n_rows
            seg = 2 * k + p
            all_ref[:, seg * n_tok:(seg + 1) * n_tok] = row
            for (c0, c1), ref in zip(chunks, chunk_refs):
                ref[:, seg * (c1 - c0):(seg + 1) * (c1 - c0)] = row[:, c0:c1]


def _dest(pad_start, idx, rank, n_rows, chunks):
    k, n_tok = idx.shape
    vmem = pl.BlockSpec(memory_space=pltpu.VMEM)
    sizes = [n_tok] + [c1 - c0 for c0, c1 in chunks]
    return pl.pallas_call(
        functools.partial(_dest_kernel, n_rows=n_rows, chunks=chunks),
        in_specs=[pl.BlockSpec(memory_space=pltpu.SMEM), vmem, vmem],
        out_specs=[vmem] * len(sizes),
        out_shape=[jax.ShapeDtypeStruct((1, 2 * k * n), I32) for n in sizes],
        name="dest",
    )(pad_start, idx, rank)


def _sc_mesh():
    return plsc.VectorSubcoreMesh(core_axis_name="core", subcore_axis_name="subcore")


def _dispatch(rows, dest, n_out):
    n, width = rows.shape
    win = SC_WINDOW
    steps = n // win

    @pl.kernel(out_type=jax.ShapeDtypeStruct((n_out, width), rows.dtype), mesh=_sc_mesh(),
               scratch_types=[], name="dispatch")
    def run(x_hbm, *refs):
        i_hbms, o_hbm = refs[:TOP_K], refs[TOP_K]

        def body(x_vmem, *i_vmems):
            for i_vmem in i_vmems:
                pltpu.sync_copy(x_vmem, o_hbm.at[i_vmem.at[0]])

        pltpu.emit_pipeline(
            body,
            grid=(steps,),
            in_specs=[pl.BlockSpec((win, width), lambda i: (i, 0))]
                     + [pl.BlockSpec((1, win), lambda i, k=k: (0, k * steps + i)) for k in range(TOP_K)],
            out_specs=[],
            core_axis_name=("core", "subcore"),
            dimension_semantics=(pltpu.PARALLEL,),
        )(x_hbm, *i_hbms)

    assert dest.shape == (1, TOP_K * n)
    return run(rows, *([dest] * TOP_K))


def _combine(rows, dest_flat):
    width = rows.shape[1]
    n = dest_flat.shape[1]
    win = SC_WINDOW

    @pl.kernel(out_type=jax.ShapeDtypeStruct((n, width), rows.dtype), mesh=_sc_mesh(),
               scratch_types=[], name="combine")
    def run(y_hbm, i_hbm, o_hbm):
        def body(i_vmem, o_vmem):
            pltpu.sync_copy(y_hbm.at[i_vmem.at[0]], o_vmem)

        pltpu.emit_pipeline(
            body,
            grid=(n // win,),
            in_specs=[pl.BlockSpec((1, win), lambda i: (0, i))],
            out_specs=[pl.BlockSpec((win, width), lambda i: (i, 0))],
            core_axis_name=("core", "subcore"),
            dimension_semantics=(pltpu.PARALLEL,),
        )(i_hbm, o_hbm)

    return run(rows, dest_flat)


def _gmm_kernel(blk_exp_ref, n_used_ref, xs_hbm, wg_hbm, wu_hbm, wd_hbm, ys_hbm,
                xbuf, ybuf, wg32, wu32, wd32, wgu_b, wd_b, xsem, ysem, wsem, run_ref):
    nx, ny, bm = xbuf.shape[0], ybuf.shape[0], xbuf.shape[2]
    n_used = n_used_ref[0]

    def x_copies(b):
        rows, slot = pl.ds(b * bm, bm), b % nx
        return [pltpu.make_async_copy(xs_hbm.at[p, rows, :], xbuf.at[slot, p], xsem.at[slot, p]) for p in range(2)]

    def y_copies(b):
        rows, slot = pl.ds(b * bm, bm), b % ny
        return [pltpu.make_async_copy(ybuf.at[slot, p], ys_hbm.at[p, rows, :], ysem.at[slot, p]) for p in range(2)]

    def start(copies, priority=0):
        for c in copies:
            c.start(priority=priority)

    weight_queue = 1

    def wait(copies):
        for c in copies:
            c.wait()

    def w_copies(e, slot):
        return (pltpu.make_async_copy(wg_hbm.at[e], wg32.at[slot], wsem.at[slot, 0]),
                pltpu.make_async_copy(wu_hbm.at[e], wu32.at[slot], wsem.at[slot, 1]),
                pltpu.make_async_copy(wd_hbm.at[e], wd32.at[slot], wsem.at[slot, 2]))

    def run_end(b):
        return lax.while_loop(lambda j: (j < n_used) & (blk_exp_ref[jnp.minimum(j, n_used - 1)] == blk_exp_ref[b]),
                              lambda j: j + 1, b + 1)

    run_ref[0] = 0
    start(x_copies(0))
    start(w_copies(blk_exp_ref[0], 0), weight_queue)
    for j in range(1, nx - 1):
        @pl.when(j < n_used)
        def _():
            start(x_copies(j))

    def block(b, carry):
        @pl.when(b + nx - 1 < n_used)
        def _():
            start(x_copies(b + nx - 1))

        @pl.when((b == 0) | (blk_exp_ref[b] != blk_exp_ref[jnp.maximum(b - 1, 0)]))
        def _():
            wslot = run_ref[0] % 2
            run_ref[0] = run_ref[0] + 1
            wait(w_copies(blk_exp_ref[b], wslot))
            wgu_b[:, 0:EXP_FF] = wg32[wslot].astype(BF16)
            wgu_b[:, EXP_FF:2 * EXP_FF] = wu32[wslot].astype(BF16)
            wd_b[...] = wd32[wslot].astype(BF16)
            nxt = run_end(b)

            @pl.when(nxt < n_used)
            def _():
                start(w_copies(blk_exp_ref[jnp.minimum(nxt, n_used - 1)], 1 - wslot), weight_queue)

        wait(x_copies(b))

        @pl.when(b >= ny)
        def _():
            wait(y_copies(b - ny))

        xslot = b % nx
        xc = _unpack_rows(xbuf[xslot, 0], xbuf[xslot, 1])
        gu = sum(_dot(c, wgu_b[i * PLANE_W:(i + 1) * PLANE_W, :]) for i, c in enumerate(xc))
        act = (_silu(gu[:, :EXP_FF]) * gu[:, EXP_FF:]).astype(BF16)
        _pack_rows(_dot(act, wd_b[...]).astype(BF16), ybuf.at[b % ny])
        start(y_copies(b))
        return carry

    lax.fori_loop(0, n_used, block, 0)

    for j in range(ny, 0, -1):
        @pl.when(n_used >= j)
        def _():
            wait(y_copies(n_used - j))


def _gmm(blk_exp, n_used, xs, w_gate, w_up, w_down):
    n_rows = xs.shape[1]
    bm = GMM_BM
    nb = n_rows // bm

    assert blk_exp.shape == (nb,)
    any_spec = pl.BlockSpec(memory_space=pl.ANY)
    grid_spec = pltpu.PrefetchScalarGridSpec(
        num_scalar_prefetch=2,
        grid=(1,),
        in_specs=[any_spec, any_spec, any_spec, any_spec],
        out_specs=any_spec,
        scratch_shapes=[pltpu.VMEM((GMM_NX, 2, bm, PLANE_W), I32), pltpu.VMEM((GMM_NY, 2, bm, PLANE_W), I32),
                        pltpu.VMEM((2, D_MODEL, EXP_FF), F32), pltpu.VMEM((2, D_MODEL, EXP_FF), F32),
                        pltpu.VMEM((2, EXP_FF, D_MODEL), F32),
                        pltpu.VMEM((D_MODEL, 2 * EXP_FF), BF16), pltpu.VMEM((EXP_FF, D_MODEL), BF16),
                        pltpu.SemaphoreType.DMA((GMM_NX, 2)), pltpu.SemaphoreType.DMA((GMM_NY, 2)),
                        pltpu.SemaphoreType.DMA((2, 3)), pltpu.SMEM((1,), I32)],
    )
    return pl.pallas_call(
        _gmm_kernel,
        grid_spec=grid_spec,
        out_shape=jax.ShapeDtypeStruct((2, n_rows, PLANE_W), I32),
        compiler_params=pltpu.CompilerParams(dimension_semantics=("arbitrary",)),
        name="gmm",
    )(blk_exp, n_used, xs, w_gate, w_up, w_down)


def _final_kernel(xmid_ref, g2_ref, z_ref, w_ref, gfin_ref, *rest, with_shared):
    y_ref = rest[-1]
    accs = [jnp.zeros((xmid_ref.shape[0], PLANE_W), F32) for _ in range(4)]
    w_cols = w_ref[...].T
    for k in range(TOP_K):
        wk = w_cols[:, k:k + 1]
        cols = _unpack_rows(z_ref[k, 0], z_ref[k, 1])
        accs = [a + wk * c.astype(F32) for a, c in zip(accs, cols)]
    acc = jnp.concatenate(accs, axis=1)
    if with_shared:
        h2_ref, w_sgu_ref, w_sd_ref = rest[:3]
        acc = acc + _shared_expert(_unpack_rows(h2_ref[0], h2_ref[1]), w_sgu_ref[...], w_sd_ref[...])
    y_ref[...] = _rms(xmid_ref[...] + g2_ref[0] * acc) * gfin_ref[...]


def _final(xmid, g2, z, w_t, gfin, tile, *, n_tiles, x_tile0, z_tile0, w_tile0, tiles_per_g2,
           shared=None, y_prev=None):
    args = [xmid, g2, z, w_t, gfin]
    in_specs = [pl.BlockSpec((tile, D_MODEL), lambda i: (x_tile0 + i, 0)),
                pl.BlockSpec((1, g2.shape[1], D_MODEL), lambda i: ((x_tile0 + i) // tiles_per_g2, 0, 0)),
                pl.BlockSpec((TOP_K, 2, tile, PLANE_W), lambda i: (0, 0, z_tile0 + i, 0)),
                pl.BlockSpec((TOP_K, tile), lambda i: (0, w_tile0 + i)),
                pl.BlockSpec((1, D_MODEL), lambda i: (0, 0))]
    if shared is not None:
        h2_all, w_sgu, w_sd = shared
        args += [h2_all, w_sgu, w_sd]
        in_specs += [pl.BlockSpec((2, tile, PLANE_W), lambda i: (0, x_tile0 + i, 0)),
                     pl.BlockSpec(w_sgu.shape, lambda i: (0, 0)), pl.BlockSpec(w_sd.shape, lambda i: (0, 0))]
    aliases = {}
    if y_prev is not None:
        args.append(y_prev)
        in_specs.append(pl.BlockSpec(memory_space=pl.ANY))
        aliases = {len(args) - 1: 0}
    return pl.pallas_call(
        functools.partial(_final_kernel, with_shared=shared is not None),
        grid=(n_tiles,),
        in_specs=in_specs,
        out_specs=pl.BlockSpec((tile, D_MODEL), lambda i: (x_tile0 + i, 0)),
        out_shape=jax.ShapeDtypeStruct(xmid.shape, F32),
        input_output_aliases=aliases,
        compiler_params=pltpu.CompilerParams(
            dimension_semantics=("arbitrary",), vmem_limit_bytes=VMEM_LIMIT),
        name="final",
    )(*args)


def kernel(x_prompt, x_sample, state_hgrn, state_conv, c_prompt, c_sample, w_ada, b_ada, norm_mix_g, norm_ffn_g, w_in, lb_logits, hgrn_norm_g, conv_w, conv_b, w_out_hgrn, w_out_conv, w_o, w_router, router_bias, w_exp_gate, w_exp_up, w_exp_down, w_sh_gate, w_sh_up, w_sh_down, final_norm_g):
    assert w_ada.shape[0] == 1 and lb_logits.shape[0] == 2
    bsz, seq, _ = x_prompt.shape
    n_smp = x_sample.shape[0]
    n_prompt = bsz * seq
    n_tok = n_prompt + n_smp

    w_in_b = w_in[0].astype(BF16)
    w_oh_b = w_out_hgrn[0].astype(BF16)
    w_oc_b = w_out_conv[0].astype(BF16)
    w_o_b = w_o[0].astype(BF16)
    wr_t = w_router[0].T
    wr_hi = wr_t.astype(BF16)
    wr_hl = jnp.concatenate([wr_hi, (wr_t - wr_hi.astype(F32)).astype(BF16)], axis=0)
    w_sgu = jnp.concatenate([w_sh_gate[0], w_sh_up[0]], axis=1).astype(BF16)
    w_sd = w_sh_down[0].astype(BF16)
    gmix = norm_mix_g[0].reshape(1, D_MODEL)
    gffn = norm_ffn_g[0].reshape(1, D_MODEL)
    hg = hgrn_norm_g[0].reshape(1, KEY_W)
    cw = conv_w[0]
    cb = conv_b[0].reshape(1, CONV_W)
    gfin = final_norm_g.reshape(1, D_MODEL)

    mod_p, mod_s = _ada(c_prompt, c_sample, w_ada[0], b_ada[0])

    rbias = router_bias[0].reshape(N_EXP, 1)
    xmid_p, h2_all, idx_p, w_p, rank_p, cnt_p, s_p, cv_p = _mix(
        x_prompt, mod_p.reshape(bsz, 6, D_MODEL), n_tok, gmix, gffn, w_in_b, lb_logits, hg, cw, cb, w_oh_b, w_oc_b, w_o_b, wr_hl, rbias)

    xs2 = x_sample.reshape(n_smp, D_MODEL)
    f, kk, q, v, gate, yb, sga, sgb, cv_s = _smp1(
        xs2, mod_s, gmix, w_in_b, lb_logits, cw, cb, state_conv[0].reshape(n_smp, (CONV_K - 1) * CONV_W))
    s_s, o_s = _smp2(f, kk, q, v, state_hgrn[0])
    xmid_s, h2_all, idx, w_tok, rank, cnt = _smp3(xs2, mod_s, o_s, gate, yb, sga, sgb, hg, gffn,
                                                  w_oh_b, w_oc_b, w_o_b, wr_hl, w_sgu, w_sd,
                                                  rbias, cnt_p, idx_p, w_p, rank_p, h2_all)

    bm = GMM_BM
    n_blocks = (n_tok * TOP_K + N_EXP * (bm - 1)) // bm
    n_rows = n_blocks * bm
    counts = cnt[:, 0]
    padded = (counts + bm - 1) // bm * bm
    pad_end = jnp.cumsum(padded)
    pad_start = pad_end - padded
    blk_row0 = jnp.arange(n_blocks, dtype=I32) * bm
    blk_exp = jnp.minimum(jnp.sum((pad_end[None, :] <= blk_row0[:, None]).astype(I32), axis=1), N_EXP - 1)
    n_used = (pad_end[-1:] // bm).astype(I32)
    cuts = [0] + [n_prompt * f // FINAL_SPLIT[-1] for f in FINAL_SPLIT]
    assert all(c % FINAL_TILE == 0 for c in cuts) and cuts[-1] == n_prompt
    chunks = tuple((c0, n_tok if c1 == n_prompt else c1) for c0, c1 in zip(cuts[:-1], cuts[1:]))
    dest, *chunk_dest = _dest(pad_start.astype(I32), idx, rank, n_rows, chunks)

    xs = _dispatch(h2_all.reshape(2 * n_tok, PLANE_W), dest, 2 * n_rows).reshape(2, n_rows, PLANE_W)
    ys = _gmm(blk_exp, n_used, xs, w_exp_gate[0], w_exp_up[0], w_exp_down[0])

    ys_flat = ys.reshape(2 * n_rows, PLANE_W)
    w_t = w_tok
    xmid_p2 = xmid_p.reshape(n_prompt, D_MODEL)
    g2_p = mod_p[:, 5 * D_MODEL:].reshape(bsz, 1, D_MODEL)
    g2_s = mod_s[:, 5 * D_MODEL:].reshape(1, n_smp, D_MODEL)
    y_p = None
    for (c0, c1), dest_c in zip(chunks, chunk_dest):
        z = _combine(ys_flat, dest_c).reshape(TOP_K, 2, c1 - c0, PLANE_W)
        y_p = _final(xmid_p2, g2_p, z, w_t, gfin, FINAL_TILE, n_tiles=(min(c1, n_prompt) - c0) // FINAL_TILE,
                     x_tile0=c0 // FINAL_TILE, z_tile0=0, w_tile0=c0 // FINAL_TILE,
                     tiles_per_g2=seq // FINAL_TILE, shared=(h2_all, w_sgu, w_sd), y_prev=y_p)
    y_s = _final(xmid_s, g2_s, z, w_t, gfin, n_smp, n_tiles=1, x_tile0=0, z_tile0=(n_prompt - c0) // n_smp,
                 w_tile0=n_prompt // n_smp, tiles_per_g2=1)

    return (y_p.reshape(bsz, seq, D_MODEL), y_s.reshape(n_smp, 1, D_MODEL),
            s_p[None], cv_p[None], s_s[None], cv_s.reshape(1, n_smp, CONV_K - 1, CONV_W))
```
